```python
import math
import jax, jax.numpy as jnp
from jax import lax
import numpy as np

D_MODEL = 1024
BATCH = 8
SEQ = 2048
DEPTH = 1
DEC_BATCH = 128
DEC_SEQ = 1
PAST_LEN = 16384
PAGE_SIZE = 128

ML_HEADS = 4
ML_DK = D_MODEL // 8
ML_DV = D_MODEL // 8
ML_WIDTH = ML_HEADS * ML_DV
ML_QK_WIDTH = ML_HEADS * ML_DK
ML_CHUNK = 128
CM_HEADS = 4
CM_HEAD_DIM = D_MODEL // 8
CM_WIDTH = CM_HEADS * CM_HEAD_DIM
CM_CHUNK = 128
MIX_WIDTH = ML_WIDTH + CM_WIDTH
IN_SIZES = (ML_QK_WIDTH, ML_QK_WIDTH, ML_WIDTH, ML_WIDTH, ML_HEADS, ML_HEADS, CM_WIDTH, CM_WIDTH)
N_IN = sum(IN_SIZES)
D_FF = -(-8 * D_MODEL // (3 * 256)) * 256
N_MOD = 6
EPS = 1e-6

kernel_name = "hybrid_mlstm_chunkmlp_decode_step"


def rmsnorm(x, g):
    x32 = x.astype(jnp.float32)
    y = x32 * lax.rsqrt(jnp.mean(x32 * x32, axis=-1, keepdims=True) + EPS)
    return (y * g.astype(jnp.float32)).astype(x.dtype)


def layernorm(x, g, b):
    x32 = x.astype(jnp.float32)
    mu = jnp.mean(x32, axis=-1, keepdims=True)
    var = jnp.mean(jnp.square(x32 - mu), axis=-1, keepdims=True)
    y = (x32 - mu) * lax.rsqrt(var + EPS)
    return (y * g.astype(jnp.float32) + b.astype(jnp.float32)).astype(x.dtype)


def split_cols(a, sizes):
    idx = np.cumsum(sizes)[:-1].tolist()
    return jnp.split(a, idx, axis=-1)


def mlstm_chunkwise(q, k, v, i_pre, f_pre, C0, n0, m0):
    B, T, H, _ = q.shape
    L = math.gcd(T, ML_CHUNK)
    nc = T // L

    def to_chunks(a):
        a = a.reshape((B, nc, L) + a.shape[2:])
        a = jnp.moveaxis(a, 3, 2)
        return jnp.moveaxis(a, 1, 0)

    qc = to_chunks(q) * (ML_DK ** -0.5)
    kc = to_chunks(k)
    vc = to_chunks(v)
    ic = to_chunks(i_pre)
    lfc = jax.nn.log_sigmoid(to_chunks(f_pre))
    causal = jnp.tril(jnp.ones((L, L), dtype=bool))

    def step(carry, xs):
        C, n, m = carry
        qb, kb, vb, ib, lfb = xs
        b = jnp.cumsum(lfb, axis=-1)
        D = jnp.where(causal, b[..., :, None] - b[..., None, :] + ib[..., None, :], -jnp.inf)
        inter = b + m[..., None]
        m_t = jnp.maximum(inter, jnp.max(D, axis=-1))
        w_intra = jnp.exp(D - m_t[..., None])
        w_inter = jnp.exp(inter - m_t)
        s = jnp.einsum('bhtd,bhsd->bhts', qb, kb) * w_intra
        num = jnp.einsum('bhts,bhse->bhte', s, vb) + w_inter[..., None] * jnp.einsum('bhed,bhtd->bhte', C, qb)
        den = jnp.sum(s, axis=-1) + w_inter * jnp.einsum('bhd,bhtd->bht', n, qb)
        h = num / jnp.maximum(jnp.abs(den), jnp.exp(-m_t))[..., None]
        g = b[..., -1:] - b + ib
        dec = b[..., -1] + m
        m_new = jnp.maximum(dec, jnp.max(g, axis=-1))
        wg = jnp.exp(g - m_new[..., None])
        wd = jnp.exp(dec - m_new)
        C_new = wd[..., None, None] * C + jnp.einsum('bhs,bhse,bhsd->bhed', wg, vb, kb)
        n_new = wd[..., None] * n + jnp.einsum('bhs,bhsd->bhd', wg, kb)
        return (C_new, n_new, m_new), h

    (C, n, m), h = lax.scan(step, (C0, n0, m0), (qc, kc, vc, ic, lfc))
    h = jnp.moveaxis(h, 0, 1)
    h = jnp.moveaxis(h, 2, 3).reshape(B, T, H, -1)
    return h, C, n, m


def chunk_spatial_gate(u, v, w_s, b_s):
    B, T = v.shape[:2]
    Tp = -(-T // CM_CHUNK) * CM_CHUNK
    vp = jnp.pad(v, ((0, 0), (0, Tp - T), (0, 0), (0, 0)))
    vp = vp.reshape(B, Tp // CM_CHUNK, CM_CHUNK, CM_HEADS, CM_HEAD_DIM)
    w = jnp.where(jnp.tril(jnp.ones((CM_CHUNK, CM_CHUNK), dtype=bool))[None], w_s, jnp.zeros_like(w_s))
    s = jnp.einsum('hts,bcshe->bcthe', w, vp) + jnp.transpose(b_s)[None, None, :, :, None]
    s = s.reshape(B, Tp, CM_HEADS, CM_HEAD_DIM)[:, :T]
    return u * s


def hybrid_layer(x, c, C0, n0, m0, w_ada, b_ada, g_norm1, w_in, b_gate, g_mlstm_head,
                 ln_v_g, ln_v_b, w_s, b_s, w_out, g_norm2, w_gate_up, w_down):
    B, T, _ = x.shape
    mod = jax.nn.silu(c) @ w_ada + b_ada
    sh1, sc1, gt1, sh2, sc2, gt2 = [m_[:, None, :] for m_ in jnp.split(mod, N_MOD, axis=-1)]

    h = rmsnorm(x, g_norm1) * (1 + sc1) + sh1
    proj = h @ w_in
    q, k, vm, o, ig, fg, u, vs = split_cols(proj, IN_SIZES)
    f32 = jnp.float32
    q = q.reshape(B, T, ML_HEADS, ML_DK).astype(f32)
    k = k.reshape(B, T, ML_HEADS, ML_DK).astype(f32)
    vm = vm.reshape(B, T, ML_HEADS, ML_DV).astype(f32)
    i_pre = ig.astype(f32) + b_gate[:ML_HEADS].astype(f32)
    f_pre = fg.astype(f32) + b_gate[ML_HEADS:].astype(f32)
    h_ml, C, n, m = mlstm_chunkwise(q, k, vm, i_pre, f_pre,
                                    C0.astype(f32), n0.astype(f32), m0.astype(f32))
    h_ml = rmsnorm(h_ml.astype(x.dtype), g_mlstm_head)
    h_ml = h_ml * jax.nn.sigmoid(o.reshape(B, T, ML_HEADS, ML_DV))

    u = jax.nn.gelu(u).reshape(B, T, CM_HEADS, CM_HEAD_DIM)
    vs = layernorm(jax.nn.gelu(vs).reshape(B, T, CM_HEADS, CM_HEAD_DIM), ln_v_g, ln_v_b)
    h_cm = chunk_spatial_gate(u, vs, w_s, b_s)

    mix = jnp.concatenate([h_ml.reshape(B, T, ML_WIDTH), h_cm.reshape(B, T, CM_WIDTH)], axis=-1)
    x = x + gt1 * (mix @ w_out)

    h2 = rmsnorm(x, g_norm2) * (1 + sc2) + sh2
    a, bb = jnp.split(h2 @ w_gate_up, 2, axis=-1)
    x = x + gt2 * ((jax.nn.silu(a) * bb) @ w_down)
    return x, C, n, m, vs


def setup_inputs(seed: int = 0) -> dict:
    key = jax.random.key(seed)
    ks = jax.random.split(key, 24)
    f = jnp.float32
    nrm = lambda k_, s: jax.random.normal(k_, s, f)
    inp = {}
    inp['x_prompt'] = nrm(ks[0], (BATCH, SEQ, D_MODEL))
    inp['x_sample'] = nrm(ks[1], (DEC_BATCH, DEC_SEQ, D_MODEL))
    inp['c_prompt'] = nrm(ks[2], (BATCH, D_MODEL))
    inp['c_sample'] = nrm(ks[3], (DEC_BATCH, D_MODEL))
    inp['state_mlstm_C'] = 0.5 * nrm(ks[4], (DEPTH, DEC_BATCH, ML_HEADS, ML_DV, ML_DK))
    inp['state_mlstm_n'] = 0.5 * nrm(ks[5], (DEPTH, DEC_BATCH, ML_HEADS, ML_DK))
    inp['state_mlstm_m'] = nrm(ks[6], (DEPTH, DEC_BATCH, ML_HEADS))
    inp['w_ada'] = nrm(ks[7], (DEPTH, D_MODEL, N_MOD * D_MODEL)) * D_MODEL ** -0.5
    inp['b_ada'] = 0.02 * nrm(ks[8], (DEPTH, N_MOD * D_MODEL))
    inp['g_norm1'] = 1.0 + 0.05 * nrm(ks[9], (DEPTH, D_MODEL))
    inp['w_in'] = nrm(ks[10], (DEPTH, D_MODEL, N_IN)) * D_MODEL ** -0.5
    f_bias = jnp.broadcast_to(jnp.linspace(3.0, 6.0, ML_HEADS, dtype=f), (DEPTH, ML_HEADS))
    i_bias = 0.1 * nrm(ks[11], (DEPTH, ML_HEADS))
    inp['b_gate'] = jnp.concatenate([i_bias, f_bias + 0.05 * nrm(ks[12], (DEPTH, ML_HEADS))], axis=-1)
    inp['g_mlstm_head'] = 1.0 + 0.05 * nrm(ks[13], (DEPTH, ML_HEADS, ML_DV))
    inp['ln_v_g'] = 1.0 + 0.05 * nrm(ks[14], (DEPTH, CM_HEADS, CM_HEAD_DIM))
    inp['ln_v_b'] = 0.02 * nrm(ks[15], (DEPTH, CM_HEADS, CM_HEAD_DIM))
    inp['w_s'] = nrm(ks[16], (DEPTH, CM_HEADS, CM_CHUNK, CM_CHUNK)) * CM_CHUNK ** -0.5
    inp['b_s'] = 1.0 + 0.1 * nrm(ks[17], (DEPTH, CM_HEADS, CM_CHUNK))
    inp['w_out'] = nrm(ks[18], (DEPTH, MIX_WIDTH, D_MODEL)) * MIX_WIDTH ** -0.5
    inp['g_norm2'] = 1.0 + 0.05 * nrm(ks[19], (DEPTH, D_MODEL))
    inp['w_gate_up'] = nrm(ks[20], (DEPTH, D_MODEL, 2 * D_FF)) * D_MODEL ** -0.5
    inp['w_down'] = nrm(ks[21], (DEPTH, D_FF, D_MODEL)) * D_FF ** -0.5
    inp['g_final'] = 1.0 + 0.05 * nrm(ks[22], (D_MODEL,))
    return inp


def reference(x_prompt, x_sample, c_prompt, c_sample, state_mlstm_C, state_mlstm_n, state_mlstm_m,
              w_ada, b_ada, g_norm1, w_in, b_gate, g_mlstm_head, ln_v_g, ln_v_b, w_s, b_s,
              w_out, g_norm2, w_gate_up, w_down, g_final):
    f32 = jnp.float32
    xp, xs = x_prompt, x_sample
    Cp_l, np_l, mp_l, Cs_l, ns_l, ms_l, vs_l = [], [], [], [], [], [], []
    for l in range(DEPTH):
        params = (w_ada[l], b_ada[l], g_norm1[l], w_in[l], b_gate[l], g_mlstm_head[l], ln_v_g[l],
                  ln_v_b[l], w_s[l], b_s[l], w_out[l], g_norm2[l], w_gate_up[l], w_down[l])
        C0 = jnp.zeros((BATCH, ML_HEADS, ML_DV, ML_DK), f32)
        n0 = jnp.zeros((BATCH, ML_HEADS, ML_DK), f32)
        m0 = jnp.zeros((BATCH, ML_HEADS), f32)
        xp, Cp, n_p, mp, _ = hybrid_layer(xp, c_prompt, C0, n0, m0, *params)
        xs, Cs, n_s, ms, vs_new = hybrid_layer(xs, c_sample, state_mlstm_C[l], state_mlstm_n[l],
                                               state_mlstm_m[l], *params)
        Cp_l.append(Cp); np_l.append(n_p); mp_l.append(mp)
        Cs_l.append(Cs); ns_l.append(n_s); ms_l.append(ms); vs_l.append(vs_new)
    y_prompt = rmsnorm(xp, g_final)
    y_sample = rmsnorm(xs, g_final)
    return (y_prompt, y_sample,
            jnp.stack(Cp_l), jnp.stack(np_l), jnp.stack(mp_l),
            jnp.stack(Cs_l), jnp.stack(ns_l), jnp.stack(ms_l),
            jnp.stack(vs_l))
```

```python
import functools
import math

import jax
import jax.numpy as jnp
from jax import lax
from jax.experimental import pallas as pl
from jax.experimental.pallas import tpu as pltpu

F32 = jnp.float32
BF16 = jnp.bfloat16

D_MODEL = 1024
HEADS = 4
HEAD_DIM = 128
GROUP_WIDTH = HEADS * HEAD_DIM
CHUNK = 128
D_FF = 2816
N_MOD = 6
EPS = 1e-6
QK_SCALE = HEAD_DIM ** -0.5
LANES = 128

N_A = 4 * GROUP_WIDTH
N_B = 2 * GROUP_WIDTH
N_PROJ = N_A + N_B + LANES

VMEM_LIMIT = 56 * 1024 * 1024


def _dot(a, b):
    return jnp.dot(a, b, preferred_element_type=F32)


def _dot_nt(a, b):
    return lax.dot_general(a, b, (((1,), (1,)), ((), ())), preferred_element_type=F32)


def _dot_tn(a, b):
    return lax.dot_general(a, b, (((0,), (0,)), ((), ())), preferred_element_type=F32)


def _sigmoid(x):
    return 1.0 / (1.0 + jnp.exp(-x))


def _gelu_tanh(x):
    c = math.sqrt(2.0 / math.pi)
    return x * (0.5 * (1.0 + jnp.tanh(c * (x + 0.044715 * (x * x * x)))))


def _log_sigmoid(x):
    return jnp.minimum(x, 0.0) - jnp.log1p(jnp.exp(-jnp.abs(x)))


def _rms(x):
    return x * lax.rsqrt(jnp.mean(x * x, axis=-1, keepdims=True) + EPS)


def _split3_bf16(x):
    hi = x.astype(BF16)
    r1 = x - hi.astype(F32)
    mid = r1.astype(BF16)
    lo = (r1 - mid.astype(F32)).astype(BF16)
    return hi, mid, lo


def _mod_kernel(c_ref, w_ref, b_ref, o_ref):
    c = c_ref[...]
    a = (c * _sigmoid(c)).astype(BF16)
    o_ref[...] = _dot(a, w_ref[...].astype(BF16)) + b_ref[...]


def _modulation(c_all, w_ada, b_ada):
    rows = c_all.shape[0]
    tn = 1024
    return pl.pallas_call(
        _mod_kernel,
        grid=(N_MOD * D_MODEL // tn,),
        in_specs=[
            pl.BlockSpec((rows, D_MODEL), lambda j: (0, 0)),
            pl.BlockSpec((D_MODEL, tn), lambda j: (0, j)),
            pl.BlockSpec((1, tn), lambda j: (0, j)),
        ],
        out_specs=pl.BlockSpec((rows, tn), lambda j: (0, j)),
        out_shape=jax.ShapeDtypeStruct((rows, N_MOD * D_MODEL), F32),
        compiler_params=pltpu.CompilerParams(dimension_semantics=("arbitrary",)),
        name="modulation",
    )(c_all, w_ada, b_ada)


def _mod_spec(piece, tm, per_row):
    if per_row:
        return pl.BlockSpec((None, tm, D_MODEL), lambda g, t: (g, t, piece))
    return pl.BlockSpec((None, 1, D_MODEL), lambda g, t: (g, 0, piece))


def _const_spec(shape):
    nd = len(shape)
    return pl.BlockSpec(shape, lambda g, t: (0,) * nd, pipeline_mode=pl.Buffered(1))


def _proj_kernel(x_ref, sh_ref, sc_ref, g1_ref, w_ref, lng_ref, lnb_ref, a_ref, b_ref, g_ref):
    x = x_ref[...]
    h = _rms(x) * g1_ref[...]
    h = h * (1.0 + sc_ref[...]) + sh_ref[...]
    p = _dot(h.astype(BF16), w_ref[...])
    a_ref[:, : 3 * GROUP_WIDTH] = p[:, : 3 * GROUP_WIDTH].astype(a_ref.dtype)
    vs = _gelu_tanh(p[:, 3 * GROUP_WIDTH: N_A])
    for hd in range(HEADS):
        sl = slice(hd * HEAD_DIM, (hd + 1) * HEAD_DIM)
        v = vs[:, sl]
        mu = jnp.mean(v, axis=-1, keepdims=True)
        vc = v - mu
        var = jnp.mean(vc * vc, axis=-1, keepdims=True)
        y = vc * lax.rsqrt(var + EPS) * lng_ref[:, sl] + lnb_ref[:, sl]
        a_ref[:, 3 * GROUP_WIDTH + hd * HEAD_DIM: 3 * GROUP_WIDTH + (hd + 1) * HEAD_DIM] = y.astype(a_ref.dtype)
    b_ref[:, :GROUP_WIDTH] = _sigmoid(p[:, N_A: N_A + GROUP_WIDTH])
    b_ref[:, GROUP_WIDTH:] = _gelu_tanh(p[:, N_A + GROUP_WIDTH: N_A + N_B])
    g_ref[...] = p[:, N_A + N_B:]


def _project(x, mod, g1, w_proj, ln_g, ln_b, *, tm, per_row, a_dtype):
    groups, t, _ = x.shape
    return pl.pallas_call(
        _proj_kernel,
        grid=(groups, t // tm),
        in_specs=[
            pl.BlockSpec((None, tm, D_MODEL), lambda g, i: (g, i, 0)),
            _mod_spec(0, tm, per_row),
            _mod_spec(1, tm, per_row),
            _const_spec((1, D_MODEL)),
            _const_spec((D_MODEL, N_PROJ)),
            _const_spec((1, GROUP_WIDTH)),
            _const_spec((1, GROUP_WIDTH)),
        ],
        out_specs=[
            pl.BlockSpec((None, tm, N_A), lambda g, i: (g, i, 0)),
            pl.BlockSpec((None, tm, N_B), lambda g, i: (g, i, 0)),
            pl.BlockSpec((None, tm, LANES), lambda g, i: (g, i, 0)),
        ],
        out_shape=[
            jax.ShapeDtypeStruct((groups, t, N_A), a_dtype),
            jax.ShapeDtypeStruct((groups, t, N_B), F32),
            jax.ShapeDtypeStruct((groups, t, LANES), F32),
        ],
        compiler_params=pltpu.CompilerParams(
            dimension_semantics=("arbitrary", "arbitrary"), vmem_limit_bytes=VMEM_LIMIT),
        name="project",
    )(x, mod, mod, g1, w_proj, ln_g, ln_b)


def _mix_prompt_kernel(a_ref, b_ref, g_ref, bias_ref, gh_ref, ws_ref, bs_ref,
                       o_ref, c_out, n_out, m_out, ct_ref, n_ref, m_ref):
    chunk = pl.program_id(1)

    @pl.when(chunk == 0)
    def _():
        ct_ref[...] = jnp.zeros_like(ct_ref)
        n_ref[...] = jnp.zeros_like(n_ref)
        m_ref[...] = jnp.zeros_like(m_ref)

    row = lax.broadcasted_iota(jnp.int32, (CHUNK, CHUNK), 0)
    col = lax.broadcasted_iota(jnp.int32, (CHUNK, CHUNK), 1)
    causal = row >= col
    tril = jnp.where(causal, 1.0, 0.0).astype(BF16)

    pre = g_ref[...] + bias_ref[...]
    lf = _log_sigmoid(pre)
    hi, mid, lo = _split3_bf16(lf)
    bc_all = _dot(tril, hi) + _dot(tril, mid) + _dot(tril, lo)
    pre_t = pre.T
    bc_t = bc_all.T

    for hd in range(HEADS):
        sl = slice(hd * HEAD_DIM, (hd + 1) * HEAD_DIM)
        q = a_ref[:, sl]
        k = a_ref[:, GROUP_WIDTH + hd * HEAD_DIM: GROUP_WIDTH + (hd + 1) * HEAD_DIM]
        v = a_ref[:, 2 * GROUP_WIDTH + hd * HEAD_DIM: 2 * GROUP_WIDTH + (hd + 1) * HEAD_DIM]
        vsn = a_ref[:, 3 * GROUP_WIDTH + hd * HEAD_DIM: 3 * GROUP_WIDTH + (hd + 1) * HEAD_DIM]
        og = b_ref[:, sl]
        ug = b_ref[:, GROUP_WIDTH + hd * HEAD_DIM: GROUP_WIDTH + (hd + 1) * HEAD_DIM]

        bcol = bc_all[:, HEADS + hd: HEADS + hd + 1]
        icol = pre[:, hd: hd + 1]
        brow = bc_t[HEADS + hd: HEADS + hd + 1, :]
        irow = pre_t[hd: hd + 1, :]
        m_prev = m_ref[hd][0:1, 0:1]
        ct = ct_ref[hd]
        n_row = n_ref[hd][0:1, :]

        dmat = jnp.where(causal, bcol - brow + irow, -jnp.inf)
        inter = bcol + m_prev
        m_t = jnp.maximum(inter, jnp.max(dmat, axis=1, keepdims=True))
        w_intra = jnp.exp(dmat - m_t)
        w_inter = jnp.exp(inter - m_t) * QK_SCALE
        s = _dot_nt(q, k) * (w_intra * QK_SCALE)
        qf = q.astype(F32)
        num = _dot(s.astype(BF16), v) + w_inter * _dot(q, ct.astype(BF16))
        den = jnp.sum(s, axis=1, keepdims=True) + w_inter * jnp.sum(qf * n_row, axis=1, keepdims=True)
        hh = num / jnp.maximum(jnp.abs(den), jnp.exp(-m_t))
        ml = _rms(hh) * gh_ref[:, sl] * og
        o_ref[:, sl] = ml.astype(o_ref.dtype)

        blast = bcol[CHUNK - 1: CHUNK, :]
        gcol = blast - bcol + icol
        dec = blast + m_prev
        m_new = jnp.maximum(dec, jnp.max(gcol, axis=0, keepdims=True))
        wg = jnp.exp(gcol - m_new)
        wd = jnp.exp(dec - m_new)
        ct_new = wd * ct + _dot_tn(k, (wg * v.astype(F32)).astype(BF16))
        n_new = wd * n_row + jnp.sum(wg * k.astype(F32), axis=0, keepdims=True)
        ct_ref[hd] = ct_new
        n_ref[hd] = jnp.broadcast_to(n_new, n_ref.shape[1:])
        m_ref[hd] = jnp.broadcast_to(m_new, m_ref.shape[1:])

        sg = _dot(ws_ref[hd], vsn) + bs_ref[:, hd: hd + 1]
        o_ref[:, GROUP_WIDTH + hd * HEAD_DIM: GROUP_WIDTH + (hd + 1) * HEAD_DIM] = (ug * sg).astype(o_ref.dtype)

    @pl.when(chunk == pl.num_programs(1) - 1)
    def _():
        for hd in range(HEADS):
            c_out[hd] = ct_ref[hd].T
            n_out[hd: hd + 1, :] = n_ref[hd][0:1, :]
            m_out[hd: hd + 1, :] = m_ref[hd][0:1, :]


def _mix_prompt(a, b, gates, bias_row, g_head, ws_tril, bs_col):
    groups, t, _ = a.shape
    return pl.pallas_call(
        _mix_prompt_kernel,
        grid=(groups, t // CHUNK),
        in_specs=[
            pl.BlockSpec((None, CHUNK, N_A), lambda g, c: (g, c, 0)),
            pl.BlockSpec((None, CHUNK, N_B), lambda g, c: (g, c, 0)),
            pl.BlockSpec((None, CHUNK, LANES), lambda g, c: (g, c, 0)),
            _const_spec((1, LANES)),
            _const_spec((1, GROUP_WIDTH)),
            _const_spec((HEADS, CHUNK, CHUNK)),
            _const_spec((CHUNK, LANES)),
        ],
        out_specs=[
            pl.BlockSpec((None, CHUNK, D_MODEL), lambda g, c: (g, c, 0)),
            pl.BlockSpec((None, HEADS, HEAD_DIM, HEAD_DIM), lambda g, c: (g, 0, 0, 0)),
            pl.BlockSpec((None, HEADS, HEAD_DIM), lambda g, c: (g, 0, 0)),
            pl.BlockSpec((None, HEADS, LANES), lambda g, c: (g, 0, 0)),
        ],
        out_shape=[
            jax.ShapeDtypeStruct((groups, t, D_MODEL), BF16),
            jax.ShapeDtypeStruct((groups, HEADS, HEAD_DIM, HEAD_DIM), F32),
            jax.ShapeDtypeStruct((groups, HEADS, HEAD_DIM), F32),
            jax.ShapeDtypeStruct((groups, HEADS, LANES), F32),
        ],
        scratch_shapes=[
            pltpu.VMEM((HEADS, HEAD_DIM, HEAD_DIM), F32),
            pltpu.VMEM((HEADS, 8, HEAD_DIM), F32),
            pltpu.VMEM((HEADS, 8, LANES), F32),
        ],
        compiler_params=pltpu.CompilerParams(
            dimension_semantics=("arbitrary", "arbitrary"), vmem_limit_bytes=VMEM_LIMIT),
        name="mix_prompt",
    )(a, b, gates, bias_row, g_head, ws_tril, bs_col)


SAMPLE_TOKENS_PER_STEP = 32


def _mix_sample_kernel(q_ref, k_ref, v_ref, vsn_ref, og_ref, ug_ref, g_ref, m0_ref, n0_ref, c_ref,
                       bias_ref, gh_ref, ws0_ref, bs0_ref,
                       mix_ref, c_out, n_out, m_out,
                       vt_ref, kp_ref, wd_ref, cqt_ref):
    hd = pl.program_id(0)
    grp = pl.program_id(1)
    nb = q_ref.shape[0]
    lane = lax.broadcasted_iota(jnp.int32, (nb, LANES), 1)

    def gate_terms():
        pre = g_ref[...] + bias_ref[...]
        i_pre = jnp.sum(jnp.where(lane == hd, pre, 0.0), axis=1, keepdims=True)
        f_pre = jnp.sum(jnp.where(lane == hd + HEADS, pre, 0.0), axis=1, keepdims=True)
        m_prev = jnp.sum(jnp.where(lane == hd, m0_ref[...], 0.0), axis=1, keepdims=True)
        inter = _log_sigmoid(f_pre) + m_prev
        m_t = jnp.maximum(inter, i_pre)
        return m_t, jnp.exp(i_pre - m_t), jnp.exp(inter - m_t)

    @pl.when(grp == 0)
    def _():
        _, w_in, w_dec = gate_terms()
        vt_ref[...] = v_ref[...].T
        kp_ref[...] = (w_in * k_ref[...]).astype(BF16)
        wd_ref[...] = jnp.broadcast_to(w_dec, wd_ref.shape)
        cqt_ref[...] = jnp.zeros_like(cqt_ref)

    lane_sq = lax.broadcasted_iota(jnp.int32, (HEAD_DIM, LANES), 1)
    tokens = c_ref.shape[0]

    def body(r, carry):
        tok = grp * tokens + r
        q_row = q_ref[pl.ds(tok, 1), :]
        c_b = c_ref[r]
        sel = lane_sq == tok
        cq_col = jnp.sum(c_b * q_row, axis=1, keepdims=True)
        cqt_ref[...] = jnp.where(sel, cq_col, cqt_ref[...])
        v_sel = jnp.where(sel, vt_ref[...], 0.0).astype(BF16)
        outer = _dot(v_sel, kp_ref[...])
        c_out[r] = wd_ref[pl.ds(tok, 1), :] * c_b + outer
        return carry

    lax.fori_loop(0, tokens, body, 0)

    @pl.when(grp == pl.num_programs(1) - 1)
    def _():
        m_t, w_in, w_dec = gate_terms()
        q = q_ref[...]
        k = k_ref[...]
        v = v_ref[...]
        n0 = n0_ref[...]
        cq = cqt_ref[...].T
        s = jnp.sum(q * k, axis=1, keepdims=True) * (QK_SCALE * w_in)
        w_inter = w_dec * QK_SCALE
        num = s * v + w_inter * cq
        den = s + w_inter * jnp.sum(n0 * q, axis=1, keepdims=True)
        hh = num / jnp.maximum(jnp.abs(den), jnp.exp(-m_t))
        ml = _rms(hh) * gh_ref[...] * og_ref[...]
        cm = ug_ref[...] * (ws0_ref[...] * vsn_ref[...] + bs0_ref[...])
        n_out[...] = w_dec * n0 + w_in * k
        m_out[...] = jnp.broadcast_to(m_t, m_out.shape)
        for hh_static in range(HEADS):
            @pl.when(hd == hh_static)
            def _():
                mix_ref[:, hh_static * HEAD_DIM: (hh_static + 1) * HEAD_DIM] = ml
                mix_ref[:, GROUP_WIDTH + hh_static * HEAD_DIM: GROUP_WIDTH + (hh_static + 1) * HEAD_DIM] = cm


def _mix_sample(a, b, gates, m0_pad, n0, c0, bias_row, g_head, ws0_row, bs0_row):
    nb = a.shape[0]
    tb = SAMPLE_TOKENS_PER_STEP

    def head_block(offset):
        return pl.BlockSpec((nb, HEAD_DIM), lambda h, g: (0, offset + h))

    full = pl.BlockSpec((nb, LANES), lambda h, g: (0, 0))
    head_row = pl.BlockSpec((1, HEAD_DIM), lambda h, g: (0, h))
    c_spec = pl.BlockSpec((tb, None, HEAD_DIM, HEAD_DIM), lambda h, g: (g, h, 0, 0))
    return pl.pallas_call(
        _mix_sample_kernel,
        grid=(HEADS, nb // tb),
        in_specs=[
            head_block(0), head_block(HEADS), head_block(2 * HEADS), head_block(3 * HEADS),
            head_block(0), head_block(HEADS),
            full, full, head_block(0), c_spec,
            pl.BlockSpec((1, LANES), lambda h, g: (0, 0)),
            head_row, head_row, head_row,
        ],
        out_specs=[
            pl.BlockSpec((nb, D_MODEL), lambda h, g: (0, 0)),
            c_spec,
            head_block(0),
            head_block(0),
        ],
        out_shape=[
            jax.ShapeDtypeStruct((nb, D_MODEL), F32),
            jax.ShapeDtypeStruct(c0.shape, F32),
            jax.ShapeDtypeStruct((nb, GROUP_WIDTH), F32),
            jax.ShapeDtypeStruct((nb, GROUP_WIDTH), F32),
        ],
        scratch_shapes=[
            pltpu.VMEM((HEAD_DIM, nb), F32),
            pltpu.VMEM((nb, HEAD_DIM), BF16),
            pltpu.VMEM((nb, LANES), F32),
            pltpu.VMEM((HEAD_DIM, nb), F32),
        ],
        compiler_params=pltpu.CompilerParams(
            dimension_semantics=("arbitrary", "arbitrary"), vmem_limit_bytes=VMEM_LIMIT),
        name="mix_sample",
    )(a, a, a, a, b, b, gates, m0_pad, n0, c0, bias_row, g_head, ws0_row, bs0_row)


def _out_kernel(x_ref, mix_ref, gt1_ref, sh2_ref, sc2_ref, gt2_ref, wo_ref, g2_ref, wgu_ref, wdn_ref, gf_ref,
                y_ref):
    x = x_ref[...]
    x1 = x + gt1_ref[...] * _dot(mix_ref[...].astype(BF16), wo_ref[...])
    h2 = _rms(x1) * g2_ref[...]
    h2 = (h2 * (1.0 + sc2_ref[...]) + sh2_ref[...]).astype(BF16)
    gate = _dot(h2, wgu_ref[:, :D_FF])
    up = _dot(h2, wgu_ref[:, D_FF:])
    act = (gate * _sigmoid(gate) * up).astype(BF16)
    x2 = x1 + gt2_ref[...] * _dot(act, wdn_ref[...])
    y_ref[...] = _rms(x2) * gf_ref[...]


def _output_stage(x, mix, mod, w_out, g2, w_gu, w_dn, g_final, *, tm, per_row):
    groups, t, _ = x.shape
    return pl.pallas_call(
        _out_kernel,
        grid=(groups, t // tm),
        in_specs=[
            pl.BlockSpec((None, tm, D_MODEL), lambda g, i: (g, i, 0)),
            pl.BlockSpec((None, tm, D_MODEL), lambda g, i: (g, i, 0)),
            _mod_spec(2, tm, per_row),
            _mod_spec(3, tm, per_row),
            _mod_spec(4, tm, per_row),
            _mod_spec(5, tm, per_row),
            _const_spec((D_MODEL, D_MODEL)),
            _const_spec((1, D_MODEL)),
            _const_spec((D_MODEL, 2 * D_FF)),
            _const_spec((D_FF, D_MODEL)),
            _const_spec((1, D_MODEL)),
        ],
        out_specs=pl.BlockSpec((None, tm, D_MODEL), lambda g, i: (g, i, 0)),
        out_shape=jax.ShapeDtypeStruct((groups, t, D_MODEL), F32),
        compiler_params=pltpu.CompilerParams(
            dimension_semantics=("arbitrary", "arbitrary"), vmem_limit_bytes=VMEM_LIMIT),
        name="output_stage",
    )(x, mix, mod, mod, mod, mod, w_out, g2, w_gu, w_dn, g_final)


def kernel(x_prompt, x_sample, c_prompt, c_sample, state_mlstm_C, state_mlstm_n, state_mlstm_m, w_ada, b_ada, g_norm1, w_in, b_gate, g_mlstm_head, ln_v_g, ln_v_b, w_s, b_s, w_out, g_norm2, w_gate_up, w_down, g_final):
    depth = w_ada.shape[0]
    assert depth == 1, "single-layer trunk"
    batch, seq, _ = x_prompt.shape
    nb = x_sample.shape[0]
    assert x_sample.shape[1] == 1

    wi = w_in[0]
    gw = GROUP_WIDTH
    o_q, o_k, o_v, o_o, o_i = 0, gw, 2 * gw, 3 * gw, 4 * gw
    o_u = o_i + 2 * HEADS
    o_vs = o_u + gw
    w_proj = jnp.concatenate(
        [wi[:, o_q:o_o], wi[:, o_vs:o_vs + gw], wi[:, o_o:o_i], wi[:, o_u:o_vs],
         jnp.pad(wi[:, o_i:o_u], ((0, 0), (0, LANES - 2 * HEADS)))], axis=1).astype(BF16)
    w_out_b = w_out[0].astype(BF16)
    w_gu_b = w_gate_up[0].astype(BF16)
    w_dn_b = w_down[0].astype(BF16)
    g1 = g_norm1[0].reshape(1, D_MODEL)
    g2 = g_norm2[0].reshape(1, D_MODEL)
    gf = g_final.reshape(1, D_MODEL)
    ln_g = ln_v_g[0].reshape(1, gw)
    ln_b = ln_v_b[0].reshape(1, gw)
    g_head = g_mlstm_head[0].reshape(1, gw)
    bias_row = jnp.pad(b_gate[0], (0, LANES - 2 * HEADS)).reshape(1, LANES)
    tril = jnp.tril(jnp.ones((CHUNK, CHUNK), dtype=bool))
    ws_tril = jnp.where(tril[None], w_s[0], 0.0).astype(BF16)
    bs_col = jnp.pad(b_s[0].T, ((0, 0), (0, LANES - HEADS)))
    ws0_row = jnp.repeat(w_s[0][:, 0, 0], HEAD_DIM).reshape(1, gw)
    bs0_row = jnp.repeat(b_s[0][:, 0], HEAD_DIM).reshape(1, gw)

    rows = batch + nb
    rows_pad = -(-rows // 16) * 16
    c_all = jnp.concatenate([c_prompt, c_sample, jnp.zeros((rows_pad - rows, D_MODEL), F32)], axis=0)
    mod = _modulation(c_all, w_ada[0], b_ada[0].reshape(1, -1))
    mod_p = mod[:batch].reshape(batch, 1, N_MOD * D_MODEL)
    mod_s = mod[batch:rows].reshape(1, nb, N_MOD * D_MODEL)

    a_p, b_p, g_p = _project(x_prompt, mod_p, g1, w_proj, ln_g, ln_b, tm=512, per_row=False, a_dtype=BF16)
    mix_p, c_p, n_p, m_p = _mix_prompt(a_p, b_p, g_p, bias_row, g_head, ws_tril, bs_col)
    y_p = _output_stage(x_prompt, mix_p, mod_p, w_out_b, g2, w_gu_b, w_dn_b, gf, tm=512, per_row=False)

    xs = x_sample.reshape(1, nb, D_MODEL)
    a_s, b_s_act, g_s = _project(xs, mod_s, g1, w_proj, ln_g, ln_b, tm=nb, per_row=True, a_dtype=F32)
    a_s2, b_s2, g_s2 = a_s[0], b_s_act[0], g_s[0]
    m0_pad = jnp.pad(state_mlstm_m[0], ((0, 0), (0, LANES - HEADS)))
    n0 = state_mlstm_n[0].reshape(nb, gw)
    mix_s, c_s, n_s, m_s = _mix_sample(a_s2, b_s2, g_s2, m0_pad, n0, state_mlstm_C[0],
                                       bias_row, g_head, ws0_row, bs0_row)
    y_s = _output_stage(xs, mix_s.reshape(1, nb, D_MODEL), mod_s, w_out_b, g2, w_gu_b, w_dn_b, gf,
                        tm=nb, per_row=True)

    return (
        y_p,
        y_s.reshape(nb, 1, D_MODEL),
        c_p[None],
        n_p[None],
        m_p[:, :, 0][None],
        c_s[None],
        n_s.reshape(nb, HEADS, HEAD_DIM)[None],
        m_s.reshape(nb, HEADS, HEAD_DIM)[:, :, 0][None],
        a_s2[:, 3 * gw:].reshape(nb, 1, HEADS, HEAD_DIM)[None],
    )
```

```python
import functools
import math

import jax
import jax.numpy as jnp
from jax import lax
from jax.experimental import pallas as pl
from jax.experimental.pallas import tpu as pltpu

F32 = jnp.float32
BF16 = jnp.bfloat16

D_MODEL = 1024
HEADS = 4
HEAD_DIM = 128
GROUP_WIDTH = HEADS * HEAD_DIM
CHUNK = 128
D_FF = 2816
N_MOD = 6
EPS = 1e-6
QK_SCALE = HEAD_DIM ** -0.5
LANES = 128

N_A = 4 * GROUP_WIDTH
N_B = 2 * GROUP_WIDTH
N_PROJ = N_A + N_B + LANES

VMEM_LIMIT = 56 * 1024 * 1024


def _dot(a, b):
    return jnp.dot(a, b, preferred_element_type=F32)


def _dot_nt(a, b):
    return lax.dot_general(a, b, (((1,), (1,)), ((), ())), preferred_element_type=F32)


def _dot_tn(a, b):
    return lax.dot_general(a, b, (((0,), (0,)), ((), ())), preferred_element_type=F32)


def _sigmoid(x):
    return 1.0 / (1.0 + jnp.exp(-x))


def _gelu_tanh(x):
    c = math.sqrt(2.0 / math.pi)
    return x * (0.5 * (1.0 + jnp.tanh(c * (x + 0.044715 * (x * x * x)))))


def _log_sigmoid(x):
    return jnp.minimum(x, 0.0) - jnp.log1p(jnp.exp(-jnp.abs(x)))


def _rms(x):
    return x * lax.rsqrt(jnp.mean(x * x, axis=-1, keepdims=True) + EPS)


def _split3_bf16(x):
    hi = x.astype(BF16)
    r1 = x - hi.astype(F32)
    mid = r1.astype(BF16)
    lo = (r1 - mid.astype(F32)).astype(BF16)
    return hi, mid, lo


def _mod_kernel(c_ref, w_ref, b_ref, o_ref):
    c = c_ref[...]
    a = (c * _sigmoid(c)).astype(BF16)
    o_ref[...] = _dot(a, w_ref[...].astype(BF16)) + b_ref[...]


def _modulation(c_all, w_ada, b_ada):
    rows = c_all.shape[0]
    tn = 1024
    return pl.pallas_call(
        _mod_kernel,
        grid=(N_MOD * D_MODEL // tn,),
        in_specs=[
            pl.BlockSpec((rows, D_MODEL), lambda j: (0, 0)),
            pl.BlockSpec((D_MODEL, tn), lambda j: (0, j)),
            pl.BlockSpec((1, tn), lambda j: (0, j)),
        ],
        out_specs=pl.BlockSpec((rows, tn), lambda j: (0, j)),
        out_shape=jax.ShapeDtypeStruct((rows, N_MOD * D_MODEL), F32),
        compiler_params=pltpu.CompilerParams(dimension_semantics=("arbitrary",)),
        name="modulation",
    )(c_all, w_ada, b_ada)


def _mod_spec(piece, tm, per_row):
    if per_row:
        return pl.BlockSpec((None, tm, D_MODEL), lambda g, t: (g, t, piece))
    return pl.BlockSpec((None, 1, D_MODEL), lambda g, t: (g, 0, piece))


def _const_spec(shape):
    nd = len(shape)
    return pl.BlockSpec(shape, lambda g, t: (0,) * nd, pipeline_mode=pl.Buffered(1))


def _proj_body(x_ref, sh_ref, sc_ref, g1_ref, w_ref, lng_ref, lnb_ref, a_ref, b_ref, g_ref):
    n_a = a_ref.shape[1]
    n_copy = n_a - GROUP_WIDTH
    x = x_ref[...]
    h = _rms(x) * g1_ref[...]
    h = (h * (1.0 + sc_ref[...]) + sh_ref[...]).astype(BF16)
    p = _dot(h, w_ref[...])
    a_ref[:, :n_copy] = p[:, :n_copy].astype(a_ref.dtype)
    vs = _gelu_tanh(p[:, n_copy:n_a])
    for hd in range(HEADS):
        sl = slice(hd * HEAD_DIM, (hd + 1) * HEAD_DIM)
        v = vs[:, sl]
        mu = jnp.mean(v, axis=-1, keepdims=True)
        vc = v - mu
        var = jnp.mean(vc * vc, axis=-1, keepdims=True)
        y = vc * lax.rsqrt(var + EPS) * lng_ref[:, sl] + lnb_ref[:, sl]
        a_ref[:, n_copy + hd * HEAD_DIM: n_copy + (hd + 1) * HEAD_DIM] = y.astype(a_ref.dtype)
    b_ref[:, :GROUP_WIDTH] = _sigmoid(p[:, n_a: n_a + GROUP_WIDTH])
    b_ref[:, GROUP_WIDTH:] = _gelu_tanh(p[:, n_a + GROUP_WIDTH: n_a + N_B])
    g_ref[...] = p[:, n_a + N_B:]
    return h


def _proj_kernel(x_ref, sh_ref, sc_ref, g1_ref, w_ref, lng_ref, lnb_ref, a_ref, b_ref, g_ref):
    _proj_body(x_ref, sh_ref, sc_ref, g1_ref, w_ref, lng_ref, lnb_ref, a_ref, b_ref, g_ref)


def _proj_kt_kernel(x_ref, sh_ref, sc_ref, g1_ref, w_ref, lng_ref, lnb_ref, wkt_ref, a_ref, b_ref, g_ref, kt_ref):
    h = _proj_body(x_ref, sh_ref, sc_ref, g1_ref, w_ref, lng_ref, lnb_ref, a_ref, b_ref, g_ref)
    kt_ref[...] = _dot_nt(wkt_ref[...], h).astype(kt_ref.dtype)


def _project(x, mod, g1, w_proj, ln_g, ln_b, w_kt=None, *, tm, per_row, a_dtype):
    groups, t, _ = x.shape
    n_proj = w_proj.shape[1]
    n_a = n_proj - N_B - LANES
    in_specs = [
        pl.BlockSpec((None, tm, D_MODEL), lambda g, i: (g, i, 0)),
        _mod_spec(0, tm, per_row),
        _mod_spec(1, tm, per_row),
        _const_spec((1, D_MODEL)),
        _const_spec((D_MODEL, n_proj)),
        _const_spec((1, GROUP_WIDTH)),
        _const_spec((1, GROUP_WIDTH)),
    ]
    out_specs = [
        pl.BlockSpec((None, tm, n_a), lambda g, i: (g, i, 0)),
        pl.BlockSpec((None, tm, N_B), lambda g, i: (g, i, 0)),
        pl.BlockSpec((None, tm, LANES), lambda g, i: (g, i, 0)),
    ]
    out_shape = [
        jax.ShapeDtypeStruct((groups, t, n_a), a_dtype),
        jax.ShapeDtypeStruct((groups, t, N_B), F32),
        jax.ShapeDtypeStruct((groups, t, LANES), F32),
    ]
    args = [x, mod, mod, g1, w_proj, ln_g, ln_b]
    body = _proj_kernel
    if w_kt is not None:
        body = _proj_kt_kernel
        in_specs.append(_const_spec((GROUP_WIDTH, D_MODEL)))
        out_specs.append(pl.BlockSpec((None, GROUP_WIDTH, tm), lambda g, i: (g, 0, i)))
        out_shape.append(jax.ShapeDtypeStruct((groups, GROUP_WIDTH, t), a_dtype))
        args.append(w_kt)
    return pl.pallas_call(
        body,
        grid=(groups, t // tm),
        in_specs=in_specs,
        out_specs=out_specs,
        out_shape=out_shape,
        compiler_params=pltpu.CompilerParams(
            dimension_semantics=("arbitrary", "arbitrary"), vmem_limit_bytes=VMEM_LIMIT),
        name="project",
    )(*args)


MIX_TILE = 512


def _dot3_rhs(lhs_bf16, rhs_f32):
    hi, mid, lo = _split3_bf16(rhs_f32)
    return _dot(lhs_bf16, hi) + _dot(lhs_bf16, mid) + _dot(lhs_bf16, lo)


def _dot3_lhs(lhs_f32, rhs_bf16):
    hi, mid, lo = _split3_bf16(lhs_f32)
    return _dot(hi, rhs_bf16) + _dot(mid, rhs_bf16) + _dot(lo, rhs_bf16)


def _mix_prompt_kernel(a_ref, kt_ref, b_ref, g_ref, bias_ref, gh_ref, ws_ref, bs_ref,
                       o_ref, c_out, n_out, m_out, cta_ref, m_ref):
    tile = pl.program_id(1)
    n_chunks = a_ref.shape[0] // CHUNK

    @pl.when(tile == 0)
    def _():
        cta_ref[...] = jnp.zeros_like(cta_ref)
        m_ref[...] = jnp.zeros_like(m_ref)

    row = lax.broadcasted_iota(jnp.int32, (CHUNK, CHUNK), 0)
    col = lax.broadcasted_iota(jnp.int32, (CHUNK, CHUNK), 1)
    causal = row >= col
    tril = jnp.where(causal, 1.0, 0.0).astype(BF16)
    triu = jnp.where(row <= col, 1.0, 0.0).astype(BF16)
    ones_blk = jnp.ones((CHUNK, HEAD_DIM), BF16)

    bc, arow, blast = [], [], []
    for c in range(n_chunks):
        pre_c = g_ref[c * CHUNK:(c + 1) * CHUNK, :] + bias_ref[...]
        bc.append(_dot3_rhs(tril, _log_sigmoid(pre_c)))
        rows = pre_c.T[0:2 * HEADS, :]
        b_rows = _dot3_lhs(_log_sigmoid(rows), triu)
        arow.append([rows[hd:hd + 1, :] - b_rows[HEADS + hd:HEADS + hd + 1, :] for hd in range(HEADS)])
        blast.append([jnp.min(b_rows[HEADS + hd:HEADS + hd + 1, :], axis=1, keepdims=True) for hd in range(HEADS)])

    n_copy = a_ref.shape[1] - GROUP_WIDTH
    for hd in range(HEADS):
        sl = slice(hd * HEAD_DIM, (hd + 1) * HEAD_DIM)
        vsn_all = jnp.concatenate(
            [a_ref[c * CHUNK:(c + 1) * CHUNK, n_copy + hd * HEAD_DIM: n_copy + (hd + 1) * HEAD_DIM]
             for c in range(n_chunks)], axis=1)
        sg_all = _dot(ws_ref[hd], vsn_all)
        bs_b = bs_ref[hd]
        m_prev = jnp.max(m_ref[hd][0:1, :], axis=1, keepdims=True)
        cta = cta_ref[hd]
        for c in range(n_chunks):
            rs = slice(c * CHUNK, (c + 1) * CHUNK)
            q = a_ref[rs, sl]
            v = a_ref[rs, GROUP_WIDTH + hd * HEAD_DIM: GROUP_WIDTH + (hd + 1) * HEAD_DIM]
            kt = kt_ref[sl, rs]
            va = jnp.concatenate([v, ones_blk], axis=1)
            a_row = arow[c][hd]
            b_last = blast[c][hd]
            b_rep = jnp.broadcast_to(bc[c][:, HEADS + hd: HEADS + hd + 1], (CHUNK, HEAD_DIM))

            amat = jnp.where(causal, a_row, -jnp.inf)
            m_row = jnp.max(amat, axis=1, keepdims=True)
            s_loc = _dot(q, kt) * jnp.exp(amat - m_row)
            nd_loc = _dot(s_loc.astype(BF16), va)
            g_row = b_last + a_row
            g_loc = jnp.max(g_row, axis=1, keepdims=True)
            kw = (kt.astype(F32) * jnp.exp(g_row - g_loc)).astype(BF16)
            u_aug = _dot(kw, va)

            inter = _dot(q, cta.astype(BF16))
            mm = jnp.maximum(m_prev, m_row)
            f_loc = jnp.exp(m_row - mm) * QK_SCALE
            f_int = jnp.exp(m_prev - mm) * QK_SCALE
            nd = f_loc * nd_loc + f_int * inter
            clamp = jnp.exp(-(b_rep + mm))
            hh = nd[:, :HEAD_DIM] / jnp.maximum(jnp.abs(nd[:, HEAD_DIM:]), clamp)
            ml = _rms(hh) * gh_ref[:, sl] * b_ref[rs, sl]
            o_ref[rs, sl] = ml.astype(o_ref.dtype)

            dec = b_last + m_prev
            m_new = jnp.maximum(dec, g_loc)
            cta = jnp.exp(dec - m_new) * cta + jnp.exp(g_loc - m_new) * u_aug
            m_prev = m_new

            sg = sg_all[:, c * CHUNK:(c + 1) * CHUNK] + bs_b
            ug = b_ref[rs, GROUP_WIDTH + hd * HEAD_DIM: GROUP_WIDTH + (hd + 1) * HEAD_DIM]
            o_ref[rs, GROUP_WIDTH + hd * HEAD_DIM: GROUP_WIDTH + (hd + 1) * HEAD_DIM] = (ug * sg).astype(o_ref.dtype)
        cta_ref[hd] = cta
        m_ref[hd] = jnp.broadcast_to(m_prev, m_ref.shape[1:])

    @pl.when(tile == pl.num_programs(1) - 1)
    def _():
        for hd in range(HEADS):
            cta = cta_ref[hd]
            c_out[hd] = cta[:, :HEAD_DIM].T
            n_out[hd: hd + 1, :] = cta[:, HEAD_DIM:].T[0:1, :]
            m_out[hd: hd + 1, :] = m_ref[hd][0:1, :]


def _mix_prompt(a, kt, b, gates, bias_row, g_head, ws_tril, bs_rep):
    groups, t, n_a = a.shape
    tm = MIX_TILE
    return pl.pallas_call(
        _mix_prompt_kernel,
        grid=(groups, t // tm),
        in_specs=[
            pl.BlockSpec((None, tm, n_a), lambda g, c: (g, c, 0)),
            pl.BlockSpec((None, GROUP_WIDTH, tm), lambda g, c: (g, 0, c)),
            pl.BlockSpec((None, tm, N_B), lambda g, c: (g, c, 0)),
            pl.BlockSpec((None, tm, LANES), lambda g, c: (g, c, 0)),
            _const_spec((1, LANES)),
            _const_spec((1, GROUP_WIDTH)),
            _const_spec((HEADS, CHUNK, CHUNK)),
            _const_spec((HEADS, CHUNK, LANES)),
        ],
        out_specs=[
            pl.BlockSpec((None, tm, D_MODEL), lambda g, c: (g, c, 0)),
            pl.BlockSpec((None, HEADS, HEAD_DIM, HEAD_DIM), lambda g, c: (g, 0, 0, 0)),
            pl.BlockSpec((None, HEADS, HEAD_DIM), lambda g, c: (g, 0, 0)),
            pl.BlockSpec((None, HEADS, LANES), lambda g, c: (g, 0, 0)),
        ],
        out_shape=[
            jax.ShapeDtypeStruct((groups, t, D_MODEL), BF16),
            jax.ShapeDtypeStruct((groups, HEADS, HEAD_DIM, HEAD_DIM), F32),
            jax.ShapeDtypeStruct((groups, HEADS, HEAD_DIM), F32),
            jax.ShapeDtypeStruct((groups, HEADS, LANES), F32),
        ],
        scratch_shapes=[
            pltpu.VMEM((HEADS, HEAD_DIM, 2 * HEAD_DIM), F32),
            pltpu.VMEM((HEADS, 8, LANES), F32),
        ],
        compiler_params=pltpu.CompilerParams(
            dimension_semantics=("arbitrary", "arbitrary"), vmem_limit_bytes=VMEM_LIMIT),
        name="mix_prompt",
    )(a, kt, b, gates, bias_row, g_head, ws_tril, bs_rep)


SAMPLE_TOKENS_PER_STEP = 32


def _mix_sample_kernel(q_ref, k_ref, v_ref, vsn_ref, og_ref, ug_ref, g_ref, m0_ref, n0_ref, c_ref,
                       bias_ref, gh_ref, ws0_ref, bs0_ref,
                       mix_ref, c_out, n_out, m_out,
                       vt_ref, kp_ref, wd_ref, cqt_ref):
    hd = pl.program_id(0)
    grp = pl.program_id(1)
    nb = q_ref.shape[0]
    lane = lax.broadcasted_iota(jnp.int32, (nb, LANES), 1)

    def gate_terms():
        pre = g_ref[...] + bias_ref[...]
        i_pre = jnp.sum(jnp.where(lane == hd, pre, 0.0), axis=1, keepdims=True)
        f_pre = jnp.sum(jnp.where(lane == hd + HEADS, pre, 0.0), axis=1, keepdims=True)
        m_prev = jnp.sum(jnp.where(lane == hd, m0_ref[...], 0.0), axis=1, keepdims=True)
        inter = _log_sigmoid(f_pre) + m_prev
        m_t = jnp.maximum(inter, i_pre)
        return m_t, jnp.exp(i_pre - m_t), jnp.exp(inter - m_t)

    @pl.when(grp == 0)
    def _():
        _, w_in, w_dec = gate_terms()
        vt_ref[...] = v_ref[...].T
        kp_ref[...] = (w_in * k_ref[...]).astype(BF16)
        wd_ref[...] = jnp.broadcast_to(w_dec, wd_ref.shape)
        cqt_ref[...] = jnp.zeros_like(cqt_ref)

    lane_sq = lax.broadcasted_iota(jnp.int32, (HEAD_DIM, LANES), 1)
    tokens = c_ref.shape[0]

    def body(r, carry):
        tok = grp * tokens + r
        q_row = q_ref[pl.ds(tok, 1), :]
        c_b = c_ref[r]
        sel = lane_sq == tok
        cq_col = jnp.sum(c_b * q_row, axis=1, keepdims=True)
        cqt_ref[...] = jnp.where(sel, cq_col, cqt_ref[...])
        v_sel = jnp.where(sel, vt_ref[...], 0.0).astype(BF16)
        outer = _dot(v_sel, kp_ref[...])
        c_out[r] = wd_ref[pl.ds(tok, 1), :] * c_b + outer
        return carry

    lax.fori_loop(0, tokens, body, 0)

    @pl.when(grp == pl.num_programs(1) - 1)
    def _():
        m_t, w_in, w_dec = gate_terms()
        q = q_ref[...]
        k = k_ref[...]
        v = v_ref[...]
        n0 = n0_ref[...]
        cq = cqt_ref[...].T
        s = jnp.sum(q * k, axis=1, keepdims=True) * (QK_SCALE * w_in)
        w_inter = w_dec * QK_SCALE
        num = s * v + w_inter * cq
        den = s + w_inter * jnp.sum(n0 * q, axis=1, keepdims=True)
        hh = num / jnp.maximum(jnp.abs(den), jnp.exp(-m_t))
        ml = _rms(hh) * gh_ref[...] * og_ref[...]
        cm = ug_ref[...] * (ws0_ref[...] * vsn_ref[...] + bs0_ref[...])
        n_out[...] = w_dec * n0 + w_in * k
        m_out[...] = jnp.broadcast_to(m_t, m_out.shape)
        for hh_static in range(HEADS):
            @pl.when(hd == hh_static)
            def _():
                mix_ref[:, hh_static * HEAD_DIM: (hh_static + 1) * HEAD_DIM] = ml
                mix_ref[:, GROUP_WIDTH + hh_static * HEAD_DIM: GROUP_WIDTH + (hh_static + 1) * HEAD_DIM] = cm


def _mix_sample(a, b, gates, m0_pad, n0, c0, bias_row, g_head, ws0_row, bs0_row):
    nb = a.shape[0]
    tb = SAMPLE_TOKENS_PER_STEP

    def head_block(offset):
        return pl.BlockSpec((nb, HEAD_DIM), lambda h, g: (0, offset + h))

    full = pl.BlockSpec((nb, LANES), lambda h, g: (0, 0))
    head_row = pl.BlockSpec((1, HEAD_DIM), lambda h, g: (0, h))
    c_spec = pl.BlockSpec((tb, None, HEAD_DIM, HEAD_DIM), lambda h, g: (g, h, 0, 0))
    return pl.pallas_call(
        _mix_sample_kernel,
        grid=(HEADS, nb // tb),
        in_specs=[
            head_block(0), head_block(HEADS), head_block(2 * HEADS), head_block(3 * HEADS),
            head_block(0), head_block(HEADS),
            full, full, head_block(0), c_spec,
            pl.BlockSpec((1, LANES), lambda h, g: (0, 0)),
            head_row, head_row, head_row,
        ],
        out_specs=[
            pl.BlockSpec((nb, D_MODEL), lambda h, g: (0, 0)),
            c_spec,
            head_block(0),
            head_block(0),
        ],
        out_shape=[
            jax.ShapeDtypeStruct((nb, D_MODEL), F32),
            jax.ShapeDtypeStruct(c0.shape, F32),
            jax.ShapeDtypeStruct((nb, GROUP_WIDTH), F32),
            jax.ShapeDtypeStruct((nb, GROUP_WIDTH), F32),
        ],
        scratch_shapes=[
            pltpu.VMEM((HEAD_DIM, nb), F32),
            pltpu.VMEM((nb, HEAD_DIM), BF16),
            pltpu.VMEM((nb, LANES), F32),
            pltpu.VMEM((HEAD_DIM, nb), F32),
        ],
        compiler_params=pltpu.CompilerParams(
            dimension_semantics=("arbitrary", "arbitrary"), vmem_limit_bytes=VMEM_LIMIT),
        name="mix_sample",
    )(a, a, a, a, b, b, gates, m0_pad, n0, c0, bias_row, g_head, ws0_row, bs0_row)


def _out_kernel(x_ref, mix_ref, gt1_ref, sh2_ref, sc2_ref, gt2_ref, wo_ref, g2_ref, wgu_ref, wdn_ref, gf_ref,
                y_ref):
    x = x_ref[...]
    x1 = x + gt1_ref[...] * _dot(mix_ref[...].astype(BF16), wo_ref[...])
    h2 = _rms(x1) * g2_ref[...]
    h2 = (h2 * (1.0 + sc2_ref[...]) + sh2_ref[...]).astype(BF16)
    gate = _dot(h2, wgu_ref[:, :D_FF])
    up = _dot(h2, wgu_ref[:, D_FF:])
    act = (gate * _sigmoid(gate) * up).astype(BF16)
    x2 = x1 + gt2_ref[...] * _dot(act, wdn_ref[...])
    y_ref[...] = _rms(x2) * gf_ref[...]


def _output_stage(x, mix, mod, w_out, g2, w_gu, w_dn, g_final, *, tm, per_row):
    groups, t, _ = x.shape
    return pl.pallas_call(
        _out_kernel,
        grid=(groups, t // tm),
        in_specs=[
            pl.BlockSpec((None, tm, D_MODEL), lambda g, i: (g, i, 0)),
            pl.BlockSpec((None, tm, D_MODEL), lambda g, i: (g, i, 0)),
            _mod_spec(2, tm, per_row),
            _mod_spec(3, tm, per_row),
            _mod_spec(4, tm, per_row),
            _mod_spec(5, tm, per_row),
            _const_spec((D_MODEL, D_MODEL)),
            _const_spec((1, D_MODEL)),
            _const_spec((D_MODEL, 2 * D_FF)),
            _const_spec((D_FF, D_MODEL)),
            _const_spec((1, D_MODEL)),
        ],
        out_specs=pl.BlockSpec((None, tm, D_MODEL), lambda g, i: (g, i, 0)),
        out_shape=jax.ShapeDtypeStruct((groups, t, D_MODEL), F32),
        compiler_params=pltpu.CompilerParams(
            dimension_semantics=("arbitrary", "arbitrary"), vmem_limit_bytes=VMEM_LIMIT),
        name="output_stage",
    )(x, mix, mod, mod, mod, mod, w_out, g2, w_gu, w_dn, g_final)


def kernel(x_prompt, x_sample, c_prompt, c_sample, state_mlstm_C, state_mlstm_n, state_mlstm_m, w_ada, b_ada, g_norm1, w_in, b_gate, g_mlstm_head, ln_v_g, ln_v_b, w_s, b_s, w_out, g_norm2, w_gate_up, w_down, g_final):
    depth = w_ada.shape[0]
    assert depth == 1, "single-layer trunk"
    batch, seq, _ = x_prompt.shape
    nb = x_sample.shape[0]
    assert x_sample.shape[1] == 1

    wi = w_in[0]
    gw = GROUP_WIDTH
    o_q, o_k, o_v, o_o, o_i = 0, gw, 2 * gw, 3 * gw, 4 * gw
    o_u = o_i + 2 * HEADS
    o_vs = o_u + gw
    w_tail = [wi[:, o_vs:o_vs + gw], wi[:, o_o:o_i], wi[:, o_u:o_vs],
              jnp.pad(wi[:, o_i:o_u], ((0, 0), (0, LANES - 2 * HEADS)))]
    w_proj_s = jnp.concatenate([wi[:, o_q:o_o]] + w_tail, axis=1).astype(BF16)
    w_proj_p = jnp.concatenate([wi[:, o_q:o_k], wi[:, o_v:o_o]] + w_tail, axis=1).astype(BF16)
    w_kt = wi[:, o_k:o_v].T.astype(BF16)
    w_out_b = w_out[0].astype(BF16)
    w_gu_b = w_gate_up[0].astype(BF16)
    w_dn_b = w_down[0].astype(BF16)
    g1 = g_norm1[0].reshape(1, D_MODEL)
    g2 = g_norm2[0].reshape(1, D_MODEL)
    gf = g_final.reshape(1, D_MODEL)
    ln_g = ln_v_g[0].reshape(1, gw)
    ln_b = ln_v_b[0].reshape(1, gw)
    g_head = g_mlstm_head[0].reshape(1, gw)
    bias_row = jnp.pad(b_gate[0], (0, LANES - 2 * HEADS)).reshape(1, LANES)
    tril = jnp.tril(jnp.ones((CHUNK, CHUNK), dtype=bool))
    ws_tril = jnp.where(tril[None], w_s[0], 0.0).astype(BF16)
    bs_rep = jnp.broadcast_to(b_s[0][:, :, None], (HEADS, CHUNK, LANES))
    ws0_row = jnp.repeat(w_s[0][:, 0, 0], HEAD_DIM).reshape(1, gw)
    bs0_row = jnp.repeat(b_s[0][:, 0], HEAD_DIM).reshape(1, gw)

    rows = batch + nb
    rows_pad = -(-rows // 16) * 16
    c_all = jnp.concatenate([c_prompt, c_sample, jnp.zeros((rows_pad - rows, D_MODEL), F32)], axis=0)
    mod = _modulation(c_all, w_ada[0], b_ada[0].reshape(1, -1))
    mod_p = mod[:batch].reshape(batch, 1, N_MOD * D_MODEL)
    mod_s = mod[batch:rows].reshape(1, nb, N_MOD * D_MODEL)

    a_p, b_p, g_p, kt_p = _project(x_prompt, mod_p, g1, w_proj_p, ln_g, ln_b, w_kt,
                                   tm=MIX_TILE, per_row=False, a_dtype=BF16)
    mix_p, c_p, n_p, m_p = _mix_prompt(a_p, kt_p, b_p, g_p, bias_row, g_head, ws_tril, bs_rep)
    y_p = _output_stage(x_prompt, mix_p, mod_p, w_out_b, g2, w_gu_b, w_dn_b, gf, tm=512, per_row=False)

    xs = x_sample.reshape(1, nb, D_MODEL)
    a_s, b_s_act, g_s = _project(xs, mod_s, g1, w_proj_s, ln_g, ln_b, tm=nb, per_row=True, a_dtype=F32)
    a_s2, b_s2, g_s2 = a_s[0], b_s_act[0], g_s[0]
    m0_pad = jnp.pad(state_mlstm_m[0], ((0, 0), (0, LANES - HEADS)))
    n0 = state_mlstm_n[0].reshape(nb, gw)
    mix_s, c_s, n_s, m_s = _mix_sample(a_s2, b_s2, g_s2, m0_pad, n0, state_mlstm_C[0],
                                       bias_row, g_head, ws0_row, bs0_row)
    y_s = _output_stage(xs, mix_s.reshape(1, nb, D_MODEL), mod_s, w_out_b, g2, w_gu_b, w_dn_b, gf,
                        tm=nb, per_row=True)

    return (
        y_p,
        y_s.reshape(nb, 1, D_MODEL),
        c_p[None],
        n_p[None],
        m_p[:, :, 0][None],
        c_s[None],
        n_s.reshape(nb, HEADS, HEAD_DIM)[None],
        m_s.reshape(nb, HEADS, HEAD_DIM)[:, :, 0][None],
        a_s2[:, 3 * gw:].reshape(nb, 1, HEADS, HEAD_DIM)[None],
    )
```

```python
import math

import jax
import jax.numpy as jnp
from jax import lax
from jax.experimental import pallas as pl
from jax.experimental.pallas import tpu as pltpu

F32 = jnp.float32
BF16 = jnp.bfloat16

D_MODEL = 1024
HEADS = 4
HEAD_DIM = 128
GROUP_WIDTH = HEADS * HEAD_DIM
CHUNK = 128
D_FF = 2816
N_MOD = 6
EPS = 1e-6
QK_SCALE = HEAD_DIM ** -0.5
LANES = 128

N_B = 2 * GROUP_WIDTH

VMEM_LIMIT = 56 * 1024 * 1024


def _dot(a, b):
    return jnp.dot(a, b, preferred_element_type=F32)


def _dot_nt(a, b):
    return lax.dot_general(a, b, (((1,), (1,)), ((), ())), preferred_element_type=F32)


def _sigmoid(x):
    return 1.0 / (1.0 + jnp.exp(-x))


def _gelu_tanh(x):
    c = math.sqrt(2.0 / math.pi)
    return x * (0.5 * (1.0 + jnp.tanh(c * (x + 0.044715 * (x * x * x)))))


def _log_sigmoid(x):
    return jnp.minimum(x, 0.0) - jnp.log1p(jnp.exp(-jnp.abs(x)))


def _rms(x):
    return x * lax.rsqrt(jnp.mean(x * x, axis=-1, keepdims=True) + EPS)


def _split3_bf16(x):
    hi = x.astype(BF16)
    r1 = x - hi.astype(F32)
    mid = r1.astype(BF16)
    lo = (r1 - mid.astype(F32)).astype(BF16)
    return hi, mid, lo


def _mod_kernel(c_ref, w_ref, b_ref, o_ref):
    c = c_ref[...]
    a = (c * _sigmoid(c)).astype(BF16)
    o_ref[...] = _dot(a, w_ref[...].astype(BF16)) + b_ref[...]


def _modulation(c_all, w_ada, b_ada):
    rows = c_all.shape[0]
    tn = 1024
    return pl.pallas_call(
        _mod_kernel,
        grid=(N_MOD * D_MODEL // tn,),
        in_specs=[
            pl.BlockSpec((rows, D_MODEL), lambda j: (0, 0)),
            pl.BlockSpec((D_MODEL, tn), lambda j: (0, j)),
            pl.BlockSpec((1, tn), lambda j: (0, j)),
        ],
        out_specs=pl.BlockSpec((rows, tn), lambda j: (0, j)),
        out_shape=jax.ShapeDtypeStruct((rows, N_MOD * D_MODEL), F32),
        compiler_params=pltpu.CompilerParams(dimension_semantics=("arbitrary",)),
        name="modulation",
    )(c_all, w_ada, b_ada)


def _mod_spec(piece, tm, per_row):
    if per_row:
        return pl.BlockSpec((None, tm, D_MODEL), lambda g, t: (g, t, piece))
    return pl.BlockSpec((None, 1, D_MODEL), lambda g, t: (g, 0, piece))


def _const_spec(shape):
    nd = len(shape)
    return pl.BlockSpec(shape, lambda g, t: (0,) * nd, pipeline_mode=pl.Buffered(1))


def _proj_body(x_ref, sh_ref, sc_ref, g1_ref, w_ref, lng_ref, lnb_ref, a_ref, b_ref, g_ref):
    n_a = a_ref.shape[1]
    n_copy = n_a - GROUP_WIDTH
    x = x_ref[...]
    h = _rms(x) * g1_ref[...]
    h = (h * (1.0 + sc_ref[...]) + sh_ref[...]).astype(BF16)
    p = _dot(h, w_ref[...])
    a_ref[:, :n_copy] = p[:, :n_copy].astype(a_ref.dtype)
    vs = _gelu_tanh(p[:, n_copy:n_a])
    for hd in range(HEADS):
        sl = slice(hd * HEAD_DIM, (hd + 1) * HEAD_DIM)
        v = vs[:, sl]
        mu = jnp.mean(v, axis=-1, keepdims=True)
        vc = v - mu
        var = jnp.mean(vc * vc, axis=-1, keepdims=True)
        y = vc * lax.rsqrt(var + EPS) * lng_ref[:, sl] + lnb_ref[:, sl]
        a_ref[:, n_copy + hd * HEAD_DIM: n_copy + (hd + 1) * HEAD_DIM] = y.astype(a_ref.dtype)
    b_ref[:, :GROUP_WIDTH] = _sigmoid(p[:, n_a: n_a + GROUP_WIDTH])
    b_ref[:, GROUP_WIDTH:] = _gelu_tanh(p[:, n_a + GROUP_WIDTH: n_a + N_B])
    g_ref[...] = p[:, n_a + N_B:]
    return h


def _proj_k_kernel(x_ref, sh_ref, sc_ref, g1_ref, w_ref, lng_ref, lnb_ref, wk_ref, a_ref, b_ref, g_ref, k_ref):
    h = _proj_body(x_ref, sh_ref, sc_ref, g1_ref, w_ref, lng_ref, lnb_ref, a_ref, b_ref, g_ref)
    k_ref[...] = _dot(h, wk_ref[...]).astype(k_ref.dtype)


def _proj_kt_kernel(x_ref, sh_ref, sc_ref, g1_ref, w_ref, lng_ref, lnb_ref, wkt_ref, a_ref, b_ref, g_ref, kt_ref):
    h = _proj_body(x_ref, sh_ref, sc_ref, g1_ref, w_ref, lng_ref, lnb_ref, a_ref, b_ref, g_ref)
    kt_ref[...] = _dot_nt(wkt_ref[...], h).astype(kt_ref.dtype)


def _project(x, mod, g1, w_proj, ln_g, ln_b, w_k, *, k_transposed, tm, per_row, a_dtype):
    groups, t, _ = x.shape
    n_proj = w_proj.shape[1]
    n_a = n_proj - N_B - LANES
    in_specs = [
        pl.BlockSpec((None, tm, D_MODEL), lambda g, i: (g, i, 0)),
        _mod_spec(0, tm, per_row),
        _mod_spec(1, tm, per_row),
        _const_spec((1, D_MODEL)),
        _const_spec((D_MODEL, n_proj)),
        _const_spec((1, GROUP_WIDTH)),
        _const_spec((1, GROUP_WIDTH)),
    ]
    out_specs = [
        pl.BlockSpec((None, tm, n_a), lambda g, i: (g, i, 0)),
        pl.BlockSpec((None, tm, N_B), lambda g, i: (g, i, 0)),
        pl.BlockSpec((None, tm, LANES), lambda g, i: (g, i, 0)),
    ]
    out_shape = [
        jax.ShapeDtypeStruct((groups, t, n_a), a_dtype),
        jax.ShapeDtypeStruct((groups, t, N_B), F32),
        jax.ShapeDtypeStruct((groups, t, LANES), F32),
    ]
    args = [x, mod, mod, g1, w_proj, ln_g, ln_b, w_k]
    in_specs.append(_const_spec(w_k.shape))
    if k_transposed:
        out_specs.append(pl.BlockSpec((None, GROUP_WIDTH, tm), lambda g, i: (g, 0, i)))
        out_shape.append(jax.ShapeDtypeStruct((groups, GROUP_WIDTH, t), a_dtype))
    else:
        out_specs.append(pl.BlockSpec((None, tm, GROUP_WIDTH), lambda g, i: (g, i, 0)))
        out_shape.append(jax.ShapeDtypeStruct((groups, t, GROUP_WIDTH), a_dtype))
    return pl.pallas_call(
        _proj_kt_kernel if k_transposed else _proj_k_kernel,
        grid=(groups, t // tm),
        in_specs=in_specs,
        out_specs=out_specs,
        out_shape=out_shape,
        compiler_params=pltpu.CompilerParams(
            dimension_semantics=("arbitrary", "arbitrary"), vmem_limit_bytes=VMEM_LIMIT),
        name="project",
    )(*args)


MIX_TILE = 512


def _dot3_rhs(lhs_bf16, rhs_f32):
    hi, mid, lo = _split3_bf16(rhs_f32)
    return _dot(lhs_bf16, hi) + _dot(lhs_bf16, mid) + _dot(lhs_bf16, lo)


def _dot3_lhs(lhs_f32, rhs_bf16):
    hi, mid, lo = _split3_bf16(lhs_f32)
    return _dot(hi, rhs_bf16) + _dot(mid, rhs_bf16) + _dot(lo, rhs_bf16)


def _mix_prompt_kernel(a_ref, kt_ref, b_ref, g_ref, bias_ref, gh_ref, ws_ref, bs_ref,
                       o_ref, c_out, n_out, m_out, cta_ref, m_ref):
    tile = pl.program_id(1)
    n_chunks = a_ref.shape[0] // CHUNK

    @pl.when(tile == 0)
    def _():
        cta_ref[...] = jnp.zeros_like(cta_ref)
        m_ref[...] = jnp.zeros_like(m_ref)

    row = lax.broadcasted_iota(jnp.int32, (CHUNK, CHUNK), 0)
    col = lax.broadcasted_iota(jnp.int32, (CHUNK, CHUNK), 1)
    causal = row >= col
    tril = jnp.where(causal, 1.0, 0.0).astype(BF16)
    triu = jnp.where(row <= col, 1.0, 0.0).astype(BF16)
    ones_blk = jnp.ones((CHUNK, HEAD_DIM), BF16)

    bc, arow, blast = [], [], []
    for c in range(n_chunks):
        pre_c = g_ref[c * CHUNK:(c + 1) * CHUNK, :] + bias_ref[...]
        bc.append(_dot3_rhs(tril, _log_sigmoid(pre_c)))
        rows = pre_c.T[0:2 * HEADS, :]
        b_rows = _dot3_lhs(_log_sigmoid(rows), triu)
        arow.append([rows[hd:hd + 1, :] - b_rows[HEADS + hd:HEADS + hd + 1, :] for hd in range(HEADS)])
        blast.append([jnp.min(b_rows[HEADS + hd:HEADS + hd + 1, :], axis=1, keepdims=True) for hd in range(HEADS)])

    n_copy = a_ref.shape[1] - GROUP_WIDTH
    for hd in range(HEADS):
        sl = slice(hd * HEAD_DIM, (hd + 1) * HEAD_DIM)
        vsn_all = jnp.concatenate(
            [a_ref[c * CHUNK:(c + 1) * CHUNK, n_copy + hd * HEAD_DIM: n_copy + (hd + 1) * HEAD_DIM]
             for c in range(n_chunks)], axis=1)
        sg_all = _dot(ws_ref[hd], vsn_all)
        bs_b = bs_ref[hd]
        m_prev = jnp.max(m_ref[hd][0:1, :], axis=1, keepdims=True)
        cta = cta_ref[hd]
        for c in range(n_chunks):
            rs = slice(c * CHUNK, (c + 1) * CHUNK)
            q = a_ref[rs, sl]
            v = a_ref[rs, GROUP_WIDTH + hd * HEAD_DIM: GROUP_WIDTH + (hd + 1) * HEAD_DIM]
            kt = kt_ref[sl, rs]
            va = jnp.concatenate([v, ones_blk], axis=1)
            a_row = arow[c][hd]
            b_last = blast[c][hd]
            b_rep = jnp.broadcast_to(bc[c][:, HEADS + hd: HEADS + hd + 1], (CHUNK, HEAD_DIM))

            amat = jnp.where(causal, a_row, -jnp.inf)
            m_row = jnp.max(amat, axis=1, keepdims=True)
            s_loc = _dot(q, kt) * jnp.exp(amat - m_row)
            nd_loc = _dot(s_loc.astype(BF16), va)
            g_row = b_last + a_row
            g_loc = jnp.max(g_row, axis=1, keepdims=True)
            kw = (kt.astype(F32) * jnp.exp(g_row - g_loc)).astype(BF16)
            u_aug = _dot(kw, va)

            inter = _dot(q, cta.astype(BF16))
            mm = jnp.maximum(m_prev, m_row)
            f_loc = jnp.exp(m_row - mm) * QK_SCALE
            f_int = jnp.exp(m_prev - mm) * QK_SCALE
            nd = f_loc * nd_loc + f_int * inter
            clamp = jnp.exp(-(b_rep + mm))
            hh = nd[:, :HEAD_DIM] / jnp.maximum(jnp.abs(nd[:, HEAD_DIM:]), clamp)
            ml = _rms(hh) * gh_ref[:, sl] * b_ref[rs, sl]
            o_ref[rs, sl] = ml.astype(o_ref.dtype)

            dec = b_last + m_prev
            m_new = jnp.maximum(dec, g_loc)
            cta = jnp.exp(dec - m_new) * cta + jnp.exp(g_loc - m_new) * u_aug
            m_prev = m_new

            sg = sg_all[:, c * CHUNK:(c + 1) * CHUNK] + bs_b
            ug = b_ref[rs, GROUP_WIDTH + hd * HEAD_DIM: GROUP_WIDTH + (hd + 1) * HEAD_DIM]
            o_ref[rs, GROUP_WIDTH + hd * HEAD_DIM: GROUP_WIDTH + (hd + 1) * HEAD_DIM] = (ug * sg).astype(o_ref.dtype)
        cta_ref[hd] = cta
        m_ref[hd] = jnp.broadcast_to(m_prev, m_ref.shape[1:])

    @pl.when(tile == pl.num_programs(1) - 1)
    def _():
        for hd in range(HEADS):
            cta = cta_ref[hd]
            c_out[hd] = cta[:, :HEAD_DIM].T
            n_out[hd: hd + 1, :] = cta[:, HEAD_DIM:].T[0:1, :]
            m_out[hd: hd + 1, :] = m_ref[hd][0:1, :]


def _mix_prompt(a, kt, b, gates, bias_row, g_head, ws_tril, bs_rep):
    groups, t, n_a = a.shape
    tm = MIX_TILE
    return pl.pallas_call(
        _mix_prompt_kernel,
        grid=(groups, t // tm),
        in_specs=[
            pl.BlockSpec((None, tm, n_a), lambda g, c: (g, c, 0)),
            pl.BlockSpec((None, GROUP_WIDTH, tm), lambda g, c: (g, 0, c)),
            pl.BlockSpec((None, tm, N_B), lambda g, c: (g, c, 0)),
            pl.BlockSpec((None, tm, LANES), lambda g, c: (g, c, 0)),
            _const_spec((1, LANES)),
            _const_spec((1, GROUP_WIDTH)),
            _const_spec((HEADS, CHUNK, CHUNK)),
            _const_spec((HEADS, CHUNK, LANES)),
        ],
        out_specs=[
            pl.BlockSpec((None, tm, D_MODEL), lambda g, c: (g, c, 0)),
            pl.BlockSpec((None, HEADS, HEAD_DIM, HEAD_DIM), lambda g, c: (g, 0, 0, 0)),
            pl.BlockSpec((None, HEADS, HEAD_DIM), lambda g, c: (g, 0, 0)),
            pl.BlockSpec((None, HEADS, LANES), lambda g, c: (g, 0, 0)),
        ],
        out_shape=[
            jax.ShapeDtypeStruct((groups, t, D_MODEL), BF16),
            jax.ShapeDtypeStruct((groups, HEADS, HEAD_DIM, HEAD_DIM), F32),
            jax.ShapeDtypeStruct((groups, HEADS, HEAD_DIM), F32),
            jax.ShapeDtypeStruct((groups, HEADS, LANES), F32),
        ],
        scratch_shapes=[
            pltpu.VMEM((HEADS, HEAD_DIM, 2 * HEAD_DIM), F32),
            pltpu.VMEM((HEADS, 8, LANES), F32),
        ],
        compiler_params=pltpu.CompilerParams(
            dimension_semantics=("arbitrary", "arbitrary"), vmem_limit_bytes=VMEM_LIMIT),
        name="mix_prompt",
    )(a, kt, b, gates, bias_row, g_head, ws_tril, bs_rep)


SAMPLE_TOKENS_PER_STEP = 32
SAMPLE_UNROLL = 8


def _mix_sample_kernel(q_ref, k_ref, v_ref, vsn_ref, og_ref, ug_ref, g_ref, m0_ref, n0_ref, c_ref,
                       bias_ref, gh_ref, ws0_ref, bs0_ref,
                       mix_ref, c_out, n_out, m_out,
                       vt_ref, kp_ref, wd_ref, cqt_ref):
    hd = pl.program_id(0)
    grp = pl.program_id(1)
    nb = q_ref.shape[0]
    lane = lax.broadcasted_iota(jnp.int32, (nb, LANES), 1)

    def gate_terms():
        pre = g_ref[...] + bias_ref[...]
        i_pre = jnp.sum(jnp.where(lane == hd, pre, 0.0), axis=1, keepdims=True)
        f_pre = jnp.sum(jnp.where(lane == hd + HEADS, pre, 0.0), axis=1, keepdims=True)
        m_prev = jnp.sum(jnp.where(lane == hd, m0_ref[...], 0.0), axis=1, keepdims=True)
        inter = _log_sigmoid(f_pre) + m_prev
        m_t = jnp.maximum(inter, i_pre)
        return m_t, jnp.exp(i_pre - m_t), jnp.exp(inter - m_t)

    @pl.when(grp == 0)
    def _():
        _, w_in, w_dec = gate_terms()
        vt_ref[...] = v_ref[...].T
        kp_ref[...] = (w_in * k_ref[...]).astype(BF16)
        wd_ref[...] = jnp.broadcast_to(w_dec, wd_ref.shape)
        cqt_ref[...] = jnp.zeros_like(cqt_ref)

    lane_sq = lax.broadcasted_iota(jnp.int32, (HEAD_DIM, LANES), 1)
    tokens = c_ref.shape[0]

    def body(r, cqt):
        tok = grp * tokens + r
        q_row = q_ref[pl.ds(tok, 1), :]
        c_b = c_ref[r]
        sel = lane_sq == tok
        cq_col = jnp.sum(c_b * q_row, axis=1, keepdims=True)
        v_sel = jnp.where(sel, vt_ref[...], 0.0).astype(BF16)
        outer = _dot(v_sel, kp_ref[...])
        c_out[r] = wd_ref[pl.ds(tok, 1), :] * c_b + outer
        return jnp.where(sel, cq_col, cqt)

    cqt_ref[...] = lax.fori_loop(0, tokens, body, cqt_ref[...], unroll=SAMPLE_UNROLL)

    @pl.when(grp == pl.num_programs(1) - 1)
    def _():
        m_t, w_in, w_dec = gate_terms()
        q = q_ref[...]
        k = k_ref[...]
        v = v_ref[...]
        n0 = n0_ref[...]
        cq = cqt_ref[...].T
        s = jnp.sum(q * k, axis=1, keepdims=True) * (QK_SCALE * w_in)
        w_inter = w_dec * QK_SCALE
        num = s * v + w_inter * cq
        den = s + w_inter * jnp.sum(n0 * q, axis=1, keepdims=True)
        hh = num / jnp.maximum(jnp.abs(den), jnp.exp(-m_t))
        ml = _rms(hh) * gh_ref[...] * og_ref[...]
        cm = ug_ref[...] * (ws0_ref[...] * vsn_ref[...] + bs0_ref[...])
        n_out[...] = w_dec * n0 + w_in * k
        m_out[...] = jnp.broadcast_to(m_t, m_out.shape)
        for hh_static in range(HEADS):
            @pl.when(hd == hh_static)
            def _():
                mix_ref[:, hh_static * HEAD_DIM: (hh_static + 1) * HEAD_DIM] = ml
                mix_ref[:, GROUP_WIDTH + hh_static * HEAD_DIM: GROUP_WIDTH + (hh_static + 1) * HEAD_DIM] = cm


def _mix_sample(a, k, b, gates, m0_pad, n0, c0, bias_row, g_head, ws0_row, bs0_row):
    nb = a.shape[0]
    tb = SAMPLE_TOKENS_PER_STEP

    def head_block(offset):
        return pl.BlockSpec((nb, HEAD_DIM), lambda h, g: (0, offset + h))

    full = pl.BlockSpec((nb, LANES), lambda h, g: (0, 0))
    head_row = pl.BlockSpec((1, HEAD_DIM), lambda h, g: (0, h))
    c_spec = pl.BlockSpec((tb, None, HEAD_DIM, HEAD_DIM), lambda h, g: (g, h, 0, 0))
    return pl.pallas_call(
        _mix_sample_kernel,
        grid=(HEADS, nb // tb),
        in_specs=[
            head_block(0), head_block(0), head_block(HEADS), head_block(2 * HEADS),
            head_block(0), head_block(HEADS),
            full, full, head_block(0), c_spec,
            pl.BlockSpec((1, LANES), lambda h, g: (0, 0)),
            head_row, head_row, head_row,
        ],
        out_specs=[
            pl.BlockSpec((nb, D_MODEL), lambda h, g: (0, 0)),
            c_spec,
            head_block(0),
            head_block(0),
        ],
        out_shape=[
            jax.ShapeDtypeStruct((nb, D_MODEL), F32),
            jax.ShapeDtypeStruct(c0.shape, F32),
            jax.ShapeDtypeStruct((nb, GROUP_WIDTH), F32),
            jax.ShapeDtypeStruct((nb, GROUP_WIDTH), F32),
        ],
        scratch_shapes=[
            pltpu.VMEM((HEAD_DIM, nb), F32),
            pltpu.VMEM((nb, HEAD_DIM), BF16),
            pltpu.VMEM((nb, LANES), F32),
            pltpu.VMEM((HEAD_DIM, nb), F32),
        ],
        compiler_params=pltpu.CompilerParams(
            dimension_semantics=("arbitrary", "arbitrary"), vmem_limit_bytes=VMEM_LIMIT),
        name="mix_sample",
    )(a, k, a, a, b, b, gates, m0_pad, n0, c0, bias_row, g_head, ws0_row, bs0_row)


OUT_ROW_BLOCKS = 2


def _out_kernel(x_ref, mix_ref, gt1_ref, sh2_ref, sc2_ref, gt2_ref, wo_ref, g2_ref, wgu_ref, wdn_ref, gf_ref,
                y_ref):
    tm = x_ref.shape[0]
    rows_per_block = tm // OUT_ROW_BLOCKS if tm % (OUT_ROW_BLOCKS * 128) == 0 else tm
    per_row = gt1_ref.shape[0] != 1
    blocks = []
    for r0 in range(0, tm, rows_per_block):
        rs = slice(r0, r0 + rows_per_block)
        ms = rs if per_row else slice(0, 1)
        x1 = x_ref[rs, :] + gt1_ref[ms, :] * _dot(mix_ref[rs, :].astype(BF16), wo_ref[...])
        h2 = _rms(x1) * g2_ref[...]
        h2 = (h2 * (1.0 + sc2_ref[ms, :]) + sh2_ref[ms, :]).astype(BF16)
        blocks.append((rs, ms, x1, h2))
    for rs, ms, x1, h2 in blocks:
        gate = _dot(h2, wgu_ref[:, :D_FF])
        up = _dot(h2, wgu_ref[:, D_FF:])
        act = (gate * _sigmoid(gate) * up).astype(BF16)
        x2 = x1 + gt2_ref[ms, :] * _dot(act, wdn_ref[...])
        y_ref[rs, :] = _rms(x2) * gf_ref[...]


def _output_stage(x, mix, mod, w_out, g2, w_gu, w_dn, g_final, *, tm, per_row):
    groups, t, _ = x.shape
    return pl.pallas_call(
        _out_kernel,
        grid=(groups, t // tm),
        in_specs=[
            pl.BlockSpec((None, tm, D_MODEL), lambda g, i: (g, i, 0)),
            pl.BlockSpec((None, tm, D_MODEL), lambda g, i: (g, i, 0)),
            _mod_spec(2, tm, per_row),
            _mod_spec(3, tm, per_row),
            _mod_spec(4, tm, per_row),
            _mod_spec(5, tm, per_row),
            _const_spec((D_MODEL, D_MODEL)),
            _const_spec((1, D_MODEL)),
            _const_spec((D_MODEL, 2 * D_FF)),
            _const_spec((D_FF, D_MODEL)),
            _const_spec((1, D_MODEL)),
        ],
        out_specs=pl.BlockSpec((None, tm, D_MODEL), lambda g, i: (g, i, 0)),
        out_shape=jax.ShapeDtypeStruct((groups, t, D_MODEL), F32),
        compiler_params=pltpu.CompilerParams(
            dimension_semantics=("arbitrary", "arbitrary"), vmem_limit_bytes=VMEM_LIMIT),
        name="output_stage",
    )(x, mix, mod, mod, mod, mod, w_out, g2, w_gu, w_dn, g_final)


def kernel(x_prompt, x_sample, c_prompt, c_sample, state_mlstm_C, state_mlstm_n, state_mlstm_m, w_ada, b_ada, g_norm1, w_in, b_gate, g_mlstm_head, ln_v_g, ln_v_b, w_s, b_s, w_out, g_norm2, w_gate_up, w_down, g_final):
    depth = w_ada.shape[0]
    assert depth == 1, "single-layer trunk"
    batch, seq, _ = x_prompt.shape
    nb = x_sample.shape[0]
    assert x_sample.shape[1] == 1

    wi = w_in[0]
    gw = GROUP_WIDTH
    o_q, o_k, o_v, o_o, o_i = 0, gw, 2 * gw, 3 * gw, 4 * gw
    o_u = o_i + 2 * HEADS
    o_vs = o_u + gw
    w_proj = jnp.concatenate(
        [wi[:, o_q:o_k], wi[:, o_v:o_o], wi[:, o_vs:o_vs + gw], wi[:, o_o:o_i], wi[:, o_u:o_vs],
         jnp.pad(wi[:, o_i:o_u], ((0, 0), (0, LANES - 2 * HEADS)))], axis=1).astype(BF16)
    w_k = wi[:, o_k:o_v].astype(BF16)
    w_kt = w_k.T
    w_out_b = w_out[0].astype(BF16)
    w_gu_b = w_gate_up[0].astype(BF16)
    w_dn_b = w_down[0].astype(BF16)
    g1 = g_norm1[0].reshape(1, D_MODEL)
    g2 = g_norm2[0].reshape(1, D_MODEL)
    gf = g_final.reshape(1, D_MODEL)
    ln_g = ln_v_g[0].reshape(1, gw)
    ln_b = ln_v_b[0].reshape(1, gw)
    g_head = g_mlstm_head[0].reshape(1, gw)
    bias_row = jnp.pad(b_gate[0], (0, LANES - 2 * HEADS)).reshape(1, LANES)
    tril = jnp.tril(jnp.ones((CHUNK, CHUNK), dtype=bool))
    ws_tril = jnp.where(tril[None], w_s[0], 0.0).astype(BF16)
    bs_rep = jnp.broadcast_to(b_s[0][:, :, None], (HEADS, CHUNK, LANES))
    ws0_row = jnp.repeat(w_s[0][:, 0, 0], HEAD_DIM).reshape(1, gw)
    bs0_row = jnp.repeat(b_s[0][:, 0], HEAD_DIM).reshape(1, gw)

    rows = batch + nb
    rows_pad = -(-rows // 16) * 16
    c_all = jnp.concatenate([c_prompt, c_sample, jnp.zeros((rows_pad - rows, D_MODEL), F32)], axis=0)
    mod = _modulation(c_all, w_ada[0], b_ada[0].reshape(1, -1))
    mod_p = mod[:batch].reshape(batch, 1, N_MOD * D_MODEL)
    mod_s = mod[batch:rows].reshape(1, nb, N_MOD * D_MODEL)

    a_p, b_p, g_p, kt_p = _project(x_prompt, mod_p, g1, w_proj, ln_g, ln_b, w_kt, k_transposed=True,
                                   tm=MIX_TILE, per_row=False, a_dtype=BF16)
    mix_p, c_p, n_p, m_p = _mix_prompt(a_p, kt_p, b_p, g_p, bias_row, g_head, ws_tril, bs_rep)
    y_p = _output_stage(x_prompt, mix_p, mod_p, w_out_b, g2, w_gu_b, w_dn_b, gf, tm=512, per_row=False)

    xs = x_sample.reshape(1, nb, D_MODEL)
    a_s, b_s_act, g_s, k_s = _project(xs, mod_s, g1, w_proj, ln_g, ln_b, w_k, k_transposed=False,
                                      tm=nb, per_row=True, a_dtype=F32)
    a_s2, b_s2, g_s2 = a_s[0], b_s_act[0], g_s[0]
    m0_pad = jnp.pad(state_mlstm_m[0], ((0, 0), (0, LANES - HEADS)))
    n0 = state_mlstm_n[0].reshape(nb, gw)
    mix_s, c_s, n_s, m_s = _mix_sample(a_s2, k_s[0], b_s2, g_s2, m0_pad, n0, state_mlstm_C[0],
                                       bias_row, g_head, ws0_row, bs0_row)
    y_s = _output_stage(xs, mix_s.reshape(1, nb, D_MODEL), mod_s, w_out_b, g2, w_gu_b, w_dn_b, gf,
                        tm=nb, per_row=True)

    return (
        y_p,
        y_s.reshape(nb, 1, D_MODEL),
        c_p[None],
        n_p[None],
        m_p[:, :, 0][None],
        c_s[None],
        n_s.reshape(nb, HEADS, HEAD_DIM)[None],
        m_s.reshape(nb, HEADS, HEAD_DIM)[:, :, 0][None],
        a_s2[:, 2 * gw:].reshape(nb, 1, HEADS, HEAD_DIM)[None],
    )
```

```python
import functools
import math

import jax
import jax.numpy as jnp
from jax import lax
from jax.experimental import pallas as pl
from jax.experimental.pallas import tpu as pltpu

F32 = jnp.float32
BF16 = jnp.bfloat16

D_MODEL = 1024
HEADS = 4
HEAD_DIM = 128
GROUP_WIDTH = HEADS * HEAD_DIM
CHUNK = 128
D_FF = 2816
N_MOD = 6
EPS = 1e-6
QK_SCALE = HEAD_DIM ** -0.5
LANES = 128

N_B = 2 * GROUP_WIDTH

VMEM_LIMIT = 56 * 1024 * 1024


def _dot(a, b):
    return jnp.dot(a, b, preferred_element_type=F32)


def _dot_nt(a, b):
    return lax.dot_general(a, b, (((1,), (1,)), ((), ())), preferred_element_type=F32)


def _sigmoid(x):
    return 1.0 / (1.0 + jnp.exp(-x))


def _gelu_tanh(x):
    c = math.sqrt(2.0 / math.pi)
    return x * (0.5 * (1.0 + jnp.tanh(c * (x + 0.044715 * (x * x * x)))))


def _log_sigmoid(x):
    return jnp.minimum(x, 0.0) - jnp.log1p(jnp.exp(-jnp.abs(x)))


def _rms(x):
    return x * lax.rsqrt(jnp.mean(x * x, axis=-1, keepdims=True) + EPS)


def _split3_bf16(x):
    hi = x.astype(BF16)
    r1 = x - hi.astype(F32)
    mid = r1.astype(BF16)
    lo = (r1 - mid.astype(F32)).astype(BF16)
    return hi, mid, lo


def _mod_kernel(c_ref, w_ref, b_ref, o_ref):
    c = c_ref[...]
    a = (c * _sigmoid(c)).astype(BF16)
    o_ref[...] = _dot(a, w_ref[...].astype(BF16)) + b_ref[...]


def _modulation(c_all, w_ada, b_ada):
    rows = c_all.shape[0]
    tn = 1024
    return pl.pallas_call(
        _mod_kernel,
        grid=(N_MOD * D_MODEL // tn,),
        in_specs=[
            pl.BlockSpec((rows, D_MODEL), lambda j: (0, 0)),
            pl.BlockSpec((D_MODEL, tn), lambda j: (0, j)),
            pl.BlockSpec((1, tn), lambda j: (0, j)),
        ],
        out_specs=pl.BlockSpec((rows, tn), lambda j: (0, j)),
        out_shape=jax.ShapeDtypeStruct((rows, N_MOD * D_MODEL), F32),
        compiler_params=pltpu.CompilerParams(dimension_semantics=("arbitrary",)),
        name="modulation",
    )(c_all, w_ada, b_ada)


def _mod_spec(piece, tm, per_row):
    if per_row:
        return pl.BlockSpec((None, tm, D_MODEL), lambda g, t: (g, t, piece))
    return pl.BlockSpec((None, 1, D_MODEL), lambda g, t: (g, 0, piece))


def _const_spec(shape):
    nd = len(shape)
    return pl.BlockSpec(shape, lambda g, t: (0,) * nd, pipeline_mode=pl.Buffered(1))


def _proj_body(x_ref, sh_ref, sc_ref, g1_ref, w_ref, lng_ref, lnb_ref, a_ref, b_ref, g_ref):
    n_a = a_ref.shape[1]
    n_copy = n_a - GROUP_WIDTH
    x = x_ref[...]
    h = _rms(x) * g1_ref[...]
    h = (h * (1.0 + sc_ref[...]) + sh_ref[...]).astype(BF16)
    p = _dot(h, w_ref[...])
    a_ref[:, :n_copy] = p[:, :n_copy].astype(a_ref.dtype)
    vs = _gelu_tanh(p[:, n_copy:n_a])
    for hd in range(HEADS):
        sl = slice(hd * HEAD_DIM, (hd + 1) * HEAD_DIM)
        v = vs[:, sl]
        mu = jnp.mean(v, axis=-1, keepdims=True)
        vc = v - mu
        var = jnp.mean(vc * vc, axis=-1, keepdims=True)
        y = vc * lax.rsqrt(var + EPS) * lng_ref[:, sl] + lnb_ref[:, sl]
        a_ref[:, n_copy + hd * HEAD_DIM: n_copy + (hd + 1) * HEAD_DIM] = y.astype(a_ref.dtype)
    b_ref[:, :GROUP_WIDTH] = _sigmoid(p[:, n_a: n_a + GROUP_WIDTH])
    b_ref[:, GROUP_WIDTH:] = _gelu_tanh(p[:, n_a + GROUP_WIDTH: n_a + N_B])
    g_ref[...] = p[:, n_a + N_B:]
    return h


def _proj_k_kernel(x_ref, sh_ref, sc_ref, g1_ref, w_ref, lng_ref, lnb_ref, wk_ref, a_ref, b_ref, g_ref, k_ref):
    h = _proj_body(x_ref, sh_ref, sc_ref, g1_ref, w_ref, lng_ref, lnb_ref, a_ref, b_ref, g_ref)
    k_ref[...] = _dot(h, wk_ref[...]).astype(k_ref.dtype)


def _proj_kt_kernel(x_ref, sh_ref, sc_ref, g1_ref, w_ref, lng_ref, lnb_ref, wkt_ref, a_ref, b_ref, g_ref, kt_ref):
    h = _proj_body(x_ref, sh_ref, sc_ref, g1_ref, w_ref, lng_ref, lnb_ref, a_ref, b_ref, g_ref)
    kt_ref[...] = _dot_nt(wkt_ref[...], h).astype(kt_ref.dtype)


def _project(x, mod, g1, w_proj, ln_g, ln_b, w_k, *, k_transposed, tm, per_row, a_dtype):
    groups, t, _ = x.shape
    n_proj = w_proj.shape[1]
    n_a = n_proj - N_B - LANES
    in_specs = [
        pl.BlockSpec((None, tm, D_MODEL), lambda g, i: (g, i, 0)),
        _mod_spec(0, tm, per_row),
        _mod_spec(1, tm, per_row),
        _const_spec((1, D_MODEL)),
        _const_spec((D_MODEL, n_proj)),
        _const_spec((1, GROUP_WIDTH)),
        _const_spec((1, GROUP_WIDTH)),
    ]
    out_specs = [
        pl.BlockSpec((None, tm, n_a), lambda g, i: (g, i, 0)),
        pl.BlockSpec((None, tm, N_B), lambda g, i: (g, i, 0)),
        pl.BlockSpec((None, tm, LANES), lambda g, i: (g, i, 0)),
    ]
    out_shape = [
        jax.ShapeDtypeStruct((groups, t, n_a), a_dtype),
        jax.ShapeDtypeStruct((groups, t, N_B), F32),
        jax.ShapeDtypeStruct((groups, t, LANES), F32),
    ]
    args = [x, mod, mod, g1, w_proj, ln_g, ln_b, w_k]
    in_specs.append(_const_spec(w_k.shape))
    if k_transposed:
        out_specs.append(pl.BlockSpec((None, GROUP_WIDTH, tm), lambda g, i: (g, 0, i)))
        out_shape.append(jax.ShapeDtypeStruct((groups, GROUP_WIDTH, t), a_dtype))
    else:
        out_specs.append(pl.BlockSpec((None, tm, GROUP_WIDTH), lambda g, i: (g, i, 0)))
        out_shape.append(jax.ShapeDtypeStruct((groups, t, GROUP_WIDTH), a_dtype))
    return pl.pallas_call(
        _proj_kt_kernel if k_transposed else _proj_k_kernel,
        grid=(groups, t // tm),
        in_specs=in_specs,
        out_specs=out_specs,
        out_shape=out_shape,
        compiler_params=pltpu.CompilerParams(
            dimension_semantics=("arbitrary", "arbitrary"), vmem_limit_bytes=VMEM_LIMIT),
        name="project",
    )(*args)


MIX_TILE = 512


def _dot3_rhs(lhs_bf16, rhs_f32):
    hi, mid, lo = _split3_bf16(rhs_f32)
    return _dot(lhs_bf16, hi) + _dot(lhs_bf16, mid) + _dot(lhs_bf16, lo)


def _dot3_lhs(lhs_f32, rhs_bf16):
    hi, mid, lo = _split3_bf16(lhs_f32)
    return _dot(hi, rhs_bf16) + _dot(mid, rhs_bf16) + _dot(lo, rhs_bf16)


FF_CHUNKS = (768, 768, 768, 512)


def _gate_rows(g_ref, bias_ref, tril, triu, n_chunks):
    bc, arow, blast = [], [], []
    for c in range(n_chunks):
        pre_c = g_ref[c * CHUNK:(c + 1) * CHUNK, :] + bias_ref[...]
        bc.append(_dot3_rhs(tril, _log_sigmoid(pre_c)))
        rows = pre_c.T[0:2 * HEADS, :]
        b_rows = _dot3_lhs(_log_sigmoid(rows), triu)
        arow.append([rows[hd:hd + 1, :] - b_rows[HEADS + hd:HEADS + hd + 1, :] for hd in range(HEADS)])
        blast.append([jnp.min(b_rows[HEADS + hd:HEADS + hd + 1, :], axis=1, keepdims=True) for hd in range(HEADS)])
    return bc, arow, blast


def _mlstm_chunk(q, kt, v, og, g_head, a_row, b_last, b_rep, cta, m_prev, causal, ones_blk):
    va = jnp.concatenate([v, ones_blk], axis=1)
    amat = jnp.where(causal, a_row, -jnp.inf)
    m_row = jnp.max(amat, axis=1, keepdims=True)
    s_loc = _dot(q, kt) * jnp.exp(amat - m_row)
    nd_loc = _dot(s_loc.astype(BF16), va)
    g_row = b_last + a_row
    g_loc = jnp.max(g_row, axis=1, keepdims=True)
    kw = (kt.astype(F32) * jnp.exp(g_row - g_loc)).astype(BF16)
    u_aug = _dot(kw, va)

    inter = _dot(q, cta.astype(BF16))
    mm = jnp.maximum(m_prev, m_row)
    f_loc = jnp.exp(m_row - mm) * QK_SCALE
    f_int = jnp.exp(m_prev - mm) * QK_SCALE
    nd = f_loc * nd_loc + f_int * inter
    clamp = jnp.exp(-(b_rep + mm))
    hh = nd[:, :HEAD_DIM] / jnp.maximum(jnp.abs(nd[:, HEAD_DIM:]), clamp)
    ml = _rms(hh) * g_head * og

    dec = b_last + m_prev
    m_new = jnp.maximum(dec, g_loc)
    cta_new = jnp.exp(dec - m_new) * cta + jnp.exp(g_loc - m_new) * u_aug
    return ml, cta_new, m_new


def _prompt_tail_kernel(a_ref, kt_ref, b_ref, g_ref, x_ref, gt1_ref, sh2_ref, sc2_ref, gt2_ref,
                        bias_ref, gh_ref, ws_ref, bs_ref, wo_ref, g2_ref, wgu_ref, wdn_ref, gf_ref,
                        y_ref, c_out, n_out, m_out,
                        mix_ref, cta_ref, m_ref, *, tiles_per_seq):
    step = pl.program_id(0)
    n_tiles = pl.num_programs(0) - 1
    tm = a_ref.shape[0]
    n_chunks = tm // CHUNK
    n_copy = a_ref.shape[1] - GROUP_WIDTH

    @pl.when(step == 0)
    def _():
        mix_ref[...] = jnp.zeros_like(mix_ref)

    @pl.when(step % tiles_per_seq == 0)
    def _():
        cta_ref[...] = jnp.zeros_like(cta_ref)
        m_ref[...] = jnp.zeros_like(m_ref)

    row = lax.broadcasted_iota(jnp.int32, (CHUNK, CHUNK), 0)
    col = lax.broadcasted_iota(jnp.int32, (CHUNK, CHUNK), 1)
    causal = row >= col
    tril = jnp.where(causal, 1.0, 0.0).astype(BF16)
    triu = jnp.where(row <= col, 1.0, 0.0).astype(BF16)
    ones_blk = jnp.ones((CHUNK, HEAD_DIM), BF16)

    x1 = x_ref[...] + gt1_ref[...] * _dot(mix_ref[...], wo_ref[...])
    h2 = _rms(x1) * g2_ref[...]
    h2 = (h2 * (1.0 + sc2_ref[...]) + sh2_ref[...]).astype(BF16)

    bc, arow, blast = _gate_rows(g_ref, bias_ref, tril, triu, n_chunks)
    state = [(cta_ref[hd], jnp.max(m_ref[hd][0:1, :], axis=1, keepdims=True)) for hd in range(HEADS)]
    sg_all = [None] * HEADS

    def mixer_piece(c, hd):
        rs = slice(c * CHUNK, (c + 1) * CHUNK)
        sl = slice(hd * HEAD_DIM, (hd + 1) * HEAD_DIM)
        if c == 0:
            vsn_all = jnp.concatenate(
                [a_ref[cc * CHUNK:(cc + 1) * CHUNK, n_copy + hd * HEAD_DIM: n_copy + (hd + 1) * HEAD_DIM]
                 for cc in range(n_chunks)], axis=1)
            sg_all[hd] = _dot(ws_ref[hd], vsn_all)
        cta, m_prev = state[hd]
        b_rep = jnp.broadcast_to(bc[c][:, HEADS + hd: HEADS + hd + 1], (CHUNK, HEAD_DIM))
        ml, cta, m_prev = _mlstm_chunk(
            a_ref[rs, sl], kt_ref[sl, rs], a_ref[rs, GROUP_WIDTH + hd * HEAD_DIM: GROUP_WIDTH + (hd + 1) * HEAD_DIM],
            b_ref[rs, sl], gh_ref[:, sl], arow[c][hd], blast[c][hd], b_rep, cta, m_prev, causal, ones_blk)
        state[hd] = (cta, m_prev)
        mix_ref[rs, sl] = ml.astype(mix_ref.dtype)
        sg = sg_all[hd][:, c * CHUNK:(c + 1) * CHUNK] + bs_ref[hd]
        ug = b_ref[rs, GROUP_WIDTH + hd * HEAD_DIM: GROUP_WIDTH + (hd + 1) * HEAD_DIM]
        mix_ref[rs, GROUP_WIDTH + hd * HEAD_DIM: GROUP_WIDTH + (hd + 1) * HEAD_DIM] = (ug * sg).astype(mix_ref.dtype)

    pieces = [(c, hd) for c in range(n_chunks) for hd in range(HEADS)]
    per_ff = -(-len(pieces) // len(FF_CHUNKS))
    acc = None
    f0 = 0
    for j, width in enumerate(FF_CHUNKS):
        gate = _dot(h2, wgu_ref[:, f0:f0 + width])
        up = _dot(h2, wgu_ref[:, D_FF + f0:D_FF + f0 + width])
        for c, hd in pieces[j * per_ff:(j + 1) * per_ff]:
            mixer_piece(c, hd)
        act = (gate * _sigmoid(gate) * up).astype(BF16)
        part = _dot(act, wdn_ref[f0:f0 + width, :])
        acc = part if acc is None else acc + part
        f0 += width
    x2 = x1 + gt2_ref[...] * acc
    y_ref[...] = _rms(x2) * gf_ref[...]

    for hd in range(HEADS):
        cta_ref[hd] = state[hd][0]
        m_ref[hd] = jnp.broadcast_to(state[hd][1], m_ref.shape[1:])

    @pl.when(jnp.logical_and(step % tiles_per_seq == tiles_per_seq - 1, step < n_tiles))
    def _():
        for hd in range(HEADS):
            cta = cta_ref[hd]
            c_out[hd] = cta[:, :HEAD_DIM].T
            n_out[hd: hd + 1, :] = cta[:, HEAD_DIM:].T[0:1, :]
            m_out[hd: hd + 1, :] = m_ref[hd][0:1, :]


def _prompt_tail(a, kt, b, gates, x, mod, bias_row, g_head, ws_tril, bs_rep, w_out, g2, w_gu, w_dn, g_final):
    groups, t, n_a = a.shape
    tm = MIX_TILE
    tps = t // tm
    n_tiles = groups * tps
    assert sum(FF_CHUNKS) == D_FF

    def cur(i):
        return jnp.minimum(i, n_tiles - 1)

    def prev(i):
        return jnp.maximum(i - 1, 0)

    def rows(tile, width):
        return pl.BlockSpec((None, tm, width), lambda i: (tile(i) // tps, tile(i) % tps, 0))

    def mod_piece(piece):
        return pl.BlockSpec((None, 1, D_MODEL), lambda i: (prev(i) // tps, 0, piece))

    def const(shape):
        nd = len(shape)
        return pl.BlockSpec(shape, lambda i: (0,) * nd, pipeline_mode=pl.Buffered(1))

    def per_seq(shape):
        nd = len(shape)
        return pl.BlockSpec((None,) + shape, lambda i: (cur(i) // tps,) + (0,) * nd)

    return pl.pallas_call(
        functools.partial(_prompt_tail_kernel, tiles_per_seq=tps),
        grid=(n_tiles + 1,),
        in_specs=[
            rows(cur, n_a),
            pl.BlockSpec((None, GROUP_WIDTH, tm), lambda i: (cur(i) // tps, 0, cur(i) % tps)),
            rows(cur, N_B),
            rows(cur, LANES),
            rows(prev, D_MODEL),
            mod_piece(2), mod_piece(3), mod_piece(4), mod_piece(5),
            const((1, LANES)),
            const((1, GROUP_WIDTH)),
            const((HEADS, CHUNK, CHUNK)),
            const((HEADS, CHUNK, LANES)),
            const((D_MODEL, D_MODEL)),
            const((1, D_MODEL)),
            const((D_MODEL, 2 * D_FF)),
            const((D_FF, D_MODEL)),
            const((1, D_MODEL)),
        ],
        out_specs=[
            rows(prev, D_MODEL),
            per_seq((HEADS, HEAD_DIM, HEAD_DIM)),
            per_seq((HEADS, HEAD_DIM)),
            per_seq((HEADS, LANES)),
        ],
        out_shape=[
            jax.ShapeDtypeStruct((groups, t, D_MODEL), F32),
            jax.ShapeDtypeStruct((groups, HEADS, HEAD_DIM, HEAD_DIM), F32),
            jax.ShapeDtypeStruct((groups, HEADS, HEAD_DIM), F32),
            jax.ShapeDtypeStruct((groups, HEADS, LANES), F32),
        ],
        scratch_shapes=[
            pltpu.VMEM((tm, D_MODEL), BF16),
            pltpu.VMEM((HEADS, HEAD_DIM, 2 * HEAD_DIM), F32),
            pltpu.VMEM((HEADS, 8, LANES), F32),
        ],
        compiler_params=pltpu.CompilerParams(
            dimension_semantics=("arbitrary",), vmem_limit_bytes=VMEM_LIMIT),
        name="prompt_tail",
    )(a, kt, b, gates, x, mod, mod, mod, mod, bias_row, g_head, ws_tril, bs_rep, w_out, g2, w_gu, w_dn, g_final)


SAMPLE_TOKENS_PER_STEP = 32
SAMPLE_UNROLL = 8


def _mix_sample_kernel(q_ref, k_ref, v_ref, vsn_ref, og_ref, ug_ref, g_ref, m0_ref, n0_ref, c_ref,
                       bias_ref, gh_ref, ws0_ref, bs0_ref,
                       mix_ref, c_out, n_out, m_out,
                       vt_ref, kp_ref, wd_ref, cqt_ref):
    hd = pl.program_id(0)
    grp = pl.program_id(1)
    nb = q_ref.shape[0]
    lane = lax.broadcasted_iota(jnp.int32, (nb, LANES), 1)

    def gate_terms():
        pre = g_ref[...] + bias_ref[...]
        i_pre = jnp.sum(jnp.where(lane == hd, pre, 0.0), axis=1, keepdims=True)
        f_pre = jnp.sum(jnp.where(lane == hd + HEADS, pre, 0.0), axis=1, keepdims=True)
        m_prev = jnp.sum(jnp.where(lane == hd, m0_ref[...], 0.0), axis=1, keepdims=True)
        inter = _log_sigmoid(f_pre) + m_prev
        m_t = jnp.maximum(inter, i_pre)
        return m_t, jnp.exp(i_pre - m_t), jnp.exp(inter - m_t)

    @pl.when(grp == 0)
    def _():
        _, w_in, w_dec = gate_terms()
        vt_ref[...] = v_ref[...].T
        kp_ref[...] = (w_in * k_ref[...]).astype(BF16)
        wd_ref[...] = jnp.broadcast_to(w_dec, wd_ref.shape)
        cqt_ref[...] = jnp.zeros_like(cqt_ref)

    lane_sq = lax.broadcasted_iota(jnp.int32, (HEAD_DIM, LANES), 1)
    tokens = c_ref.shape[0]

    def body(r, cqt):
        tok = grp * tokens + r
        q_row = q_ref[pl.ds(tok, 1), :]
        c_b = c_ref[r]
        sel = lane_sq == tok
        cq_col = jnp.sum(c_b * q_row, axis=1, keepdims=True)
        v_sel = jnp.where(sel, vt_ref[...], 0.0).astype(BF16)
        outer = _dot(v_sel, kp_ref[...])
        c_out[r] = wd_ref[pl.ds(tok, 1), :] * c_b + outer
        return jnp.where(sel, cq_col, cqt)

    cqt_ref[...] = lax.fori_loop(0, tokens, body, cqt_ref[...], unroll=SAMPLE_UNROLL)

    @pl.when(grp == pl.num_programs(1) - 1)
    def _():
        m_t, w_in, w_dec = gate_terms()
        q = q_ref[...]
        k = k_ref[...]
        v = v_ref[...]
        n0 = n0_ref[...]
        cq = cqt_ref[...].T
        s = jnp.sum(q * k, axis=1, keepdims=True) * (QK_SCALE * w_in)
        w_inter = w_dec * QK_SCALE
        num = s * v + w_inter * cq
        den = s + w_inter * jnp.sum(n0 * q, axis=1, keepdims=True)
        hh = num / jnp.maximum(jnp.abs(den), jnp.exp(-m_t))
        ml = _rms(hh) * gh_ref[...] * og_ref[...]
        cm = ug_ref[...] * (ws0_ref[...] * vsn_ref[...] + bs0_ref[...])
        n_out[...] = w_dec * n0 + w_in * k
        m_out[...] = jnp.broadcast_to(m_t, m_out.shape)
        for hh_static in range(HEADS):
            @pl.when(hd == hh_static)
            def _():
                mix_ref[:, hh_static * HEAD_DIM: (hh_static + 1) * HEAD_DIM] = ml
                mix_ref[:, GROUP_WIDTH + hh_static * HEAD_DIM: GROUP_WIDTH + (hh_static + 1) * HEAD_DIM] = cm


def _mix_sample(a, k, b, gates, m0_pad, n0, c0, bias_row, g_head, ws0_row, bs0_row):
    nb = a.shape[0]
    tb = SAMPLE_TOKENS_PER_STEP

    def head_block(offset):
        return pl.BlockSpec((nb, HEAD_DIM), lambda h, g: (0, offset + h))

    full = pl.BlockSpec((nb, LANES), lambda h, g: (0, 0))
    head_row = pl.BlockSpec((1, HEAD_DIM), lambda h, g: (0, h))
    c_spec = pl.BlockSpec((tb, None, HEAD_DIM, HEAD_DIM), lambda h, g: (g, h, 0, 0))
    return pl.pallas_call(
        _mix_sample_kernel,
        grid=(HEADS, nb // tb),
        in_specs=[
            head_block(0), head_block(0), head_block(HEADS), head_block(2 * HEADS),
            head_block(0), head_block(HEADS),
            full, full, head_block(0), c_spec,
            pl.BlockSpec((1, LANES), lambda h, g: (0, 0)),
            head_row, head_row, head_row,
        ],
        out_specs=[
            pl.BlockSpec((nb, D_MODEL), lambda h, g: (0, 0)),
            c_spec,
            head_block(0),
            head_block(0),
        ],
        out_shape=[
            jax.ShapeDtypeStruct((nb, D_MODEL), F32),
            jax.ShapeDtypeStruct(c0.shape, F32),
            jax.ShapeDtypeStruct((nb, GROUP_WIDTH), F32),
            jax.ShapeDtypeStruct((nb, GROUP_WIDTH), F32),
        ],
        scratch_shapes=[
            pltpu.VMEM((HEAD_DIM, nb), F32),
            pltpu.VMEM((nb, HEAD_DIM), BF16),
            pltpu.VMEM((nb, LANES), F32),
            pltpu.VMEM((HEAD_DIM, nb), F32),
        ],
        compiler_params=pltpu.CompilerParams(
            dimension_semantics=("arbitrary", "arbitrary"), vmem_limit_bytes=VMEM_LIMIT),
        name="mix_sample",
    )(a, k, a, a, b, b, gates, m0_pad, n0, c0, bias_row, g_head, ws0_row, bs0_row)


OUT_ROW_BLOCKS = 2


def _out_kernel(x_ref, mix_ref, gt1_ref, sh2_ref, sc2_ref, gt2_ref, wo_ref, g2_ref, wgu_ref, wdn_ref, gf_ref,
                y_ref):
    tm = x_ref.shape[0]
    rows_per_block = tm // OUT_ROW_BLOCKS if tm % (OUT_ROW_BLOCKS * 128) == 0 else tm
    per_row = gt1_ref.shape[0] != 1
    blocks = []
    for r0 in range(0, tm, rows_per_block):
        rs = slice(r0, r0 + rows_per_block)
        ms = rs if per_row else slice(0, 1)
        x1 = x_ref[rs, :] + gt1_ref[ms, :] * _dot(mix_ref[rs, :].astype(BF16), wo_ref[...])
        h2 = _rms(x1) * g2_ref[...]
        h2 = (h2 * (1.0 + sc2_ref[ms, :]) + sh2_ref[ms, :]).astype(BF16)
        blocks.append((rs, ms, x1, h2))
    for rs, ms, x1, h2 in blocks:
        gate = _dot(h2, wgu_ref[:, :D_FF])
        up = _dot(h2, wgu_ref[:, D_FF:])
        act = (gate * _sigmoid(gate) * up).astype(BF16)
        x2 = x1 + gt2_ref[ms, :] * _dot(act, wdn_ref[...])
        y_ref[rs, :] = _rms(x2) * gf_ref[...]


def _output_stage(x, mix, mod, w_out, g2, w_gu, w_dn, g_final, *, tm, per_row):
    groups, t, _ = x.shape
    return pl.pallas_call(
        _out_kernel,
        grid=(groups, t // tm),
        in_specs=[
            pl.BlockSpec((None, tm, D_MODEL), lambda g, i: (g, i, 0)),
            pl.BlockSpec((None, tm, D_MODEL), lambda g, i: (g, i, 0)),
            _mod_spec(2, tm, per_row),
            _mod_spec(3, tm, per_row),
            _mod_spec(4, tm, per_row),
            _mod_spec(5, tm, per_row),
            _const_spec((D_MODEL, D_MODEL)),
            _const_spec((1, D_MODEL)),
            _const_spec((D_MODEL, 2 * D_FF)),
            _const_spec((D_FF, D_MODEL)),
            _const_spec((1, D_MODEL)),
        ],
        out_specs=pl.BlockSpec((None, tm, D_MODEL), lambda g, i: (g, i, 0)),
        out_shape=jax.ShapeDtypeStruct((groups, t, D_MODEL), F32),
        compiler_params=pltpu.CompilerParams(
            dimension_semantics=("arbitrary", "arbitrary"), vmem_limit_bytes=VMEM_LIMIT),
        name="output_stage",
    )(x, mix, mod, mod, mod, mod, w_out, g2, w_gu, w_dn, g_final)


def kernel(x_prompt, x_sample, c_prompt, c_sample, state_mlstm_C, state_mlstm_n, state_mlstm_m, w_ada, b_ada, g_norm1, w_in, b_gate, g_mlstm_head, ln_v_g, ln_v_b, w_s, b_s, w_out, g_norm2, w_gate_up, w_down, g_final):
    depth = w_ada.shape[0]
    assert depth == 1, "single-layer trunk"
    batch, seq, _ = x_prompt.shape
    nb = x_sample.shape[0]
    assert x_sample.shape[1] == 1

    wi = w_in[0]
    gw = GROUP_WIDTH
    o_q, o_k, o_v, o_o, o_i = 0, gw, 2 * gw, 3 * gw, 4 * gw
    o_u = o_i + 2 * HEADS
    o_vs = o_u + gw
    w_proj = jnp.concatenate(
        [wi[:, o_q:o_k], wi[:, o_v:o_o], wi[:, o_vs:o_vs + gw], wi[:, o_o:o_i], wi[:, o_u:o_vs],
         jnp.pad(wi[:, o_i:o_u], ((0, 0), (0, LANES - 2 * HEADS)))], axis=1).astype(BF16)
    w_k = wi[:, o_k:o_v].astype(BF16)
    w_kt = w_k.T
    w_out_b = w_out[0].astype(BF16)
    w_gu_b = w_gate_up[0].astype(BF16)
    w_dn_b = w_down[0].astype(BF16)
    g1 = g_norm1[0].reshape(1, D_MODEL)
    g2 = g_norm2[0].reshape(1, D_MODEL)
    gf = g_final.reshape(1, D_MODEL)
    ln_g = ln_v_g[0].reshape(1, gw)
    ln_b = ln_v_b[0].reshape(1, gw)
    g_head = g_mlstm_head[0].reshape(1, gw)
    bias_row = jnp.pad(b_gate[0], (0, LANES - 2 * HEADS)).reshape(1, LANES)
    tril = jnp.tril(jnp.ones((CHUNK, CHUNK), dtype=bool))
    ws_tril = jnp.where(tril[None], w_s[0], 0.0).astype(BF16)
    bs_rep = jnp.broadcast_to(b_s[0][:, :, None], (HEADS, CHUNK, LANES))
    ws0_row = jnp.repeat(w_s[0][:, 0, 0], HEAD_DIM).reshape(1, gw)
    bs0_row = jnp.repeat(b_s[0][:, 0], HEAD_DIM).reshape(1, gw)

    rows = batch + nb
    rows_pad = -(-rows // 16) * 16
    c_all = jnp.concatenate([c_prompt, c_sample, jnp.zeros((rows_pad - rows, D_MODEL), F32)], axis=0)
    mod = _modulation(c_all, w_ada[0], b_ada[0].reshape(1, -1))
    mod_p = mod[:batch].reshape(batch, 1, N_MOD * D_MODEL)
    mod_s = mod[batch:rows].reshape(1, nb, N_MOD * D_MODEL)

    a_p, b_p, g_p, kt_p = _project(x_prompt, mod_p, g1, w_proj, ln_g, ln_b, w_kt, k_transposed=True,
                                   tm=MIX_TILE, per_row=False, a_dtype=BF16)
    y_p, c_p, n_p, m_p = _prompt_tail(a_p, kt_p, b_p, g_p, x_prompt, mod_p, bias_row, g_head, ws_tril, bs_rep,
                                      w_out_b, g2, w_gu_b, w_dn_b, gf)

    xs = x_sample.reshape(1, nb, D_MODEL)
    a_s, b_s_act, g_s, k_s = _project(xs, mod_s, g1, w_proj, ln_g, ln_b, w_k, k_transposed=False,
                                      tm=nb, per_row=True, a_dtype=F32)
    a_s2, b_s2, g_s2 = a_s[0], b_s_act[0], g_s[0]
    m0_pad = jnp.pad(state_mlstm_m[0], ((0, 0), (0, LANES - HEADS)))
    n0 = state_mlstm_n[0].reshape(nb, gw)
    mix_s, c_s, n_s, m_s = _mix_sample(a_s2, k_s[0], b_s2, g_s2, m0_pad, n0, state_mlstm_C[0],
                                       bias_row, g_head, ws0_row, bs0_row)
    y_s = _output_stage(xs, mix_s.reshape(1, nb, D_MODEL), mod_s, w_out_b, g2, w_gu_b, w_dn_b, gf,
                        tm=nb, per_row=True)

    return (
        y_p,
        y_s.reshape(nb, 1, D_MODEL),
        c_p[None],
        n_p[None],
        m_p[:, :, 0][None],
        c_s[None],
        n_s.reshape(nb, HEADS, HEAD_DIM)[None],
        m_s.reshape(nb, HEADS, HEAD_DIM)[:, :, 0][None],
        a_s2[:, 2 * gw:].reshape(nb, 1, HEADS, HEAD_DIM)[None],
    )
```

```python
import functools
import math

import jax
import jax.numpy as jnp
from jax import lax
from jax.experimental import pallas as pl
from jax.experimental.pallas import tpu as pltpu

F32 = jnp.float32
BF16 = jnp.bfloat16

D_MODEL = 1024
HEADS = 4
HEAD_DIM = 128
GROUP_WIDTH = HEADS * HEAD_DIM
CHUNK = 128
D_FF = 2816
N_MOD = 6
EPS = 1e-6
QK_SCALE = HEAD_DIM ** -0.5
LANES = 128

N_B = 2 * GROUP_WIDTH

VMEM_LIMIT = 56 * 1024 * 1024


def _dot(a, b):
    return jnp.dot(a, b, preferred_element_type=F32)


def _dot_nt(a, b):
    return lax.dot_general(a, b, (((1,), (1,)), ((), ())), preferred_element_type=F32)


def _sigmoid(x):
    return 1.0 / (1.0 + jnp.exp(-x))


def _gelu_tanh(x):
    c = math.sqrt(2.0 / math.pi)
    return x * (0.5 * (1.0 + jnp.tanh(c * (x + 0.044715 * (x * x * x)))))


def _log_sigmoid(x):
    return jnp.minimum(x, 0.0) - jnp.log1p(jnp.exp(-jnp.abs(x)))


def _rms(x):
    return x * lax.rsqrt(jnp.mean(x * x, axis=-1, keepdims=True) + EPS)


def _split3_bf16(x):
    hi = x.astype(BF16)
    r1 = x - hi.astype(F32)
    mid = r1.astype(BF16)
    lo = (r1 - mid.astype(F32)).astype(BF16)
    return hi, mid, lo


def _mod_kernel(c_ref, w_ref, b_ref, o_ref):
    c = c_ref[...]
    a = (c * _sigmoid(c)).astype(BF16)
    o_ref[...] = _dot(a, w_ref[...].astype(BF16)) + b_ref[...]


def _modulation(c_all, w_ada, b_ada):
    rows = c_all.shape[0]
    tn = 1024
    return pl.pallas_call(
        _mod_kernel,
        grid=(N_MOD * D_MODEL // tn,),
        in_specs=[
            pl.BlockSpec((rows, D_MODEL), lambda j: (0, 0)),
            pl.BlockSpec((D_MODEL, tn), lambda j: (0, j)),
            pl.BlockSpec((1, tn), lambda j: (0, j)),
        ],
        out_specs=pl.BlockSpec((rows, tn), lambda j: (0, j)),
        out_shape=jax.ShapeDtypeStruct((rows, N_MOD * D_MODEL), F32),
        compiler_params=pltpu.CompilerParams(dimension_semantics=("arbitrary",)),
        name="modulation",
    )(c_all, w_ada, b_ada)


def _mod_spec(piece, tm, per_row):
    if per_row:
        return pl.BlockSpec((None, tm, D_MODEL), lambda g, t: (g, t, piece))
    return pl.BlockSpec((None, 1, D_MODEL), lambda g, t: (g, 0, piece))


def _const_spec(shape):
    nd = len(shape)
    return pl.BlockSpec(shape, lambda g, t: (0,) * nd, pipeline_mode=pl.Buffered(1))


def _proj_body(x_ref, sh_ref, sc_ref, g1_ref, w_ref, lng_ref, lnb_ref, a_ref, b_ref, g_ref):
    n_a = a_ref.shape[1]
    n_copy = n_a - GROUP_WIDTH
    x = x_ref[...]
    h = _rms(x) * g1_ref[...]
    h = (h * (1.0 + sc_ref[...]) + sh_ref[...]).astype(BF16)
    p = _dot(h, w_ref[...])
    a_ref[:, :n_copy] = p[:, :n_copy].astype(a_ref.dtype)
    vs = _gelu_tanh(p[:, n_copy:n_a])
    for hd in range(HEADS):
        sl = slice(hd * HEAD_DIM, (hd + 1) * HEAD_DIM)
        v = vs[:, sl]
        mu = jnp.mean(v, axis=-1, keepdims=True)
        vc = v - mu
        var = jnp.mean(vc * vc, axis=-1, keepdims=True)
        y = vc * lax.rsqrt(var + EPS) * lng_ref[:, sl] + lnb_ref[:, sl]
        a_ref[:, n_copy + hd * HEAD_DIM: n_copy + (hd + 1) * HEAD_DIM] = y.astype(a_ref.dtype)
    b_ref[:, :GROUP_WIDTH] = _sigmoid(p[:, n_a: n_a + GROUP_WIDTH])
    b_ref[:, GROUP_WIDTH:] = _gelu_tanh(p[:, n_a + GROUP_WIDTH: n_a + N_B])
    g_ref[...] = p[:, n_a + N_B:]
    return h


def _proj_k_kernel(x_ref, sh_ref, sc_ref, g1_ref, w_ref, lng_ref, lnb_ref, wk_ref, a_ref, b_ref, g_ref, k_ref):
    h = _proj_body(x_ref, sh_ref, sc_ref, g1_ref, w_ref, lng_ref, lnb_ref, a_ref, b_ref, g_ref)
    k_ref[...] = _dot(h, wk_ref[...]).astype(k_ref.dtype)


def _proj_kt_kernel(x_ref, sh_ref, sc_ref, g1_ref, w_ref, lng_ref, lnb_ref, wkt_ref, a_ref, b_ref, g_ref, kt_ref):
    h = _proj_body(x_ref, sh_ref, sc_ref, g1_ref, w_ref, lng_ref, lnb_ref, a_ref, b_ref, g_ref)
    kt_ref[...] = _dot_nt(wkt_ref[...], h).astype(kt_ref.dtype)


def _project(x, mod, g1, w_proj, ln_g, ln_b, w_k, *, k_transposed, tm, per_row, a_dtype):
    groups, t, _ = x.shape
    n_proj = w_proj.shape[1]
    n_a = n_proj - N_B - LANES
    in_specs = [
        pl.BlockSpec((None, tm, D_MODEL), lambda g, i: (g, i, 0)),
        _mod_spec(0, tm, per_row),
        _mod_spec(1, tm, per_row),
        _const_spec((1, D_MODEL)),
        _const_spec((D_MODEL, n_proj)),
        _const_spec((1, GROUP_WIDTH)),
        _const_spec((1, GROUP_WIDTH)),
    ]
    out_specs = [
        pl.BlockSpec((None, tm, n_a), lambda g, i: (g, i, 0)),
        pl.BlockSpec((None, tm, N_B), lambda g, i: (g, i, 0)),
        pl.BlockSpec((None, tm, LANES), lambda g, i: (g, i, 0)),
    ]
    out_shape = [
        jax.ShapeDtypeStruct((groups, t, n_a), a_dtype),
        jax.ShapeDtypeStruct((groups, t, N_B), F32),
        jax.ShapeDtypeStruct((groups, t, LANES), F32),
    ]
    args = [x, mod, mod, g1, w_proj, ln_g, ln_b, w_k]
    in_specs.append(_const_spec(w_k.shape))
    if k_transposed:
        out_specs.append(pl.BlockSpec((None, GROUP_WIDTH, tm), lambda g, i: (g, 0, i)))
        out_shape.append(jax.ShapeDtypeStruct((groups, GROUP_WIDTH, t), a_dtype))
    else:
        out_specs.append(pl.BlockSpec((None, tm, GROUP_WIDTH), lambda g, i: (g, i, 0)))
        out_shape.append(jax.ShapeDtypeStruct((groups, t, GROUP_WIDTH), a_dtype))
    return pl.pallas_call(
        _proj_kt_kernel if k_transposed else _proj_k_kernel,
        grid=(groups, t // tm),
        in_specs=in_specs,
        out_specs=out_specs,
        out_shape=out_shape,
        compiler_params=pltpu.CompilerParams(
            dimension_semantics=("arbitrary", "arbitrary"), vmem_limit_bytes=VMEM_LIMIT),
        name="project",
    )(*args)


MIX_TILE = 512


def _dot3_rhs(lhs_bf16, rhs_f32):
    hi, mid, lo = _split3_bf16(rhs_f32)
    return _dot(lhs_bf16, hi) + _dot(lhs_bf16, mid) + _dot(lhs_bf16, lo)


def _dot3_lhs(lhs_f32, rhs_bf16):
    hi, mid, lo = _split3_bf16(lhs_f32)
    return _dot(hi, rhs_bf16) + _dot(mid, rhs_bf16) + _dot(lo, rhs_bf16)


FF_CHUNKS = (768, 768, 768, 512)
MIXER_LOOKAHEAD = 2


def _gate_rows(g_ref, bias_ref, tril, triu, n_chunks):
    pre = [g_ref[c * CHUNK:(c + 1) * CHUNK, :] + bias_ref[...] for c in range(n_chunks)]
    bc_all = _dot3_rhs(tril, jnp.concatenate([_log_sigmoid(p) for p in pre], axis=1))
    rows_all = jnp.concatenate([p.T[0:2 * HEADS, :] for p in pre], axis=0)
    b_rows_all = _dot3_lhs(_log_sigmoid(rows_all), triu)
    bc, arow, blast = [], [], []
    for c in range(n_chunks):
        bc.append(bc_all[:, c * LANES:(c + 1) * LANES])
        rows = rows_all[c * 2 * HEADS:(c + 1) * 2 * HEADS, :]
        b_rows = b_rows_all[c * 2 * HEADS:(c + 1) * 2 * HEADS, :]
        arow.append([rows[hd:hd + 1, :] - b_rows[HEADS + hd:HEADS + hd + 1, :] for hd in range(HEADS)])
        blast.append([jnp.min(b_rows[HEADS + hd:HEADS + hd + 1, :], axis=1, keepdims=True) for hd in range(HEADS)])
    return bc, arow, blast


def _mlstm_local(q, kt, a_row, b_last, causal):
    amat = jnp.where(causal, a_row, -jnp.inf)
    m_row = jnp.max(amat, axis=1, keepdims=True)
    s_loc = (_dot(q, kt) * jnp.exp(amat - m_row)).astype(BF16)
    g_row = b_last + a_row
    g_loc = jnp.max(g_row, axis=1, keepdims=True)
    kw = (kt.astype(F32) * jnp.exp(g_row - g_loc)).astype(BF16)
    return m_row, s_loc, g_loc, kw


def _mlstm_readout(local, q, v, og, g_head, b_last, b_rep, cta, m_prev, ones_blk):
    m_row, s_loc, g_loc, kw = local
    va = jnp.concatenate([v, ones_blk], axis=1)
    nd_loc = _dot(s_loc, va)
    u_aug = _dot(kw, va)
    inter = _dot(q, cta.astype(BF16))
    mm = jnp.maximum(m_prev, m_row)
    f_loc = jnp.exp(m_row - mm) * QK_SCALE
    f_int = jnp.exp(m_prev - mm) * QK_SCALE
    nd = f_loc * nd_loc + f_int * inter
    clamp = jnp.exp(-(b_rep + mm))
    hh = nd[:, :HEAD_DIM] / jnp.maximum(jnp.abs(nd[:, HEAD_DIM:]), clamp)
    ml = _rms(hh) * g_head * og

    dec = b_last + m_prev
    m_new = jnp.maximum(dec, g_loc)
    cta_new = jnp.exp(dec - m_new) * cta + jnp.exp(g_loc - m_new) * u_aug
    return ml, cta_new, m_new


def _prompt_tail_kernel(a_ref, kt_ref, b_ref, g_ref, x_ref, gt1_ref, sh2_ref, sc2_ref, gt2_ref,
                        bias_ref, gh_ref, ws_ref, bs_ref, wo_ref, g2_ref, wgu_ref, wdn_ref, gf_ref,
                        y_ref, c_out, n_out, m_out,
                        mix_ref, cta_ref, m_ref, *, tiles_per_seq):
    step = pl.program_id(0)
    n_tiles = pl.num_programs(0) - 1
    tm = a_ref.shape[0]
    n_chunks = tm // CHUNK
    n_copy = a_ref.shape[1] - GROUP_WIDTH

    @pl.when(step == 0)
    def _():
        mix_ref[...] = jnp.zeros_like(mix_ref)

    @pl.when(step % tiles_per_seq == 0)
    def _():
        cta_ref[...] = jnp.zeros_like(cta_ref)
        m_ref[...] = jnp.zeros_like(m_ref)

    row = lax.broadcasted_iota(jnp.int32, (CHUNK, CHUNK), 0)
    col = lax.broadcasted_iota(jnp.int32, (CHUNK, CHUNK), 1)
    causal = row >= col
    tril = jnp.where(causal, 1.0, 0.0).astype(BF16)
    triu = jnp.where(row <= col, 1.0, 0.0).astype(BF16)
    ones_blk = jnp.ones((CHUNK, HEAD_DIM), BF16)

    x1 = x_ref[...] + gt1_ref[...] * _dot(mix_ref[...], wo_ref[...])
    h2 = _rms(x1) * g2_ref[...]
    h2 = (h2 * (1.0 + sc2_ref[...]) + sh2_ref[...]).astype(BF16)

    bc, arow, blast = _gate_rows(g_ref, bias_ref, tril, triu, n_chunks)
    state = [(cta_ref[hd], jnp.max(m_ref[hd][0:1, :], axis=1, keepdims=True)) for hd in range(HEADS)]
    sg_all = [None] * HEADS

    def mixer_local(c, hd):
        rs = slice(c * CHUNK, (c + 1) * CHUNK)
        sl = slice(hd * HEAD_DIM, (hd + 1) * HEAD_DIM)
        return _mlstm_local(a_ref[rs, sl], kt_ref[sl, rs], arow[c][hd], blast[c][hd], causal)

    def mixer_readout(c, hd, local):
        rs = slice(c * CHUNK, (c + 1) * CHUNK)
        sl = slice(hd * HEAD_DIM, (hd + 1) * HEAD_DIM)
        if c == 0:
            vsn_all = jnp.concatenate(
                [a_ref[cc * CHUNK:(cc + 1) * CHUNK, n_copy + hd * HEAD_DIM: n_copy + (hd + 1) * HEAD_DIM]
                 for cc in range(n_chunks)], axis=1)
            sg_all[hd] = _dot(ws_ref[hd], vsn_all)
        cta, m_prev = state[hd]
        b_rep = jnp.broadcast_to(bc[c][:, HEADS + hd: HEADS + hd + 1], (CHUNK, HEAD_DIM))
        ml, cta, m_prev = _mlstm_readout(
            local, a_ref[rs, sl], a_ref[rs, GROUP_WIDTH + hd * HEAD_DIM: GROUP_WIDTH + (hd + 1) * HEAD_DIM],
            b_ref[rs, sl], gh_ref[:, sl], blast[c][hd], b_rep, cta, m_prev, ones_blk)
        state[hd] = (cta, m_prev)
        mix_ref[rs, sl] = ml.astype(mix_ref.dtype)
        sg = sg_all[hd][:, c * CHUNK:(c + 1) * CHUNK] + bs_ref[hd]
        ug = b_ref[rs, GROUP_WIDTH + hd * HEAD_DIM: GROUP_WIDTH + (hd + 1) * HEAD_DIM]
        mix_ref[rs, GROUP_WIDTH + hd * HEAD_DIM: GROUP_WIDTH + (hd + 1) * HEAD_DIM] = (ug * sg).astype(mix_ref.dtype)

    pieces = [(c, hd) for c in range(n_chunks) for hd in range(HEADS)]
    per_ff = -(-len(pieces) // len(FF_CHUNKS))
    ahead = [mixer_local(*pieces[p]) for p in range(MIXER_LOOKAHEAD)]
    acc = None
    f0 = 0
    for j, width in enumerate(FF_CHUNKS):
        gate = _dot(h2, wgu_ref[:, f0:f0 + width])
        up = _dot(h2, wgu_ref[:, D_FF + f0:D_FF + f0 + width])
        for idx in range(j * per_ff, min((j + 1) * per_ff, len(pieces))):
            if idx + MIXER_LOOKAHEAD < len(pieces):
                ahead.append(mixer_local(*pieces[idx + MIXER_LOOKAHEAD]))
            mixer_readout(*pieces[idx], ahead.pop(0))
        act = (gate * _sigmoid(gate) * up).astype(BF16)
        part = _dot(act, wdn_ref[f0:f0 + width, :])
        acc = part if acc is None else acc + part
        f0 += width
    x2 = x1 + gt2_ref[...] * acc
    y_ref[...] = _rms(x2) * gf_ref[...]

    for hd in range(HEADS):
        cta_ref[hd] = state[hd][0]
        m_ref[hd] = jnp.broadcast_to(state[hd][1], m_ref.shape[1:])

    @pl.when(jnp.logical_and(step % tiles_per_seq == tiles_per_seq - 1, step < n_tiles))
    def _():
        for hd in range(HEADS):
            cta = cta_ref[hd]
            c_out[hd] = cta[:, :HEAD_DIM].T
            n_out[hd: hd + 1, :] = cta[:, HEAD_DIM:].T[0:1, :]
            m_out[hd: hd + 1, :] = m_ref[hd][0:1, :]


def _prompt_tail(a, kt, b, gates, x, mod, bias_row, g_head, ws_tril, bs_rep, w_out, g2, w_gu, w_dn, g_final):
    groups, t, n_a = a.shape
    tm = MIX_TILE
    tps = t // tm
    n_tiles = groups * tps
    assert sum(FF_CHUNKS) == D_FF

    def cur(i):
        return jnp.minimum(i, n_tiles - 1)

    def prev(i):
        return jnp.maximum(i - 1, 0)

    def rows(tile, width):
        return pl.BlockSpec((None, tm, width), lambda i: (tile(i) // tps, tile(i) % tps, 0))

    def mod_piece(piece):
        return pl.BlockSpec((None, 1, D_MODEL), lambda i: (prev(i) // tps, 0, piece))

    def const(shape):
        nd = len(shape)
        return pl.BlockSpec(shape, lambda i: (0,) * nd, pipeline_mode=pl.Buffered(1))

    def per_seq(shape):
        nd = len(shape)
        return pl.BlockSpec((None,) + shape, lambda i: (cur(i) // tps,) + (0,) * nd)

    return pl.pallas_call(
        functools.partial(_prompt_tail_kernel, tiles_per_seq=tps),
        grid=(n_tiles + 1,),
        in_specs=[
            rows(cur, n_a),
            pl.BlockSpec((None, GROUP_WIDTH, tm), lambda i: (cur(i) // tps, 0, cur(i) % tps)),
            rows(cur, N_B),
            rows(cur, LANES),
            rows(prev, D_MODEL),
            mod_piece(2), mod_piece(3), mod_piece(4), mod_piece(5),
            const((1, LANES)),
            const((1, GROUP_WIDTH)),
            const((HEADS, CHUNK, CHUNK)),
            const((HEADS, CHUNK, LANES)),
            const((D_MODEL, D_MODEL)),
            const((1, D_MODEL)),
            const((D_MODEL, 2 * D_FF)),
            const((D_FF, D_MODEL)),
            const((1, D_MODEL)),
        ],
        out_specs=[
            rows(prev, D_MODEL),
            per_seq((HEADS, HEAD_DIM, HEAD_DIM)),
            per_seq((HEADS, HEAD_DIM)),
            per_seq((HEADS, LANES)),
        ],
        out_shape=[
            jax.ShapeDtypeStruct((groups, t, D_MODEL), F32),
            jax.ShapeDtypeStruct((groups, HEADS, HEAD_DIM, HEAD_DIM), F32),
            jax.ShapeDtypeStruct((groups, HEADS, HEAD_DIM), F32),
            jax.ShapeDtypeStruct((groups, HEADS, LANES), F32),
        ],
        scratch_shapes=[
            pltpu.VMEM((tm, D_MODEL), BF16),
            pltpu.VMEM((HEADS, HEAD_DIM, 2 * HEAD_DIM), F32),
            pltpu.VMEM((HEADS, 8, LANES), F32),
        ],
        compiler_params=pltpu.CompilerParams(
            dimension_semantics=("arbitrary",), vmem_limit_bytes=VMEM_LIMIT),
        name="prompt_tail",
    )(a, kt, b, gates, x, mod, mod, mod, mod, bias_row, g_head, ws_tril, bs_rep, w_out, g2, w_gu, w_dn, g_final)


SAMPLE_TOKENS_PER_STEP = 32
SAMPLE_UNROLL = 8


def _mix_sample_kernel(q_ref, k_ref, v_ref, vsn_ref, og_ref, ug_ref, g_ref, m0_ref, n0_ref, c_ref,
                       bias_ref, gh_ref, ws0_ref, bs0_ref,
                       mix_ref, c_out, n_out, m_out,
                       vt_ref, kp_ref, wd_ref, cqt_ref):
    hd = pl.program_id(0)
    grp = pl.program_id(1)
    nb = q_ref.shape[0]
    lane = lax.broadcasted_iota(jnp.int32, (nb, LANES), 1)

    def gate_terms():
        pre = g_ref[...] + bias_ref[...]
        i_pre = jnp.sum(jnp.where(lane == hd, pre, 0.0), axis=1, keepdims=True)
        f_pre = jnp.sum(jnp.where(lane == hd + HEADS, pre, 0.0), axis=1, keepdims=True)
        m_prev = jnp.sum(jnp.where(lane == hd, m0_ref[...], 0.0), axis=1, keepdims=True)
        inter = _log_sigmoid(f_pre) + m_prev
        m_t = jnp.maximum(inter, i_pre)
        return m_t, jnp.exp(i_pre - m_t), jnp.exp(inter - m_t)

    @pl.when(grp == 0)
    def _():
        _, w_in, w_dec = gate_terms()
        vt_ref[...] = v_ref[...].T
        kp_ref[...] = (w_in * k_ref[...]).astype(BF16)
        wd_ref[...] = jnp.broadcast_to(w_dec, wd_ref.shape)
        cqt_ref[...] = jnp.zeros_like(cqt_ref)

    lane_sq = lax.broadcasted_iota(jnp.int32, (HEAD_DIM, LANES), 1)
    tokens = c_ref.shape[0]

    def body(r, cqt):
        tok = grp * tokens + r
        q_row = q_ref[pl.ds(tok, 1), :]
        c_b = c_ref[r]
        sel = lane_sq == tok
        cq_col = jnp.sum(c_b * q_row, axis=1, keepdims=True)
        v_sel = jnp.where(sel, vt_ref[...], 0.0).astype(BF16)
        outer = _dot(v_sel, kp_ref[...])
        c_out[r] = wd_ref[pl.ds(tok, 1), :] * c_b + outer
        return jnp.where(sel, cq_col, cqt)

    cqt_ref[...] = lax.fori_loop(0, tokens, body, cqt_ref[...], unroll=SAMPLE_UNROLL)

    @pl.when(grp == pl.num_programs(1) - 1)
    def _():
        m_t, w_in, w_dec = gate_terms()
        q = q_ref[...]
        k = k_ref[...]
        v = v_ref[...]
        n0 = n0_ref[...]
        cq = cqt_ref[...].T
        s = jnp.sum(q * k, axis=1, keepdims=True) * (QK_SCALE * w_in)
        w_inter = w_dec * QK_SCALE
        num = s * v + w_inter * cq
        den = s + w_inter * jnp.sum(n0 * q, axis=1, keepdims=True)
        hh = num / jnp.maximum(jnp.abs(den), jnp.exp(-m_t))
        ml = _rms(hh) * gh_ref[...] * og_ref[...]
        cm = ug_ref[...] * (ws0_ref[...] * vsn_ref[...] + bs0_ref[...])
        n_out[...] = w_dec * n0 + w_in * k
        m_out[...] = jnp.broadcast_to(m_t, m_out.shape)
        for hh_static in range(HEADS):
            @pl.when(hd == hh_static)
            def _():
                mix_ref[:, hh_static * HEAD_DIM: (hh_static + 1) * HEAD_DIM] = ml
                mix_ref[:, GROUP_WIDTH + hh_static * HEAD_DIM: GROUP_WIDTH + (hh_static + 1) * HEAD_DIM] = cm


def _mix_sample(a, k, b, gates, m0_pad, n0, c0, bias_row, g_head, ws0_row, bs0_row):
    nb = a.shape[0]
    tb = SAMPLE_TOKENS_PER_STEP

    def head_block(offset):
        return pl.BlockSpec((nb, HEAD_DIM), lambda h, g: (0, offset + h))

    full = pl.BlockSpec((nb, LANES), lambda h, g: (0, 0))
    head_row = pl.BlockSpec((1, HEAD_DIM), lambda h, g: (0, h))
    c_spec = pl.BlockSpec((tb, None, HEAD_DIM, HEAD_DIM), lambda h, g: (g, h, 0, 0))
    return pl.pallas_call(
        _mix_sample_kernel,
        grid=(HEADS, nb // tb),
        in_specs=[
            head_block(0), head_block(0), head_block(HEADS), head_block(2 * HEADS),
            head_block(0), head_block(HEADS),
            full, full, head_block(0), c_spec,
            pl.BlockSpec((1, LANES), lambda h, g: (0, 0)),
            head_row, head_row, head_row,
        ],
        out_specs=[
            pl.BlockSpec((nb, D_MODEL), lambda h, g: (0, 0)),
            c_spec,
            head_block(0),
            head_block(0),
        ],
        out_shape=[
            jax.ShapeDtypeStruct((nb, D_MODEL), F32),
            jax.ShapeDtypeStruct(c0.shape, F32),
            jax.ShapeDtypeStruct((nb, GROUP_WIDTH), F32),
            jax.ShapeDtypeStruct((nb, GROUP_WIDTH), F32),
        ],
        scratch_shapes=[
            pltpu.VMEM((HEAD_DIM, nb), F32),
            pltpu.VMEM((nb, HEAD_DIM), BF16),
            pltpu.VMEM((nb, LANES), F32),
            pltpu.VMEM((HEAD_DIM, nb), F32),
        ],
        compiler_params=pltpu.CompilerParams(
            dimension_semantics=("arbitrary", "arbitrary"), vmem_limit_bytes=VMEM_LIMIT),
        name="mix_sample",
    )(a, k, a, a, b, b, gates, m0_pad, n0, c0, bias_row, g_head, ws0_row, bs0_row)


OUT_ROW_BLOCKS = 2


def _out_kernel(x_ref, mix_ref, gt1_ref, sh2_ref, sc2_ref, gt2_ref, wo_ref, g2_ref, wgu_ref, wdn_ref, gf_ref,
                y_ref):
    tm = x_ref.shape[0]
    rows_per_block = tm // OUT_ROW_BLOCKS if tm % (OUT_ROW_BLOCKS * 128) == 0 else tm
    per_row = gt1_ref.shape[0] != 1
    blocks = []
    for r0 in range(0, tm, rows_per_block):
        rs = slice(r0, r0 + rows_per_block)
        ms = rs if per_row else slice(0, 1)
        x1 = x_ref[rs, :] + gt1_ref[ms, :] * _dot(mix_ref[rs, :].astype(BF16), wo_ref[...])
        h2 = _rms(x1) * g2_ref[...]
        h2 = (h2 * (1.0 + sc2_ref[ms, :]) + sh2_ref[ms, :]).astype(BF16)
        blocks.append((rs, ms, x1, h2))
    for rs, ms, x1, h2 in blocks:
        gate = _dot(h2, wgu_ref[:, :D_FF])
        up = _dot(h2, wgu_ref[:, D_FF:])
        act = (gate * _sigmoid(gate) * up).astype(BF16)
        x2 = x1 + gt2_ref[ms, :] * _dot(act, wdn_ref[...])
        y_ref[rs, :] = _rms(x2) * gf_ref[...]


def _output_stage(x, mix, mod, w_out, g2, w_gu, w_dn, g_final, *, tm, per_row):
    groups, t, _ = x.shape
    return pl.pallas_call(
        _out_kernel,
        grid=(groups, t // tm),
        in_specs=[
            pl.BlockSpec((None, tm, D_MODEL), lambda g, i: (g, i, 0)),
            pl.BlockSpec((None, tm, D_MODEL), lambda g, i: (g, i, 0)),
            _mod_spec(2, tm, per_row),
            _mod_spec(3, tm, per_row),
            _mod_spec(4, tm, per_row),
            _mod_spec(5, tm, per_row),
            _const_spec((D_MODEL, D_MODEL)),
            _const_spec((1, D_MODEL)),
            _const_spec((D_MODEL, 2 * D_FF)),
            _const_spec((D_FF, D_MODEL)),
            _const_spec((1, D_MODEL)),
        ],
        out_specs=pl.BlockSpec((None, tm, D_MODEL), lambda g, i: (g, i, 0)),
        out_shape=jax.ShapeDtypeStruct((groups, t, D_MODEL), F32),
        compiler_params=pltpu.CompilerParams(
            dimension_semantics=("arbitrary", "arbitrary"), vmem_limit_bytes=VMEM_LIMIT),
        name="output_stage",
    )(x, mix, mod, mod, mod, mod, w_out, g2, w_gu, w_dn, g_final)


def kernel(x_prompt, x_sample, c_prompt, c_sample, state_mlstm_C, state_mlstm_n, state_mlstm_m, w_ada, b_ada, g_norm1, w_in, b_gate, g_mlstm_head, ln_v_g, ln_v_b, w_s, b_s, w_out, g_norm2, w_gate_up, w_down, g_final):
    depth = w_ada.shape[0]
    assert depth == 1, "single-layer trunk"
    batch, seq, _ = x_prompt.shape
    nb = x_sample.shape[0]
    assert x_sample.shape[1] == 1

    wi = w_in[0]
    gw = GROUP_WIDTH
    o_q, o_k, o_v, o_o, o_i = 0, gw, 2 * gw, 3 * gw, 4 * gw
    o_u = o_i + 2 * HEADS
    o_vs = o_u + gw
    w_proj = jnp.concatenate(
        [wi[:, o_q:o_k], wi[:, o_v:o_o], wi[:, o_vs:o_vs + gw], wi[:, o_o:o_i], wi[:, o_u:o_vs],
         jnp.pad(wi[:, o_i:o_u], ((0, 0), (0, LANES - 2 * HEADS)))], axis=1).astype(BF16)
    w_k = wi[:, o_k:o_v].astype(BF16)
    w_kt = w_k.T
    w_out_b = w_out[0].astype(BF16)
    w_gu_b = w_gate_up[0].astype(BF16)
    w_dn_b = w_down[0].astype(BF16)
    g1 = g_norm1[0].reshape(1, D_MODEL)
    g2 = g_norm2[0].reshape(1, D_MODEL)
    gf = g_final.reshape(1, D_MODEL)
    ln_g = ln_v_g[0].reshape(1, gw)
    ln_b = ln_v_b[0].reshape(1, gw)
    g_head = g_mlstm_head[0].reshape(1, gw)
    bias_row = jnp.pad(b_gate[0], (0, LANES - 2 * HEADS)).reshape(1, LANES)
    tril = jnp.tril(jnp.ones((CHUNK, CHUNK), dtype=bool))
    ws_tril = jnp.where(tril[None], w_s[0], 0.0).astype(BF16)
    bs_rep = jnp.broadcast_to(b_s[0][:, :, None], (HEADS, CHUNK, LANES))
    ws0_row = jnp.repeat(w_s[0][:, 0, 0], HEAD_DIM).reshape(1, gw)
    bs0_row = jnp.repeat(b_s[0][:, 0], HEAD_DIM).reshape(1, gw)

    rows = batch + nb
    rows_pad = -(-rows // 16) * 16
    c_all = jnp.concatenate([c_prompt, c_sample, jnp.zeros((rows_pad - rows, D_MODEL), F32)], axis=0)
    mod = _modulation(c_all, w_ada[0], b_ada[0].reshape(1, -1))
    mod_p = mod[:batch].reshape(batch, 1, N_MOD * D_MODEL)
    mod_s = mod[batch:rows].reshape(1, nb, N_MOD * D_MODEL)

    a_p, b_p, g_p, kt_p = _project(x_prompt, mod_p, g1, w_proj, ln_g, ln_b, w_kt, k_transposed=True,
                                   tm=MIX_TILE, per_row=False, a_dtype=BF16)
    y_p, c_p, n_p, m_p = _prompt_tail(a_p, kt_p, b_p, g_p, x_prompt, mod_p, bias_row, g_head, ws_tril, bs_rep,
                                      w_out_b, g2, w_gu_b, w_dn_b, gf)

    xs = x_sample.reshape(1, nb, D_MODEL)
    a_s, b_s_act, g_s, k_s = _project(xs, mod_s, g1, w_proj, ln_g, ln_b, w_k, k_transposed=False,
                                      tm=nb, per_row=True, a_dtype=F32)
    a_s2, b_s2, g_s2 = a_s[0], b_s_act[0], g_s[0]
    m0_pad = jnp.pad(state_mlstm_m[0], ((0, 0), (0, LANES - HEADS)))
    n0 = state_mlstm_n[0].reshape(nb, gw)
    mix_s, c_s, n_s, m_s = _mix_sample(a_s2, k_s[0], b_s2, g_s2, m0_pad, n0, state_mlstm_C[0],
                                       bias_row, g_head, ws0_row, bs0_row)
    y_s = _output_stage(xs, mix_s.reshape(1, nb, D_MODEL), mod_s, w_out_b, g2, w_gu_b, w_dn_b, gf,
                        tm=nb, per_row=True)

    return (
        y_p,
        y_s.reshape(nb, 1, D_MODEL),
        c_p[None],
        n_p[None],
        m_p[:, :, 0][None],
        c_s[None],
        n_s.reshape(nb, HEADS, HEAD_DIM)[None],
        m_s.reshape(nb, HEADS, HEAD_DIM)[:, :, 0][None],
        a_s2[:, 2 * gw:].reshape(nb, 1, HEADS, HEAD_DIM)[None],
    )
```

```python
import functools
import math

import jax
import jax.numpy as jnp
from jax import lax
from jax.experimental import pallas as pl
from jax.experimental.pallas import tpu as pltpu

F32 = jnp.float32
BF16 = jnp.bfloat16

D_MODEL = 1024
HEADS = 4
HEAD_DIM = 128
GROUP_WIDTH = HEADS * HEAD_DIM
CHUNK = 128
D_FF = 2816
N_MOD = 6
EPS = 1e-6
QK_SCALE = HEAD_DIM ** -0.5
LANES = 128

N_B = 2 * GROUP_WIDTH

VMEM_LIMIT = 56 * 1024 * 1024


def _dot(a, b):
    return jnp.dot(a, b, preferred_element_type=F32)


def _dot_nt(a, b):
    return lax.dot_general(a, b, (((1,), (1,)), ((), ())), preferred_element_type=F32)


def _sigmoid(x):
    return 1.0 / (1.0 + jnp.exp(-x))


def _gelu_tanh(x):
    c = math.sqrt(2.0 / math.pi)
    return x * (0.5 * (1.0 + jnp.tanh(c * (x + 0.044715 * (x * x * x)))))


def _log_sigmoid(x):
    return jnp.minimum(x, 0.0) - jnp.log1p(jnp.exp(-jnp.abs(x)))


def _rms(x):
    return x * lax.rsqrt(jnp.mean(x * x, axis=-1, keepdims=True) + EPS)


def _split3_bf16(x):
    hi = x.astype(BF16)
    r1 = x - hi.astype(F32)
    mid = r1.astype(BF16)
    lo = (r1 - mid.astype(F32)).astype(BF16)
    return hi, mid, lo


def _mod_kernel(c_ref, w_ref, b_ref, o_ref):
    c = c_ref[...]
    a = (c * _sigmoid(c)).astype(BF16)
    o_ref[...] = _dot(a, w_ref[...].astype(BF16)) + b_ref[...]


def _modulation(c_all, w_ada, b_ada):
    rows = c_all.shape[0]
    tn = 1024
    return pl.pallas_call(
        _mod_kernel,
        grid=(N_MOD * D_MODEL // tn,),
        in_specs=[
            pl.BlockSpec((rows, D_MODEL), lambda j: (0, 0)),
            pl.BlockSpec((D_MODEL, tn), lambda j: (0, j)),
            pl.BlockSpec((1, tn), lambda j: (0, j)),
        ],
        out_specs=pl.BlockSpec((rows, tn), lambda j: (0, j)),
        out_shape=jax.ShapeDtypeStruct((rows, N_MOD * D_MODEL), F32),
        compiler_params=pltpu.CompilerParams(dimension_semantics=("arbitrary",)),
        name="modulation",
    )(c_all, w_ada, b_ada)


def _mod_spec(piece, tm, per_row):
    if per_row:
        return pl.BlockSpec((None, tm, D_MODEL), lambda g, t: (g, t, piece))
    return pl.BlockSpec((None, 1, D_MODEL), lambda g, t: (g, 0, piece))


def _const_spec(shape):
    nd = len(shape)
    return pl.BlockSpec(shape, lambda g, t: (0,) * nd, pipeline_mode=pl.Buffered(1))


def _proj_body(x_ref, sh_ref, sc_ref, g1_ref, w_ref, lng_ref, lnb_ref, a_ref, b_ref, g_ref):
    n_a = a_ref.shape[1]
    n_copy = n_a - GROUP_WIDTH
    x = x_ref[...]
    h = _rms(x) * g1_ref[...]
    h = (h * (1.0 + sc_ref[...]) + sh_ref[...]).astype(BF16)
    p = _dot(h, w_ref[...])
    a_ref[:, :n_copy] = p[:, :n_copy].astype(a_ref.dtype)
    vs = _gelu_tanh(p[:, n_copy:n_a])
    for hd in range(HEADS):
        sl = slice(hd * HEAD_DIM, (hd + 1) * HEAD_DIM)
        v = vs[:, sl]
        mu = jnp.mean(v, axis=-1, keepdims=True)
        vc = v - mu
        var = jnp.mean(vc * vc, axis=-1, keepdims=True)
        y = vc * lax.rsqrt(var + EPS) * lng_ref[:, sl] + lnb_ref[:, sl]
        a_ref[:, n_copy + hd * HEAD_DIM: n_copy + (hd + 1) * HEAD_DIM] = y.astype(a_ref.dtype)
    b_ref[:, :GROUP_WIDTH] = _sigmoid(p[:, n_a: n_a + GROUP_WIDTH])
    b_ref[:, GROUP_WIDTH:] = _gelu_tanh(p[:, n_a + GROUP_WIDTH: n_a + N_B])
    g_ref[...] = p[:, n_a + N_B:]
    return h


def _proj_k_kernel(x_ref, sh_ref, sc_ref, g1_ref, w_ref, lng_ref, lnb_ref, wk_ref, a_ref, b_ref, g_ref, k_ref):
    h = _proj_body(x_ref, sh_ref, sc_ref, g1_ref, w_ref, lng_ref, lnb_ref, a_ref, b_ref, g_ref)
    k_ref[...] = _dot(h, wk_ref[...]).astype(k_ref.dtype)


def _proj_kt_kernel(x_ref, sh_ref, sc_ref, g1_ref, w_ref, lng_ref, lnb_ref, wkt_ref, a_ref, b_ref, g_ref, kt_ref):
    h = _proj_body(x_ref, sh_ref, sc_ref, g1_ref, w_ref, lng_ref, lnb_ref, a_ref, b_ref, g_ref)
    kt_ref[...] = _dot_nt(wkt_ref[...], h).astype(kt_ref.dtype)


def _project(x, mod, g1, w_proj, ln_g, ln_b, w_k, *, k_transposed, tm, per_row, a_dtype):
    groups, t, _ = x.shape
    n_proj = w_proj.shape[1]
    n_a = n_proj - N_B - LANES
    in_specs = [
        pl.BlockSpec((None, tm, D_MODEL), lambda g, i: (g, i, 0)),
        _mod_spec(0, tm, per_row),
        _mod_spec(1, tm, per_row),
        _const_spec((1, D_MODEL)),
        _const_spec((D_MODEL, n_proj)),
        _const_spec((1, GROUP_WIDTH)),
        _const_spec((1, GROUP_WIDTH)),
    ]
    out_specs = [
        pl.BlockSpec((None, tm, n_a), lambda g, i: (g, i, 0)),
        pl.BlockSpec((None, tm, N_B), lambda g, i: (g, i, 0)),
        pl.BlockSpec((None, tm, LANES), lambda g, i: (g, i, 0)),
    ]
    out_shape = [
        jax.ShapeDtypeStruct((groups, t, n_a), a_dtype),
        jax.ShapeDtypeStruct((groups, t, N_B), F32),
        jax.ShapeDtypeStruct((groups, t, LANES), F32),
    ]
    args = [x, mod, mod, g1, w_proj, ln_g, ln_b, w_k]
    in_specs.append(_const_spec(w_k.shape))
    if k_transposed:
        out_specs.append(pl.BlockSpec((None, GROUP_WIDTH, tm), lambda g, i: (g, 0, i)))
        out_shape.append(jax.ShapeDtypeStruct((groups, GROUP_WIDTH, t), a_dtype))
    else:
        out_specs.append(pl.BlockSpec((None, tm, GROUP_WIDTH), lambda g, i: (g, i, 0)))
        out_shape.append(jax.ShapeDtypeStruct((groups, t, GROUP_WIDTH), a_dtype))
    return pl.pallas_call(
        _proj_kt_kernel if k_transposed else _proj_k_kernel,
        grid=(groups, t // tm),
        in_specs=in_specs,
        out_specs=out_specs,
        out_shape=out_shape,
        compiler_params=pltpu.CompilerParams(
            dimension_semantics=("arbitrary", "arbitrary"), vmem_limit_bytes=VMEM_LIMIT),
        name="project",
    )(*args)


MIX_TILE = 512


def _dot3_rhs(lhs_bf16, rhs_f32):
    hi, mid, lo = _split3_bf16(rhs_f32)
    return _dot(lhs_bf16, hi) + _dot(lhs_bf16, mid) + _dot(lhs_bf16, lo)


def _dot3_lhs(lhs_f32, rhs_bf16):
    hi, mid, lo = _split3_bf16(lhs_f32)
    return _dot(hi, rhs_bf16) + _dot(mid, rhs_bf16) + _dot(lo, rhs_bf16)


FF_CHUNKS = (768, 768, 768, 512)
MIXER_LOOKAHEAD = 2


def _gate_rows(g_ref, bias_ref, tril, triu, n_chunks):
    pre = [g_ref[c * CHUNK:(c + 1) * CHUNK, :] + bias_ref[...] for c in range(n_chunks)]
    bc_all = _dot3_rhs(tril, jnp.concatenate([_log_sigmoid(p) for p in pre], axis=1))
    rows_all = jnp.concatenate([p.T[0:2 * HEADS, :] for p in pre], axis=0)
    b_rows_all = _dot3_lhs(_log_sigmoid(rows_all), triu)
    bc, arow, blast = [], [], []
    for c in range(n_chunks):
        bc.append(bc_all[:, c * LANES:(c + 1) * LANES])
        rows = rows_all[c * 2 * HEADS:(c + 1) * 2 * HEADS, :]
        b_rows = b_rows_all[c * 2 * HEADS:(c + 1) * 2 * HEADS, :]
        arow.append([rows[hd:hd + 1, :] - b_rows[HEADS + hd:HEADS + hd + 1, :] for hd in range(HEADS)])
        blast.append([jnp.min(b_rows[HEADS + hd:HEADS + hd + 1, :], axis=1, keepdims=True) for hd in range(HEADS)])
    return bc, arow, blast


def _mlstm_local(q, kt, a_row, b_last, causal):
    amat = jnp.where(causal, a_row, -jnp.inf)
    m_row = jnp.max(amat, axis=1, keepdims=True)
    s_loc = (_dot(q, kt) * jnp.exp(amat - m_row)).astype(BF16)
    g_row = b_last + a_row
    g_loc = jnp.max(g_row, axis=1, keepdims=True)
    kw = (kt.astype(F32) * jnp.exp(g_row - g_loc)).astype(BF16)
    return m_row, s_loc, g_loc, kw


def _mlstm_readout(local, q, v, og, g_head, b_last, b_rep, cta, m_prev, ones_blk):
    m_row, s_loc, g_loc, kw = local
    va = jnp.concatenate([v, ones_blk], axis=1)
    nd_loc = _dot(s_loc, va)
    u_aug = _dot(kw, va)
    inter = _dot(q, cta.astype(BF16))
    mm = jnp.maximum(m_prev, m_row)
    f_loc = jnp.exp(m_row - mm) * QK_SCALE
    f_int = jnp.exp(m_prev - mm) * QK_SCALE
    nd = f_loc * nd_loc + f_int * inter
    clamp = jnp.exp(-(b_rep + mm))
    hh = nd[:, :HEAD_DIM] / jnp.maximum(jnp.abs(nd[:, HEAD_DIM:]), clamp)
    ml = _rms(hh) * g_head * og

    dec = b_last + m_prev
    m_new = jnp.maximum(dec, g_loc)
    cta_new = jnp.exp(dec - m_new) * cta + jnp.exp(g_loc - m_new) * u_aug
    return ml, cta_new, m_new


def _prompt_tail_kernel(a_ref, kt_ref, b_ref, g_ref, x_ref, gt1_ref, sh2_ref, sc2_ref, gt2_ref,
                        bias_ref, gh_ref, ws_ref, bs_ref, wo_ref, g2_ref, wgu_ref, wdn_ref, gf_ref,
                        y_ref, c_out, n_out, m_out,
                        mix_ref, cta_ref, m_ref, *, tiles_per_seq):
    step = pl.program_id(0)
    n_tiles = pl.num_programs(0) - 1
    tm = a_ref.shape[0]
    n_chunks = tm // CHUNK
    n_copy = a_ref.shape[1] - GROUP_WIDTH

    @pl.when(step == 0)
    def _():
        mix_ref[...] = jnp.zeros_like(mix_ref)

    @pl.when(step % tiles_per_seq == 0)
    def _():
        cta_ref[...] = jnp.zeros_like(cta_ref)
        m_ref[...] = jnp.zeros_like(m_ref)

    row = lax.broadcasted_iota(jnp.int32, (CHUNK, CHUNK), 0)
    col = lax.broadcasted_iota(jnp.int32, (CHUNK, CHUNK), 1)
    causal = row >= col
    tril = jnp.where(causal, 1.0, 0.0).astype(BF16)
    triu = jnp.where(row <= col, 1.0, 0.0).astype(BF16)
    ones_blk = jnp.ones((CHUNK, HEAD_DIM), BF16)

    blocks = [slice(r0, r0 + tm // OUT_ROW_BLOCKS) for r0 in range(0, tm, tm // OUT_ROW_BLOCKS)]
    outproj = [_dot(mix_ref[rs, :], wo_ref[...]) for rs in blocks]
    x1, h2 = [], []
    for rs, op in zip(blocks, outproj):
        x1.append(x_ref[rs, :] + gt1_ref[...] * op)
        h = _rms(x1[-1]) * g2_ref[...]
        h2.append((h * (1.0 + sc2_ref[...]) + sh2_ref[...]).astype(BF16))

    bc, arow, blast = _gate_rows(g_ref, bias_ref, tril, triu, n_chunks)
    state = [(cta_ref[hd], jnp.max(m_ref[hd][0:1, :], axis=1, keepdims=True)) for hd in range(HEADS)]
    sg_all = [None] * HEADS

    def mixer_local(c, hd):
        rs = slice(c * CHUNK, (c + 1) * CHUNK)
        sl = slice(hd * HEAD_DIM, (hd + 1) * HEAD_DIM)
        return _mlstm_local(a_ref[rs, sl], kt_ref[sl, rs], arow[c][hd], blast[c][hd], causal)

    def mixer_readout(c, hd, local):
        rs = slice(c * CHUNK, (c + 1) * CHUNK)
        sl = slice(hd * HEAD_DIM, (hd + 1) * HEAD_DIM)
        if c == 0:
            vsn_all = jnp.concatenate(
                [a_ref[cc * CHUNK:(cc + 1) * CHUNK, n_copy + hd * HEAD_DIM: n_copy + (hd + 1) * HEAD_DIM]
                 for cc in range(n_chunks)], axis=1)
            sg_all[hd] = _dot(ws_ref[hd], vsn_all)
        cta, m_prev = state[hd]
        b_rep = jnp.broadcast_to(bc[c][:, HEADS + hd: HEADS + hd + 1], (CHUNK, HEAD_DIM))
        ml, cta, m_prev = _mlstm_readout(
            local, a_ref[rs, sl], a_ref[rs, GROUP_WIDTH + hd * HEAD_DIM: GROUP_WIDTH + (hd + 1) * HEAD_DIM],
            b_ref[rs, sl], gh_ref[:, sl], blast[c][hd], b_rep, cta, m_prev, ones_blk)
        state[hd] = (cta, m_prev)
        mix_ref[rs, sl] = ml.astype(mix_ref.dtype)
        sg = sg_all[hd][:, c * CHUNK:(c + 1) * CHUNK] + bs_ref[hd]
        ug = b_ref[rs, GROUP_WIDTH + hd * HEAD_DIM: GROUP_WIDTH + (hd + 1) * HEAD_DIM]
        mix_ref[rs, GROUP_WIDTH + hd * HEAD_DIM: GROUP_WIDTH + (hd + 1) * HEAD_DIM] = (ug * sg).astype(mix_ref.dtype)

    pieces = [(c, hd) for c in range(n_chunks) for hd in range(HEADS)]
    per_ff = -(-len(pieces) // len(FF_CHUNKS))
    ahead = [mixer_local(*pieces[p]) for p in range(MIXER_LOOKAHEAD)]
    acc = [None] * len(blocks)
    f0 = 0
    for j, width in enumerate(FF_CHUNKS):
        gate_up = [(_dot(h, wgu_ref[:, f0:f0 + width]), _dot(h, wgu_ref[:, D_FF + f0:D_FF + f0 + width]))
                   for h in h2]
        for idx in range(j * per_ff, min((j + 1) * per_ff, len(pieces))):
            if idx + MIXER_LOOKAHEAD < len(pieces):
                ahead.append(mixer_local(*pieces[idx + MIXER_LOOKAHEAD]))
            mixer_readout(*pieces[idx], ahead.pop(0))
        for r, (gate, up) in enumerate(gate_up):
            act = (gate * _sigmoid(gate) * up).astype(BF16)
            part = _dot(act, wdn_ref[f0:f0 + width, :])
            acc[r] = part if acc[r] is None else acc[r] + part
            if j + 1 == len(FF_CHUNKS):
                x2 = x1[r] + gt2_ref[...] * acc[r]
                y_ref[blocks[r], :] = _rms(x2) * gf_ref[...]
        f0 += width

    for hd in range(HEADS):
        cta_ref[hd] = state[hd][0]
        m_ref[hd] = jnp.broadcast_to(state[hd][1], m_ref.shape[1:])

    @pl.when(jnp.logical_and(step % tiles_per_seq == tiles_per_seq - 1, step < n_tiles))
    def _():
        for hd in range(HEADS):
            cta = cta_ref[hd]
            c_out[hd] = cta[:, :HEAD_DIM].T
            n_out[hd: hd + 1, :] = cta[:, HEAD_DIM:].T[0:1, :]
            m_out[hd: hd + 1, :] = m_ref[hd][0:1, :]


def _prompt_tail(a, kt, b, gates, x, mod, bias_row, g_head, ws_tril, bs_rep, w_out, g2, w_gu, w_dn, g_final):
    groups, t, n_a = a.shape
    tm = MIX_TILE
    tps = t // tm
    n_tiles = groups * tps
    assert sum(FF_CHUNKS) == D_FF

    def cur(i):
        return jnp.minimum(i, n_tiles - 1)

    def prev(i):
        return jnp.maximum(i - 1, 0)

    def rows(tile, width):
        return pl.BlockSpec((None, tm, width), lambda i: (tile(i) // tps, tile(i) % tps, 0))

    def mod_piece(piece):
        return pl.BlockSpec((None, 1, D_MODEL), lambda i: (prev(i) // tps, 0, piece))

    def const(shape):
        nd = len(shape)
        return pl.BlockSpec(shape, lambda i: (0,) * nd, pipeline_mode=pl.Buffered(1))

    def per_seq(shape):
        nd = len(shape)
        return pl.BlockSpec((None,) + shape, lambda i: (cur(i) // tps,) + (0,) * nd)

    return pl.pallas_call(
        functools.partial(_prompt_tail_kernel, tiles_per_seq=tps),
        grid=(n_tiles + 1,),
        in_specs=[
            rows(cur, n_a),
            pl.BlockSpec((None, GROUP_WIDTH, tm), lambda i: (cur(i) // tps, 0, cur(i) % tps)),
            rows(cur, N_B),
            rows(cur, LANES),
            rows(prev, D_MODEL),
            mod_piece(2), mod_piece(3), mod_piece(4), mod_piece(5),
            const((1, LANES)),
            const((1, GROUP_WIDTH)),
            const((HEADS, CHUNK, CHUNK)),
            const((HEADS, CHUNK, LANES)),
            const((D_MODEL, D_MODEL)),
            const((1, D_MODEL)),
            const((D_MODEL, 2 * D_FF)),
            const((D_FF, D_MODEL)),
            const((1, D_MODEL)),
        ],
        out_specs=[
            rows(prev, D_MODEL),
            per_seq((HEADS, HEAD_DIM, HEAD_DIM)),
            per_seq((HEADS, HEAD_DIM)),
            per_seq((HEADS, LANES)),
        ],
        out_shape=[
            jax.ShapeDtypeStruct((groups, t, D_MODEL), F32),
            jax.ShapeDtypeStruct((groups, HEADS, HEAD_DIM, HEAD_DIM), F32),
            jax.ShapeDtypeStruct((groups, HEADS, HEAD_DIM), F32),
            jax.ShapeDtypeStruct((groups, HEADS, LANES), F32),
        ],
        scratch_shapes=[
            pltpu.VMEM((tm, D_MODEL), BF16),
            pltpu.VMEM((HEADS, HEAD_DIM, 2 * HEAD_DIM), F32),
            pltpu.VMEM((HEADS, 8, LANES), F32),
        ],
        compiler_params=pltpu.CompilerParams(
            dimension_semantics=("arbitrary",), vmem_limit_bytes=VMEM_LIMIT),
        name="prompt_tail",
    )(a, kt, b, gates, x, mod, mod, mod, mod, bias_row, g_head, ws_tril, bs_rep, w_out, g2, w_gu, w_dn, g_final)


SAMPLE_TOKENS_PER_STEP = 32
SAMPLE_UNROLL = 8


def _mix_sample_kernel(q_ref, k_ref, v_ref, vsn_ref, og_ref, ug_ref, g_ref, m0_ref, n0_ref, c_ref,
                       bias_ref, gh_ref, ws0_ref, bs0_ref,
                       mix_ref, c_out, n_out, m_out,
                       vt_ref, kp_ref, wd_ref, cqt_ref):
    hd = pl.program_id(0)
    grp = pl.program_id(1)
    nb = q_ref.shape[0]
    lane = lax.broadcasted_iota(jnp.int32, (nb, LANES), 1)

    def gate_terms():
        pre = g_ref[...] + bias_ref[...]
        i_pre = jnp.sum(jnp.where(lane == hd, pre, 0.0), axis=1, keepdims=True)
        f_pre = jnp.sum(jnp.where(lane == hd + HEADS, pre, 0.0), axis=1, keepdims=True)
        m_prev = jnp.sum(jnp.where(lane == hd, m0_ref[...], 0.0), axis=1, keepdims=True)
        inter = _log_sigmoid(f_pre) + m_prev
        m_t = jnp.maximum(inter, i_pre)
        return m_t, jnp.exp(i_pre - m_t), jnp.exp(inter - m_t)

    @pl.when(grp == 0)
    def _():
        _, w_in, w_dec = gate_terms()
        vt_ref[...] = v_ref[...].T
        kp_ref[...] = (w_in * k_ref[...]).astype(BF16)
        wd_ref[...] = jnp.broadcast_to(w_dec, wd_ref.shape)
        cqt_ref[...] = jnp.zeros_like(cqt_ref)

    lane_sq = lax.broadcasted_iota(jnp.int32, (HEAD_DIM, LANES), 1)
    tokens = c_ref.shape[0]

    def body(r, cqt):
        tok = grp * tokens + r
        q_row = q_ref[pl.ds(tok, 1), :]
        c_b = c_ref[r]
        sel = lane_sq == tok
        cq_col = jnp.sum(c_b * q_row, axis=1, keepdims=True)
        v_sel = jnp.where(sel, vt_ref[...], 0.0).astype(BF16)
        outer = _dot(v_sel, kp_ref[...])
        c_out[r] = wd_ref[pl.ds(tok, 1), :] * c_b + outer
        return jnp.where(sel, cq_col, cqt)

    cqt_ref[...] = lax.fori_loop(0, tokens, body, cqt_ref[...], unroll=SAMPLE_UNROLL)

    @pl.when(grp == pl.num_programs(1) - 1)
    def _():
        m_t, w_in, w_dec = gate_terms()
        q = q_ref[...]
        k = k_ref[...]
        v = v_ref[...]
        n0 = n0_ref[...]
        cq = cqt_ref[...].T
        s = jnp.sum(q * k, axis=1, keepdims=True) * (QK_SCALE * w_in)
        w_inter = w_dec * QK_SCALE
        num = s * v + w_inter * cq
        den = s + w_inter * jnp.sum(n0 * q, axis=1, keepdims=True)
        hh = num / jnp.maximum(jnp.abs(den), jnp.exp(-m_t))
        ml = _rms(hh) * gh_ref[...] * og_ref[...]
        cm = ug_ref[...] * (ws0_ref[...] * vsn_ref[...] + bs0_ref[...])
        n_out[...] = w_dec * n0 + w_in * k
        m_out[...] = jnp.broadcast_to(m_t, m_out.shape)
        for hh_static in range(HEADS):
            @pl.when(hd == hh_static)
            def _():
                mix_ref[:, hh_static * HEAD_DIM: (hh_static + 1) * HEAD_DIM] = ml
                mix_ref[:, GROUP_WIDTH + hh_static * HEAD_DIM: GROUP_WIDTH + (hh_static + 1) * HEAD_DIM] = cm


def _mix_sample(a, k, b, gates, m0_pad, n0, c0, bias_row, g_head, ws0_row, bs0_row):
    nb = a.shape[0]
    tb = SAMPLE_TOKENS_PER_STEP

    def head_block(offset):
        return pl.BlockSpec((nb, HEAD_DIM), lambda h, g: (0, offset + h))

    full = pl.BlockSpec((nb, LANES), lambda h, g: (0, 0))
    head_row = pl.BlockSpec((1, HEAD_DIM), lambda h, g: (0, h))
    c_spec = pl.BlockSpec((tb, None, HEAD_DIM, HEAD_DIM), lambda h, g: (g, h, 0, 0))
    return pl.pallas_call(
        _mix_sample_kernel,
        grid=(HEADS, nb // tb),
        in_specs=[
            head_block(0), head_block(0), head_block(HEADS), head_block(2 * HEADS),
            head_block(0), head_block(HEADS),
            full, full, head_block(0), c_spec,
            pl.BlockSpec((1, LANES), lambda h, g: (0, 0)),
            head_row, head_row, head_row,
        ],
        out_specs=[
            pl.BlockSpec((nb, D_MODEL), lambda h, g: (0, 0)),
            c_spec,
            head_block(0),
            head_block(0),
        ],
        out_shape=[
            jax.ShapeDtypeStruct((nb, D_MODEL), F32),
            jax.ShapeDtypeStruct(c0.shape, F32),
            jax.ShapeDtypeStruct((nb, GROUP_WIDTH), F32),
            jax.ShapeDtypeStruct((nb, GROUP_WIDTH), F32),
        ],
        scratch_shapes=[
            pltpu.VMEM((HEAD_DIM, nb), F32),
            pltpu.VMEM((nb, HEAD_DIM), BF16),
            pltpu.VMEM((nb, LANES), F32),
            pltpu.VMEM((HEAD_DIM, nb), F32),
        ],
        compiler_params=pltpu.CompilerParams(
            dimension_semantics=("arbitrary", "arbitrary"), vmem_limit_bytes=VMEM_LIMIT),
        name="mix_sample",
    )(a, k, a, a, b, b, gates, m0_pad, n0, c0, bias_row, g_head, ws0_row, bs0_row)


OUT_ROW_BLOCKS = 2


def _out_kernel(x_ref, mix_ref, gt1_ref, sh2_ref, sc2_ref, gt2_ref, wo_ref, g2_ref, wgu_ref, wdn_ref, gf_ref,
                y_ref):
    tm = x_ref.shape[0]
    rows_per_block = tm // OUT_ROW_BLOCKS if tm % (OUT_ROW_BLOCKS * 128) == 0 else tm
    per_row = gt1_ref.shape[0] != 1
    blocks = []
    for r0 in range(0, tm, rows_per_block):
        rs = slice(r0, r0 + rows_per_block)
        ms = rs if per_row else slice(0, 1)
        x1 = x_ref[rs, :] + gt1_ref[ms, :] * _dot(mix_ref[rs, :].astype(BF16), wo_ref[...])
        h2 = _rms(x1) * g2_ref[...]
        h2 = (h2 * (1.0 + sc2_ref[ms, :]) + sh2_ref[ms, :]).astype(BF16)
        blocks.append((rs, ms, x1, h2))
    for rs, ms, x1, h2 in blocks:
        gate = _dot(h2, wgu_ref[:, :D_FF])
        up = _dot(h2, wgu_ref[:, D_FF:])
        act = (gate * _sigmoid(gate) * up).astype(BF16)
        x2 = x1 + gt2_ref[ms, :] * _dot(act, wdn_ref[...])
        y_ref[rs, :] = _rms(x2) * gf_ref[...]


def _output_stage(x, mix, mod, w_out, g2, w_gu, w_dn, g_final, *, tm, per_row):
    groups, t, _ = x.shape
    return pl.pallas_call(
        _out_kernel,
        grid=(groups, t // tm),
        in_specs=[
            pl.BlockSpec((None, tm, D_MODEL), lambda g, i: (g, i, 0)),
            pl.BlockSpec((None, tm, D_MODEL), lambda g, i: (g, i, 0)),
            _mod_spec(2, tm, per_row),
            _mod_spec(3, tm, per_row),
            _mod_spec(4, tm, per_row),
            _mod_spec(5, tm, per_row),
            _const_spec((D_MODEL, D_MODEL)),
            _const_spec((1, D_MODEL)),
            _const_spec((D_MODEL, 2 * D_FF)),
            _const_spec((D_FF, D_MODEL)),
            _const_spec((1, D_MODEL)),
        ],
        out_specs=pl.BlockSpec((None, tm, D_MODEL), lambda g, i: (g, i, 0)),
        out_shape=jax.ShapeDtypeStruct((groups, t, D_MODEL), F32),
        compiler_params=pltpu.CompilerParams(
            dimension_semantics=("arbitrary", "arbitrary"), vmem_limit_bytes=VMEM_LIMIT),
        name="output_stage",
    )(x, mix, mod, mod, mod, mod, w_out, g2, w_gu, w_dn, g_final)


def kernel(x_prompt, x_sample, c_prompt, c_sample, state_mlstm_C, state_mlstm_n, state_mlstm_m, w_ada, b_ada, g_norm1, w_in, b_gate, g_mlstm_head, ln_v_g, ln_v_b, w_s, b_s, w_out, g_norm2, w_gate_up, w_down, g_final):
    depth = w_ada.shape[0]
    assert depth == 1, "single-layer trunk"
    batch, seq, _ = x_prompt.shape
    nb = x_sample.shape[0]
    assert x_sample.shape[1] == 1

    wi = w_in[0]
    gw = GROUP_WIDTH
    o_q, o_k, o_v, o_o, o_i = 0, gw, 2 * gw, 3 * gw, 4 * gw
    o_u = o_i + 2 * HEADS
    o_vs = o_u + gw
    w_proj = jnp.concatenate(
        [wi[:, o_q:o_k], wi[:, o_v:o_o], wi[:, o_vs:o_vs + gw], wi[:, o_o:o_i], wi[:, o_u:o_vs],
         jnp.pad(wi[:, o_i:o_u], ((0, 0), (0, LANES - 2 * HEADS)))], axis=1).astype(BF16)
    w_k = wi[:, o_k:o_v].astype(BF16)
    w_kt = w_k.T
    w_out_b = w_out[0].astype(BF16)
    w_gu_b = w_gate_up[0].astype(BF16)
    w_dn_b = w_down[0].astype(BF16)
    g1 = g_norm1[0].reshape(1, D_MODEL)
    g2 = g_norm2[0].reshape(1, D_MODEL)
    gf = g_final.reshape(1, D_MODEL)
    ln_g = ln_v_g[0].reshape(1, gw)
    ln_b = ln_v_b[0].reshape(1, gw)
    g_head = g_mlstm_head[0].reshape(1, gw)
    bias_row = jnp.pad(b_gate[0], (0, LANES - 2 * HEADS)).reshape(1, LANES)
    tril = jnp.tril(jnp.ones((CHUNK, CHUNK), dtype=bool))
    ws_tril = jnp.where(tril[None], w_s[0], 0.0).astype(BF16)
    bs_rep = jnp.broadcast_to(b_s[0][:, :, None], (HEADS, CHUNK, LANES))
    ws0_row = jnp.repeat(w_s[0][:, 0, 0], HEAD_DIM).reshape(1, gw)
    bs0_row = jnp.repeat(b_s[0][:, 0], HEAD_DIM).reshape(1, gw)

    rows = batch + nb
    rows_pad = -(-rows // 16) * 16
    c_all = jnp.concatenate([c_prompt, c_sample, jnp.zeros((rows_pad - rows, D_MODEL), F32)], axis=0)
    mod = _modulation(c_all, w_ada[0], b_ada[0].reshape(1, -1))
    mod_p = mod[:batch].reshape(batch, 1, N_MOD * D_MODEL)
    mod_s = mod[batch:rows].reshape(1, nb, N_MOD * D_MODEL)

    a_p, b_p, g_p, kt_p = _project(x_prompt, mod_p, g1, w_proj, ln_g, ln_b, w_kt, k_transposed=True,
                                   tm=MIX_TILE, per_row=False, a_dtype=BF16)
    y_p, c_p, n_p, m_p = _prompt_tail(a_p, kt_p, b_p, g_p, x_prompt, mod_p, bias_row, g_head, ws_tril, bs_rep,
                                      w_out_b, g2, w_gu_b, w_dn_b, gf)

    xs = x_sample.reshape(1, nb, D_MODEL)
    a_s, b_s_act, g_s, k_s = _project(xs, mod_s, g1, w_proj, ln_g, ln_b, w_k, k_transposed=False,
                                      tm=nb, per_row=True, a_dtype=F32)
    a_s2, b_s2, g_s2 = a_s[0], b_s_act[0], g_s[0]
    m0_pad = jnp.pad(state_mlstm_m[0], ((0, 0), (0, LANES - HEADS)))
    n0 = state_mlstm_n[0].reshape(nb, gw)
    mix_s, c_s, n_s, m_s = _mix_sample(a_s2, k_s[0], b_s2, g_s2, m0_pad, n0, state_mlstm_C[0],
                                       bias_row, g_head, ws0_row, bs0_row)
    y_s = _output_stage(xs, mix_s.reshape(1, nb, D_MODEL), mod_s, w_out_b, g2, w_gu_b, w_dn_b, gf,
                        tm=nb, per_row=True)

    return (
        y_p,
        y_s.reshape(nb, 1, D_MODEL),
        c_p[None],
        n_p[None],
        m_p[:, :, 0][None],
        c_s[None],
        n_s.reshape(nb, HEADS, HEAD_DIM)[None],
        m_s.reshape(nb, HEADS, HEAD_DIM)[:, :, 0][None],
        a_s2[:, 2 * gw:].reshape(nb, 1, HEADS, HEAD_DIM)[None],
    )
```

```python
import functools
import math

import jax
import jax.numpy as jnp
from jax import lax
from jax.experimental import pallas as pl
from jax.experimental.pallas import tpu as pltpu

F32 = jnp.float32
BF16 = jnp.bfloat16

D_MODEL = 1024
HEADS = 4
HEAD_DIM = 128
GROUP_WIDTH = HEADS * HEAD_DIM
CHUNK = 128
D_FF = 2816
N_MOD = 6
EPS = 1e-6
QK_SCALE = HEAD_DIM ** -0.5
LANES = 128

N_B = 2 * GROUP_WIDTH

VMEM_LIMIT = 56 * 1024 * 1024


def _dot(a, b):
    return jnp.dot(a, b, preferred_element_type=F32)


def _dot_nt(a, b):
    return lax.dot_general(a, b, (((1,), (1,)), ((), ())), preferred_element_type=F32)


def _sigmoid(x):
    return 1.0 / (1.0 + jnp.exp(-x))


def _gelu_tanh(x):
    c = math.sqrt(2.0 / math.pi)
    return x * (0.5 * (1.0 + jnp.tanh(c * (x + 0.044715 * (x * x * x)))))


def _log_sigmoid(x):
    return jnp.minimum(x, 0.0) - jnp.log1p(jnp.exp(-jnp.abs(x)))


def _rms(x):
    return x * lax.rsqrt(jnp.mean(x * x, axis=-1, keepdims=True) + EPS)


def _split3_bf16(x):
    hi = x.astype(BF16)
    r1 = x - hi.astype(F32)
    mid = r1.astype(BF16)
    lo = (r1 - mid.astype(F32)).astype(BF16)
    return hi, mid, lo


MOD_TILE = 512


def _mod_kernel(c_ref, w_ref, b_ref, o_ref):
    c = c_ref[...]
    a = (c * _sigmoid(c)).astype(BF16)
    o_ref[...] = _dot(a, w_ref[...].astype(BF16)) + b_ref[...]


def _modulation(c_all, w_ada, b_ada):
    rows = c_all.shape[0]
    tn = MOD_TILE
    return pl.pallas_call(
        _mod_kernel,
        grid=(N_MOD * D_MODEL // tn,),
        in_specs=[
            pl.BlockSpec((rows, D_MODEL), lambda j: (0, 0)),
            pl.BlockSpec((D_MODEL, tn), lambda j: (0, j)),
            pl.BlockSpec((1, tn), lambda j: (0, j)),
        ],
        out_specs=pl.BlockSpec((rows, tn), lambda j: (0, j)),
        out_shape=jax.ShapeDtypeStruct((rows, N_MOD * D_MODEL), F32),
        compiler_params=pltpu.CompilerParams(dimension_semantics=("arbitrary",)),
        name="modulation",
    )(c_all, w_ada, b_ada)


def _mod_spec(piece, tm, per_row):
    if per_row:
        return pl.BlockSpec((None, tm, D_MODEL), lambda g, t: (g, t, piece))
    return pl.BlockSpec((None, 1, D_MODEL), lambda g, t: (g, 0, piece))


def _const_spec(shape):
    nd = len(shape)
    return pl.BlockSpec(shape, lambda g, t: (0,) * nd, pipeline_mode=pl.Buffered(1))


PROJ_TILE = 1024


def _proj_body(x_ref, sh_ref, sc_ref, g1_ref, w_ref, lng_ref, lnb_ref, a_ref, b_ref, g_ref):
    n_a = a_ref.shape[1]
    n_copy = n_a - GROUP_WIDTH
    x = x_ref[...]
    h = _rms(x) * g1_ref[...]
    h = (h * (1.0 + sc_ref[...]) + sh_ref[...]).astype(BF16)
    p = _dot(h, w_ref[...])
    a_ref[:, :n_copy] = p[:, :n_copy].astype(a_ref.dtype)
    vs = _gelu_tanh(p[:, n_copy:n_a])
    for hd in range(HEADS):
        sl = slice(hd * HEAD_DIM, (hd + 1) * HEAD_DIM)
        v = vs[:, sl]
        mu = jnp.mean(v, axis=-1, keepdims=True)
        vc = v - mu
        var = jnp.mean(vc * vc, axis=-1, keepdims=True)
        y = vc * lax.rsqrt(var + EPS) * lng_ref[:, sl] + lnb_ref[:, sl]
        a_ref[:, n_copy + hd * HEAD_DIM: n_copy + (hd + 1) * HEAD_DIM] = y.astype(a_ref.dtype)
    b_ref[:, :GROUP_WIDTH] = _sigmoid(p[:, n_a: n_a + GROUP_WIDTH])
    b_ref[:, GROUP_WIDTH:] = _gelu_tanh(p[:, n_a + GROUP_WIDTH: n_a + N_B])
    g_ref[...] = p[:, n_a + N_B:]
    return h


def _proj_kernel(*refs, k_transposed, n_cast):
    ins, outs = refs[:8 + n_cast], refs[8 + n_cast:]
    wk_ref, k_ref = ins[7], outs[3]
    h = _proj_body(*ins[:7], *outs[:3])
    if k_transposed:
        k_ref[...] = _dot_nt(wk_ref[...], h).astype(k_ref.dtype)
    else:
        k_ref[...] = _dot(h, wk_ref[...]).astype(k_ref.dtype)
    for src, dst in zip(ins[8:], outs[4:]):
        dst[...] = src[...].astype(dst.dtype)


def _project(x, mod, g1, w_proj, ln_g, ln_b, w_k, cast_weights=(), *, k_transposed, tm, per_row, a_dtype):
    groups, t, _ = x.shape
    n_proj = w_proj.shape[1]
    n_a = n_proj - N_B - LANES
    steps_per_group = t // tm
    n_steps = groups * steps_per_group
    in_specs = [
        pl.BlockSpec((None, tm, D_MODEL), lambda g, i: (g, i, 0)),
        _mod_spec(0, tm, per_row),
        _mod_spec(1, tm, per_row),
        _const_spec((1, D_MODEL)),
        _const_spec((D_MODEL, n_proj)),
        _const_spec((1, GROUP_WIDTH)),
        _const_spec((1, GROUP_WIDTH)),
    ]
    out_specs = [
        pl.BlockSpec((None, tm, n_a), lambda g, i: (g, i, 0)),
        pl.BlockSpec((None, tm, N_B), lambda g, i: (g, i, 0)),
        pl.BlockSpec((None, tm, LANES), lambda g, i: (g, i, 0)),
    ]
    out_shape = [
        jax.ShapeDtypeStruct((groups, t, n_a), a_dtype),
        jax.ShapeDtypeStruct((groups, t, N_B), F32),
        jax.ShapeDtypeStruct((groups, t, LANES), F32),
    ]
    args = [x, mod, mod, g1, w_proj, ln_g, ln_b, w_k]
    in_specs.append(_const_spec(w_k.shape))
    if k_transposed:
        out_specs.append(pl.BlockSpec((None, GROUP_WIDTH, tm), lambda g, i: (g, 0, i)))
        out_shape.append(jax.ShapeDtypeStruct((groups, GROUP_WIDTH, t), a_dtype))
    else:
        out_specs.append(pl.BlockSpec((None, tm, GROUP_WIDTH), lambda g, i: (g, i, 0)))
        out_shape.append(jax.ShapeDtypeStruct((groups, t, GROUP_WIDTH), a_dtype))
    for w in cast_weights:
        rows, cols = w.shape
        assert rows % (n_steps * 16) == 0, "row block must be a whole number of bf16 sublane tiles"
        spec = pl.BlockSpec((rows // n_steps, cols), lambda g, i: (g * steps_per_group + i, 0))
        in_specs.append(spec)
        out_specs.append(spec)
        out_shape.append(jax.ShapeDtypeStruct((rows, cols), BF16))
        args.append(w)
    return pl.pallas_call(
        functools.partial(_proj_kernel, k_transposed=k_transposed, n_cast=len(cast_weights)),
        grid=(groups, t // tm),
        in_specs=in_specs,
        out_specs=out_specs,
        out_shape=out_shape,
        compiler_params=pltpu.CompilerParams(
            dimension_semantics=("arbitrary", "arbitrary"), vmem_limit_bytes=VMEM_LIMIT),
        name="project",
    )(*args)


MIX_TILE = 512


def _dot3_rhs(lhs_bf16, rhs_f32):
    hi, mid, lo = _split3_bf16(rhs_f32)
    return _dot(lhs_bf16, hi) + _dot(lhs_bf16, mid) + _dot(lhs_bf16, lo)


def _dot3_lhs(lhs_f32, rhs_bf16):
    hi, mid, lo = _split3_bf16(lhs_f32)
    return _dot(hi, rhs_bf16) + _dot(mid, rhs_bf16) + _dot(lo, rhs_bf16)


FF_CHUNKS = (768, 768, 768, 512)
MIXER_LOOKAHEAD = 2


def _gate_rows(g_ref, bias_ref, tril, triu, n_chunks):
    pre = [g_ref[c * CHUNK:(c + 1) * CHUNK, :] + bias_ref[...] for c in range(n_chunks)]
    bc_all = _dot3_rhs(tril, jnp.concatenate([_log_sigmoid(p) for p in pre], axis=1))
    rows_all = jnp.concatenate([p.T[0:2 * HEADS, :] for p in pre], axis=0)
    b_rows_all = _dot3_lhs(_log_sigmoid(rows_all), triu)
    bc, arow, blast = [], [], []
    for c in range(n_chunks):
        bc.append(bc_all[:, c * LANES:(c + 1) * LANES])
        rows = rows_all[c * 2 * HEADS:(c + 1) * 2 * HEADS, :]
        b_rows = b_rows_all[c * 2 * HEADS:(c + 1) * 2 * HEADS, :]
        arow.append([rows[hd:hd + 1, :] - b_rows[HEADS + hd:HEADS + hd + 1, :] for hd in range(HEADS)])
        blast.append([jnp.min(b_rows[HEADS + hd:HEADS + hd + 1, :], axis=1, keepdims=True) for hd in range(HEADS)])
    return bc, arow, blast


def _mlstm_local(q, kt, a_row, b_last, causal):
    amat = jnp.where(causal, a_row, -jnp.inf)
    m_row = jnp.max(amat, axis=1, keepdims=True)
    s_loc = (_dot(q, kt) * jnp.exp(amat - m_row)).astype(BF16)
    g_row = b_last + a_row
    g_loc = jnp.max(g_row, axis=1, keepdims=True)
    kw = (kt.astype(F32) * jnp.exp(g_row - g_loc)).astype(BF16)
    return m_row, s_loc, g_loc, kw


def _mlstm_readout(local, q, v, og, g_head, b_last, b_rep, cta, m_prev, ones_blk):
    m_row, s_loc, g_loc, kw = local
    va = jnp.concatenate([v, ones_blk], axis=1)
    nd_loc = _dot(s_loc, va)
    u_aug = _dot(kw, va)
    inter = _dot(q, cta.astype(BF16))
    mm = jnp.maximum(m_prev, m_row)
    f_loc = jnp.exp(m_row - mm) * QK_SCALE
    f_int = jnp.exp(m_prev - mm) * QK_SCALE
    nd = f_loc * nd_loc + f_int * inter
    clamp = jnp.exp(-(b_rep + mm))
    hh = nd[:, :HEAD_DIM] / jnp.maximum(jnp.abs(nd[:, HEAD_DIM:]), clamp)
    ml = _rms(hh) * g_head * og

    dec = b_last + m_prev
    m_new = jnp.maximum(dec, g_loc)
    cta_new = jnp.exp(dec - m_new) * cta + jnp.exp(g_loc - m_new) * u_aug
    return ml, cta_new, m_new


def _prompt_tail_kernel(a_ref, kt_ref, b_ref, g_ref, x_ref, gt1_ref, sh2_ref, sc2_ref, gt2_ref,
                        bias_ref, gh_ref, ws_ref, bs_ref, wo_ref, g2_ref, wgu_ref, wdn_ref, gf_ref,
                        y_ref, c_out, n_out, m_out,
                        mix_ref, cta_ref, m_ref, *, tiles_per_seq):
    step = pl.program_id(0)
    n_tiles = pl.num_programs(0) - 1
    tm = a_ref.shape[0]
    n_chunks = tm // CHUNK
    n_copy = a_ref.shape[1] - GROUP_WIDTH

    @pl.when(step == 0)
    def _():
        mix_ref[...] = jnp.zeros_like(mix_ref)

    @pl.when(step % tiles_per_seq == 0)
    def _():
        cta_ref[...] = jnp.zeros_like(cta_ref)
        m_ref[...] = jnp.zeros_like(m_ref)

    row = lax.broadcasted_iota(jnp.int32, (CHUNK, CHUNK), 0)
    col = lax.broadcasted_iota(jnp.int32, (CHUNK, CHUNK), 1)
    causal = row >= col
    tril = jnp.where(causal, 1.0, 0.0).astype(BF16)
    triu = jnp.where(row <= col, 1.0, 0.0).astype(BF16)
    ones_blk = jnp.ones((CHUNK, HEAD_DIM), BF16)

    blocks = [slice(r0, r0 + tm // OUT_ROW_BLOCKS) for r0 in range(0, tm, tm // OUT_ROW_BLOCKS)]
    outproj = [_dot(mix_ref[rs, :], wo_ref[...]) for rs in blocks]
    x1, h2 = [], []
    for rs, op in zip(blocks, outproj):
        x1.append(x_ref[rs, :] + gt1_ref[...] * op)
        h = _rms(x1[-1]) * g2_ref[...]
        h2.append((h * (1.0 + sc2_ref[...]) + sh2_ref[...]).astype(BF16))

    bc, arow, blast = _gate_rows(g_ref, bias_ref, tril, triu, n_chunks)
    state = [(cta_ref[hd], jnp.max(m_ref[hd][0:1, :], axis=1, keepdims=True)) for hd in range(HEADS)]
    sg_all = [None] * HEADS

    def mixer_local(c, hd):
        rs = slice(c * CHUNK, (c + 1) * CHUNK)
        sl = slice(hd * HEAD_DIM, (hd + 1) * HEAD_DIM)
        return _mlstm_local(a_ref[rs, sl], kt_ref[sl, rs], arow[c][hd], blast[c][hd], causal)

    def mixer_readout(c, hd, local):
        rs = slice(c * CHUNK, (c + 1) * CHUNK)
        sl = slice(hd * HEAD_DIM, (hd + 1) * HEAD_DIM)
        if c == 0:
            vsn_all = jnp.concatenate(
                [a_ref[cc * CHUNK:(cc + 1) * CHUNK, n_copy + hd * HEAD_DIM: n_copy + (hd + 1) * HEAD_DIM]
                 for cc in range(n_chunks)], axis=1)
            sg_all[hd] = _dot(ws_ref[hd], vsn_all)
        cta, m_prev = state[hd]
        b_rep = jnp.broadcast_to(bc[c][:, HEADS + hd: HEADS + hd + 1], (CHUNK, HEAD_DIM))
        ml, cta, m_prev = _mlstm_readout(
            local, a_ref[rs, sl], a_ref[rs, GROUP_WIDTH + hd * HEAD_DIM: GROUP_WIDTH + (hd + 1) * HEAD_DIM],
            b_ref[rs, sl], gh_ref[:, sl], blast[c][hd], b_rep, cta, m_prev, ones_blk)
        state[hd] = (cta, m_prev)
        mix_ref[rs, sl] = ml.astype(mix_ref.dtype)
        sg = sg_all[hd][:, c * CHUNK:(c + 1) * CHUNK] + bs_ref[hd]
        ug = b_ref[rs, GROUP_WIDTH + hd * HEAD_DIM: GROUP_WIDTH + (hd + 1) * HEAD_DIM]
        mix_ref[rs, GROUP_WIDTH + hd * HEAD_DIM: GROUP_WIDTH + (hd + 1) * HEAD_DIM] = (ug * sg).astype(mix_ref.dtype)

    pieces = [(c, hd) for c in range(n_chunks) for hd in range(HEADS)]
    per_ff = -(-len(pieces) // len(FF_CHUNKS))
    ahead = [mixer_local(*pieces[p]) for p in range(MIXER_LOOKAHEAD)]
    acc = [None] * len(blocks)
    f0 = 0
    for j, width in enumerate(FF_CHUNKS):
        gate_up = [(_dot(h, wgu_ref[:, f0:f0 + width]), _dot(h, wgu_ref[:, D_FF + f0:D_FF + f0 + width]))
                   for h in h2]
        for idx in range(j * per_ff, min((j + 1) * per_ff, len(pieces))):
            if idx + MIXER_LOOKAHEAD < len(pieces):
                ahead.append(mixer_local(*pieces[idx + MIXER_LOOKAHEAD]))
            mixer_readout(*pieces[idx], ahead.pop(0))
        for r, (gate, up) in enumerate(gate_up):
            act = (gate * _sigmoid(gate) * up).astype(BF16)
            part = _dot(act, wdn_ref[f0:f0 + width, :])
            acc[r] = part if acc[r] is None else acc[r] + part
            if j + 1 == len(FF_CHUNKS):
                x2 = x1[r] + gt2_ref[...] * acc[r]
                y_ref[blocks[r], :] = _rms(x2) * gf_ref[...]
        f0 += width

    for hd in range(HEADS):
        cta_ref[hd] = state[hd][0]
        m_ref[hd] = jnp.broadcast_to(state[hd][1], m_ref.shape[1:])

    @pl.when(jnp.logical_and(step % tiles_per_seq == tiles_per_seq - 1, step < n_tiles))
    def _():
        for hd in range(HEADS):
            cta = cta_ref[hd]
            c_out[hd] = cta[:, :HEAD_DIM].T
            n_out[hd: hd + 1, :] = cta[:, HEAD_DIM:].T[0:1, :]
            m_out[hd: hd + 1, :] = m_ref[hd][0:1, :]


def _prompt_tail(a, kt, b, gates, x, mod, bias_row, g_head, ws_tril, bs_rep, w_out, g2, w_gu, w_dn, g_final):
    groups, t, n_a = a.shape
    tm = MIX_TILE
    tps = t // tm
    n_tiles = groups * tps
    assert sum(FF_CHUNKS) == D_FF

    def cur(i):
        return jnp.minimum(i, n_tiles - 1)

    def prev(i):
        return jnp.maximum(i - 1, 0)

    def rows(tile, width):
        return pl.BlockSpec((None, tm, width), lambda i: (tile(i) // tps, tile(i) % tps, 0))

    def mod_piece(piece):
        return pl.BlockSpec((None, 1, D_MODEL), lambda i: (prev(i) // tps, 0, piece))

    def const(shape):
        nd = len(shape)
        return pl.BlockSpec(shape, lambda i: (0,) * nd, pipeline_mode=pl.Buffered(1))

    def per_seq(shape):
        nd = len(shape)
        return pl.BlockSpec((None,) + shape, lambda i: (cur(i) // tps,) + (0,) * nd)

    return pl.pallas_call(
        functools.partial(_prompt_tail_kernel, tiles_per_seq=tps),
        grid=(n_tiles + 1,),
        in_specs=[
            rows(cur, n_a),
            pl.BlockSpec((None, GROUP_WIDTH, tm), lambda i: (cur(i) // tps, 0, cur(i) % tps)),
            rows(cur, N_B),
            rows(cur, LANES),
            rows(prev, D_MODEL),
            mod_piece(2), mod_piece(3), mod_piece(4), mod_piece(5),
            const((1, LANES)),
            const((1, GROUP_WIDTH)),
            const((HEADS, CHUNK, CHUNK)),
            const((HEADS, CHUNK, LANES)),
            const((D_MODEL, D_MODEL)),
            const((1, D_MODEL)),
            const((D_MODEL, 2 * D_FF)),
            const((D_FF, D_MODEL)),
            const((1, D_MODEL)),
        ],
        out_specs=[
            rows(prev, D_MODEL),
            per_seq((HEADS, HEAD_DIM, HEAD_DIM)),
            per_seq((HEADS, HEAD_DIM)),
            per_seq((HEADS, LANES)),
        ],
        out_shape=[
            jax.ShapeDtypeStruct((groups, t, D_MODEL), F32),
            jax.ShapeDtypeStruct((groups, HEADS, HEAD_DIM, HEAD_DIM), F32),
            jax.ShapeDtypeStruct((groups, HEADS, HEAD_DIM), F32),
            jax.ShapeDtypeStruct((groups, HEADS, LANES), F32),
        ],
        scratch_shapes=[
            pltpu.VMEM((tm, D_MODEL), BF16),
            pltpu.VMEM((HEADS, HEAD_DIM, 2 * HEAD_DIM), F32),
            pltpu.VMEM((HEADS, 8, LANES), F32),
        ],
        compiler_params=pltpu.CompilerParams(
            dimension_semantics=("arbitrary",), vmem_limit_bytes=VMEM_LIMIT),
        name="prompt_tail",
    )(a, kt, b, gates, x, mod, mod, mod, mod, bias_row, g_head, ws_tril, bs_rep, w_out, g2, w_gu, w_dn, g_final)


SAMPLE_TOKENS_PER_STEP = 64
SAMPLE_UNROLL = 8


def _mix_sample_kernel(q_ref, k_ref, v_ref, vsn_ref, og_ref, ug_ref, g_ref, m0_ref, n0_ref, c_ref,
                       bias_ref, gh_ref, ws0_ref, bs0_ref,
                       mix_ref, c_out, n_out, m_out,
                       vt_ref, kp_ref, wd_ref, cqt_ref):
    hd = pl.program_id(0)
    grp = pl.program_id(1)
    nb = q_ref.shape[0]
    lane = lax.broadcasted_iota(jnp.int32, (nb, LANES), 1)

    def gate_terms():
        pre = g_ref[...] + bias_ref[...]
        i_pre = jnp.sum(jnp.where(lane == hd, pre, 0.0), axis=1, keepdims=True)
        f_pre = jnp.sum(jnp.where(lane == hd + HEADS, pre, 0.0), axis=1, keepdims=True)
        m_prev = jnp.sum(jnp.where(lane == hd, m0_ref[...], 0.0), axis=1, keepdims=True)
        inter = _log_sigmoid(f_pre) + m_prev
        m_t = jnp.maximum(inter, i_pre)
        return m_t, jnp.exp(i_pre - m_t), jnp.exp(inter - m_t)

    @pl.when(grp == 0)
    def _():
        _, w_in, w_dec = gate_terms()
        vt_ref[...] = v_ref[...].T
        kp_ref[...] = (w_in * k_ref[...]).astype(BF16)
        wd_ref[...] = jnp.broadcast_to(w_dec, wd_ref.shape)
        cqt_ref[...] = jnp.zeros_like(cqt_ref)

    lane_sq = lax.broadcasted_iota(jnp.int32, (HEAD_DIM, LANES), 1)
    tokens = c_ref.shape[0]

    def body(r, cqt):
        tok = grp * tokens + r
        q_row = q_ref[pl.ds(tok, 1), :]
        c_b = c_ref[r]
        sel = lane_sq == tok
        cq_col = jnp.sum(c_b * q_row, axis=1, keepdims=True)
        v_sel = jnp.where(sel, vt_ref[...], 0.0).astype(BF16)
        outer = _dot(v_sel, kp_ref[...])
        c_out[r] = wd_ref[pl.ds(tok, 1), :] * c_b + outer
        return jnp.where(sel, cq_col, cqt)

    cqt_ref[...] = lax.fori_loop(0, tokens, body, cqt_ref[...], unroll=SAMPLE_UNROLL)

    @pl.when(grp == pl.num_programs(1) - 1)
    def _():
        m_t, w_in, w_dec = gate_terms()
        q = q_ref[...]
        k = k_ref[...]
        v = v_ref[...]
        n0 = n0_ref[...]
        cq = cqt_ref[...].T
        s = jnp.sum(q * k, axis=1, keepdims=True) * (QK_SCALE * w_in)
        w_inter = w_dec * QK_SCALE
        num = s * v + w_inter * cq
        den = s + w_inter * jnp.sum(n0 * q, axis=1, keepdims=True)
        hh = num / jnp.maximum(jnp.abs(den), jnp.exp(-m_t))
        ml = _rms(hh) * gh_ref[...] * og_ref[...]
        cm = ug_ref[...] * (ws0_ref[...] * vsn_ref[...] + bs0_ref[...])
        n_out[...] = w_dec * n0 + w_in * k
        m_out[...] = jnp.broadcast_to(m_t, m_out.shape)
        for hh_static in range(HEADS):
            @pl.when(hd == hh_static)
            def _():
                mix_ref[:, hh_static * HEAD_DIM: (hh_static + 1) * HEAD_DIM] = ml
                mix_ref[:, GROUP_WIDTH + hh_static * HEAD_DIM: GROUP_WIDTH + (hh_static + 1) * HEAD_DIM] = cm


def _mix_sample(a, k, b, gates, m0_pad, n0, c0, bias_row, g_head, ws0_row, bs0_row):
    nb = a.shape[0]
    tb = SAMPLE_TOKENS_PER_STEP

    def head_block(offset):
        return pl.BlockSpec((nb, HEAD_DIM), lambda h, g: (0, offset + h))

    full = pl.BlockSpec((nb, LANES), lambda h, g: (0, 0))
    head_row = pl.BlockSpec((1, HEAD_DIM), lambda h, g: (0, h))
    c_spec = pl.BlockSpec((tb, None, HEAD_DIM, HEAD_DIM), lambda h, g: (g, h, 0, 0))
    return pl.pallas_call(
        _mix_sample_kernel,
        grid=(HEADS, nb // tb),
        in_specs=[
            head_block(0), head_block(0), head_block(HEADS), head_block(2 * HEADS),
            head_block(0), head_block(HEADS),
            full, full, head_block(0), c_spec,
            pl.BlockSpec((1, LANES), lambda h, g: (0, 0)),
            head_row, head_row, head_row,
        ],
        out_specs=[
            pl.BlockSpec((nb, D_MODEL), lambda h, g: (0, 0)),
            c_spec,
            head_block(0),
            head_block(0),
        ],
        out_shape=[
            jax.ShapeDtypeStruct((nb, D_MODEL), F32),
            jax.ShapeDtypeStruct(c0.shape, F32),
            jax.ShapeDtypeStruct((nb, GROUP_WIDTH), F32),
            jax.ShapeDtypeStruct((nb, GROUP_WIDTH), F32),
        ],
        scratch_shapes=[
            pltpu.VMEM((HEAD_DIM, nb), F32),
            pltpu.VMEM((nb, HEAD_DIM), BF16),
            pltpu.VMEM((nb, LANES), F32),
            pltpu.VMEM((HEAD_DIM, nb), F32),
        ],
        compiler_params=pltpu.CompilerParams(
            dimension_semantics=("arbitrary", "arbitrary"), vmem_limit_bytes=VMEM_LIMIT),
        name="mix_sample",
    )(a, k, a, a, b, b, gates, m0_pad, n0, c0, bias_row, g_head, ws0_row, bs0_row)


OUT_ROW_BLOCKS = 2


def _out_kernel(x_ref, mix_ref, gt1_ref, sh2_ref, sc2_ref, gt2_ref, wo_ref, g2_ref, wgu_ref, wdn_ref, gf_ref,
                y_ref):
    tm = x_ref.shape[0]
    rows_per_block = tm // OUT_ROW_BLOCKS if tm % (OUT_ROW_BLOCKS * 128) == 0 else tm
    per_row = gt1_ref.shape[0] != 1
    blocks = []
    for r0 in range(0, tm, rows_per_block):
        rs = slice(r0, r0 + rows_per_block)
        ms = rs if per_row else slice(0, 1)
        x1 = x_ref[rs, :] + gt1_ref[ms, :] * _dot(mix_ref[rs, :].astype(BF16), wo_ref[...])
        h2 = _rms(x1) * g2_ref[...]
        h2 = (h2 * (1.0 + sc2_ref[ms, :]) + sh2_ref[ms, :]).astype(BF16)
        blocks.append((rs, ms, x1, h2))
    for rs, ms, x1, h2 in blocks:
        gate = _dot(h2, wgu_ref[:, :D_FF])
        up = _dot(h2, wgu_ref[:, D_FF:])
        act = (gate * _sigmoid(gate) * up).astype(BF16)
        x2 = x1 + gt2_ref[ms, :] * _dot(act, wdn_ref[...])
        y_ref[rs, :] = _rms(x2) * gf_ref[...]


def _output_stage(x, mix, mod, w_out, g2, w_gu, w_dn, g_final, *, tm, per_row):
    groups, t, _ = x.shape
    return pl.pallas_call(
        _out_kernel,
        grid=(groups, t // tm),
        in_specs=[
            pl.BlockSpec((None, tm, D_MODEL), lambda g, i: (g, i, 0)),
            pl.BlockSpec((None, tm, D_MODEL), lambda g, i: (g, i, 0)),
            _mod_spec(2, tm, per_row),
            _mod_spec(3, tm, per_row),
            _mod_spec(4, tm, per_row),
            _mod_spec(5, tm, per_row),
            _const_spec((D_MODEL, D_MODEL)),
            _const_spec((1, D_MODEL)),
            _const_spec((D_MODEL, 2 * D_FF)),
            _const_spec((D_FF, D_MODEL)),
            _const_spec((1, D_MODEL)),
        ],
        out_specs=pl.BlockSpec((None, tm, D_MODEL), lambda g, i: (g, i, 0)),
        out_shape=jax.ShapeDtypeStruct((groups, t, D_MODEL), F32),
        compiler_params=pltpu.CompilerParams(
            dimension_semantics=("arbitrary", "arbitrary"), vmem_limit_bytes=VMEM_LIMIT),
        name="output_stage",
    )(x, mix, mod, mod, mod, mod, w_out, g2, w_gu, w_dn, g_final)


def kernel(x_prompt, x_sample, c_prompt, c_sample, state_mlstm_C, state_mlstm_n, state_mlstm_m, w_ada, b_ada, g_norm1, w_in, b_gate, g_mlstm_head, ln_v_g, ln_v_b, w_s, b_s, w_out, g_norm2, w_gate_up, w_down, g_final):
    depth = w_ada.shape[0]
    assert depth == 1, "single-layer trunk"
    batch, seq, _ = x_prompt.shape
    nb = x_sample.shape[0]
    assert x_sample.shape[1] == 1

    wi = w_in[0]
    gw = GROUP_WIDTH
    o_q, o_k, o_v, o_o, o_i = 0, gw, 2 * gw, 3 * gw, 4 * gw
    o_u = o_i + 2 * HEADS
    o_vs = o_u + gw
    w_proj = jnp.concatenate(
        [wi[:, o_q:o_k], wi[:, o_v:o_o], wi[:, o_vs:o_vs + gw], wi[:, o_o:o_i], wi[:, o_u:o_vs],
         jnp.pad(wi[:, o_i:o_u], ((0, 0), (0, LANES - 2 * HEADS)))], axis=1).astype(BF16)
    w_k = wi[:, o_k:o_v].astype(BF16)
    w_kt = w_k.T
    g1 = g_norm1[0].reshape(1, D_MODEL)
    g2 = g_norm2[0].reshape(1, D_MODEL)
    gf = g_final.reshape(1, D_MODEL)
    ln_g = ln_v_g[0].reshape(1, gw)
    ln_b = ln_v_b[0].reshape(1, gw)
    g_head = g_mlstm_head[0].reshape(1, gw)
    bias_row = jnp.pad(b_gate[0], (0, LANES - 2 * HEADS)).reshape(1, LANES)
    tril = jnp.tril(jnp.ones((CHUNK, CHUNK), dtype=bool))
    ws_tril = jnp.where(tril[None], w_s[0], 0.0).astype(BF16)
    bs_rep = jnp.broadcast_to(b_s[0][:, :, None], (HEADS, CHUNK, LANES))
    ws0_row = jnp.repeat(w_s[0][:, 0, 0], HEAD_DIM).reshape(1, gw)
    bs0_row = jnp.repeat(b_s[0][:, 0], HEAD_DIM).reshape(1, gw)

    rows = batch + nb
    rows_pad = -(-rows // 16) * 16
    c_all = jnp.concatenate([c_prompt, c_sample, jnp.zeros((rows_pad - rows, D_MODEL), F32)], axis=0)
    mod = _modulation(c_all, w_ada[0], b_ada[0].reshape(1, -1))
    mod_p = mod[:batch].reshape(batch, 1, N_MOD * D_MODEL)
    mod_s = mod[batch:rows].reshape(1, nb, N_MOD * D_MODEL)

    a_p, b_p, g_p, kt_p, w_out_b, w_gu_b, w_dn_b = _project(
        x_prompt, mod_p, g1, w_proj, ln_g, ln_b, w_kt, (w_out[0], w_gate_up[0], w_down[0]),
        k_transposed=True, tm=PROJ_TILE, per_row=False, a_dtype=BF16)
    y_p, c_p, n_p, m_p = _prompt_tail(a_p, kt_p, b_p, g_p, x_prompt, mod_p, bias_row, g_head, ws_tril, bs_rep,
                                      w_out_b, g2, w_gu_b, w_dn_b, gf)

    xs = x_sample.reshape(1, nb, D_MODEL)
    a_s, b_s_act, g_s, k_s = _project(xs, mod_s, g1, w_proj, ln_g, ln_b, w_k, k_transposed=False,
                                      tm=nb, per_row=True, a_dtype=F32)
    a_s2, b_s2, g_s2 = a_s[0], b_s_act[0], g_s[0]
    m0_pad = jnp.pad(state_mlstm_m[0], ((0, 0), (0, LANES - HEADS)))
    n0 = state_mlstm_n[0].reshape(nb, gw)
    mix_s, c_s, n_s, m_s = _mix_sample(a_s2, k_s[0], b_s2, g_s2, m0_pad, n0, state_mlstm_C[0],
                                       bias_row, g_head, ws0_row, bs0_row)
    y_s = _output_stage(xs, mix_s.reshape(1, nb, D_MODEL), mod_s, w_out_b, g2, w_gu_b, w_dn_b, gf,
                        tm=nb, per_row=True)

    return (
        y_p,
        y_s.reshape(nb, 1, D_MODEL),
        c_p[None],
        n_p[None],
        m_p[:, :, 0][None],
        c_s[None],
        n_s.reshape(nb, HEADS, HEAD_DIM)[None],
        m_s.reshape(nb, HEADS, HEAD_DIM)[:, :, 0][None],
        a_s2[:, 2 * gw:].reshape(nb, 1, HEADS, HEAD_DIM)[None],
    )
```

```python
import functools
import math

import jax
import jax.numpy as jnp
from jax import lax
from jax.experimental import pallas as pl
from jax.experimental.pallas import tpu as pltpu

F32 = jnp.float32
BF16 = jnp.bfloat16

D_MODEL = 1024
HEADS = 4
HEAD_DIM = 128
GROUP_WIDTH = HEADS * HEAD_DIM
CHUNK = 128
D_FF = 2816
N_MOD = 6
EPS = 1e-6
QK_SCALE = HEAD_DIM ** -0.5
LANES = 128

N_B = 2 * GROUP_WIDTH

VMEM_LIMIT = 56 * 1024 * 1024


def _dot(a, b):
    return jnp.dot(a, b, preferred_element_type=F32)


def _dot_nt(a, b):
    return lax.dot_general(a, b, (((1,), (1,)), ((), ())), preferred_element_type=F32)


def _sigmoid(x):
    return 1.0 / (1.0 + jnp.exp(-x))


def _gelu_tanh(x):
    c = math.sqrt(2.0 / math.pi)
    return x * (0.5 * (1.0 + jnp.tanh(c * (x + 0.044715 * (x * x * x)))))


def _log_sigmoid(x):
    return jnp.minimum(x, 0.0) - jnp.log1p(jnp.exp(-jnp.abs(x)))


def _rms(x):
    return x * lax.rsqrt(jnp.mean(x * x, axis=-1, keepdims=True) + EPS)


def _split3_bf16(x):
    hi = x.astype(BF16)
    r1 = x - hi.astype(F32)
    mid = r1.astype(BF16)
    lo = (r1 - mid.astype(F32)).astype(BF16)
    return hi, mid, lo


MOD_TILE = 768

IN_Q, IN_K, IN_V, IN_O, IN_GATES = 0, GROUP_WIDTH, 2 * GROUP_WIDTH, 3 * GROUP_WIDTH, 4 * GROUP_WIDTH
IN_U = IN_GATES + 2 * HEADS
IN_VS = IN_U + GROUP_WIDTH
N_IN = IN_VS + GROUP_WIDTH
N_PROJ = 5 * GROUP_WIDTH + LANES


def _mod_kernel(c_ref, w_ref, b_ref, win_ref, o_ref, wproj_ref, wk_ref, wkt_ref):
    c = c_ref[...]
    a = (c * _sigmoid(c)).astype(BF16)
    o_ref[...] = _dot(a, w_ref[...].astype(BF16)) + b_ref[...]

    w = win_ref[...]
    gw = GROUP_WIDTH
    for dst, src in enumerate((IN_Q, IN_V, IN_VS, IN_O, IN_U)):
        wproj_ref[:, dst * gw:(dst + 1) * gw] = w[:, src:src + gw].astype(BF16)
    lane = lax.broadcasted_iota(jnp.int32, (w.shape[0], LANES), 1)
    gates = jnp.where(lane < 2 * HEADS, w[:, IN_GATES:IN_GATES + LANES], 0.0)
    wproj_ref[:, 5 * gw:] = gates.astype(BF16)
    wk = w[:, IN_K:IN_K + gw]
    wk_ref[...] = wk.astype(BF16)
    wkt_ref[...] = wk.T.astype(BF16)


def _modulation(c_all, w_ada, b_ada, w_in):
    rows = c_all.shape[0]
    tn = MOD_TILE
    steps = N_MOD * D_MODEL // tn
    rb = D_MODEL // steps
    assert w_in.shape == (D_MODEL, N_IN) and rb % LANES == 0
    return pl.pallas_call(
        _mod_kernel,
        grid=(steps,),
        in_specs=[
            pl.BlockSpec((rows, D_MODEL), lambda j: (0, 0)),
            pl.BlockSpec((D_MODEL, tn), lambda j: (0, j)),
            pl.BlockSpec((1, tn), lambda j: (0, j)),
            pl.BlockSpec((rb, N_IN), lambda j: (j, 0)),
        ],
        out_specs=[
            pl.BlockSpec((rows, tn), lambda j: (0, j)),
            pl.BlockSpec((rb, N_PROJ), lambda j: (j, 0)),
            pl.BlockSpec((rb, GROUP_WIDTH), lambda j: (j, 0)),
            pl.BlockSpec((GROUP_WIDTH, rb), lambda j: (0, j)),
        ],
        out_shape=[
            jax.ShapeDtypeStruct((rows, N_MOD * D_MODEL), F32),
            jax.ShapeDtypeStruct((D_MODEL, N_PROJ), BF16),
            jax.ShapeDtypeStruct((D_MODEL, GROUP_WIDTH), BF16),
            jax.ShapeDtypeStruct((GROUP_WIDTH, D_MODEL), BF16),
        ],
        compiler_params=pltpu.CompilerParams(dimension_semantics=("arbitrary",)),
        name="modulation",
    )(c_all, w_ada, b_ada, w_in)


def _mod_spec(piece, tm, per_row):
    if per_row:
        return pl.BlockSpec((None, tm, D_MODEL), lambda g, t: (g, t, piece))
    return pl.BlockSpec((None, 1, D_MODEL), lambda g, t: (g, 0, piece))


def _const_spec(shape):
    nd = len(shape)
    return pl.BlockSpec(shape, lambda g, t: (0,) * nd, pipeline_mode=pl.Buffered(1))


PROJ_TILE = 1024


def _proj_body(x_ref, sh_ref, sc_ref, g1_ref, w_ref, lng_ref, lnb_ref, a_ref, b_ref, g_ref):
    n_a = a_ref.shape[1]
    n_copy = n_a - GROUP_WIDTH
    x = x_ref[...]
    h = _rms(x) * g1_ref[...]
    h = (h * (1.0 + sc_ref[...]) + sh_ref[...]).astype(BF16)
    p = _dot(h, w_ref[...])
    a_ref[:, :n_copy] = p[:, :n_copy].astype(a_ref.dtype)
    vs = _gelu_tanh(p[:, n_copy:n_a])
    for hd in range(HEADS):
        sl = slice(hd * HEAD_DIM, (hd + 1) * HEAD_DIM)
        v = vs[:, sl]
        mu = jnp.mean(v, axis=-1, keepdims=True)
        vc = v - mu
        var = jnp.mean(vc * vc, axis=-1, keepdims=True)
        y = vc * lax.rsqrt(var + EPS) * lng_ref[:, sl] + lnb_ref[:, sl]
        a_ref[:, n_copy + hd * HEAD_DIM: n_copy + (hd + 1) * HEAD_DIM] = y.astype(a_ref.dtype)
    b_ref[:, :GROUP_WIDTH] = _sigmoid(p[:, n_a: n_a + GROUP_WIDTH])
    b_ref[:, GROUP_WIDTH:] = _gelu_tanh(p[:, n_a + GROUP_WIDTH: n_a + N_B])
    g_ref[...] = p[:, n_a + N_B:]
    return h


def _proj_kernel(*refs, k_transposed, n_cast):
    ins, outs = refs[:8 + n_cast], refs[8 + n_cast:]
    wk_ref, k_ref = ins[7], outs[3]
    h = _proj_body(*ins[:7], *outs[:3])
    if k_transposed:
        k_ref[...] = _dot_nt(wk_ref[...], h).astype(k_ref.dtype)
    else:
        k_ref[...] = _dot(h, wk_ref[...]).astype(k_ref.dtype)
    for src, dst in zip(ins[8:], outs[4:]):
        dst[...] = src[...].astype(dst.dtype)


def _project(x, mod, g1, w_proj, ln_g, ln_b, w_k, cast_weights=(), *, k_transposed, tm, per_row, a_dtype):
    groups, t, _ = x.shape
    n_proj = w_proj.shape[1]
    n_a = n_proj - N_B - LANES
    steps_per_group = t // tm
    n_steps = groups * steps_per_group
    in_specs = [
        pl.BlockSpec((None, tm, D_MODEL), lambda g, i: (g, i, 0)),
        _mod_spec(0, tm, per_row),
        _mod_spec(1, tm, per_row),
        _const_spec((1, D_MODEL)),
        _const_spec((D_MODEL, n_proj)),
        _const_spec((1, GROUP_WIDTH)),
        _const_spec((1, GROUP_WIDTH)),
    ]
    out_specs = [
        pl.BlockSpec((None, tm, n_a), lambda g, i: (g, i, 0)),
        pl.BlockSpec((None, tm, N_B), lambda g, i: (g, i, 0)),
        pl.BlockSpec((None, tm, LANES), lambda g, i: (g, i, 0)),
    ]
    out_shape = [
        jax.ShapeDtypeStruct((groups, t, n_a), a_dtype),
        jax.ShapeDtypeStruct((groups, t, N_B), F32),
        jax.ShapeDtypeStruct((groups, t, LANES), F32),
    ]
    args = [x, mod, mod, g1, w_proj, ln_g, ln_b, w_k]
    in_specs.append(_const_spec(w_k.shape))
    if k_transposed:
        out_specs.append(pl.BlockSpec((None, GROUP_WIDTH, tm), lambda g, i: (g, 0, i)))
        out_shape.append(jax.ShapeDtypeStruct((groups, GROUP_WIDTH, t), a_dtype))
    else:
        out_specs.append(pl.BlockSpec((None, tm, GROUP_WIDTH), lambda g, i: (g, i, 0)))
        out_shape.append(jax.ShapeDtypeStruct((groups, t, GROUP_WIDTH), a_dtype))
    for w in cast_weights:
        rows, cols = w.shape
        assert rows % (n_steps * 16) == 0, "row block must be a whole number of bf16 sublane tiles"
        spec = pl.BlockSpec((rows // n_steps, cols), lambda g, i: (g * steps_per_group + i, 0))
        in_specs.append(spec)
        out_specs.append(spec)
        out_shape.append(jax.ShapeDtypeStruct((rows, cols), BF16))
        args.append(w)
    return pl.pallas_call(
        functools.partial(_proj_kernel, k_transposed=k_transposed, n_cast=len(cast_weights)),
        grid=(groups, t // tm),
        in_specs=in_specs,
        out_specs=out_specs,
        out_shape=out_shape,
        compiler_params=pltpu.CompilerParams(
            dimension_semantics=("arbitrary", "arbitrary"), vmem_limit_bytes=VMEM_LIMIT),
        name="project",
    )(*args)


MIX_TILE = 512


def _dot3_rhs(lhs_bf16, rhs_f32):
    hi, mid, lo = _split3_bf16(rhs_f32)
    return _dot(lhs_bf16, hi) + _dot(lhs_bf16, mid) + _dot(lhs_bf16, lo)


def _dot3_lhs(lhs_f32, rhs_bf16):
    hi, mid, lo = _split3_bf16(lhs_f32)
    return _dot(hi, rhs_bf16) + _dot(mid, rhs_bf16) + _dot(lo, rhs_bf16)


FF_CHUNKS = (768, 768, 768, 512)
MIXER_LOOKAHEAD = 2


def _gate_rows(g_ref, bias_ref, tril, triu, n_chunks):
    pre = [g_ref[c * CHUNK:(c + 1) * CHUNK, :] + bias_ref[...] for c in range(n_chunks)]
    bc_all = _dot3_rhs(tril, jnp.concatenate([_log_sigmoid(p) for p in pre], axis=1))
    rows_all = jnp.concatenate([p.T[0:2 * HEADS, :] for p in pre], axis=0)
    b_rows_all = _dot3_lhs(_log_sigmoid(rows_all), triu)
    bc, arow, blast = [], [], []
    for c in range(n_chunks):
        bc.append(bc_all[:, c * LANES:(c + 1) * LANES])
        rows = rows_all[c * 2 * HEADS:(c + 1) * 2 * HEADS, :]
        b_rows = b_rows_all[c * 2 * HEADS:(c + 1) * 2 * HEADS, :]
        arow.append([rows[hd:hd + 1, :] - b_rows[HEADS + hd:HEADS + hd + 1, :] for hd in range(HEADS)])
        blast.append([jnp.min(b_rows[HEADS + hd:HEADS + hd + 1, :], axis=1, keepdims=True) for hd in range(HEADS)])
    return bc, arow, blast


def _mlstm_local(q, kt, a_row, b_last, causal):
    amat = jnp.where(causal, a_row, -jnp.inf)
    m_row = jnp.max(amat, axis=1, keepdims=True)
    s_loc = (_dot(q, kt) * jnp.exp(amat - m_row)).astype(BF16)
    g_row = b_last + a_row
    g_loc = jnp.max(g_row, axis=1, keepdims=True)
    kw = (kt.astype(F32) * jnp.exp(g_row - g_loc)).astype(BF16)
    return m_row, s_loc, g_loc, kw


def _mlstm_readout(local, q, v, og, g_head, b_last, b_rep, cta, m_prev, ones_blk):
    m_row, s_loc, g_loc, kw = local
    va = jnp.concatenate([v, ones_blk], axis=1)
    nd_loc = _dot(s_loc, va)
    u_aug = _dot(kw, va)
    inter = _dot(q, cta.astype(BF16))
    mm = jnp.maximum(m_prev, m_row)
    f_loc = jnp.exp(m_row - mm) * QK_SCALE
    f_int = jnp.exp(m_prev - mm) * QK_SCALE
    nd = f_loc * nd_loc + f_int * inter
    clamp = jnp.exp(-(b_rep + mm))
    hh = nd[:, :HEAD_DIM] / jnp.maximum(jnp.abs(nd[:, HEAD_DIM:]), clamp)
    ml = _rms(hh) * g_head * og

    dec = b_last + m_prev
    m_new = jnp.maximum(dec, g_loc)
    cta_new = jnp.exp(dec - m_new) * cta + jnp.exp(g_loc - m_new) * u_aug
    return ml, cta_new, m_new


def _prompt_tail_kernel(a_ref, kt_ref, b_ref, g_ref, x_ref, gt1_ref, sh2_ref, sc2_ref, gt2_ref,
                        bias_ref, gh_ref, ws_ref, bs_ref, wo_ref, g2_ref, wgu_ref, wdn_ref, gf_ref,
                        y_ref, c_out, n_out, m_out,
                        mix_ref, cta_ref, m_ref, *, tiles_per_seq):
    step = pl.program_id(0)
    n_tiles = pl.num_programs(0) - 1
    tm = a_ref.shape[0]
    n_chunks = tm // CHUNK
    n_copy = a_ref.shape[1] - GROUP_WIDTH

    @pl.when(step == 0)
    def _():
        mix_ref[...] = jnp.zeros_like(mix_ref)

    @pl.when(step % tiles_per_seq == 0)
    def _():
        cta_ref[...] = jnp.zeros_like(cta_ref)
        m_ref[...] = jnp.zeros_like(m_ref)

    row = lax.broadcasted_iota(jnp.int32, (CHUNK, CHUNK), 0)
    col = lax.broadcasted_iota(jnp.int32, (CHUNK, CHUNK), 1)
    causal = row >= col
    tril = jnp.where(causal, 1.0, 0.0).astype(BF16)
    triu = jnp.where(row <= col, 1.0, 0.0).astype(BF16)
    ones_blk = jnp.ones((CHUNK, HEAD_DIM), BF16)

    blocks = [slice(r0, r0 + tm // OUT_ROW_BLOCKS) for r0 in range(0, tm, tm // OUT_ROW_BLOCKS)]
    outproj = [_dot(mix_ref[rs, :], wo_ref[...]) for rs in blocks]
    x1, h2 = [], []
    for rs, op in zip(blocks, outproj):
        x1.append(x_ref[rs, :] + gt1_ref[...] * op)
        h = _rms(x1[-1]) * g2_ref[...]
        h2.append((h * (1.0 + sc2_ref[...]) + sh2_ref[...]).astype(BF16))

    bc, arow, blast = _gate_rows(g_ref, bias_ref, tril, triu, n_chunks)
    state = [(cta_ref[hd], jnp.max(m_ref[hd][0:1, :], axis=1, keepdims=True)) for hd in range(HEADS)]
    sg_all = [None] * HEADS

    def mixer_local(c, hd):
        rs = slice(c * CHUNK, (c + 1) * CHUNK)
        sl = slice(hd * HEAD_DIM, (hd + 1) * HEAD_DIM)
        return _mlstm_local(a_ref[rs, sl], kt_ref[sl, rs], arow[c][hd], blast[c][hd], causal)

    def mixer_readout(c, hd, local):
        rs = slice(c * CHUNK, (c + 1) * CHUNK)
        sl = slice(hd * HEAD_DIM, (hd + 1) * HEAD_DIM)
        if c == 0:
            vsn_all = jnp.concatenate(
                [a_ref[cc * CHUNK:(cc + 1) * CHUNK, n_copy + hd * HEAD_DIM: n_copy + (hd + 1) * HEAD_DIM]
                 for cc in range(n_chunks)], axis=1)
            sg_all[hd] = _dot(ws_ref[hd], vsn_all)
        cta, m_prev = state[hd]
        b_rep = jnp.broadcast_to(bc[c][:, HEADS + hd: HEADS + hd + 1], (CHUNK, HEAD_DIM))
        ml, cta, m_prev = _mlstm_readout(
            local, a_ref[rs, sl], a_ref[rs, GROUP_WIDTH + hd * HEAD_DIM: GROUP_WIDTH + (hd + 1) * HEAD_DIM],
            b_ref[rs, sl], gh_ref[:, sl], blast[c][hd], b_rep, cta, m_prev, ones_blk)
        state[hd] = (cta, m_prev)
        mix_ref[rs, sl] = ml.astype(mix_ref.dtype)
        sg = sg_all[hd][:, c * CHUNK:(c + 1) * CHUNK] + bs_ref[hd]
        ug = b_ref[rs, GROUP_WIDTH + hd * HEAD_DIM: GROUP_WIDTH + (hd + 1) * HEAD_DIM]
        mix_ref[rs, GROUP_WIDTH + hd * HEAD_DIM: GROUP_WIDTH + (hd + 1) * HEAD_DIM] = (ug * sg).astype(mix_ref.dtype)

    pieces = [(c, hd) for c in range(n_chunks) for hd in range(HEADS)]
    per_ff = -(-len(pieces) // len(FF_CHUNKS))
    ahead = [mixer_local(*pieces[p]) for p in range(MIXER_LOOKAHEAD)]
    acc = [None] * len(blocks)
    f0 = 0
    for j, width in enumerate(FF_CHUNKS):
        gate_up = [(_dot(h, wgu_ref[:, f0:f0 + width]), _dot(h, wgu_ref[:, D_FF + f0:D_FF + f0 + width]))
                   for h in h2]
        for idx in range(j * per_ff, min((j + 1) * per_ff, len(pieces))):
            if idx + MIXER_LOOKAHEAD < len(pieces):
                ahead.append(mixer_local(*pieces[idx + MIXER_LOOKAHEAD]))
            mixer_readout(*pieces[idx], ahead.pop(0))
        for r, (gate, up) in enumerate(gate_up):
            act = (gate * _sigmoid(gate) * up).astype(BF16)
            part = _dot(act, wdn_ref[f0:f0 + width, :])
            acc[r] = part if acc[r] is None else acc[r] + part
            if j + 1 == len(FF_CHUNKS):
                x2 = x1[r] + gt2_ref[...] * acc[r]
                y_ref[blocks[r], :] = _rms(x2) * gf_ref[...]
        f0 += width

    for hd in range(HEADS):
        cta_ref[hd] = state[hd][0]
        m_ref[hd] = jnp.broadcast_to(state[hd][1], m_ref.shape[1:])

    @pl.when(jnp.logical_and(step % tiles_per_seq == tiles_per_seq - 1, step < n_tiles))
    def _():
        for hd in range(HEADS):
            cta = cta_ref[hd]
            c_out[hd] = cta[:, :HEAD_DIM].T
            n_out[hd: hd + 1, :] = cta[:, HEAD_DIM:].T[0:1, :]
            m_out[hd: hd + 1, :] = m_ref[hd][0:1, :]


def _prompt_tail(a, kt, b, gates, x, mod, bias_row, g_head, ws_tril, bs_rep, w_out, g2, w_gu, w_dn, g_final):
    groups, t, n_a = a.shape
    tm = MIX_TILE
    tps = t // tm
    n_tiles = groups * tps
    assert sum(FF_CHUNKS) == D_FF

    def cur(i):
        return jnp.minimum(i, n_tiles - 1)

    def prev(i):
        return jnp.maximum(i - 1, 0)

    def rows(tile, width):
        return pl.BlockSpec((None, tm, width), lambda i: (tile(i) // tps, tile(i) % tps, 0))

    def mod_piece(piece):
        return pl.BlockSpec((None, 1, D_MODEL), lambda i: (prev(i) // tps, 0, piece))

    def const(shape):
        nd = len(shape)
        return pl.BlockSpec(shape, lambda i: (0,) * nd, pipeline_mode=pl.Buffered(1))

    def per_seq(shape):
        nd = len(shape)
        return pl.BlockSpec((None,) + shape, lambda i: (cur(i) // tps,) + (0,) * nd)

    return pl.pallas_call(
        functools.partial(_prompt_tail_kernel, tiles_per_seq=tps),
        grid=(n_tiles + 1,),
        in_specs=[
            rows(cur, n_a),
            pl.BlockSpec((None, GROUP_WIDTH, tm), lambda i: (cur(i) // tps, 0, cur(i) % tps)),
            rows(cur, N_B),
            rows(cur, LANES),
            rows(prev, D_MODEL),
            mod_piece(2), mod_piece(3), mod_piece(4), mod_piece(5),
            const((1, LANES)),
            const((1, GROUP_WIDTH)),
            const((HEADS, CHUNK, CHUNK)),
            const((HEADS, CHUNK, LANES)),
            const((D_MODEL, D_MODEL)),
            const((1, D_MODEL)),
            const((D_MODEL, 2 * D_FF)),
            const((D_FF, D_MODEL)),
            const((1, D_MODEL)),
        ],
        out_specs=[
            rows(prev, D_MODEL),
            per_seq((HEADS, HEAD_DIM, HEAD_DIM)),
            per_seq((HEADS, HEAD_DIM)),
            per_seq((HEADS, LANES)),
        ],
        out_shape=[
            jax.ShapeDtypeStruct((groups, t, D_MODEL), F32),
            jax.ShapeDtypeStruct((groups, HEADS, HEAD_DIM, HEAD_DIM), F32),
            jax.ShapeDtypeStruct((groups, HEADS, HEAD_DIM), F32),
            jax.ShapeDtypeStruct((groups, HEADS, LANES), F32),
        ],
        scratch_shapes=[
            pltpu.VMEM((tm, D_MODEL), BF16),
            pltpu.VMEM((HEADS, HEAD_DIM, 2 * HEAD_DIM), F32),
            pltpu.VMEM((HEADS, 8, LANES), F32),
        ],
        compiler_params=pltpu.CompilerParams(
            dimension_semantics=("arbitrary",), vmem_limit_bytes=VMEM_LIMIT),
        name="prompt_tail",
    )(a, kt, b, gates, x, mod, mod, mod, mod, bias_row, g_head, ws_tril, bs_rep, w_out, g2, w_gu, w_dn, g_final)


SAMPLE_TOKENS_PER_STEP = 64
SAMPLE_UNROLL = 8


def _mix_sample_kernel(q_ref, k_ref, v_ref, vsn_ref, og_ref, ug_ref, g_ref, m0_ref, n0_ref, c_ref,
                       bias_ref, gh_ref, ws0_ref, bs0_ref,
                       mix_ref, c_out, n_out, m_out,
                       vt_ref, kp_ref, wd_ref, cqt_ref):
    hd = pl.program_id(0)
    grp = pl.program_id(1)
    nb = q_ref.shape[0]
    lane = lax.broadcasted_iota(jnp.int32, (nb, LANES), 1)

    def gate_terms():
        pre = g_ref[...] + bias_ref[...]
        i_pre = jnp.sum(jnp.where(lane == hd, pre, 0.0), axis=1, keepdims=True)
        f_pre = jnp.sum(jnp.where(lane == hd + HEADS, pre, 0.0), axis=1, keepdims=True)
        m_prev = jnp.sum(jnp.where(lane == hd, m0_ref[...], 0.0), axis=1, keepdims=True)
        inter = _log_sigmoid(f_pre) + m_prev
        m_t = jnp.maximum(inter, i_pre)
        return m_t, jnp.exp(i_pre - m_t), jnp.exp(inter - m_t)

    @pl.when(grp == 0)
    def _():
        _, w_in, w_dec = gate_terms()
        vt_ref[...] = v_ref[...].T
        kp_ref[...] = (w_in * k_ref[...]).astype(BF16)
        wd_ref[...] = jnp.broadcast_to(w_dec, wd_ref.shape)
        cqt_ref[...] = jnp.zeros_like(cqt_ref)

    lane_sq = lax.broadcasted_iota(jnp.int32, (HEAD_DIM, LANES), 1)
    tokens = c_ref.shape[0]

    def body(r, cqt):
        tok = grp * tokens + r
        q_row = q_ref[pl.ds(tok, 1), :]
        c_b = c_ref[r]
        sel = lane_sq == tok
        cq_col = jnp.sum(c_b * q_row, axis=1, keepdims=True)
        v_sel = jnp.where(sel, vt_ref[...], 0.0).astype(BF16)
        outer = _dot(v_sel, kp_ref[...])
        c_out[r] = wd_ref[pl.ds(tok, 1), :] * c_b + outer
        return jnp.where(sel, cq_col, cqt)

    cqt_ref[...] = lax.fori_loop(0, tokens, body, cqt_ref[...], unroll=SAMPLE_UNROLL)

    @pl.when(grp == pl.num_programs(1) - 1)
    def _():
        m_t, w_in, w_dec = gate_terms()
        q = q_ref[...]
        k = k_ref[...]
        v = v_ref[...]
        n0 = n0_ref[...]
        cq = cqt_ref[...].T
        s = jnp.sum(q * k, axis=1, keepdims=True) * (QK_SCALE * w_in)
        w_inter = w_dec * QK_SCALE
        num = s * v + w_inter * cq
        den = s + w_inter * jnp.sum(n0 * q, axis=1, keepdims=True)
        hh = num / jnp.maximum(jnp.abs(den), jnp.exp(-m_t))
        ml = _rms(hh) * gh_ref[...] * og_ref[...]
        cm = ug_ref[...] * (ws0_ref[...] * vsn_ref[...] + bs0_ref[...])
        n_out[...] = w_dec * n0 + w_in * k
        m_out[...] = jnp.broadcast_to(m_t, m_out.shape)
        for hh_static in range(HEADS):
            @pl.when(hd == hh_static)
            def _():
                mix_ref[:, hh_static * HEAD_DIM: (hh_static + 1) * HEAD_DIM] = ml
                mix_ref[:, GROUP_WIDTH + hh_static * HEAD_DIM: GROUP_WIDTH + (hh_static + 1) * HEAD_DIM] = cm


def _mix_sample(a, k, b, gates, m0_pad, n0, c0, bias_row, g_head, ws0_row, bs0_row):
    nb = a.shape[0]
    tb = SAMPLE_TOKENS_PER_STEP

    def head_block(offset):
        return pl.BlockSpec((nb, HEAD_DIM), lambda h, g: (0, offset + h))

    full = pl.BlockSpec((nb, LANES), lambda h, g: (0, 0))
    head_row = pl.BlockSpec((1, HEAD_DIM), lambda h, g: (0, h))
    c_spec = pl.BlockSpec((tb, None, HEAD_DIM, HEAD_DIM), lambda h, g: (g, h, 0, 0))
    return pl.pallas_call(
        _mix_sample_kernel,
        grid=(HEADS, nb // tb),
        in_specs=[
            head_block(0), head_block(0), head_block(HEADS), head_block(2 * HEADS),
            head_block(0), head_block(HEADS),
            full, full, head_block(0), c_spec,
            pl.BlockSpec((1, LANES), lambda h, g: (0, 0)),
            head_row, head_row, head_row,
        ],
        out_specs=[
            pl.BlockSpec((nb, D_MODEL), lambda h, g: (0, 0)),
            c_spec,
            head_block(0),
            head_block(0),
        ],
        out_shape=[
            jax.ShapeDtypeStruct((nb, D_MODEL), F32),
            jax.ShapeDtypeStruct(c0.shape, F32),
            jax.ShapeDtypeStruct((nb, GROUP_WIDTH), F32),
            jax.ShapeDtypeStruct((nb, GROUP_WIDTH), F32),
        ],
        scratch_shapes=[
            pltpu.VMEM((HEAD_DIM, nb), F32),
            pltpu.VMEM((nb, HEAD_DIM), BF16),
            pltpu.VMEM((nb, LANES), F32),
            pltpu.VMEM((HEAD_DIM, nb), F32),
        ],
        compiler_params=pltpu.CompilerParams(
            dimension_semantics=("arbitrary", "arbitrary"), vmem_limit_bytes=VMEM_LIMIT),
        name="mix_sample",
    )(a, k, a, a, b, b, gates, m0_pad, n0, c0, bias_row, g_head, ws0_row, bs0_row)


OUT_ROW_BLOCKS = 2


def _out_kernel(x_ref, mix_ref, gt1_ref, sh2_ref, sc2_ref, gt2_ref, wo_ref, g2_ref, wgu_ref, wdn_ref, gf_ref,
                y_ref):
    tm = x_ref.shape[0]
    rows_per_block = tm // OUT_ROW_BLOCKS if tm % (OUT_ROW_BLOCKS * 128) == 0 else tm
    per_row = gt1_ref.shape[0] != 1
    blocks = []
    for r0 in range(0, tm, rows_per_block):
        rs = slice(r0, r0 + rows_per_block)
        ms = rs if per_row else slice(0, 1)
        x1 = x_ref[rs, :] + gt1_ref[ms, :] * _dot(mix_ref[rs, :].astype(BF16), wo_ref[...])
        h2 = _rms(x1) * g2_ref[...]
        h2 = (h2 * (1.0 + sc2_ref[ms, :]) + sh2_ref[ms, :]).astype(BF16)
        blocks.append((rs, ms, x1, h2))
    for rs, ms, x1, h2 in blocks:
        gate = _dot(h2, wgu_ref[:, :D_FF])
        up = _dot(h2, wgu_ref[:, D_FF:])
        act = (gate * _sigmoid(gate) * up).astype(BF16)
        x2 = x1 + gt2_ref[ms, :] * _dot(act, wdn_ref[...])
        y_ref[rs, :] = _rms(x2) * gf_ref[...]


def _output_stage(x, mix, mod, w_out, g2, w_gu, w_dn, g_final, *, tm, per_row):
    groups, t, _ = x.shape
    return pl.pallas_call(
        _out_kernel,
        grid=(groups, t // tm),
        in_specs=[
            pl.BlockSpec((None, tm, D_MODEL), lambda g, i: (g, i, 0)),
            pl.BlockSpec((None, tm, D_MODEL), lambda g, i: (g, i, 0)),
            _mod_spec(2, tm, per_row),
            _mod_spec(3, tm, per_row),
            _mod_spec(4, tm, per_row),
            _mod_spec(5, tm, per_row),
            _const_spec((D_MODEL, D_MODEL)),
            _const_spec((1, D_MODEL)),
            _const_spec((D_MODEL, 2 * D_FF)),
            _const_spec((D_FF, D_MODEL)),
            _const_spec((1, D_MODEL)),
        ],
        out_specs=pl.BlockSpec((None, tm, D_MODEL), lambda g, i: (g, i, 0)),
        out_shape=jax.ShapeDtypeStruct((groups, t, D_MODEL), F32),
        compiler_params=pltpu.CompilerParams(
            dimension_semantics=("arbitrary", "arbitrary"), vmem_limit_bytes=VMEM_LIMIT),
        name="output_stage",
    )(x, mix, mod, mod, mod, mod, w_out, g2, w_gu, w_dn, g_final)


def kernel(x_prompt, x_sample, c_prompt, c_sample, state_mlstm_C, state_mlstm_n, state_mlstm_m, w_ada, b_ada, g_norm1, w_in, b_gate, g_mlstm_head, ln_v_g, ln_v_b, w_s, b_s, w_out, g_norm2, w_gate_up, w_down, g_final):
    depth = w_ada.shape[0]
    assert depth == 1, "single-layer trunk"
    batch, seq, _ = x_prompt.shape
    nb = x_sample.shape[0]
    assert x_sample.shape[1] == 1

    gw = GROUP_WIDTH
    g1 = g_norm1[0].reshape(1, D_MODEL)
    g2 = g_norm2[0].reshape(1, D_MODEL)
    gf = g_final.reshape(1, D_MODEL)
    ln_g = ln_v_g[0].reshape(1, gw)
    ln_b = ln_v_b[0].reshape(1, gw)
    g_head = g_mlstm_head[0].reshape(1, gw)
    bias_row = jnp.pad(b_gate[0], (0, LANES - 2 * HEADS)).reshape(1, LANES)
    tril = jnp.tril(jnp.ones((CHUNK, CHUNK), dtype=bool))
    ws_tril = jnp.where(tril[None], w_s[0], 0.0).astype(BF16)
    bs_rep = jnp.broadcast_to(b_s[0][:, :, None], (HEADS, CHUNK, LANES))
    ws0_row = jnp.repeat(w_s[0][:, 0, 0], HEAD_DIM).reshape(1, gw)
    bs0_row = jnp.repeat(b_s[0][:, 0], HEAD_DIM).reshape(1, gw)

    rows = batch + nb
    rows_pad = -(-rows // 16) * 16
    c_all = jnp.concatenate([c_prompt, c_sample, jnp.zeros((rows_pad - rows, D_MODEL), F32)], axis=0)
    mod, w_proj, w_k, w_kt = _modulation(c_all, w_ada[0], b_ada[0].reshape(1, -1), w_in[0])
    mod_p = mod[:batch].reshape(batch, 1, N_MOD * D_MODEL)
    mod_s = mod[batch:rows].reshape(1, nb, N_MOD * D_MODEL)

    a_p, b_p, g_p, kt_p, w_out_b, w_gu_b, w_dn_b = _project(
        x_prompt, mod_p, g1, w_proj, ln_g, ln_b, w_kt, (w_out[0], w_gate_up[0], w_down[0]),
        k_transposed=True, tm=PROJ_TILE, per_row=False, a_dtype=BF16)
    y_p, c_p, n_p, m_p = _prompt_tail(a_p, kt_p, b_p, g_p, x_prompt, mod_p, bias_row, g_head, ws_tril, bs_rep,
                                      w_out_b, g2, w_gu_b, w_dn_b, gf)

    xs = x_sample.reshape(1, nb, D_MODEL)
    a_s, b_s_act, g_s, k_s = _project(xs, mod_s, g1, w_proj, ln_g, ln_b, w_k, k_transposed=False,
                                      tm=nb, per_row=True, a_dtype=F32)
    a_s2, b_s2, g_s2 = a_s[0], b_s_act[0], g_s[0]
    m0_pad = jnp.pad(state_mlstm_m[0], ((0, 0), (0, LANES - HEADS)))
    n0 = state_mlstm_n[0].reshape(nb, gw)
    mix_s, c_s, n_s, m_s = _mix_sample(a_s2, k_s[0], b_s2, g_s2, m0_pad, n0, state_mlstm_C[0],
                                       bias_row, g_head, ws0_row, bs0_row)
    y_s = _output_stage(xs, mix_s.reshape(1, nb, D_MODEL), mod_s, w_out_b, g2, w_gu_b, w_dn_b, gf,
                        tm=nb, per_row=True)

    return (
        y_p,
        y_s.reshape(nb, 1, D_MODEL),
        c_p[None],
        n_p[None],
        m_p[:, :, 0][None],
        c_s[None],
        n_s.reshape(nb, HEADS, HEAD_DIM)[None],
        m_s.reshape(nb, HEADS, HEAD_DIM)[:, :, 0][None],
        a_s2[:, 2 * gw:].reshape(nb, 1, HEADS, HEAD_DIM)[None],
    )
```

```python
import functools
import math

import jax
import jax.numpy as jnp
from jax import lax
from jax.experimental import pallas as pl
from jax.experimental.pallas import tpu as pltpu

F32 = jnp.float32
BF16 = jnp.bfloat16

D_MODEL = 1024
HEADS = 4
HEAD_DIM = 128
GROUP_WIDTH = HEADS * HEAD_DIM
CHUNK = 128
D_FF = 2816
N_MOD = 6
EPS = 1e-6
QK_SCALE = HEAD_DIM ** -0.5
LANES = 128

N_B = 2 * GROUP_WIDTH

VMEM_LIMIT = 56 * 1024 * 1024


def _dot(a, b):
    return jnp.dot(a, b, preferred_element_type=F32)


def _dot_nt(a, b):
    return lax.dot_general(a, b, (((1,), (1,)), ((), ())), preferred_element_type=F32)


def _sigmoid(x):
    return 1.0 / (1.0 + jnp.exp(-x))


def _gelu_tanh(x):
    c = math.sqrt(2.0 / math.pi)
    return x * (0.5 * (1.0 + jnp.tanh(c * (x + 0.044715 * (x * x * x)))))


def _log_sigmoid(x):
    return jnp.minimum(x, 0.0) - jnp.log1p(jnp.exp(-jnp.abs(x)))


def _rms(x):
    return x * lax.rsqrt(jnp.mean(x * x, axis=-1, keepdims=True) + EPS)


def _split3_bf16(x):
    hi = x.astype(BF16)
    r1 = x - hi.astype(F32)
    mid = r1.astype(BF16)
    lo = (r1 - mid.astype(F32)).astype(BF16)
    return hi, mid, lo


MOD_TILE = 1024

IN_Q, IN_K, IN_V, IN_O, IN_GATES = 0, GROUP_WIDTH, 2 * GROUP_WIDTH, 3 * GROUP_WIDTH, 4 * GROUP_WIDTH
IN_U = IN_GATES + 2 * HEADS
IN_VS = IN_U + GROUP_WIDTH
N_IN = IN_VS + GROUP_WIDTH
N_PROJ = 5 * GROUP_WIDTH + LANES


def _mod_kernel(c_ref, w_ref, b_ref, wint_ref, o_ref, wproj_ref, wk_ref, wkt_ref):
    c = c_ref[...]
    a = (c * _sigmoid(c)).astype(BF16)
    o_ref[...] = _dot(a, w_ref[...].astype(BF16)) + b_ref[...]

    @pl.when(pl.program_id(0) == 0)
    def _():
        dst = 0
        for src in (IN_Q, IN_V, IN_VS, IN_O, IN_U):
            for blk in range(GROUP_WIDTH // LANES):
                rows = wint_ref[src + blk * LANES: src + (blk + 1) * LANES, :]
                wproj_ref[:, dst:dst + LANES] = rows.T.astype(BF16)
                dst += LANES
        gate_rows = jnp.concatenate(
            [wint_ref[IN_GATES:IN_U, :], jnp.zeros((LANES - 2 * HEADS, D_MODEL), F32)], axis=0)
        wproj_ref[:, dst:dst + LANES] = gate_rows.T.astype(BF16)
        wkt_ref[...] = wint_ref[IN_K:IN_V, :].astype(BF16)
        for blk in range(GROUP_WIDTH // LANES):
            rows = wint_ref[IN_K + blk * LANES: IN_K + (blk + 1) * LANES, :]
            wk_ref[:, blk * LANES:(blk + 1) * LANES] = rows.T.astype(BF16)


def _modulation(c_all, w_ada, b_ada, w_in_t):
    rows = c_all.shape[0]
    tn = MOD_TILE
    assert w_in_t.shape == (N_IN, D_MODEL)

    def whole(shape):
        return pl.BlockSpec(shape, lambda j: (0, 0))

    return pl.pallas_call(
        _mod_kernel,
        grid=(N_MOD * D_MODEL // tn,),
        in_specs=[
            whole((rows, D_MODEL)),
            pl.BlockSpec((D_MODEL, tn), lambda j: (0, j)),
            pl.BlockSpec((1, tn), lambda j: (0, j)),
            pl.BlockSpec((N_IN, D_MODEL), lambda j: (0, 0), pipeline_mode=pl.Buffered(1)),
        ],
        out_specs=[
            pl.BlockSpec((rows, tn), lambda j: (0, j)),
            whole((D_MODEL, N_PROJ)),
            whole((D_MODEL, GROUP_WIDTH)),
            whole((GROUP_WIDTH, D_MODEL)),
        ],
        out_shape=[
            jax.ShapeDtypeStruct((rows, N_MOD * D_MODEL), F32),
            jax.ShapeDtypeStruct((D_MODEL, N_PROJ), BF16),
            jax.ShapeDtypeStruct((D_MODEL, GROUP_WIDTH), BF16),
            jax.ShapeDtypeStruct((GROUP_WIDTH, D_MODEL), BF16),
        ],
        compiler_params=pltpu.CompilerParams(dimension_semantics=("arbitrary",), vmem_limit_bytes=VMEM_LIMIT),
        name="modulation",
    )(c_all, w_ada, b_ada, w_in_t)


def _mod_spec(piece, tm, per_row):
    if per_row:
        return pl.BlockSpec((None, tm, D_MODEL), lambda g, t: (g, t, piece))
    return pl.BlockSpec((None, 1, D_MODEL), lambda g, t: (g, 0, piece))


def _const_spec(shape):
    nd = len(shape)
    return pl.BlockSpec(shape, lambda g, t: (0,) * nd, pipeline_mode=pl.Buffered(1))


PROJ_TILE = 1024


def _proj_body(x_ref, sh_ref, sc_ref, g1_ref, w_ref, lng_ref, lnb_ref, a_ref, b_ref, g_ref):
    n_a = a_ref.shape[1]
    n_copy = n_a - GROUP_WIDTH
    x = x_ref[...]
    h = _rms(x) * g1_ref[...]
    h = (h * (1.0 + sc_ref[...]) + sh_ref[...]).astype(BF16)
    p = _dot(h, w_ref[...])
    a_ref[:, :n_copy] = p[:, :n_copy].astype(a_ref.dtype)
    vs = _gelu_tanh(p[:, n_copy:n_a])
    for hd in range(HEADS):
        sl = slice(hd * HEAD_DIM, (hd + 1) * HEAD_DIM)
        v = vs[:, sl]
        mu = jnp.mean(v, axis=-1, keepdims=True)
        vc = v - mu
        var = jnp.mean(vc * vc, axis=-1, keepdims=True)
        y = vc * lax.rsqrt(var + EPS) * lng_ref[:, sl] + lnb_ref[:, sl]
        a_ref[:, n_copy + hd * HEAD_DIM: n_copy + (hd + 1) * HEAD_DIM] = y.astype(a_ref.dtype)
    b_ref[:, :GROUP_WIDTH] = _sigmoid(p[:, n_a: n_a + GROUP_WIDTH])
    b_ref[:, GROUP_WIDTH:] = _gelu_tanh(p[:, n_a + GROUP_WIDTH: n_a + N_B])
    g_ref[...] = p[:, n_a + N_B:]
    return h


def _proj_kernel(*refs, k_transposed, n_cast):
    ins, outs = refs[:8 + n_cast], refs[8 + n_cast:]
    wk_ref, k_ref = ins[7], outs[3]
    h = _proj_body(*ins[:7], *outs[:3])
    if k_transposed:
        k_ref[...] = _dot_nt(wk_ref[...], h).astype(k_ref.dtype)
    else:
        k_ref[...] = _dot(h, wk_ref[...]).astype(k_ref.dtype)
    for src, dst in zip(ins[8:], outs[4:]):
        dst[...] = src[...].astype(dst.dtype)


def _project(x, mod, g1, w_proj, ln_g, ln_b, w_k, cast_weights=(), *, k_transposed, tm, per_row, a_dtype):
    groups, t, _ = x.shape
    n_proj = w_proj.shape[1]
    n_a = n_proj - N_B - LANES
    steps_per_group = t // tm
    n_steps = groups * steps_per_group
    in_specs = [
        pl.BlockSpec((None, tm, D_MODEL), lambda g, i: (g, i, 0)),
        _mod_spec(0, tm, per_row),
        _mod_spec(1, tm, per_row),
        _const_spec((1, D_MODEL)),
        _const_spec((D_MODEL, n_proj)),
        _const_spec((1, GROUP_WIDTH)),
        _const_spec((1, GROUP_WIDTH)),
    ]
    out_specs = [
        pl.BlockSpec((None, tm, n_a), lambda g, i: (g, i, 0)),
        pl.BlockSpec((None, tm, N_B), lambda g, i: (g, i, 0)),
        pl.BlockSpec((None, tm, LANES), lambda g, i: (g, i, 0)),
    ]
    out_shape = [
        jax.ShapeDtypeStruct((groups, t, n_a), a_dtype),
        jax.ShapeDtypeStruct((groups, t, N_B), F32),
        jax.ShapeDtypeStruct((groups, t, LANES), F32),
    ]
    args = [x, mod, mod, g1, w_proj, ln_g, ln_b, w_k]
    in_specs.append(_const_spec(w_k.shape))
    if k_transposed:
        out_specs.append(pl.BlockSpec((None, GROUP_WIDTH, tm), lambda g, i: (g, 0, i)))
        out_shape.append(jax.ShapeDtypeStruct((groups, GROUP_WIDTH, t), a_dtype))
    else:
        out_specs.append(pl.BlockSpec((None, tm, GROUP_WIDTH), lambda g, i: (g, i, 0)))
        out_shape.append(jax.ShapeDtypeStruct((groups, t, GROUP_WIDTH), a_dtype))
    for w in cast_weights:
        rows, cols = w.shape
        assert rows % (n_steps * 16) == 0, "row block must be a whole number of bf16 sublane tiles"
        spec = pl.BlockSpec((rows // n_steps, cols), lambda g, i: (g * steps_per_group + i, 0))
        in_specs.append(spec)
        out_specs.append(spec)
        out_shape.append(jax.ShapeDtypeStruct((rows, cols), BF16))
        args.append(w)
    return pl.pallas_call(
        functools.partial(_proj_kernel, k_transposed=k_transposed, n_cast=len(cast_weights)),
        grid=(groups, t // tm),
        in_specs=in_specs,
        out_specs=out_specs,
        out_shape=out_shape,
        compiler_params=pltpu.CompilerParams(
            dimension_semantics=("arbitrary", "arbitrary"), vmem_limit_bytes=VMEM_LIMIT),
        name="project",
    )(*args)


MIX_TILE = 512


def _dot3_rhs(lhs_bf16, rhs_f32):
    hi, mid, lo = _split3_bf16(rhs_f32)
    return _dot(lhs_bf16, hi) + _dot(lhs_bf16, mid) + _dot(lhs_bf16, lo)


def _dot3_lhs(lhs_f32, rhs_bf16):
    hi, mid, lo = _split3_bf16(lhs_f32)
    return _dot(hi, rhs_bf16) + _dot(mid, rhs_bf16) + _dot(lo, rhs_bf16)


FF_CHUNKS = (768, 768, 768, 512)
MIXER_LOOKAHEAD = 2
OUT_ROW_BLOCKS = 2


def _gate_rows(g_ref, bias_ref, tril, triu, n_chunks):
    pre = [g_ref[c * CHUNK:(c + 1) * CHUNK, :] + bias_ref[...] for c in range(n_chunks)]
    bc_all = _dot3_rhs(tril, jnp.concatenate([_log_sigmoid(p) for p in pre], axis=1))
    rows_all = jnp.concatenate([p.T[0:2 * HEADS, :] for p in pre], axis=0)
    b_rows_all = _dot3_lhs(_log_sigmoid(rows_all), triu)
    bc, arow, blast = [], [], []
    for c in range(n_chunks):
        bc.append(bc_all[:, c * LANES:(c + 1) * LANES])
        rows = rows_all[c * 2 * HEADS:(c + 1) * 2 * HEADS, :]
        b_rows = b_rows_all[c * 2 * HEADS:(c + 1) * 2 * HEADS, :]
        arow.append([rows[hd:hd + 1, :] - b_rows[HEADS + hd:HEADS + hd + 1, :] for hd in range(HEADS)])
        blast.append([jnp.min(b_rows[HEADS + hd:HEADS + hd + 1, :], axis=1, keepdims=True) for hd in range(HEADS)])
    return bc, arow, blast


def _mlstm_local(q, kt, a_row, b_last, causal):
    amat = jnp.where(causal, a_row, -jnp.inf)
    m_row = jnp.max(amat, axis=1, keepdims=True)
    s_loc = (_dot(q, kt) * jnp.exp(amat - m_row)).astype(BF16)
    g_row = b_last + a_row
    g_loc = jnp.max(g_row, axis=1, keepdims=True)
    kw = (kt.astype(F32) * jnp.exp(g_row - g_loc)).astype(BF16)
    return m_row, s_loc, g_loc, kw


def _mlstm_readout(local, q, v, og, g_head, b_last, b_rep, cta, m_prev, ones_blk):
    m_row, s_loc, g_loc, kw = local
    va = jnp.concatenate([v, ones_blk], axis=1)
    nd_loc = _dot(s_loc, va)
    u_aug = _dot(kw, va)
    inter = _dot(q, cta.astype(BF16))
    mm = jnp.maximum(m_prev, m_row)
    f_loc = jnp.exp(m_row - mm) * QK_SCALE
    f_int = jnp.exp(m_prev - mm) * QK_SCALE
    nd = f_loc * nd_loc + f_int * inter
    clamp = jnp.exp(-(b_rep + mm))
    hh = nd[:, :HEAD_DIM] / jnp.maximum(jnp.abs(nd[:, HEAD_DIM:]), clamp)
    ml = _rms(hh) * g_head * og

    dec = b_last + m_prev
    m_new = jnp.maximum(dec, g_loc)
    cta_new = jnp.exp(dec - m_new) * cta + jnp.exp(g_loc - m_new) * u_aug
    return ml, cta_new, m_new


def _prompt_tail_kernel(a_ref, kt_ref, b_ref, g_ref, x_ref, gt1_ref, sh2_ref, sc2_ref, gt2_ref,
                        xs_ref, mixs_ref, gt1s_ref, sh2s_ref, sc2s_ref, gt2s_ref,
                        bias_ref, gh_ref, ws_ref, bs_ref, wo_ref, g2_ref, wgu_ref, wdn_ref, gf_ref,
                        y_ref, ys_ref, c_out, n_out, m_out,
                        mix_ref, cta_ref, m_ref, *, tiles_per_seq):
    step = pl.program_id(0)
    n_tiles = pl.num_programs(0) - 1
    tm = a_ref.shape[0]
    n_chunks = tm // CHUNK
    n_copy = a_ref.shape[1] - GROUP_WIDTH
    ns = xs_ref.shape[0]
    first = step == 0

    @pl.when(first)
    def _():
        mix_ref[:ns, :] = mixs_ref[...].astype(mix_ref.dtype)
        mix_ref[ns:, :] = jnp.zeros((tm - ns, D_MODEL), mix_ref.dtype)

    @pl.when(step % tiles_per_seq == 0)
    def _():
        cta_ref[...] = jnp.zeros_like(cta_ref)
        m_ref[...] = jnp.zeros_like(m_ref)

    row = lax.broadcasted_iota(jnp.int32, (CHUNK, CHUNK), 0)
    col = lax.broadcasted_iota(jnp.int32, (CHUNK, CHUNK), 1)
    causal = row >= col
    tril = jnp.where(causal, 1.0, 0.0).astype(BF16)
    triu = jnp.where(row <= col, 1.0, 0.0).astype(BF16)
    ones_blk = jnp.ones((CHUNK, HEAD_DIM), BF16)

    rows_per_block = tm // OUT_ROW_BLOCKS
    assert ns <= rows_per_block
    blocks = [slice(r0, r0 + rows_per_block) for r0 in range(0, tm, rows_per_block)]

    def block_rows(r, tile_ref, sample_ref):
        if tile_ref.shape[0] == 1:
            if r != 0:
                return tile_ref[...]
            row = tile_ref[...]
            return jnp.concatenate([jnp.where(first, sample_ref[...], row),
                                    jnp.broadcast_to(row, (rows_per_block - ns, D_MODEL))], axis=0)
        if r != 0:
            return tile_ref[blocks[r], :]
        return jnp.concatenate([jnp.where(first, sample_ref[...], tile_ref[:ns, :]),
                                tile_ref[ns:rows_per_block, :]], axis=0)

    gt1 = [block_rows(r, gt1_ref, gt1s_ref) for r in range(len(blocks))]
    sh2 = [block_rows(r, sh2_ref, sh2s_ref) for r in range(len(blocks))]
    sc2 = [block_rows(r, sc2_ref, sc2s_ref) for r in range(len(blocks))]
    gt2 = [block_rows(r, gt2_ref, gt2s_ref) for r in range(len(blocks))]
    outproj = [_dot(mix_ref[rs, :], wo_ref[...]) for rs in blocks]
    x1, h2 = [], []
    for r, op in enumerate(outproj):
        x1.append(block_rows(r, x_ref, xs_ref) + gt1[r] * op)
        h = _rms(x1[-1]) * g2_ref[...]
        h2.append((h * (1.0 + sc2[r]) + sh2[r]).astype(BF16))

    bc, arow, blast = _gate_rows(g_ref, bias_ref, tril, triu, n_chunks)
    state = [(cta_ref[hd], jnp.max(m_ref[hd][0:1, :], axis=1, keepdims=True)) for hd in range(HEADS)]
    sg_all = [None] * HEADS

    def mixer_local(c, hd):
        rs = slice(c * CHUNK, (c + 1) * CHUNK)
        sl = slice(hd * HEAD_DIM, (hd + 1) * HEAD_DIM)
        return _mlstm_local(a_ref[rs, sl], kt_ref[sl, rs], arow[c][hd], blast[c][hd], causal)

    def mixer_readout(c, hd, local):
        rs = slice(c * CHUNK, (c + 1) * CHUNK)
        sl = slice(hd * HEAD_DIM, (hd + 1) * HEAD_DIM)
        if c == 0:
            vsn_all = jnp.concatenate(
                [a_ref[cc * CHUNK:(cc + 1) * CHUNK, n_copy + hd * HEAD_DIM: n_copy + (hd + 1) * HEAD_DIM]
                 for cc in range(n_chunks)], axis=1)
            sg_all[hd] = _dot(ws_ref[hd], vsn_all)
        cta, m_prev = state[hd]
        b_rep = jnp.broadcast_to(bc[c][:, HEADS + hd: HEADS + hd + 1], (CHUNK, HEAD_DIM))
        ml, cta, m_prev = _mlstm_readout(
            local, a_ref[rs, sl], a_ref[rs, GROUP_WIDTH + hd * HEAD_DIM: GROUP_WIDTH + (hd + 1) * HEAD_DIM],
            b_ref[rs, sl], gh_ref[:, sl], blast[c][hd], b_rep, cta, m_prev, ones_blk)
        state[hd] = (cta, m_prev)
        mix_ref[rs, sl] = ml.astype(mix_ref.dtype)
        sg = sg_all[hd][:, c * CHUNK:(c + 1) * CHUNK] + bs_ref[hd]
        ug = b_ref[rs, GROUP_WIDTH + hd * HEAD_DIM: GROUP_WIDTH + (hd + 1) * HEAD_DIM]
        mix_ref[rs, GROUP_WIDTH + hd * HEAD_DIM: GROUP_WIDTH + (hd + 1) * HEAD_DIM] = (ug * sg).astype(mix_ref.dtype)

    pieces = [(c, hd) for c in range(n_chunks) for hd in range(HEADS)]
    per_ff = -(-len(pieces) // len(FF_CHUNKS))
    ahead = [mixer_local(*pieces[p]) for p in range(MIXER_LOOKAHEAD)]
    acc = [None] * len(blocks)
    f0 = 0
    for j, width in enumerate(FF_CHUNKS):
        gate_up = [(_dot(h, wgu_ref[:, f0:f0 + width]), _dot(h, wgu_ref[:, D_FF + f0:D_FF + f0 + width]))
                   for h in h2]
        for idx in range(j * per_ff, min((j + 1) * per_ff, len(pieces))):
            if idx + MIXER_LOOKAHEAD < len(pieces):
                ahead.append(mixer_local(*pieces[idx + MIXER_LOOKAHEAD]))
            mixer_readout(*pieces[idx], ahead.pop(0))
        for r, (gate, up) in enumerate(gate_up):
            act = (gate * _sigmoid(gate) * up).astype(BF16)
            part = _dot(act, wdn_ref[f0:f0 + width, :])
            acc[r] = part if acc[r] is None else acc[r] + part
            if j + 1 == len(FF_CHUNKS):
                x2 = x1[r] + gt2[r] * acc[r]
                y_ref[blocks[r], :] = _rms(x2) * gf_ref[...]
        f0 += width

    for hd in range(HEADS):
        cta_ref[hd] = state[hd][0]
        m_ref[hd] = jnp.broadcast_to(state[hd][1], m_ref.shape[1:])

    @pl.when(first)
    def _():
        ys_ref[...] = y_ref[:ns, :]

    @pl.when(jnp.logical_and(step % tiles_per_seq == tiles_per_seq - 1, step < n_tiles))
    def _():
        for hd in range(HEADS):
            cta = cta_ref[hd]
            c_out[hd] = cta[:, :HEAD_DIM].T
            n_out[hd: hd + 1, :] = cta[:, HEAD_DIM:].T[0:1, :]
            m_out[hd: hd + 1, :] = m_ref[hd][0:1, :]


def _prompt_tail(a, kt, b, gates, x, mod, x_s, mix_s, mod_s, bias_row, g_head, ws_tril, bs_rep,
                 w_out, g2, w_gu, w_dn, g_final):
    groups, t, n_a = a.shape
    tm = MIX_TILE
    tps = t // tm
    n_tiles = groups * tps
    ns = x_s.shape[0]
    assert sum(FF_CHUNKS) == D_FF

    def sample_piece(piece):
        return pl.BlockSpec((ns, D_MODEL), lambda i: (0, piece), pipeline_mode=pl.Buffered(1))

    def cur(i):
        return jnp.minimum(i, n_tiles - 1)

    def prev(i):
        return jnp.maximum(i - 1, 0)

    def rows(tile, width):
        return pl.BlockSpec((None, tm, width), lambda i: (tile(i) // tps, tile(i) % tps, 0))

    def mod_piece(piece):
        return pl.BlockSpec((None, 1, D_MODEL), lambda i: (prev(i) // tps, 0, piece))

    def const(shape):
        nd = len(shape)
        return pl.BlockSpec(shape, lambda i: (0,) * nd, pipeline_mode=pl.Buffered(1))

    def per_seq(shape):
        nd = len(shape)
        return pl.BlockSpec((None,) + shape, lambda i: (cur(i) // tps,) + (0,) * nd)

    return pl.pallas_call(
        functools.partial(_prompt_tail_kernel, tiles_per_seq=tps),
        grid=(n_tiles + 1,),
        in_specs=[
            rows(cur, n_a),
            pl.BlockSpec((None, GROUP_WIDTH, tm), lambda i: (cur(i) // tps, 0, cur(i) % tps)),
            rows(cur, N_B),
            rows(cur, LANES),
            rows(prev, D_MODEL),
            mod_piece(2), mod_piece(3), mod_piece(4), mod_piece(5),
            const((ns, D_MODEL)),
            const((ns, D_MODEL)),
            sample_piece(2), sample_piece(3), sample_piece(4), sample_piece(5),
            const((1, LANES)),
            const((1, GROUP_WIDTH)),
            const((HEADS, CHUNK, CHUNK)),
            const((HEADS, CHUNK, LANES)),
            const((D_MODEL, D_MODEL)),
            const((1, D_MODEL)),
            const((D_MODEL, 2 * D_FF)),
            const((D_FF, D_MODEL)),
            const((1, D_MODEL)),
        ],
        out_specs=[
            rows(prev, D_MODEL),
            pl.BlockSpec((ns, D_MODEL), lambda i: (0, 0)),
            per_seq((HEADS, HEAD_DIM, HEAD_DIM)),
            per_seq((HEADS, HEAD_DIM)),
            per_seq((HEADS, LANES)),
        ],
        out_shape=[
            jax.ShapeDtypeStruct((groups, t, D_MODEL), F32),
            jax.ShapeDtypeStruct((ns, D_MODEL), F32),
            jax.ShapeDtypeStruct((groups, HEADS, HEAD_DIM, HEAD_DIM), F32),
            jax.ShapeDtypeStruct((groups, HEADS, HEAD_DIM), F32),
            jax.ShapeDtypeStruct((groups, HEADS, LANES), F32),
        ],
        scratch_shapes=[
            pltpu.VMEM((tm, D_MODEL), BF16),
            pltpu.VMEM((HEADS, HEAD_DIM, 2 * HEAD_DIM), F32),
            pltpu.VMEM((HEADS, 8, LANES), F32),
        ],
        compiler_params=pltpu.CompilerParams(
            dimension_semantics=("arbitrary",), vmem_limit_bytes=VMEM_LIMIT),
        name="prompt_tail",
    )(a, kt, b, gates, x, mod, mod, mod, mod, x_s, mix_s, mod_s, mod_s, mod_s, mod_s,
      bias_row, g_head, ws_tril, bs_rep, w_out, g2, w_gu, w_dn, g_final)


SAMPLE_TOKENS_PER_STEP = 64
SAMPLE_UNROLL = 8


def _mix_sample_kernel(q_ref, k_ref, v_ref, vsn_ref, og_ref, ug_ref, g_ref, m0_ref, n0_ref, c_ref,
                       bias_ref, gh_ref, ws0_ref, bs0_ref,
                       mix_ref, c_out, n_out, m_out,
                       vt_ref, kp_ref, wd_ref, cqt_ref):
    hd = pl.program_id(0)
    grp = pl.program_id(1)
    nb = q_ref.shape[0]
    lane = lax.broadcasted_iota(jnp.int32, (nb, LANES), 1)

    def gate_terms():
        pre = g_ref[...] + bias_ref[...]
        i_pre = jnp.sum(jnp.where(lane == hd, pre, 0.0), axis=1, keepdims=True)
        f_pre = jnp.sum(jnp.where(lane == hd + HEADS, pre, 0.0), axis=1, keepdims=True)
        m_prev = jnp.sum(jnp.where(lane == hd, m0_ref[...], 0.0), axis=1, keepdims=True)
        inter = _log_sigmoid(f_pre) + m_prev
        m_t = jnp.maximum(inter, i_pre)
        return m_t, jnp.exp(i_pre - m_t), jnp.exp(inter - m_t)

    @pl.when(grp == 0)
    def _():
        _, w_in, w_dec = gate_terms()
        vt_ref[...] = v_ref[...].T
        kp_ref[...] = (w_in * k_ref[...]).astype(BF16)
        wd_ref[...] = jnp.broadcast_to(w_dec, wd_ref.shape)
        cqt_ref[...] = jnp.zeros_like(cqt_ref)

    lane_sq = lax.broadcasted_iota(jnp.int32, (HEAD_DIM, LANES), 1)
    tokens = c_ref.shape[0]

    def body(r, cqt):
        tok = grp * tokens + r
        q_row = q_ref[pl.ds(tok, 1), :]
        c_b = c_ref[r]
        sel = lane_sq == tok
        cq_col = jnp.sum(c_b * q_row, axis=1, keepdims=True)
        v_sel = jnp.where(sel, vt_ref[...], 0.0).astype(BF16)
        outer = _dot(v_sel, kp_ref[...])
        c_out[r] = wd_ref[pl.ds(tok, 1), :] * c_b + outer
        return jnp.where(sel, cq_col, cqt)

    cqt_ref[...] = lax.fori_loop(0, tokens, body, cqt_ref[...], unroll=SAMPLE_UNROLL)

    @pl.when(grp == pl.num_programs(1) - 1)
    def _():
        m_t, w_in, w_dec = gate_terms()
        q = q_ref[...]
        k = k_ref[...]
        v = v_ref[...]
        n0 = n0_ref[...]
        cq = cqt_ref[...].T
        s = jnp.sum(q * k, axis=1, keepdims=True) * (QK_SCALE * w_in)
        w_inter = w_dec * QK_SCALE
        num = s * v + w_inter * cq
        den = s + w_inter * jnp.sum(n0 * q, axis=1, keepdims=True)
        hh = num / jnp.maximum(jnp.abs(den), jnp.exp(-m_t))
        ml = _rms(hh) * gh_ref[...] * og_ref[...]
        cm = ug_ref[...] * (ws0_ref[...] * vsn_ref[...] + bs0_ref[...])
        n_out[...] = w_dec * n0 + w_in * k
        m_out[...] = jnp.broadcast_to(m_t, m_out.shape)
        for hh_static in range(HEADS):
            @pl.when(hd == hh_static)
            def _():
                mix_ref[:, hh_static * HEAD_DIM: (hh_static + 1) * HEAD_DIM] = ml
                mix_ref[:, GROUP_WIDTH + hh_static * HEAD_DIM: GROUP_WIDTH + (hh_static + 1) * HEAD_DIM] = cm


def _mix_sample(a, k, b, gates, m0_pad, n0, c0, bias_row, g_head, ws0_row, bs0_row):
    nb = a.shape[0]
    tb = SAMPLE_TOKENS_PER_STEP

    def head_block(offset):
        return pl.BlockSpec((nb, HEAD_DIM), lambda h, g: (0, offset + h))

    full = pl.BlockSpec((nb, LANES), lambda h, g: (0, 0))
    head_row = pl.BlockSpec((1, HEAD_DIM), lambda h, g: (0, h))
    c_spec = pl.BlockSpec((tb, None, HEAD_DIM, HEAD_DIM), lambda h, g: (g, h, 0, 0))
    return pl.pallas_call(
        _mix_sample_kernel,
        grid=(HEADS, nb // tb),
        in_specs=[
            head_block(0), head_block(0), head_block(HEADS), head_block(2 * HEADS),
            head_block(0), head_block(HEADS),
            full, full, head_block(0), c_spec,
            pl.BlockSpec((1, LANES), lambda h, g: (0, 0)),
            head_row, head_row, head_row,
        ],
        out_specs=[
            pl.BlockSpec((nb, D_MODEL), lambda h, g: (0, 0)),
            c_spec,
            head_block(0),
            head_block(0),
        ],
        out_shape=[
            jax.ShapeDtypeStruct((nb, D_MODEL), F32),
            jax.ShapeDtypeStruct(c0.shape, F32),
            jax.ShapeDtypeStruct((nb, GROUP_WIDTH), F32),
            jax.ShapeDtypeStruct((nb, GROUP_WIDTH), F32),
        ],
        scratch_shapes=[
            pltpu.VMEM((HEAD_DIM, nb), F32),
            pltpu.VMEM((nb, HEAD_DIM), BF16),
            pltpu.VMEM((nb, LANES), F32),
            pltpu.VMEM((HEAD_DIM, nb), F32),
        ],
        compiler_params=pltpu.CompilerParams(
            dimension_semantics=("arbitrary", "arbitrary"), vmem_limit_bytes=VMEM_LIMIT),
        name="mix_sample",
    )(a, k, a, a, b, b, gates, m0_pad, n0, c0, bias_row, g_head, ws0_row, bs0_row)


def kernel(x_prompt, x_sample, c_prompt, c_sample, state_mlstm_C, state_mlstm_n, state_mlstm_m, w_ada, b_ada, g_norm1, w_in, b_gate, g_mlstm_head, ln_v_g, ln_v_b, w_s, b_s, w_out, g_norm2, w_gate_up, w_down, g_final):
    depth = w_ada.shape[0]
    assert depth == 1, "single-layer trunk"
    batch, seq, _ = x_prompt.shape
    nb = x_sample.shape[0]
    assert x_sample.shape[1] == 1

    gw = GROUP_WIDTH
    g1 = g_norm1[0].reshape(1, D_MODEL)
    g2 = g_norm2[0].reshape(1, D_MODEL)
    gf = g_final.reshape(1, D_MODEL)
    ln_g = ln_v_g[0].reshape(1, gw)
    ln_b = ln_v_b[0].reshape(1, gw)
    g_head = g_mlstm_head[0].reshape(1, gw)
    bias_row = jnp.pad(b_gate[0], (0, LANES - 2 * HEADS)).reshape(1, LANES)
    tril = jnp.tril(jnp.ones((CHUNK, CHUNK), dtype=bool))
    ws_tril = jnp.where(tril[None], w_s[0], 0.0).astype(BF16)
    bs_rep = jnp.broadcast_to(b_s[0][:, :, None], (HEADS, CHUNK, LANES))
    ws0_row = jnp.repeat(w_s[0][:, 0, 0], HEAD_DIM).reshape(1, gw)
    bs0_row = jnp.repeat(b_s[0][:, 0], HEAD_DIM).reshape(1, gw)

    rows = batch + nb
    rows_pad = -(-rows // 16) * 16
    c_all = jnp.concatenate([c_prompt, c_sample, jnp.zeros((rows_pad - rows, D_MODEL), F32)], axis=0)
    mod, w_proj, w_k, w_kt = _modulation(c_all, w_ada[0], b_ada[0].reshape(1, -1), w_in[0].T)
    mod_p = mod[:batch].reshape(batch, 1, N_MOD * D_MODEL)
    mod_s = mod[batch:rows].reshape(1, nb, N_MOD * D_MODEL)

    a_p, b_p, g_p, kt_p, w_out_b, w_gu_b, w_dn_b = _project(
        x_prompt, mod_p, g1, w_proj, ln_g, ln_b, w_kt, (w_out[0], w_gate_up[0], w_down[0]),
        k_transposed=True, tm=PROJ_TILE, per_row=False, a_dtype=BF16)

    xs = x_sample.reshape(1, nb, D_MODEL)
    a_s, b_s_act, g_s, k_s = _project(xs, mod_s, g1, w_proj, ln_g, ln_b, w_k, k_transposed=False,
                                      tm=nb, per_row=True, a_dtype=F32)
    a_s2, b_s2, g_s2 = a_s[0], b_s_act[0], g_s[0]
    m0_pad = jnp.pad(state_mlstm_m[0], ((0, 0), (0, LANES - HEADS)))
    n0 = state_mlstm_n[0].reshape(nb, gw)
    mix_s, c_s, n_s, m_s = _mix_sample(a_s2, k_s[0], b_s2, g_s2, m0_pad, n0, state_mlstm_C[0],
                                       bias_row, g_head, ws0_row, bs0_row)

    y_p, y_s, c_p, n_p, m_p = _prompt_tail(a_p, kt_p, b_p, g_p, x_prompt, mod_p, xs[0], mix_s, mod_s[0],
                                           bias_row, g_head, ws_tril, bs_rep, w_out_b, g2, w_gu_b, w_dn_b, gf)

    return (
        y_p,
        y_s.reshape(nb, 1, D_MODEL),
        c_p[None],
        n_p[None],
        m_p[:, :, 0][None],
        c_s[None],
        n_s.reshape(nb, HEADS, HEAD_DIM)[None],
        m_s.reshape(nb, HEADS, HEAD_DIM)[:, :, 0][None],
        a_s2[:, 2 * gw:].reshape(nb, 1, HEADS, HEAD_DIM)[None],
    )
```

```python
import functools
import math

import jax
import jax.numpy as jnp
from jax import lax
from jax.experimental import pallas as pl
from jax.experimental.pallas import tpu as pltpu

F32 = jnp.float32
BF16 = jnp.bfloat16

D_MODEL = 1024
HEADS = 4
HEAD_DIM = 128
GROUP_WIDTH = HEADS * HEAD_DIM
CHUNK = 128
D_FF = 2816
N_MOD = 6
EPS = 1e-6
QK_SCALE = HEAD_DIM ** -0.5
LANES = 128

N_B = 2 * GROUP_WIDTH

VMEM_LIMIT = 56 * 1024 * 1024


def _dot(a, b):
    return jnp.dot(a, b, preferred_element_type=F32)


def _dot_nt(a, b):
    return lax.dot_general(a, b, (((1,), (1,)), ((), ())), preferred_element_type=F32)


def _sigmoid(x):
    return 1.0 / (1.0 + jnp.exp(-x))


def _gelu_tanh(x):
    c = math.sqrt(2.0 / math.pi)
    return x * (0.5 * (1.0 + jnp.tanh(c * (x + 0.044715 * (x * x * x)))))


def _log_sigmoid(x):
    return jnp.minimum(x, 0.0) - jnp.log1p(jnp.exp(-jnp.abs(x)))


def _rms(x):
    return x * lax.rsqrt(jnp.mean(x * x, axis=-1, keepdims=True) + EPS)


def _split3_bf16(x):
    hi = x.astype(BF16)
    r1 = x - hi.astype(F32)
    mid = r1.astype(BF16)
    lo = (r1 - mid.astype(F32)).astype(BF16)
    return hi, mid, lo


MOD_TILE = 1024

IN_Q, IN_K, IN_V, IN_O, IN_GATES = 0, GROUP_WIDTH, 2 * GROUP_WIDTH, 3 * GROUP_WIDTH, 4 * GROUP_WIDTH
IN_U = IN_GATES + 2 * HEADS
IN_VS = IN_U + GROUP_WIDTH
N_IN = IN_VS + GROUP_WIDTH
N_PROJ = 5 * GROUP_WIDTH + LANES


def _mod_kernel(c_ref, w_ref, b_ref, wint_ref, o_ref, wproj_ref, wk_ref, wkt_ref):
    c = c_ref[...]
    a = (c * _sigmoid(c)).astype(BF16)
    o_ref[...] = _dot(a, w_ref[...].astype(BF16)) + b_ref[...]

    @pl.when(pl.program_id(0) == 0)
    def _():
        dst = 0
        for src in (IN_Q, IN_V, IN_VS, IN_O, IN_U):
            for blk in range(GROUP_WIDTH // LANES):
                rows = wint_ref[src + blk * LANES: src + (blk + 1) * LANES, :]
                wproj_ref[:, dst:dst + LANES] = rows.T.astype(BF16)
                dst += LANES
        gate_rows = jnp.concatenate(
            [wint_ref[IN_GATES:IN_U, :], jnp.zeros((LANES - 2 * HEADS, D_MODEL), F32)], axis=0)
        wproj_ref[:, dst:dst + LANES] = gate_rows.T.astype(BF16)
        wkt_ref[...] = wint_ref[IN_K:IN_V, :].astype(BF16)
        for blk in range(GROUP_WIDTH // LANES):
            rows = wint_ref[IN_K + blk * LANES: IN_K + (blk + 1) * LANES, :]
            wk_ref[:, blk * LANES:(blk + 1) * LANES] = rows.T.astype(BF16)


def _modulation(c_all, w_ada, b_ada, w_in_t):
    rows = c_all.shape[0]
    tn = MOD_TILE
    assert w_in_t.shape == (N_IN, D_MODEL)

    def whole(shape):
        return pl.BlockSpec(shape, lambda j: (0, 0))

    return pl.pallas_call(
        _mod_kernel,
        grid=(N_MOD * D_MODEL // tn,),
        in_specs=[
            whole((rows, D_MODEL)),
            pl.BlockSpec((D_MODEL, tn), lambda j: (0, j)),
            pl.BlockSpec((1, tn), lambda j: (0, j)),
            pl.BlockSpec((N_IN, D_MODEL), lambda j: (0, 0), pipeline_mode=pl.Buffered(1)),
        ],
        out_specs=[
            pl.BlockSpec((rows, tn), lambda j: (0, j)),
            whole((D_MODEL, N_PROJ)),
            whole((D_MODEL, GROUP_WIDTH)),
            whole((GROUP_WIDTH, D_MODEL)),
        ],
        out_shape=[
            jax.ShapeDtypeStruct((rows, N_MOD * D_MODEL), F32),
            jax.ShapeDtypeStruct((D_MODEL, N_PROJ), BF16),
            jax.ShapeDtypeStruct((D_MODEL, GROUP_WIDTH), BF16),
            jax.ShapeDtypeStruct((GROUP_WIDTH, D_MODEL), BF16),
        ],
        compiler_params=pltpu.CompilerParams(dimension_semantics=("arbitrary",), vmem_limit_bytes=VMEM_LIMIT),
        name="modulation",
    )(c_all, w_ada, b_ada, w_in_t)


def _mod_spec(piece, tm, per_row):
    if per_row:
        return pl.BlockSpec((None, tm, D_MODEL), lambda g, t: (g, t, piece))
    return pl.BlockSpec((None, 1, D_MODEL), lambda g, t: (g, 0, piece))


def _const_spec(shape):
    nd = len(shape)
    return pl.BlockSpec(shape, lambda g, t: (0,) * nd, pipeline_mode=pl.Buffered(1))


PROJ_TILE = 1024


def _proj_body(x_ref, sh_ref, sc_ref, g1_ref, w_ref, lng_ref, lnb_ref, a_ref, b_ref, g_ref):
    n_a = a_ref.shape[1]
    n_copy = n_a - GROUP_WIDTH
    x = x_ref[...]
    h = _rms(x) * g1_ref[...]
    h = (h * (1.0 + sc_ref[...]) + sh_ref[...]).astype(BF16)
    p = _dot(h, w_ref[...])
    a_ref[:, :n_copy] = p[:, :n_copy].astype(a_ref.dtype)
    vs = _gelu_tanh(p[:, n_copy:n_a])
    for hd in range(HEADS):
        sl = slice(hd * HEAD_DIM, (hd + 1) * HEAD_DIM)
        v = vs[:, sl]
        mu = jnp.mean(v, axis=-1, keepdims=True)
        vc = v - mu
        var = jnp.mean(vc * vc, axis=-1, keepdims=True)
        y = vc * lax.rsqrt(var + EPS) * lng_ref[:, sl] + lnb_ref[:, sl]
        a_ref[:, n_copy + hd * HEAD_DIM: n_copy + (hd + 1) * HEAD_DIM] = y.astype(a_ref.dtype)
    b_ref[:, :GROUP_WIDTH] = _sigmoid(p[:, n_a: n_a + GROUP_WIDTH])
    b_ref[:, GROUP_WIDTH:] = _gelu_tanh(p[:, n_a + GROUP_WIDTH: n_a + N_B])
    g_ref[...] = p[:, n_a + N_B:]
    return h


def _proj_kernel(*refs, k_transposed, n_cast):
    ins, outs = refs[:8 + n_cast], refs[8 + n_cast:]
    wk_ref, k_ref = ins[7], outs[3]
    h = _proj_body(*ins[:7], *outs[:3])
    if k_transposed:
        k_ref[...] = _dot_nt(wk_ref[...], h).astype(k_ref.dtype)
    else:
        k_ref[...] = _dot(h, wk_ref[...]).astype(k_ref.dtype)
    for src, dst in zip(ins[8:], outs[4:]):
        dst[...] = src[...].astype(dst.dtype)


def _project(x, mod, g1, w_proj, ln_g, ln_b, w_k, cast_weights=(), *, k_transposed, tm, per_row, a_dtype):
    groups, t, _ = x.shape
    n_proj = w_proj.shape[1]
    n_a = n_proj - N_B - LANES
    steps_per_group = t // tm
    n_steps = groups * steps_per_group
    in_specs = [
        pl.BlockSpec((None, tm, D_MODEL), lambda g, i: (g, i, 0)),
        _mod_spec(0, tm, per_row),
        _mod_spec(1, tm, per_row),
        _const_spec((1, D_MODEL)),
        _const_spec((D_MODEL, n_proj)),
        _const_spec((1, GROUP_WIDTH)),
        _const_spec((1, GROUP_WIDTH)),
    ]
    out_specs = [
        pl.BlockSpec((None, tm, n_a), lambda g, i: (g, i, 0)),
        pl.BlockSpec((None, tm, N_B), lambda g, i: (g, i, 0)),
        pl.BlockSpec((None, tm, LANES), lambda g, i: (g, i, 0)),
    ]
    out_shape = [
        jax.ShapeDtypeStruct((groups, t, n_a), a_dtype),
        jax.ShapeDtypeStruct((groups, t, N_B), F32),
        jax.ShapeDtypeStruct((groups, t, LANES), F32),
    ]
    args = [x, mod, mod, g1, w_proj, ln_g, ln_b, w_k]
    in_specs.append(_const_spec(w_k.shape))
    if k_transposed:
        out_specs.append(pl.BlockSpec((None, GROUP_WIDTH, tm), lambda g, i: (g, 0, i)))
        out_shape.append(jax.ShapeDtypeStruct((groups, GROUP_WIDTH, t), a_dtype))
    else:
        out_specs.append(pl.BlockSpec((None, tm, GROUP_WIDTH), lambda g, i: (g, i, 0)))
        out_shape.append(jax.ShapeDtypeStruct((groups, t, GROUP_WIDTH), a_dtype))
    for w in cast_weights:
        rows, cols = w.shape
        assert rows % (n_steps * 16) == 0, "row block must be a whole number of bf16 sublane tiles"
        spec = pl.BlockSpec((rows // n_steps, cols), lambda g, i: (g * steps_per_group + i, 0))
        in_specs.append(spec)
        out_specs.append(spec)
        out_shape.append(jax.ShapeDtypeStruct((rows, cols), BF16))
        args.append(w)
    return pl.pallas_call(
        functools.partial(_proj_kernel, k_transposed=k_transposed, n_cast=len(cast_weights)),
        grid=(groups, t // tm),
        in_specs=in_specs,
        out_specs=out_specs,
        out_shape=out_shape,
        compiler_params=pltpu.CompilerParams(
            dimension_semantics=("arbitrary", "arbitrary"), vmem_limit_bytes=VMEM_LIMIT),
        name="project",
    )(*args)


MIX_TILE = 512


def _dot3_rhs(lhs_bf16, rhs_f32):
    hi, mid, lo = _split3_bf16(rhs_f32)
    return _dot(lhs_bf16, hi) + _dot(lhs_bf16, mid) + _dot(lhs_bf16, lo)


def _dot3_lhs(lhs_f32, rhs_bf16):
    hi, mid, lo = _split3_bf16(lhs_f32)
    return _dot(hi, rhs_bf16) + _dot(mid, rhs_bf16) + _dot(lo, rhs_bf16)


FF_CHUNKS = (768, 768, 768, 512)
MIXER_LOOKAHEAD = 2
OUT_ROW_BLOCKS = 2


def _gate_rows(g_ref, bias_ref, tril, triu, n_chunks):
    pre = [g_ref[c * CHUNK:(c + 1) * CHUNK, :] + bias_ref[...] for c in range(n_chunks)]
    bc_all = _dot3_rhs(tril, jnp.concatenate([_log_sigmoid(p) for p in pre], axis=1))
    rows_all = jnp.concatenate([p.T[0:2 * HEADS, :] for p in pre], axis=0)
    b_rows_all = _dot3_lhs(_log_sigmoid(rows_all), triu)
    bc, arow, blast = [], [], []
    for c in range(n_chunks):
        bc.append(bc_all[:, c * LANES:(c + 1) * LANES])
        rows = rows_all[c * 2 * HEADS:(c + 1) * 2 * HEADS, :]
        b_rows = b_rows_all[c * 2 * HEADS:(c + 1) * 2 * HEADS, :]
        arow.append([rows[hd:hd + 1, :] - b_rows[HEADS + hd:HEADS + hd + 1, :] for hd in range(HEADS)])
        blast.append([jnp.min(b_rows[HEADS + hd:HEADS + hd + 1, :], axis=1, keepdims=True) for hd in range(HEADS)])
    return bc, arow, blast


def _mlstm_local(q, kt, a_row, b_last, causal):
    amat = jnp.where(causal, a_row, -jnp.inf)
    m_row = jnp.max(amat, axis=1, keepdims=True)
    s_loc = (_dot(q, kt) * jnp.exp(amat - m_row)).astype(BF16)
    g_row = b_last + a_row
    g_loc = jnp.max(g_row, axis=1, keepdims=True)
    kw = (kt.astype(F32) * jnp.exp(g_row - g_loc)).astype(BF16)
    return m_row, s_loc, g_loc, kw


def _mlstm_readout(local, q, v, og, g_head, b_last, b_rep, cta, m_prev, ones_blk):
    m_row, s_loc, g_loc, kw = local
    va = jnp.concatenate([v, ones_blk], axis=1)
    nd_loc = _dot(s_loc, va)
    u_aug = _dot(kw, va)
    inter = _dot(q, cta.astype(BF16))
    mm = jnp.maximum(m_prev, m_row)
    f_loc = jnp.exp(m_row - mm) * QK_SCALE
    f_int = jnp.exp(m_prev - mm) * QK_SCALE
    nd = f_loc * nd_loc + f_int * inter
    clamp = jnp.exp(-(b_rep + mm))
    hh = nd[:, :HEAD_DIM] / jnp.maximum(jnp.abs(nd[:, HEAD_DIM:]), clamp)
    ml = _rms(hh) * g_head * og

    dec = b_last + m_prev
    m_new = jnp.maximum(dec, g_loc)
    cta_new = jnp.exp(dec - m_new) * cta + jnp.exp(g_loc - m_new) * u_aug
    return ml, cta_new, m_new


def _prompt_tail_kernel(a_ref, kt_ref, b_ref, g_ref, x_ref, gt1_ref, sh2_ref, sc2_ref, gt2_ref,
                        bias_ref, gh_ref, ws_ref, bs_ref, wo_ref, g2_ref, wgu_ref, wdn_ref, gf_ref,
                        y_ref, c_out, n_out, m_out,
                        mix_ref, cta_ref, m_ref, *, tiles_per_seq):
    step = pl.program_id(0)
    n_tiles = pl.num_programs(0) - 1
    tm = a_ref.shape[0]
    n_chunks = tm // CHUNK
    n_copy = a_ref.shape[1] - GROUP_WIDTH

    @pl.when(step == 0)
    def _():
        mix_ref[...] = jnp.zeros_like(mix_ref)

    @pl.when(step % tiles_per_seq == 0)
    def _():
        cta_ref[...] = jnp.zeros_like(cta_ref)
        m_ref[...] = jnp.zeros_like(m_ref)

    row = lax.broadcasted_iota(jnp.int32, (CHUNK, CHUNK), 0)
    col = lax.broadcasted_iota(jnp.int32, (CHUNK, CHUNK), 1)
    causal = row >= col
    tril = jnp.where(causal, 1.0, 0.0).astype(BF16)
    triu = jnp.where(row <= col, 1.0, 0.0).astype(BF16)
    ones_blk = jnp.ones((CHUNK, HEAD_DIM), BF16)

    blocks = [slice(r0, r0 + tm // OUT_ROW_BLOCKS) for r0 in range(0, tm, tm // OUT_ROW_BLOCKS)]
    outproj = [_dot(mix_ref[rs, :], wo_ref[...]) for rs in blocks]
    x1, h2 = [], []
    for rs, op in zip(blocks, outproj):
        x1.append(x_ref[rs, :] + gt1_ref[...] * op)
        h = _rms(x1[-1]) * g2_ref[...]
        h2.append((h * (1.0 + sc2_ref[...]) + sh2_ref[...]).astype(BF16))

    bc, arow, blast = _gate_rows(g_ref, bias_ref, tril, triu, n_chunks)
    state = [(cta_ref[hd], jnp.max(m_ref[hd][0:1, :], axis=1, keepdims=True)) for hd in range(HEADS)]
    sg_all = [None] * HEADS

    def mixer_local(c, hd):
        rs = slice(c * CHUNK, (c + 1) * CHUNK)
        sl = slice(hd * HEAD_DIM, (hd + 1) * HEAD_DIM)
        return _mlstm_local(a_ref[rs, sl], kt_ref[sl, rs], arow[c][hd], blast[c][hd], causal)

    def mixer_readout(c, hd, local):
        rs = slice(c * CHUNK, (c + 1) * CHUNK)
        sl = slice(hd * HEAD_DIM, (hd + 1) * HEAD_DIM)
        if c == 0:
            vsn_all = jnp.concatenate(
                [a_ref[cc * CHUNK:(cc + 1) * CHUNK, n_copy + hd * HEAD_DIM: n_copy + (hd + 1) * HEAD_DIM]
                 for cc in range(n_chunks)], axis=1)
            sg_all[hd] = _dot(ws_ref[hd], vsn_all)
        cta, m_prev = state[hd]
        b_rep = jnp.broadcast_to(bc[c][:, HEADS + hd: HEADS + hd + 1], (CHUNK, HEAD_DIM))
        ml, cta, m_prev = _mlstm_readout(
            local, a_ref[rs, sl], a_ref[rs, GROUP_WIDTH + hd * HEAD_DIM: GROUP_WIDTH + (hd + 1) * HEAD_DIM],
            b_ref[rs, sl], gh_ref[:, sl], blast[c][hd], b_rep, cta, m_prev, ones_blk)
        state[hd] = (cta, m_prev)
        mix_ref[rs, sl] = ml.astype(mix_ref.dtype)
        sg = sg_all[hd][:, c * CHUNK:(c + 1) * CHUNK] + bs_ref[hd]
        ug = b_ref[rs, GROUP_WIDTH + hd * HEAD_DIM: GROUP_WIDTH + (hd + 1) * HEAD_DIM]
        mix_ref[rs, GROUP_WIDTH + hd * HEAD_DIM: GROUP_WIDTH + (hd + 1) * HEAD_DIM] = (ug * sg).astype(mix_ref.dtype)

    pieces = [(c, hd) for c in range(n_chunks) for hd in range(HEADS)]
    per_ff = -(-len(pieces) // len(FF_CHUNKS))
    ahead = [mixer_local(*pieces[p]) for p in range(MIXER_LOOKAHEAD)]
    acc = [None] * len(blocks)
    f0 = 0
    for j, width in enumerate(FF_CHUNKS):
        gate_up = [(_dot(h, wgu_ref[:, f0:f0 + width]), _dot(h, wgu_ref[:, D_FF + f0:D_FF + f0 + width]))
                   for h in h2]
        for idx in range(j * per_ff, min((j + 1) * per_ff, len(pieces))):
            if idx + MIXER_LOOKAHEAD < len(pieces):
                ahead.append(mixer_local(*pieces[idx + MIXER_LOOKAHEAD]))
            mixer_readout(*pieces[idx], ahead.pop(0))
        for r, (gate, up) in enumerate(gate_up):
            act = (gate * _sigmoid(gate) * up).astype(BF16)
            part = _dot(act, wdn_ref[f0:f0 + width, :])
            acc[r] = part if acc[r] is None else acc[r] + part
            if j + 1 == len(FF_CHUNKS):
                x2 = x1[r] + gt2_ref[...] * acc[r]
                y_ref[blocks[r], :] = _rms(x2) * gf_ref[...]
        f0 += width

    for hd in range(HEADS):
        cta_ref[hd] = state[hd][0]
        m_ref[hd] = jnp.broadcast_to(state[hd][1], m_ref.shape[1:])

    @pl.when(jnp.logical_and(step % tiles_per_seq == tiles_per_seq - 1, step < n_tiles))
    def _():
        for hd in range(HEADS):
            cta = cta_ref[hd]
            c_out[hd] = cta[:, :HEAD_DIM].T
            n_out[hd: hd + 1, :] = cta[:, HEAD_DIM:].T[0:1, :]
            m_out[hd: hd + 1, :] = m_ref[hd][0:1, :]


def _prompt_tail(a, kt, b, gates, x, mod, bias_row, g_head, ws_tril, bs_rep, w_out, g2, w_gu, w_dn, g_final):
    groups, t, n_a = a.shape
    tm = MIX_TILE
    tps = t // tm
    n_tiles = groups * tps
    assert sum(FF_CHUNKS) == D_FF

    def cur(i):
        return jnp.minimum(i, n_tiles - 1)

    def prev(i):
        return jnp.maximum(i - 1, 0)

    def rows(tile, width):
        return pl.BlockSpec((None, tm, width), lambda i: (tile(i) // tps, tile(i) % tps, 0))

    def mod_piece(piece):
        return pl.BlockSpec((None, 1, D_MODEL), lambda i: (prev(i) // tps, 0, piece))

    def const(shape):
        nd = len(shape)
        return pl.BlockSpec(shape, lambda i: (0,) * nd, pipeline_mode=pl.Buffered(1))

    def per_seq(shape):
        nd = len(shape)
        return pl.BlockSpec((None,) + shape, lambda i: (cur(i) // tps,) + (0,) * nd)

    return pl.pallas_call(
        functools.partial(_prompt_tail_kernel, tiles_per_seq=tps),
        grid=(n_tiles + 1,),
        in_specs=[
            rows(cur, n_a),
            pl.BlockSpec((None, GROUP_WIDTH, tm), lambda i: (cur(i) // tps, 0, cur(i) % tps)),
            rows(cur, N_B),
            rows(cur, LANES),
            rows(prev, D_MODEL),
            mod_piece(2), mod_piece(3), mod_piece(4), mod_piece(5),
            const((1, LANES)),
            const((1, GROUP_WIDTH)),
            const((HEADS, CHUNK, CHUNK)),
            const((HEADS, CHUNK, LANES)),
            const((D_MODEL, D_MODEL)),
            const((1, D_MODEL)),
            const((D_MODEL, 2 * D_FF)),
            const((D_FF, D_MODEL)),
            const((1, D_MODEL)),
        ],
        out_specs=[
            rows(prev, D_MODEL),
            per_seq((HEADS, HEAD_DIM, HEAD_DIM)),
            per_seq((HEADS, HEAD_DIM)),
            per_seq((HEADS, LANES)),
        ],
        out_shape=[
            jax.ShapeDtypeStruct((groups, t, D_MODEL), F32),
            jax.ShapeDtypeStruct((groups, HEADS, HEAD_DIM, HEAD_DIM), F32),
            jax.ShapeDtypeStruct((groups, HEADS, HEAD_DIM), F32),
            jax.ShapeDtypeStruct((groups, HEADS, LANES), F32),
        ],
        scratch_shapes=[
            pltpu.VMEM((tm, D_MODEL), BF16),
            pltpu.VMEM((HEADS, HEAD_DIM, 2 * HEAD_DIM), F32),
            pltpu.VMEM((HEADS, 8, LANES), F32),
        ],
        compiler_params=pltpu.CompilerParams(
            dimension_semantics=("arbitrary",), vmem_limit_bytes=VMEM_LIMIT),
        name="prompt_tail",
    )(a, kt, b, gates, x, mod, mod, mod, mod, bias_row, g_head, ws_tril, bs_rep, w_out, g2, w_gu, w_dn, g_final)


SAMPLE_TOKENS_PER_STEP = 64
SAMPLE_UNROLL = 8


def _mix_sample_kernel(q_ref, k_ref, v_ref, vsn_ref, og_ref, ug_ref, g_ref, m0_ref, n0_ref, c_ref,
                       bias_ref, gh_ref, ws0_ref, bs0_ref,
                       mix_ref, c_out, n_out, m_out,
                       vt_ref, kp_ref, wd_ref, cqt_ref):
    hd = pl.program_id(0)
    grp = pl.program_id(1)
    nb = q_ref.shape[0]
    lane = lax.broadcasted_iota(jnp.int32, (nb, LANES), 1)

    def gate_terms():
        pre = g_ref[...] + bias_ref[...]
        i_pre = jnp.sum(jnp.where(lane == hd, pre, 0.0), axis=1, keepdims=True)
        f_pre = jnp.sum(jnp.where(lane == hd + HEADS, pre, 0.0), axis=1, keepdims=True)
        m_prev = jnp.sum(jnp.where(lane == hd, m0_ref[...], 0.0), axis=1, keepdims=True)
        inter = _log_sigmoid(f_pre) + m_prev
        m_t = jnp.maximum(inter, i_pre)
        return m_t, jnp.exp(i_pre - m_t), jnp.exp(inter - m_t)

    @pl.when(grp == 0)
    def _():
        _, w_in, w_dec = gate_terms()
        vt_ref[...] = v_ref[...].T
        kp_ref[...] = (w_in * k_ref[...]).astype(BF16)
        wd_ref[...] = jnp.broadcast_to(w_dec, wd_ref.shape)
        cqt_ref[...] = jnp.zeros_like(cqt_ref)

    lane_sq = lax.broadcasted_iota(jnp.int32, (HEAD_DIM, LANES), 1)
    tokens = c_ref.shape[0]

    def body(r, cqt):
        tok = grp * tokens + r
        q_row = q_ref[pl.ds(tok, 1), :]
        c_b = c_ref[r]
        sel = lane_sq == tok
        cq_col = jnp.sum(c_b * q_row, axis=1, keepdims=True)
        v_sel = jnp.where(sel, vt_ref[...], 0.0).astype(BF16)
        outer = _dot(v_sel, kp_ref[...])
        c_out[r] = wd_ref[pl.ds(tok, 1), :] * c_b + outer
        return jnp.where(sel, cq_col, cqt)

    cqt_ref[...] = lax.fori_loop(0, tokens, body, cqt_ref[...], unroll=SAMPLE_UNROLL)

    @pl.when(grp == pl.num_programs(1) - 1)
    def _():
        m_t, w_in, w_dec = gate_terms()
        q = q_ref[...]
        k = k_ref[...]
        v = v_ref[...]
        n0 = n0_ref[...]
        cq = cqt_ref[...].T
        s = jnp.sum(q * k, axis=1, keepdims=True) * (QK_SCALE * w_in)
        w_inter = w_dec * QK_SCALE
        num = s * v + w_inter * cq
        den = s + w_inter * jnp.sum(n0 * q, axis=1, keepdims=True)
        hh = num / jnp.maximum(jnp.abs(den), jnp.exp(-m_t))
        ml = _rms(hh) * gh_ref[...] * og_ref[...]
        cm = ug_ref[...] * (ws0_ref[...] * vsn_ref[...] + bs0_ref[...])
        n_out[...] = w_dec * n0 + w_in * k
        m_out[...] = jnp.broadcast_to(m_t, m_out.shape)
        for hh_static in range(HEADS):
            @pl.when(hd == hh_static)
            def _():
                mix_ref[:, hh_static * HEAD_DIM: (hh_static + 1) * HEAD_DIM] = ml
                mix_ref[:, GROUP_WIDTH + hh_static * HEAD_DIM: GROUP_WIDTH + (hh_static + 1) * HEAD_DIM] = cm


def _mix_sample(a, k, b, gates, m0_pad, n0, c0, bias_row, g_head, ws0_row, bs0_row):
    nb = a.shape[0]
    tb = SAMPLE_TOKENS_PER_STEP

    def head_block(offset):
        return pl.BlockSpec((nb, HEAD_DIM), lambda h, g: (0, offset + h))

    full = pl.BlockSpec((nb, LANES), lambda h, g: (0, 0))
    head_row = pl.BlockSpec((1, HEAD_DIM), lambda h, g: (0, h))
    c_spec = pl.BlockSpec((tb, None, HEAD_DIM, HEAD_DIM), lambda h, g: (g, h, 0, 0))
    return pl.pallas_call(
        _mix_sample_kernel,
        grid=(HEADS, nb // tb),
        in_specs=[
            head_block(0), head_block(0), head_block(HEADS), head_block(2 * HEADS),
            head_block(0), head_block(HEADS),
            full, full, head_block(0), c_spec,
            pl.BlockSpec((1, LANES), lambda h, g: (0, 0)),
            head_row, head_row, head_row,
        ],
        out_specs=[
            pl.BlockSpec((nb, D_MODEL), lambda h, g: (0, 0)),
            c_spec,
            head_block(0),
            head_block(0),
        ],
        out_shape=[
            jax.ShapeDtypeStruct((nb, D_MODEL), F32),
            jax.ShapeDtypeStruct(c0.shape, F32),
            jax.ShapeDtypeStruct((nb, GROUP_WIDTH), F32),
            jax.ShapeDtypeStruct((nb, GROUP_WIDTH), F32),
        ],
        scratch_shapes=[
            pltpu.VMEM((HEAD_DIM, nb), F32),
            pltpu.VMEM((nb, HEAD_DIM), BF16),
            pltpu.VMEM((nb, LANES), F32),
            pltpu.VMEM((HEAD_DIM, nb), F32),
        ],
        compiler_params=pltpu.CompilerParams(
            dimension_semantics=("arbitrary", "arbitrary"), vmem_limit_bytes=VMEM_LIMIT),
        name="mix_sample",
    )(a, k, a, a, b, b, gates, m0_pad, n0, c0, bias_row, g_head, ws0_row, bs0_row)


def _out_kernel(x_ref, mix_ref, gt1_ref, sh2_ref, sc2_ref, gt2_ref, wo_ref, g2_ref, wgu_ref, wdn_ref, gf_ref,
                y_ref):
    x1 = x_ref[...] + gt1_ref[...] * _dot(mix_ref[...].astype(BF16), wo_ref[...])
    h2 = _rms(x1) * g2_ref[...]
    h2 = (h2 * (1.0 + sc2_ref[...]) + sh2_ref[...]).astype(BF16)
    gate = _dot(h2, wgu_ref[:, :D_FF])
    up = _dot(h2, wgu_ref[:, D_FF:])
    act = (gate * _sigmoid(gate) * up).astype(BF16)
    x2 = x1 + gt2_ref[...] * _dot(act, wdn_ref[...])
    y_ref[...] = _rms(x2) * gf_ref[...]


def _output_stage(x, mix, mod, w_out, g2, w_gu, w_dn, g_final, *, tm, per_row):
    groups, t, _ = x.shape
    return pl.pallas_call(
        _out_kernel,
        grid=(groups, t // tm),
        in_specs=[
            pl.BlockSpec((None, tm, D_MODEL), lambda g, i: (g, i, 0)),
            pl.BlockSpec((None, tm, D_MODEL), lambda g, i: (g, i, 0)),
            _mod_spec(2, tm, per_row),
            _mod_spec(3, tm, per_row),
            _mod_spec(4, tm, per_row),
            _mod_spec(5, tm, per_row),
            _const_spec((D_MODEL, D_MODEL)),
            _const_spec((1, D_MODEL)),
            _const_spec((D_MODEL, 2 * D_FF)),
            _const_spec((D_FF, D_MODEL)),
            _const_spec((1, D_MODEL)),
        ],
        out_specs=pl.BlockSpec((None, tm, D_MODEL), lambda g, i: (g, i, 0)),
        out_shape=jax.ShapeDtypeStruct((groups, t, D_MODEL), F32),
        compiler_params=pltpu.CompilerParams(
            dimension_semantics=("arbitrary", "arbitrary"), vmem_limit_bytes=VMEM_LIMIT),
        name="output_stage",
    )(x, mix, mod, mod, mod, mod, w_out, g2, w_gu, w_dn, g_final)


def kernel(x_prompt, x_sample, c_prompt, c_sample, state_mlstm_C, state_mlstm_n, state_mlstm_m, w_ada, b_ada, g_norm1, w_in, b_gate, g_mlstm_head, ln_v_g, ln_v_b, w_s, b_s, w_out, g_norm2, w_gate_up, w_down, g_final):
    depth = w_ada.shape[0]
    assert depth == 1, "single-layer trunk"
    batch, seq, _ = x_prompt.shape
    nb = x_sample.shape[0]
    assert x_sample.shape[1] == 1

    gw = GROUP_WIDTH
    g1 = g_norm1[0].reshape(1, D_MODEL)
    g2 = g_norm2[0].reshape(1, D_MODEL)
    gf = g_final.reshape(1, D_MODEL)
    ln_g = ln_v_g[0].reshape(1, gw)
    ln_b = ln_v_b[0].reshape(1, gw)
    g_head = g_mlstm_head[0].reshape(1, gw)
    bias_row = jnp.pad(b_gate[0], (0, LANES - 2 * HEADS)).reshape(1, LANES)
    tril = jnp.tril(jnp.ones((CHUNK, CHUNK), dtype=bool))
    ws_tril = jnp.where(tril[None], w_s[0], 0.0).astype(BF16)
    bs_rep = jnp.broadcast_to(b_s[0][:, :, None], (HEADS, CHUNK, LANES))
    ws0_row = jnp.repeat(w_s[0][:, 0, 0], HEAD_DIM).reshape(1, gw)
    bs0_row = jnp.repeat(b_s[0][:, 0], HEAD_DIM).reshape(1, gw)

    rows = batch + nb
    rows_pad = -(-rows // 16) * 16
    c_all = jnp.concatenate([c_prompt, c_sample, jnp.zeros((rows_pad - rows, D_MODEL), F32)], axis=0)
    mod, w_proj, w_k, w_kt = _modulation(c_all, w_ada[0], b_ada[0].reshape(1, -1), w_in[0].T)
    mod_p = mod[:batch].reshape(batch, 1, N_MOD * D_MODEL)
    mod_s = mod[batch:rows].reshape(1, nb, N_MOD * D_MODEL)

    a_p, b_p, g_p, kt_p, w_out_b, w_gu_b, w_dn_b = _project(
        x_prompt, mod_p, g1, w_proj, ln_g, ln_b, w_kt, (w_out[0], w_gate_up[0], w_down[0]),
        k_transposed=True, tm=PROJ_TILE, per_row=False, a_dtype=BF16)
    y_p, c_p, n_p, m_p = _prompt_tail(a_p, kt_p, b_p, g_p, x_prompt, mod_p, bias_row, g_head, ws_tril, bs_rep,
                                      w_out_b, g2, w_gu_b, w_dn_b, gf)

    xs = x_sample.reshape(1, nb, D_MODEL)
    a_s, b_s_act, g_s, k_s = _project(xs, mod_s, g1, w_proj, ln_g, ln_b, w_k, k_transposed=False,
                                      tm=nb, per_row=True, a_dtype=F32)
    a_s2, b_s2, g_s2 = a_s[0], b_s_act[0], g_s[0]
    m0_pad = jnp.pad(state_mlstm_m[0], ((0, 0), (0, LANES - HEADS)))
    n0 = state_mlstm_n[0].reshape(nb, gw)
    mix_s, c_s, n_s, m_s = _mix_sample(a_s2, k_s[0], b_s2, g_s2, m0_pad, n0, state_mlstm_C[0],
                                       bias_row, g_head, ws0_row, bs0_row)
    y_s = _output_stage(xs, mix_s.reshape(1, nb, D_MODEL), mod_s, w_out_b, g2, w_gu_b, w_dn_b, gf,
                        tm=nb, per_row=True)

    return (
        y_p,
        y_s.reshape(nb, 1, D_MODEL),
        c_p[None],
        n_p[None],
        m_p[:, :, 0][None],
        c_s[None],
        n_s.reshape(nb, HEADS, HEAD_DIM)[None],
        m_s.reshape(nb, HEADS, HEAD_DIM)[:, :, 0][None],
        a_s2[:, 2 * gw:].reshape(nb, 1, HEADS, HEAD_DIM)[None],
    )
```

```python
import functools
import math

import jax
import jax.numpy as jnp
from jax import lax
from jax.experimental import pallas as pl
from jax.experimental.pallas import tpu as pltpu

F32 = jnp.float32
BF16 = jnp.bfloat16

D_MODEL = 1024
HEADS = 4
HEAD_DIM = 128
GROUP_WIDTH = HEADS * HEAD_DIM
CHUNK = 128
D_FF = 2816
N_MOD = 6
EPS = 1e-6
QK_SCALE = HEAD_DIM ** -0.5
LANES = 128

N_B = 2 * GROUP_WIDTH

VMEM_LIMIT = 56 * 1024 * 1024


def _dot(a, b):
    return jnp.dot(a, b, preferred_element_type=F32)


def _dot_nt(a, b):
    return lax.dot_general(a, b, (((1,), (1,)), ((), ())), preferred_element_type=F32)


def _sigmoid(x):
    return 1.0 / (1.0 + jnp.exp(-x))


def _gelu_tanh(x):
    c = math.sqrt(2.0 / math.pi)
    return x * (0.5 * (1.0 + jnp.tanh(c * (x + 0.044715 * (x * x * x)))))


def _log_sigmoid(x):
    return jnp.minimum(x, 0.0) - jnp.log1p(jnp.exp(-jnp.abs(x)))


def _rms(x):
    return x * lax.rsqrt(jnp.mean(x * x, axis=-1, keepdims=True) + EPS)


def _split3_bf16(x):
    hi = x.astype(BF16)
    r1 = x - hi.astype(F32)
    mid = r1.astype(BF16)
    lo = (r1 - mid.astype(F32)).astype(BF16)
    return hi, mid, lo


MOD_TILE = 1024

IN_Q, IN_K, IN_V, IN_O, IN_GATES = 0, GROUP_WIDTH, 2 * GROUP_WIDTH, 3 * GROUP_WIDTH, 4 * GROUP_WIDTH
IN_U = IN_GATES + 2 * HEADS
IN_VS = IN_U + GROUP_WIDTH
N_IN = IN_VS + GROUP_WIDTH
N_PROJ = 5 * GROUP_WIDTH + LANES


def _mod_kernel(c_ref, w_ref, b_ref, wint_ref, o_ref, wproj_ref, wk_ref, wkt_ref):
    c = c_ref[...]
    a = (c * _sigmoid(c)).astype(BF16)
    o_ref[...] = _dot(a, w_ref[...].astype(BF16)) + b_ref[...]

    @pl.when(pl.program_id(0) == 0)
    def _():
        dst = 0
        for src in (IN_Q, IN_V, IN_VS, IN_O, IN_U):
            for blk in range(GROUP_WIDTH // LANES):
                rows = wint_ref[src + blk * LANES: src + (blk + 1) * LANES, :]
                wproj_ref[:, dst:dst + LANES] = rows.T.astype(BF16)
                dst += LANES
        gate_rows = jnp.concatenate(
            [wint_ref[IN_GATES:IN_U, :], jnp.zeros((LANES - 2 * HEADS, D_MODEL), F32)], axis=0)
        wproj_ref[:, dst:dst + LANES] = gate_rows.T.astype(BF16)
        wkt_ref[...] = wint_ref[IN_K:IN_V, :].astype(BF16)
        for blk in range(GROUP_WIDTH // LANES):
            rows = wint_ref[IN_K + blk * LANES: IN_K + (blk + 1) * LANES, :]
            wk_ref[:, blk * LANES:(blk + 1) * LANES] = rows.T.astype(BF16)


def _modulation(c_all, w_ada, b_ada, w_in_t):
    rows = c_all.shape[0]
    tn = MOD_TILE
    assert w_in_t.shape == (N_IN, D_MODEL)

    def whole(shape):
        return pl.BlockSpec(shape, lambda j: (0, 0))

    return pl.pallas_call(
        _mod_kernel,
        grid=(N_MOD * D_MODEL // tn,),
        in_specs=[
            whole((rows, D_MODEL)),
            pl.BlockSpec((D_MODEL, tn), lambda j: (0, j)),
            pl.BlockSpec((1, tn), lambda j: (0, j)),
            pl.BlockSpec((N_IN, D_MODEL), lambda j: (0, 0), pipeline_mode=pl.Buffered(1)),
        ],
        out_specs=[
            pl.BlockSpec((rows, tn), lambda j: (0, j)),
            whole((D_MODEL, N_PROJ)),
            whole((D_MODEL, GROUP_WIDTH)),
            whole((GROUP_WIDTH, D_MODEL)),
        ],
        out_shape=[
            jax.ShapeDtypeStruct((rows, N_MOD * D_MODEL), F32),
            jax.ShapeDtypeStruct((D_MODEL, N_PROJ), BF16),
            jax.ShapeDtypeStruct((D_MODEL, GROUP_WIDTH), BF16),
            jax.ShapeDtypeStruct((GROUP_WIDTH, D_MODEL), BF16),
        ],
        compiler_params=pltpu.CompilerParams(dimension_semantics=("arbitrary",), vmem_limit_bytes=VMEM_LIMIT),
        name="modulation",
    )(c_all, w_ada, b_ada, w_in_t)


def _mod_spec(piece, tm, per_row):
    if per_row:
        return pl.BlockSpec((None, tm, D_MODEL), lambda g, t: (g, t, piece))
    return pl.BlockSpec((None, 1, D_MODEL), lambda g, t: (g, 0, piece))


def _const_spec(shape):
    nd = len(shape)
    return pl.BlockSpec(shape, lambda g, t: (0,) * nd, pipeline_mode=pl.Buffered(1))


PROJ_TILE = 1024


def _proj_body(x_ref, sh_ref, sc_ref, g1_ref, w_ref, lng_ref, lnb_ref, a_ref, b_ref, g_ref):
    n_a = a_ref.shape[1]
    n_copy = n_a - GROUP_WIDTH
    x = x_ref[...]
    h = _rms(x) * g1_ref[...]
    h = (h * (1.0 + sc_ref[...]) + sh_ref[...]).astype(BF16)
    p = _dot(h, w_ref[...])
    a_ref[:, :n_copy] = p[:, :n_copy].astype(a_ref.dtype)
    vs = _gelu_tanh(p[:, n_copy:n_a])
    for hd in range(HEADS):
        sl = slice(hd * HEAD_DIM, (hd + 1) * HEAD_DIM)
        v = vs[:, sl]
        mu = jnp.mean(v, axis=-1, keepdims=True)
        vc = v - mu
        var = jnp.mean(vc * vc, axis=-1, keepdims=True)
        y = vc * lax.rsqrt(var + EPS) * lng_ref[:, sl] + lnb_ref[:, sl]
        a_ref[:, n_copy + hd * HEAD_DIM: n_copy + (hd + 1) * HEAD_DIM] = y.astype(a_ref.dtype)
    b_ref[:, :GROUP_WIDTH] = _sigmoid(p[:, n_a: n_a + GROUP_WIDTH])
    b_ref[:, GROUP_WIDTH:] = _gelu_tanh(p[:, n_a + GROUP_WIDTH: n_a + N_B])
    g_ref[...] = p[:, n_a + N_B:]
    return h


def _proj_kernel(*refs, k_transposed, n_cast):
    ins, outs = refs[:8 + n_cast], refs[8 + n_cast:]
    wk_ref, k_ref = ins[7], outs[3]
    h = _proj_body(*ins[:7], *outs[:3])
    if k_transposed:
        k_ref[...] = _dot_nt(wk_ref[...], h).astype(k_ref.dtype)
    else:
        k_ref[...] = _dot(h, wk_ref[...]).astype(k_ref.dtype)
    for src, dst in zip(ins[8:], outs[4:]):
        dst[...] = src[...].astype(dst.dtype)


def _project(x, mod, g1, w_proj, ln_g, ln_b, w_k, cast_weights=(), *, k_transposed, tm, per_row, a_dtype):
    groups, t, _ = x.shape
    n_proj = w_proj.shape[1]
    n_a = n_proj - N_B - LANES
    steps_per_group = t // tm
    n_steps = groups * steps_per_group
    in_specs = [
        pl.BlockSpec((None, tm, D_MODEL), lambda g, i: (g, i, 0)),
        _mod_spec(0, tm, per_row),
        _mod_spec(1, tm, per_row),
        _const_spec((1, D_MODEL)),
        _const_spec((D_MODEL, n_proj)),
        _const_spec((1, GROUP_WIDTH)),
        _const_spec((1, GROUP_WIDTH)),
    ]
    out_specs = [
        pl.BlockSpec((None, tm, n_a), lambda g, i: (g, i, 0)),
        pl.BlockSpec((None, tm, N_B), lambda g, i: (g, i, 0)),
        pl.BlockSpec((None, tm, LANES), lambda g, i: (g, i, 0)),
    ]
    out_shape = [
        jax.ShapeDtypeStruct((groups, t, n_a), a_dtype),
        jax.ShapeDtypeStruct((groups, t, N_B), F32),
        jax.ShapeDtypeStruct((groups, t, LANES), F32),
    ]
    args = [x, mod, mod, g1, w_proj, ln_g, ln_b, w_k]
    in_specs.append(_const_spec(w_k.shape))
    if k_transposed:
        out_specs.append(pl.BlockSpec((None, GROUP_WIDTH, tm), lambda g, i: (g, 0, i)))
        out_shape.append(jax.ShapeDtypeStruct((groups, GROUP_WIDTH, t), a_dtype))
    else:
        out_specs.append(pl.BlockSpec((None, tm, GROUP_WIDTH), lambda g, i: (g, i, 0)))
        out_shape.append(jax.ShapeDtypeStruct((groups, t, GROUP_WIDTH), a_dtype))
    for w in cast_weights:
        rows, cols = w.shape
        assert rows % (n_steps * 16) == 0, "row block must be a whole number of bf16 sublane tiles"
        spec = pl.BlockSpec((rows // n_steps, cols), lambda g, i: (g * steps_per_group + i, 0))
        in_specs.append(spec)
        out_specs.append(spec)
        out_shape.append(jax.ShapeDtypeStruct((rows, cols), BF16))
        args.append(w)
    return pl.pallas_call(
        functools.partial(_proj_kernel, k_transposed=k_transposed, n_cast=len(cast_weights)),
        grid=(groups, t // tm),
        in_specs=in_specs,
        out_specs=out_specs,
        out_shape=out_shape,
        compiler_params=pltpu.CompilerParams(
            dimension_semantics=("arbitrary", "arbitrary"), vmem_limit_bytes=VMEM_LIMIT),
        name="project",
    )(*args)


MIX_TILE = 512


def _dot3_rhs(lhs_bf16, rhs_f32):
    hi, mid, lo = _split3_bf16(rhs_f32)
    return _dot(lhs_bf16, hi) + _dot(lhs_bf16, mid) + _dot(lhs_bf16, lo)


def _dot3_lhs(lhs_f32, rhs_bf16):
    hi, mid, lo = _split3_bf16(lhs_f32)
    return _dot(hi, rhs_bf16) + _dot(mid, rhs_bf16) + _dot(lo, rhs_bf16)


FF_CHUNKS = (768, 768, 768, 512)
MIXER_LOOKAHEAD = 2
OUT_ROW_BLOCKS = 2


def _gate_rows(g_ref, bias_ref, tril, triu, n_chunks):
    pre = [g_ref[c * CHUNK:(c + 1) * CHUNK, :] + bias_ref[...] for c in range(n_chunks)]
    bc_all = _dot3_rhs(tril, jnp.concatenate([_log_sigmoid(p) for p in pre], axis=1))
    rows_all = jnp.concatenate([p.T[0:2 * HEADS, :] for p in pre], axis=0)
    b_rows_all = _dot3_lhs(_log_sigmoid(rows_all), triu)
    bc, arow, blast = [], [], []
    for c in range(n_chunks):
        bc.append(bc_all[:, c * LANES:(c + 1) * LANES])
        rows = rows_all[c * 2 * HEADS:(c + 1) * 2 * HEADS, :]
        b_rows = b_rows_all[c * 2 * HEADS:(c + 1) * 2 * HEADS, :]
        arow.append([rows[hd:hd + 1, :] - b_rows[HEADS + hd:HEADS + hd + 1, :] for hd in range(HEADS)])
        blast.append([jnp.min(b_rows[HEADS + hd:HEADS + hd + 1, :], axis=1, keepdims=True) for hd in range(HEADS)])
    return bc, arow, blast


def _mlstm_local(q, kt, a_row, b_last, causal):
    amat = jnp.where(causal, a_row, -jnp.inf)
    m_row = jnp.max(amat, axis=1, keepdims=True)
    s_loc = (_dot(q, kt) * jnp.exp(amat - m_row)).astype(BF16)
    g_row = b_last + a_row
    g_loc = jnp.max(g_row, axis=1, keepdims=True)
    kw = (kt.astype(F32) * jnp.exp(g_row - g_loc)).astype(BF16)
    return m_row, s_loc, g_loc, kw


def _mlstm_readout(local, q, v, og, g_head, b_last, b_rep, cta, m_prev, ones_blk):
    m_row, s_loc, g_loc, kw = local
    va = jnp.concatenate([v, ones_blk], axis=1)
    nd_loc = _dot(s_loc, va)
    u_aug = _dot(kw, va)
    inter = _dot(q, cta.astype(BF16))
    mm = jnp.maximum(m_prev, m_row)
    f_loc = jnp.exp(m_row - mm) * QK_SCALE
    f_int = jnp.exp(m_prev - mm) * QK_SCALE
    nd = f_loc * nd_loc + f_int * inter
    clamp = jnp.exp(-(b_rep + mm))
    hh = nd[:, :HEAD_DIM] / jnp.maximum(jnp.abs(nd[:, HEAD_DIM:]), clamp)
    ml = _rms(hh) * g_head * og

    dec = b_last + m_prev
    m_new = jnp.maximum(dec, g_loc)
    cta_new = jnp.exp(dec - m_new) * cta + jnp.exp(g_loc - m_new) * u_aug
    return ml, cta_new, m_new


def _prompt_tail_kernel(a_ref, kt_ref, b_ref, g_ref, x_ref, gt1_ref, sh2_ref, sc2_ref, gt2_ref,
                        bias_ref, gh_ref, ws_ref, bs_ref, wo_ref, g2_ref, wgu_ref, wdn_ref, gf_ref,
                        y_ref, c_out, n_out, m_out,
                        mix_ref, cta_ref, m_ref, *, tiles_per_seq):
    step = pl.program_id(0)
    n_tiles = pl.num_programs(0) - 1
    refs = (a_ref, kt_ref, b_ref, g_ref, x_ref, gt1_ref, sh2_ref, sc2_ref, gt2_ref,
            bias_ref, gh_ref, ws_ref, bs_ref, wo_ref, g2_ref, wgu_ref, wdn_ref, gf_ref,
            y_ref, mix_ref, cta_ref, m_ref)

    @pl.when(step % tiles_per_seq == 0)
    def _():
        cta_ref[...] = jnp.zeros_like(cta_ref)
        m_ref[...] = jnp.zeros_like(m_ref)

    @pl.when(step == 0)
    def _():
        _prompt_tail_body(refs, with_mixer=True, with_out=False)

    @pl.when(jnp.logical_and(step > 0, step < n_tiles))
    def _():
        _prompt_tail_body(refs, with_mixer=True, with_out=True)

    @pl.when(step == n_tiles)
    def _():
        _prompt_tail_body(refs, with_mixer=False, with_out=True)

    @pl.when(jnp.logical_and(step % tiles_per_seq == tiles_per_seq - 1, step < n_tiles))
    def _():
        for hd in range(HEADS):
            cta = cta_ref[hd]
            c_out[hd] = cta[:, :HEAD_DIM].T
            n_out[hd: hd + 1, :] = cta[:, HEAD_DIM:].T[0:1, :]
            m_out[hd: hd + 1, :] = m_ref[hd][0:1, :]


def _prompt_tail_body(refs, *, with_mixer, with_out):
    (a_ref, kt_ref, b_ref, g_ref, x_ref, gt1_ref, sh2_ref, sc2_ref, gt2_ref,
     bias_ref, gh_ref, ws_ref, bs_ref, wo_ref, g2_ref, wgu_ref, wdn_ref, gf_ref,
     y_ref, mix_ref, cta_ref, m_ref) = refs
    tm = a_ref.shape[0]
    n_chunks = tm // CHUNK
    n_copy = a_ref.shape[1] - GROUP_WIDTH

    row = lax.broadcasted_iota(jnp.int32, (CHUNK, CHUNK), 0)
    col = lax.broadcasted_iota(jnp.int32, (CHUNK, CHUNK), 1)
    causal = row >= col
    tril = jnp.where(causal, 1.0, 0.0).astype(BF16)
    triu = jnp.where(row <= col, 1.0, 0.0).astype(BF16)
    ones_blk = jnp.ones((CHUNK, HEAD_DIM), BF16)

    blocks = [slice(r0, r0 + tm // OUT_ROW_BLOCKS) for r0 in range(0, tm, tm // OUT_ROW_BLOCKS)] if with_out else []
    outproj = [_dot(mix_ref[rs, :], wo_ref[...]) for rs in blocks]
    x1, h2 = [], []
    for rs, op in zip(blocks, outproj):
        x1.append(x_ref[rs, :] + gt1_ref[...] * op)
        h = _rms(x1[-1]) * g2_ref[...]
        h2.append((h * (1.0 + sc2_ref[...]) + sh2_ref[...]).astype(BF16))

    if with_mixer:
        bc, arow, blast = _gate_rows(g_ref, bias_ref, tril, triu, n_chunks)
        state = [(cta_ref[hd], jnp.max(m_ref[hd][0:1, :], axis=1, keepdims=True)) for hd in range(HEADS)]
    sg_all = [None] * HEADS

    def mixer_local(c, hd):
        rs = slice(c * CHUNK, (c + 1) * CHUNK)
        sl = slice(hd * HEAD_DIM, (hd + 1) * HEAD_DIM)
        return _mlstm_local(a_ref[rs, sl], kt_ref[sl, rs], arow[c][hd], blast[c][hd], causal)

    def mixer_readout(c, hd, local):
        rs = slice(c * CHUNK, (c + 1) * CHUNK)
        sl = slice(hd * HEAD_DIM, (hd + 1) * HEAD_DIM)
        if c == 0:
            vsn_all = jnp.concatenate(
                [a_ref[cc * CHUNK:(cc + 1) * CHUNK, n_copy + hd * HEAD_DIM: n_copy + (hd + 1) * HEAD_DIM]
                 for cc in range(n_chunks)], axis=1)
            sg_all[hd] = _dot(ws_ref[hd], vsn_all)
        cta, m_prev = state[hd]
        b_rep = jnp.broadcast_to(bc[c][:, HEADS + hd: HEADS + hd + 1], (CHUNK, HEAD_DIM))
        ml, cta, m_prev = _mlstm_readout(
            local, a_ref[rs, sl], a_ref[rs, GROUP_WIDTH + hd * HEAD_DIM: GROUP_WIDTH + (hd + 1) * HEAD_DIM],
            b_ref[rs, sl], gh_ref[:, sl], blast[c][hd], b_rep, cta, m_prev, ones_blk)
        state[hd] = (cta, m_prev)
        mix_ref[rs, sl] = ml.astype(mix_ref.dtype)
        sg = sg_all[hd][:, c * CHUNK:(c + 1) * CHUNK] + bs_ref[hd]
        ug = b_ref[rs, GROUP_WIDTH + hd * HEAD_DIM: GROUP_WIDTH + (hd + 1) * HEAD_DIM]
        mix_ref[rs, GROUP_WIDTH + hd * HEAD_DIM: GROUP_WIDTH + (hd + 1) * HEAD_DIM] = (ug * sg).astype(mix_ref.dtype)

    pieces = [(c, hd) for c in range(n_chunks) for hd in range(HEADS)] if with_mixer else []
    per_ff = -(-len(pieces) // len(FF_CHUNKS))
    ahead = [mixer_local(*pieces[p]) for p in range(min(MIXER_LOOKAHEAD, len(pieces)))]
    acc = [None] * len(blocks)
    f0 = 0
    for j, width in enumerate(FF_CHUNKS):
        gate_up = [(_dot(h, wgu_ref[:, f0:f0 + width]), _dot(h, wgu_ref[:, D_FF + f0:D_FF + f0 + width]))
                   for h in h2]
        for idx in range(j * per_ff, min((j + 1) * per_ff, len(pieces))):
            if idx + MIXER_LOOKAHEAD < len(pieces):
                ahead.append(mixer_local(*pieces[idx + MIXER_LOOKAHEAD]))
            mixer_readout(*pieces[idx], ahead.pop(0))
        for r, (gate, up) in enumerate(gate_up):
            act = (gate * _sigmoid(gate) * up).astype(BF16)
            part = _dot(act, wdn_ref[f0:f0 + width, :])
            acc[r] = part if acc[r] is None else acc[r] + part
            if j + 1 == len(FF_CHUNKS):
                x2 = x1[r] + gt2_ref[...] * acc[r]
                y_ref[blocks[r], :] = _rms(x2) * gf_ref[...]
        f0 += width

    if with_mixer:
        for hd in range(HEADS):
            cta_ref[hd] = state[hd][0]
            m_ref[hd] = jnp.broadcast_to(state[hd][1], m_ref.shape[1:])


def _prompt_tail(a, kt, b, gates, x, mod, bias_row, g_head, ws_tril, bs_rep, w_out, g2, w_gu, w_dn, g_final):
    groups, t, n_a = a.shape
    tm = MIX_TILE
    tps = t // tm
    n_tiles = groups * tps
    assert sum(FF_CHUNKS) == D_FF

    def cur(i):
        return jnp.minimum(i, n_tiles - 1)

    def prev(i):
        return jnp.maximum(i - 1, 0)

    def rows(tile, width):
        return pl.BlockSpec((None, tm, width), lambda i: (tile(i) // tps, tile(i) % tps, 0))

    def mod_piece(piece):
        return pl.BlockSpec((None, 1, D_MODEL), lambda i: (prev(i) // tps, 0, piece))

    def const(shape):
        nd = len(shape)
        return pl.BlockSpec(shape, lambda i: (0,) * nd, pipeline_mode=pl.Buffered(1))

    def per_seq(shape):
        nd = len(shape)
        return pl.BlockSpec((None,) + shape, lambda i: (cur(i) // tps,) + (0,) * nd)

    return pl.pallas_call(
        functools.partial(_prompt_tail_kernel, tiles_per_seq=tps),
        grid=(n_tiles + 1,),
        in_specs=[
            rows(cur, n_a),
            pl.BlockSpec((None, GROUP_WIDTH, tm), lambda i: (cur(i) // tps, 0, cur(i) % tps)),
            rows(cur, N_B),
            rows(cur, LANES),
            rows(prev, D_MODEL),
            mod_piece(2), mod_piece(3), mod_piece(4), mod_piece(5),
            const((1, LANES)),
            const((1, GROUP_WIDTH)),
            const((HEADS, CHUNK, CHUNK)),
            const((HEADS, CHUNK, LANES)),
            const((D_MODEL, D_MODEL)),
            const((1, D_MODEL)),
            const((D_MODEL, 2 * D_FF)),
            const((D_FF, D_MODEL)),
            const((1, D_MODEL)),
        ],
        out_specs=[
            rows(prev, D_MODEL),
            per_seq((HEADS, HEAD_DIM, HEAD_DIM)),
            per_seq((HEADS, HEAD_DIM)),
            per_seq((HEADS, LANES)),
        ],
        out_shape=[
            jax.ShapeDtypeStruct((groups, t, D_MODEL), F32),
            jax.ShapeDtypeStruct((groups, HEADS, HEAD_DIM, HEAD_DIM), F32),
            jax.ShapeDtypeStruct((groups, HEADS, HEAD_DIM), F32),
            jax.ShapeDtypeStruct((groups, HEADS, LANES), F32),
        ],
        scratch_shapes=[
            pltpu.VMEM((tm, D_MODEL), BF16),
            pltpu.VMEM((HEADS, HEAD_DIM, 2 * HEAD_DIM), F32),
            pltpu.VMEM((HEADS, 8, LANES), F32),
        ],
        compiler_params=pltpu.CompilerParams(
            dimension_semantics=("arbitrary",), vmem_limit_bytes=VMEM_LIMIT),
        name="prompt_tail",
    )(a, kt, b, gates, x, mod, mod, mod, mod, bias_row, g_head, ws_tril, bs_rep, w_out, g2, w_gu, w_dn, g_final)


SAMPLE_TOKENS_PER_STEP = 64
SAMPLE_UNROLL = 8


def _mix_sample_kernel(q_ref, k_ref, v_ref, vsn_ref, og_ref, ug_ref, g_ref, m0_ref, n0_ref, c_ref,
                       bias_ref, gh_ref, ws0_ref, bs0_ref,
                       mix_ref, c_out, n_out, m_out,
                       vt_ref, kp_ref, wd_ref, cqt_ref):
    hd = pl.program_id(0)
    grp = pl.program_id(1)
    nb = q_ref.shape[0]
    lane = lax.broadcasted_iota(jnp.int32, (nb, LANES), 1)

    def gate_terms():
        pre = g_ref[...] + bias_ref[...]
        i_pre = jnp.sum(jnp.where(lane == hd, pre, 0.0), axis=1, keepdims=True)
        f_pre = jnp.sum(jnp.where(lane == hd + HEADS, pre, 0.0), axis=1, keepdims=True)
        m_prev = jnp.sum(jnp.where(lane == hd, m0_ref[...], 0.0), axis=1, keepdims=True)
        inter = _log_sigmoid(f_pre) + m_prev
        m_t = jnp.maximum(inter, i_pre)
        return m_t, jnp.exp(i_pre - m_t), jnp.exp(inter - m_t)

    @pl.when(grp == 0)
    def _():
        _, w_in, w_dec = gate_terms()
        vt_ref[...] = v_ref[...].T
        kp_ref[...] = (w_in * k_ref[...]).astype(BF16)
        wd_ref[...] = jnp.broadcast_to(w_dec, wd_ref.shape)
        cqt_ref[...] = jnp.zeros_like(cqt_ref)

    lane_sq = lax.broadcasted_iota(jnp.int32, (HEAD_DIM, LANES), 1)
    tokens = c_ref.shape[0]

    def body(r, cqt):
        tok = grp * tokens + r
        q_row = q_ref[pl.ds(tok, 1), :]
        c_b = c_ref[r]
        sel = lane_sq == tok
        cq_col = jnp.sum(c_b * q_row, axis=1, keepdims=True)
        v_sel = jnp.where(sel, vt_ref[...], 0.0).astype(BF16)
        outer = _dot(v_sel, kp_ref[...])
        c_out[r] = wd_ref[pl.ds(tok, 1), :] * c_b + outer
        return jnp.where(sel, cq_col, cqt)

    cqt_ref[...] = lax.fori_loop(0, tokens, body, cqt_ref[...], unroll=SAMPLE_UNROLL)

    @pl.when(grp == pl.num_programs(1) - 1)
    def _():
        m_t, w_in, w_dec = gate_terms()
        q = q_ref[...]
        k = k_ref[...]
        v = v_ref[...]
        n0 = n0_ref[...]
        cq = cqt_ref[...].T
        s = jnp.sum(q * k, axis=1, keepdims=True) * (QK_SCALE * w_in)
        w_inter = w_dec * QK_SCALE
        num = s * v + w_inter * cq
        den = s + w_inter * jnp.sum(n0 * q, axis=1, keepdims=True)
        hh = num / jnp.maximum(jnp.abs(den), jnp.exp(-m_t))
        ml = _rms(hh) * gh_ref[...] * og_ref[...]
        cm = ug_ref[...] * (ws0_ref[...] * vsn_ref[...] + bs0_ref[...])
        n_out[...] = w_dec * n0 + w_in * k
        m_out[...] = jnp.broadcast_to(m_t, m_out.shape)
        for hh_static in range(HEADS):
            @pl.when(hd == hh_static)
            def _():
                mix_ref[:, hh_static * HEAD_DIM: (hh_static + 1) * HEAD_DIM] = ml
                mix_ref[:, GROUP_WIDTH + hh_static * HEAD_DIM: GROUP_WIDTH + (hh_static + 1) * HEAD_DIM] = cm


def _mix_sample(a, k, b, gates, m0_pad, n0, c0, bias_row, g_head, ws0_row, bs0_row):
    nb = a.shape[0]
    tb = SAMPLE_TOKENS_PER_STEP

    def head_block(offset):
        return pl.BlockSpec((nb, HEAD_DIM), lambda h, g: (0, offset + h))

    full = pl.BlockSpec((nb, LANES), lambda h, g: (0, 0))
    head_row = pl.BlockSpec((1, HEAD_DIM), lambda h, g: (0, h))
    c_spec = pl.BlockSpec((tb, None, HEAD_DIM, HEAD_DIM), lambda h, g: (g, h, 0, 0))
    return pl.pallas_call(
        _mix_sample_kernel,
        grid=(HEADS, nb // tb),
        in_specs=[
            head_block(0), head_block(0), head_block(HEADS), head_block(2 * HEADS),
            head_block(0), head_block(HEADS),
            full, full, head_block(0), c_spec,
            pl.BlockSpec((1, LANES), lambda h, g: (0, 0)),
            head_row, head_row, head_row,
        ],
        out_specs=[
            pl.BlockSpec((nb, D_MODEL), lambda h, g: (0, 0)),
            c_spec,
            head_block(0),
            head_block(0),
        ],
        out_shape=[
            jax.ShapeDtypeStruct((nb, D_MODEL), F32),
            jax.ShapeDtypeStruct(c0.shape, F32),
            jax.ShapeDtypeStruct((nb, GROUP_WIDTH), F32),
            jax.ShapeDtypeStruct((nb, GROUP_WIDTH), F32),
        ],
        scratch_shapes=[
            pltpu.VMEM((HEAD_DIM, nb), F32),
            pltpu.VMEM((nb, HEAD_DIM), BF16),
            pltpu.VMEM((nb, LANES), F32),
            pltpu.VMEM((HEAD_DIM, nb), F32),
        ],
        compiler_params=pltpu.CompilerParams(
            dimension_semantics=("arbitrary", "arbitrary"), vmem_limit_bytes=VMEM_LIMIT),
        name="mix_sample",
    )(a, k, a, a, b, b, gates, m0_pad, n0, c0, bias_row, g_head, ws0_row, bs0_row)


def _out_kernel(x_ref, mix_ref, gt1_ref, sh2_ref, sc2_ref, gt2_ref, wo_ref, g2_ref, wgu_ref, wdn_ref, gf_ref,
                y_ref):
    x1 = x_ref[...] + gt1_ref[...] * _dot(mix_ref[...].astype(BF16), wo_ref[...])
    h2 = _rms(x1) * g2_ref[...]
    h2 = (h2 * (1.0 + sc2_ref[...]) + sh2_ref[...]).astype(BF16)
    gate = _dot(h2, wgu_ref[:, :D_FF])
    up = _dot(h2, wgu_ref[:, D_FF:])
    act = (gate * _sigmoid(gate) * up).astype(BF16)
    x2 = x1 + gt2_ref[...] * _dot(act, wdn_ref[...])
    y_ref[...] = _rms(x2) * gf_ref[...]


def _output_stage(x, mix, mod, w_out, g2, w_gu, w_dn, g_final, *, tm, per_row):
    groups, t, _ = x.shape
    return pl.pallas_call(
        _out_kernel,
        grid=(groups, t // tm),
        in_specs=[
            pl.BlockSpec((None, tm, D_MODEL), lambda g, i: (g, i, 0)),
            pl.BlockSpec((None, tm, D_MODEL), lambda g, i: (g, i, 0)),
            _mod_spec(2, tm, per_row),
            _mod_spec(3, tm, per_row),
            _mod_spec(4, tm, per_row),
            _mod_spec(5, tm, per_row),
            _const_spec((D_MODEL, D_MODEL)),
            _const_spec((1, D_MODEL)),
            _const_spec((D_MODEL, 2 * D_FF)),
            _const_spec((D_FF, D_MODEL)),
            _const_spec((1, D_MODEL)),
        ],
        out_specs=pl.BlockSpec((None, tm, D_MODEL), lambda g, i: (g, i, 0)),
        out_shape=jax.ShapeDtypeStruct((groups, t, D_MODEL), F32),
        compiler_params=pltpu.CompilerParams(
            dimension_semantics=("arbitrary", "arbitrary"), vmem_limit_bytes=VMEM_LIMIT),
        name="output_stage",
    )(x, mix, mod, mod, mod, mod, w_out, g2, w_gu, w_dn, g_final)


def kernel(x_prompt, x_sample, c_prompt, c_sample, state_mlstm_C, state_mlstm_n, state_mlstm_m, w_ada, b_ada, g_norm1, w_in, b_gate, g_mlstm_head, ln_v_g, ln_v_b, w_s, b_s, w_out, g_norm2, w_gate_up, w_down, g_final):
    depth = w_ada.shape[0]
    assert depth == 1, "single-layer trunk"
    batch, seq, _ = x_prompt.shape
    nb = x_sample.shape[0]
    assert x_sample.shape[1] == 1

    gw = GROUP_WIDTH
    g1 = g_norm1[0].reshape(1, D_MODEL)
    g2 = g_norm2[0].reshape(1, D_MODEL)
    gf = g_final.reshape(1, D_MODEL)
    ln_g = ln_v_g[0].reshape(1, gw)
    ln_b = ln_v_b[0].reshape(1, gw)
    g_head = g_mlstm_head[0].reshape(1, gw)
    bias_row = jnp.pad(b_gate[0], (0, LANES - 2 * HEADS)).reshape(1, LANES)
    tril = jnp.tril(jnp.ones((CHUNK, CHUNK), dtype=bool))
    ws_tril = jnp.where(tril[None], w_s[0], 0.0).astype(BF16)
    bs_rep = jnp.broadcast_to(b_s[0][:, :, None], (HEADS, CHUNK, LANES))
    ws0_row = jnp.repeat(w_s[0][:, 0, 0], HEAD_DIM).reshape(1, gw)
    bs0_row = jnp.repeat(b_s[0][:, 0], HEAD_DIM).reshape(1, gw)

    rows = batch + nb
    rows_pad = -(-rows // 16) * 16
    c_all = jnp.concatenate([c_prompt, c_sample, jnp.zeros((rows_pad - rows, D_MODEL), F32)], axis=0)
    mod, w_proj, w_k, w_kt = _modulation(c_all, w_ada[0], b_ada[0].reshape(1, -1), w_in[0].T)
    mod_p = mod[:batch].reshape(batch, 1, N_MOD * D_MODEL)
    mod_s = mod[batch:rows].reshape(1, nb, N_MOD * D_MODEL)

    a_p, b_p, g_p, kt_p, w_out_b, w_gu_b, w_dn_b = _project(
        x_prompt, mod_p, g1, w_proj, ln_g, ln_b, w_kt, (w_out[0], w_gate_up[0], w_down[0]),
        k_transposed=True, tm=PROJ_TILE, per_row=False, a_dtype=BF16)
    y_p, c_p, n_p, m_p = _prompt_tail(a_p, kt_p, b_p, g_p, x_prompt, mod_p, bias_row, g_head, ws_tril, bs_rep,
                                      w_out_b, g2, w_gu_b, w_dn_b, gf)

    xs = x_sample.reshape(1, nb, D_MODEL)
    a_s, b_s_act, g_s, k_s = _project(xs, mod_s, g1, w_proj, ln_g, ln_b, w_k, k_transposed=False,
                                      tm=nb, per_row=True, a_dtype=F32)
    a_s2, b_s2, g_s2 = a_s[0], b_s_act[0], g_s[0]
    m0_pad = jnp.pad(state_mlstm_m[0], ((0, 0), (0, LANES - HEADS)))
    n0 = state_mlstm_n[0].reshape(nb, gw)
    mix_s, c_s, n_s, m_s = _mix_sample(a_s2, k_s[0], b_s2, g_s2, m0_pad, n0, state_mlstm_C[0],
                                       bias_row, g_head, ws0_row, bs0_row)
    y_s = _output_stage(xs, mix_s.reshape(1, nb, D_MODEL), mod_s, w_out_b, g2, w_gu_b, w_dn_b, gf,
                        tm=nb, per_row=True)

    return (
        y_p,
        y_s.reshape(nb, 1, D_MODEL),
        c_p[None],
        n_p[None],
        m_p[:, :, 0][None],
        c_s[None],
        n_s.reshape(nb, HEADS, HEAD_DIM)[None],
        m_s.reshape(nb, HEADS, HEAD_DIM)[:, :, 0][None],
        a_s2[:, 2 * gw:].reshape(nb, 1, HEADS, HEAD_DIM)[None],
    )
```

```python
import functools
import math

import jax
import jax.numpy as jnp
from jax import lax
from jax.experimental import pallas as pl
from jax.experimental.pallas import tpu as pltpu

F32 = jnp.float32
BF16 = jnp.bfloat16

D_MODEL = 1024
HEADS = 4
HEAD_DIM = 128
GROUP_WIDTH = HEADS * HEAD_DIM
CHUNK = 128
D_FF = 2816
N_MOD = 6
EPS = 1e-6
QK_SCALE = HEAD_DIM ** -0.5
LANES = 128

N_B = 2 * GROUP_WIDTH

VMEM_LIMIT = 56 * 1024 * 1024


def _dot(a, b):
    return jnp.dot(a, b, preferred_element_type=F32)


def _dot_nt(a, b):
    return lax.dot_general(a, b, (((1,), (1,)), ((), ())), preferred_element_type=F32)


def _sigmoid(x):
    return 1.0 / (1.0 + jnp.exp(-x))


def _gelu_tanh(x):
    c = math.sqrt(2.0 / math.pi)
    return x * (0.5 * (1.0 + jnp.tanh(c * (x + 0.044715 * (x * x * x)))))


def _log_sigmoid(x):
    return jnp.minimum(x, 0.0) - jnp.log1p(jnp.exp(-jnp.abs(x)))


def _rms(x):
    return x * lax.rsqrt(jnp.mean(x * x, axis=-1, keepdims=True) + EPS)


def _split3_bf16(x):
    hi = x.astype(BF16)
    r1 = x - hi.astype(F32)
    mid = r1.astype(BF16)
    lo = (r1 - mid.astype(F32)).astype(BF16)
    return hi, mid, lo


MOD_TILE = 1024

IN_Q, IN_K, IN_V, IN_O, IN_GATES = 0, GROUP_WIDTH, 2 * GROUP_WIDTH, 3 * GROUP_WIDTH, 4 * GROUP_WIDTH
IN_U = IN_GATES + 2 * HEADS
IN_VS = IN_U + GROUP_WIDTH
N_IN = IN_VS + GROUP_WIDTH
N_PROJ = 5 * GROUP_WIDTH + LANES


def _mod_kernel(c_ref, w_ref, b_ref, wint_ref, o_ref, wproj_ref, wk_ref, wkt_ref):
    c = c_ref[...]
    a = (c * _sigmoid(c)).astype(BF16)
    o_ref[...] = _dot(a, w_ref[...].astype(BF16)) + b_ref[...]

    @pl.when(pl.program_id(0) == 0)
    def _():
        dst = 0
        for src in (IN_Q, IN_V, IN_VS, IN_O, IN_U):
            for blk in range(GROUP_WIDTH // LANES):
                rows = wint_ref[src + blk * LANES: src + (blk + 1) * LANES, :]
                wproj_ref[:, dst:dst + LANES] = rows.T.astype(BF16)
                dst += LANES
        gate_rows = jnp.concatenate(
            [wint_ref[IN_GATES:IN_U, :], jnp.zeros((LANES - 2 * HEADS, D_MODEL), F32)], axis=0)
        wproj_ref[:, dst:dst + LANES] = gate_rows.T.astype(BF16)
        wkt_ref[...] = wint_ref[IN_K:IN_V, :].astype(BF16)
        for blk in range(GROUP_WIDTH // LANES):
            rows = wint_ref[IN_K + blk * LANES: IN_K + (blk + 1) * LANES, :]
            wk_ref[:, blk * LANES:(blk + 1) * LANES] = rows.T.astype(BF16)


def _modulation(c_all, w_ada, b_ada, w_in_t):
    rows = c_all.shape[0]
    tn = MOD_TILE
    assert w_in_t.shape == (N_IN, D_MODEL)

    def whole(shape):
        return pl.BlockSpec(shape, lambda j: (0, 0))

    return pl.pallas_call(
        _mod_kernel,
        grid=(N_MOD * D_MODEL // tn,),
        in_specs=[
            whole((rows, D_MODEL)),
            pl.BlockSpec((D_MODEL, tn), lambda j: (0, j)),
            pl.BlockSpec((1, tn), lambda j: (0, j)),
            pl.BlockSpec((N_IN, D_MODEL), lambda j: (0, 0), pipeline_mode=pl.Buffered(1)),
        ],
        out_specs=[
            pl.BlockSpec((rows, tn), lambda j: (0, j)),
            whole((D_MODEL, N_PROJ)),
            whole((D_MODEL, GROUP_WIDTH)),
            whole((GROUP_WIDTH, D_MODEL)),
        ],
        out_shape=[
            jax.ShapeDtypeStruct((rows, N_MOD * D_MODEL), F32),
            jax.ShapeDtypeStruct((D_MODEL, N_PROJ), BF16),
            jax.ShapeDtypeStruct((D_MODEL, GROUP_WIDTH), BF16),
            jax.ShapeDtypeStruct((GROUP_WIDTH, D_MODEL), BF16),
        ],
        compiler_params=pltpu.CompilerParams(dimension_semantics=("arbitrary",), vmem_limit_bytes=VMEM_LIMIT),
        name="modulation",
    )(c_all, w_ada, b_ada, w_in_t)


def _mod_spec(piece, tm, per_row):
    if per_row:
        return pl.BlockSpec((None, tm, D_MODEL), lambda g, t: (g, t, piece))
    return pl.BlockSpec((None, 1, D_MODEL), lambda g, t: (g, 0, piece))


def _const_spec(shape):
    nd = len(shape)
    return pl.BlockSpec(shape, lambda g, t: (0,) * nd, pipeline_mode=pl.Buffered(1))


PROJ_TILE = 1024


def _proj_body(x_ref, sh_ref, sc_ref, g1_ref, w_ref, lng_ref, lnb_ref, a_ref, b_ref, g_ref):
    n_a = a_ref.shape[1]
    n_copy = n_a - GROUP_WIDTH
    x = x_ref[...]
    h = _rms(x) * g1_ref[...]
    h = (h * (1.0 + sc_ref[...]) + sh_ref[...]).astype(BF16)
    p = _dot(h, w_ref[...])
    a_ref[:, :n_copy] = p[:, :n_copy].astype(a_ref.dtype)
    vs = _gelu_tanh(p[:, n_copy:n_a])
    for hd in range(HEADS):
        sl = slice(hd * HEAD_DIM, (hd + 1) * HEAD_DIM)
        v = vs[:, sl]
        mu = jnp.mean(v, axis=-1, keepdims=True)
        vc = v - mu
        var = jnp.mean(vc * vc, axis=-1, keepdims=True)
        y = vc * lax.rsqrt(var + EPS) * lng_ref[:, sl] + lnb_ref[:, sl]
        a_ref[:, n_copy + hd * HEAD_DIM: n_copy + (hd + 1) * HEAD_DIM] = y.astype(a_ref.dtype)
    b_ref[:, :GROUP_WIDTH] = _sigmoid(p[:, n_a: n_a + GROUP_WIDTH])
    b_ref[:, GROUP_WIDTH:] = _gelu_tanh(p[:, n_a + GROUP_WIDTH: n_a + N_B])
    g_ref[...] = p[:, n_a + N_B:]
    return h


def _proj_kernel(*refs, k_transposed, n_cast):
    ins, outs = refs[:8 + n_cast], refs[8 + n_cast:]
    wk_ref, k_ref = ins[7], outs[3]
    h = _proj_body(*ins[:7], *outs[:3])
    if k_transposed:
        k_ref[...] = _dot_nt(wk_ref[...], h).astype(k_ref.dtype)
    else:
        k_ref[...] = _dot(h, wk_ref[...]).astype(k_ref.dtype)
    for src, dst in zip(ins[8:], outs[4:]):
        dst[...] = src[...].astype(dst.dtype)


def _project(x, mod, g1, w_proj, ln_g, ln_b, w_k, cast_weights=(), *, k_transposed, tm, per_row, a_dtype):
    groups, t, _ = x.shape
    n_proj = w_proj.shape[1]
    n_a = n_proj - N_B - LANES
    steps_per_group = t // tm
    n_steps = groups * steps_per_group
    in_specs = [
        pl.BlockSpec((None, tm, D_MODEL), lambda g, i: (g, i, 0)),
        _mod_spec(0, tm, per_row),
        _mod_spec(1, tm, per_row),
        _const_spec((1, D_MODEL)),
        _const_spec((D_MODEL, n_proj)),
        _const_spec((1, GROUP_WIDTH)),
        _const_spec((1, GROUP_WIDTH)),
    ]
    out_specs = [
        pl.BlockSpec((None, tm, n_a), lambda g, i: (g, i, 0)),
        pl.BlockSpec((None, tm, N_B), lambda g, i: (g, i, 0)),
        pl.BlockSpec((None, tm, LANES), lambda g, i: (g, i, 0)),
    ]
    out_shape = [
        jax.ShapeDtypeStruct((groups, t, n_a), a_dtype),
        jax.ShapeDtypeStruct((groups, t, N_B), F32),
        jax.ShapeDtypeStruct((groups, t, LANES), F32),
    ]
    args = [x, mod, mod, g1, w_proj, ln_g, ln_b, w_k]
    in_specs.append(_const_spec(w_k.shape))
    if k_transposed:
        out_specs.append(pl.BlockSpec((None, GROUP_WIDTH, tm), lambda g, i: (g, 0, i)))
        out_shape.append(jax.ShapeDtypeStruct((groups, GROUP_WIDTH, t), a_dtype))
    else:
        out_specs.append(pl.BlockSpec((None, tm, GROUP_WIDTH), lambda g, i: (g, i, 0)))
        out_shape.append(jax.ShapeDtypeStruct((groups, t, GROUP_WIDTH), a_dtype))
    for w in cast_weights:
        rows, cols = w.shape
        assert rows % (n_steps * 16) == 0, "row block must be a whole number of bf16 sublane tiles"
        spec = pl.BlockSpec((rows // n_steps, cols), lambda g, i: (g * steps_per_group + i, 0))
        in_specs.append(spec)
        out_specs.append(spec)
        out_shape.append(jax.ShapeDtypeStruct((rows, cols), BF16))
        args.append(w)
    return pl.pallas_call(
        functools.partial(_proj_kernel, k_transposed=k_transposed, n_cast=len(cast_weights)),
        grid=(groups, t // tm),
        in_specs=in_specs,
        out_specs=out_specs,
        out_shape=out_shape,
        compiler_params=pltpu.CompilerParams(
            dimension_semantics=("arbitrary", "arbitrary"), vmem_limit_bytes=VMEM_LIMIT),
        name="project",
    )(*args)


MIX_TILE = 512


def _dot3_rhs(lhs_bf16, rhs_f32):
    hi, mid, lo = _split3_bf16(rhs_f32)
    return _dot(lhs_bf16, hi) + _dot(lhs_bf16, mid) + _dot(lhs_bf16, lo)


def _dot3_lhs(lhs_f32, rhs_bf16):
    hi, mid, lo = _split3_bf16(lhs_f32)
    return _dot(hi, rhs_bf16) + _dot(mid, rhs_bf16) + _dot(lo, rhs_bf16)


FF_CHUNKS = (768, 768, 768, 512)
MIXER_LOOKAHEAD = 2
OUT_ROW_BLOCKS = 2
MIN_BLOCK_ROWS = 256


def _gate_rows(g_ref, bias_ref, tril, triu, n_chunks):
    pre = [g_ref[c * CHUNK:(c + 1) * CHUNK, :] + bias_ref[...] for c in range(n_chunks)]
    bc_all = _dot3_rhs(tril, jnp.concatenate([_log_sigmoid(p) for p in pre], axis=1))
    rows_all = jnp.concatenate([p.T[0:2 * HEADS, :] for p in pre], axis=0)
    b_rows_all = _dot3_lhs(_log_sigmoid(rows_all), triu)
    bc, arow, blast = [], [], []
    for c in range(n_chunks):
        bc.append(bc_all[:, c * LANES:(c + 1) * LANES])
        rows = rows_all[c * 2 * HEADS:(c + 1) * 2 * HEADS, :]
        b_rows = b_rows_all[c * 2 * HEADS:(c + 1) * 2 * HEADS, :]
        arow.append([rows[hd:hd + 1, :] - b_rows[HEADS + hd:HEADS + hd + 1, :] for hd in range(HEADS)])
        blast.append([jnp.min(b_rows[HEADS + hd:HEADS + hd + 1, :], axis=1, keepdims=True) for hd in range(HEADS)])
    return bc, arow, blast


def _mlstm_local(q, kt, a_row, b_last, causal):
    amat = jnp.where(causal, a_row, -jnp.inf)
    m_row = jnp.max(amat, axis=1, keepdims=True)
    s_loc = (_dot(q, kt) * jnp.exp(amat - m_row)).astype(BF16)
    g_row = b_last + a_row
    g_loc = jnp.max(g_row, axis=1, keepdims=True)
    kw = (kt.astype(F32) * jnp.exp(g_row - g_loc)).astype(BF16)
    return m_row, s_loc, g_loc, kw


def _mlstm_readout(local, q, v, og, g_head, b_last, b_rep, cta, m_prev, ones_blk):
    m_row, s_loc, g_loc, kw = local
    va = jnp.concatenate([v, ones_blk], axis=1)
    nd_loc = _dot(s_loc, va)
    u_aug = _dot(kw, va)
    inter = _dot(q, cta.astype(BF16))
    mm = jnp.maximum(m_prev, m_row)
    f_loc = jnp.exp(m_row - mm) * QK_SCALE
    f_int = jnp.exp(m_prev - mm) * QK_SCALE
    nd = f_loc * nd_loc + f_int * inter
    clamp = jnp.exp(-(b_rep + mm))
    hh = nd[:, :HEAD_DIM] / jnp.maximum(jnp.abs(nd[:, HEAD_DIM:]), clamp)
    ml = _rms(hh) * g_head * og

    dec = b_last + m_prev
    m_new = jnp.maximum(dec, g_loc)
    cta_new = jnp.exp(dec - m_new) * cta + jnp.exp(g_loc - m_new) * u_aug
    return ml, cta_new, m_new


def _prompt_tail_kernel(a_ref, kt_ref, b_ref, g_ref, x_ref, gt1_ref, sh2_ref, sc2_ref, gt2_ref,
                        xs_ref, mixs_ref, gt1s_ref, sh2s_ref, sc2s_ref, gt2s_ref,
                        bias_ref, gh_ref, ws_ref, bs_ref, wo_ref, g2_ref, wgu_ref, wdn_ref, gf_ref,
                        y_ref, ys_ref, c_out, n_out, m_out,
                        mix_ref, cta_ref, m_ref, *, tiles_per_seq):
    step = pl.program_id(0)
    n_tiles = pl.num_programs(0) - 2
    shared = (bias_ref, gh_ref, ws_ref, bs_ref, wo_ref, g2_ref, wgu_ref, wdn_ref, gf_ref)
    refs = (a_ref, kt_ref, b_ref, g_ref, x_ref, gt1_ref, sh2_ref, sc2_ref, gt2_ref) + shared + (
        y_ref, mix_ref, cta_ref, m_ref)
    sample_refs = (None, None, None, None, xs_ref, gt1s_ref, sh2s_ref, sc2s_ref, gt2s_ref) + shared + (
        ys_ref, mixs_ref, None, None)

    @pl.when(step % tiles_per_seq == 0)
    def _():
        cta_ref[...] = jnp.zeros_like(cta_ref)
        m_ref[...] = jnp.zeros_like(m_ref)

    @pl.when(step == 0)
    def _():
        _prompt_tail_body(refs, with_mixer=True, with_out=False)

    @pl.when(jnp.logical_and(step > 0, step < n_tiles))
    def _():
        _prompt_tail_body(refs, with_mixer=True, with_out=True)

    @pl.when(step == n_tiles)
    def _():
        _prompt_tail_body(refs, with_mixer=False, with_out=True)

    @pl.when(step == n_tiles + 1)
    def _():
        _prompt_tail_body(sample_refs, with_mixer=False, with_out=True)

    @pl.when(jnp.logical_and(step % tiles_per_seq == tiles_per_seq - 1, step < n_tiles))
    def _():
        for hd in range(HEADS):
            cta = cta_ref[hd]
            c_out[hd] = cta[:, :HEAD_DIM].T
            n_out[hd: hd + 1, :] = cta[:, HEAD_DIM:].T[0:1, :]
            m_out[hd: hd + 1, :] = m_ref[hd][0:1, :]


def _prompt_tail_body(refs, *, with_mixer, with_out):
    (a_ref, kt_ref, b_ref, g_ref, x_ref, gt1_ref, sh2_ref, sc2_ref, gt2_ref,
     bias_ref, gh_ref, ws_ref, bs_ref, wo_ref, g2_ref, wgu_ref, wdn_ref, gf_ref,
     y_ref, mix_ref, cta_ref, m_ref) = refs
    tm = x_ref.shape[0]
    n_chunks = tm // CHUNK
    n_copy = a_ref.shape[1] - GROUP_WIDTH if with_mixer else None

    row = lax.broadcasted_iota(jnp.int32, (CHUNK, CHUNK), 0)
    col = lax.broadcasted_iota(jnp.int32, (CHUNK, CHUNK), 1)
    causal = row >= col
    tril = jnp.where(causal, 1.0, 0.0).astype(BF16)
    triu = jnp.where(row <= col, 1.0, 0.0).astype(BF16)
    ones_blk = jnp.ones((CHUNK, HEAD_DIM), BF16)

    def mod_rows(ref, rs):
        return ref[...] if ref.shape[0] == 1 else ref[rs, :]

    rows_per_block = max(tm // OUT_ROW_BLOCKS, min(tm, MIN_BLOCK_ROWS))
    blocks = [slice(r0, r0 + rows_per_block) for r0 in range(0, tm, rows_per_block)] if with_out else []
    outproj = [_dot(mix_ref[rs, :].astype(BF16), wo_ref[...]) for rs in blocks]
    x1, h2 = [], []
    for rs, op in zip(blocks, outproj):
        x1.append(x_ref[rs, :] + mod_rows(gt1_ref, rs) * op)
        h = _rms(x1[-1]) * g2_ref[...]
        h2.append((h * (1.0 + mod_rows(sc2_ref, rs)) + mod_rows(sh2_ref, rs)).astype(BF16))

    if with_mixer:
        bc, arow, blast = _gate_rows(g_ref, bias_ref, tril, triu, n_chunks)
        state = [(cta_ref[hd], jnp.max(m_ref[hd][0:1, :], axis=1, keepdims=True)) for hd in range(HEADS)]
    sg_all = [None] * HEADS

    def mixer_local(c, hd):
        rs = slice(c * CHUNK, (c + 1) * CHUNK)
        sl = slice(hd * HEAD_DIM, (hd + 1) * HEAD_DIM)
        return _mlstm_local(a_ref[rs, sl], kt_ref[sl, rs], arow[c][hd], blast[c][hd], causal)

    def mixer_readout(c, hd, local):
        rs = slice(c * CHUNK, (c + 1) * CHUNK)
        sl = slice(hd * HEAD_DIM, (hd + 1) * HEAD_DIM)
        if c == 0:
            vsn_all = jnp.concatenate(
                [a_ref[cc * CHUNK:(cc + 1) * CHUNK, n_copy + hd * HEAD_DIM: n_copy + (hd + 1) * HEAD_DIM]
                 for cc in range(n_chunks)], axis=1)
            sg_all[hd] = _dot(ws_ref[hd], vsn_all)
        cta, m_prev = state[hd]
        b_rep = jnp.broadcast_to(bc[c][:, HEADS + hd: HEADS + hd + 1], (CHUNK, HEAD_DIM))
        ml, cta, m_prev = _mlstm_readout(
            local, a_ref[rs, sl], a_ref[rs, GROUP_WIDTH + hd * HEAD_DIM: GROUP_WIDTH + (hd + 1) * HEAD_DIM],
            b_ref[rs, sl], gh_ref[:, sl], blast[c][hd], b_rep, cta, m_prev, ones_blk)
        state[hd] = (cta, m_prev)
        mix_ref[rs, sl] = ml.astype(mix_ref.dtype)
        sg = sg_all[hd][:, c * CHUNK:(c + 1) * CHUNK] + bs_ref[hd]
        ug = b_ref[rs, GROUP_WIDTH + hd * HEAD_DIM: GROUP_WIDTH + (hd + 1) * HEAD_DIM]
        mix_ref[rs, GROUP_WIDTH + hd * HEAD_DIM: GROUP_WIDTH + (hd + 1) * HEAD_DIM] = (ug * sg).astype(mix_ref.dtype)

    pieces = [(c, hd) for c in range(n_chunks) for hd in range(HEADS)] if with_mixer else []
    per_ff = -(-len(pieces) // len(FF_CHUNKS))
    ahead = [mixer_local(*pieces[p]) for p in range(min(MIXER_LOOKAHEAD, len(pieces)))]
    acc = [None] * len(blocks)
    f0 = 0
    for j, width in enumerate(FF_CHUNKS):
        gate_up = [(_dot(h, wgu_ref[:, f0:f0 + width]), _dot(h, wgu_ref[:, D_FF + f0:D_FF + f0 + width]))
                   for h in h2]
        for idx in range(j * per_ff, min((j + 1) * per_ff, len(pieces))):
            if idx + MIXER_LOOKAHEAD < len(pieces):
                ahead.append(mixer_local(*pieces[idx + MIXER_LOOKAHEAD]))
            mixer_readout(*pieces[idx], ahead.pop(0))
        for r, (gate, up) in enumerate(gate_up):
            act = (gate * _sigmoid(gate) * up).astype(BF16)
            part = _dot(act, wdn_ref[f0:f0 + width, :])
            acc[r] = part if acc[r] is None else acc[r] + part
            if j + 1 == len(FF_CHUNKS):
                x2 = x1[r] + mod_rows(gt2_ref, blocks[r]) * acc[r]
                y_ref[blocks[r], :] = _rms(x2) * gf_ref[...]
        f0 += width

    if with_mixer:
        for hd in range(HEADS):
            cta_ref[hd] = state[hd][0]
            m_ref[hd] = jnp.broadcast_to(state[hd][1], m_ref.shape[1:])


def _prompt_tail(a, kt, b, gates, x, mod, x_s, mix_s, mod_s, bias_row, g_head, ws_tril, bs_rep,
                 w_out, g2, w_gu, w_dn, g_final):
    groups, t, n_a = a.shape
    ns = x_s.shape[0]
    tm = MIX_TILE
    tps = t // tm
    n_tiles = groups * tps
    assert sum(FF_CHUNKS) == D_FF

    def cur(i):
        return jnp.minimum(i, n_tiles - 1)

    def prev(i):
        return jnp.clip(i - 1, 0, n_tiles - 1)

    def rows(tile, width):
        return pl.BlockSpec((None, tm, width), lambda i: (tile(i) // tps, tile(i) % tps, 0))

    def mod_piece(piece):
        return pl.BlockSpec((None, 1, D_MODEL), lambda i: (prev(i) // tps, 0, piece))

    def const(shape):
        nd = len(shape)
        return pl.BlockSpec(shape, lambda i: (0,) * nd, pipeline_mode=pl.Buffered(1))

    def sample_piece(piece):
        return pl.BlockSpec((ns, D_MODEL), lambda i: (0, piece), pipeline_mode=pl.Buffered(1))

    def per_seq(shape):
        nd = len(shape)
        return pl.BlockSpec((None,) + shape, lambda i: (cur(i) // tps,) + (0,) * nd)

    return pl.pallas_call(
        functools.partial(_prompt_tail_kernel, tiles_per_seq=tps),
        grid=(n_tiles + 2,),
        in_specs=[
            rows(cur, n_a),
            pl.BlockSpec((None, GROUP_WIDTH, tm), lambda i: (cur(i) // tps, 0, cur(i) % tps)),
            rows(cur, N_B),
            rows(cur, LANES),
            rows(prev, D_MODEL),
            mod_piece(2), mod_piece(3), mod_piece(4), mod_piece(5),
            const((ns, D_MODEL)),
            const((ns, D_MODEL)),
            sample_piece(2), sample_piece(3), sample_piece(4), sample_piece(5),
            const((1, LANES)),
            const((1, GROUP_WIDTH)),
            const((HEADS, CHUNK, CHUNK)),
            const((HEADS, CHUNK, LANES)),
            const((D_MODEL, D_MODEL)),
            const((1, D_MODEL)),
            const((D_MODEL, 2 * D_FF)),
            const((D_FF, D_MODEL)),
            const((1, D_MODEL)),
        ],
        out_specs=[
            rows(prev, D_MODEL),
            pl.BlockSpec((ns, D_MODEL), lambda i: (0, 0)),
            per_seq((HEADS, HEAD_DIM, HEAD_DIM)),
            per_seq((HEADS, HEAD_DIM)),
            per_seq((HEADS, LANES)),
        ],
        out_shape=[
            jax.ShapeDtypeStruct((groups, t, D_MODEL), F32),
            jax.ShapeDtypeStruct((ns, D_MODEL), F32),
            jax.ShapeDtypeStruct((groups, HEADS, HEAD_DIM, HEAD_DIM), F32),
            jax.ShapeDtypeStruct((groups, HEADS, HEAD_DIM), F32),
            jax.ShapeDtypeStruct((groups, HEADS, LANES), F32),
        ],
        scratch_shapes=[
            pltpu.VMEM((tm, D_MODEL), BF16),
            pltpu.VMEM((HEADS, HEAD_DIM, 2 * HEAD_DIM), F32),
            pltpu.VMEM((HEADS, 8, LANES), F32),
        ],
        compiler_params=pltpu.CompilerParams(
            dimension_semantics=("arbitrary",), vmem_limit_bytes=VMEM_LIMIT),
        name="prompt_tail",
    )(a, kt, b, gates, x, mod, mod, mod, mod, x_s, mix_s, mod_s, mod_s, mod_s, mod_s,
      bias_row, g_head, ws_tril, bs_rep, w_out, g2, w_gu, w_dn, g_final)


SAMPLE_TOKENS_PER_STEP = 64
SAMPLE_UNROLL = 8


def _mix_sample_kernel(q_ref, k_ref, v_ref, vsn_ref, og_ref, ug_ref, g_ref, m0_ref, n0_ref, c_ref,
                       bias_ref, gh_ref, ws0_ref, bs0_ref,
                       mix_ref, c_out, n_out, m_out,
                       vt_ref, kp_ref, wd_ref, cqt_ref):
    hd = pl.program_id(0)
    grp = pl.program_id(1)
    nb = q_ref.shape[0]
    lane = lax.broadcasted_iota(jnp.int32, (nb, LANES), 1)

    def gate_terms():
        pre = g_ref[...] + bias_ref[...]
        i_pre = jnp.sum(jnp.where(lane == hd, pre, 0.0), axis=1, keepdims=True)
        f_pre = jnp.sum(jnp.where(lane == hd + HEADS, pre, 0.0), axis=1, keepdims=True)
        m_prev = jnp.sum(jnp.where(lane == hd, m0_ref[...], 0.0), axis=1, keepdims=True)
        inter = _log_sigmoid(f_pre) + m_prev
        m_t = jnp.maximum(inter, i_pre)
        return m_t, jnp.exp(i_pre - m_t), jnp.exp(inter - m_t)

    @pl.when(grp == 0)
    def _():
        _, w_in, w_dec = gate_terms()
        vt_ref[...] = v_ref[...].T
        kp_ref[...] = (w_in * k_ref[...]).astype(BF16)
        wd_ref[...] = jnp.broadcast_to(w_dec, wd_ref.shape)
        cqt_ref[...] = jnp.zeros_like(cqt_ref)

    lane_sq = lax.broadcasted_iota(jnp.int32, (HEAD_DIM, LANES), 1)
    tokens = c_ref.shape[0]

    def body(r, cqt):
        tok = grp * tokens + r
        q_row = q_ref[pl.ds(tok, 1), :]
        c_b = c_ref[r]
        sel = lane_sq == tok
        cq_col = jnp.sum(c_b * q_row, axis=1, keepdims=True)
        v_sel = jnp.where(sel, vt_ref[...], 0.0).astype(BF16)
        outer = _dot(v_sel, kp_ref[...])
        c_out[r] = wd_ref[pl.ds(tok, 1), :] * c_b + outer
        return jnp.where(sel, cq_col, cqt)

    cqt_ref[...] = lax.fori_loop(0, tokens, body, cqt_ref[...], unroll=SAMPLE_UNROLL)

    @pl.when(grp == pl.num_programs(1) - 1)
    def _():
        m_t, w_in, w_dec = gate_terms()
        q = q_ref[...]
        k = k_ref[...]
        v = v_ref[...]
        n0 = n0_ref[...]
        cq = cqt_ref[...].T
        s = jnp.sum(q * k, axis=1, keepdims=True) * (QK_SCALE * w_in)
        w_inter = w_dec * QK_SCALE
        num = s * v + w_inter * cq
        den = s + w_inter * jnp.sum(n0 * q, axis=1, keepdims=True)
        hh = num / jnp.maximum(jnp.abs(den), jnp.exp(-m_t))
        ml = _rms(hh) * gh_ref[...] * og_ref[...]
        cm = ug_ref[...] * (ws0_ref[...] * vsn_ref[...] + bs0_ref[...])
        n_out[...] = w_dec * n0 + w_in * k
        m_out[...] = jnp.broadcast_to(m_t, m_out.shape)
        for hh_static in range(HEADS):
            @pl.when(hd == hh_static)
            def _():
                mix_ref[:, hh_static * HEAD_DIM: (hh_static + 1) * HEAD_DIM] = ml
                mix_ref[:, GROUP_WIDTH + hh_static * HEAD_DIM: GROUP_WIDTH + (hh_static + 1) * HEAD_DIM] = cm


def _mix_sample(a, k, b, gates, m0_pad, n0, c0, bias_row, g_head, ws0_row, bs0_row):
    nb = a.shape[0]
    tb = SAMPLE_TOKENS_PER_STEP

    def head_block(offset):
        return pl.BlockSpec((nb, HEAD_DIM), lambda h, g: (0, offset + h))

    full = pl.BlockSpec((nb, LANES), lambda h, g: (0, 0))
    head_row = pl.BlockSpec((1, HEAD_DIM), lambda h, g: (0, h))
    c_spec = pl.BlockSpec((tb, None, HEAD_DIM, HEAD_DIM), lambda h, g: (g, h, 0, 0))
    return pl.pallas_call(
        _mix_sample_kernel,
        grid=(HEADS, nb // tb),
        in_specs=[
            head_block(0), head_block(0), head_block(HEADS), head_block(2 * HEADS),
            head_block(0), head_block(HEADS),
            full, full, head_block(0), c_spec,
            pl.BlockSpec((1, LANES), lambda h, g: (0, 0)),
            head_row, head_row, head_row,
        ],
        out_specs=[
            pl.BlockSpec((nb, D_MODEL), lambda h, g: (0, 0)),
            c_spec,
            head_block(0),
            head_block(0),
        ],
        out_shape=[
            jax.ShapeDtypeStruct((nb, D_MODEL), F32),
            jax.ShapeDtypeStruct(c0.shape, F32),
            jax.ShapeDtypeStruct((nb, GROUP_WIDTH), F32),
            jax.ShapeDtypeStruct((nb, GROUP_WIDTH), F32),
        ],
        scratch_shapes=[
            pltpu.VMEM((HEAD_DIM, nb), F32),
            pltpu.VMEM((nb, HEAD_DIM), BF16),
            pltpu.VMEM((nb, LANES), F32),
            pltpu.VMEM((HEAD_DIM, nb), F32),
        ],
        compiler_params=pltpu.CompilerParams(
            dimension_semantics=("arbitrary", "arbitrary"), vmem_limit_bytes=VMEM_LIMIT),
        name="mix_sample",
    )(a, k, a, a, b, b, gates, m0_pad, n0, c0, bias_row, g_head, ws0_row, bs0_row)


def kernel(x_prompt, x_sample, c_prompt, c_sample, state_mlstm_C, state_mlstm_n, state_mlstm_m, w_ada, b_ada, g_norm1, w_in, b_gate, g_mlstm_head, ln_v_g, ln_v_b, w_s, b_s, w_out, g_norm2, w_gate_up, w_down, g_final):
    depth = w_ada.shape[0]
    assert depth == 1, "single-layer trunk"
    batch, seq, _ = x_prompt.shape
    nb = x_sample.shape[0]
    assert x_sample.shape[1] == 1

    gw = GROUP_WIDTH
    g1 = g_norm1[0].reshape(1, D_MODEL)
    g2 = g_norm2[0].reshape(1, D_MODEL)
    gf = g_final.reshape(1, D_MODEL)
    ln_g = ln_v_g[0].reshape(1, gw)
    ln_b = ln_v_b[0].reshape(1, gw)
    g_head = g_mlstm_head[0].reshape(1, gw)
    bias_row = jnp.pad(b_gate[0], (0, LANES - 2 * HEADS)).reshape(1, LANES)
    tril = jnp.tril(jnp.ones((CHUNK, CHUNK), dtype=bool))
    ws_tril = jnp.where(tril[None], w_s[0], 0.0).astype(BF16)
    bs_rep = jnp.broadcast_to(b_s[0][:, :, None], (HEADS, CHUNK, LANES))
    ws0_row = jnp.repeat(w_s[0][:, 0, 0], HEAD_DIM).reshape(1, gw)
    bs0_row = jnp.repeat(b_s[0][:, 0], HEAD_DIM).reshape(1, gw)

    rows = batch + nb
    rows_pad = -(-rows // 16) * 16
    c_all = jnp.concatenate([c_prompt, c_sample, jnp.zeros((rows_pad - rows, D_MODEL), F32)], axis=0)
    mod, w_proj, w_k, w_kt = _modulation(c_all, w_ada[0], b_ada[0].reshape(1, -1), w_in[0].T)
    mod_p = mod[:batch].reshape(batch, 1, N_MOD * D_MODEL)
    mod_s = mod[batch:rows].reshape(1, nb, N_MOD * D_MODEL)

    a_p, b_p, g_p, kt_p, w_out_b, w_gu_b, w_dn_b = _project(
        x_prompt, mod_p, g1, w_proj, ln_g, ln_b, w_kt, (w_out[0], w_gate_up[0], w_down[0]),
        k_transposed=True, tm=PROJ_TILE, per_row=False, a_dtype=BF16)

    xs = x_sample.reshape(1, nb, D_MODEL)
    a_s, b_s_act, g_s, k_s = _project(xs, mod_s, g1, w_proj, ln_g, ln_b, w_k, k_transposed=False,
                                      tm=nb, per_row=True, a_dtype=F32)
    a_s2, b_s2, g_s2 = a_s[0], b_s_act[0], g_s[0]
    m0_pad = jnp.pad(state_mlstm_m[0], ((0, 0), (0, LANES - HEADS)))
    n0 = state_mlstm_n[0].reshape(nb, gw)
    mix_s, c_s, n_s, m_s = _mix_sample(a_s2, k_s[0], b_s2, g_s2, m0_pad, n0, state_mlstm_C[0],
                                       bias_row, g_head, ws0_row, bs0_row)

    y_p, y_s, c_p, n_p, m_p = _prompt_tail(a_p, kt_p, b_p, g_p, x_prompt, mod_p, xs[0], mix_s, mod_s[0],
                                           bias_row, g_head, ws_tril, bs_rep, w_out_b, g2, w_gu_b, w_dn_b, gf)

    return (
        y_p,
        y_s.reshape(nb, 1, D_MODEL),
        c_p[None],
        n_p[None],
        m_p[:, :, 0][None],
        c_s[None],
        n_s.reshape(nb, HEADS, HEAD_DIM)[None],
        m_s.reshape(nb, HEADS, HEAD_DIM)[:, :, 0][None],
        a_s2[:, 2 * gw:].reshape(nb, 1, HEADS, HEAD_DIM)[None],
    )
```

```python
import functools
import math

import jax
import jax.numpy as jnp
from jax import lax
from jax.experimental import pallas as pl
from jax.experimental.pallas import tpu as pltpu

F32 = jnp.float32
BF16 = jnp.bfloat16

D_MODEL = 1024
HEADS = 4
HEAD_DIM = 128
GROUP_WIDTH = HEADS * HEAD_DIM
CHUNK = 128
D_FF = 2816
N_MOD = 6
EPS = 1e-6
QK_SCALE = HEAD_DIM ** -0.5
LANES = 128

N_B = 2 * GROUP_WIDTH

VMEM_LIMIT = 56 * 1024 * 1024


def _dot(a, b):
    return jnp.dot(a, b, preferred_element_type=F32)


def _dot_nt(a, b):
    return lax.dot_general(a, b, (((1,), (1,)), ((), ())), preferred_element_type=F32)


def _sigmoid(x):
    return 1.0 / (1.0 + jnp.exp(-x))


def _gelu_tanh(x):
    c = math.sqrt(2.0 / math.pi)
    return x * (0.5 * (1.0 + jnp.tanh(c * (x + 0.044715 * (x * x * x)))))


def _log_sigmoid(x):
    return jnp.minimum(x, 0.0) - jnp.log1p(jnp.exp(-jnp.abs(x)))


def _rms(x):
    return x * lax.rsqrt(jnp.mean(x * x, axis=-1, keepdims=True) + EPS)


def _split3_bf16(x):
    hi = x.astype(BF16)
    r1 = x - hi.astype(F32)
    mid = r1.astype(BF16)
    lo = (r1 - mid.astype(F32)).astype(BF16)
    return hi, mid, lo


MOD_TILE = 1024

IN_Q, IN_K, IN_V, IN_O, IN_GATES = 0, GROUP_WIDTH, 2 * GROUP_WIDTH, 3 * GROUP_WIDTH, 4 * GROUP_WIDTH
IN_U = IN_GATES + 2 * HEADS
IN_VS = IN_U + GROUP_WIDTH
N_IN = IN_VS + GROUP_WIDTH
N_PROJ = 5 * GROUP_WIDTH + LANES


def _mod_kernel(c_ref, w_ref, b_ref, wint_ref, o_ref, wproj_ref, wk_ref, wkt_ref):
    c = c_ref[...]
    a = (c * _sigmoid(c)).astype(BF16)
    o_ref[...] = _dot(a, w_ref[...].astype(BF16)) + b_ref[...]

    @pl.when(pl.program_id(0) == 0)
    def _():
        dst = 0
        for src in (IN_Q, IN_V, IN_VS, IN_O, IN_U):
            for blk in range(GROUP_WIDTH // LANES):
                rows = wint_ref[src + blk * LANES: src + (blk + 1) * LANES, :]
                wproj_ref[:, dst:dst + LANES] = rows.T.astype(BF16)
                dst += LANES
        gate_rows = jnp.concatenate(
            [wint_ref[IN_GATES:IN_U, :], jnp.zeros((LANES - 2 * HEADS, D_MODEL), F32)], axis=0)
        wproj_ref[:, dst:dst + LANES] = gate_rows.T.astype(BF16)
        wkt_ref[...] = wint_ref[IN_K:IN_V, :].astype(BF16)
        for blk in range(GROUP_WIDTH // LANES):
            rows = wint_ref[IN_K + blk * LANES: IN_K + (blk + 1) * LANES, :]
            wk_ref[:, blk * LANES:(blk + 1) * LANES] = rows.T.astype(BF16)


def _modulation(c_all, w_ada, b_ada, w_in_t):
    rows = c_all.shape[0]
    tn = MOD_TILE
    assert w_in_t.shape == (N_IN, D_MODEL)

    def whole(shape):
        return pl.BlockSpec(shape, lambda j: (0, 0))

    return pl.pallas_call(
        _mod_kernel,
        grid=(N_MOD * D_MODEL // tn,),
        in_specs=[
            whole((rows, D_MODEL)),
            pl.BlockSpec((D_MODEL, tn), lambda j: (0, j)),
            pl.BlockSpec((1, tn), lambda j: (0, j)),
            pl.BlockSpec((N_IN, D_MODEL), lambda j: (0, 0), pipeline_mode=pl.Buffered(1)),
        ],
        out_specs=[
            pl.BlockSpec((rows, tn), lambda j: (0, j)),
            whole((D_MODEL, N_PROJ)),
            whole((D_MODEL, GROUP_WIDTH)),
            whole((GROUP_WIDTH, D_MODEL)),
        ],
        out_shape=[
            jax.ShapeDtypeStruct((rows, N_MOD * D_MODEL), F32),
            jax.ShapeDtypeStruct((D_MODEL, N_PROJ), BF16),
            jax.ShapeDtypeStruct((D_MODEL, GROUP_WIDTH), BF16),
            jax.ShapeDtypeStruct((GROUP_WIDTH, D_MODEL), BF16),
        ],
        compiler_params=pltpu.CompilerParams(dimension_semantics=("arbitrary",), vmem_limit_bytes=VMEM_LIMIT),
        name="modulation",
    )(c_all, w_ada, b_ada, w_in_t)


def _mod_spec(piece, tm, per_row):
    if per_row:
        return pl.BlockSpec((None, tm, D_MODEL), lambda g, t: (g, t, piece))
    return pl.BlockSpec((None, 1, D_MODEL), lambda g, t: (g, 0, piece))


def _const_spec(shape):
    nd = len(shape)
    return pl.BlockSpec(shape, lambda g, t: (0,) * nd, pipeline_mode=pl.Buffered(1))


PROJ_TILE = 1024


def _proj_body(x_ref, sh_ref, sc_ref, g1_ref, w_ref, lng_ref, lnb_ref, a_ref, b_ref, g_ref):
    n_a = a_ref.shape[1]
    n_copy = n_a - GROUP_WIDTH
    x = x_ref[...]
    h = _rms(x) * g1_ref[...]
    h = (h * (1.0 + sc_ref[...]) + sh_ref[...]).astype(BF16)
    p = _dot(h, w_ref[...])
    a_ref[:, :n_copy] = p[:, :n_copy].astype(a_ref.dtype)
    vs = _gelu_tanh(p[:, n_copy:n_a])
    for hd in range(HEADS):
        sl = slice(hd * HEAD_DIM, (hd + 1) * HEAD_DIM)
        v = vs[:, sl]
        mu = jnp.mean(v, axis=-1, keepdims=True)
        vc = v - mu
        var = jnp.mean(vc * vc, axis=-1, keepdims=True)
        y = vc * lax.rsqrt(var + EPS) * lng_ref[:, sl] + lnb_ref[:, sl]
        a_ref[:, n_copy + hd * HEAD_DIM: n_copy + (hd + 1) * HEAD_DIM] = y.astype(a_ref.dtype)
    b_ref[:, :GROUP_WIDTH] = _sigmoid(p[:, n_a: n_a + GROUP_WIDTH])
    b_ref[:, GROUP_WIDTH:] = _gelu_tanh(p[:, n_a + GROUP_WIDTH: n_a + N_B])
    g_ref[...] = p[:, n_a + N_B:]
    return h


def _proj_kernel(*refs, k_transposed, n_cast):
    ins, outs = refs[:8 + n_cast], refs[8 + n_cast:]
    wk_ref, k_ref = ins[7], outs[3]
    h = _proj_body(*ins[:7], *outs[:3])
    if k_transposed:
        k_ref[...] = _dot_nt(wk_ref[...], h).astype(k_ref.dtype)
    else:
        k_ref[...] = _dot(h, wk_ref[...]).astype(k_ref.dtype)
    for src, dst in zip(ins[8:], outs[4:]):
        dst[...] = src[...].astype(dst.dtype)


def _project(x, mod, g1, w_proj, ln_g, ln_b, w_k, cast_weights=(), *, k_transposed, tm, per_row, a_dtype):
    groups, t, _ = x.shape
    n_proj = w_proj.shape[1]
    n_a = n_proj - N_B - LANES
    steps_per_group = t // tm
    n_steps = groups * steps_per_group
    in_specs = [
        pl.BlockSpec((None, tm, D_MODEL), lambda g, i: (g, i, 0)),
        _mod_spec(0, tm, per_row),
        _mod_spec(1, tm, per_row),
        _const_spec((1, D_MODEL)),
        _const_spec((D_MODEL, n_proj)),
        _const_spec((1, GROUP_WIDTH)),
        _const_spec((1, GROUP_WIDTH)),
    ]
    out_specs = [
        pl.BlockSpec((None, tm, n_a), lambda g, i: (g, i, 0)),
        pl.BlockSpec((None, tm, N_B), lambda g, i: (g, i, 0)),
        pl.BlockSpec((None, tm, LANES), lambda g, i: (g, i, 0)),
    ]
    out_shape = [
        jax.ShapeDtypeStruct((groups, t, n_a), a_dtype),
        jax.ShapeDtypeStruct((groups, t, N_B), F32),
        jax.ShapeDtypeStruct((groups, t, LANES), F32),
    ]
    args = [x, mod, mod, g1, w_proj, ln_g, ln_b, w_k]
    in_specs.append(_const_spec(w_k.shape))
    if k_transposed:
        out_specs.append(pl.BlockSpec((None, GROUP_WIDTH, tm), lambda g, i: (g, 0, i)))
        out_shape.append(jax.ShapeDtypeStruct((groups, GROUP_WIDTH, t), a_dtype))
    else:
        out_specs.append(pl.BlockSpec((None, tm, GROUP_WIDTH), lambda g, i: (g, i, 0)))
        out_shape.append(jax.ShapeDtypeStruct((groups, t, GROUP_WIDTH), a_dtype))
    for w in cast_weights:
        rows, cols = w.shape
        assert rows % (n_steps * 16) == 0, "row block must be a whole number of bf16 sublane tiles"
        spec = pl.BlockSpec((rows // n_steps, cols), lambda g, i: (g * steps_per_group + i, 0))
        in_specs.append(spec)
        out_specs.append(spec)
        out_shape.append(jax.ShapeDtypeStruct((rows, cols), BF16))
        args.append(w)
    return pl.pallas_call(
        functools.partial(_proj_kernel, k_transposed=k_transposed, n_cast=len(cast_weights)),
        grid=(groups, t // tm),
        in_specs=in_specs,
        out_specs=out_specs,
        out_shape=out_shape,
        compiler_params=pltpu.CompilerParams(
            dimension_semantics=("arbitrary", "arbitrary"), vmem_limit_bytes=VMEM_LIMIT),
        name="project",
    )(*args)


MIX_TILE = 512


def _dot3_rhs(lhs_bf16, rhs_f32):
    hi, mid, lo = _split3_bf16(rhs_f32)
    return _dot(lhs_bf16, hi) + _dot(lhs_bf16, mid) + _dot(lhs_bf16, lo)


def _dot3_lhs(lhs_f32, rhs_bf16):
    hi, mid, lo = _split3_bf16(lhs_f32)
    return _dot(hi, rhs_bf16) + _dot(mid, rhs_bf16) + _dot(lo, rhs_bf16)


FF_CHUNKS = (768, 768, 768, 512)
MIXER_LOOKAHEAD = 2
OUT_ROW_BLOCKS = 2


def _gate_rows(g_ref, bias_ref, tril, triu, n_chunks):
    pre = [g_ref[c * CHUNK:(c + 1) * CHUNK, :] + bias_ref[...] for c in range(n_chunks)]
    bc_all = _dot3_rhs(tril, jnp.concatenate([_log_sigmoid(p) for p in pre], axis=1))
    rows_all = jnp.concatenate([p.T[0:2 * HEADS, :] for p in pre], axis=0)
    b_rows_all = _dot3_lhs(_log_sigmoid(rows_all), triu)
    bc, arow, blast = [], [], []
    for c in range(n_chunks):
        bc.append(bc_all[:, c * LANES:(c + 1) * LANES])
        rows = rows_all[c * 2 * HEADS:(c + 1) * 2 * HEADS, :]
        b_rows = b_rows_all[c * 2 * HEADS:(c + 1) * 2 * HEADS, :]
        arow.append([rows[hd:hd + 1, :] - b_rows[HEADS + hd:HEADS + hd + 1, :] for hd in range(HEADS)])
        blast.append([jnp.min(b_rows[HEADS + hd:HEADS + hd + 1, :], axis=1, keepdims=True) for hd in range(HEADS)])
    return bc, arow, blast


def _mlstm_local(q, kt, a_row, b_last, causal):
    amat = jnp.where(causal, a_row, -jnp.inf)
    m_row = jnp.max(amat, axis=1, keepdims=True)
    s_loc = (_dot(q, kt) * jnp.exp(amat - m_row)).astype(BF16)
    g_row = b_last + a_row
    g_loc = jnp.max(g_row, axis=1, keepdims=True)
    kw = (kt.astype(F32) * jnp.exp(g_row - g_loc)).astype(BF16)
    return m_row, s_loc, g_loc, kw


def _mlstm_readout(local, q, v, og, g_head, b_last, b_rep, cta, m_prev, ones_blk):
    m_row, s_loc, g_loc, kw = local
    va = jnp.concatenate([v, ones_blk], axis=1)
    nd_loc = _dot(s_loc, va)
    u_aug = _dot(kw, va)
    inter = _dot(q, cta.astype(BF16))
    mm = jnp.maximum(m_prev, m_row)
    f_loc = jnp.exp(m_row - mm) * QK_SCALE
    f_int = jnp.exp(m_prev - mm) * QK_SCALE
    nd = f_loc * nd_loc + f_int * inter
    clamp = jnp.exp(-(b_rep + mm))
    hh = nd[:, :HEAD_DIM] / jnp.maximum(jnp.abs(nd[:, HEAD_DIM:]), clamp)
    ml = _rms(hh) * g_head * og

    dec = b_last + m_prev
    m_new = jnp.maximum(dec, g_loc)
    cta_new = jnp.exp(dec - m_new) * cta + jnp.exp(g_loc - m_new) * u_aug
    return ml, cta_new, m_new


def _prompt_tail_kernel(a_ref, kt_ref, b_ref, g_ref, x_ref, mod_ref, rows_ref, ws_ref, bs_ref,
                        wo_ref, wgu_ref, wdn_ref,
                        y_ref, c_out, nm_out,
                        mix_ref, cta_ref, m_ref, *, tiles_per_seq):
    step = pl.program_id(0)
    n_tiles = pl.num_programs(0) - 1
    gt1_ref, sh2_ref, sc2_ref, gt2_ref = (mod_ref.at[:, p * D_MODEL:(p + 1) * D_MODEL] for p in range(2, N_MOD))
    g2_ref, gf_ref = rows_ref.at[0:1, :], rows_ref.at[1:2, :]
    gh_ref, bias_ref = rows_ref.at[2:3, :GROUP_WIDTH], rows_ref.at[3:4, :LANES]
    n_out, m_out = nm_out.at[:, :HEAD_DIM], nm_out.at[:, HEAD_DIM:]
    refs = (a_ref, kt_ref, b_ref, g_ref, x_ref, gt1_ref, sh2_ref, sc2_ref, gt2_ref,
            bias_ref, gh_ref, ws_ref, bs_ref, wo_ref, g2_ref, wgu_ref, wdn_ref, gf_ref,
            y_ref, mix_ref, cta_ref, m_ref)

    @pl.when(step % tiles_per_seq == 0)
    def _():
        cta_ref[...] = jnp.zeros_like(cta_ref)
        m_ref[...] = jnp.zeros_like(m_ref)

    @pl.when(step == 0)
    def _():
        _prompt_tail_body(refs, with_mixer=True, with_out=False)

    @pl.when(jnp.logical_and(step > 0, step < n_tiles))
    def _():
        _prompt_tail_body(refs, with_mixer=True, with_out=True)

    @pl.when(step == n_tiles)
    def _():
        _prompt_tail_body(refs, with_mixer=False, with_out=True)

    @pl.when(jnp.logical_and(step % tiles_per_seq == tiles_per_seq - 1, step < n_tiles))
    def _():
        for hd in range(HEADS):
            cta = cta_ref[hd]
            c_out[hd] = cta[:, :HEAD_DIM].T
            n_out[hd: hd + 1, :] = cta[:, HEAD_DIM:].T[0:1, :]
            m_out[hd: hd + 1, :] = m_ref[hd][0:1, :]


def _prompt_tail_body(refs, *, with_mixer, with_out):
    (a_ref, kt_ref, b_ref, g_ref, x_ref, gt1_ref, sh2_ref, sc2_ref, gt2_ref,
     bias_ref, gh_ref, ws_ref, bs_ref, wo_ref, g2_ref, wgu_ref, wdn_ref, gf_ref,
     y_ref, mix_ref, cta_ref, m_ref) = refs
    tm = a_ref.shape[0]
    n_chunks = tm // CHUNK
    n_copy = a_ref.shape[1] - GROUP_WIDTH

    row = lax.broadcasted_iota(jnp.int32, (CHUNK, CHUNK), 0)
    col = lax.broadcasted_iota(jnp.int32, (CHUNK, CHUNK), 1)
    causal = row >= col
    tril = jnp.where(causal, 1.0, 0.0).astype(BF16)
    triu = jnp.where(row <= col, 1.0, 0.0).astype(BF16)
    ones_blk = jnp.ones((CHUNK, HEAD_DIM), BF16)

    blocks = [slice(r0, r0 + tm // OUT_ROW_BLOCKS) for r0 in range(0, tm, tm // OUT_ROW_BLOCKS)] if with_out else []
    outproj = [_dot(mix_ref[rs, :], wo_ref[...]) for rs in blocks]
    x1, h2 = [], []
    for rs, op in zip(blocks, outproj):
        x1.append(x_ref[rs, :] + gt1_ref[...] * op)
        h = _rms(x1[-1]) * g2_ref[...]
        h2.append((h * (1.0 + sc2_ref[...]) + sh2_ref[...]).astype(BF16))

    if with_mixer:
        bc, arow, blast = _gate_rows(g_ref, bias_ref, tril, triu, n_chunks)
        state = [(cta_ref[hd], jnp.max(m_ref[hd][0:1, :], axis=1, keepdims=True)) for hd in range(HEADS)]
    sg_all = [None] * HEADS

    def mixer_local(c, hd):
        rs = slice(c * CHUNK, (c + 1) * CHUNK)
        sl = slice(hd * HEAD_DIM, (hd + 1) * HEAD_DIM)
        return _mlstm_local(a_ref[rs, sl], kt_ref[sl, rs], arow[c][hd], blast[c][hd], causal)

    def mixer_readout(c, hd, local):
        rs = slice(c * CHUNK, (c + 1) * CHUNK)
        sl = slice(hd * HEAD_DIM, (hd + 1) * HEAD_DIM)
        if c == 0:
            vsn_all = jnp.concatenate(
                [a_ref[cc * CHUNK:(cc + 1) * CHUNK, n_copy + hd * HEAD_DIM: n_copy + (hd + 1) * HEAD_DIM]
                 for cc in range(n_chunks)], axis=1)
            sg_all[hd] = _dot(ws_ref[hd], vsn_all)
        cta, m_prev = state[hd]
        b_rep = jnp.broadcast_to(bc[c][:, HEADS + hd: HEADS + hd + 1], (CHUNK, HEAD_DIM))
        ml, cta, m_prev = _mlstm_readout(
            local, a_ref[rs, sl], a_ref[rs, GROUP_WIDTH + hd * HEAD_DIM: GROUP_WIDTH + (hd + 1) * HEAD_DIM],
            b_ref[rs, sl], gh_ref[:, sl], blast[c][hd], b_rep, cta, m_prev, ones_blk)
        state[hd] = (cta, m_prev)
        mix_ref[rs, sl] = ml.astype(mix_ref.dtype)
        sg = sg_all[hd][:, c * CHUNK:(c + 1) * CHUNK] + bs_ref[hd]
        ug = b_ref[rs, GROUP_WIDTH + hd * HEAD_DIM: GROUP_WIDTH + (hd + 1) * HEAD_DIM]
        mix_ref[rs, GROUP_WIDTH + hd * HEAD_DIM: GROUP_WIDTH + (hd + 1) * HEAD_DIM] = (ug * sg).astype(mix_ref.dtype)

    pieces = [(c, hd) for c in range(n_chunks) for hd in range(HEADS)] if with_mixer else []
    per_ff = -(-len(pieces) // len(FF_CHUNKS))
    ahead = [mixer_local(*pieces[p]) for p in range(min(MIXER_LOOKAHEAD, len(pieces)))]
    acc = [None] * len(blocks)
    f0 = 0
    for j, width in enumerate(FF_CHUNKS):
        gate_up = [(_dot(h, wgu_ref[:, f0:f0 + width]), _dot(h, wgu_ref[:, D_FF + f0:D_FF + f0 + width]))
                   for h in h2]
        for idx in range(j * per_ff, min((j + 1) * per_ff, len(pieces))):
            if idx + MIXER_LOOKAHEAD < len(pieces):
                ahead.append(mixer_local(*pieces[idx + MIXER_LOOKAHEAD]))
            mixer_readout(*pieces[idx], ahead.pop(0))
        for r, (gate, up) in enumerate(gate_up):
            act = (gate * _sigmoid(gate) * up).astype(BF16)
            part = _dot(act, wdn_ref[f0:f0 + width, :])
            acc[r] = part if acc[r] is None else acc[r] + part
            if j + 1 == len(FF_CHUNKS):
                x2 = x1[r] + gt2_ref[...] * acc[r]
                y_ref[blocks[r], :] = _rms(x2) * gf_ref[...]
        f0 += width

    if with_mixer:
        for hd in range(HEADS):
            cta_ref[hd] = state[hd][0]
            m_ref[hd] = jnp.broadcast_to(state[hd][1], m_ref.shape[1:])


def _pack_rows(g2, g_final, g_head, bias_row):
    def row(v):
        return jnp.pad(v.reshape(1, -1), ((0, 0), (0, D_MODEL - v.size)))
    return jnp.concatenate([row(g2), row(g_final), row(g_head), row(bias_row), jnp.zeros((4, D_MODEL), F32)], axis=0)


def _prompt_tail(a, kt, b, gates, x, mod, bias_row, g_head, ws_tril, bs_rep, w_out, g2, w_gu, w_dn, g_final):
    groups, t, n_a = a.shape
    tm = MIX_TILE
    tps = t // tm
    n_tiles = groups * tps
    assert sum(FF_CHUNKS) == D_FF

    def cur(i):
        return jnp.minimum(i, n_tiles - 1)

    def prev(i):
        return jnp.maximum(i - 1, 0)

    def rows(tile, width):
        return pl.BlockSpec((None, tm, width), lambda i: (tile(i) // tps, tile(i) % tps, 0))

    def const(shape):
        nd = len(shape)
        return pl.BlockSpec(shape, lambda i: (0,) * nd, pipeline_mode=pl.Buffered(1))

    def per_seq(shape):
        nd = len(shape)
        return pl.BlockSpec((None,) + shape, lambda i: (cur(i) // tps,) + (0,) * nd)

    return pl.pallas_call(
        functools.partial(_prompt_tail_kernel, tiles_per_seq=tps),
        grid=(n_tiles + 1,),
        in_specs=[
            rows(cur, n_a),
            pl.BlockSpec((None, GROUP_WIDTH, tm), lambda i: (cur(i) // tps, 0, cur(i) % tps)),
            rows(cur, N_B),
            rows(cur, LANES),
            rows(prev, D_MODEL),
            pl.BlockSpec((None, 1, N_MOD * D_MODEL), lambda i: (prev(i) // tps, 0, 0)),
            const((8, D_MODEL)),
            const((HEADS, CHUNK, CHUNK)),
            const((HEADS, CHUNK, LANES)),
            const((D_MODEL, D_MODEL)),
            const((D_MODEL, 2 * D_FF)),
            const((D_FF, D_MODEL)),
        ],
        out_specs=[
            rows(prev, D_MODEL),
            per_seq((HEADS, HEAD_DIM, HEAD_DIM)),
            per_seq((HEADS, HEAD_DIM + LANES)),
        ],
        out_shape=[
            jax.ShapeDtypeStruct((groups, t, D_MODEL), F32),
            jax.ShapeDtypeStruct((groups, HEADS, HEAD_DIM, HEAD_DIM), F32),
            jax.ShapeDtypeStruct((groups, HEADS, HEAD_DIM + LANES), F32),
        ],
        scratch_shapes=[
            pltpu.VMEM((tm, D_MODEL), BF16),
            pltpu.VMEM((HEADS, HEAD_DIM, 2 * HEAD_DIM), F32),
            pltpu.VMEM((HEADS, 8, LANES), F32),
        ],
        compiler_params=pltpu.CompilerParams(
            dimension_semantics=("arbitrary",), vmem_limit_bytes=VMEM_LIMIT),
        name="prompt_tail",
    )(a, kt, b, gates, x, mod, _pack_rows(g2, g_final, g_head, bias_row), ws_tril, bs_rep, w_out, w_gu, w_dn)


SAMPLE_TOKENS_PER_STEP = 64
SAMPLE_UNROLL = 8


def _mix_sample_kernel(q_ref, k_ref, v_ref, vsn_ref, og_ref, ug_ref, g_ref, m0_ref, n0_ref, c_ref,
                       bias_ref, gh_ref, ws0_ref, bs0_ref,
                       mix_ref, c_out, n_out, m_out,
                       vt_ref, kp_ref, wd_ref, cqt_ref):
    hd = pl.program_id(0)
    grp = pl.program_id(1)
    nb = q_ref.shape[0]
    lane = lax.broadcasted_iota(jnp.int32, (nb, LANES), 1)

    def gate_terms():
        pre = g_ref[...] + bias_ref[...]
        i_pre = jnp.sum(jnp.where(lane == hd, pre, 0.0), axis=1, keepdims=True)
        f_pre = jnp.sum(jnp.where(lane == hd + HEADS, pre, 0.0), axis=1, keepdims=True)
        m_prev = jnp.sum(jnp.where(lane == hd, m0_ref[...], 0.0), axis=1, keepdims=True)
        inter = _log_sigmoid(f_pre) + m_prev
        m_t = jnp.maximum(inter, i_pre)
        return m_t, jnp.exp(i_pre - m_t), jnp.exp(inter - m_t)

    @pl.when(grp == 0)
    def _():
        _, w_in, w_dec = gate_terms()
        vt_ref[...] = v_ref[...].T
        kp_ref[...] = (w_in * k_ref[...]).astype(BF16)
        wd_ref[...] = jnp.broadcast_to(w_dec, wd_ref.shape)
        cqt_ref[...] = jnp.zeros_like(cqt_ref)

    lane_sq = lax.broadcasted_iota(jnp.int32, (HEAD_DIM, LANES), 1)
    tokens = c_ref.shape[0]

    def body(r, cqt):
        tok = grp * tokens + r
        q_row = q_ref[pl.ds(tok, 1), :]
        c_b = c_ref[r]
        sel = lane_sq == tok
        cq_col = jnp.sum(c_b * q_row, axis=1, keepdims=True)
        v_sel = jnp.where(sel, vt_ref[...], 0.0).astype(BF16)
        outer = _dot(v_sel, kp_ref[...])
        c_out[r] = wd_ref[pl.ds(tok, 1), :] * c_b + outer
        return jnp.where(sel, cq_col, cqt)

    cqt_ref[...] = lax.fori_loop(0, tokens, body, cqt_ref[...], unroll=SAMPLE_UNROLL)

    @pl.when(grp == pl.num_programs(1) - 1)
    def _():
        m_t, w_in, w_dec = gate_terms()
        q = q_ref[...]
        k = k_ref[...]
        v = v_ref[...]
        n0 = n0_ref[...]
        cq = cqt_ref[...].T
        s = jnp.sum(q * k, axis=1, keepdims=True) * (QK_SCALE * w_in)
        w_inter = w_dec * QK_SCALE
        num = s * v + w_inter * cq
        den = s + w_inter * jnp.sum(n0 * q, axis=1, keepdims=True)
        hh = num / jnp.maximum(jnp.abs(den), jnp.exp(-m_t))
        ml = _rms(hh) * gh_ref[...] * og_ref[...]
        cm = ug_ref[...] * (ws0_ref[...] * vsn_ref[...] + bs0_ref[...])
        n_out[...] = w_dec * n0 + w_in * k
        m_out[...] = jnp.broadcast_to(m_t, m_out.shape)
        for hh_static in range(HEADS):
            @pl.when(hd == hh_static)
            def _():
                mix_ref[:, hh_static * HEAD_DIM: (hh_static + 1) * HEAD_DIM] = ml
                mix_ref[:, GROUP_WIDTH + hh_static * HEAD_DIM: GROUP_WIDTH + (hh_static + 1) * HEAD_DIM] = cm


def _mix_sample(a, k, b, gates, m0_pad, n0, c0, bias_row, g_head, ws0_row, bs0_row):
    nb = a.shape[0]
    tb = SAMPLE_TOKENS_PER_STEP

    def head_block(offset):
        return pl.BlockSpec((nb, HEAD_DIM), lambda h, g: (0, offset + h))

    full = pl.BlockSpec((nb, LANES), lambda h, g: (0, 0))
    head_row = pl.BlockSpec((1, HEAD_DIM), lambda h, g: (0, h))
    c_spec = pl.BlockSpec((tb, None, HEAD_DIM, HEAD_DIM), lambda h, g: (g, h, 0, 0))
    return pl.pallas_call(
        _mix_sample_kernel,
        grid=(HEADS, nb // tb),
        in_specs=[
            head_block(0), head_block(0), head_block(HEADS), head_block(2 * HEADS),
            head_block(0), head_block(HEADS),
            full, full, head_block(0), c_spec,
            pl.BlockSpec((1, LANES), lambda h, g: (0, 0)),
            head_row, head_row, head_row,
        ],
        out_specs=[
            pl.BlockSpec((nb, D_MODEL), lambda h, g: (0, 0)),
            c_spec,
            head_block(0),
            head_block(0),
        ],
        out_shape=[
            jax.ShapeDtypeStruct((nb, D_MODEL), F32),
            jax.ShapeDtypeStruct(c0.shape, F32),
            jax.ShapeDtypeStruct((nb, GROUP_WIDTH), F32),
            jax.ShapeDtypeStruct((nb, GROUP_WIDTH), F32),
        ],
        scratch_shapes=[
            pltpu.VMEM((HEAD_DIM, nb), F32),
            pltpu.VMEM((nb, HEAD_DIM), BF16),
            pltpu.VMEM((nb, LANES), F32),
            pltpu.VMEM((HEAD_DIM, nb), F32),
        ],
        compiler_params=pltpu.CompilerParams(
            dimension_semantics=("arbitrary", "arbitrary"), vmem_limit_bytes=VMEM_LIMIT),
        name="mix_sample",
    )(a, k, a, a, b, b, gates, m0_pad, n0, c0, bias_row, g_head, ws0_row, bs0_row)


def _out_kernel(x_ref, mix_ref, gt1_ref, sh2_ref, sc2_ref, gt2_ref, wo_ref, g2_ref, wgu_ref, wdn_ref, gf_ref,
                y_ref):
    x1 = x_ref[...] + gt1_ref[...] * _dot(mix_ref[...].astype(BF16), wo_ref[...])
    h2 = _rms(x1) * g2_ref[...]
    h2 = (h2 * (1.0 + sc2_ref[...]) + sh2_ref[...]).astype(BF16)
    gate = _dot(h2, wgu_ref[:, :D_FF])
    up = _dot(h2, wgu_ref[:, D_FF:])
    act = (gate * _sigmoid(gate) * up).astype(BF16)
    x2 = x1 + gt2_ref[...] * _dot(act, wdn_ref[...])
    y_ref[...] = _rms(x2) * gf_ref[...]


def _output_stage(x, mix, mod, w_out, g2, w_gu, w_dn, g_final, *, tm, per_row):
    groups, t, _ = x.shape
    return pl.pallas_call(
        _out_kernel,
        grid=(groups, t // tm),
        in_specs=[
            pl.BlockSpec((None, tm, D_MODEL), lambda g, i: (g, i, 0)),
            pl.BlockSpec((None, tm, D_MODEL), lambda g, i: (g, i, 0)),
            _mod_spec(2, tm, per_row),
            _mod_spec(3, tm, per_row),
            _mod_spec(4, tm, per_row),
            _mod_spec(5, tm, per_row),
            _const_spec((D_MODEL, D_MODEL)),
            _const_spec((1, D_MODEL)),
            _const_spec((D_MODEL, 2 * D_FF)),
            _const_spec((D_FF, D_MODEL)),
            _const_spec((1, D_MODEL)),
        ],
        out_specs=pl.BlockSpec((None, tm, D_MODEL), lambda g, i: (g, i, 0)),
        out_shape=jax.ShapeDtypeStruct((groups, t, D_MODEL), F32),
        compiler_params=pltpu.CompilerParams(
            dimension_semantics=("arbitrary", "arbitrary"), vmem_limit_bytes=VMEM_LIMIT),
        name="output_stage",
    )(x, mix, mod, mod, mod, mod, w_out, g2, w_gu, w_dn, g_final)


def kernel(x_prompt, x_sample, c_prompt, c_sample, state_mlstm_C, state_mlstm_n, state_mlstm_m, w_ada, b_ada, g_norm1, w_in, b_gate, g_mlstm_head, ln_v_g, ln_v_b, w_s, b_s, w_out, g_norm2, w_gate_up, w_down, g_final):
    depth = w_ada.shape[0]
    assert depth == 1, "single-layer trunk"
    batch, seq, _ = x_prompt.shape
    nb = x_sample.shape[0]
    assert x_sample.shape[1] == 1

    gw = GROUP_WIDTH
    g1 = g_norm1[0].reshape(1, D_MODEL)
    g2 = g_norm2[0].reshape(1, D_MODEL)
    gf = g_final.reshape(1, D_MODEL)
    ln_g = ln_v_g[0].reshape(1, gw)
    ln_b = ln_v_b[0].reshape(1, gw)
    g_head = g_mlstm_head[0].reshape(1, gw)
    bias_row = jnp.pad(b_gate[0], (0, LANES - 2 * HEADS)).reshape(1, LANES)
    tril = jnp.tril(jnp.ones((CHUNK, CHUNK), dtype=bool))
    ws_tril = jnp.where(tril[None], w_s[0], 0.0).astype(BF16)
    bs_rep = jnp.broadcast_to(b_s[0][:, :, None], (HEADS, CHUNK, LANES))
    ws0_row = jnp.repeat(w_s[0][:, 0, 0], HEAD_DIM).reshape(1, gw)
    bs0_row = jnp.repeat(b_s[0][:, 0], HEAD_DIM).reshape(1, gw)

    rows = batch + nb
    rows_pad = -(-rows // 16) * 16
    c_all = jnp.concatenate([c_prompt, c_sample, jnp.zeros((rows_pad - rows, D_MODEL), F32)], axis=0)
    mod, w_proj, w_k, w_kt = _modulation(c_all, w_ada[0], b_ada[0].reshape(1, -1), w_in[0].T)
    mod_p = mod[:batch].reshape(batch, 1, N_MOD * D_MODEL)
    mod_s = mod[batch:rows].reshape(1, nb, N_MOD * D_MODEL)

    a_p, b_p, g_p, kt_p, w_out_b, w_gu_b, w_dn_b = _project(
        x_prompt, mod_p, g1, w_proj, ln_g, ln_b, w_kt, (w_out[0], w_gate_up[0], w_down[0]),
        k_transposed=True, tm=PROJ_TILE, per_row=False, a_dtype=BF16)
    y_p, c_p, nm_p = _prompt_tail(a_p, kt_p, b_p, g_p, x_prompt, mod_p, bias_row, g_head, ws_tril, bs_rep,
                                  w_out_b, g2, w_gu_b, w_dn_b, gf)

    xs = x_sample.reshape(1, nb, D_MODEL)
    a_s, b_s_act, g_s, k_s = _project(xs, mod_s, g1, w_proj, ln_g, ln_b, w_k, k_transposed=False,
                                      tm=nb, per_row=True, a_dtype=F32)
    a_s2, b_s2, g_s2 = a_s[0], b_s_act[0], g_s[0]
    m0_pad = jnp.pad(state_mlstm_m[0], ((0, 0), (0, LANES - HEADS)))
    n0 = state_mlstm_n[0].reshape(nb, gw)
    mix_s, c_s, n_s, m_s = _mix_sample(a_s2, k_s[0], b_s2, g_s2, m0_pad, n0, state_mlstm_C[0],
                                       bias_row, g_head, ws0_row, bs0_row)
    y_s = _output_stage(xs, mix_s.reshape(1, nb, D_MODEL), mod_s, w_out_b, g2, w_gu_b, w_dn_b, gf,
                        tm=nb, per_row=True)

    return (
        y_p,
        y_s.reshape(nb, 1, D_MODEL),
        c_p[None],
        nm_p[:, :, :HEAD_DIM][None],
        nm_p[:, :, HEAD_DIM][None],
        c_s[None],
        n_s.reshape(nb, HEADS, HEAD_DIM)[None],
        m_s.reshape(nb, HEADS, HEAD_DIM)[:, :, 0][None],
        a_s2[:, 2 * gw:].reshape(nb, 1, HEADS, HEAD_DIM)[None],
    )
```

```python
import functools
import math

import jax
import jax.numpy as jnp
from jax import lax
from jax.experimental import pallas as pl
from jax.experimental.pallas import tpu as pltpu

F32 = jnp.float32
BF16 = jnp.bfloat16

D_MODEL = 1024
HEADS = 4
HEAD_DIM = 128
GROUP_WIDTH = HEADS * HEAD_DIM
CHUNK = 128
D_FF = 2816
N_MOD = 6
EPS = 1e-6
QK_SCALE = HEAD_DIM ** -0.5
LANES = 128
BF16_SUBLANES = 16

N_B = 2 * GROUP_WIDTH

VMEM_LIMIT = 56 * 1024 * 1024


def _dot(a, b):
    return jnp.dot(a, b, preferred_element_type=F32)


def _dot_nt(a, b):
    return lax.dot_general(a, b, (((1,), (1,)), ((), ())), preferred_element_type=F32)


def _sigmoid(x):
    return 1.0 / (1.0 + jnp.exp(-x))


def _gelu_tanh(x):
    c = math.sqrt(2.0 / math.pi)
    return x * (0.5 * (1.0 + jnp.tanh(c * (x + 0.044715 * (x * x * x)))))


def _log_sigmoid(x):
    return jnp.minimum(x, 0.0) - jnp.log1p(jnp.exp(-jnp.abs(x)))


def _rms(x):
    return x * lax.rsqrt(jnp.mean(x * x, axis=-1, keepdims=True) + EPS)


def _split3_bf16(x):
    hi = x.astype(BF16)
    r1 = x - hi.astype(F32)
    mid = r1.astype(BF16)
    lo = (r1 - mid.astype(F32)).astype(BF16)
    return hi, mid, lo


MOD_TILE = 1024

IN_Q, IN_K, IN_V, IN_O, IN_GATES = 0, GROUP_WIDTH, 2 * GROUP_WIDTH, 3 * GROUP_WIDTH, 4 * GROUP_WIDTH
IN_U = IN_GATES + 2 * HEADS
IN_VS = IN_U + GROUP_WIDTH
N_IN = IN_VS + GROUP_WIDTH
N_PROJ = 5 * GROUP_WIDTH + LANES


def _mod_kernel(cs_ref, cp_ref, w_ref, b_ref, wint_ref, os_ref, op_ref, wproj_ref, wk_ref, wkt_ref):
    ns, npr = cs_ref.shape[0], cp_ref.shape[0]
    pad = -(ns + npr) % BF16_SUBLANES
    c = jnp.concatenate([cs_ref[...], cp_ref[...], jnp.zeros((pad, D_MODEL), F32)], axis=0)
    a = (c * _sigmoid(c)).astype(BF16)
    res = _dot(a, w_ref[...].astype(BF16)) + b_ref[...]
    os_ref[...] = res[:ns]
    op_ref[...] = res[ns:ns + npr]

    @pl.when(pl.program_id(0) == 0)
    def _():
        dst = 0
        for src in (IN_Q, IN_V, IN_VS, IN_O, IN_U):
            for blk in range(GROUP_WIDTH // LANES):
                rows = wint_ref[src + blk * LANES: src + (blk + 1) * LANES, :]
                wproj_ref[:, dst:dst + LANES] = rows.T.astype(BF16)
                dst += LANES
        gate_rows = jnp.concatenate(
            [wint_ref[IN_GATES:IN_U, :], jnp.zeros((LANES - 2 * HEADS, D_MODEL), F32)], axis=0)
        wproj_ref[:, dst:dst + LANES] = gate_rows.T.astype(BF16)
        wkt_ref[...] = wint_ref[IN_K:IN_V, :].astype(BF16)
        for blk in range(GROUP_WIDTH // LANES):
            rows = wint_ref[IN_K + blk * LANES: IN_K + (blk + 1) * LANES, :]
            wk_ref[:, blk * LANES:(blk + 1) * LANES] = rows.T.astype(BF16)


def _modulation(c_sample, c_prompt, w_ada, b_ada, w_in_t):
    ns, npr = c_sample.shape[0], c_prompt.shape[0]
    tn = MOD_TILE
    assert w_in_t.shape == (N_IN, D_MODEL)

    def whole(shape):
        return pl.BlockSpec(shape, lambda j: (0, 0))

    return pl.pallas_call(
        _mod_kernel,
        grid=(N_MOD * D_MODEL // tn,),
        in_specs=[
            whole((ns, D_MODEL)),
            whole((npr, D_MODEL)),
            pl.BlockSpec((D_MODEL, tn), lambda j: (0, j)),
            pl.BlockSpec((1, tn), lambda j: (0, j)),
            pl.BlockSpec((N_IN, D_MODEL), lambda j: (0, 0), pipeline_mode=pl.Buffered(1)),
        ],
        out_specs=[
            pl.BlockSpec((ns, tn), lambda j: (0, j)),
            pl.BlockSpec((npr, tn), lambda j: (0, j)),
            whole((D_MODEL, N_PROJ)),
            whole((D_MODEL, GROUP_WIDTH)),
            whole((GROUP_WIDTH, D_MODEL)),
        ],
        out_shape=[
            jax.ShapeDtypeStruct((ns, N_MOD * D_MODEL), F32),
            jax.ShapeDtypeStruct((npr, N_MOD * D_MODEL), F32),
            jax.ShapeDtypeStruct((D_MODEL, N_PROJ), BF16),
            jax.ShapeDtypeStruct((D_MODEL, GROUP_WIDTH), BF16),
            jax.ShapeDtypeStruct((GROUP_WIDTH, D_MODEL), BF16),
        ],
        compiler_params=pltpu.CompilerParams(dimension_semantics=("arbitrary",), vmem_limit_bytes=VMEM_LIMIT),
        name="modulation",
    )(c_sample, c_prompt, w_ada, b_ada, w_in_t)


def _mod_spec(piece, tm, per_row):
    if per_row:
        return pl.BlockSpec((None, tm, D_MODEL), lambda g, t: (g, t, piece))
    return pl.BlockSpec((None, 1, D_MODEL), lambda g, t: (g, 0, piece))


def _const_spec(shape):
    nd = len(shape)
    return pl.BlockSpec(shape, lambda g, t: (0,) * nd, pipeline_mode=pl.Buffered(1))


PROJ_TILE = 1024


def _proj_body(x_ref, sh_ref, sc_ref, g1_ref, w_ref, lng_ref, lnb_ref, a_ref, b_ref, g_ref):
    n_a = a_ref.shape[1]
    n_copy = n_a - GROUP_WIDTH
    x = x_ref[...]
    h = _rms(x) * g1_ref[...]
    h = (h * (1.0 + sc_ref[...]) + sh_ref[...]).astype(BF16)
    p = _dot(h, w_ref[...])
    a_ref[:, :n_copy] = p[:, :n_copy].astype(a_ref.dtype)
    vs = _gelu_tanh(p[:, n_copy:n_a])
    for hd in range(HEADS):
        sl = slice(hd * HEAD_DIM, (hd + 1) * HEAD_DIM)
        v = vs[:, sl]
        mu = jnp.mean(v, axis=-1, keepdims=True)
        vc = v - mu
        var = jnp.mean(vc * vc, axis=-1, keepdims=True)
        y = vc * lax.rsqrt(var + EPS) * lng_ref[:, sl] + lnb_ref[:, sl]
        a_ref[:, n_copy + hd * HEAD_DIM: n_copy + (hd + 1) * HEAD_DIM] = y.astype(a_ref.dtype)
    b_ref[:, :GROUP_WIDTH] = _sigmoid(p[:, n_a: n_a + GROUP_WIDTH])
    b_ref[:, GROUP_WIDTH:] = _gelu_tanh(p[:, n_a + GROUP_WIDTH: n_a + N_B])
    g_ref[...] = p[:, n_a + N_B:]
    return h


def _proj_kernel(*refs, k_transposed, n_cast):
    ins, outs = refs[:8 + n_cast], refs[8 + n_cast:]
    wk_ref, k_ref = ins[7], outs[3]
    h = _proj_body(*ins[:7], *outs[:3])
    if k_transposed:
        k_ref[...] = _dot_nt(wk_ref[...], h).astype(k_ref.dtype)
    else:
        k_ref[...] = _dot(h, wk_ref[...]).astype(k_ref.dtype)
    for src, dst in zip(ins[8:], outs[4:]):
        dst[...] = src[...].astype(dst.dtype)


def _project(x, mod, g1, w_proj, ln_g, ln_b, w_k, cast_weights=(), *, k_transposed, tm, per_row, a_dtype):
    groups, t, _ = x.shape
    n_proj = w_proj.shape[1]
    n_a = n_proj - N_B - LANES
    steps_per_group = t // tm
    n_steps = groups * steps_per_group
    in_specs = [
        pl.BlockSpec((None, tm, D_MODEL), lambda g, i: (g, i, 0)),
        _mod_spec(0, tm, per_row),
        _mod_spec(1, tm, per_row),
        _const_spec((1, D_MODEL)),
        _const_spec((D_MODEL, n_proj)),
        _const_spec((1, GROUP_WIDTH)),
        _const_spec((1, GROUP_WIDTH)),
    ]
    out_specs = [
        pl.BlockSpec((None, tm, n_a), lambda g, i: (g, i, 0)),
        pl.BlockSpec((None, tm, N_B), lambda g, i: (g, i, 0)),
        pl.BlockSpec((None, tm, LANES), lambda g, i: (g, i, 0)),
    ]
    out_shape = [
        jax.ShapeDtypeStruct((groups, t, n_a), a_dtype),
        jax.ShapeDtypeStruct((groups, t, N_B), F32),
        jax.ShapeDtypeStruct((groups, t, LANES), F32),
    ]
    args = [x, mod, mod, g1, w_proj, ln_g, ln_b, w_k]
    in_specs.append(_const_spec(w_k.shape))
    if k_transposed:
        out_specs.append(pl.BlockSpec((None, GROUP_WIDTH, tm), lambda g, i: (g, 0, i)))
        out_shape.append(jax.ShapeDtypeStruct((groups, GROUP_WIDTH, t), a_dtype))
    else:
        out_specs.append(pl.BlockSpec((None, tm, GROUP_WIDTH), lambda g, i: (g, i, 0)))
        out_shape.append(jax.ShapeDtypeStruct((groups, t, GROUP_WIDTH), a_dtype))
    for w in cast_weights:
        rows, cols = w.shape
        assert rows % (n_steps * 16) == 0, "row block must be a whole number of bf16 sublane tiles"
        spec = pl.BlockSpec((rows // n_steps, cols), lambda g, i: (g * steps_per_group + i, 0))
        in_specs.append(spec)
        out_specs.append(spec)
        out_shape.append(jax.ShapeDtypeStruct((rows, cols), BF16))
        args.append(w)
    return pl.pallas_call(
        functools.partial(_proj_kernel, k_transposed=k_transposed, n_cast=len(cast_weights)),
        grid=(groups, t // tm),
        in_specs=in_specs,
        out_specs=out_specs,
        out_shape=out_shape,
        compiler_params=pltpu.CompilerParams(
            dimension_semantics=("arbitrary", "arbitrary"), vmem_limit_bytes=VMEM_LIMIT),
        name="project",
    )(*args)


MIX_TILE = 512


def _dot3_rhs(lhs_bf16, rhs_f32):
    hi, mid, lo = _split3_bf16(rhs_f32)
    return _dot(lhs_bf16, hi) + _dot(lhs_bf16, mid) + _dot(lhs_bf16, lo)


def _dot3_lhs(lhs_f32, rhs_bf16):
    hi, mid, lo = _split3_bf16(lhs_f32)
    return _dot(hi, rhs_bf16) + _dot(mid, rhs_bf16) + _dot(lo, rhs_bf16)


FF_CHUNKS = (768, 768, 768, 512)
MIXER_LOOKAHEAD = 2
OUT_ROW_BLOCKS = 2


def _gate_rows(g_ref, bias_ref, tril, triu, n_chunks):
    pre = [g_ref[c * CHUNK:(c + 1) * CHUNK, :] + bias_ref[...] for c in range(n_chunks)]
    bc_all = _dot3_rhs(tril, jnp.concatenate([_log_sigmoid(p) for p in pre], axis=1))
    rows_all = jnp.concatenate([p.T[0:2 * HEADS, :] for p in pre], axis=0)
    b_rows_all = _dot3_lhs(_log_sigmoid(rows_all), triu)
    bc, arow, blast = [], [], []
    for c in range(n_chunks):
        bc.append(bc_all[:, c * LANES:(c + 1) * LANES])
        rows = rows_all[c * 2 * HEADS:(c + 1) * 2 * HEADS, :]
        b_rows = b_rows_all[c * 2 * HEADS:(c + 1) * 2 * HEADS, :]
        arow.append([rows[hd:hd + 1, :] - b_rows[HEADS + hd:HEADS + hd + 1, :] for hd in range(HEADS)])
        blast.append([jnp.min(b_rows[HEADS + hd:HEADS + hd + 1, :], axis=1, keepdims=True) for hd in range(HEADS)])
    return bc, arow, blast


def _mlstm_local(q, kt, a_row, b_last, causal):
    amat = jnp.where(causal, a_row, -jnp.inf)
    m_row = jnp.max(amat, axis=1, keepdims=True)
    s_loc = (_dot(q, kt) * jnp.exp(amat - m_row)).astype(BF16)
    g_row = b_last + a_row
    g_loc = jnp.max(g_row, axis=1, keepdims=True)
    kw = (kt.astype(F32) * jnp.exp(g_row - g_loc)).astype(BF16)
    return m_row, s_loc, g_loc, kw


def _mlstm_readout(local, q, v, og, g_head, b_last, b_rep, cta, m_prev, ones_blk):
    m_row, s_loc, g_loc, kw = local
    va = jnp.concatenate([v, ones_blk], axis=1)
    nd_loc = _dot(s_loc, va)
    u_aug = _dot(kw, va)
    inter = _dot(q, cta.astype(BF16))
    mm = jnp.maximum(m_prev, m_row)
    f_loc = jnp.exp(m_row - mm) * QK_SCALE
    f_int = jnp.exp(m_prev - mm) * QK_SCALE
    nd = f_loc * nd_loc + f_int * inter
    clamp = jnp.exp(-(b_rep + mm))
    hh = nd[:, :HEAD_DIM] / jnp.maximum(jnp.abs(nd[:, HEAD_DIM:]), clamp)
    ml = _rms(hh) * g_head * og

    dec = b_last + m_prev
    m_new = jnp.maximum(dec, g_loc)
    cta_new = jnp.exp(dec - m_new) * cta + jnp.exp(g_loc - m_new) * u_aug
    return ml, cta_new, m_new


def _prompt_tail_kernel(a_ref, kt_ref, b_ref, g_ref, x_ref, gt1_ref, sh2_ref, sc2_ref, gt2_ref,
                        bias_ref, gh_ref, ws_ref, bs_ref, wo_ref, g2_ref, wgu_ref, wdn_ref, gf_ref,
                        y_ref, c_out, n_out, m_out,
                        mix_ref, cta_ref, m_ref, *, tiles_per_seq):
    step = pl.program_id(0)
    n_tiles = pl.num_programs(0) - 1
    refs = (a_ref, kt_ref, b_ref, g_ref, x_ref, gt1_ref, sh2_ref, sc2_ref, gt2_ref,
            bias_ref, gh_ref, ws_ref, bs_ref, wo_ref, g2_ref, wgu_ref, wdn_ref, gf_ref,
            y_ref, mix_ref, cta_ref, m_ref)

    @pl.when(step % tiles_per_seq == 0)
    def _():
        cta_ref[...] = jnp.zeros_like(cta_ref)
        m_ref[...] = jnp.zeros_like(m_ref)

    @pl.when(step == 0)
    def _():
        _prompt_tail_body(refs, with_mixer=True, with_out=False)

    @pl.when(jnp.logical_and(step > 0, step < n_tiles))
    def _():
        _prompt_tail_body(refs, with_mixer=True, with_out=True)

    @pl.when(step == n_tiles)
    def _():
        _prompt_tail_body(refs, with_mixer=False, with_out=True)

    @pl.when(jnp.logical_and(step % tiles_per_seq == tiles_per_seq - 1, step < n_tiles))
    def _():
        for hd in range(HEADS):
            cta = cta_ref[hd]
            c_out[hd] = cta[:, :HEAD_DIM].T
            n_out[hd: hd + 1, :] = cta[:, HEAD_DIM:].T[0:1, :]
            m_out[hd: hd + 1, :] = m_ref[hd][0:1, :]


def _prompt_tail_body(refs, *, with_mixer, with_out):
    (a_ref, kt_ref, b_ref, g_ref, x_ref, gt1_ref, sh2_ref, sc2_ref, gt2_ref,
     bias_ref, gh_ref, ws_ref, bs_ref, wo_ref, g2_ref, wgu_ref, wdn_ref, gf_ref,
     y_ref, mix_ref, cta_ref, m_ref) = refs
    tm = a_ref.shape[0]
    n_chunks = tm // CHUNK
    n_copy = a_ref.shape[1] - GROUP_WIDTH

    row = lax.broadcasted_iota(jnp.int32, (CHUNK, CHUNK), 0)
    col = lax.broadcasted_iota(jnp.int32, (CHUNK, CHUNK), 1)
    causal = row >= col
    tril = jnp.where(causal, 1.0, 0.0).astype(BF16)
    triu = jnp.where(row <= col, 1.0, 0.0).astype(BF16)
    ones_blk = jnp.ones((CHUNK, HEAD_DIM), BF16)

    blocks = [slice(r0, r0 + tm // OUT_ROW_BLOCKS) for r0 in range(0, tm, tm // OUT_ROW_BLOCKS)] if with_out else []
    outproj = [_dot(mix_ref[rs, :], wo_ref[...]) for rs in blocks]
    x1, h2 = [], []
    for rs, op in zip(blocks, outproj):
        x1.append(x_ref[rs, :] + gt1_ref[...] * op)
        h = _rms(x1[-1]) * g2_ref[...]
        h2.append((h * (1.0 + sc2_ref[...]) + sh2_ref[...]).astype(BF16))

    if with_mixer:
        bc, arow, blast = _gate_rows(g_ref, bias_ref, tril, triu, n_chunks)
        state = [(cta_ref[hd], jnp.max(m_ref[hd][0:1, :], axis=1, keepdims=True)) for hd in range(HEADS)]
    sg_all = [None] * HEADS

    def mixer_local(c, hd):
        rs = slice(c * CHUNK, (c + 1) * CHUNK)
        sl = slice(hd * HEAD_DIM, (hd + 1) * HEAD_DIM)
        return _mlstm_local(a_ref[rs, sl], kt_ref[sl, rs], arow[c][hd], blast[c][hd], causal)

    def mixer_readout(c, hd, local):
        rs = slice(c * CHUNK, (c + 1) * CHUNK)
        sl = slice(hd * HEAD_DIM, (hd + 1) * HEAD_DIM)
        if c == 0:
            vsn_all = jnp.concatenate(
                [a_ref[cc * CHUNK:(cc + 1) * CHUNK, n_copy + hd * HEAD_DIM: n_copy + (hd + 1) * HEAD_DIM]
                 for cc in range(n_chunks)], axis=1)
            sg_all[hd] = _dot(ws_ref[hd], vsn_all)
        cta, m_prev = state[hd]
        b_rep = jnp.broadcast_to(bc[c][:, HEADS + hd: HEADS + hd + 1], (CHUNK, HEAD_DIM))
        ml, cta, m_prev = _mlstm_readout(
            local, a_ref[rs, sl], a_ref[rs, GROUP_WIDTH + hd * HEAD_DIM: GROUP_WIDTH + (hd + 1) * HEAD_DIM],
            b_ref[rs, sl], gh_ref[:, sl], blast[c][hd], b_rep, cta, m_prev, ones_blk)
        state[hd] = (cta, m_prev)
        mix_ref[rs, sl] = ml.astype(mix_ref.dtype)
        sg = sg_all[hd][:, c * CHUNK:(c + 1) * CHUNK] + bs_ref[hd]
        ug = b_ref[rs, GROUP_WIDTH + hd * HEAD_DIM: GROUP_WIDTH + (hd + 1) * HEAD_DIM]
        mix_ref[rs, GROUP_WIDTH + hd * HEAD_DIM: GROUP_WIDTH + (hd + 1) * HEAD_DIM] = (ug * sg).astype(mix_ref.dtype)

    pieces = [(c, hd) for c in range(n_chunks) for hd in range(HEADS)] if with_mixer else []
    per_ff = -(-len(pieces) // len(FF_CHUNKS))
    ahead = [mixer_local(*pieces[p]) for p in range(min(MIXER_LOOKAHEAD, len(pieces)))]
    acc = [None] * len(blocks)
    f0 = 0
    for j, width in enumerate(FF_CHUNKS):
        gate_up = [(_dot(h, wgu_ref[:, f0:f0 + width]), _dot(h, wgu_ref[:, D_FF + f0:D_FF + f0 + width]))
                   for h in h2]
        for idx in range(j * per_ff, min((j + 1) * per_ff, len(pieces))):
            if idx + MIXER_LOOKAHEAD < len(pieces):
                ahead.append(mixer_local(*pieces[idx + MIXER_LOOKAHEAD]))
            mixer_readout(*pieces[idx], ahead.pop(0))
        for r, (gate, up) in enumerate(gate_up):
            act = (gate * _sigmoid(gate) * up).astype(BF16)
            part = _dot(act, wdn_ref[f0:f0 + width, :])
            acc[r] = part if acc[r] is None else acc[r] + part
            if j + 1 == len(FF_CHUNKS):
                x2 = x1[r] + gt2_ref[...] * acc[r]
                y_ref[blocks[r], :] = _rms(x2) * gf_ref[...]
        f0 += width

    if with_mixer:
        for hd in range(HEADS):
            cta_ref[hd] = state[hd][0]
            m_ref[hd] = jnp.broadcast_to(state[hd][1], m_ref.shape[1:])


def _prompt_tail(a, kt, b, gates, x, mod, bias_row, g_head, ws_tril, bs_rep, w_out, g2, w_gu, w_dn, g_final):
    groups, t, n_a = a.shape
    tm = MIX_TILE
    tps = t // tm
    n_tiles = groups * tps
    assert sum(FF_CHUNKS) == D_FF

    def cur(i):
        return jnp.minimum(i, n_tiles - 1)

    def prev(i):
        return jnp.maximum(i - 1, 0)

    def rows(tile, width):
        return pl.BlockSpec((None, tm, width), lambda i: (tile(i) // tps, tile(i) % tps, 0))

    def mod_piece(piece):
        return pl.BlockSpec((None, 1, D_MODEL), lambda i: (prev(i) // tps, 0, piece))

    def const(shape):
        nd = len(shape)
        return pl.BlockSpec(shape, lambda i: (0,) * nd, pipeline_mode=pl.Buffered(1))

    def per_seq(shape):
        nd = len(shape)
        return pl.BlockSpec((None,) + shape, lambda i: (cur(i) // tps,) + (0,) * nd)

    return pl.pallas_call(
        functools.partial(_prompt_tail_kernel, tiles_per_seq=tps),
        grid=(n_tiles + 1,),
        in_specs=[
            rows(cur, n_a),
            pl.BlockSpec((None, GROUP_WIDTH, tm), lambda i: (cur(i) // tps, 0, cur(i) % tps)),
            rows(cur, N_B),
            rows(cur, LANES),
            rows(prev, D_MODEL),
            mod_piece(2), mod_piece(3), mod_piece(4), mod_piece(5),
            const((1, LANES)),
            const((1, GROUP_WIDTH)),
            const((HEADS, CHUNK, CHUNK)),
            const((HEADS, CHUNK, LANES)),
            const((D_MODEL, D_MODEL)),
            const((1, D_MODEL)),
            const((D_MODEL, 2 * D_FF)),
            const((D_FF, D_MODEL)),
            const((1, D_MODEL)),
        ],
        out_specs=[
            rows(prev, D_MODEL),
            per_seq((HEADS, HEAD_DIM, HEAD_DIM)),
            per_seq((HEADS, HEAD_DIM)),
            per_seq((HEADS, LANES)),
        ],
        out_shape=[
            jax.ShapeDtypeStruct((groups, t, D_MODEL), F32),
            jax.ShapeDtypeStruct((groups, HEADS, HEAD_DIM, HEAD_DIM), F32),
            jax.ShapeDtypeStruct((groups, HEADS, HEAD_DIM), F32),
            jax.ShapeDtypeStruct((groups, HEADS, LANES), F32),
        ],
        scratch_shapes=[
            pltpu.VMEM((tm, D_MODEL), BF16),
            pltpu.VMEM((HEADS, HEAD_DIM, 2 * HEAD_DIM), F32),
            pltpu.VMEM((HEADS, 8, LANES), F32),
        ],
        compiler_params=pltpu.CompilerParams(
            dimension_semantics=("arbitrary",), vmem_limit_bytes=VMEM_LIMIT),
        name="prompt_tail",
    )(a, kt, b, gates, x, mod, mod, mod, mod, bias_row, g_head, ws_tril, bs_rep, w_out, g2, w_gu, w_dn, g_final)


SAMPLE_TOKENS_PER_STEP = 64
SAMPLE_UNROLL = 8


def _mix_sample_kernel(q_ref, k_ref, v_ref, vsn_ref, og_ref, ug_ref, g_ref, m0_ref, n0_ref, c_ref,
                       bias_ref, gh_ref, ws0_ref, bs0_ref,
                       mix_ref, c_out, n_out, m_out,
                       vt_ref, kp_ref, wd_ref, cqt_ref):
    hd = pl.program_id(0)
    grp = pl.program_id(1)
    nb = q_ref.shape[0]
    lane = lax.broadcasted_iota(jnp.int32, (nb, LANES), 1)

    def gate_terms():
        pre = g_ref[...] + bias_ref[...]
        i_pre = jnp.sum(jnp.where(lane == hd, pre, 0.0), axis=1, keepdims=True)
        f_pre = jnp.sum(jnp.where(lane == hd + HEADS, pre, 0.0), axis=1, keepdims=True)
        m_prev = jnp.sum(jnp.where(lane == hd, m0_ref[...], 0.0), axis=1, keepdims=True)
        inter = _log_sigmoid(f_pre) + m_prev
        m_t = jnp.maximum(inter, i_pre)
        return m_t, jnp.exp(i_pre - m_t), jnp.exp(inter - m_t)

    @pl.when(grp == 0)
    def _():
        _, w_in, w_dec = gate_terms()
        vt_ref[...] = v_ref[...].T
        kp_ref[...] = (w_in * k_ref[...]).astype(BF16)
        wd_ref[...] = jnp.broadcast_to(w_dec, wd_ref.shape)
        cqt_ref[...] = jnp.zeros_like(cqt_ref)

    lane_sq = lax.broadcasted_iota(jnp.int32, (HEAD_DIM, LANES), 1)
    tokens = c_ref.shape[0]

    def body(r, cqt):
        tok = grp * tokens + r
        q_row = q_ref[pl.ds(tok, 1), :]
        c_b = c_ref[r]
        sel = lane_sq == tok
        cq_col = jnp.sum(c_b * q_row, axis=1, keepdims=True)
        v_sel = jnp.where(sel, vt_ref[...], 0.0).astype(BF16)
        outer = _dot(v_sel, kp_ref[...])
        c_out[r] = wd_ref[pl.ds(tok, 1), :] * c_b + outer
        return jnp.where(sel, cq_col, cqt)

    cqt_ref[...] = lax.fori_loop(0, tokens, body, cqt_ref[...], unroll=SAMPLE_UNROLL)

    @pl.when(grp == pl.num_programs(1) - 1)
    def _():
        m_t, w_in, w_dec = gate_terms()
        q = q_ref[...]
        k = k_ref[...]
        v = v_ref[...]
        n0 = n0_ref[...]
        cq = cqt_ref[...].T
        s = jnp.sum(q * k, axis=1, keepdims=True) * (QK_SCALE * w_in)
        w_inter = w_dec * QK_SCALE
        num = s * v + w_inter * cq
        den = s + w_inter * jnp.sum(n0 * q, axis=1, keepdims=True)
        hh = num / jnp.maximum(jnp.abs(den), jnp.exp(-m_t))
        ml = _rms(hh) * gh_ref[...] * og_ref[...]
        cm = ug_ref[...] * (ws0_ref[...] * vsn_ref[...] + bs0_ref[...])
        n_out[...] = w_dec * n0 + w_in * k
        m_out[...] = jnp.broadcast_to(m_t, m_out.shape)
        for hh_static in range(HEADS):
            @pl.when(hd == hh_static)
            def _():
                mix_ref[:, hh_static * HEAD_DIM: (hh_static + 1) * HEAD_DIM] = ml
                mix_ref[:, GROUP_WIDTH + hh_static * HEAD_DIM: GROUP_WIDTH + (hh_static + 1) * HEAD_DIM] = cm


def _mix_sample(a, k, b, gates, m0_pad, n0, c0, bias_row, g_head, ws0_row, bs0_row):
    nb = a.shape[0]
    tb = SAMPLE_TOKENS_PER_STEP

    def head_block(offset):
        return pl.BlockSpec((nb, HEAD_DIM), lambda h, g: (0, offset + h))

    full = pl.BlockSpec((nb, LANES), lambda h, g: (0, 0))
    head_row = pl.BlockSpec((1, HEAD_DIM), lambda h, g: (0, h))
    c_spec = pl.BlockSpec((tb, None, HEAD_DIM, HEAD_DIM), lambda h, g: (g, h, 0, 0))
    return pl.pallas_call(
        _mix_sample_kernel,
        grid=(HEADS, nb // tb),
        in_specs=[
            head_block(0), head_block(0), head_block(HEADS), head_block(2 * HEADS),
            head_block(0), head_block(HEADS),
            full, full, head_block(0), c_spec,
            pl.BlockSpec((1, LANES), lambda h, g: (0, 0)),
            head_row, head_row, head_row,
        ],
        out_specs=[
            pl.BlockSpec((nb, D_MODEL), lambda h, g: (0, 0)),
            c_spec,
            head_block(0),
            head_block(0),
        ],
        out_shape=[
            jax.ShapeDtypeStruct((nb, D_MODEL), F32),
            jax.ShapeDtypeStruct(c0.shape, F32),
            jax.ShapeDtypeStruct((nb, GROUP_WIDTH), F32),
            jax.ShapeDtypeStruct((nb, GROUP_WIDTH), F32),
        ],
        scratch_shapes=[
            pltpu.VMEM((HEAD_DIM, nb), F32),
            pltpu.VMEM((nb, HEAD_DIM), BF16),
            pltpu.VMEM((nb, LANES), F32),
            pltpu.VMEM((HEAD_DIM, nb), F32),
        ],
        compiler_params=pltpu.CompilerParams(
            dimension_semantics=("arbitrary", "arbitrary"), vmem_limit_bytes=VMEM_LIMIT),
        name="mix_sample",
    )(a, k, a, a, b, b, gates, m0_pad, n0, c0, bias_row, g_head, ws0_row, bs0_row)


def _out_kernel(x_ref, mix_ref, gt1_ref, sh2_ref, sc2_ref, gt2_ref, wo_ref, g2_ref, wgu_ref, wdn_ref, gf_ref,
                y_ref):
    x1 = x_ref[...] + gt1_ref[...] * _dot(mix_ref[...].astype(BF16), wo_ref[...])
    h2 = _rms(x1) * g2_ref[...]
    h2 = (h2 * (1.0 + sc2_ref[...]) + sh2_ref[...]).astype(BF16)
    gate = _dot(h2, wgu_ref[:, :D_FF])
    up = _dot(h2, wgu_ref[:, D_FF:])
    act = (gate * _sigmoid(gate) * up).astype(BF16)
    x2 = x1 + gt2_ref[...] * _dot(act, wdn_ref[...])
    y_ref[...] = _rms(x2) * gf_ref[...]


def _output_stage(x, mix, mod, w_out, g2, w_gu, w_dn, g_final, *, tm, per_row):
    groups, t, _ = x.shape
    return pl.pallas_call(
        _out_kernel,
        grid=(groups, t // tm),
        in_specs=[
            pl.BlockSpec((None, tm, D_MODEL), lambda g, i: (g, i, 0)),
            pl.BlockSpec((None, tm, D_MODEL), lambda g, i: (g, i, 0)),
            _mod_spec(2, tm, per_row),
            _mod_spec(3, tm, per_row),
            _mod_spec(4, tm, per_row),
            _mod_spec(5, tm, per_row),
            _const_spec((D_MODEL, D_MODEL)),
            _const_spec((1, D_MODEL)),
            _const_spec((D_MODEL, 2 * D_FF)),
            _const_spec((D_FF, D_MODEL)),
            _const_spec((1, D_MODEL)),
        ],
        out_specs=pl.BlockSpec((None, tm, D_MODEL), lambda g, i: (g, i, 0)),
        out_shape=jax.ShapeDtypeStruct((groups, t, D_MODEL), F32),
        compiler_params=pltpu.CompilerParams(
            dimension_semantics=("arbitrary", "arbitrary"), vmem_limit_bytes=VMEM_LIMIT),
        name="output_stage",
    )(x, mix, mod, mod, mod, mod, w_out, g2, w_gu, w_dn, g_final)


def kernel(x_prompt, x_sample, c_prompt, c_sample, state_mlstm_C, state_mlstm_n, state_mlstm_m, w_ada, b_ada, g_norm1, w_in, b_gate, g_mlstm_head, ln_v_g, ln_v_b, w_s, b_s, w_out, g_norm2, w_gate_up, w_down, g_final):
    depth = w_ada.shape[0]
    assert depth == 1, "single-layer trunk"
    batch, seq, _ = x_prompt.shape
    nb = x_sample.shape[0]
    assert x_sample.shape[1] == 1

    gw = GROUP_WIDTH
    g1 = g_norm1[0].reshape(1, D_MODEL)
    g2 = g_norm2[0].reshape(1, D_MODEL)
    gf = g_final.reshape(1, D_MODEL)
    ln_g = ln_v_g[0].reshape(1, gw)
    ln_b = ln_v_b[0].reshape(1, gw)
    g_head = g_mlstm_head[0].reshape(1, gw)
    bias_row = jnp.pad(b_gate[0], (0, LANES - 2 * HEADS)).reshape(1, LANES)
    tril = jnp.tril(jnp.ones((CHUNK, CHUNK), dtype=bool))
    ws_tril = jnp.where(tril[None], w_s[0], 0.0).astype(BF16)
    bs_rep = jnp.broadcast_to(b_s[0][:, :, None], (HEADS, CHUNK, LANES))
    ws0_row = jnp.repeat(w_s[0][:, 0, 0], HEAD_DIM).reshape(1, gw)
    bs0_row = jnp.repeat(b_s[0][:, 0], HEAD_DIM).reshape(1, gw)

    mod_s, mod_p, w_proj, w_k, w_kt = _modulation(c_sample, c_prompt, w_ada[0], b_ada[0].reshape(1, -1), w_in[0].T)
    mod_p = mod_p.reshape(batch, 1, N_MOD * D_MODEL)
    mod_s = mod_s.reshape(1, nb, N_MOD * D_MODEL)

    a_p, b_p, g_p, kt_p, w_out_b, w_gu_b, w_dn_b = _project(
        x_prompt, mod_p, g1, w_proj, ln_g, ln_b, w_kt, (w_out[0], w_gate_up[0], w_down[0]),
        k_transposed=True, tm=PROJ_TILE, per_row=False, a_dtype=BF16)
    y_p, c_p, n_p, m_p = _prompt_tail(a_p, kt_p, b_p, g_p, x_prompt, mod_p, bias_row, g_head, ws_tril, bs_rep,
                                      w_out_b, g2, w_gu_b, w_dn_b, gf)

    xs = x_sample.reshape(1, nb, D_MODEL)
    a_s, b_s_act, g_s, k_s = _project(xs, mod_s, g1, w_proj, ln_g, ln_b, w_k, k_transposed=False,
                                      tm=nb, per_row=True, a_dtype=F32)
    a_s2, b_s2, g_s2 = a_s[0], b_s_act[0], g_s[0]
    m0_pad = jnp.pad(state_mlstm_m[0], ((0, 0), (0, LANES - HEADS)))
    n0 = state_mlstm_n[0].reshape(nb, gw)
    mix_s, c_s, n_s, m_s = _mix_sample(a_s2, k_s[0], b_s2, g_s2, m0_pad, n0, state_mlstm_C[0],
                                       bias_row, g_head, ws0_row, bs0_row)
    y_s = _output_stage(xs, mix_s.reshape(1, nb, D_MODEL), mod_s, w_out_b, g2, w_gu_b, w_dn_b, gf,
                        tm=nb, per_row=True)

    return (
        y_p,
        y_s.reshape(nb, 1, D_MODEL),
        c_p[None],
        n_p[None],
        m_p[:, :, 0][None],
        c_s[None],
        n_s.reshape(nb, HEADS, HEAD_DIM)[None],
        m_s.reshape(nb, HEADS, HEAD_DIM)[:, :, 0][None],
        a_s2[:, 2 * gw:].reshape(nb, 1, HEADS, HEAD_DIM)[None],
    )
```

```python
import functools
import math

import jax
import jax.numpy as jnp
from jax import lax
from jax.experimental import pallas as pl
from jax.experimental.pallas import tpu as pltpu

F32 = jnp.float32
BF16 = jnp.bfloat16

D_MODEL = 1024
HEADS = 4
HEAD_DIM = 128
GROUP_WIDTH = HEADS * HEAD_DIM
CHUNK = 128
D_FF = 2816
N_MOD = 6
EPS = 1e-6
QK_SCALE = HEAD_DIM ** -0.5
LANES = 128
BF16_SUBLANES = 16

N_B = 2 * GROUP_WIDTH

VMEM_LIMIT = 56 * 1024 * 1024


def _dot(a, b):
    return jnp.dot(a, b, preferred_element_type=F32)


def _dot_nt(a, b):
    return lax.dot_general(a, b, (((1,), (1,)), ((), ())), preferred_element_type=F32)


def _sigmoid(x):
    return 1.0 / (1.0 + jnp.exp(-x))


def _gelu_tanh(x):
    c = math.sqrt(2.0 / math.pi)
    return x * (0.5 * (1.0 + jnp.tanh(c * (x + 0.044715 * (x * x * x)))))


def _log_sigmoid(x):
    return jnp.minimum(x, 0.0) - jnp.log1p(jnp.exp(-jnp.abs(x)))


def _rms(x):
    return x * lax.rsqrt(jnp.mean(x * x, axis=-1, keepdims=True) + EPS)


def _split3_bf16(x):
    hi = x.astype(BF16)
    r1 = x - hi.astype(F32)
    mid = r1.astype(BF16)
    lo = (r1 - mid.astype(F32)).astype(BF16)
    return hi, mid, lo


MOD_TILE = 1024

IN_Q, IN_K, IN_V, IN_O, IN_GATES = 0, GROUP_WIDTH, 2 * GROUP_WIDTH, 3 * GROUP_WIDTH, 4 * GROUP_WIDTH
IN_U = IN_GATES + 2 * HEADS
IN_VS = IN_U + GROUP_WIDTH
N_IN = IN_VS + GROUP_WIDTH
MAIN_GROUPS = (IN_Q, IN_V, IN_VS, IN_O, IN_U)
N_MAIN = len(MAIN_GROUPS) * GROUP_WIDTH


def _mod_kernel(cs_ref, cp_ref, w_ref, b_ref, grp_ref, gate_ref, os_ref, op_ref, wmain_ref, wgate_ref, wk_ref, wkt_ref):
    step = pl.program_id(0)
    ns, npr = cs_ref.shape[0], cp_ref.shape[0]
    pad = -(ns + npr) % BF16_SUBLANES
    c = jnp.concatenate([cs_ref[...], cp_ref[...], jnp.zeros((pad, D_MODEL), F32)], axis=0)
    a = (c * _sigmoid(c)).astype(BF16)
    res = _dot(a, w_ref[...].astype(BF16)) + b_ref[...]
    os_ref[...] = res[:ns]
    op_ref[...] = res[ns:ns + npr]

    def transposed_into(dst_ref):
        for blk in range(GROUP_WIDTH // LANES):
            rows = grp_ref[blk * LANES:(blk + 1) * LANES, :]
            dst_ref[:, blk * LANES:(blk + 1) * LANES] = rows.T.astype(BF16)

    @pl.when(step < len(MAIN_GROUPS))
    def _():
        transposed_into(wmain_ref)

    @pl.when(step == len(MAIN_GROUPS))
    def _():
        wkt_ref[...] = grp_ref[...].astype(BF16)
        transposed_into(wk_ref)

    @pl.when(step == 0)
    def _():
        gate_rows = jnp.concatenate([gate_ref[...], jnp.zeros((LANES - 2 * HEADS, D_MODEL), F32)], axis=0)
        wgate_ref[...] = gate_rows.T.astype(BF16)


def _modulation(c_sample, c_prompt, w_ada, b_ada, w_in_t):
    ns, npr = c_sample.shape[0], c_prompt.shape[0]
    tn = MOD_TILE
    steps = N_MOD * D_MODEL // tn
    group_starts = MAIN_GROUPS + (IN_K,)
    assert w_in_t.shape == (N_IN, D_MODEL) and steps == len(group_starts)

    def whole(shape):
        return pl.BlockSpec(shape, lambda j: (0, 0))

    def group_start(j):
        assert all(start % 8 == 0 for start in group_starts)
        return pl.multiple_of(sum(jnp.where(j == i, start, 0) for i, start in enumerate(group_starts)), 8)

    return pl.pallas_call(
        _mod_kernel,
        grid=(steps,),
        in_specs=[
            whole((ns, D_MODEL)),
            whole((npr, D_MODEL)),
            pl.BlockSpec((D_MODEL, tn), lambda j: (0, j)),
            pl.BlockSpec((1, tn), lambda j: (0, j)),
            pl.BlockSpec((pl.Element(GROUP_WIDTH), pl.Element(D_MODEL)), lambda j: (group_start(j), 0)),
            pl.BlockSpec((pl.Element(2 * HEADS), pl.Element(D_MODEL)), lambda j: (IN_GATES, 0)),
        ],
        out_specs=[
            pl.BlockSpec((ns, tn), lambda j: (0, j)),
            pl.BlockSpec((npr, tn), lambda j: (0, j)),
            pl.BlockSpec((D_MODEL, GROUP_WIDTH), lambda j: (0, jnp.minimum(j, len(MAIN_GROUPS) - 1))),
            whole((D_MODEL, LANES)),
            whole((D_MODEL, GROUP_WIDTH)),
            whole((GROUP_WIDTH, D_MODEL)),
        ],
        out_shape=[
            jax.ShapeDtypeStruct((ns, N_MOD * D_MODEL), F32),
            jax.ShapeDtypeStruct((npr, N_MOD * D_MODEL), F32),
            jax.ShapeDtypeStruct((D_MODEL, N_MAIN), BF16),
            jax.ShapeDtypeStruct((D_MODEL, LANES), BF16),
            jax.ShapeDtypeStruct((D_MODEL, GROUP_WIDTH), BF16),
            jax.ShapeDtypeStruct((GROUP_WIDTH, D_MODEL), BF16),
        ],
        compiler_params=pltpu.CompilerParams(dimension_semantics=("arbitrary",), vmem_limit_bytes=VMEM_LIMIT),
        name="modulation",
    )(c_sample, c_prompt, w_ada, b_ada, w_in_t, w_in_t)


def _mod_spec(piece, tm, per_row):
    if per_row:
        return pl.BlockSpec((None, tm, D_MODEL), lambda g, t: (g, t, piece))
    return pl.BlockSpec((None, 1, D_MODEL), lambda g, t: (g, 0, piece))


def _const_spec(shape):
    nd = len(shape)
    return pl.BlockSpec(shape, lambda g, t: (0,) * nd, pipeline_mode=pl.Buffered(1))


PROJ_TILE = 1024


def _proj_body(x_ref, sh_ref, sc_ref, g1_ref, w_ref, wg_ref, lng_ref, lnb_ref, a_ref, b_ref, g_ref):
    n_a = a_ref.shape[1]
    n_copy = n_a - GROUP_WIDTH
    x = x_ref[...]
    h = _rms(x) * g1_ref[...]
    h = (h * (1.0 + sc_ref[...]) + sh_ref[...]).astype(BF16)
    p = _dot(h, w_ref[...])
    a_ref[:, :n_copy] = p[:, :n_copy].astype(a_ref.dtype)
    vs = _gelu_tanh(p[:, n_copy:n_a])
    for hd in range(HEADS):
        sl = slice(hd * HEAD_DIM, (hd + 1) * HEAD_DIM)
        v = vs[:, sl]
        mu = jnp.mean(v, axis=-1, keepdims=True)
        vc = v - mu
        var = jnp.mean(vc * vc, axis=-1, keepdims=True)
        y = vc * lax.rsqrt(var + EPS) * lng_ref[:, sl] + lnb_ref[:, sl]
        a_ref[:, n_copy + hd * HEAD_DIM: n_copy + (hd + 1) * HEAD_DIM] = y.astype(a_ref.dtype)
    b_ref[:, :GROUP_WIDTH] = _sigmoid(p[:, n_a: n_a + GROUP_WIDTH])
    b_ref[:, GROUP_WIDTH:] = _gelu_tanh(p[:, n_a + GROUP_WIDTH: n_a + N_B])
    g_ref[...] = _dot(h, wg_ref[...])
    return h


def _proj_kernel(*refs, k_transposed, n_cast):
    ins, outs = refs[:9 + n_cast], refs[9 + n_cast:]
    wk_ref, k_ref = ins[8], outs[3]
    h = _proj_body(*ins[:8], *outs[:3])
    if k_transposed:
        k_ref[...] = _dot_nt(wk_ref[...], h).astype(k_ref.dtype)
    else:
        k_ref[...] = _dot(h, wk_ref[...]).astype(k_ref.dtype)
    for src, dst in zip(ins[9:], outs[4:]):
        dst[...] = src[...].astype(dst.dtype)


def _project(x, mod, g1, w_main, w_gates, ln_g, ln_b, w_k, cast_weights=(), *, k_transposed, tm, per_row, a_dtype):
    groups, t, _ = x.shape
    n_a = w_main.shape[1] - N_B
    steps_per_group = t // tm
    n_steps = groups * steps_per_group
    in_specs = [
        pl.BlockSpec((None, tm, D_MODEL), lambda g, i: (g, i, 0)),
        _mod_spec(0, tm, per_row),
        _mod_spec(1, tm, per_row),
        _const_spec((1, D_MODEL)),
        _const_spec(w_main.shape),
        _const_spec(w_gates.shape),
        _const_spec((1, GROUP_WIDTH)),
        _const_spec((1, GROUP_WIDTH)),
    ]
    out_specs = [
        pl.BlockSpec((None, tm, n_a), lambda g, i: (g, i, 0)),
        pl.BlockSpec((None, tm, N_B), lambda g, i: (g, i, 0)),
        pl.BlockSpec((None, tm, LANES), lambda g, i: (g, i, 0)),
    ]
    out_shape = [
        jax.ShapeDtypeStruct((groups, t, n_a), a_dtype),
        jax.ShapeDtypeStruct((groups, t, N_B), F32),
        jax.ShapeDtypeStruct((groups, t, LANES), F32),
    ]
    args = [x, mod, mod, g1, w_main, w_gates, ln_g, ln_b, w_k]
    in_specs.append(_const_spec(w_k.shape))
    if k_transposed:
        out_specs.append(pl.BlockSpec((None, GROUP_WIDTH, tm), lambda g, i: (g, 0, i)))
        out_shape.append(jax.ShapeDtypeStruct((groups, GROUP_WIDTH, t), a_dtype))
    else:
        out_specs.append(pl.BlockSpec((None, tm, GROUP_WIDTH), lambda g, i: (g, i, 0)))
        out_shape.append(jax.ShapeDtypeStruct((groups, t, GROUP_WIDTH), a_dtype))
    for w in cast_weights:
        rows, cols = w.shape
        assert rows % (n_steps * 16) == 0, "row block must be a whole number of bf16 sublane tiles"
        spec = pl.BlockSpec((rows // n_steps, cols), lambda g, i: (g * steps_per_group + i, 0))
        in_specs.append(spec)
        out_specs.append(spec)
        out_shape.append(jax.ShapeDtypeStruct((rows, cols), BF16))
        args.append(w)
    return pl.pallas_call(
        functools.partial(_proj_kernel, k_transposed=k_transposed, n_cast=len(cast_weights)),
        grid=(groups, t // tm),
        in_specs=in_specs,
        out_specs=out_specs,
        out_shape=out_shape,
        compiler_params=pltpu.CompilerParams(
            dimension_semantics=("arbitrary", "arbitrary"), vmem_limit_bytes=VMEM_LIMIT),
        name="project",
    )(*args)


MIX_TILE = 512


def _dot3_rhs(lhs_bf16, rhs_f32):
    hi, mid, lo = _split3_bf16(rhs_f32)
    return _dot(lhs_bf16, hi) + _dot(lhs_bf16, mid) + _dot(lhs_bf16, lo)


def _dot3_lhs(lhs_f32, rhs_bf16):
    hi, mid, lo = _split3_bf16(lhs_f32)
    return _dot(hi, rhs_bf16) + _dot(mid, rhs_bf16) + _dot(lo, rhs_bf16)


FF_CHUNKS = (768, 768, 768, 512)
MIXER_LOOKAHEAD = 2
OUT_ROW_BLOCKS = 2


def _gate_rows(g_ref, bias_ref, tril, triu, n_chunks):
    pre = [g_ref[c * CHUNK:(c + 1) * CHUNK, :] + bias_ref[...] for c in range(n_chunks)]
    bc_all = _dot3_rhs(tril, jnp.concatenate([_log_sigmoid(p) for p in pre], axis=1))
    rows_all = jnp.concatenate([p.T[0:2 * HEADS, :] for p in pre], axis=0)
    b_rows_all = _dot3_lhs(_log_sigmoid(rows_all), triu)
    bc, arow, blast = [], [], []
    for c in range(n_chunks):
        bc.append(bc_all[:, c * LANES:(c + 1) * LANES])
        rows = rows_all[c * 2 * HEADS:(c + 1) * 2 * HEADS, :]
        b_rows = b_rows_all[c * 2 * HEADS:(c + 1) * 2 * HEADS, :]
        arow.append([rows[hd:hd + 1, :] - b_rows[HEADS + hd:HEADS + hd + 1, :] for hd in range(HEADS)])
        blast.append([jnp.min(b_rows[HEADS + hd:HEADS + hd + 1, :], axis=1, keepdims=True) for hd in range(HEADS)])
    return bc, arow, blast


def _mlstm_local(q, kt, a_row, b_last, causal):
    amat = jnp.where(causal, a_row, -jnp.inf)
    m_row = jnp.max(amat, axis=1, keepdims=True)
    s_loc = (_dot(q, kt) * jnp.exp(amat - m_row)).astype(BF16)
    g_row = b_last + a_row
    g_loc = jnp.max(g_row, axis=1, keepdims=True)
    kw = (kt.astype(F32) * jnp.exp(g_row - g_loc)).astype(BF16)
    return m_row, s_loc, g_loc, kw


def _mlstm_readout(local, q, v, og, g_head, b_last, b_rep, cta, m_prev, ones_blk):
    m_row, s_loc, g_loc, kw = local
    va = jnp.concatenate([v, ones_blk], axis=1)
    nd_loc = _dot(s_loc, va)
    u_aug = _dot(kw, va)
    inter = _dot(q, cta.astype(BF16))
    mm = jnp.maximum(m_prev, m_row)
    f_loc = jnp.exp(m_row - mm) * QK_SCALE
    f_int = jnp.exp(m_prev - mm) * QK_SCALE
    nd = f_loc * nd_loc + f_int * inter
    clamp = jnp.exp(-(b_rep + mm))
    hh = nd[:, :HEAD_DIM] / jnp.maximum(jnp.abs(nd[:, HEAD_DIM:]), clamp)
    ml = _rms(hh) * g_head * og

    dec = b_last + m_prev
    m_new = jnp.maximum(dec, g_loc)
    cta_new = jnp.exp(dec - m_new) * cta + jnp.exp(g_loc - m_new) * u_aug
    return ml, cta_new, m_new


def _prompt_tail_kernel(a_ref, kt_ref, b_ref, g_ref, x_ref, gt1_ref, sh2_ref, sc2_ref, gt2_ref,
                        bias_ref, gh_ref, ws_ref, bs_ref, wo_ref, g2_ref, wgu_ref, wdn_ref, gf_ref,
                        y_ref, c_out, n_out, m_out,
                        mix_ref, cta_ref, m_ref, *, tiles_per_seq):
    step = pl.program_id(0)
    n_tiles = pl.num_programs(0) - 1
    refs = (a_ref, kt_ref, b_ref, g_ref, x_ref, gt1_ref, sh2_ref, sc2_ref, gt2_ref,
            bias_ref, gh_ref, ws_ref, bs_ref, wo_ref, g2_ref, wgu_ref, wdn_ref, gf_ref,
            y_ref, mix_ref, cta_ref, m_ref)

    @pl.when(step % tiles_per_seq == 0)
    def _():
        cta_ref[...] = jnp.zeros_like(cta_ref)
        m_ref[...] = jnp.zeros_like(m_ref)

    @pl.when(step == 0)
    def _():
        _prompt_tail_body(refs, with_mixer=True, with_out=False)

    @pl.when(jnp.logical_and(step > 0, step < n_tiles))
    def _():
        _prompt_tail_body(refs, with_mixer=True, with_out=True)

    @pl.when(step == n_tiles)
    def _():
        _prompt_tail_body(refs, with_mixer=False, with_out=True)

    @pl.when(jnp.logical_and(step % tiles_per_seq == tiles_per_seq - 1, step < n_tiles))
    def _():
        for hd in range(HEADS):
            cta = cta_ref[hd]
            c_out[hd] = cta[:, :HEAD_DIM].T
            n_out[hd: hd + 1, :] = cta[:, HEAD_DIM:].T[0:1, :]
            m_out[hd: hd + 1, :] = m_ref[hd][0:1, :]


def _prompt_tail_body(refs, *, with_mixer, with_out):
    (a_ref, kt_ref, b_ref, g_ref, x_ref, gt1_ref, sh2_ref, sc2_ref, gt2_ref,
     bias_ref, gh_ref, ws_ref, bs_ref, wo_ref, g2_ref, wgu_ref, wdn_ref, gf_ref,
     y_ref, mix_ref, cta_ref, m_ref) = refs
    tm = a_ref.shape[0]
    n_chunks = tm // CHUNK
    n_copy = a_ref.shape[1] - GROUP_WIDTH

    row = lax.broadcasted_iota(jnp.int32, (CHUNK, CHUNK), 0)
    col = lax.broadcasted_iota(jnp.int32, (CHUNK, CHUNK), 1)
    causal = row >= col
    tril = jnp.where(causal, 1.0, 0.0).astype(BF16)
    triu = jnp.where(row <= col, 1.0, 0.0).astype(BF16)
    ones_blk = jnp.ones((CHUNK, HEAD_DIM), BF16)

    blocks = [slice(r0, r0 + tm // OUT_ROW_BLOCKS) for r0 in range(0, tm, tm // OUT_ROW_BLOCKS)] if with_out else []
    outproj = [_dot(mix_ref[rs, :], wo_ref[...]) for rs in blocks]
    x1, h2 = [], []
    for rs, op in zip(blocks, outproj):
        x1.append(x_ref[rs, :] + gt1_ref[...] * op)
        h = _rms(x1[-1]) * g2_ref[...]
        h2.append((h * (1.0 + sc2_ref[...]) + sh2_ref[...]).astype(BF16))

    if with_mixer:
        bc, arow, blast = _gate_rows(g_ref, bias_ref, tril, triu, n_chunks)
        state = [(cta_ref[hd], jnp.max(m_ref[hd][0:1, :], axis=1, keepdims=True)) for hd in range(HEADS)]
    sg_all = [None] * HEADS

    def mixer_local(c, hd):
        rs = slice(c * CHUNK, (c + 1) * CHUNK)
        sl = slice(hd * HEAD_DIM, (hd + 1) * HEAD_DIM)
        return _mlstm_local(a_ref[rs, sl], kt_ref[sl, rs], arow[c][hd], blast[c][hd], causal)

    def mixer_readout(c, hd, local):
        rs = slice(c * CHUNK, (c + 1) * CHUNK)
        sl = slice(hd * HEAD_DIM, (hd + 1) * HEAD_DIM)
        if c == 0:
            vsn_all = jnp.concatenate(
                [a_ref[cc * CHUNK:(cc + 1) * CHUNK, n_copy + hd * HEAD_DIM: n_copy + (hd + 1) * HEAD_DIM]
                 for cc in range(n_chunks)], axis=1)
            sg_all[hd] = _dot(ws_ref[hd], vsn_all)
        cta, m_prev = state[hd]
        b_rep = jnp.broadcast_to(bc[c][:, HEADS + hd: HEADS + hd + 1], (CHUNK, HEAD_DIM))
        ml, cta, m_prev = _mlstm_readout(
            local, a_ref[rs, sl], a_ref[rs, GROUP_WIDTH + hd * HEAD_DIM: GROUP_WIDTH + (hd + 1) * HEAD_DIM],
            b_ref[rs, sl], gh_ref[:, sl], blast[c][hd], b_rep, cta, m_prev, ones_blk)
        state[hd] = (cta, m_prev)
        mix_ref[rs, sl] = ml.astype(mix_ref.dtype)
        sg = sg_all[hd][:, c * CHUNK:(c + 1) * CHUNK] + bs_ref[hd]
        ug = b_ref[rs, GROUP_WIDTH + hd * HEAD_DIM: GROUP_WIDTH + (hd + 1) * HEAD_DIM]
        mix_ref[rs, GROUP_WIDTH + hd * HEAD_DIM: GROUP_WIDTH + (hd + 1) * HEAD_DIM] = (ug * sg).astype(mix_ref.dtype)

    pieces = [(c, hd) for c in range(n_chunks) for hd in range(HEADS)] if with_mixer else []
    per_ff = -(-len(pieces) // len(FF_CHUNKS))
    ahead = [mixer_local(*pieces[p]) for p in range(min(MIXER_LOOKAHEAD, len(pieces)))]
    acc = [None] * len(blocks)
    f0 = 0
    for j, width in enumerate(FF_CHUNKS):
        gate_up = [(_dot(h, wgu_ref[:, f0:f0 + width]), _dot(h, wgu_ref[:, D_FF + f0:D_FF + f0 + width]))
                   for h in h2]
        for idx in range(j * per_ff, min((j + 1) * per_ff, len(pieces))):
            if idx + MIXER_LOOKAHEAD < len(pieces):
                ahead.append(mixer_local(*pieces[idx + MIXER_LOOKAHEAD]))
            mixer_readout(*pieces[idx], ahead.pop(0))
        for r, (gate, up) in enumerate(gate_up):
            act = (gate * _sigmoid(gate) * up).astype(BF16)
            part = _dot(act, wdn_ref[f0:f0 + width, :])
            acc[r] = part if acc[r] is None else acc[r] + part
            if j + 1 == len(FF_CHUNKS):
                x2 = x1[r] + gt2_ref[...] * acc[r]
                y_ref[blocks[r], :] = _rms(x2) * gf_ref[...]
        f0 += width

    if with_mixer:
        for hd in range(HEADS):
            cta_ref[hd] = state[hd][0]
            m_ref[hd] = jnp.broadcast_to(state[hd][1], m_ref.shape[1:])


def _prompt_tail(a, kt, b, gates, x, mod, bias_row, g_head, ws_tril, bs_rep, w_out, g2, w_gu, w_dn, g_final):
    groups, t, n_a = a.shape
    tm = MIX_TILE
    tps = t // tm
    n_tiles = groups * tps
    assert sum(FF_CHUNKS) == D_FF

    def cur(i):
        return jnp.minimum(i, n_tiles - 1)

    def prev(i):
        return jnp.maximum(i - 1, 0)

    def rows(tile, width):
        return pl.BlockSpec((None, tm, width), lambda i: (tile(i) // tps, tile(i) % tps, 0))

    def mod_piece(piece):
        return pl.BlockSpec((None, 1, D_MODEL), lambda i: (prev(i) // tps, 0, piece))

    def const(shape):
        nd = len(shape)
        return pl.BlockSpec(shape, lambda i: (0,) * nd, pipeline_mode=pl.Buffered(1))

    def per_seq(shape):
        nd = len(shape)
        return pl.BlockSpec((None,) + shape, lambda i: (cur(i) // tps,) + (0,) * nd)

    return pl.pallas_call(
        functools.partial(_prompt_tail_kernel, tiles_per_seq=tps),
        grid=(n_tiles + 1,),
        in_specs=[
            rows(cur, n_a),
            pl.BlockSpec((None, GROUP_WIDTH, tm), lambda i: (cur(i) // tps, 0, cur(i) % tps)),
            rows(cur, N_B),
            rows(cur, LANES),
            rows(prev, D_MODEL),
            mod_piece(2), mod_piece(3), mod_piece(4), mod_piece(5),
            const((1, LANES)),
            const((1, GROUP_WIDTH)),
            const((HEADS, CHUNK, CHUNK)),
            const((HEADS, CHUNK, LANES)),
            const((D_MODEL, D_MODEL)),
            const((1, D_MODEL)),
            const((D_MODEL, 2 * D_FF)),
            const((D_FF, D_MODEL)),
            const((1, D_MODEL)),
        ],
        out_specs=[
            rows(prev, D_MODEL),
            per_seq((HEADS, HEAD_DIM, HEAD_DIM)),
            per_seq((HEADS, HEAD_DIM)),
            per_seq((HEADS, LANES)),
        ],
        out_shape=[
            jax.ShapeDtypeStruct((groups, t, D_MODEL), F32),
            jax.ShapeDtypeStruct((groups, HEADS, HEAD_DIM, HEAD_DIM), F32),
            jax.ShapeDtypeStruct((groups, HEADS, HEAD_DIM), F32),
            jax.ShapeDtypeStruct((groups, HEADS, LANES), F32),
        ],
        scratch_shapes=[
            pltpu.VMEM((tm, D_MODEL), BF16),
            pltpu.VMEM((HEADS, HEAD_DIM, 2 * HEAD_DIM), F32),
            pltpu.VMEM((HEADS, 8, LANES), F32),
        ],
        compiler_params=pltpu.CompilerParams(
            dimension_semantics=("arbitrary",), vmem_limit_bytes=VMEM_LIMIT),
        name="prompt_tail",
    )(a, kt, b, gates, x, mod, mod, mod, mod, bias_row, g_head, ws_tril, bs_rep, w_out, g2, w_gu, w_dn, g_final)


SAMPLE_TOKENS_PER_STEP = 64
SAMPLE_UNROLL = 8


def _mix_sample_kernel(q_ref, k_ref, v_ref, vsn_ref, og_ref, ug_ref, g_ref, m0_ref, n0_ref, c_ref,
                       bias_ref, gh_ref, ws0_ref, bs0_ref,
                       mix_ref, c_out, n_out, m_out,
                       vt_ref, kp_ref, wd_ref, cqt_ref):
    hd = pl.program_id(0)
    grp = pl.program_id(1)
    nb = q_ref.shape[0]
    lane = lax.broadcasted_iota(jnp.int32, (nb, LANES), 1)

    def gate_terms():
        pre = g_ref[...] + bias_ref[...]
        i_pre = jnp.sum(jnp.where(lane == hd, pre, 0.0), axis=1, keepdims=True)
        f_pre = jnp.sum(jnp.where(lane == hd + HEADS, pre, 0.0), axis=1, keepdims=True)
        m_prev = jnp.sum(jnp.where(lane == hd, m0_ref[...], 0.0), axis=1, keepdims=True)
        inter = _log_sigmoid(f_pre) + m_prev
        m_t = jnp.maximum(inter, i_pre)
        return m_t, jnp.exp(i_pre - m_t), jnp.exp(inter - m_t)

    @pl.when(grp == 0)
    def _():
        _, w_in, w_dec = gate_terms()
        vt_ref[...] = v_ref[...].T
        kp_ref[...] = (w_in * k_ref[...]).astype(BF16)
        wd_ref[...] = jnp.broadcast_to(w_dec, wd_ref.shape)
        cqt_ref[...] = jnp.zeros_like(cqt_ref)

    lane_sq = lax.broadcasted_iota(jnp.int32, (HEAD_DIM, LANES), 1)
    tokens = c_ref.shape[0]

    def body(r, cqt):
        tok = grp * tokens + r
        q_row = q_ref[pl.ds(tok, 1), :]
        c_b = c_ref[r]
        sel = lane_sq == tok
        cq_col = jnp.sum(c_b * q_row, axis=1, keepdims=True)
        v_sel = jnp.where(sel, vt_ref[...], 0.0).astype(BF16)
        outer = _dot(v_sel, kp_ref[...])
        c_out[r] = wd_ref[pl.ds(tok, 1), :] * c_b + outer
        return jnp.where(sel, cq_col, cqt)

    cqt_ref[...] = lax.fori_loop(0, tokens, body, cqt_ref[...], unroll=SAMPLE_UNROLL)

    @pl.when(grp == pl.num_programs(1) - 1)
    def _():
        m_t, w_in, w_dec = gate_terms()
        q = q_ref[...]
        k = k_ref[...]
        v = v_ref[...]
        n0 = n0_ref[...]
        cq = cqt_ref[...].T
        s = jnp.sum(q * k, axis=1, keepdims=True) * (QK_SCALE * w_in)
        w_inter = w_dec * QK_SCALE
        num = s * v + w_inter * cq
        den = s + w_inter * jnp.sum(n0 * q, axis=1, keepdims=True)
        hh = num / jnp.maximum(jnp.abs(den), jnp.exp(-m_t))
        ml = _rms(hh) * gh_ref[...] * og_ref[...]
        cm = ug_ref[...] * (ws0_ref[...] * vsn_ref[...] + bs0_ref[...])
        n_out[...] = w_dec * n0 + w_in * k
        m_out[...] = jnp.broadcast_to(m_t, m_out.shape)
        for hh_static in range(HEADS):
            @pl.when(hd == hh_static)
            def _():
                mix_ref[:, hh_static * HEAD_DIM: (hh_static + 1) * HEAD_DIM] = ml
                mix_ref[:, GROUP_WIDTH + hh_static * HEAD_DIM: GROUP_WIDTH + (hh_static + 1) * HEAD_DIM] = cm


def _mix_sample(a, k, b, gates, m0_pad, n0, c0, bias_row, g_head, ws0_row, bs0_row):
    nb = a.shape[0]
    tb = SAMPLE_TOKENS_PER_STEP

    def head_block(offset):
        return pl.BlockSpec((nb, HEAD_DIM), lambda h, g: (0, offset + h))

    full = pl.BlockSpec((nb, LANES), lambda h, g: (0, 0))
    head_row = pl.BlockSpec((1, HEAD_DIM), lambda h, g: (0, h))
    c_spec = pl.BlockSpec((tb, None, HEAD_DIM, HEAD_DIM), lambda h, g: (g, h, 0, 0))
    return pl.pallas_call(
        _mix_sample_kernel,
        grid=(HEADS, nb // tb),
        in_specs=[
            head_block(0), head_block(0), head_block(HEADS), head_block(2 * HEADS),
            head_block(0), head_block(HEADS),
            full, full, head_block(0), c_spec,
            pl.BlockSpec((1, LANES), lambda h, g: (0, 0)),
            head_row, head_row, head_row,
        ],
        out_specs=[
            pl.BlockSpec((nb, D_MODEL), lambda h, g: (0, 0)),
            c_spec,
            head_block(0),
            head_block(0),
        ],
        out_shape=[
            jax.ShapeDtypeStruct((nb, D_MODEL), F32),
            jax.ShapeDtypeStruct(c0.shape, F32),
            jax.ShapeDtypeStruct((nb, GROUP_WIDTH), F32),
            jax.ShapeDtypeStruct((nb, GROUP_WIDTH), F32),
        ],
        scratch_shapes=[
            pltpu.VMEM((HEAD_DIM, nb), F32),
            pltpu.VMEM((nb, HEAD_DIM), BF16),
            pltpu.VMEM((nb, LANES), F32),
            pltpu.VMEM((HEAD_DIM, nb), F32),
        ],
        compiler_params=pltpu.CompilerParams(
            dimension_semantics=("arbitrary", "arbitrary"), vmem_limit_bytes=VMEM_LIMIT),
        name="mix_sample",
    )(a, k, a, a, b, b, gates, m0_pad, n0, c0, bias_row, g_head, ws0_row, bs0_row)


def _out_kernel(x_ref, mix_ref, gt1_ref, sh2_ref, sc2_ref, gt2_ref, wo_ref, g2_ref, wgu_ref, wdn_ref, gf_ref,
                y_ref):
    x1 = x_ref[...] + gt1_ref[...] * _dot(mix_ref[...].astype(BF16), wo_ref[...])
    h2 = _rms(x1) * g2_ref[...]
    h2 = (h2 * (1.0 + sc2_ref[...]) + sh2_ref[...]).astype(BF16)
    gate = _dot(h2, wgu_ref[:, :D_FF])
    up = _dot(h2, wgu_ref[:, D_FF:])
    act = (gate * _sigmoid(gate) * up).astype(BF16)
    x2 = x1 + gt2_ref[...] * _dot(act, wdn_ref[...])
    y_ref[...] = _rms(x2) * gf_ref[...]


def _output_stage(x, mix, mod, w_out, g2, w_gu, w_dn, g_final, *, tm, per_row):
    groups, t, _ = x.shape
    return pl.pallas_call(
        _out_kernel,
        grid=(groups, t // tm),
        in_specs=[
            pl.BlockSpec((None, tm, D_MODEL), lambda g, i: (g, i, 0)),
            pl.BlockSpec((None, tm, D_MODEL), lambda g, i: (g, i, 0)),
            _mod_spec(2, tm, per_row),
            _mod_spec(3, tm, per_row),
            _mod_spec(4, tm, per_row),
            _mod_spec(5, tm, per_row),
            _const_spec((D_MODEL, D_MODEL)),
            _const_spec((1, D_MODEL)),
            _const_spec((D_MODEL, 2 * D_FF)),
            _const_spec((D_FF, D_MODEL)),
            _const_spec((1, D_MODEL)),
        ],
        out_specs=pl.BlockSpec((None, tm, D_MODEL), lambda g, i: (g, i, 0)),
        out_shape=jax.ShapeDtypeStruct((groups, t, D_MODEL), F32),
        compiler_params=pltpu.CompilerParams(
            dimension_semantics=("arbitrary", "arbitrary"), vmem_limit_bytes=VMEM_LIMIT),
        name="output_stage",
    )(x, mix, mod, mod, mod, mod, w_out, g2, w_gu, w_dn, g_final)


def kernel(x_prompt, x_sample, c_prompt, c_sample, state_mlstm_C, state_mlstm_n, state_mlstm_m, w_ada, b_ada, g_norm1, w_in, b_gate, g_mlstm_head, ln_v_g, ln_v_b, w_s, b_s, w_out, g_norm2, w_gate_up, w_down, g_final):
    depth = w_ada.shape[0]
    assert depth == 1, "single-layer trunk"
    batch, seq, _ = x_prompt.shape
    nb = x_sample.shape[0]
    assert x_sample.shape[1] == 1

    gw = GROUP_WIDTH
    g1 = g_norm1[0].reshape(1, D_MODEL)
    g2 = g_norm2[0].reshape(1, D_MODEL)
    gf = g_final.reshape(1, D_MODEL)
    ln_g = ln_v_g[0].reshape(1, gw)
    ln_b = ln_v_b[0].reshape(1, gw)
    g_head = g_mlstm_head[0].reshape(1, gw)
    bias_row = jnp.pad(b_gate[0], (0, LANES - 2 * HEADS)).reshape(1, LANES)
    tril = jnp.tril(jnp.ones((CHUNK, CHUNK), dtype=bool))
    ws_tril = jnp.where(tril[None], w_s[0], 0.0).astype(BF16)
    bs_rep = jnp.broadcast_to(b_s[0][:, :, None], (HEADS, CHUNK, LANES))
    ws0_row = jnp.repeat(w_s[0][:, 0, 0], HEAD_DIM).reshape(1, gw)
    bs0_row = jnp.repeat(b_s[0][:, 0], HEAD_DIM).reshape(1, gw)

    mod_s, mod_p, w_main, w_gates, w_k, w_kt = _modulation(c_sample, c_prompt, w_ada[0], b_ada[0].reshape(1, -1), w_in[0].T)
    mod_p = mod_p.reshape(batch, 1, N_MOD * D_MODEL)
    mod_s = mod_s.reshape(1, nb, N_MOD * D_MODEL)

    a_p, b_p, g_p, kt_p, w_out_b, w_gu_b, w_dn_b = _project(
        x_prompt, mod_p, g1, w_main, w_gates, ln_g, ln_b, w_kt, (w_out[0], w_gate_up[0], w_down[0]),
        k_transposed=True, tm=PROJ_TILE, per_row=False, a_dtype=BF16)
    y_p, c_p, n_p, m_p = _prompt_tail(a_p, kt_p, b_p, g_p, x_prompt, mod_p, bias_row, g_head, ws_tril, bs_rep,
                                      w_out_b, g2, w_gu_b, w_dn_b, gf)

    xs = x_sample.reshape(1, nb, D_MODEL)
    a_s, b_s_act, g_s, k_s = _project(xs, mod_s, g1, w_main, w_gates, ln_g, ln_b, w_k, k_transposed=False,
                                      tm=nb, per_row=True, a_dtype=F32)
    a_s2, b_s2, g_s2 = a_s[0], b_s_act[0], g_s[0]
    m0_pad = jnp.pad(state_mlstm_m[0], ((0, 0), (0, LANES - HEADS)))
    n0 = state_mlstm_n[0].reshape(nb, gw)
    mix_s, c_s, n_s, m_s = _mix_sample(a_s2, k_s[0], b_s2, g_s2, m0_pad, n0, state_mlstm_C[0],
                                       bias_row, g_head, ws0_row, bs0_row)
    y_s = _output_stage(xs, mix_s.reshape(1, nb, D_MODEL), mod_s, w_out_b, g2, w_gu_b, w_dn_b, gf,
                        tm=nb, per_row=True)

    return (
        y_p,
        y_s.reshape(nb, 1, D_MODEL),
        c_p[None],
        n_p[None],
        m_p[:, :, 0][None],
        c_s[None],
        n_s.reshape(nb, HEADS, HEAD_DIM)[None],
        m_s.reshape(nb, HEADS, HEAD_DIM)[:, :, 0][None],
        a_s2[:, 2 * gw:].reshape(nb, 1, HEADS, HEAD_DIM)[None],
    )
```

```python
import functools
import math

import jax
import jax.numpy as jnp
from jax import lax
from jax.experimental import pallas as pl
from jax.experimental.pallas import tpu as pltpu

F32 = jnp.float32
BF16 = jnp.bfloat16

D_MODEL = 1024
HEADS = 4
HEAD_DIM = 128
GROUP_WIDTH = HEADS * HEAD_DIM
CHUNK = 128
D_FF = 2816
N_MOD = 6
EPS = 1e-6
QK_SCALE = HEAD_DIM ** -0.5
LANES = 128
BF16_SUBLANES = 16

N_B = 2 * GROUP_WIDTH

VMEM_LIMIT = 56 * 1024 * 1024


def _dot(a, b):
    return jnp.dot(a, b, preferred_element_type=F32)


def _dot_nt(a, b):
    return lax.dot_general(a, b, (((1,), (1,)), ((), ())), preferred_element_type=F32)


def _sigmoid(x):
    return 1.0 / (1.0 + jnp.exp(-x))


def _gelu_tanh(x):
    c = math.sqrt(2.0 / math.pi)
    return x * (0.5 * (1.0 + jnp.tanh(c * (x + 0.044715 * (x * x * x)))))


def _log_sigmoid(x):
    return jnp.minimum(x, 0.0) - jnp.log1p(jnp.exp(-jnp.abs(x)))


def _rms(x):
    return x * lax.rsqrt(jnp.mean(x * x, axis=-1, keepdims=True) + EPS)


def _split3_bf16(x):
    hi = x.astype(BF16)
    r1 = x - hi.astype(F32)
    mid = r1.astype(BF16)
    lo = (r1 - mid.astype(F32)).astype(BF16)
    return hi, mid, lo


MOD_TILE = 1024

IN_Q, IN_K, IN_V, IN_O, IN_GATES = 0, GROUP_WIDTH, 2 * GROUP_WIDTH, 3 * GROUP_WIDTH, 4 * GROUP_WIDTH
IN_U = IN_GATES + 2 * HEADS
IN_VS = IN_U + GROUP_WIDTH
N_IN = IN_VS + GROUP_WIDTH
MAIN_GROUPS = (IN_Q, IN_V, IN_VS, IN_O, IN_U)
N_MAIN = len(MAIN_GROUPS) * GROUP_WIDTH


def _mod_kernel(cs_ref, cp_ref, w_ref, b_ref, grp_ref, gate_ref, os_ref, op_ref, wmain_ref, wgate_ref, wk_ref, wkt_ref):
    step = pl.program_id(0)
    ns, npr = cs_ref.shape[0], cp_ref.shape[0]
    pad = -(ns + npr) % BF16_SUBLANES
    c = jnp.concatenate([cs_ref[...], cp_ref[...], jnp.zeros((pad, D_MODEL), F32)], axis=0)
    a = (c * _sigmoid(c)).astype(BF16)
    res = _dot(a, w_ref[...].astype(BF16)) + b_ref[...]
    os_ref[...] = res[:ns]
    op_ref[:, 0, :] = res[ns:ns + npr]

    def transposed_into(dst_ref):
        for blk in range(GROUP_WIDTH // LANES):
            rows = grp_ref[blk * LANES:(blk + 1) * LANES, :]
            dst_ref[:, blk * LANES:(blk + 1) * LANES] = rows.T.astype(BF16)

    @pl.when(step < len(MAIN_GROUPS))
    def _():
        transposed_into(wmain_ref)

    @pl.when(step == len(MAIN_GROUPS))
    def _():
        wkt_ref[...] = grp_ref[...].astype(BF16)
        transposed_into(wk_ref)

    @pl.when(step == 0)
    def _():
        gate_rows = jnp.concatenate([gate_ref[...], jnp.zeros((LANES - 2 * HEADS, D_MODEL), F32)], axis=0)
        wgate_ref[...] = gate_rows.T.astype(BF16)


def _modulation(c_sample, c_prompt, w_ada, b_ada, w_in_t):
    ns, npr = c_sample.shape[0], c_prompt.shape[0]
    tn = MOD_TILE
    steps = N_MOD * D_MODEL // tn
    group_starts = MAIN_GROUPS + (IN_K,)
    assert w_in_t.shape == (N_IN, D_MODEL) and steps == len(group_starts)

    def whole(shape):
        return pl.BlockSpec(shape, lambda j: (0, 0))

    def group_start(j):
        assert all(start % 8 == 0 for start in group_starts)
        return pl.multiple_of(sum(jnp.where(j == i, start, 0) for i, start in enumerate(group_starts)), 8)

    return pl.pallas_call(
        _mod_kernel,
        grid=(steps,),
        in_specs=[
            whole((ns, D_MODEL)),
            whole((npr, D_MODEL)),
            pl.BlockSpec((D_MODEL, tn), lambda j: (0, j)),
            pl.BlockSpec((1, tn), lambda j: (0, j)),
            pl.BlockSpec((pl.Element(GROUP_WIDTH), pl.Element(D_MODEL)), lambda j: (group_start(j), 0)),
            pl.BlockSpec((pl.Element(2 * HEADS), pl.Element(D_MODEL)), lambda j: (IN_GATES, 0)),
        ],
        out_specs=[
            pl.BlockSpec((ns, tn), lambda j: (0, j)),
            pl.BlockSpec((npr, 1, tn), lambda j: (0, 0, j)),
            pl.BlockSpec((D_MODEL, GROUP_WIDTH), lambda j: (0, jnp.minimum(j, len(MAIN_GROUPS) - 1))),
            whole((D_MODEL, LANES)),
            whole((D_MODEL, GROUP_WIDTH)),
            whole((GROUP_WIDTH, D_MODEL)),
        ],
        out_shape=[
            jax.ShapeDtypeStruct((ns, N_MOD * D_MODEL), F32),
            jax.ShapeDtypeStruct((npr, 1, N_MOD * D_MODEL), F32),
            jax.ShapeDtypeStruct((D_MODEL, N_MAIN), BF16),
            jax.ShapeDtypeStruct((D_MODEL, LANES), BF16),
            jax.ShapeDtypeStruct((D_MODEL, GROUP_WIDTH), BF16),
            jax.ShapeDtypeStruct((GROUP_WIDTH, D_MODEL), BF16),
        ],
        compiler_params=pltpu.CompilerParams(dimension_semantics=("arbitrary",), vmem_limit_bytes=VMEM_LIMIT),
        name="modulation",
    )(c_sample, c_prompt, w_ada, b_ada, w_in_t, w_in_t)


def _mod_spec(piece, tm, per_row):
    if per_row:
        return pl.BlockSpec((None, tm, D_MODEL), lambda g, t: (g, t, piece))
    return pl.BlockSpec((None, 1, D_MODEL), lambda g, t: (g, 0, piece))


def _const_spec(shape):
    nd = len(shape)
    return pl.BlockSpec(shape, lambda g, t: (0,) * nd, pipeline_mode=pl.Buffered(1))


PROJ_TILE = 1024


def _proj_body(x_ref, sh_ref, sc_ref, g1_ref, w_ref, wg_ref, lng_ref, lnb_ref, a_ref, b_ref, g_ref):
    n_a = a_ref.shape[1]
    n_copy = n_a - GROUP_WIDTH
    x = x_ref[...]
    h = _rms(x) * g1_ref[...]
    h = (h * (1.0 + sc_ref[...]) + sh_ref[...]).astype(BF16)
    p = _dot(h, w_ref[...])
    a_ref[:, :n_copy] = p[:, :n_copy].astype(a_ref.dtype)
    vs = _gelu_tanh(p[:, n_copy:n_a])
    for hd in range(HEADS):
        sl = slice(hd * HEAD_DIM, (hd + 1) * HEAD_DIM)
        v = vs[:, sl]
        mu = jnp.mean(v, axis=-1, keepdims=True)
        vc = v - mu
        var = jnp.mean(vc * vc, axis=-1, keepdims=True)
        y = vc * lax.rsqrt(var + EPS) * lng_ref[:, sl] + lnb_ref[:, sl]
        a_ref[:, n_copy + hd * HEAD_DIM: n_copy + (hd + 1) * HEAD_DIM] = y.astype(a_ref.dtype)
    b_ref[:, :GROUP_WIDTH] = _sigmoid(p[:, n_a: n_a + GROUP_WIDTH])
    b_ref[:, GROUP_WIDTH:] = _gelu_tanh(p[:, n_a + GROUP_WIDTH: n_a + N_B])
    g_ref[...] = _dot(h, wg_ref[...])
    return h


def _proj_kernel(*refs, k_transposed, n_cast):
    ins, outs = refs[:9 + n_cast], refs[9 + n_cast:]
    wk_ref, k_ref = ins[8], outs[3]
    h = _proj_body(*ins[:8], *outs[:3])
    if k_transposed:
        k_ref[...] = _dot_nt(wk_ref[...], h).astype(k_ref.dtype)
    else:
        k_ref[...] = _dot(h, wk_ref[...]).astype(k_ref.dtype)
    for src, dst in zip(ins[9:], outs[4:]):
        dst[...] = src[...].astype(dst.dtype)


def _project(x, mod, g1, w_main, w_gates, ln_g, ln_b, w_k, cast_weights=(), *, k_transposed, tm, per_row, a_dtype):
    groups, t, _ = x.shape
    n_a = w_main.shape[1] - N_B
    steps_per_group = t // tm
    n_steps = groups * steps_per_group
    in_specs = [
        pl.BlockSpec((None, tm, D_MODEL), lambda g, i: (g, i, 0)),
        _mod_spec(0, tm, per_row),
        _mod_spec(1, tm, per_row),
        _const_spec((1, D_MODEL)),
        _const_spec(w_main.shape),
        _const_spec(w_gates.shape),
        _const_spec((1, GROUP_WIDTH)),
        _const_spec((1, GROUP_WIDTH)),
    ]
    out_specs = [
        pl.BlockSpec((None, tm, n_a), lambda g, i: (g, i, 0)),
        pl.BlockSpec((None, tm, N_B), lambda g, i: (g, i, 0)),
        pl.BlockSpec((None, tm, LANES), lambda g, i: (g, i, 0)),
    ]
    out_shape = [
        jax.ShapeDtypeStruct((groups, t, n_a), a_dtype),
        jax.ShapeDtypeStruct((groups, t, N_B), F32),
        jax.ShapeDtypeStruct((groups, t, LANES), F32),
    ]
    args = [x, mod, mod, g1, w_main, w_gates, ln_g, ln_b, w_k]
    in_specs.append(_const_spec(w_k.shape))
    if k_transposed:
        out_specs.append(pl.BlockSpec((None, GROUP_WIDTH, tm), lambda g, i: (g, 0, i)))
        out_shape.append(jax.ShapeDtypeStruct((groups, GROUP_WIDTH, t), a_dtype))
    else:
        out_specs.append(pl.BlockSpec((None, tm, GROUP_WIDTH), lambda g, i: (g, i, 0)))
        out_shape.append(jax.ShapeDtypeStruct((groups, t, GROUP_WIDTH), a_dtype))
    for w in cast_weights:
        rows, cols = w.shape
        assert rows % (n_steps * 16) == 0, "row block must be a whole number of bf16 sublane tiles"
        spec = pl.BlockSpec((rows // n_steps, cols), lambda g, i: (g * steps_per_group + i, 0))
        in_specs.append(spec)
        out_specs.append(spec)
        out_shape.append(jax.ShapeDtypeStruct((rows, cols), BF16))
        args.append(w)
    return pl.pallas_call(
        functools.partial(_proj_kernel, k_transposed=k_transposed, n_cast=len(cast_weights)),
        grid=(groups, t // tm),
        in_specs=in_specs,
        out_specs=out_specs,
        out_shape=out_shape,
        compiler_params=pltpu.CompilerParams(
            dimension_semantics=("arbitrary", "arbitrary"), vmem_limit_bytes=VMEM_LIMIT),
        name="project",
    )(*args)


MIX_TILE = 512


def _dot3_rhs(lhs_bf16, rhs_f32):
    hi, mid, lo = _split3_bf16(rhs_f32)
    return _dot(lhs_bf16, hi) + _dot(lhs_bf16, mid) + _dot(lhs_bf16, lo)


def _dot3_lhs(lhs_f32, rhs_bf16):
    hi, mid, lo = _split3_bf16(lhs_f32)
    return _dot(hi, rhs_bf16) + _dot(mid, rhs_bf16) + _dot(lo, rhs_bf16)


FF_CHUNKS = (768, 768, 768, 512)
MIXER_LOOKAHEAD = 2
OUT_ROW_BLOCKS = 2


def _gate_rows(g_ref, bias_ref, tril, triu, n_chunks):
    pre = [g_ref[c * CHUNK:(c + 1) * CHUNK, :] + bias_ref[...] for c in range(n_chunks)]
    bc_all = _dot3_rhs(tril, jnp.concatenate([_log_sigmoid(p) for p in pre], axis=1))
    rows_all = jnp.concatenate([p.T[0:2 * HEADS, :] for p in pre], axis=0)
    b_rows_all = _dot3_lhs(_log_sigmoid(rows_all), triu)
    bc, arow, blast = [], [], []
    for c in range(n_chunks):
        bc.append(bc_all[:, c * LANES:(c + 1) * LANES])
        rows = rows_all[c * 2 * HEADS:(c + 1) * 2 * HEADS, :]
        b_rows = b_rows_all[c * 2 * HEADS:(c + 1) * 2 * HEADS, :]
        arow.append([rows[hd:hd + 1, :] - b_rows[HEADS + hd:HEADS + hd + 1, :] for hd in range(HEADS)])
        blast.append([jnp.min(b_rows[HEADS + hd:HEADS + hd + 1, :], axis=1, keepdims=True) for hd in range(HEADS)])
    return bc, arow, blast


def _mlstm_local(q, kt, a_row, b_last, causal):
    amat = jnp.where(causal, a_row, -jnp.inf)
    m_row = jnp.max(amat, axis=1, keepdims=True)
    s_loc = (_dot(q, kt) * jnp.exp(amat - m_row)).astype(BF16)
    g_row = b_last + a_row
    g_loc = jnp.max(g_row, axis=1, keepdims=True)
    kw = (kt.astype(F32) * jnp.exp(g_row - g_loc)).astype(BF16)
    return m_row, s_loc, g_loc, kw


def _mlstm_readout(local, q, v, og, g_head, b_last, b_rep, cta, m_prev, ones_blk):
    m_row, s_loc, g_loc, kw = local
    va = jnp.concatenate([v, ones_blk], axis=1)
    nd_loc = _dot(s_loc, va)
    u_aug = _dot(kw, va)
    inter = _dot(q, cta.astype(BF16))
    mm = jnp.maximum(m_prev, m_row)
    f_loc = jnp.exp(m_row - mm) * QK_SCALE
    f_int = jnp.exp(m_prev - mm) * QK_SCALE
    nd = f_loc * nd_loc + f_int * inter
    clamp = jnp.exp(-(b_rep + mm))
    hh = nd[:, :HEAD_DIM] / jnp.maximum(jnp.abs(nd[:, HEAD_DIM:]), clamp)
    ml = _rms(hh) * g_head * og

    dec = b_last + m_prev
    m_new = jnp.maximum(dec, g_loc)
    cta_new = jnp.exp(dec - m_new) * cta + jnp.exp(g_loc - m_new) * u_aug
    return ml, cta_new, m_new


def _prompt_tail_kernel(a_ref, kt_ref, b_ref, g_ref, x_ref, gt1_ref, sh2_ref, sc2_ref, gt2_ref,
                        bias_ref, gh_ref, ws_ref, bs_ref, wo_ref, g2_ref, wgu_ref, wdn_ref, gf_ref,
                        y_ref, c_out, n_out, m_out,
                        mix_ref, cta_ref, m_ref, *, tiles_per_seq):
    step = pl.program_id(0)
    n_tiles = pl.num_programs(0) - 1
    refs = (a_ref, kt_ref, b_ref, g_ref, x_ref, gt1_ref, sh2_ref, sc2_ref, gt2_ref,
            bias_ref, gh_ref, ws_ref, bs_ref, wo_ref, g2_ref, wgu_ref, wdn_ref, gf_ref,
            y_ref, mix_ref, cta_ref, m_ref)

    @pl.when(step % tiles_per_seq == 0)
    def _():
        cta_ref[...] = jnp.zeros_like(cta_ref)
        m_ref[...] = jnp.zeros_like(m_ref)

    @pl.when(step == 0)
    def _():
        _prompt_tail_body(refs, with_mixer=True, with_out=False)

    @pl.when(jnp.logical_and(step > 0, step < n_tiles))
    def _():
        _prompt_tail_body(refs, with_mixer=True, with_out=True)

    @pl.when(step == n_tiles)
    def _():
        _prompt_tail_body(refs, with_mixer=False, with_out=True)

    @pl.when(jnp.logical_and(step % tiles_per_seq == tiles_per_seq - 1, step < n_tiles))
    def _():
        for hd in range(HEADS):
            cta = cta_ref[hd]
            c_out[hd] = cta[:, :HEAD_DIM].T
            n_out[hd: hd + 1, :] = cta[:, HEAD_DIM:].T[0:1, :]
            m_out[hd: hd + 1, :] = m_ref[hd][0:1, :]


def _prompt_tail_body(refs, *, with_mixer, with_out):
    (a_ref, kt_ref, b_ref, g_ref, x_ref, gt1_ref, sh2_ref, sc2_ref, gt2_ref,
     bias_ref, gh_ref, ws_ref, bs_ref, wo_ref, g2_ref, wgu_ref, wdn_ref, gf_ref,
     y_ref, mix_ref, cta_ref, m_ref) = refs
    tm = a_ref.shape[0]
    n_chunks = tm // CHUNK
    n_copy = a_ref.shape[1] - GROUP_WIDTH

    row = lax.broadcasted_iota(jnp.int32, (CHUNK, CHUNK), 0)
    col = lax.broadcasted_iota(jnp.int32, (CHUNK, CHUNK), 1)
    causal = row >= col
    tril = jnp.where(causal, 1.0, 0.0).astype(BF16)
    triu = jnp.where(row <= col, 1.0, 0.0).astype(BF16)
    ones_blk = jnp.ones((CHUNK, HEAD_DIM), BF16)

    blocks = [slice(r0, r0 + tm // OUT_ROW_BLOCKS) for r0 in range(0, tm, tm // OUT_ROW_BLOCKS)] if with_out else []
    outproj = [_dot(mix_ref[rs, :], wo_ref[...]) for rs in blocks]
    x1, h2 = [], []
    for rs, op in zip(blocks, outproj):
        x1.append(x_ref[rs, :] + gt1_ref[...] * op)
        h = _rms(x1[-1]) * g2_ref[...]
        h2.append((h * (1.0 + sc2_ref[...]) + sh2_ref[...]).astype(BF16))

    if with_mixer:
        bc, arow, blast = _gate_rows(g_ref, bias_ref, tril, triu, n_chunks)
        state = [(cta_ref[hd], jnp.max(m_ref[hd][0:1, :], axis=1, keepdims=True)) for hd in range(HEADS)]
    sg_all = [None] * HEADS

    def mixer_local(c, hd):
        rs = slice(c * CHUNK, (c + 1) * CHUNK)
        sl = slice(hd * HEAD_DIM, (hd + 1) * HEAD_DIM)
        return _mlstm_local(a_ref[rs, sl], kt_ref[sl, rs], arow[c][hd], blast[c][hd], causal)

    def mixer_readout(c, hd, local):
        rs = slice(c * CHUNK, (c + 1) * CHUNK)
        sl = slice(hd * HEAD_DIM, (hd + 1) * HEAD_DIM)
        if c == 0:
            vsn_all = jnp.concatenate(
                [a_ref[cc * CHUNK:(cc + 1) * CHUNK, n_copy + hd * HEAD_DIM: n_copy + (hd + 1) * HEAD_DIM]
                 for cc in range(n_chunks)], axis=1)
            sg_all[hd] = _dot(ws_ref[hd], vsn_all)
        cta, m_prev = state[hd]
        b_rep = jnp.broadcast_to(bc[c][:, HEADS + hd: HEADS + hd + 1], (CHUNK, HEAD_DIM))
        ml, cta, m_prev = _mlstm_readout(
            local, a_ref[rs, sl], a_ref[rs, GROUP_WIDTH + hd * HEAD_DIM: GROUP_WIDTH + (hd + 1) * HEAD_DIM],
            b_ref[rs, sl], gh_ref[:, sl], blast[c][hd], b_rep, cta, m_prev, ones_blk)
        state[hd] = (cta, m_prev)
        mix_ref[rs, sl] = ml.astype(mix_ref.dtype)
        sg = sg_all[hd][:, c * CHUNK:(c + 1) * CHUNK] + bs_ref[hd]
        ug = b_ref[rs, GROUP_WIDTH + hd * HEAD_DIM: GROUP_WIDTH + (hd + 1) * HEAD_DIM]
        mix_ref[rs, GROUP_WIDTH + hd * HEAD_DIM: GROUP_WIDTH + (hd + 1) * HEAD_DIM] = (ug * sg).astype(mix_ref.dtype)

    pieces = [(c, hd) for c in range(n_chunks) for hd in range(HEADS)] if with_mixer else []
    per_ff = -(-len(pieces) // len(FF_CHUNKS))
    ahead = [mixer_local(*pieces[p]) for p in range(min(MIXER_LOOKAHEAD, len(pieces)))]
    acc = [None] * len(blocks)
    f0 = 0
    for j, width in enumerate(FF_CHUNKS):
        gate_up = [(_dot(h, wgu_ref[:, f0:f0 + width]), _dot(h, wgu_ref[:, D_FF + f0:D_FF + f0 + width]))
                   for h in h2]
        for idx in range(j * per_ff, min((j + 1) * per_ff, len(pieces))):
            if idx + MIXER_LOOKAHEAD < len(pieces):
                ahead.append(mixer_local(*pieces[idx + MIXER_LOOKAHEAD]))
            mixer_readout(*pieces[idx], ahead.pop(0))
        for r, (gate, up) in enumerate(gate_up):
            act = (gate * _sigmoid(gate) * up).astype(BF16)
            part = _dot(act, wdn_ref[f0:f0 + width, :])
            acc[r] = part if acc[r] is None else acc[r] + part
            if j + 1 == len(FF_CHUNKS):
                x2 = x1[r] + gt2_ref[...] * acc[r]
                y_ref[blocks[r], :] = _rms(x2) * gf_ref[...]
        f0 += width

    if with_mixer:
        for hd in range(HEADS):
            cta_ref[hd] = state[hd][0]
            m_ref[hd] = jnp.broadcast_to(state[hd][1], m_ref.shape[1:])


def _prompt_tail(a, kt, b, gates, x, mod, bias_row, g_head, ws_tril, bs_rep, w_out, g2, w_gu, w_dn, g_final):
    groups, t, n_a = a.shape
    tm = MIX_TILE
    tps = t // tm
    n_tiles = groups * tps
    assert sum(FF_CHUNKS) == D_FF

    def cur(i):
        return jnp.minimum(i, n_tiles - 1)

    def prev(i):
        return jnp.maximum(i - 1, 0)

    def rows(tile, width):
        return pl.BlockSpec((None, tm, width), lambda i: (tile(i) // tps, tile(i) % tps, 0))

    def mod_piece(piece):
        return pl.BlockSpec((None, 1, D_MODEL), lambda i: (prev(i) // tps, 0, piece))

    def const(shape):
        nd = len(shape)
        return pl.BlockSpec(shape, lambda i: (0,) * nd, pipeline_mode=pl.Buffered(1))

    def per_seq(shape):
        nd = len(shape)
        return pl.BlockSpec((None,) + shape, lambda i: (cur(i) // tps,) + (0,) * nd)

    return pl.pallas_call(
        functools.partial(_prompt_tail_kernel, tiles_per_seq=tps),
        grid=(n_tiles + 1,),
        in_specs=[
            rows(cur, n_a),
            pl.BlockSpec((None, GROUP_WIDTH, tm), lambda i: (cur(i) // tps, 0, cur(i) % tps)),
            rows(cur, N_B),
            rows(cur, LANES),
            rows(prev, D_MODEL),
            mod_piece(2), mod_piece(3), mod_piece(4), mod_piece(5),
            const((1, LANES)),
            const((1, GROUP_WIDTH)),
            const((HEADS, CHUNK, CHUNK)),
            const((HEADS, CHUNK, LANES)),
            const((D_MODEL, D_MODEL)),
            const((1, D_MODEL)),
            const((D_MODEL, 2 * D_FF)),
            const((D_FF, D_MODEL)),
            const((1, D_MODEL)),
        ],
        out_specs=[
            rows(prev, D_MODEL),
            per_seq((HEADS, HEAD_DIM, HEAD_DIM)),
            per_seq((HEADS, HEAD_DIM)),
            per_seq((HEADS, LANES)),
        ],
        out_shape=[
            jax.ShapeDtypeStruct((groups, t, D_MODEL), F32),
            jax.ShapeDtypeStruct((groups, HEADS, HEAD_DIM, HEAD_DIM), F32),
            jax.ShapeDtypeStruct((groups, HEADS, HEAD_DIM), F32),
            jax.ShapeDtypeStruct((groups, HEADS, LANES), F32),
        ],
        scratch_shapes=[
            pltpu.VMEM((tm, D_MODEL), BF16),
            pltpu.VMEM((HEADS, HEAD_DIM, 2 * HEAD_DIM), F32),
            pltpu.VMEM((HEADS, 8, LANES), F32),
        ],
        compiler_params=pltpu.CompilerParams(
            dimension_semantics=("arbitrary",), vmem_limit_bytes=VMEM_LIMIT),
        name="prompt_tail",
    )(a, kt, b, gates, x, mod, mod, mod, mod, bias_row, g_head, ws_tril, bs_rep, w_out, g2, w_gu, w_dn, g_final)


SAMPLE_TOKENS_PER_STEP = 64
SAMPLE_UNROLL = 8


def _mix_sample_kernel(q_ref, k_ref, v_ref, vsn_ref, og_ref, ug_ref, g_ref, m0_ref, n0_ref, c_ref,
                       bias_ref, gh_ref, ws0_ref, bs0_ref,
                       mix_ref, c_out, n_out, m_out,
                       vt_ref, kp_ref, wd_ref, cqt_ref):
    hd = pl.program_id(0)
    grp = pl.program_id(1)
    nb = q_ref.shape[0]
    lane = lax.broadcasted_iota(jnp.int32, (nb, LANES), 1)

    def gate_terms():
        pre = g_ref[...] + bias_ref[...]
        i_pre = jnp.sum(jnp.where(lane == hd, pre, 0.0), axis=1, keepdims=True)
        f_pre = jnp.sum(jnp.where(lane == hd + HEADS, pre, 0.0), axis=1, keepdims=True)
        m_prev = jnp.sum(jnp.where(lane == hd, m0_ref[...], 0.0), axis=1, keepdims=True)
        inter = _log_sigmoid(f_pre) + m_prev
        m_t = jnp.maximum(inter, i_pre)
        return m_t, jnp.exp(i_pre - m_t), jnp.exp(inter - m_t)

    @pl.when(grp == 0)
    def _():
        _, w_in, w_dec = gate_terms()
        vt_ref[...] = v_ref[...].T
        kp_ref[...] = (w_in * k_ref[...]).astype(BF16)
        wd_ref[...] = jnp.broadcast_to(w_dec, wd_ref.shape)
        cqt_ref[...] = jnp.zeros_like(cqt_ref)

    lane_sq = lax.broadcasted_iota(jnp.int32, (HEAD_DIM, LANES), 1)
    tokens = c_ref.shape[0]

    def body(r, cqt):
        tok = grp * tokens + r
        q_row = q_ref[pl.ds(tok, 1), :]
        c_b = c_ref[r]
        sel = lane_sq == tok
        cq_col = jnp.sum(c_b * q_row, axis=1, keepdims=True)
        v_sel = jnp.where(sel, vt_ref[...], 0.0).astype(BF16)
        outer = _dot(v_sel, kp_ref[...])
        c_out[r] = wd_ref[pl.ds(tok, 1), :] * c_b + outer
        return jnp.where(sel, cq_col, cqt)

    cqt_ref[...] = lax.fori_loop(0, tokens, body, cqt_ref[...], unroll=SAMPLE_UNROLL)

    @pl.when(grp == pl.num_programs(1) - 1)
    def _():
        m_t, w_in, w_dec = gate_terms()
        q = q_ref[...]
        k = k_ref[...]
        v = v_ref[...]
        n0 = n0_ref[...]
        cq = cqt_ref[...].T
        s = jnp.sum(q * k, axis=1, keepdims=True) * (QK_SCALE * w_in)
        w_inter = w_dec * QK_SCALE
        num = s * v + w_inter * cq
        den = s + w_inter * jnp.sum(n0 * q, axis=1, keepdims=True)
        hh = num / jnp.maximum(jnp.abs(den), jnp.exp(-m_t))
        ml = _rms(hh) * gh_ref[...] * og_ref[...]
        cm = ug_ref[...] * (ws0_ref[...] * vsn_ref[...] + bs0_ref[...])
        n_out[...] = w_dec * n0 + w_in * k
        m_out[...] = jnp.broadcast_to(m_t, m_out.shape)
        for hh_static in range(HEADS):
            @pl.when(hd == hh_static)
            def _():
                mix_ref[:, hh_static * HEAD_DIM: (hh_static + 1) * HEAD_DIM] = ml
                mix_ref[:, GROUP_WIDTH + hh_static * HEAD_DIM: GROUP_WIDTH + (hh_static + 1) * HEAD_DIM] = cm


def _mix_sample(a, k, b, gates, m0_pad, n0, c0, bias_row, g_head, ws0_row, bs0_row):
    nb = a.shape[0]
    tb = SAMPLE_TOKENS_PER_STEP

    def head_block(offset):
        return pl.BlockSpec((nb, HEAD_DIM), lambda h, g: (0, offset + h))

    full = pl.BlockSpec((nb, LANES), lambda h, g: (0, 0))
    head_row = pl.BlockSpec((1, HEAD_DIM), lambda h, g: (0, h))
    c_spec = pl.BlockSpec((tb, None, HEAD_DIM, HEAD_DIM), lambda h, g: (g, h, 0, 0))
    return pl.pallas_call(
        _mix_sample_kernel,
        grid=(HEADS, nb // tb),
        in_specs=[
            head_block(0), head_block(0), head_block(HEADS), head_block(2 * HEADS),
            head_block(0), head_block(HEADS),
            full, full, head_block(0), c_spec,
            pl.BlockSpec((1, LANES), lambda h, g: (0, 0)),
            head_row, head_row, head_row,
        ],
        out_specs=[
            pl.BlockSpec((nb, D_MODEL), lambda h, g: (0, 0)),
            c_spec,
            head_block(0),
            head_block(0),
        ],
        out_shape=[
            jax.ShapeDtypeStruct((nb, D_MODEL), F32),
            jax.ShapeDtypeStruct(c0.shape, F32),
            jax.ShapeDtypeStruct((nb, GROUP_WIDTH), F32),
            jax.ShapeDtypeStruct((nb, GROUP_WIDTH), F32),
        ],
        scratch_shapes=[
            pltpu.VMEM((HEAD_DIM, nb), F32),
            pltpu.VMEM((nb, HEAD_DIM), BF16),
            pltpu.VMEM((nb, LANES), F32),
            pltpu.VMEM((HEAD_DIM, nb), F32),
        ],
        compiler_params=pltpu.CompilerParams(
            dimension_semantics=("arbitrary", "arbitrary"), vmem_limit_bytes=VMEM_LIMIT),
        name="mix_sample",
    )(a, k, a, a, b, b, gates, m0_pad, n0, c0, bias_row, g_head, ws0_row, bs0_row)


def _out_kernel(x_ref, mix_ref, gt1_ref, sh2_ref, sc2_ref, gt2_ref, wo_ref, g2_ref, wgu_ref, wdn_ref, gf_ref,
                y_ref):
    x1 = x_ref[...] + gt1_ref[...] * _dot(mix_ref[...].astype(BF16), wo_ref[...])
    h2 = _rms(x1) * g2_ref[...]
    h2 = (h2 * (1.0 + sc2_ref[...]) + sh2_ref[...]).astype(BF16)
    gate = _dot(h2, wgu_ref[:, :D_FF])
    up = _dot(h2, wgu_ref[:, D_FF:])
    act = (gate * _sigmoid(gate) * up).astype(BF16)
    x2 = x1 + gt2_ref[...] * _dot(act, wdn_ref[...])
    y_ref[:, 0, :] = _rms(x2) * gf_ref[...]


def _output_stage(x, mix, mod, w_out, g2, w_gu, w_dn, g_final, *, tm, per_row):
    groups, t, _ = x.shape
    return pl.pallas_call(
        _out_kernel,
        grid=(groups, t // tm),
        in_specs=[
            pl.BlockSpec((None, tm, D_MODEL), lambda g, i: (g, i, 0)),
            pl.BlockSpec((None, tm, D_MODEL), lambda g, i: (g, i, 0)),
            _mod_spec(2, tm, per_row),
            _mod_spec(3, tm, per_row),
            _mod_spec(4, tm, per_row),
            _mod_spec(5, tm, per_row),
            _const_spec((D_MODEL, D_MODEL)),
            _const_spec((1, D_MODEL)),
            _const_spec((D_MODEL, 2 * D_FF)),
            _const_spec((D_FF, D_MODEL)),
            _const_spec((1, D_MODEL)),
        ],
        out_specs=pl.BlockSpec((tm, 1, D_MODEL), lambda g, i: (g * (t // tm) + i, 0, 0)),
        out_shape=jax.ShapeDtypeStruct((groups * t, 1, D_MODEL), F32),
        compiler_params=pltpu.CompilerParams(
            dimension_semantics=("arbitrary", "arbitrary"), vmem_limit_bytes=VMEM_LIMIT),
        name="output_stage",
    )(x, mix, mod, mod, mod, mod, w_out, g2, w_gu, w_dn, g_final)


def kernel(x_prompt, x_sample, c_prompt, c_sample, state_mlstm_C, state_mlstm_n, state_mlstm_m, w_ada, b_ada, g_norm1, w_in, b_gate, g_mlstm_head, ln_v_g, ln_v_b, w_s, b_s, w_out, g_norm2, w_gate_up, w_down, g_final):
    depth = w_ada.shape[0]
    assert depth == 1, "single-layer trunk"
    batch, seq, _ = x_prompt.shape
    nb = x_sample.shape[0]
    assert x_sample.shape[1] == 1

    gw = GROUP_WIDTH
    g1 = g_norm1[0].reshape(1, D_MODEL)
    g2 = g_norm2[0].reshape(1, D_MODEL)
    gf = g_final.reshape(1, D_MODEL)
    ln_g = ln_v_g[0].reshape(1, gw)
    ln_b = ln_v_b[0].reshape(1, gw)
    g_head = g_mlstm_head[0].reshape(1, gw)
    bias_row = jnp.pad(b_gate[0], (0, LANES - 2 * HEADS)).reshape(1, LANES)
    tril = jnp.tril(jnp.ones((CHUNK, CHUNK), dtype=bool))
    ws_tril = jnp.where(tril[None], w_s[0], 0.0).astype(BF16)
    bs_rep = jnp.broadcast_to(b_s[0][:, :, None], (HEADS, CHUNK, LANES))
    ws0_row = jnp.repeat(w_s[0][:, 0, 0], HEAD_DIM).reshape(1, gw)
    bs0_row = jnp.repeat(b_s[0][:, 0], HEAD_DIM).reshape(1, gw)

    mod_s, mod_p, w_main, w_gates, w_k, w_kt = _modulation(c_sample, c_prompt, w_ada[0], b_ada[0].reshape(1, -1), w_in[0].T)
    mod_s = mod_s.reshape(1, nb, N_MOD * D_MODEL)

    a_p, b_p, g_p, kt_p, w_out_b, w_gu_b, w_dn_b = _project(
        x_prompt, mod_p, g1, w_main, w_gates, ln_g, ln_b, w_kt, (w_out[0], w_gate_up[0], w_down[0]),
        k_transposed=True, tm=PROJ_TILE, per_row=False, a_dtype=BF16)
    y_p, c_p, n_p, m_p = _prompt_tail(a_p, kt_p, b_p, g_p, x_prompt, mod_p, bias_row, g_head, ws_tril, bs_rep,
                                      w_out_b, g2, w_gu_b, w_dn_b, gf)

    xs = x_sample.reshape(1, nb, D_MODEL)
    a_s, b_s_act, g_s, k_s = _project(xs, mod_s, g1, w_main, w_gates, ln_g, ln_b, w_k, k_transposed=False,
                                      tm=nb, per_row=True, a_dtype=F32)
    a_s2, b_s2, g_s2 = a_s[0], b_s_act[0], g_s[0]
    m0_pad = jnp.pad(state_mlstm_m[0], ((0, 0), (0, LANES - HEADS)))
    n0 = state_mlstm_n[0].reshape(nb, gw)
    mix_s, c_s, n_s, m_s = _mix_sample(a_s2, k_s[0], b_s2, g_s2, m0_pad, n0, state_mlstm_C[0],
                                       bias_row, g_head, ws0_row, bs0_row)
    y_s = _output_stage(xs, mix_s.reshape(1, nb, D_MODEL), mod_s, w_out_b, g2, w_gu_b, w_dn_b, gf,
                        tm=nb, per_row=True)

    return (
        y_p,
        y_s,
        c_p[None],
        n_p[None],
        m_p[:, :, 0][None],
        c_s[None],
        n_s.reshape(nb, HEADS, HEAD_DIM)[None],
        m_s.reshape(nb, HEADS, HEAD_DIM)[:, :, 0][None],
        a_s2[:, 2 * gw:].reshape(nb, 1, HEADS, HEAD_DIM)[None],
    )
```

```python
import functools
import math

import jax
import jax.numpy as jnp
from jax import lax
from jax.experimental import pallas as pl
from jax.experimental.pallas import tpu as pltpu

F32 = jnp.float32
BF16 = jnp.bfloat16

D_MODEL = 1024
HEADS = 4
HEAD_DIM = 128
GROUP_WIDTH = HEADS * HEAD_DIM
CHUNK = 128
D_FF = 2816
N_MOD = 6
EPS = 1e-6
QK_SCALE = HEAD_DIM ** -0.5
LANES = 128
BF16_SUBLANES = 16

N_B = 2 * GROUP_WIDTH

VMEM_LIMIT = 56 * 1024 * 1024


def _dot(a, b):
    return jnp.dot(a, b, preferred_element_type=F32)


def _dot_nt(a, b):
    return lax.dot_general(a, b, (((1,), (1,)), ((), ())), preferred_element_type=F32)


def _sigmoid(x):
    return 1.0 / (1.0 + jnp.exp(-x))


def _gelu_tanh(x):
    c = math.sqrt(2.0 / math.pi)
    return x * (0.5 * (1.0 + jnp.tanh(c * (x + 0.044715 * (x * x * x)))))


def _log_sigmoid(x):
    return jnp.minimum(x, 0.0) - jnp.log1p(jnp.exp(-jnp.abs(x)))


def _rms(x):
    return x * lax.rsqrt(jnp.mean(x * x, axis=-1, keepdims=True) + EPS)


def _split3_bf16(x):
    hi = x.astype(BF16)
    r1 = x - hi.astype(F32)
    mid = r1.astype(BF16)
    lo = (r1 - mid.astype(F32)).astype(BF16)
    return hi, mid, lo


MOD_TILE = 1024

IN_Q, IN_K, IN_V, IN_O, IN_GATES = 0, GROUP_WIDTH, 2 * GROUP_WIDTH, 3 * GROUP_WIDTH, 4 * GROUP_WIDTH
IN_U = IN_GATES + 2 * HEADS
IN_VS = IN_U + GROUP_WIDTH
N_IN = IN_VS + GROUP_WIDTH
MAIN_GROUPS = (IN_Q, IN_V, IN_VS, IN_O, IN_U)
N_MAIN = len(MAIN_GROUPS) * GROUP_WIDTH


def _mod_kernel(cs_ref, cp_ref, w_ref, b_ref, grp_ref, gate_ref, os_ref, op_ref, wmain_ref, wgate_ref, wk_ref, wkt_ref):
    step = pl.program_id(0)
    ns, npr = cs_ref.shape[0], cp_ref.shape[0]
    pad = -(ns + npr) % BF16_SUBLANES
    c = jnp.concatenate([cs_ref[...], cp_ref[...], jnp.zeros((pad, D_MODEL), F32)], axis=0)
    a = (c * _sigmoid(c)).astype(BF16)
    res = _dot(a, w_ref[...].astype(BF16)) + b_ref[...]
    os_ref[...] = res[:ns]
    op_ref[:, 0, :] = res[ns:ns + npr]

    def transposed_into(dst_ref):
        for blk in range(GROUP_WIDTH // LANES):
            rows = grp_ref[blk * LANES:(blk + 1) * LANES, :]
            dst_ref[:, blk * LANES:(blk + 1) * LANES] = rows.T.astype(BF16)

    @pl.when(step < len(MAIN_GROUPS))
    def _():
        transposed_into(wmain_ref)

    @pl.when(step == len(MAIN_GROUPS))
    def _():
        wkt_ref[...] = grp_ref[...].astype(BF16)
        transposed_into(wk_ref)

    @pl.when(step == 0)
    def _():
        gate_rows = jnp.concatenate([gate_ref[...], jnp.zeros((LANES - 2 * HEADS, D_MODEL), F32)], axis=0)
        wgate_ref[...] = gate_rows.T.astype(BF16)


def _modulation(c_sample, c_prompt, w_ada, b_ada, w_in_t):
    ns, npr = c_sample.shape[0], c_prompt.shape[0]
    tn = MOD_TILE
    steps = N_MOD * D_MODEL // tn
    group_starts = MAIN_GROUPS + (IN_K,)
    assert w_in_t.shape == (N_IN, D_MODEL) and steps == len(group_starts)

    def whole(shape):
        return pl.BlockSpec(shape, lambda j: (0, 0))

    def group_start(j):
        assert all(start % 8 == 0 for start in group_starts)
        return pl.multiple_of(sum(jnp.where(j == i, start, 0) for i, start in enumerate(group_starts)), 8)

    return pl.pallas_call(
        _mod_kernel,
        grid=(steps,),
        in_specs=[
            whole((ns, D_MODEL)),
            whole((npr, D_MODEL)),
            pl.BlockSpec((D_MODEL, tn), lambda j: (0, j)),
            pl.BlockSpec((1, tn), lambda j: (0, j)),
            pl.BlockSpec((pl.Element(GROUP_WIDTH), pl.Element(D_MODEL)), lambda j: (group_start(j), 0)),
            pl.BlockSpec((pl.Element(2 * HEADS), pl.Element(D_MODEL)), lambda j: (IN_GATES, 0)),
        ],
        out_specs=[
            pl.BlockSpec((ns, tn), lambda j: (0, j)),
            pl.BlockSpec((npr, 1, tn), lambda j: (0, 0, j)),
            pl.BlockSpec((D_MODEL, GROUP_WIDTH), lambda j: (0, jnp.minimum(j, len(MAIN_GROUPS) - 1))),
            whole((D_MODEL, LANES)),
            whole((D_MODEL, GROUP_WIDTH)),
            whole((GROUP_WIDTH, D_MODEL)),
        ],
        out_shape=[
            jax.ShapeDtypeStruct((ns, N_MOD * D_MODEL), F32),
            jax.ShapeDtypeStruct((npr, 1, N_MOD * D_MODEL), F32),
            jax.ShapeDtypeStruct((D_MODEL, N_MAIN), BF16),
            jax.ShapeDtypeStruct((D_MODEL, LANES), BF16),
            jax.ShapeDtypeStruct((D_MODEL, GROUP_WIDTH), BF16),
            jax.ShapeDtypeStruct((GROUP_WIDTH, D_MODEL), BF16),
        ],
        compiler_params=pltpu.CompilerParams(dimension_semantics=("arbitrary",), vmem_limit_bytes=VMEM_LIMIT),
        name="modulation",
    )(c_sample, c_prompt, w_ada, b_ada, w_in_t, w_in_t)


def _mod_spec(piece, tm, per_row):
    if per_row:
        return pl.BlockSpec((None, tm, D_MODEL), lambda g, t: (g, t, piece))
    return pl.BlockSpec((None, 1, D_MODEL), lambda g, t: (g, 0, piece))


def _const_spec(shape):
    nd = len(shape)
    return pl.BlockSpec(shape, lambda g, t: (0,) * nd, pipeline_mode=pl.Buffered(1))


PROJ_TILE = 1024


def _proj_body(x_ref, sh_ref, sc_ref, g1_ref, w_ref, wg_ref, lng_ref, lnb_ref, a_ref, b_ref, g_ref):
    n_a = a_ref.shape[1]
    n_copy = n_a - GROUP_WIDTH
    x = x_ref[...]
    h = _rms(x) * g1_ref[...]
    h = (h * (1.0 + sc_ref[...]) + sh_ref[...]).astype(BF16)
    p = _dot(h, w_ref[...])
    a_ref[:, :n_copy] = p[:, :n_copy].astype(a_ref.dtype)
    vs = _gelu_tanh(p[:, n_copy:n_a])
    for hd in range(HEADS):
        sl = slice(hd * HEAD_DIM, (hd + 1) * HEAD_DIM)
        v = vs[:, sl]
        mu = jnp.mean(v, axis=-1, keepdims=True)
        vc = v - mu
        var = jnp.mean(vc * vc, axis=-1, keepdims=True)
        y = vc * lax.rsqrt(var + EPS) * lng_ref[:, sl] + lnb_ref[:, sl]
        a_ref[:, n_copy + hd * HEAD_DIM: n_copy + (hd + 1) * HEAD_DIM] = y.astype(a_ref.dtype)
    b_ref[:, :GROUP_WIDTH] = _sigmoid(p[:, n_a: n_a + GROUP_WIDTH])
    b_ref[:, GROUP_WIDTH:] = _gelu_tanh(p[:, n_a + GROUP_WIDTH: n_a + N_B])
    g_ref[...] = _dot(h, wg_ref[...])
    return h


def _proj_kernel(*refs, k_transposed, n_cast):
    ins, outs = refs[:9 + n_cast], refs[9 + n_cast:]
    wk_ref, k_ref = ins[8], outs[3]
    h = _proj_body(*ins[:8], *outs[:3])
    if k_transposed:
        k_ref[...] = _dot_nt(wk_ref[...], h).astype(k_ref.dtype)
    else:
        k_ref[...] = _dot(h, wk_ref[...]).astype(k_ref.dtype)
    for src, dst in zip(ins[9:], outs[4:]):
        dst[...] = src[...].astype(dst.dtype)


def _project(x, mod, g1, w_main, w_gates, ln_g, ln_b, w_k, cast_weights=(), *, k_transposed, tm, per_row, a_dtype):
    groups, t, _ = x.shape
    n_a = w_main.shape[1] - N_B
    steps_per_group = t // tm
    n_steps = groups * steps_per_group
    in_specs = [
        pl.BlockSpec((None, tm, D_MODEL), lambda g, i: (g, i, 0)),
        _mod_spec(0, tm, per_row),
        _mod_spec(1, tm, per_row),
        _const_spec((1, D_MODEL)),
        _const_spec(w_main.shape),
        _const_spec(w_gates.shape),
        _const_spec((1, GROUP_WIDTH)),
        _const_spec((1, GROUP_WIDTH)),
    ]
    out_specs = [
        pl.BlockSpec((None, tm, n_a), lambda g, i: (g, i, 0)),
        pl.BlockSpec((None, tm, N_B), lambda g, i: (g, i, 0)),
        pl.BlockSpec((None, tm, LANES), lambda g, i: (g, i, 0)),
    ]
    out_shape = [
        jax.ShapeDtypeStruct((groups, t, n_a), a_dtype),
        jax.ShapeDtypeStruct((groups, t, N_B), F32),
        jax.ShapeDtypeStruct((groups, t, LANES), F32),
    ]
    args = [x, mod, mod, g1, w_main, w_gates, ln_g, ln_b, w_k]
    in_specs.append(_const_spec(w_k.shape))
    if k_transposed:
        out_specs.append(pl.BlockSpec((None, GROUP_WIDTH, tm), lambda g, i: (g, 0, i)))
        out_shape.append(jax.ShapeDtypeStruct((groups, GROUP_WIDTH, t), a_dtype))
    else:
        out_specs.append(pl.BlockSpec((None, tm, GROUP_WIDTH), lambda g, i: (g, i, 0)))
        out_shape.append(jax.ShapeDtypeStruct((groups, t, GROUP_WIDTH), a_dtype))
    for w in cast_weights:
        rows, cols = w.shape
        assert rows % (n_steps * 16) == 0, "row block must be a whole number of bf16 sublane tiles"
        spec = pl.BlockSpec((rows // n_steps, cols), lambda g, i: (g * steps_per_group + i, 0))
        in_specs.append(spec)
        out_specs.append(spec)
        out_shape.append(jax.ShapeDtypeStruct((rows, cols), BF16))
        args.append(w)
    return pl.pallas_call(
        functools.partial(_proj_kernel, k_transposed=k_transposed, n_cast=len(cast_weights)),
        grid=(groups, t // tm),
        in_specs=in_specs,
        out_specs=out_specs,
        out_shape=out_shape,
        compiler_params=pltpu.CompilerParams(
            dimension_semantics=("arbitrary", "arbitrary"), vmem_limit_bytes=VMEM_LIMIT),
        name="project",
    )(*args)


MIX_TILE = 512


def _dot3_rhs(lhs_bf16, rhs_f32):
    hi, mid, lo = _split3_bf16(rhs_f32)
    return _dot(lhs_bf16, hi) + _dot(lhs_bf16, mid) + _dot(lhs_bf16, lo)


def _dot3_lhs(lhs_f32, rhs_bf16):
    hi, mid, lo = _split3_bf16(lhs_f32)
    return _dot(hi, rhs_bf16) + _dot(mid, rhs_bf16) + _dot(lo, rhs_bf16)


FF_CHUNKS = (768, 768, 768, 512)
MIXER_LOOKAHEAD = 2
OUT_ROW_BLOCKS = 2


def _gate_rows(g_ref, bias_ref, tril, triu, n_chunks):
    pre = [g_ref[c * CHUNK:(c + 1) * CHUNK, :] + bias_ref[...] for c in range(n_chunks)]
    bc_all = _dot3_rhs(tril, jnp.concatenate([_log_sigmoid(p) for p in pre], axis=1))
    rows_all = jnp.concatenate([p.T[0:2 * HEADS, :] for p in pre], axis=0)
    b_rows_all = _dot3_lhs(_log_sigmoid(rows_all), triu)
    bc, arow, blast = [], [], []
    for c in range(n_chunks):
        bc.append(bc_all[:, c * LANES:(c + 1) * LANES])
        rows = rows_all[c * 2 * HEADS:(c + 1) * 2 * HEADS, :]
        b_rows = b_rows_all[c * 2 * HEADS:(c + 1) * 2 * HEADS, :]
        arow.append([rows[hd:hd + 1, :] - b_rows[HEADS + hd:HEADS + hd + 1, :] for hd in range(HEADS)])
        blast.append([jnp.min(b_rows[HEADS + hd:HEADS + hd + 1, :], axis=1, keepdims=True) for hd in range(HEADS)])
    return bc, arow, blast


def _mlstm_local(q, kt, a_row, b_last, causal):
    amat = jnp.where(causal, a_row, -jnp.inf)
    m_row = jnp.max(amat, axis=1, keepdims=True)
    s_loc = (_dot(q, kt) * jnp.exp(amat - m_row)).astype(BF16)
    g_row = b_last + a_row
    g_loc = jnp.max(g_row, axis=1, keepdims=True)
    kw = (kt.astype(F32) * jnp.exp(g_row - g_loc)).astype(BF16)
    return m_row, s_loc, g_loc, kw


def _mlstm_readout(local, q, v, og, g_head, b_last, b_rep, cta, m_prev, ones_blk):
    m_row, s_loc, g_loc, kw = local
    va = jnp.concatenate([v, ones_blk], axis=1)
    nd_loc = _dot(s_loc, va)
    u_aug = _dot(kw, va)
    inter = _dot(q, cta.astype(BF16))
    mm = jnp.maximum(m_prev, m_row)
    f_loc = jnp.exp(m_row - mm) * QK_SCALE
    f_int = jnp.exp(m_prev - mm) * QK_SCALE
    nd = f_loc * nd_loc + f_int * inter
    clamp = jnp.exp(-(b_rep + mm))
    hh = nd[:, :HEAD_DIM] / jnp.maximum(jnp.abs(nd[:, HEAD_DIM:]), clamp)
    ml = _rms(hh) * g_head * og

    dec = b_last + m_prev
    m_new = jnp.maximum(dec, g_loc)
    cta_new = jnp.exp(dec - m_new) * cta + jnp.exp(g_loc - m_new) * u_aug
    return ml, cta_new, m_new


def _prompt_tail_kernel(a_ref, kt_ref, b_ref, g_ref, x_ref, gt1_ref, sh2_ref, sc2_ref, gt2_ref,
                        bias_ref, gh_ref, ws_ref, bs_ref, wo_ref, g2_ref, wgu_ref, wdn_ref, gf_ref,
                        y_ref, c_out, n_out, m_out,
                        mix_ref, cta_ref, m_ref, *, tiles_per_seq):
    step = pl.program_id(0)
    n_tiles = pl.num_programs(0) - 1
    refs = (a_ref, kt_ref, b_ref, g_ref, x_ref, gt1_ref, sh2_ref, sc2_ref, gt2_ref,
            bias_ref, gh_ref, ws_ref, bs_ref, wo_ref, g2_ref, wgu_ref, wdn_ref, gf_ref,
            y_ref, mix_ref, cta_ref, m_ref)

    @pl.when(step % tiles_per_seq == 0)
    def _():
        cta_ref[...] = jnp.zeros_like(cta_ref)
        m_ref[...] = jnp.zeros_like(m_ref)

    @pl.when(step == 0)
    def _():
        _prompt_tail_body(refs, with_mixer=True, with_out=False)

    @pl.when(jnp.logical_and(step > 0, step < n_tiles))
    def _():
        _prompt_tail_body(refs, with_mixer=True, with_out=True)

    @pl.when(step == n_tiles)
    def _():
        _prompt_tail_body(refs, with_mixer=False, with_out=True)

    @pl.when(jnp.logical_and(step % tiles_per_seq == tiles_per_seq - 1, step < n_tiles))
    def _():
        for hd in range(HEADS):
            cta = cta_ref[hd]
            c_out[hd] = cta[:, :HEAD_DIM].T
            n_out[hd: hd + 1, :] = cta[:, HEAD_DIM:].T[0:1, :]
            m_out[hd: hd + 1, :] = m_ref[hd][0:1, :]


def _prompt_tail_body(refs, *, with_mixer, with_out):
    (a_ref, kt_ref, b_ref, g_ref, x_ref, gt1_ref, sh2_ref, sc2_ref, gt2_ref,
     bias_ref, gh_ref, ws_ref, bs_ref, wo_ref, g2_ref, wgu_ref, wdn_ref, gf_ref,
     y_ref, mix_ref, cta_ref, m_ref) = refs
    tm = a_ref.shape[0]
    n_chunks = tm // CHUNK
    n_copy = a_ref.shape[1] - GROUP_WIDTH

    row = lax.broadcasted_iota(jnp.int32, (CHUNK, CHUNK), 0)
    col = lax.broadcasted_iota(jnp.int32, (CHUNK, CHUNK), 1)
    causal = row >= col
    tril = jnp.where(causal, 1.0, 0.0).astype(BF16)
    triu = jnp.where(row <= col, 1.0, 0.0).astype(BF16)
    ones_blk = jnp.ones((CHUNK, HEAD_DIM), BF16)

    blocks = [slice(r0, r0 + tm // OUT_ROW_BLOCKS) for r0 in range(0, tm, tm // OUT_ROW_BLOCKS)] if with_out else []
    outproj = [_dot(mix_ref[rs, :], wo_ref[...]) for rs in blocks]
    x1, h2 = [], []
    for rs, op in zip(blocks, outproj):
        x1.append(x_ref[rs, :] + gt1_ref[...] * op)
        h = _rms(x1[-1]) * g2_ref[...]
        h2.append((h * (1.0 + sc2_ref[...]) + sh2_ref[...]).astype(BF16))

    if with_mixer:
        bc, arow, blast = _gate_rows(g_ref, bias_ref, tril, triu, n_chunks)
        state = [(cta_ref[hd], jnp.max(m_ref[hd][0:1, :], axis=1, keepdims=True)) for hd in range(HEADS)]
    sg_all = [None] * HEADS

    def mixer_local(c, hd):
        rs = slice(c * CHUNK, (c + 1) * CHUNK)
        sl = slice(hd * HEAD_DIM, (hd + 1) * HEAD_DIM)
        return _mlstm_local(a_ref[rs, sl], kt_ref[sl, rs], arow[c][hd], blast[c][hd], causal)

    def mixer_readout(c, hd, local):
        rs = slice(c * CHUNK, (c + 1) * CHUNK)
        sl = slice(hd * HEAD_DIM, (hd + 1) * HEAD_DIM)
        if c == 0:
            vsn_all = jnp.concatenate(
                [a_ref[cc * CHUNK:(cc + 1) * CHUNK, n_copy + hd * HEAD_DIM: n_copy + (hd + 1) * HEAD_DIM]
                 for cc in range(n_chunks)], axis=1)
            sg_all[hd] = _dot(ws_ref[hd], vsn_all)
        cta, m_prev = state[hd]
        b_rep = jnp.broadcast_to(bc[c][:, HEADS + hd: HEADS + hd + 1], (CHUNK, HEAD_DIM))
        ml, cta, m_prev = _mlstm_readout(
            local, a_ref[rs, sl], a_ref[rs, GROUP_WIDTH + hd * HEAD_DIM: GROUP_WIDTH + (hd + 1) * HEAD_DIM],
            b_ref[rs, sl], gh_ref[:, sl], blast[c][hd], b_rep, cta, m_prev, ones_blk)
        state[hd] = (cta, m_prev)
        mix_ref[rs, sl] = ml.astype(mix_ref.dtype)
        sg = sg_all[hd][:, c * CHUNK:(c + 1) * CHUNK] + bs_ref[hd]
        ug = b_ref[rs, GROUP_WIDTH + hd * HEAD_DIM: GROUP_WIDTH + (hd + 1) * HEAD_DIM]
        mix_ref[rs, GROUP_WIDTH + hd * HEAD_DIM: GROUP_WIDTH + (hd + 1) * HEAD_DIM] = (ug * sg).astype(mix_ref.dtype)

    pieces = [(c, hd) for c in range(n_chunks) for hd in range(HEADS)] if with_mixer else []
    per_ff = -(-len(pieces) // len(FF_CHUNKS))
    ahead = [mixer_local(*pieces[p]) for p in range(min(MIXER_LOOKAHEAD, len(pieces)))]
    acc = [None] * len(blocks)
    f0 = 0
    for j, width in enumerate(FF_CHUNKS):
        gate_up = [(_dot(h, wgu_ref[:, f0:f0 + width]), _dot(h, wgu_ref[:, D_FF + f0:D_FF + f0 + width]))
                   for h in h2]
        for idx in range(j * per_ff, min((j + 1) * per_ff, len(pieces))):
            if idx + MIXER_LOOKAHEAD < len(pieces):
                ahead.append(mixer_local(*pieces[idx + MIXER_LOOKAHEAD]))
            mixer_readout(*pieces[idx], ahead.pop(0))
        for r, (gate, up) in enumerate(gate_up):
            act = (gate * _sigmoid(gate) * up).astype(BF16)
            part = _dot(act, wdn_ref[f0:f0 + width, :])
            acc[r] = part if acc[r] is None else acc[r] + part
            if j + 1 == len(FF_CHUNKS):
                x2 = x1[r] + gt2_ref[...] * acc[r]
                y_ref[blocks[r], :] = _rms(x2) * gf_ref[...]
        f0 += width

    if with_mixer:
        for hd in range(HEADS):
            cta_ref[hd] = state[hd][0]
            m_ref[hd] = jnp.broadcast_to(state[hd][1], m_ref.shape[1:])


def _prompt_tail(a, kt, b, gates, x, mod, bias_row, g_head, ws_tril, bs_rep, w_out, g2, w_gu, w_dn, g_final):
    groups, t, n_a = a.shape
    tm = MIX_TILE
    tps = t // tm
    n_tiles = groups * tps
    assert sum(FF_CHUNKS) == D_FF

    def cur(i):
        return jnp.minimum(i, n_tiles - 1)

    def prev(i):
        return jnp.maximum(i - 1, 0)

    def rows(tile, width):
        return pl.BlockSpec((None, tm, width), lambda i: (tile(i) // tps, tile(i) % tps, 0))

    def mod_piece(piece):
        return pl.BlockSpec((None, 1, D_MODEL), lambda i: (prev(i) // tps, 0, piece))

    def const(shape):
        nd = len(shape)
        return pl.BlockSpec(shape, lambda i: (0,) * nd, pipeline_mode=pl.Buffered(1))

    def per_seq(shape):
        nd = len(shape)
        return pl.BlockSpec((None,) + shape, lambda i: (cur(i) // tps,) + (0,) * nd)

    return pl.pallas_call(
        functools.partial(_prompt_tail_kernel, tiles_per_seq=tps),
        grid=(n_tiles + 1,),
        in_specs=[
            rows(cur, n_a),
            pl.BlockSpec((None, GROUP_WIDTH, tm), lambda i: (cur(i) // tps, 0, cur(i) % tps)),
            rows(cur, N_B),
            rows(cur, LANES),
            rows(prev, D_MODEL),
            mod_piece(2), mod_piece(3), mod_piece(4), mod_piece(5),
            const((1, LANES)),
            const((1, GROUP_WIDTH)),
            const((HEADS, CHUNK, CHUNK)),
            const((HEADS, CHUNK, LANES)),
            const((D_MODEL, D_MODEL)),
            const((1, D_MODEL)),
            const((D_MODEL, 2 * D_FF)),
            const((D_FF, D_MODEL)),
            const((1, D_MODEL)),
        ],
        out_specs=[
            rows(prev, D_MODEL),
            per_seq((HEADS, HEAD_DIM, HEAD_DIM)),
            per_seq((HEADS, HEAD_DIM)),
            per_seq((HEADS, LANES)),
        ],
        out_shape=[
            jax.ShapeDtypeStruct((groups, t, D_MODEL), F32),
            jax.ShapeDtypeStruct((groups, HEADS, HEAD_DIM, HEAD_DIM), F32),
            jax.ShapeDtypeStruct((groups, HEADS, HEAD_DIM), F32),
            jax.ShapeDtypeStruct((groups, HEADS, LANES), F32),
        ],
        scratch_shapes=[
            pltpu.VMEM((tm, D_MODEL), BF16),
            pltpu.VMEM((HEADS, HEAD_DIM, 2 * HEAD_DIM), F32),
            pltpu.VMEM((HEADS, 8, LANES), F32),
        ],
        compiler_params=pltpu.CompilerParams(
            dimension_semantics=("arbitrary",), vmem_limit_bytes=VMEM_LIMIT),
        name="prompt_tail",
    )(a, kt, b, gates, x, mod, mod, mod, mod, bias_row, g_head, ws_tril, bs_rep, w_out, g2, w_gu, w_dn, g_final)


SAMPLE_TOKENS_PER_STEP = 128
SAMPLE_UNROLL = 8


def _mix_sample_kernel(q_ref, k_ref, v_ref, vsn_ref, og_ref, ug_ref, g_ref, m0_ref, n0_ref, c_ref,
                       bias_ref, gh_ref, ws0_ref, bs0_ref,
                       mix_ref, c_out, n_out, m_out,
                       vt_ref, kp_ref, wd_ref, cqt_ref):
    hd = pl.program_id(0)
    grp = pl.program_id(1)
    nb = q_ref.shape[0]
    lane = lax.broadcasted_iota(jnp.int32, (nb, LANES), 1)

    def gate_terms():
        pre = g_ref[...] + bias_ref[...]
        i_pre = jnp.sum(jnp.where(lane == hd, pre, 0.0), axis=1, keepdims=True)
        f_pre = jnp.sum(jnp.where(lane == hd + HEADS, pre, 0.0), axis=1, keepdims=True)
        m_prev = jnp.sum(jnp.where(lane == hd, m0_ref[...], 0.0), axis=1, keepdims=True)
        inter = _log_sigmoid(f_pre) + m_prev
        m_t = jnp.maximum(inter, i_pre)
        return m_t, jnp.exp(i_pre - m_t), jnp.exp(inter - m_t)

    @pl.when(grp == 0)
    def _():
        _, w_in, w_dec = gate_terms()
        vt_ref[...] = v_ref[...].T
        kp_ref[...] = (w_in * k_ref[...]).astype(BF16)
        wd_ref[...] = jnp.broadcast_to(w_dec, wd_ref.shape)
        cqt_ref[...] = jnp.zeros_like(cqt_ref)

    lane_sq = lax.broadcasted_iota(jnp.int32, (HEAD_DIM, LANES), 1)
    tokens = c_ref.shape[0]

    def body(r, cqt):
        tok = grp * tokens + r
        q_row = q_ref[pl.ds(tok, 1), :]
        c_b = c_ref[r]
        sel = lane_sq == tok
        cq_col = jnp.sum(c_b * q_row, axis=1, keepdims=True)
        v_sel = jnp.where(sel, vt_ref[...], 0.0).astype(BF16)
        outer = _dot(v_sel, kp_ref[...])
        c_out[r] = wd_ref[pl.ds(tok, 1), :] * c_b + outer
        return jnp.where(sel, cq_col, cqt)

    cqt_ref[...] = lax.fori_loop(0, tokens, body, cqt_ref[...], unroll=SAMPLE_UNROLL)

    @pl.when(grp == pl.num_programs(1) - 1)
    def _():
        m_t, w_in, w_dec = gate_terms()
        q = q_ref[...]
        k = k_ref[...]
        v = v_ref[...]
        n0 = n0_ref[...]
        cq = cqt_ref[...].T
        s = jnp.sum(q * k, axis=1, keepdims=True) * (QK_SCALE * w_in)
        w_inter = w_dec * QK_SCALE
        num = s * v + w_inter * cq
        den = s + w_inter * jnp.sum(n0 * q, axis=1, keepdims=True)
        hh = num / jnp.maximum(jnp.abs(den), jnp.exp(-m_t))
        ml = _rms(hh) * gh_ref[...] * og_ref[...]
        cm = ug_ref[...] * (ws0_ref[...] * vsn_ref[...] + bs0_ref[...])
        n_out[...] = w_dec * n0 + w_in * k
        m_out[...] = jnp.broadcast_to(m_t, m_out.shape)
        for hh_static in range(HEADS):
            @pl.when(hd == hh_static)
            def _():
                mix_ref[:, hh_static * HEAD_DIM: (hh_static + 1) * HEAD_DIM] = ml
                mix_ref[:, GROUP_WIDTH + hh_static * HEAD_DIM: GROUP_WIDTH + (hh_static + 1) * HEAD_DIM] = cm


def _mix_sample(a, k, b, gates, m0_pad, n0, c0, bias_row, g_head, ws0_row, bs0_row):
    nb = a.shape[0]
    tb = SAMPLE_TOKENS_PER_STEP

    def head_block(offset):
        return pl.BlockSpec((nb, HEAD_DIM), lambda h, g: (0, offset + h))

    full = pl.BlockSpec((nb, LANES), lambda h, g: (0, 0))
    head_row = pl.BlockSpec((1, HEAD_DIM), lambda h, g: (0, h))
    c_spec = pl.BlockSpec((tb, None, HEAD_DIM, HEAD_DIM), lambda h, g: (g, h, 0, 0))
    return pl.pallas_call(
        _mix_sample_kernel,
        grid=(HEADS, nb // tb),
        in_specs=[
            head_block(0), head_block(0), head_block(HEADS), head_block(2 * HEADS),
            head_block(0), head_block(HEADS),
            full, full, head_block(0), c_spec,
            pl.BlockSpec((1, LANES), lambda h, g: (0, 0)),
            head_row, head_row, head_row,
        ],
        out_specs=[
            pl.BlockSpec((nb, D_MODEL), lambda h, g: (0, 0)),
            c_spec,
            head_block(0),
            head_block(0),
        ],
        out_shape=[
            jax.ShapeDtypeStruct((nb, D_MODEL), F32),
            jax.ShapeDtypeStruct(c0.shape, F32),
            jax.ShapeDtypeStruct((nb, GROUP_WIDTH), F32),
            jax.ShapeDtypeStruct((nb, GROUP_WIDTH), F32),
        ],
        scratch_shapes=[
            pltpu.VMEM((HEAD_DIM, nb), F32),
            pltpu.VMEM((nb, HEAD_DIM), BF16),
            pltpu.VMEM((nb, LANES), F32),
            pltpu.VMEM((HEAD_DIM, nb), F32),
        ],
        compiler_params=pltpu.CompilerParams(
            dimension_semantics=("arbitrary", "arbitrary"), vmem_limit_bytes=VMEM_LIMIT),
        name="mix_sample",
    )(a, k, a, a, b, b, gates, m0_pad, n0, c0, bias_row, g_head, ws0_row, bs0_row)


def _out_kernel(x_ref, mix_ref, gt1_ref, sh2_ref, sc2_ref, gt2_ref, wo_ref, g2_ref, wgu_ref, wdn_ref, gf_ref,
                y_ref):
    x1 = x_ref[...] + gt1_ref[...] * _dot(mix_ref[...].astype(BF16), wo_ref[...])
    h2 = _rms(x1) * g2_ref[...]
    h2 = (h2 * (1.0 + sc2_ref[...]) + sh2_ref[...]).astype(BF16)
    gate = _dot(h2, wgu_ref[:, :D_FF])
    up = _dot(h2, wgu_ref[:, D_FF:])
    act = (gate * _sigmoid(gate) * up).astype(BF16)
    x2 = x1 + gt2_ref[...] * _dot(act, wdn_ref[...])
    y_ref[:, 0, :] = _rms(x2) * gf_ref[...]


def _output_stage(x, mix, mod, w_out, g2, w_gu, w_dn, g_final, *, tm, per_row):
    groups, t, _ = x.shape
    return pl.pallas_call(
        _out_kernel,
        grid=(groups, t // tm),
        in_specs=[
            pl.BlockSpec((None, tm, D_MODEL), lambda g, i: (g, i, 0)),
            pl.BlockSpec((None, tm, D_MODEL), lambda g, i: (g, i, 0)),
            _mod_spec(2, tm, per_row),
            _mod_spec(3, tm, per_row),
            _mod_spec(4, tm, per_row),
            _mod_spec(5, tm, per_row),
            _const_spec((D_MODEL, D_MODEL)),
            _const_spec((1, D_MODEL)),
            _const_spec((D_MODEL, 2 * D_FF)),
            _const_spec((D_FF, D_MODEL)),
            _const_spec((1, D_MODEL)),
        ],
        out_specs=pl.BlockSpec((tm, 1, D_MODEL), lambda g, i: (g * (t // tm) + i, 0, 0)),
        out_shape=jax.ShapeDtypeStruct((groups * t, 1, D_MODEL), F32),
        compiler_params=pltpu.CompilerParams(
            dimension_semantics=("arbitrary", "arbitrary"), vmem_limit_bytes=VMEM_LIMIT),
        name="output_stage",
    )(x, mix, mod, mod, mod, mod, w_out, g2, w_gu, w_dn, g_final)


def kernel(x_prompt, x_sample, c_prompt, c_sample, state_mlstm_C, state_mlstm_n, state_mlstm_m, w_ada, b_ada, g_norm1, w_in, b_gate, g_mlstm_head, ln_v_g, ln_v_b, w_s, b_s, w_out, g_norm2, w_gate_up, w_down, g_final):
    depth = w_ada.shape[0]
    assert depth == 1, "single-layer trunk"
    batch, seq, _ = x_prompt.shape
    nb = x_sample.shape[0]
    assert x_sample.shape[1] == 1

    gw = GROUP_WIDTH
    g1 = g_norm1[0].reshape(1, D_MODEL)
    g2 = g_norm2[0].reshape(1, D_MODEL)
    gf = g_final.reshape(1, D_MODEL)
    ln_g = ln_v_g[0].reshape(1, gw)
    ln_b = ln_v_b[0].reshape(1, gw)
    g_head = g_mlstm_head[0].reshape(1, gw)
    bias_row = jnp.pad(b_gate[0], (0, LANES - 2 * HEADS)).reshape(1, LANES)
    tril = jnp.tril(jnp.ones((CHUNK, CHUNK), dtype=bool))
    ws_tril = jnp.where(tril[None], w_s[0], 0.0).astype(BF16)
    bs_rep = jnp.broadcast_to(b_s[0][:, :, None], (HEADS, CHUNK, LANES))
    ws0_row = jnp.repeat(w_s[0][:, 0, 0], HEAD_DIM).reshape(1, gw)
    bs0_row = jnp.repeat(b_s[0][:, 0], HEAD_DIM).reshape(1, gw)

    mod_s, mod_p, w_main, w_gates, w_k, w_kt = _modulation(c_sample, c_prompt, w_ada[0], b_ada[0].reshape(1, -1), w_in[0].T)
    mod_s = mod_s.reshape(1, nb, N_MOD * D_MODEL)

    a_p, b_p, g_p, kt_p, w_out_b, w_gu_b, w_dn_b = _project(
        x_prompt, mod_p, g1, w_main, w_gates, ln_g, ln_b, w_kt, (w_out[0], w_gate_up[0], w_down[0]),
        k_transposed=True, tm=PROJ_TILE, per_row=False, a_dtype=BF16)
    y_p, c_p, n_p, m_p = _prompt_tail(a_p, kt_p, b_p, g_p, x_prompt, mod_p, bias_row, g_head, ws_tril, bs_rep,
                                      w_out_b, g2, w_gu_b, w_dn_b, gf)

    xs = x_sample.reshape(1, nb, D_MODEL)
    a_s, b_s_act, g_s, k_s = _project(xs, mod_s, g1, w_main, w_gates, ln_g, ln_b, w_k, k_transposed=False,
                                      tm=nb, per_row=True, a_dtype=F32)
    a_s2, b_s2, g_s2 = a_s[0], b_s_act[0], g_s[0]
    m0_pad = jnp.pad(state_mlstm_m[0], ((0, 0), (0, LANES - HEADS)))
    n0 = state_mlstm_n[0].reshape(nb, gw)
    mix_s, c_s, n_s, m_s = _mix_sample(a_s2, k_s[0], b_s2, g_s2, m0_pad, n0, state_mlstm_C[0],
                                       bias_row, g_head, ws0_row, bs0_row)
    y_s = _output_stage(xs, mix_s.reshape(1, nb, D_MODEL), mod_s, w_out_b, g2, w_gu_b, w_dn_b, gf,
                        tm=nb, per_row=True)

    return (
        y_p,
        y_s,
        c_p[None],
        n_p[None],
        m_p[:, :, 0][None],
        c_s[None],
        n_s.reshape(nb, HEADS, HEAD_DIM)[None],
        m_s.reshape(nb, HEADS, HEAD_DIM)[:, :, 0][None],
        a_s2[:, 2 * gw:].reshape(nb, 1, HEADS, HEAD_DIM)[None],
    )
```

```python
import functools
import math

import jax
import jax.numpy as jnp
from jax import lax
from jax.experimental import pallas as pl
from jax.experimental.pallas import tpu as pltpu

F32 = jnp.float32
BF16 = jnp.bfloat16

D_MODEL = 1024
HEADS = 4
HEAD_DIM = 128
GROUP_WIDTH = HEADS * HEAD_DIM
CHUNK = 128
D_FF = 2816
N_MOD = 6
EPS = 1e-6
QK_SCALE = HEAD_DIM ** -0.5
LANES = 128
BF16_SUBLANES = 16

N_B = 2 * GROUP_WIDTH

VMEM_LIMIT = 56 * 1024 * 1024


def _dot(a, b):
    return jnp.dot(a, b, preferred_element_type=F32)


def _dot_nt(a, b):
    return lax.dot_general(a, b, (((1,), (1,)), ((), ())), preferred_element_type=F32)


def _sigmoid(x):
    return 1.0 / (1.0 + jnp.exp(-x))


def _gelu_tanh(x):
    c = math.sqrt(2.0 / math.pi)
    return x * (0.5 * (1.0 + jnp.tanh(c * (x + 0.044715 * (x * x * x)))))


def _log_sigmoid(x):
    return jnp.minimum(x, 0.0) - jnp.log1p(jnp.exp(-jnp.abs(x)))


def _rms(x):
    return x * lax.rsqrt(jnp.mean(x * x, axis=-1, keepdims=True) + EPS)


def _split3_bf16(x):
    hi = x.astype(BF16)
    r1 = x - hi.astype(F32)
    mid = r1.astype(BF16)
    lo = (r1 - mid.astype(F32)).astype(BF16)
    return hi, mid, lo


MOD_TILE = 1024

IN_Q, IN_K, IN_V, IN_O, IN_GATES = 0, GROUP_WIDTH, 2 * GROUP_WIDTH, 3 * GROUP_WIDTH, 4 * GROUP_WIDTH
IN_U = IN_GATES + 2 * HEADS
IN_VS = IN_U + GROUP_WIDTH
N_IN = IN_VS + GROUP_WIDTH
MAIN_GROUPS = (IN_Q, IN_V, IN_VS, IN_O, IN_U)
N_MAIN = len(MAIN_GROUPS) * GROUP_WIDTH
GATE_ROWS = BF16_SUBLANES


def _mod_kernel(cs_ref, cp_ref, w_ref, b_ref, grp_ref, gate_ref, os_ref, op_ref, wmain_ref, wgate_ref, wk_ref, wkt_ref):
    step = pl.program_id(0)
    ns, npr = cs_ref.shape[0], cp_ref.shape[0]
    pad = -(ns + npr) % BF16_SUBLANES
    c = jnp.concatenate([cs_ref[...], cp_ref[...], jnp.zeros((pad, D_MODEL), F32)], axis=0)
    a = (c * _sigmoid(c)).astype(BF16)
    res = _dot(a, w_ref[...].astype(BF16)) + b_ref[...]
    os_ref[...] = res[:ns]
    op_ref[:, 0, :] = res[ns:ns + npr]

    def transposed_into(dst_ref):
        for blk in range(GROUP_WIDTH // LANES):
            rows = grp_ref[blk * LANES:(blk + 1) * LANES, :]
            dst_ref[:, blk * LANES:(blk + 1) * LANES] = rows.T.astype(BF16)

    @pl.when(step < len(MAIN_GROUPS))
    def _():
        transposed_into(wmain_ref)

    @pl.when(step == len(MAIN_GROUPS))
    def _():
        wkt_ref[:GROUP_WIDTH, :] = grp_ref[...].astype(BF16)
        wkt_ref[GROUP_WIDTH:, :] = jnp.concatenate(
            [gate_ref[...], jnp.zeros((GATE_ROWS - 2 * HEADS, D_MODEL), F32)], axis=0).astype(BF16)
        transposed_into(wk_ref)

    @pl.when(step == 0)
    def _():
        gate_rows = jnp.concatenate([gate_ref[...], jnp.zeros((LANES - 2 * HEADS, D_MODEL), F32)], axis=0)
        wgate_ref[...] = gate_rows.T.astype(BF16)


def _modulation(c_sample, c_prompt, w_ada, b_ada, w_in_t):
    ns, npr = c_sample.shape[0], c_prompt.shape[0]
    tn = MOD_TILE
    steps = N_MOD * D_MODEL // tn
    group_starts = MAIN_GROUPS + (IN_K,)
    assert w_in_t.shape == (N_IN, D_MODEL) and steps == len(group_starts)

    def whole(shape):
        return pl.BlockSpec(shape, lambda j: (0, 0))

    def group_start(j):
        assert all(start % 8 == 0 for start in group_starts)
        return pl.multiple_of(sum(jnp.where(j == i, start, 0) for i, start in enumerate(group_starts)), 8)

    return pl.pallas_call(
        _mod_kernel,
        grid=(steps,),
        in_specs=[
            whole((ns, D_MODEL)),
            whole((npr, D_MODEL)),
            pl.BlockSpec((D_MODEL, tn), lambda j: (0, j)),
            pl.BlockSpec((1, tn), lambda j: (0, j)),
            pl.BlockSpec((pl.Element(GROUP_WIDTH), pl.Element(D_MODEL)), lambda j: (group_start(j), 0)),
            pl.BlockSpec((pl.Element(2 * HEADS), pl.Element(D_MODEL)), lambda j: (IN_GATES, 0)),
        ],
        out_specs=[
            pl.BlockSpec((ns, tn), lambda j: (0, j)),
            pl.BlockSpec((npr, 1, tn), lambda j: (0, 0, j)),
            pl.BlockSpec((D_MODEL, GROUP_WIDTH), lambda j: (0, jnp.minimum(j, len(MAIN_GROUPS) - 1))),
            whole((D_MODEL, LANES)),
            whole((D_MODEL, GROUP_WIDTH)),
            whole((GROUP_WIDTH + GATE_ROWS, D_MODEL)),
        ],
        out_shape=[
            jax.ShapeDtypeStruct((ns, N_MOD * D_MODEL), F32),
            jax.ShapeDtypeStruct((npr, 1, N_MOD * D_MODEL), F32),
            jax.ShapeDtypeStruct((D_MODEL, N_MAIN), BF16),
            jax.ShapeDtypeStruct((D_MODEL, LANES), BF16),
            jax.ShapeDtypeStruct((D_MODEL, GROUP_WIDTH), BF16),
            jax.ShapeDtypeStruct((GROUP_WIDTH + GATE_ROWS, D_MODEL), BF16),
        ],
        compiler_params=pltpu.CompilerParams(dimension_semantics=("arbitrary",), vmem_limit_bytes=VMEM_LIMIT),
        name="modulation",
    )(c_sample, c_prompt, w_ada, b_ada, w_in_t, w_in_t)


def _mod_spec(piece, tm, per_row):
    if per_row:
        return pl.BlockSpec((None, tm, D_MODEL), lambda g, t: (g, t, piece))
    return pl.BlockSpec((None, 1, D_MODEL), lambda g, t: (g, 0, piece))


def _const_spec(shape):
    nd = len(shape)
    return pl.BlockSpec(shape, lambda g, t: (0,) * nd, pipeline_mode=pl.Buffered(1))


PROJ_TILE = 1024


def _proj_body(x_ref, sh_ref, sc_ref, g1_ref, w_ref, wg_ref, lng_ref, lnb_ref, a_ref, b_ref, g_ref):
    n_a = a_ref.shape[1]
    n_copy = n_a - GROUP_WIDTH
    x = x_ref[...]
    h = _rms(x) * g1_ref[...]
    h = (h * (1.0 + sc_ref[...]) + sh_ref[...]).astype(BF16)
    p = _dot(h, w_ref[...])
    a_ref[:, :n_copy] = p[:, :n_copy].astype(a_ref.dtype)
    vs = _gelu_tanh(p[:, n_copy:n_a])
    for hd in range(HEADS):
        sl = slice(hd * HEAD_DIM, (hd + 1) * HEAD_DIM)
        v = vs[:, sl]
        mu = jnp.mean(v, axis=-1, keepdims=True)
        vc = v - mu
        var = jnp.mean(vc * vc, axis=-1, keepdims=True)
        y = vc * lax.rsqrt(var + EPS) * lng_ref[:, sl] + lnb_ref[:, sl]
        a_ref[:, n_copy + hd * HEAD_DIM: n_copy + (hd + 1) * HEAD_DIM] = y.astype(a_ref.dtype)
    b_ref[:, :GROUP_WIDTH] = _sigmoid(p[:, n_a: n_a + GROUP_WIDTH])
    b_ref[:, GROUP_WIDTH:] = _gelu_tanh(p[:, n_a + GROUP_WIDTH: n_a + N_B])
    if wg_ref is not None:
        g_ref[...] = _dot(h, wg_ref[...])
    return h


def _proj_kernel(*refs, k_transposed, n_cast):
    ins, outs = refs[:9 + n_cast], refs[9 + n_cast:]
    wk_ref, k_ref = ins[8], outs[3]
    if k_transposed:
        h = _proj_body(*ins[:5], None, *ins[6:8], *outs[:3])
        kt_gates = _dot_nt(wk_ref[...], h)
        k_ref[...] = kt_gates[:GROUP_WIDTH].astype(k_ref.dtype)
        outs[2][...] = kt_gates[GROUP_WIDTH:]
    else:
        h = _proj_body(*ins[:8], *outs[:3])
        k_ref[...] = _dot(h, wk_ref[...]).astype(k_ref.dtype)
    for src, dst in zip(ins[9:], outs[4:]):
        dst[...] = src[...].astype(dst.dtype)


def _project(x, mod, g1, w_main, w_gates, ln_g, ln_b, w_k, cast_weights=(), *, k_transposed, tm, per_row, a_dtype):
    groups, t, _ = x.shape
    n_a = w_main.shape[1] - N_B
    steps_per_group = t // tm
    n_steps = groups * steps_per_group
    in_specs = [
        pl.BlockSpec((None, tm, D_MODEL), lambda g, i: (g, i, 0)),
        _mod_spec(0, tm, per_row),
        _mod_spec(1, tm, per_row),
        _const_spec((1, D_MODEL)),
        _const_spec(w_main.shape),
        _const_spec(w_gates.shape),
        _const_spec((1, GROUP_WIDTH)),
        _const_spec((1, GROUP_WIDTH)),
    ]
    out_specs = [
        pl.BlockSpec((None, tm, n_a), lambda g, i: (g, i, 0)),
        pl.BlockSpec((None, tm, N_B), lambda g, i: (g, i, 0)),
    ]
    out_shape = [
        jax.ShapeDtypeStruct((groups, t, n_a), a_dtype),
        jax.ShapeDtypeStruct((groups, t, N_B), F32),
    ]
    if k_transposed:
        out_specs.append(pl.BlockSpec((None, GATE_ROWS, tm), lambda g, i: (g, 0, i)))
        out_shape.append(jax.ShapeDtypeStruct((groups, GATE_ROWS, t), F32))
    else:
        out_specs.append(pl.BlockSpec((None, tm, LANES), lambda g, i: (g, i, 0)))
        out_shape.append(jax.ShapeDtypeStruct((groups, t, LANES), F32))
    args = [x, mod, mod, g1, w_main, w_gates, ln_g, ln_b, w_k]
    in_specs.append(_const_spec(w_k.shape))
    if k_transposed:
        out_specs.append(pl.BlockSpec((None, GROUP_WIDTH, tm), lambda g, i: (g, 0, i)))
        out_shape.append(jax.ShapeDtypeStruct((groups, GROUP_WIDTH, t), a_dtype))
    else:
        out_specs.append(pl.BlockSpec((None, tm, GROUP_WIDTH), lambda g, i: (g, i, 0)))
        out_shape.append(jax.ShapeDtypeStruct((groups, t, GROUP_WIDTH), a_dtype))
    for w in cast_weights:
        rows, cols = w.shape
        assert rows % (n_steps * 16) == 0, "row block must be a whole number of bf16 sublane tiles"
        spec = pl.BlockSpec((rows // n_steps, cols), lambda g, i: (g * steps_per_group + i, 0))
        in_specs.append(spec)
        out_specs.append(spec)
        out_shape.append(jax.ShapeDtypeStruct((rows, cols), BF16))
        args.append(w)
    return pl.pallas_call(
        functools.partial(_proj_kernel, k_transposed=k_transposed, n_cast=len(cast_weights)),
        grid=(groups, t // tm),
        in_specs=in_specs,
        out_specs=out_specs,
        out_shape=out_shape,
        compiler_params=pltpu.CompilerParams(
            dimension_semantics=("arbitrary", "arbitrary"), vmem_limit_bytes=VMEM_LIMIT),
        name="project",
    )(*args)


MIX_TILE = 512


def _dot3_rhs(lhs_bf16, rhs_f32):
    hi, mid, lo = _split3_bf16(rhs_f32)
    return _dot(lhs_bf16, hi) + _dot(lhs_bf16, mid) + _dot(lhs_bf16, lo)


def _dot3_lhs(lhs_f32, rhs_bf16):
    hi, mid, lo = _split3_bf16(lhs_f32)
    return _dot(hi, rhs_bf16) + _dot(mid, rhs_bf16) + _dot(lo, rhs_bf16)


FF_CHUNKS = (768, 768, 768, 512)
MIXER_LOOKAHEAD = 2
OUT_ROW_BLOCKS = 2


def _gate_rows(g_ref, bias_ref, tril, triu, n_chunks):
    n_gates = 2 * HEADS
    rows = [g_ref[0:n_gates, c * CHUNK:(c + 1) * CHUNK] + bias_ref[...] for c in range(n_chunks)]
    pad = jnp.zeros((LANES - n_gates, CHUNK), F32)
    cols = [jnp.concatenate([r, pad], axis=0).T for r in rows]
    bc_all = _dot3_rhs(tril, jnp.concatenate([_log_sigmoid(p) for p in cols], axis=1))
    rows_all = jnp.concatenate(rows, axis=0)
    b_rows_all = _dot3_lhs(_log_sigmoid(rows_all), triu)
    bc, arow, blast = [], [], []
    for c in range(n_chunks):
        bc.append(bc_all[:, c * LANES:(c + 1) * LANES])
        rows = rows_all[c * 2 * HEADS:(c + 1) * 2 * HEADS, :]
        b_rows = b_rows_all[c * 2 * HEADS:(c + 1) * 2 * HEADS, :]
        arow.append([rows[hd:hd + 1, :] - b_rows[HEADS + hd:HEADS + hd + 1, :] for hd in range(HEADS)])
        blast.append([jnp.min(b_rows[HEADS + hd:HEADS + hd + 1, :], axis=1, keepdims=True) for hd in range(HEADS)])
    return bc, arow, blast


def _mlstm_local(q, kt, a_row, b_last, causal):
    amat = jnp.where(causal, a_row, -jnp.inf)
    m_row = jnp.max(amat, axis=1, keepdims=True)
    s_loc = (_dot(q, kt) * jnp.exp(amat - m_row)).astype(BF16)
    g_row = b_last + a_row
    g_loc = jnp.max(g_row, axis=1, keepdims=True)
    kw = (kt.astype(F32) * jnp.exp(g_row - g_loc)).astype(BF16)
    return m_row, s_loc, g_loc, kw


def _mlstm_readout(local, q, v, og, g_head, b_last, b_rep, cta, m_prev, ones_blk):
    m_row, s_loc, g_loc, kw = local
    va = jnp.concatenate([v, ones_blk], axis=1)
    nd_loc = _dot(s_loc, va)
    u_aug = _dot(kw, va)
    inter = _dot(q, cta.astype(BF16))
    mm = jnp.maximum(m_prev, m_row)
    f_loc = jnp.exp(m_row - mm) * QK_SCALE
    f_int = jnp.exp(m_prev - mm) * QK_SCALE
    nd = f_loc * nd_loc + f_int * inter
    clamp = jnp.exp(-(b_rep + mm))
    hh = nd[:, :HEAD_DIM] / jnp.maximum(jnp.abs(nd[:, HEAD_DIM:]), clamp)
    ml = _rms(hh) * g_head * og

    dec = b_last + m_prev
    m_new = jnp.maximum(dec, g_loc)
    cta_new = jnp.exp(dec - m_new) * cta + jnp.exp(g_loc - m_new) * u_aug
    return ml, cta_new, m_new


def _prompt_tail_kernel(a_ref, kt_ref, b_ref, g_ref, x_ref, gt1_ref, sh2_ref, sc2_ref, gt2_ref,
                        bias_ref, gh_ref, ws_ref, bs_ref, wo_ref, g2_ref, wgu_ref, wdn_ref, gf_ref,
                        y_ref, c_out, n_out, m_out,
                        mix_ref, cta_ref, m_ref, *, tiles_per_seq):
    step = pl.program_id(0)
    n_tiles = pl.num_programs(0) - 1
    refs = (a_ref, kt_ref, b_ref, g_ref, x_ref, gt1_ref, sh2_ref, sc2_ref, gt2_ref,
            bias_ref, gh_ref, ws_ref, bs_ref, wo_ref, g2_ref, wgu_ref, wdn_ref, gf_ref,
            y_ref, mix_ref, cta_ref, m_ref)

    @pl.when(step % tiles_per_seq == 0)
    def _():
        cta_ref[...] = jnp.zeros_like(cta_ref)
        m_ref[...] = jnp.zeros_like(m_ref)

    @pl.when(step == 0)
    def _():
        _prompt_tail_body(refs, with_mixer=True, with_out=False)

    @pl.when(jnp.logical_and(step > 0, step < n_tiles))
    def _():
        _prompt_tail_body(refs, with_mixer=True, with_out=True)

    @pl.when(step == n_tiles)
    def _():
        _prompt_tail_body(refs, with_mixer=False, with_out=True)

    @pl.when(jnp.logical_and(step % tiles_per_seq == tiles_per_seq - 1, step < n_tiles))
    def _():
        for hd in range(HEADS):
            cta = cta_ref[hd]
            c_out[hd] = cta[:, :HEAD_DIM].T
            n_out[hd: hd + 1, :] = cta[:, HEAD_DIM:].T[0:1, :]
            m_out[hd: hd + 1, :] = m_ref[hd][0:1, :]


def _prompt_tail_body(refs, *, with_mixer, with_out):
    (a_ref, kt_ref, b_ref, g_ref, x_ref, gt1_ref, sh2_ref, sc2_ref, gt2_ref,
     bias_ref, gh_ref, ws_ref, bs_ref, wo_ref, g2_ref, wgu_ref, wdn_ref, gf_ref,
     y_ref, mix_ref, cta_ref, m_ref) = refs
    tm = a_ref.shape[0]
    n_chunks = tm // CHUNK
    n_copy = a_ref.shape[1] - GROUP_WIDTH

    row = lax.broadcasted_iota(jnp.int32, (CHUNK, CHUNK), 0)
    col = lax.broadcasted_iota(jnp.int32, (CHUNK, CHUNK), 1)
    causal = row >= col
    tril = jnp.where(causal, 1.0, 0.0).astype(BF16)
    triu = jnp.where(row <= col, 1.0, 0.0).astype(BF16)
    ones_blk = jnp.ones((CHUNK, HEAD_DIM), BF16)

    blocks = [slice(r0, r0 + tm // OUT_ROW_BLOCKS) for r0 in range(0, tm, tm // OUT_ROW_BLOCKS)] if with_out else []
    outproj = [_dot(mix_ref[rs, :], wo_ref[...]) for rs in blocks]
    x1, h2 = [], []
    for rs, op in zip(blocks, outproj):
        x1.append(x_ref[rs, :] + gt1_ref[...] * op)
        h = _rms(x1[-1]) * g2_ref[...]
        h2.append((h * (1.0 + sc2_ref[...]) + sh2_ref[...]).astype(BF16))

    if with_mixer:
        bc, arow, blast = _gate_rows(g_ref, bias_ref, tril, triu, n_chunks)
        state = [(cta_ref[hd], jnp.max(m_ref[hd][0:1, :], axis=1, keepdims=True)) for hd in range(HEADS)]
    sg_all = [None] * HEADS

    def mixer_local(c, hd):
        rs = slice(c * CHUNK, (c + 1) * CHUNK)
        sl = slice(hd * HEAD_DIM, (hd + 1) * HEAD_DIM)
        return _mlstm_local(a_ref[rs, sl], kt_ref[sl, rs], arow[c][hd], blast[c][hd], causal)

    def mixer_readout(c, hd, local):
        rs = slice(c * CHUNK, (c + 1) * CHUNK)
        sl = slice(hd * HEAD_DIM, (hd + 1) * HEAD_DIM)
        if c == 0:
            vsn_all = jnp.concatenate(
                [a_ref[cc * CHUNK:(cc + 1) * CHUNK, n_copy + hd * HEAD_DIM: n_copy + (hd + 1) * HEAD_DIM]
                 for cc in range(n_chunks)], axis=1)
            sg_all[hd] = _dot(ws_ref[hd], vsn_all)
        cta, m_prev = state[hd]
        b_rep = jnp.broadcast_to(bc[c][:, HEADS + hd: HEADS + hd + 1], (CHUNK, HEAD_DIM))
        ml, cta, m_prev = _mlstm_readout(
            local, a_ref[rs, sl], a_ref[rs, GROUP_WIDTH + hd * HEAD_DIM: GROUP_WIDTH + (hd + 1) * HEAD_DIM],
            b_ref[rs, sl], gh_ref[:, sl], blast[c][hd], b_rep, cta, m_prev, ones_blk)
        state[hd] = (cta, m_prev)
        mix_ref[rs, sl] = ml.astype(mix_ref.dtype)
        sg = sg_all[hd][:, c * CHUNK:(c + 1) * CHUNK] + bs_ref[hd]
        ug = b_ref[rs, GROUP_WIDTH + hd * HEAD_DIM: GROUP_WIDTH + (hd + 1) * HEAD_DIM]
        mix_ref[rs, GROUP_WIDTH + hd * HEAD_DIM: GROUP_WIDTH + (hd + 1) * HEAD_DIM] = (ug * sg).astype(mix_ref.dtype)

    pieces = [(c, hd) for c in range(n_chunks) for hd in range(HEADS)] if with_mixer else []
    per_ff = -(-len(pieces) // len(FF_CHUNKS))
    ahead = [mixer_local(*pieces[p]) for p in range(min(MIXER_LOOKAHEAD, len(pieces)))]
    acc = [None] * len(blocks)
    f0 = 0
    for j, width in enumerate(FF_CHUNKS):
        gate_up = [(_dot(h, wgu_ref[:, f0:f0 + width]), _dot(h, wgu_ref[:, D_FF + f0:D_FF + f0 + width]))
                   for h in h2]
        for idx in range(j * per_ff, min((j + 1) * per_ff, len(pieces))):
            if idx + MIXER_LOOKAHEAD < len(pieces):
                ahead.append(mixer_local(*pieces[idx + MIXER_LOOKAHEAD]))
            mixer_readout(*pieces[idx], ahead.pop(0))
        for r, (gate, up) in enumerate(gate_up):
            act = (gate * _sigmoid(gate) * up).astype(BF16)
            part = _dot(act, wdn_ref[f0:f0 + width, :])
            acc[r] = part if acc[r] is None else acc[r] + part
            if j + 1 == len(FF_CHUNKS):
                x2 = x1[r] + gt2_ref[...] * acc[r]
                y_ref[blocks[r], :] = _rms(x2) * gf_ref[...]
        f0 += width

    if with_mixer:
        for hd in range(HEADS):
            cta_ref[hd] = state[hd][0]
            m_ref[hd] = jnp.broadcast_to(state[hd][1], m_ref.shape[1:])


def _prompt_tail(a, kt, b, gates, x, mod, bias_row, g_head, ws_tril, bs_rep, w_out, g2, w_gu, w_dn, g_final):
    groups, t, n_a = a.shape
    tm = MIX_TILE
    tps = t // tm
    n_tiles = groups * tps
    assert sum(FF_CHUNKS) == D_FF

    def cur(i):
        return jnp.minimum(i, n_tiles - 1)

    def prev(i):
        return jnp.maximum(i - 1, 0)

    def rows(tile, width):
        return pl.BlockSpec((None, tm, width), lambda i: (tile(i) // tps, tile(i) % tps, 0))

    def mod_piece(piece):
        return pl.BlockSpec((None, 1, D_MODEL), lambda i: (prev(i) // tps, 0, piece))

    def const(shape):
        nd = len(shape)
        return pl.BlockSpec(shape, lambda i: (0,) * nd, pipeline_mode=pl.Buffered(1))

    def per_seq(shape):
        nd = len(shape)
        return pl.BlockSpec((None,) + shape, lambda i: (cur(i) // tps,) + (0,) * nd)

    return pl.pallas_call(
        functools.partial(_prompt_tail_kernel, tiles_per_seq=tps),
        grid=(n_tiles + 1,),
        in_specs=[
            rows(cur, n_a),
            pl.BlockSpec((None, GROUP_WIDTH, tm), lambda i: (cur(i) // tps, 0, cur(i) % tps)),
            rows(cur, N_B),
            pl.BlockSpec((None, GATE_ROWS, tm), lambda i: (cur(i) // tps, 0, cur(i) % tps)),
            rows(prev, D_MODEL),
            mod_piece(2), mod_piece(3), mod_piece(4), mod_piece(5),
            const((2 * HEADS, LANES)),
            const((1, GROUP_WIDTH)),
            const((HEADS, CHUNK, CHUNK)),
            const((HEADS, CHUNK, LANES)),
            const((D_MODEL, D_MODEL)),
            const((1, D_MODEL)),
            const((D_MODEL, 2 * D_FF)),
            const((D_FF, D_MODEL)),
            const((1, D_MODEL)),
        ],
        out_specs=[
            rows(prev, D_MODEL),
            per_seq((HEADS, HEAD_DIM, HEAD_DIM)),
            per_seq((HEADS, HEAD_DIM)),
            per_seq((HEADS, LANES)),
        ],
        out_shape=[
            jax.ShapeDtypeStruct((groups, t, D_MODEL), F32),
            jax.ShapeDtypeStruct((groups, HEADS, HEAD_DIM, HEAD_DIM), F32),
            jax.ShapeDtypeStruct((groups, HEADS, HEAD_DIM), F32),
            jax.ShapeDtypeStruct((groups, HEADS, LANES), F32),
        ],
        scratch_shapes=[
            pltpu.VMEM((tm, D_MODEL), BF16),
            pltpu.VMEM((HEADS, HEAD_DIM, 2 * HEAD_DIM), F32),
            pltpu.VMEM((HEADS, 8, LANES), F32),
        ],
        compiler_params=pltpu.CompilerParams(
            dimension_semantics=("arbitrary",), vmem_limit_bytes=VMEM_LIMIT),
        name="prompt_tail",
    )(a, kt, b, gates, x, mod, mod, mod, mod, bias_row, g_head, ws_tril, bs_rep, w_out, g2, w_gu, w_dn, g_final)


SAMPLE_TOKENS_PER_STEP = 128
SAMPLE_UNROLL = 8


def _mix_sample_kernel(q_ref, k_ref, v_ref, vsn_ref, og_ref, ug_ref, g_ref, m0_ref, n0_ref, c_ref,
                       bias_ref, gh_ref, ws0_ref, bs0_ref,
                       mix_ref, c_out, n_out, m_out,
                       vt_ref, kp_ref, wd_ref, cqt_ref):
    hd = pl.program_id(0)
    grp = pl.program_id(1)
    nb = q_ref.shape[0]
    lane = lax.broadcasted_iota(jnp.int32, (nb, LANES), 1)

    def gate_terms():
        pre = g_ref[...] + bias_ref[...]
        i_pre = jnp.sum(jnp.where(lane == hd, pre, 0.0), axis=1, keepdims=True)
        f_pre = jnp.sum(jnp.where(lane == hd + HEADS, pre, 0.0), axis=1, keepdims=True)
        m_prev = jnp.sum(jnp.where(lane == hd, m0_ref[...], 0.0), axis=1, keepdims=True)
        inter = _log_sigmoid(f_pre) + m_prev
        m_t = jnp.maximum(inter, i_pre)
        return m_t, jnp.exp(i_pre - m_t), jnp.exp(inter - m_t)

    @pl.when(grp == 0)
    def _():
        _, w_in, w_dec = gate_terms()
        vt_ref[...] = v_ref[...].T
        kp_ref[...] = (w_in * k_ref[...]).astype(BF16)
        wd_ref[...] = jnp.broadcast_to(w_dec, wd_ref.shape)
        cqt_ref[...] = jnp.zeros_like(cqt_ref)

    lane_sq = lax.broadcasted_iota(jnp.int32, (HEAD_DIM, LANES), 1)
    tokens = c_ref.shape[0]

    def body(r, cqt):
        tok = grp * tokens + r
        q_row = q_ref[pl.ds(tok, 1), :]
        c_b = c_ref[r]
        sel = lane_sq == tok
        cq_col = jnp.sum(c_b * q_row, axis=1, keepdims=True)
        v_sel = jnp.where(sel, vt_ref[...], 0.0).astype(BF16)
        outer = _dot(v_sel, kp_ref[...])
        c_out[r] = wd_ref[pl.ds(tok, 1), :] * c_b + outer
        return jnp.where(sel, cq_col, cqt)

    cqt_ref[...] = lax.fori_loop(0, tokens, body, cqt_ref[...], unroll=SAMPLE_UNROLL)

    @pl.when(grp == pl.num_programs(1) - 1)
    def _():
        m_t, w_in, w_dec = gate_terms()
        q = q_ref[...]
        k = k_ref[...]
        v = v_ref[...]
        n0 = n0_ref[...]
        cq = cqt_ref[...].T
        s = jnp.sum(q * k, axis=1, keepdims=True) * (QK_SCALE * w_in)
        w_inter = w_dec * QK_SCALE
        num = s * v + w_inter * cq
        den = s + w_inter * jnp.sum(n0 * q, axis=1, keepdims=True)
        hh = num / jnp.maximum(jnp.abs(den), jnp.exp(-m_t))
        ml = _rms(hh) * gh_ref[...] * og_ref[...]
        cm = ug_ref[...] * (ws0_ref[...] * vsn_ref[...] + bs0_ref[...])
        n_out[...] = w_dec * n0 + w_in * k
        m_out[...] = jnp.broadcast_to(m_t, m_out.shape)
        for hh_static in range(HEADS):
            @pl.when(hd == hh_static)
            def _():
                mix_ref[:, hh_static * HEAD_DIM: (hh_static + 1) * HEAD_DIM] = ml
                mix_ref[:, GROUP_WIDTH + hh_static * HEAD_DIM: GROUP_WIDTH + (hh_static + 1) * HEAD_DIM] = cm


def _mix_sample(a, k, b, gates, m0_pad, n0, c0, bias_row, g_head, ws0_row, bs0_row):
    nb = a.shape[0]
    tb = SAMPLE_TOKENS_PER_STEP

    def head_block(offset):
        return pl.BlockSpec((nb, HEAD_DIM), lambda h, g: (0, offset + h))

    full = pl.BlockSpec((nb, LANES), lambda h, g: (0, 0))
    head_row = pl.BlockSpec((1, HEAD_DIM), lambda h, g: (0, h))
    c_spec = pl.BlockSpec((tb, None, HEAD_DIM, HEAD_DIM), lambda h, g: (g, h, 0, 0))
    return pl.pallas_call(
        _mix_sample_kernel,
        grid=(HEADS, nb // tb),
        in_specs=[
            head_block(0), head_block(0), head_block(HEADS), head_block(2 * HEADS),
            head_block(0), head_block(HEADS),
            full, full, head_block(0), c_spec,
            pl.BlockSpec((1, LANES), lambda h, g: (0, 0)),
            head_row, head_row, head_row,
        ],
        out_specs=[
            pl.BlockSpec((nb, D_MODEL), lambda h, g: (0, 0)),
            c_spec,
            head_block(0),
            head_block(0),
        ],
        out_shape=[
            jax.ShapeDtypeStruct((nb, D_MODEL), F32),
            jax.ShapeDtypeStruct(c0.shape, F32),
            jax.ShapeDtypeStruct((nb, GROUP_WIDTH), F32),
            jax.ShapeDtypeStruct((nb, GROUP_WIDTH), F32),
        ],
        scratch_shapes=[
            pltpu.VMEM((HEAD_DIM, nb), F32),
            pltpu.VMEM((nb, HEAD_DIM), BF16),
            pltpu.VMEM((nb, LANES), F32),
            pltpu.VMEM((HEAD_DIM, nb), F32),
        ],
        compiler_params=pltpu.CompilerParams(
            dimension_semantics=("arbitrary", "arbitrary"), vmem_limit_bytes=VMEM_LIMIT),
        name="mix_sample",
    )(a, k, a, a, b, b, gates, m0_pad, n0, c0, bias_row, g_head, ws0_row, bs0_row)


def _out_kernel(x_ref, mix_ref, gt1_ref, sh2_ref, sc2_ref, gt2_ref, wo_ref, g2_ref, wgu_ref, wdn_ref, gf_ref,
                y_ref):
    x1 = x_ref[...] + gt1_ref[...] * _dot(mix_ref[...].astype(BF16), wo_ref[...])
    h2 = _rms(x1) * g2_ref[...]
    h2 = (h2 * (1.0 + sc2_ref[...]) + sh2_ref[...]).astype(BF16)
    gate = _dot(h2, wgu_ref[:, :D_FF])
    up = _dot(h2, wgu_ref[:, D_FF:])
    act = (gate * _sigmoid(gate) * up).astype(BF16)
    x2 = x1 + gt2_ref[...] * _dot(act, wdn_ref[...])
    y_ref[:, 0, :] = _rms(x2) * gf_ref[...]


def _output_stage(x, mix, mod, w_out, g2, w_gu, w_dn, g_final, *, tm, per_row):
    groups, t, _ = x.shape
    return pl.pallas_call(
        _out_kernel,
        grid=(groups, t // tm),
        in_specs=[
            pl.BlockSpec((None, tm, D_MODEL), lambda g, i: (g, i, 0)),
            pl.BlockSpec((None, tm, D_MODEL), lambda g, i: (g, i, 0)),
            _mod_spec(2, tm, per_row),
            _mod_spec(3, tm, per_row),
            _mod_spec(4, tm, per_row),
            _mod_spec(5, tm, per_row),
            _const_spec((D_MODEL, D_MODEL)),
            _const_spec((1, D_MODEL)),
            _const_spec((D_MODEL, 2 * D_FF)),
            _const_spec((D_FF, D_MODEL)),
            _const_spec((1, D_MODEL)),
        ],
        out_specs=pl.BlockSpec((tm, 1, D_MODEL), lambda g, i: (g * (t // tm) + i, 0, 0)),
        out_shape=jax.ShapeDtypeStruct((groups * t, 1, D_MODEL), F32),
        compiler_params=pltpu.CompilerParams(
            dimension_semantics=("arbitrary", "arbitrary"), vmem_limit_bytes=VMEM_LIMIT),
        name="output_stage",
    )(x, mix, mod, mod, mod, mod, w_out, g2, w_gu, w_dn, g_final)


def kernel(x_prompt, x_sample, c_prompt, c_sample, state_mlstm_C, state_mlstm_n, state_mlstm_m, w_ada, b_ada, g_norm1, w_in, b_gate, g_mlstm_head, ln_v_g, ln_v_b, w_s, b_s, w_out, g_norm2, w_gate_up, w_down, g_final):
    depth = w_ada.shape[0]
    assert depth == 1, "single-layer trunk"
    batch, seq, _ = x_prompt.shape
    nb = x_sample.shape[0]
    assert x_sample.shape[1] == 1

    gw = GROUP_WIDTH
    g1 = g_norm1[0].reshape(1, D_MODEL)
    g2 = g_norm2[0].reshape(1, D_MODEL)
    gf = g_final.reshape(1, D_MODEL)
    ln_g = ln_v_g[0].reshape(1, gw)
    ln_b = ln_v_b[0].reshape(1, gw)
    g_head = g_mlstm_head[0].reshape(1, gw)
    bias_row = jnp.pad(b_gate[0], (0, LANES - 2 * HEADS)).reshape(1, LANES)
    bias_rep = jnp.broadcast_to(b_gate[0][:, None], (2 * HEADS, LANES))
    tril =jnp.tril(jnp.ones((CHUNK, CHUNK), dtype=bool))
    ws_tril = jnp.where(tril[None], w_s[0], 0.0).astype(BF16)
    bs_rep = jnp.broadcast_to(b_s[0][:, :, None], (HEADS, CHUNK, LANES))
    ws0_row = jnp.repeat(w_s[0][:, 0, 0], HEAD_DIM).reshape(1, gw)
    bs0_row = jnp.repeat(b_s[0][:, 0], HEAD_DIM).reshape(1, gw)

    mod_s, mod_p, w_main, w_gates, w_k, w_kt = _modulation(c_sample, c_prompt, w_ada[0], b_ada[0].reshape(1, -1), w_in[0].T)
    mod_s = mod_s.reshape(1, nb, N_MOD * D_MODEL)

    a_p, b_p, g_p, kt_p, w_out_b, w_gu_b, w_dn_b = _project(
        x_prompt, mod_p, g1, w_main, w_gates, ln_g, ln_b, w_kt, (w_out[0], w_gate_up[0], w_down[0]),
        k_transposed=True, tm=PROJ_TILE, per_row=False, a_dtype=BF16)
    y_p, c_p, n_p, m_p = _prompt_tail(a_p, kt_p, b_p, g_p, x_prompt, mod_p, bias_rep, g_head, ws_tril, bs_rep,
                                      w_out_b, g2, w_gu_b, w_dn_b, gf)

    xs = x_sample.reshape(1, nb, D_MODEL)
    a_s, b_s_act, g_s, k_s = _project(xs, mod_s, g1, w_main, w_gates, ln_g, ln_b, w_k, k_transposed=False,
                                      tm=nb, per_row=True, a_dtype=F32)
    a_s2, b_s2, g_s2 = a_s[0], b_s_act[0], g_s[0]
    m0_pad = jnp.pad(state_mlstm_m[0], ((0, 0), (0, LANES - HEADS)))
    n0 = state_mlstm_n[0].reshape(nb, gw)
    mix_s, c_s, n_s, m_s = _mix_sample(a_s2, k_s[0], b_s2, g_s2, m0_pad, n0, state_mlstm_C[0],
                                       bias_row, g_head, ws0_row, bs0_row)
    y_s = _output_stage(xs, mix_s.reshape(1, nb, D_MODEL), mod_s, w_out_b, g2, w_gu_b, w_dn_b, gf,
                        tm=nb, per_row=True)

    return (
        y_p,
        y_s,
        c_p[None],
        n_p[None],
        m_p[:, :, 0][None],
        c_s[None],
        n_s.reshape(nb, HEADS, HEAD_DIM)[None],
        m_s.reshape(nb, HEADS, HEAD_DIM)[:, :, 0][None],
        a_s2[:, 2 * gw:].reshape(nb, 1, HEADS, HEAD_DIM)[None],
    )
```

```python
import functools
import math

import jax
import jax.numpy as jnp
from jax import lax
from jax.experimental import pallas as pl
from jax.experimental.pallas import tpu as pltpu

F32 = jnp.float32
BF16 = jnp.bfloat16

D_MODEL = 1024
HEADS = 4
HEAD_DIM = 128
GROUP_WIDTH = HEADS * HEAD_DIM
CHUNK = 128
D_FF = 2816
N_MOD = 6
EPS = 1e-6
QK_SCALE = HEAD_DIM ** -0.5
LANES = 128
F32_SUBLANES = 8
BF16_SUBLANES = 16

N_B = 2 * GROUP_WIDTH

VMEM_LIMIT = 56 * 1024 * 1024


def _dot(a, b):
    return jnp.dot(a, b, preferred_element_type=F32)


def _dot_nt(a, b):
    return lax.dot_general(a, b, (((1,), (1,)), ((), ())), preferred_element_type=F32)


def _sigmoid(x):
    return 1.0 / (1.0 + jnp.exp(-x))


def _gelu_tanh(x):
    c = math.sqrt(2.0 / math.pi)
    return x * (0.5 * (1.0 + jnp.tanh(c * (x + 0.044715 * (x * x * x)))))


def _log_sigmoid(x):
    return jnp.minimum(x, 0.0) - jnp.log1p(jnp.exp(-jnp.abs(x)))


def _rms(x):
    return x * lax.rsqrt(jnp.mean(x * x, axis=-1, keepdims=True) + EPS)


def _split3_bf16(x):
    hi = x.astype(BF16)
    r1 = x - hi.astype(F32)
    mid = r1.astype(BF16)
    lo = (r1 - mid.astype(F32)).astype(BF16)
    return hi, mid, lo


MOD_TILE = 1024

IN_Q, IN_K, IN_V, IN_O, IN_GATES = 0, GROUP_WIDTH, 2 * GROUP_WIDTH, 3 * GROUP_WIDTH, 4 * GROUP_WIDTH
IN_U = IN_GATES + 2 * HEADS
IN_VS = IN_U + GROUP_WIDTH
N_IN = IN_VS + GROUP_WIDTH
MAIN_GROUPS = (IN_Q, IN_V, IN_VS, IN_O, IN_U)
N_MAIN = len(MAIN_GROUPS) * GROUP_WIDTH
GATE_ROWS = BF16_SUBLANES


def _mod_kernel(cs_ref, cp_ref, w_ref, b_ref, grp_ref, gate_ref, os_ref, op_ref, wmain_ref, wgate_ref, wk_ref, wkt_ref):
    step = pl.program_id(0)
    ns, npr = cs_ref.shape[0], cp_ref.shape[0]
    pad = -(ns + npr) % BF16_SUBLANES
    c = jnp.concatenate([cs_ref[...], cp_ref[...], jnp.zeros((pad, D_MODEL), F32)], axis=0)
    a = (c * _sigmoid(c)).astype(BF16)
    res = _dot(a, w_ref[...].astype(BF16)) + b_ref[...]
    os_ref[...] = res[:ns]
    op_ref[:, 0, :] = res[ns:ns + npr]

    def transposed_into(dst_ref):
        for blk in range(GROUP_WIDTH // LANES):
            rows = grp_ref[blk * LANES:(blk + 1) * LANES, :]
            dst_ref[:, blk * LANES:(blk + 1) * LANES] = rows.T.astype(BF16)

    @pl.when(step < len(MAIN_GROUPS))
    def _():
        transposed_into(wmain_ref)

    @pl.when(step == len(MAIN_GROUPS))
    def _():
        wkt_ref[:GROUP_WIDTH, :] = grp_ref[...].astype(BF16)
        wkt_ref[GROUP_WIDTH:, :] = jnp.concatenate(
            [gate_ref[...], jnp.zeros((GATE_ROWS - 2 * HEADS, D_MODEL), F32)], axis=0).astype(BF16)
        transposed_into(wk_ref)

    @pl.when(step == 0)
    def _():
        gate_rows = jnp.concatenate([gate_ref[...], jnp.zeros((LANES - 2 * HEADS, D_MODEL), F32)], axis=0)
        wgate_ref[...] = gate_rows.T.astype(BF16)


def _modulation(c_sample, c_prompt, w_ada, b_ada, w_in_t):
    ns, npr = c_sample.shape[0], c_prompt.shape[0]
    tn = MOD_TILE
    steps = N_MOD * D_MODEL // tn
    group_starts = MAIN_GROUPS + (IN_K,)
    assert w_in_t.shape == (N_IN, D_MODEL) and steps == len(group_starts)

    def whole(shape):
        return pl.BlockSpec(shape, lambda j: (0, 0))

    def group_start(j):
        assert all(start % F32_SUBLANES == 0 for start in group_starts)
        return pl.multiple_of(sum(jnp.where(j == i, start, 0) for i, start in enumerate(group_starts)), F32_SUBLANES)

    return pl.pallas_call(
        _mod_kernel,
        grid=(steps,),
        in_specs=[
            whole((ns, D_MODEL)),
            whole((npr, D_MODEL)),
            pl.BlockSpec((D_MODEL, tn), lambda j: (0, j)),
            pl.BlockSpec((1, tn), lambda j: (0, j)),
            pl.BlockSpec((pl.Element(GROUP_WIDTH), pl.Element(D_MODEL)), lambda j: (group_start(j), 0)),
            pl.BlockSpec((pl.Element(2 * HEADS), pl.Element(D_MODEL)), lambda j: (IN_GATES, 0)),
        ],
        out_specs=[
            pl.BlockSpec((ns, tn), lambda j: (0, j)),
            pl.BlockSpec((npr, 1, tn), lambda j: (0, 0, j)),
            pl.BlockSpec((D_MODEL, GROUP_WIDTH), lambda j: (0, jnp.minimum(j, len(MAIN_GROUPS) - 1))),
            whole((D_MODEL, LANES)),
            whole((D_MODEL, GROUP_WIDTH)),
            whole((GROUP_WIDTH + GATE_ROWS, D_MODEL)),
        ],
        out_shape=[
            jax.ShapeDtypeStruct((ns, N_MOD * D_MODEL), F32),
            jax.ShapeDtypeStruct((npr, 1, N_MOD * D_MODEL), F32),
            jax.ShapeDtypeStruct((D_MODEL, N_MAIN), BF16),
            jax.ShapeDtypeStruct((D_MODEL, LANES), BF16),
            jax.ShapeDtypeStruct((D_MODEL, GROUP_WIDTH), BF16),
            jax.ShapeDtypeStruct((GROUP_WIDTH + GATE_ROWS, D_MODEL), BF16),
        ],
        compiler_params=pltpu.CompilerParams(dimension_semantics=("arbitrary",), vmem_limit_bytes=VMEM_LIMIT),
        name="modulation",
    )(c_sample, c_prompt, w_ada, b_ada, w_in_t, w_in_t)


def _mod_spec(piece, tm, per_row):
    if per_row:
        return pl.BlockSpec((None, tm, D_MODEL), lambda g, t: (g, t, piece))
    return pl.BlockSpec((None, 1, D_MODEL), lambda g, t: (g, 0, piece))


def _const_spec(shape):
    nd = len(shape)
    return pl.BlockSpec(shape, lambda g, t: (0,) * nd, pipeline_mode=pl.Buffered(1))


PROJ_TILE = 1024


def _proj_body(x_ref, sh_ref, sc_ref, g1_ref, w_ref, wg_ref, lng_ref, lnb_ref, a_ref, b_ref, g_ref):
    n_a = a_ref.shape[1]
    n_copy = n_a - GROUP_WIDTH
    x = x_ref[...]
    h = _rms(x) * g1_ref[...]
    h = (h * (1.0 + sc_ref[...]) + sh_ref[...]).astype(BF16)
    p = _dot(h, w_ref[...])
    a_ref[:, :n_copy] = p[:, :n_copy].astype(a_ref.dtype)
    vs = _gelu_tanh(p[:, n_copy:n_a])
    for hd in range(HEADS):
        sl = slice(hd * HEAD_DIM, (hd + 1) * HEAD_DIM)
        v = vs[:, sl]
        mu = jnp.mean(v, axis=-1, keepdims=True)
        vc = v - mu
        var = jnp.mean(vc * vc, axis=-1, keepdims=True)
        y = vc * lax.rsqrt(var + EPS) * lng_ref[:, sl] + lnb_ref[:, sl]
        a_ref[:, n_copy + hd * HEAD_DIM: n_copy + (hd + 1) * HEAD_DIM] = y.astype(a_ref.dtype)
    b_ref[:, :GROUP_WIDTH] = _sigmoid(p[:, n_a: n_a + GROUP_WIDTH])
    b_ref[:, GROUP_WIDTH:] = _gelu_tanh(p[:, n_a + GROUP_WIDTH: n_a + N_B])
    if wg_ref is not None:
        g_ref[...] = _dot(h, wg_ref[...])
    return h


def _proj_kernel(*refs, k_transposed, n_cast):
    ins, outs = refs[:9 + n_cast], refs[9 + n_cast:]
    wk_ref, k_ref = ins[8], outs[3]
    if k_transposed:
        h = _proj_body(*ins[:5], None, *ins[6:8], *outs[:3])
        kt_gates = _dot_nt(wk_ref[...], h)
        k_ref[...] = kt_gates[:GROUP_WIDTH].astype(k_ref.dtype)
        outs[2][...] = kt_gates[GROUP_WIDTH:]
    else:
        h = _proj_body(*ins[:8], *outs[:3])
        k_ref[...] = _dot(h, wk_ref[...]).astype(k_ref.dtype)
    for src, dst in zip(ins[9:], outs[4:]):
        dst[...] = src[...].astype(dst.dtype)


def _project(x, mod, g1, w_main, w_gates, ln_g, ln_b, w_k, cast_weights=(), *, k_transposed, tm, per_row, a_dtype):
    groups, t, _ = x.shape
    n_a = w_main.shape[1] - N_B
    steps_per_group = t // tm
    n_steps = groups * steps_per_group
    in_specs = [
        pl.BlockSpec((None, tm, D_MODEL), lambda g, i: (g, i, 0)),
        _mod_spec(0, tm, per_row),
        _mod_spec(1, tm, per_row),
        _const_spec((1, D_MODEL)),
        _const_spec(w_main.shape),
        _const_spec(w_gates.shape),
        _const_spec((1, GROUP_WIDTH)),
        _const_spec((1, GROUP_WIDTH)),
    ]
    out_specs = [
        pl.BlockSpec((None, tm, n_a), lambda g, i: (g, i, 0)),
        pl.BlockSpec((None, tm, N_B), lambda g, i: (g, i, 0)),
    ]
    out_shape = [
        jax.ShapeDtypeStruct((groups, t, n_a), a_dtype),
        jax.ShapeDtypeStruct((groups, t, N_B), F32),
    ]
    if k_transposed:
        out_specs.append(pl.BlockSpec((None, GATE_ROWS, tm), lambda g, i: (g, 0, i)))
        out_shape.append(jax.ShapeDtypeStruct((groups, GATE_ROWS, t), F32))
    else:
        out_specs.append(pl.BlockSpec((None, tm, LANES), lambda g, i: (g, i, 0)))
        out_shape.append(jax.ShapeDtypeStruct((groups, t, LANES), F32))
    args = [x, mod, mod, g1, w_main, w_gates, ln_g, ln_b, w_k]
    in_specs.append(_const_spec(w_k.shape))
    if k_transposed:
        out_specs.append(pl.BlockSpec((None, GROUP_WIDTH, tm), lambda g, i: (g, 0, i)))
        out_shape.append(jax.ShapeDtypeStruct((groups, GROUP_WIDTH, t), a_dtype))
    else:
        out_specs.append(pl.BlockSpec((None, tm, GROUP_WIDTH), lambda g, i: (g, i, 0)))
        out_shape.append(jax.ShapeDtypeStruct((groups, t, GROUP_WIDTH), a_dtype))
    for w in cast_weights:
        rows, cols = w.shape
        assert rows % (n_steps * 16) == 0, "row block must be a whole number of bf16 sublane tiles"
        spec = pl.BlockSpec((rows // n_steps, cols), lambda g, i: (g * steps_per_group + i, 0))
        in_specs.append(spec)
        out_specs.append(spec)
        out_shape.append(jax.ShapeDtypeStruct((rows, cols), BF16))
        args.append(w)
    return pl.pallas_call(
        functools.partial(_proj_kernel, k_transposed=k_transposed, n_cast=len(cast_weights)),
        grid=(groups, t // tm),
        in_specs=in_specs,
        out_specs=out_specs,
        out_shape=out_shape,
        compiler_params=pltpu.CompilerParams(
            dimension_semantics=("arbitrary", "arbitrary"), vmem_limit_bytes=VMEM_LIMIT),
        name="project",
    )(*args)


MIX_TILE = 512


def _dot3_rhs(lhs_bf16, rhs_f32):
    hi, mid, lo = _split3_bf16(rhs_f32)
    return _dot(lhs_bf16, hi) + _dot(lhs_bf16, mid) + _dot(lhs_bf16, lo)


def _dot3_lhs(lhs_f32, rhs_bf16):
    hi, mid, lo = _split3_bf16(lhs_f32)
    return _dot(hi, rhs_bf16) + _dot(mid, rhs_bf16) + _dot(lo, rhs_bf16)


FF_CHUNKS = (768, 768, 768, 512)
MIXER_LOOKAHEAD = 2
OUT_ROW_BLOCKS = 2


def _gate_rows(g_ref, bias_ref, tril, triu, n_chunks):
    n_gates = 2 * HEADS
    rows = [g_ref[0:n_gates, c * CHUNK:(c + 1) * CHUNK] + bias_ref[...] for c in range(n_chunks)]
    pad = jnp.zeros((LANES - n_gates, CHUNK), F32)
    cols = [jnp.concatenate([r, pad], axis=0).T for r in rows]
    bc_all = _dot3_rhs(tril, jnp.concatenate([_log_sigmoid(p) for p in cols], axis=1))
    rows_all = jnp.concatenate(rows, axis=0)
    b_rows_all = _dot3_lhs(_log_sigmoid(rows_all), triu)
    bc, arow, blast = [], [], []
    for c in range(n_chunks):
        bc.append(bc_all[:, c * LANES:(c + 1) * LANES])
        b_rows = b_rows_all[c * n_gates:(c + 1) * n_gates, :]
        arow.append([rows[c][hd:hd + 1, :] - b_rows[HEADS + hd:HEADS + hd + 1, :] for hd in range(HEADS)])
        blast.append([jnp.min(b_rows[HEADS + hd:HEADS + hd + 1, :], axis=1, keepdims=True) for hd in range(HEADS)])
    return bc, arow, blast


def _mlstm_local(q, kt, a_row, b_last, causal):
    amat = jnp.where(causal, a_row, -jnp.inf)
    m_row = jnp.max(amat, axis=1, keepdims=True)
    s_loc = (_dot(q, kt) * jnp.exp(amat - m_row)).astype(BF16)
    g_row = b_last + a_row
    g_loc = jnp.max(g_row, axis=1, keepdims=True)
    kw = (kt.astype(F32) * jnp.exp(g_row - g_loc)).astype(BF16)
    return m_row, s_loc, g_loc, kw


def _mlstm_readout(local, q, v, og, g_head, b_last, b_rep, cta, m_prev, ones_blk):
    m_row, s_loc, g_loc, kw = local
    va = jnp.concatenate([v, ones_blk], axis=1)
    nd_loc = _dot(s_loc, va)
    u_aug = _dot(kw, va)
    inter = _dot(q, cta.astype(BF16))
    mm = jnp.maximum(m_prev, m_row)
    f_loc = jnp.exp(m_row - mm) * QK_SCALE
    f_int = jnp.exp(m_prev - mm) * QK_SCALE
    nd = f_loc * nd_loc + f_int * inter
    clamp = jnp.exp(-(b_rep + mm))
    hh = nd[:, :HEAD_DIM] / jnp.maximum(jnp.abs(nd[:, HEAD_DIM:]), clamp)
    ml = _rms(hh) * g_head * og

    dec = b_last + m_prev
    m_new = jnp.maximum(dec, g_loc)
    cta_new = jnp.exp(dec - m_new) * cta + jnp.exp(g_loc - m_new) * u_aug
    return ml, cta_new, m_new


def _prompt_tail_kernel(a_ref, kt_ref, b_ref, g_ref, x_ref, gt1_ref, sh2_ref, sc2_ref, gt2_ref,
                        bias_ref, gh_ref, ws_ref, bs_ref, wo_ref, g2_ref, wgu_ref, wdn_ref, gf_ref,
                        y_ref, c_out, n_out, m_out,
                        mix_ref, cta_ref, m_ref, *, tiles_per_seq):
    step = pl.program_id(0)
    n_tiles = pl.num_programs(0) - 1
    refs = (a_ref, kt_ref, b_ref, g_ref, x_ref, gt1_ref, sh2_ref, sc2_ref, gt2_ref,
            bias_ref, gh_ref, ws_ref, bs_ref, wo_ref, g2_ref, wgu_ref, wdn_ref, gf_ref,
            y_ref, mix_ref, cta_ref, m_ref)

    @pl.when(step % tiles_per_seq == 0)
    def _():
        cta_ref[...] = jnp.zeros_like(cta_ref)
        m_ref[...] = jnp.zeros_like(m_ref)

    @pl.when(step == 0)
    def _():
        _prompt_tail_body(refs, with_mixer=True, with_out=False)

    @pl.when(jnp.logical_and(step > 0, step < n_tiles))
    def _():
        _prompt_tail_body(refs, with_mixer=True, with_out=True)

    @pl.when(step == n_tiles)
    def _():
        _prompt_tail_body(refs, with_mixer=False, with_out=True)

    @pl.when(jnp.logical_and(step % tiles_per_seq == tiles_per_seq - 1, step < n_tiles))
    def _():
        for hd in range(HEADS):
            cta = cta_ref[hd]
            c_out[hd] = cta[:, :HEAD_DIM].T
            n_out[hd: hd + 1, :] = cta[:, HEAD_DIM:].T[0:1, :]
            m_out[hd: hd + 1, :] = m_ref[hd][0:1, :]


def _prompt_tail_body(refs, *, with_mixer, with_out):
    (a_ref, kt_ref, b_ref, g_ref, x_ref, gt1_ref, sh2_ref, sc2_ref, gt2_ref,
     bias_ref, gh_ref, ws_ref, bs_ref, wo_ref, g2_ref, wgu_ref, wdn_ref, gf_ref,
     y_ref, mix_ref, cta_ref, m_ref) = refs
    tm = a_ref.shape[0]
    n_chunks = tm // CHUNK
    n_copy = a_ref.shape[1] - GROUP_WIDTH

    row = lax.broadcasted_iota(jnp.int32, (CHUNK, CHUNK), 0)
    col = lax.broadcasted_iota(jnp.int32, (CHUNK, CHUNK), 1)
    causal = row >= col
    tril = jnp.where(causal, 1.0, 0.0).astype(BF16)
    triu = jnp.where(row <= col, 1.0, 0.0).astype(BF16)
    ones_blk = jnp.ones((CHUNK, HEAD_DIM), BF16)

    blocks = [slice(r0, r0 + tm // OUT_ROW_BLOCKS) for r0 in range(0, tm, tm // OUT_ROW_BLOCKS)] if with_out else []
    outproj = [_dot(mix_ref[rs, :], wo_ref[...]) for rs in blocks]
    x1, h2 = [], []
    for rs, op in zip(blocks, outproj):
        x1.append(x_ref[rs, :] + gt1_ref[...] * op)
        h = _rms(x1[-1]) * g2_ref[...]
        h2.append((h * (1.0 + sc2_ref[...]) + sh2_ref[...]).astype(BF16))

    if with_mixer:
        bc, arow, blast = _gate_rows(g_ref, bias_ref, tril, triu, n_chunks)
        state = [(cta_ref[hd], jnp.max(m_ref[hd][0:1, :], axis=1, keepdims=True)) for hd in range(HEADS)]
    sg_all = [None] * HEADS

    def mixer_local(c, hd):
        rs = slice(c * CHUNK, (c + 1) * CHUNK)
        sl = slice(hd * HEAD_DIM, (hd + 1) * HEAD_DIM)
        return _mlstm_local(a_ref[rs, sl], kt_ref[sl, rs], arow[c][hd], blast[c][hd], causal)

    def mixer_readout(c, hd, local):
        rs = slice(c * CHUNK, (c + 1) * CHUNK)
        sl = slice(hd * HEAD_DIM, (hd + 1) * HEAD_DIM)
        if c == 0:
            vsn_all = jnp.concatenate(
                [a_ref[cc * CHUNK:(cc + 1) * CHUNK, n_copy + hd * HEAD_DIM: n_copy + (hd + 1) * HEAD_DIM]
                 for cc in range(n_chunks)], axis=1)
            sg_all[hd] = _dot(ws_ref[hd], vsn_all)
        cta, m_prev = state[hd]
        b_rep = jnp.broadcast_to(bc[c][:, HEADS + hd: HEADS + hd + 1], (CHUNK, HEAD_DIM))
        ml, cta, m_prev = _mlstm_readout(
            local, a_ref[rs, sl], a_ref[rs, GROUP_WIDTH + hd * HEAD_DIM: GROUP_WIDTH + (hd + 1) * HEAD_DIM],
            b_ref[rs, sl], gh_ref[:, sl], blast[c][hd], b_rep, cta, m_prev, ones_blk)
        state[hd] = (cta, m_prev)
        mix_ref[rs, sl] = ml.astype(mix_ref.dtype)
        sg = sg_all[hd][:, c * CHUNK:(c + 1) * CHUNK] + bs_ref[hd]
        ug = b_ref[rs, GROUP_WIDTH + hd * HEAD_DIM: GROUP_WIDTH + (hd + 1) * HEAD_DIM]
        mix_ref[rs, GROUP_WIDTH + hd * HEAD_DIM: GROUP_WIDTH + (hd + 1) * HEAD_DIM] = (ug * sg).astype(mix_ref.dtype)

    pieces = [(c, hd) for c in range(n_chunks) for hd in range(HEADS)] if with_mixer else []
    per_ff = -(-len(pieces) // len(FF_CHUNKS))
    ahead = [mixer_local(*pieces[p]) for p in range(min(MIXER_LOOKAHEAD, len(pieces)))]
    acc = [None] * len(blocks)
    f0 = 0
    for j, width in enumerate(FF_CHUNKS):
        gate_up = [(_dot(h, wgu_ref[:, f0:f0 + width]), _dot(h, wgu_ref[:, D_FF + f0:D_FF + f0 + width]))
                   for h in h2]
        for idx in range(j * per_ff, min((j + 1) * per_ff, len(pieces))):
            if idx + MIXER_LOOKAHEAD < len(pieces):
                ahead.append(mixer_local(*pieces[idx + MIXER_LOOKAHEAD]))
            mixer_readout(*pieces[idx], ahead.pop(0))
        for r, (gate, up) in enumerate(gate_up):
            act = (gate * _sigmoid(gate) * up).astype(BF16)
            part = _dot(act, wdn_ref[f0:f0 + width, :])
            acc[r] = part if acc[r] is None else acc[r] + part
            if j + 1 == len(FF_CHUNKS):
                x2 = x1[r] + gt2_ref[...] * acc[r]
                y_ref[blocks[r], :] = _rms(x2) * gf_ref[...]
        f0 += width

    if with_mixer:
        for hd in range(HEADS):
            cta_ref[hd] = state[hd][0]
            m_ref[hd] = jnp.broadcast_to(state[hd][1], m_ref.shape[1:])


def _prompt_tail(a, kt, b, gates, x, mod, bias_row, g_head, ws_tril, bs_rep, w_out, g2, w_gu, w_dn, g_final):
    groups, t, n_a = a.shape
    tm = MIX_TILE
    tps = t // tm
    n_tiles = groups * tps
    assert sum(FF_CHUNKS) == D_FF

    def cur(i):
        return jnp.minimum(i, n_tiles - 1)

    def prev(i):
        return jnp.maximum(i - 1, 0)

    def rows(tile, width):
        return pl.BlockSpec((None, tm, width), lambda i: (tile(i) // tps, tile(i) % tps, 0))

    def mod_piece(piece):
        return pl.BlockSpec((None, 1, D_MODEL), lambda i: (prev(i) // tps, 0, piece))

    def const(shape):
        nd = len(shape)
        return pl.BlockSpec(shape, lambda i: (0,) * nd, pipeline_mode=pl.Buffered(1))

    def per_seq(shape):
        nd = len(shape)
        return pl.BlockSpec((None,) + shape, lambda i: (cur(i) // tps,) + (0,) * nd)

    return pl.pallas_call(
        functools.partial(_prompt_tail_kernel, tiles_per_seq=tps),
        grid=(n_tiles + 1,),
        in_specs=[
            rows(cur, n_a),
            pl.BlockSpec((None, GROUP_WIDTH, tm), lambda i: (cur(i) // tps, 0, cur(i) % tps)),
            rows(cur, N_B),
            pl.BlockSpec((None, GATE_ROWS, tm), lambda i: (cur(i) // tps, 0, cur(i) % tps)),
            rows(prev, D_MODEL),
            mod_piece(2), mod_piece(3), mod_piece(4), mod_piece(5),
            const((2 * HEADS, LANES)),
            const((1, GROUP_WIDTH)),
            const((HEADS, CHUNK, CHUNK)),
            const((HEADS, CHUNK, LANES)),
            const((D_MODEL, D_MODEL)),
            const((1, D_MODEL)),
            const((D_MODEL, 2 * D_FF)),
            const((D_FF, D_MODEL)),
            const((1, D_MODEL)),
        ],
        out_specs=[
            rows(prev, D_MODEL),
            per_seq((HEADS, HEAD_DIM, HEAD_DIM)),
            per_seq((HEADS, HEAD_DIM)),
            per_seq((HEADS, LANES)),
        ],
        out_shape=[
            jax.ShapeDtypeStruct((groups, t, D_MODEL), F32),
            jax.ShapeDtypeStruct((groups, HEADS, HEAD_DIM, HEAD_DIM), F32),
            jax.ShapeDtypeStruct((groups, HEADS, HEAD_DIM), F32),
            jax.ShapeDtypeStruct((groups, HEADS, LANES), F32),
        ],
        scratch_shapes=[
            pltpu.VMEM((tm, D_MODEL), BF16),
            pltpu.VMEM((HEADS, HEAD_DIM, 2 * HEAD_DIM), F32),
            pltpu.VMEM((HEADS, F32_SUBLANES, LANES), F32),
        ],
        compiler_params=pltpu.CompilerParams(
            dimension_semantics=("arbitrary",), vmem_limit_bytes=VMEM_LIMIT),
        name="prompt_tail",
    )(a, kt, b, gates, x, mod, mod, mod, mod, bias_row, g_head, ws_tril, bs_rep, w_out, g2, w_gu, w_dn, g_final)


SAMPLE_TOKENS_PER_STEP = 128
SAMPLE_UNROLL = 8


def _mix_sample_kernel(q_ref, k_ref, v_ref, vsn_ref, og_ref, ug_ref, g_ref, m0_ref, n0_ref, c_ref,
                       bias_ref, gh_ref, ws0_ref, bs0_ref,
                       mix_ref, c_out, n_out, m_out,
                       vt_ref, kp_ref, wd_ref, cqt_ref):
    hd = pl.program_id(0)
    grp = pl.program_id(1)
    nb = q_ref.shape[0]
    lane = lax.broadcasted_iota(jnp.int32, (nb, LANES), 1)

    def gate_terms():
        pre = g_ref[...] + bias_ref[...]
        i_pre = jnp.sum(jnp.where(lane == hd, pre, 0.0), axis=1, keepdims=True)
        f_pre = jnp.sum(jnp.where(lane == hd + HEADS, pre, 0.0), axis=1, keepdims=True)
        m_prev = jnp.sum(jnp.where(lane == hd, m0_ref[...], 0.0), axis=1, keepdims=True)
        inter = _log_sigmoid(f_pre) + m_prev
        m_t = jnp.maximum(inter, i_pre)
        return m_t, jnp.exp(i_pre - m_t), jnp.exp(inter - m_t)

    @pl.when(grp == 0)
    def _():
        _, w_in, w_dec = gate_terms()
        vt_ref[...] = v_ref[...].T
        kp_ref[...] = (w_in * k_ref[...]).astype(BF16)
        wd_ref[...] = jnp.broadcast_to(w_dec, wd_ref.shape)
        cqt_ref[...] = jnp.zeros_like(cqt_ref)

    lane_sq = lax.broadcasted_iota(jnp.int32, (HEAD_DIM, LANES), 1)
    tokens = c_ref.shape[0]

    def body(r, cqt):
        tok = grp * tokens + r
        q_row = q_ref[pl.ds(tok, 1), :]
        c_b = c_ref[r]
        sel = lane_sq == tok
        cq_col = jnp.sum(c_b * q_row, axis=1, keepdims=True)
        v_sel = jnp.where(sel, vt_ref[...], 0.0).astype(BF16)
        outer = _dot(v_sel, kp_ref[...])
        c_out[r] = wd_ref[pl.ds(tok, 1), :] * c_b + outer
        return jnp.where(sel, cq_col, cqt)

    cqt_ref[...] = lax.fori_loop(0, tokens, body, cqt_ref[...], unroll=SAMPLE_UNROLL)

    @pl.when(grp == pl.num_programs(1) - 1)
    def _():
        m_t, w_in, w_dec = gate_terms()
        q = q_ref[...]
        k = k_ref[...]
        v = v_ref[...]
        n0 = n0_ref[...]
        cq = cqt_ref[...].T
        s = jnp.sum(q * k, axis=1, keepdims=True) * (QK_SCALE * w_in)
        w_inter = w_dec * QK_SCALE
        num = s * v + w_inter * cq
        den = s + w_inter * jnp.sum(n0 * q, axis=1, keepdims=True)
        hh = num / jnp.maximum(jnp.abs(den), jnp.exp(-m_t))
        ml = _rms(hh) * gh_ref[...] * og_ref[...]
        cm = ug_ref[...] * (ws0_ref[...] * vsn_ref[...] + bs0_ref[...])
        n_out[...] = w_dec * n0 + w_in * k
        m_out[...] = jnp.broadcast_to(m_t, m_out.shape)
        for hh_static in range(HEADS):
            @pl.when(hd == hh_static)
            def _():
                mix_ref[:, hh_static * HEAD_DIM: (hh_static + 1) * HEAD_DIM] = ml
                mix_ref[:, GROUP_WIDTH + hh_static * HEAD_DIM: GROUP_WIDTH + (hh_static + 1) * HEAD_DIM] = cm


def _mix_sample(a, k, b, gates, m0_pad, n0, c0, bias_row, g_head, ws0_row, bs0_row):
    nb = a.shape[0]
    tb = SAMPLE_TOKENS_PER_STEP

    def head_block(offset):
        return pl.BlockSpec((nb, HEAD_DIM), lambda h, g: (0, offset + h))

    full = pl.BlockSpec((nb, LANES), lambda h, g: (0, 0))
    head_row = pl.BlockSpec((1, HEAD_DIM), lambda h, g: (0, h))
    c_spec = pl.BlockSpec((tb, None, HEAD_DIM, HEAD_DIM), lambda h, g: (g, h, 0, 0))
    return pl.pallas_call(
        _mix_sample_kernel,
        grid=(HEADS, nb // tb),
        in_specs=[
            head_block(0), head_block(0), head_block(HEADS), head_block(2 * HEADS),
            head_block(0), head_block(HEADS),
            full, full, head_block(0), c_spec,
            pl.BlockSpec((1, LANES), lambda h, g: (0, 0)),
            head_row, head_row, head_row,
        ],
        out_specs=[
            pl.BlockSpec((nb, D_MODEL), lambda h, g: (0, 0)),
            c_spec,
            head_block(0),
            head_block(0),
        ],
        out_shape=[
            jax.ShapeDtypeStruct((nb, D_MODEL), F32),
            jax.ShapeDtypeStruct(c0.shape, F32),
            jax.ShapeDtypeStruct((nb, GROUP_WIDTH), F32),
            jax.ShapeDtypeStruct((nb, GROUP_WIDTH), F32),
        ],
        scratch_shapes=[
            pltpu.VMEM((HEAD_DIM, nb), F32),
            pltpu.VMEM((nb, HEAD_DIM), BF16),
            pltpu.VMEM((nb, LANES), F32),
            pltpu.VMEM((HEAD_DIM, nb), F32),
        ],
        compiler_params=pltpu.CompilerParams(
            dimension_semantics=("arbitrary", "arbitrary"), vmem_limit_bytes=VMEM_LIMIT),
        name="mix_sample",
    )(a, k, a, a, b, b, gates, m0_pad, n0, c0, bias_row, g_head, ws0_row, bs0_row)


def _out_kernel(x_ref, mix_ref, gt1_ref, sh2_ref, sc2_ref, gt2_ref, wo_ref, g2_ref, wgu_ref, wdn_ref, gf_ref,
                y_ref):
    x1 = x_ref[...] + gt1_ref[...] * _dot(mix_ref[...].astype(BF16), wo_ref[...])
    h2 = _rms(x1) * g2_ref[...]
    h2 = (h2 * (1.0 + sc2_ref[...]) + sh2_ref[...]).astype(BF16)
    gate = _dot(h2, wgu_ref[:, :D_FF])
    up = _dot(h2, wgu_ref[:, D_FF:])
    act = (gate * _sigmoid(gate) * up).astype(BF16)
    x2 = x1 + gt2_ref[...] * _dot(act, wdn_ref[...])
    y_ref[:, 0, :] = _rms(x2) * gf_ref[...]


def _output_stage(x, mix, mod, w_out, g2, w_gu, w_dn, g_final, *, tm, per_row):
    groups, t, _ = x.shape
    return pl.pallas_call(
        _out_kernel,
        grid=(groups, t // tm),
        in_specs=[
            pl.BlockSpec((None, tm, D_MODEL), lambda g, i: (g, i, 0)),
            pl.BlockSpec((None, tm, D_MODEL), lambda g, i: (g, i, 0)),
            _mod_spec(2, tm, per_row),
            _mod_spec(3, tm, per_row),
            _mod_spec(4, tm, per_row),
            _mod_spec(5, tm, per_row),
            _const_spec((D_MODEL, D_MODEL)),
            _const_spec((1, D_MODEL)),
            _const_spec((D_MODEL, 2 * D_FF)),
            _const_spec((D_FF, D_MODEL)),
            _const_spec((1, D_MODEL)),
        ],
        out_specs=pl.BlockSpec((tm, 1, D_MODEL), lambda g, i: (g * (t // tm) + i, 0, 0)),
        out_shape=jax.ShapeDtypeStruct((groups * t, 1, D_MODEL), F32),
        compiler_params=pltpu.CompilerParams(
            dimension_semantics=("arbitrary", "arbitrary"), vmem_limit_bytes=VMEM_LIMIT),
        name="output_stage",
    )(x, mix, mod, mod, mod, mod, w_out, g2, w_gu, w_dn, g_final)


def kernel(x_prompt, x_sample, c_prompt, c_sample, state_mlstm_C, state_mlstm_n, state_mlstm_m, w_ada, b_ada, g_norm1, w_in, b_gate, g_mlstm_head, ln_v_g, ln_v_b, w_s, b_s, w_out, g_norm2, w_gate_up, w_down, g_final):
    depth = w_ada.shape[0]
    assert depth == 1, "single-layer trunk"
    batch, seq, _ = x_prompt.shape
    nb = x_sample.shape[0]
    assert x_sample.shape[1] == 1

    gw = GROUP_WIDTH
    g1 = g_norm1[0].reshape(1, D_MODEL)
    g2 = g_norm2[0].reshape(1, D_MODEL)
    gf = g_final.reshape(1, D_MODEL)
    ln_g = ln_v_g[0].reshape(1, gw)
    ln_b = ln_v_b[0].reshape(1, gw)
    g_head = g_mlstm_head[0].reshape(1, gw)
    bias_row = jnp.pad(b_gate[0], (0, LANES - 2 * HEADS)).reshape(1, LANES)
    bias_rep = jnp.broadcast_to(b_gate[0][:, None], (2 * HEADS, LANES))
    tril =jnp.tril(jnp.ones((CHUNK, CHUNK), dtype=bool))
    ws_tril = jnp.where(tril[None], w_s[0], 0.0).astype(BF16)
    bs_rep = jnp.broadcast_to(b_s[0][:, :, None], (HEADS, CHUNK, LANES))
    ws0_row = jnp.repeat(w_s[0][:, 0, 0], HEAD_DIM).reshape(1, gw)
    bs0_row = jnp.repeat(b_s[0][:, 0], HEAD_DIM).reshape(1, gw)

    mod_s, mod_p, w_main, w_gates, w_k, w_kt = _modulation(c_sample, c_prompt, w_ada[0], b_ada[0].reshape(1, -1), w_in[0].T)
    mod_s = mod_s.reshape(1, nb, N_MOD * D_MODEL)

    a_p, b_p, g_p, kt_p, w_out_b, w_gu_b, w_dn_b = _project(
        x_prompt, mod_p, g1, w_main, w_gates, ln_g, ln_b, w_kt, (w_out[0], w_gate_up[0], w_down[0]),
        k_transposed=True, tm=PROJ_TILE, per_row=False, a_dtype=BF16)
    y_p, c_p, n_p, m_p = _prompt_tail(a_p, kt_p, b_p, g_p, x_prompt, mod_p, bias_rep, g_head, ws_tril, bs_rep,
                                      w_out_b, g2, w_gu_b, w_dn_b, gf)

    xs = x_sample.reshape(1, nb, D_MODEL)
    a_s, b_s_act, g_s, k_s = _project(xs, mod_s, g1, w_main, w_gates, ln_g, ln_b, w_k, k_transposed=False,
                                      tm=nb, per_row=True, a_dtype=F32)
    a_s2, b_s2, g_s2 = a_s[0], b_s_act[0], g_s[0]
    m0_pad = jnp.pad(state_mlstm_m[0], ((0, 0), (0, LANES - HEADS)))
    n0 = state_mlstm_n[0].reshape(nb, gw)
    mix_s, c_s, n_s, m_s = _mix_sample(a_s2, k_s[0], b_s2, g_s2, m0_pad, n0, state_mlstm_C[0],
                                       bias_row, g_head, ws0_row, bs0_row)
    y_s = _output_stage(xs, mix_s.reshape(1, nb, D_MODEL), mod_s, w_out_b, g2, w_gu_b, w_dn_b, gf,
                        tm=nb, per_row=True)

    return (
        y_p,
        y_s,
        c_p[None],
        n_p[None],
        m_p[:, :, 0][None],
        c_s[None],
        n_s.reshape(nb, HEADS, HEAD_DIM)[None],
        m_s.reshape(nb, HEADS, HEAD_DIM)[:, :, 0][None],
        a_s2[:, 2 * gw:].reshape(nb, 1, HEADS, HEAD_DIM)[None],
    )
```

```python
import functools
import math

import jax
import jax.numpy as jnp
from jax import lax
from jax.experimental import pallas as pl
from jax.experimental.pallas import tpu as pltpu

F32 = jnp.float32
BF16 = jnp.bfloat16

D_MODEL = 1024
HEADS = 4
HEAD_DIM = 128
GROUP_WIDTH = HEADS * HEAD_DIM
CHUNK = 128
D_FF = 2816
N_MOD = 6
EPS = 1e-6
QK_SCALE = HEAD_DIM ** -0.5
LANES = 128
F32_SUBLANES = 8
BF16_SUBLANES = 16

N_B = 2 * GROUP_WIDTH

VMEM_LIMIT = 56 * 1024 * 1024


def _dot(a, b):
    return jnp.dot(a, b, preferred_element_type=F32)


def _dot_nt(a, b):
    return lax.dot_general(a, b, (((1,), (1,)), ((), ())), preferred_element_type=F32)


def _sigmoid(x):
    return 1.0 / (1.0 + jnp.exp(-x))


def _gelu_tanh(x):
    c = math.sqrt(2.0 / math.pi)
    return x * (0.5 * (1.0 + jnp.tanh(c * (x + 0.044715 * (x * x * x)))))


def _log_sigmoid(x):
    return jnp.minimum(x, 0.0) - jnp.log1p(jnp.exp(-jnp.abs(x)))


def _rms(x):
    return x * lax.rsqrt(jnp.mean(x * x, axis=-1, keepdims=True) + EPS)


def _split3_bf16(x):
    hi = x.astype(BF16)
    r1 = x - hi.astype(F32)
    mid = r1.astype(BF16)
    lo = (r1 - mid.astype(F32)).astype(BF16)
    return hi, mid, lo


MOD_TILE = 1024

IN_Q, IN_K, IN_V, IN_O, IN_GATES = 0, GROUP_WIDTH, 2 * GROUP_WIDTH, 3 * GROUP_WIDTH, 4 * GROUP_WIDTH
IN_U = IN_GATES + 2 * HEADS
IN_VS = IN_U + GROUP_WIDTH
N_IN = IN_VS + GROUP_WIDTH
MAIN_GROUPS = (IN_Q, IN_V, IN_VS, IN_O, IN_U)
N_MAIN = len(MAIN_GROUPS) * GROUP_WIDTH
GATE_ROWS = BF16_SUBLANES


def _mod_kernel(cs_ref, cp_ref, w_ref, b_ref, grp_ref, gate_ref, os_ref, op_ref, wmain_ref, wgate_ref, wk_ref, wkt_ref):
    step = pl.program_id(0)
    ns, npr = cs_ref.shape[0], cp_ref.shape[0]
    pad = -(ns + npr) % BF16_SUBLANES
    c = jnp.concatenate([cs_ref[...], cp_ref[...], jnp.zeros((pad, D_MODEL), F32)], axis=0)
    a = (c * _sigmoid(c)).astype(BF16)
    res = _dot(a, w_ref[...].astype(BF16)) + b_ref[...]
    os_ref[...] = res[:ns]
    op_ref[:, 0, :] = res[ns:ns + npr]

    def transposed_into(dst_ref):
        for blk in range(GROUP_WIDTH // LANES):
            rows = grp_ref[blk * LANES:(blk + 1) * LANES, :]
            dst_ref[:, blk * LANES:(blk + 1) * LANES] = rows.T.astype(BF16)

    @pl.when(step < len(MAIN_GROUPS))
    def _():
        transposed_into(wmain_ref)

    @pl.when(step == len(MAIN_GROUPS))
    def _():
        wkt_ref[:GROUP_WIDTH, :] = grp_ref[...].astype(BF16)
        wkt_ref[GROUP_WIDTH:, :] = jnp.concatenate(
            [gate_ref[...], jnp.zeros((GATE_ROWS - 2 * HEADS, D_MODEL), F32)], axis=0).astype(BF16)
        transposed_into(wk_ref)

    @pl.when(step == 0)
    def _():
        gate_rows = jnp.concatenate([gate_ref[...], jnp.zeros((LANES - 2 * HEADS, D_MODEL), F32)], axis=0)
        wgate_ref[...] = gate_rows.T.astype(BF16)


def _modulation(c_sample, c_prompt, w_ada, b_ada, w_in_t):
    ns, npr = c_sample.shape[0], c_prompt.shape[0]
    tn = MOD_TILE
    steps = N_MOD * D_MODEL // tn
    group_starts = MAIN_GROUPS + (IN_K,)
    assert w_in_t.shape == (N_IN, D_MODEL) and steps == len(group_starts)

    def whole(shape):
        return pl.BlockSpec(shape, lambda j: (0, 0))

    def group_start(j):
        assert all(start % F32_SUBLANES == 0 for start in group_starts)
        return pl.multiple_of(sum(jnp.where(j == i, start, 0) for i, start in enumerate(group_starts)), F32_SUBLANES)

    return pl.pallas_call(
        _mod_kernel,
        grid=(steps,),
        in_specs=[
            whole((ns, D_MODEL)),
            whole((npr, D_MODEL)),
            pl.BlockSpec((D_MODEL, tn), lambda j: (0, j)),
            pl.BlockSpec((1, tn), lambda j: (0, j)),
            pl.BlockSpec((pl.Element(GROUP_WIDTH), pl.Element(D_MODEL)), lambda j: (group_start(j), 0)),
            pl.BlockSpec((pl.Element(2 * HEADS), pl.Element(D_MODEL)), lambda j: (IN_GATES, 0)),
        ],
        out_specs=[
            pl.BlockSpec((ns, tn), lambda j: (0, j)),
            pl.BlockSpec((npr, 1, tn), lambda j: (0, 0, j)),
            pl.BlockSpec((D_MODEL, GROUP_WIDTH), lambda j: (0, jnp.minimum(j, len(MAIN_GROUPS) - 1))),
            whole((D_MODEL, LANES)),
            whole((D_MODEL, GROUP_WIDTH)),
            whole((GROUP_WIDTH + GATE_ROWS, D_MODEL)),
        ],
        out_shape=[
            jax.ShapeDtypeStruct((ns, N_MOD * D_MODEL), F32),
            jax.ShapeDtypeStruct((npr, 1, N_MOD * D_MODEL), F32),
            jax.ShapeDtypeStruct((D_MODEL, N_MAIN), BF16),
            jax.ShapeDtypeStruct((D_MODEL, LANES), BF16),
            jax.ShapeDtypeStruct((D_MODEL, GROUP_WIDTH), BF16),
            jax.ShapeDtypeStruct((GROUP_WIDTH + GATE_ROWS, D_MODEL), BF16),
        ],
        compiler_params=pltpu.CompilerParams(dimension_semantics=("arbitrary",), vmem_limit_bytes=VMEM_LIMIT),
        name="modulation",
    )(c_sample, c_prompt, w_ada, b_ada, w_in_t, w_in_t)


def _mod_spec(piece, tm, per_row):
    if per_row:
        return pl.BlockSpec((None, tm, D_MODEL), lambda g, t: (g, t, piece))
    return pl.BlockSpec((None, 1, D_MODEL), lambda g, t: (g, 0, piece))


def _const_spec(shape):
    nd = len(shape)
    return pl.BlockSpec(shape, lambda g, t: (0,) * nd, pipeline_mode=pl.Buffered(1))


PROJ_TILE = 1024


def _proj_body(x_ref, sh_ref, sc_ref, g1_ref, w_ref, wg_ref, lng_ref, lnb_ref, a_ref, b_ref, g_ref):
    n_a = a_ref.shape[1]
    n_copy = n_a - GROUP_WIDTH
    x = x_ref[...]
    h = _rms(x) * g1_ref[...]
    h = (h * (1.0 + sc_ref[...]) + sh_ref[...]).astype(BF16)
    p = _dot(h, w_ref[...])
    a_ref[:, :n_copy] = p[:, :n_copy].astype(a_ref.dtype)
    vs = _gelu_tanh(p[:, n_copy:n_a])
    for hd in range(HEADS):
        sl = slice(hd * HEAD_DIM, (hd + 1) * HEAD_DIM)
        v = vs[:, sl]
        mu = jnp.mean(v, axis=-1, keepdims=True)
        vc = v - mu
        var = jnp.mean(vc * vc, axis=-1, keepdims=True)
        y = vc * lax.rsqrt(var + EPS) * lng_ref[:, sl] + lnb_ref[:, sl]
        a_ref[:, n_copy + hd * HEAD_DIM: n_copy + (hd + 1) * HEAD_DIM] = y.astype(a_ref.dtype)
    b_ref[:, :GROUP_WIDTH] = _sigmoid(p[:, n_a: n_a + GROUP_WIDTH])
    b_ref[:, GROUP_WIDTH:] = _gelu_tanh(p[:, n_a + GROUP_WIDTH: n_a + N_B])
    if wg_ref is not None:
        g_ref[...] = _dot(h, wg_ref[...])
    return h


def _proj_kernel(*refs, k_transposed, n_cast):
    ins, outs = refs[:9 + n_cast], refs[9 + n_cast:]
    wk_ref, k_ref = ins[8], outs[3]
    if k_transposed:
        h = _proj_body(*ins[:5], None, *ins[6:8], *outs[:3])
        kt_gates = _dot_nt(wk_ref[...], h)
        k_ref[...] = kt_gates[:GROUP_WIDTH].astype(k_ref.dtype)
        outs[2][...] = kt_gates[GROUP_WIDTH:]
    else:
        h = _proj_body(*ins[:8], *outs[:3])
        k_ref[...] = _dot(h, wk_ref[...]).astype(k_ref.dtype)
    for src, dst in zip(ins[9:], outs[4:]):
        dst[...] = src[...].astype(dst.dtype)


def _project(x, mod, g1, w_main, w_gates, ln_g, ln_b, w_k, cast_weights=(), *, k_transposed, tm, per_row, a_dtype):
    groups, t, _ = x.shape
    n_a = w_main.shape[1] - N_B
    steps_per_group = t // tm
    n_steps = groups * steps_per_group
    in_specs = [
        pl.BlockSpec((None, tm, D_MODEL), lambda g, i: (g, i, 0)),
        _mod_spec(0, tm, per_row),
        _mod_spec(1, tm, per_row),
        _const_spec((1, D_MODEL)),
        _const_spec(w_main.shape),
        _const_spec(w_gates.shape),
        _const_spec((1, GROUP_WIDTH)),
        _const_spec((1, GROUP_WIDTH)),
    ]
    out_specs = [
        pl.BlockSpec((None, tm, n_a), lambda g, i: (g, i, 0)),
        pl.BlockSpec((None, tm, N_B), lambda g, i: (g, i, 0)),
    ]
    out_shape = [
        jax.ShapeDtypeStruct((groups, t, n_a), a_dtype),
        jax.ShapeDtypeStruct((groups, t, N_B), F32),
    ]
    if k_transposed:
        out_specs.append(pl.BlockSpec((None, GATE_ROWS, tm), lambda g, i: (g, 0, i)))
        out_shape.append(jax.ShapeDtypeStruct((groups, GATE_ROWS, t), F32))
    else:
        out_specs.append(pl.BlockSpec((None, tm, LANES), lambda g, i: (g, i, 0)))
        out_shape.append(jax.ShapeDtypeStruct((groups, t, LANES), F32))
    args = [x, mod, mod, g1, w_main, w_gates, ln_g, ln_b, w_k]
    in_specs.append(_const_spec(w_k.shape))
    if k_transposed:
        out_specs.append(pl.BlockSpec((None, GROUP_WIDTH, tm), lambda g, i: (g, 0, i)))
        out_shape.append(jax.ShapeDtypeStruct((groups, GROUP_WIDTH, t), a_dtype))
    else:
        out_specs.append(pl.BlockSpec((None, tm, GROUP_WIDTH), lambda g, i: (g, i, 0)))
        out_shape.append(jax.ShapeDtypeStruct((groups, t, GROUP_WIDTH), a_dtype))
    for w in cast_weights:
        rows, cols = w.shape
        assert rows % (n_steps * 16) == 0, "row block must be a whole number of bf16 sublane tiles"
        spec = pl.BlockSpec((rows // n_steps, cols), lambda g, i: (g * steps_per_group + i, 0))
        in_specs.append(spec)
        out_specs.append(spec)
        out_shape.append(jax.ShapeDtypeStruct((rows, cols), BF16))
        args.append(w)
    return pl.pallas_call(
        functools.partial(_proj_kernel, k_transposed=k_transposed, n_cast=len(cast_weights)),
        grid=(groups, t // tm),
        in_specs=in_specs,
        out_specs=out_specs,
        out_shape=out_shape,
        compiler_params=pltpu.CompilerParams(
            dimension_semantics=("arbitrary", "arbitrary"), vmem_limit_bytes=VMEM_LIMIT),
        name="project",
    )(*args)


MIX_TILE = 512


def _dot3_rhs(lhs_bf16, rhs_f32):
    hi, mid, lo = _split3_bf16(rhs_f32)
    return _dot(lhs_bf16, hi) + _dot(lhs_bf16, mid) + _dot(lhs_bf16, lo)


def _dot3_lhs(lhs_f32, rhs_bf16):
    hi, mid, lo = _split3_bf16(lhs_f32)
    return _dot(hi, rhs_bf16) + _dot(mid, rhs_bf16) + _dot(lo, rhs_bf16)


FF_CHUNKS = (768, 768, 768, 512)
MIXER_LOOKAHEAD = 2
OUT_ROW_BLOCKS = 2


def _gate_rows(g_ref, bias_ref, tril, triu, n_chunks):
    n_gates = 2 * HEADS
    assert n_chunks * n_gates <= LANES
    rows = [g_ref[0:n_gates, c * CHUNK:(c + 1) * CHUNK] + bias_ref[...] for c in range(n_chunks)]
    rows_all = jnp.concatenate(rows, axis=0)
    lf_rows = _log_sigmoid(rows_all)
    b_rows_all = _dot3_lhs(lf_rows, triu)
    lf_cols = jnp.concatenate([lf_rows, jnp.zeros((LANES - n_chunks * n_gates, CHUNK), F32)], axis=0).T
    bc_all = _dot3_rhs(tril, lf_cols)
    bc, arow, blast = [], [], []
    for c in range(n_chunks):
        bc.append(bc_all[:, c * n_gates:(c + 1) * n_gates])
        b_rows = b_rows_all[c * n_gates:(c + 1) * n_gates, :]
        arow.append([rows[c][hd:hd + 1, :] - b_rows[HEADS + hd:HEADS + hd + 1, :] for hd in range(HEADS)])
        blast.append([jnp.min(b_rows[HEADS + hd:HEADS + hd + 1, :], axis=1, keepdims=True) for hd in range(HEADS)])
    return bc, arow, blast


def _mlstm_local(q, kt, a_row, b_last, causal):
    amat = jnp.where(causal, a_row, -jnp.inf)
    m_row = jnp.max(amat, axis=1, keepdims=True)
    s_loc = (_dot(q, kt) * jnp.exp(amat - m_row)).astype(BF16)
    g_row = b_last + a_row
    g_loc = jnp.max(g_row, axis=1, keepdims=True)
    kw = (kt.astype(F32) * jnp.exp(g_row - g_loc)).astype(BF16)
    return m_row, s_loc, g_loc, kw


def _mlstm_readout(local, q, v, og, g_head, b_last, b_rep, cta, m_prev, ones_blk):
    m_row, s_loc, g_loc, kw = local
    va = jnp.concatenate([v, ones_blk], axis=1)
    nd_loc = _dot(s_loc, va)
    u_aug = _dot(kw, va)
    inter = _dot(q, cta.astype(BF16))
    mm = jnp.maximum(m_prev, m_row)
    f_loc = jnp.exp(m_row - mm) * QK_SCALE
    f_int = jnp.exp(m_prev - mm) * QK_SCALE
    nd = f_loc * nd_loc + f_int * inter
    clamp = jnp.exp(-(b_rep + mm))
    hh = nd[:, :HEAD_DIM] / jnp.maximum(jnp.abs(nd[:, HEAD_DIM:]), clamp)
    ml = _rms(hh) * g_head * og

    dec = b_last + m_prev
    m_new = jnp.maximum(dec, g_loc)
    cta_new = jnp.exp(dec - m_new) * cta + jnp.exp(g_loc - m_new) * u_aug
    return ml, cta_new, m_new


def _prompt_tail_kernel(a_ref, kt_ref, b_ref, g_ref, x_ref, gt1_ref, sh2_ref, sc2_ref, gt2_ref,
                        bias_ref, gh_ref, ws_ref, bs_ref, wo_ref, g2_ref, wgu_ref, wdn_ref, gf_ref,
                        y_ref, c_out, n_out, m_out,
                        mix_ref, cta_ref, m_ref, *, tiles_per_seq):
    step = pl.program_id(0)
    n_tiles = pl.num_programs(0) - 1
    refs = (a_ref, kt_ref, b_ref, g_ref, x_ref, gt1_ref, sh2_ref, sc2_ref, gt2_ref,
            bias_ref, gh_ref, ws_ref, bs_ref, wo_ref, g2_ref, wgu_ref, wdn_ref, gf_ref,
            y_ref, mix_ref, cta_ref, m_ref)

    @pl.when(step % tiles_per_seq == 0)
    def _():
        cta_ref[...] = jnp.zeros_like(cta_ref)
        m_ref[...] = jnp.zeros_like(m_ref)

    @pl.when(step == 0)
    def _():
        _prompt_tail_body(refs, with_mixer=True, with_out=False)

    @pl.when(jnp.logical_and(step > 0, step < n_tiles))
    def _():
        _prompt_tail_body(refs, with_mixer=True, with_out=True)

    @pl.when(step == n_tiles)
    def _():
        _prompt_tail_body(refs, with_mixer=False, with_out=True)

    @pl.when(jnp.logical_and(step % tiles_per_seq == tiles_per_seq - 1, step < n_tiles))
    def _():
        for hd in range(HEADS):
            cta = cta_ref[hd]
            c_out[hd] = cta[:, :HEAD_DIM].T
            n_out[hd: hd + 1, :] = cta[:, HEAD_DIM:].T[0:1, :]
            m_out[hd: hd + 1, :] = m_ref[hd][0:1, :]


def _prompt_tail_body(refs, *, with_mixer, with_out):
    (a_ref, kt_ref, b_ref, g_ref, x_ref, gt1_ref, sh2_ref, sc2_ref, gt2_ref,
     bias_ref, gh_ref, ws_ref, bs_ref, wo_ref, g2_ref, wgu_ref, wdn_ref, gf_ref,
     y_ref, mix_ref, cta_ref, m_ref) = refs
    tm = a_ref.shape[0]
    n_chunks = tm // CHUNK
    n_copy = a_ref.shape[1] - GROUP_WIDTH

    row = lax.broadcasted_iota(jnp.int32, (CHUNK, CHUNK), 0)
    col = lax.broadcasted_iota(jnp.int32, (CHUNK, CHUNK), 1)
    causal = row >= col
    tril = jnp.where(causal, 1.0, 0.0).astype(BF16)
    triu = jnp.where(row <= col, 1.0, 0.0).astype(BF16)
    ones_blk = jnp.ones((CHUNK, HEAD_DIM), BF16)

    blocks = [slice(r0, r0 + tm // OUT_ROW_BLOCKS) for r0 in range(0, tm, tm // OUT_ROW_BLOCKS)] if with_out else []
    outproj = [_dot(mix_ref[rs, :], wo_ref[...]) for rs in blocks]
    x1, h2 = [], []
    for rs, op in zip(blocks, outproj):
        x1.append(x_ref[rs, :] + gt1_ref[...] * op)
        h = _rms(x1[-1]) * g2_ref[...]
        h2.append((h * (1.0 + sc2_ref[...]) + sh2_ref[...]).astype(BF16))

    if with_mixer:
        bc, arow, blast = _gate_rows(g_ref, bias_ref, tril, triu, n_chunks)
        state = [(cta_ref[hd], jnp.max(m_ref[hd][0:1, :], axis=1, keepdims=True)) for hd in range(HEADS)]
    sg_all = [None] * HEADS

    def mixer_local(c, hd):
        rs = slice(c * CHUNK, (c + 1) * CHUNK)
        sl = slice(hd * HEAD_DIM, (hd + 1) * HEAD_DIM)
        return _mlstm_local(a_ref[rs, sl], kt_ref[sl, rs], arow[c][hd], blast[c][hd], causal)

    def mixer_readout(c, hd, local):
        rs = slice(c * CHUNK, (c + 1) * CHUNK)
        sl = slice(hd * HEAD_DIM, (hd + 1) * HEAD_DIM)
        if c == 0:
            vsn_all = jnp.concatenate(
                [a_ref[cc * CHUNK:(cc + 1) * CHUNK, n_copy + hd * HEAD_DIM: n_copy + (hd + 1) * HEAD_DIM]
                 for cc in range(n_chunks)], axis=1)
            sg_all[hd] = _dot(ws_ref[hd], vsn_all)
        cta, m_prev = state[hd]
        b_rep = jnp.broadcast_to(bc[c][:, HEADS + hd: HEADS + hd + 1], (CHUNK, HEAD_DIM))
        ml, cta, m_prev = _mlstm_readout(
            local, a_ref[rs, sl], a_ref[rs, GROUP_WIDTH + hd * HEAD_DIM: GROUP_WIDTH + (hd + 1) * HEAD_DIM],
            b_ref[rs, sl], gh_ref[:, sl], blast[c][hd], b_rep, cta, m_prev, ones_blk)
        state[hd] = (cta, m_prev)
        mix_ref[rs, sl] = ml.astype(mix_ref.dtype)
        sg = sg_all[hd][:, c * CHUNK:(c + 1) * CHUNK] + bs_ref[hd]
        ug = b_ref[rs, GROUP_WIDTH + hd * HEAD_DIM: GROUP_WIDTH + (hd + 1) * HEAD_DIM]
        mix_ref[rs, GROUP_WIDTH + hd * HEAD_DIM: GROUP_WIDTH + (hd + 1) * HEAD_DIM] = (ug * sg).astype(mix_ref.dtype)

    pieces = [(c, hd) for c in range(n_chunks) for hd in range(HEADS)] if with_mixer else []
    per_ff = -(-len(pieces) // len(FF_CHUNKS))
    ahead = [mixer_local(*pieces[p]) for p in range(min(MIXER_LOOKAHEAD, len(pieces)))]
    acc = [None] * len(blocks)
    f0 = 0
    for j, width in enumerate(FF_CHUNKS):
        gate_up = [(_dot(h, wgu_ref[:, f0:f0 + width]), _dot(h, wgu_ref[:, D_FF + f0:D_FF + f0 + width]))
                   for h in h2]
        for idx in range(j * per_ff, min((j + 1) * per_ff, len(pieces))):
            if idx + MIXER_LOOKAHEAD < len(pieces):
                ahead.append(mixer_local(*pieces[idx + MIXER_LOOKAHEAD]))
            mixer_readout(*pieces[idx], ahead.pop(0))
        for r, (gate, up) in enumerate(gate_up):
            act = (gate * _sigmoid(gate) * up).astype(BF16)
            part = _dot(act, wdn_ref[f0:f0 + width, :])
            acc[r] = part if acc[r] is None else acc[r] + part
            if j + 1 == len(FF_CHUNKS):
                x2 = x1[r] + gt2_ref[...] * acc[r]
                y_ref[blocks[r], :] = _rms(x2) * gf_ref[...]
        f0 += width

    if with_mixer:
        for hd in range(HEADS):
            cta_ref[hd] = state[hd][0]
            m_ref[hd] = jnp.broadcast_to(state[hd][1], m_ref.shape[1:])


def _prompt_tail(a, kt, b, gates, x, mod, bias_row, g_head, ws_tril, bs_rep, w_out, g2, w_gu, w_dn, g_final):
    groups, t, n_a = a.shape
    tm = MIX_TILE
    tps = t // tm
    n_tiles = groups * tps
    assert sum(FF_CHUNKS) == D_FF

    def cur(i):
        return jnp.minimum(i, n_tiles - 1)

    def prev(i):
        return jnp.maximum(i - 1, 0)

    def rows(tile, width):
        return pl.BlockSpec((None, tm, width), lambda i: (tile(i) // tps, tile(i) % tps, 0))

    def mod_piece(piece):
        return pl.BlockSpec((None, 1, D_MODEL), lambda i: (prev(i) // tps, 0, piece))

    def const(shape):
        nd = len(shape)
        return pl.BlockSpec(shape, lambda i: (0,) * nd, pipeline_mode=pl.Buffered(1))

    def per_seq(shape):
        nd = len(shape)
        return pl.BlockSpec((None,) + shape, lambda i: (cur(i) // tps,) + (0,) * nd)

    return pl.pallas_call(
        functools.partial(_prompt_tail_kernel, tiles_per_seq=tps),
        grid=(n_tiles + 1,),
        in_specs=[
            rows(cur, n_a),
            pl.BlockSpec((None, GROUP_WIDTH, tm), lambda i: (cur(i) // tps, 0, cur(i) % tps)),
            rows(cur, N_B),
            pl.BlockSpec((None, GATE_ROWS, tm), lambda i: (cur(i) // tps, 0, cur(i) % tps)),
            rows(prev, D_MODEL),
            mod_piece(2), mod_piece(3), mod_piece(4), mod_piece(5),
            const((2 * HEADS, LANES)),
            const((1, GROUP_WIDTH)),
            const((HEADS, CHUNK, CHUNK)),
            const((HEADS, CHUNK, LANES)),
            const((D_MODEL, D_MODEL)),
            const((1, D_MODEL)),
            const((D_MODEL, 2 * D_FF)),
            const((D_FF, D_MODEL)),
            const((1, D_MODEL)),
        ],
        out_specs=[
            rows(prev, D_MODEL),
            per_seq((HEADS, HEAD_DIM, HEAD_DIM)),
            per_seq((HEADS, HEAD_DIM)),
            per_seq((HEADS, LANES)),
        ],
        out_shape=[
            jax.ShapeDtypeStruct((groups, t, D_MODEL), F32),
            jax.ShapeDtypeStruct((groups, HEADS, HEAD_DIM, HEAD_DIM), F32),
            jax.ShapeDtypeStruct((groups, HEADS, HEAD_DIM), F32),
            jax.ShapeDtypeStruct((groups, HEADS, LANES), F32),
        ],
        scratch_shapes=[
            pltpu.VMEM((tm, D_MODEL), BF16),
            pltpu.VMEM((HEADS, HEAD_DIM, 2 * HEAD_DIM), F32),
            pltpu.VMEM((HEADS, F32_SUBLANES, LANES), F32),
        ],
        compiler_params=pltpu.CompilerParams(
            dimension_semantics=("arbitrary",), vmem_limit_bytes=VMEM_LIMIT),
        name="prompt_tail",
    )(a, kt, b, gates, x, mod, mod, mod, mod, bias_row, g_head, ws_tril, bs_rep, w_out, g2, w_gu, w_dn, g_final)


SAMPLE_TOKENS_PER_STEP = 128
SAMPLE_UNROLL = 8


def _mix_sample_kernel(q_ref, k_ref, v_ref, vsn_ref, og_ref, ug_ref, g_ref, m0_ref, n0_ref, c_ref,
                       bias_ref, gh_ref, ws0_ref, bs0_ref,
                       mix_ref, c_out, n_out, m_out,
                       vt_ref, kp_ref, wd_ref, cqt_ref):
    hd = pl.program_id(0)
    grp = pl.program_id(1)
    nb = q_ref.shape[0]
    lane = lax.broadcasted_iota(jnp.int32, (nb, LANES), 1)

    def gate_terms():
        pre = g_ref[...] + bias_ref[...]
        i_pre = jnp.sum(jnp.where(lane == hd, pre, 0.0), axis=1, keepdims=True)
        f_pre = jnp.sum(jnp.where(lane == hd + HEADS, pre, 0.0), axis=1, keepdims=True)
        m_prev = jnp.sum(jnp.where(lane == hd, m0_ref[...], 0.0), axis=1, keepdims=True)
        inter = _log_sigmoid(f_pre) + m_prev
        m_t = jnp.maximum(inter, i_pre)
        return m_t, jnp.exp(i_pre - m_t), jnp.exp(inter - m_t)

    @pl.when(grp == 0)
    def _():
        _, w_in, w_dec = gate_terms()
        vt_ref[...] = v_ref[...].T
        kp_ref[...] = (w_in * k_ref[...]).astype(BF16)
        wd_ref[...] = jnp.broadcast_to(w_dec, wd_ref.shape)
        cqt_ref[...] = jnp.zeros_like(cqt_ref)

    lane_sq = lax.broadcasted_iota(jnp.int32, (HEAD_DIM, LANES), 1)
    tokens = c_ref.shape[0]

    def body(r, cqt):
        tok = grp * tokens + r
        q_row = q_ref[pl.ds(tok, 1), :]
        c_b = c_ref[r]
        sel = lane_sq == tok
        cq_col = jnp.sum(c_b * q_row, axis=1, keepdims=True)
        v_sel = jnp.where(sel, vt_ref[...], 0.0).astype(BF16)
        outer = _dot(v_sel, kp_ref[...])
        c_out[r] = wd_ref[pl.ds(tok, 1), :] * c_b + outer
        return jnp.where(sel, cq_col, cqt)

    cqt_ref[...] = lax.fori_loop(0, tokens, body, cqt_ref[...], unroll=SAMPLE_UNROLL)

    @pl.when(grp == pl.num_programs(1) - 1)
    def _():
        m_t, w_in, w_dec = gate_terms()
        q = q_ref[...]
        k = k_ref[...]
        v = v_ref[...]
        n0 = n0_ref[...]
        cq = cqt_ref[...].T
        s = jnp.sum(q * k, axis=1, keepdims=True) * (QK_SCALE * w_in)
        w_inter = w_dec * QK_SCALE
        num = s * v + w_inter * cq
        den = s + w_inter * jnp.sum(n0 * q, axis=1, keepdims=True)
        hh = num / jnp.maximum(jnp.abs(den), jnp.exp(-m_t))
        ml = _rms(hh) * gh_ref[...] * og_ref[...]
        cm = ug_ref[...] * (ws0_ref[...] * vsn_ref[...] + bs0_ref[...])
        n_out[...] = w_dec * n0 + w_in * k
        m_out[...] = jnp.broadcast_to(m_t, m_out.shape)
        for hh_static in range(HEADS):
            @pl.when(hd == hh_static)
            def _():
                mix_ref[:, hh_static * HEAD_DIM: (hh_static + 1) * HEAD_DIM] = ml
                mix_ref[:, GROUP_WIDTH + hh_static * HEAD_DIM: GROUP_WIDTH + (hh_static + 1) * HEAD_DIM] = cm


def _mix_sample(a, k, b, gates, m0_pad, n0, c0, bias_row, g_head, ws0_row, bs0_row):
    nb = a.shape[0]
    tb = SAMPLE_TOKENS_PER_STEP

    def head_block(offset):
        return pl.BlockSpec((nb, HEAD_DIM), lambda h, g: (0, offset + h))

    full = pl.BlockSpec((nb, LANES), lambda h, g: (0, 0))
    head_row = pl.BlockSpec((1, HEAD_DIM), lambda h, g: (0, h))
    c_spec = pl.BlockSpec((tb, None, HEAD_DIM, HEAD_DIM), lambda h, g: (g, h, 0, 0))
    return pl.pallas_call(
        _mix_sample_kernel,
        grid=(HEADS, nb // tb),
        in_specs=[
            head_block(0), head_block(0), head_block(HEADS), head_block(2 * HEADS),
            head_block(0), head_block(HEADS),
            full, full, head_block(0), c_spec,
            pl.BlockSpec((1, LANES), lambda h, g: (0, 0)),
            head_row, head_row, head_row,
        ],
        out_specs=[
            pl.BlockSpec((nb, D_MODEL), lambda h, g: (0, 0)),
            c_spec,
            head_block(0),
            head_block(0),
        ],
        out_shape=[
            jax.ShapeDtypeStruct((nb, D_MODEL), F32),
            jax.ShapeDtypeStruct(c0.shape, F32),
            jax.ShapeDtypeStruct((nb, GROUP_WIDTH), F32),
            jax.ShapeDtypeStruct((nb, GROUP_WIDTH), F32),
        ],
        scratch_shapes=[
            pltpu.VMEM((HEAD_DIM, nb), F32),
            pltpu.VMEM((nb, HEAD_DIM), BF16),
            pltpu.VMEM((nb, LANES), F32),
            pltpu.VMEM((HEAD_DIM, nb), F32),
        ],
        compiler_params=pltpu.CompilerParams(
            dimension_semantics=("arbitrary", "arbitrary"), vmem_limit_bytes=VMEM_LIMIT),
        name="mix_sample",
    )(a, k, a, a, b, b, gates, m0_pad, n0, c0, bias_row, g_head, ws0_row, bs0_row)


def _out_kernel(x_ref, mix_ref, gt1_ref, sh2_ref, sc2_ref, gt2_ref, wo_ref, g2_ref, wgu_ref, wdn_ref, gf_ref,
                y_ref):
    x1 = x_ref[...] + gt1_ref[...] * _dot(mix_ref[...].astype(BF16), wo_ref[...])
    h2 = _rms(x1) * g2_ref[...]
    h2 = (h2 * (1.0 + sc2_ref[...]) + sh2_ref[...]).astype(BF16)
    gate = _dot(h2, wgu_ref[:, :D_FF])
    up = _dot(h2, wgu_ref[:, D_FF:])
    act = (gate * _sigmoid(gate) * up).astype(BF16)
    x2 = x1 + gt2_ref[...] * _dot(act, wdn_ref[...])
    y_ref[:, 0, :] = _rms(x2) * gf_ref[...]


def _output_stage(x, mix, mod, w_out, g2, w_gu, w_dn, g_final, *, tm, per_row):
    groups, t, _ = x.shape
    return pl.pallas_call(
        _out_kernel,
        grid=(groups, t // tm),
        in_specs=[
            pl.BlockSpec((None, tm, D_MODEL), lambda g, i: (g, i, 0)),
            pl.BlockSpec((None, tm, D_MODEL), lambda g, i: (g, i, 0)),
            _mod_spec(2, tm, per_row),
            _mod_spec(3, tm, per_row),
            _mod_spec(4, tm, per_row),
            _mod_spec(5, tm, per_row),
            _const_spec((D_MODEL, D_MODEL)),
            _const_spec((1, D_MODEL)),
            _const_spec((D_MODEL, 2 * D_FF)),
            _const_spec((D_FF, D_MODEL)),
            _const_spec((1, D_MODEL)),
        ],
        out_specs=pl.BlockSpec((tm, 1, D_MODEL), lambda g, i: (g * (t // tm) + i, 0, 0)),
        out_shape=jax.ShapeDtypeStruct((groups * t, 1, D_MODEL), F32),
        compiler_params=pltpu.CompilerParams(
            dimension_semantics=("arbitrary", "arbitrary"), vmem_limit_bytes=VMEM_LIMIT),
        name="output_stage",
    )(x, mix, mod, mod, mod, mod, w_out, g2, w_gu, w_dn, g_final)


def kernel(x_prompt, x_sample, c_prompt, c_sample, state_mlstm_C, state_mlstm_n, state_mlstm_m, w_ada, b_ada, g_norm1, w_in, b_gate, g_mlstm_head, ln_v_g, ln_v_b, w_s, b_s, w_out, g_norm2, w_gate_up, w_down, g_final):
    depth = w_ada.shape[0]
    assert depth == 1, "single-layer trunk"
    batch, seq, _ = x_prompt.shape
    nb = x_sample.shape[0]
    assert x_sample.shape[1] == 1

    gw = GROUP_WIDTH
    g1 = g_norm1[0].reshape(1, D_MODEL)
    g2 = g_norm2[0].reshape(1, D_MODEL)
    gf = g_final.reshape(1, D_MODEL)
    ln_g = ln_v_g[0].reshape(1, gw)
    ln_b = ln_v_b[0].reshape(1, gw)
    g_head = g_mlstm_head[0].reshape(1, gw)
    bias_row = jnp.pad(b_gate[0], (0, LANES - 2 * HEADS)).reshape(1, LANES)
    bias_rep = jnp.broadcast_to(b_gate[0][:, None], (2 * HEADS, LANES))
    tril =jnp.tril(jnp.ones((CHUNK, CHUNK), dtype=bool))
    ws_tril = jnp.where(tril[None], w_s[0], 0.0).astype(BF16)
    bs_rep = jnp.broadcast_to(b_s[0][:, :, None], (HEADS, CHUNK, LANES))
    ws0_row = jnp.repeat(w_s[0][:, 0, 0], HEAD_DIM).reshape(1, gw)
    bs0_row = jnp.repeat(b_s[0][:, 0], HEAD_DIM).reshape(1, gw)

    mod_s, mod_p, w_main, w_gates, w_k, w_kt = _modulation(c_sample, c_prompt, w_ada[0], b_ada[0].reshape(1, -1), w_in[0].T)
    mod_s = mod_s.reshape(1, nb, N_MOD * D_MODEL)

    a_p, b_p, g_p, kt_p, w_out_b, w_gu_b, w_dn_b = _project(
        x_prompt, mod_p, g1, w_main, w_gates, ln_g, ln_b, w_kt, (w_out[0], w_gate_up[0], w_down[0]),
        k_transposed=True, tm=PROJ_TILE, per_row=False, a_dtype=BF16)
    y_p, c_p, n_p, m_p = _prompt_tail(a_p, kt_p, b_p, g_p, x_prompt, mod_p, bias_rep, g_head, ws_tril, bs_rep,
                                      w_out_b, g2, w_gu_b, w_dn_b, gf)

    xs = x_sample.reshape(1, nb, D_MODEL)
    a_s, b_s_act, g_s, k_s = _project(xs, mod_s, g1, w_main, w_gates, ln_g, ln_b, w_k, k_transposed=False,
                                      tm=nb, per_row=True, a_dtype=F32)
    a_s2, b_s2, g_s2 = a_s[0], b_s_act[0], g_s[0]
    m0_pad = jnp.pad(state_mlstm_m[0], ((0, 0), (0, LANES - HEADS)))
    n0 = state_mlstm_n[0].reshape(nb, gw)
    mix_s, c_s, n_s, m_s = _mix_sample(a_s2, k_s[0], b_s2, g_s2, m0_pad, n0, state_mlstm_C[0],
                                       bias_row, g_head, ws0_row, bs0_row)
    y_s = _output_stage(xs, mix_s.reshape(1, nb, D_MODEL), mod_s, w_out_b, g2, w_gu_b, w_dn_b, gf,
                        tm=nb, per_row=True)

    return (
        y_p,
        y_s,
        c_p[None],
        n_p[None],
        m_p[:, :, 0][None],
        c_s[None],
        n_s.reshape(nb, HEADS, HEAD_DIM)[None],
        m_s.reshape(nb, HEADS, HEAD_DIM)[:, :, 0][None],
        a_s2[:, 2 * gw:].reshape(nb, 1, HEADS, HEAD_DIM)[None],
    )
```

```python
import functools
import math

import jax
import jax.numpy as jnp
from jax import lax
from jax.experimental import pallas as pl
from jax.experimental.pallas import tpu as pltpu

F32 = jnp.float32
BF16 = jnp.bfloat16

D_MODEL = 1024
HEADS = 4
HEAD_DIM = 128
GROUP_WIDTH = HEADS * HEAD_DIM
CHUNK = 128
D_FF = 2816
N_MOD = 6
EPS = 1e-6
QK_SCALE = HEAD_DIM ** -0.5
LANES = 128
F32_SUBLANES = 8
BF16_SUBLANES = 16

N_B = 2 * GROUP_WIDTH

VMEM_LIMIT = 56 * 1024 * 1024


def _dot(a, b):
    return jnp.dot(a, b, preferred_element_type=F32)


def _dot_nt(a, b):
    return lax.dot_general(a, b, (((1,), (1,)), ((), ())), preferred_element_type=F32)


def _sigmoid(x):
    return 1.0 / (1.0 + jnp.exp(-x))


def _gelu_tanh(x):
    c = math.sqrt(2.0 / math.pi)
    return x * (0.5 * (1.0 + jnp.tanh(c * (x + 0.044715 * (x * x * x)))))


def _log_sigmoid(x):
    return jnp.minimum(x, 0.0) - jnp.log1p(jnp.exp(-jnp.abs(x)))


def _rms(x):
    return x * lax.rsqrt(jnp.mean(x * x, axis=-1, keepdims=True) + EPS)


def _split3_bf16(x):
    hi = x.astype(BF16)
    r1 = x - hi.astype(F32)
    mid = r1.astype(BF16)
    lo = (r1 - mid.astype(F32)).astype(BF16)
    return hi, mid, lo


MOD_TILE = 1024

IN_Q, IN_K, IN_V, IN_O, IN_GATES = 0, GROUP_WIDTH, 2 * GROUP_WIDTH, 3 * GROUP_WIDTH, 4 * GROUP_WIDTH
IN_U = IN_GATES + 2 * HEADS
IN_VS = IN_U + GROUP_WIDTH
N_IN = IN_VS + GROUP_WIDTH
MAIN_GROUPS = (IN_Q, IN_V, IN_VS, IN_O, IN_U)
N_MAIN = len(MAIN_GROUPS) * GROUP_WIDTH
GATE_ROWS = BF16_SUBLANES


def _mod_kernel(cs_ref, cp_ref, w_ref, b_ref, grp_ref, gate_ref, os_ref, op_ref, wmain_ref, wgate_ref, wk_ref, wkt_ref):
    step = pl.program_id(0)
    ns, npr = cs_ref.shape[0], cp_ref.shape[0]
    pad = -(ns + npr) % BF16_SUBLANES
    c = jnp.concatenate([cs_ref[...], cp_ref[...], jnp.zeros((pad, D_MODEL), F32)], axis=0)
    a = (c * _sigmoid(c)).astype(BF16)
    res = _dot(a, w_ref[...].astype(BF16)) + b_ref[...]
    os_ref[...] = res[:ns]
    op_ref[:, 0, :] = res[ns:ns + npr]

    def transposed_into(dst_ref):
        for blk in range(GROUP_WIDTH // LANES):
            rows = grp_ref[blk * LANES:(blk + 1) * LANES, :]
            dst_ref[:, blk * LANES:(blk + 1) * LANES] = rows.T.astype(BF16)

    @pl.when(step < len(MAIN_GROUPS))
    def _():
        transposed_into(wmain_ref)

    @pl.when(step == len(MAIN_GROUPS))
    def _():
        wkt_ref[:GROUP_WIDTH, :] = grp_ref[...].astype(BF16)
        wkt_ref[GROUP_WIDTH:, :] = jnp.concatenate(
            [gate_ref[...], jnp.zeros((GATE_ROWS - 2 * HEADS, D_MODEL), F32)], axis=0).astype(BF16)
        transposed_into(wk_ref)

    @pl.when(step == 0)
    def _():
        gate_rows = jnp.concatenate([gate_ref[...], jnp.zeros((LANES - 2 * HEADS, D_MODEL), F32)], axis=0)
        wgate_ref[...] = gate_rows.T.astype(BF16)


def _modulation(c_sample, c_prompt, w_ada, b_ada, w_in_t):
    ns, npr = c_sample.shape[0], c_prompt.shape[0]
    tn = MOD_TILE
    steps = N_MOD * D_MODEL // tn
    group_starts = MAIN_GROUPS + (IN_K,)
    assert w_in_t.shape == (N_IN, D_MODEL) and steps == len(group_starts)

    def whole(shape):
        return pl.BlockSpec(shape, lambda j: (0, 0))

    def group_start(j):
        assert all(start % F32_SUBLANES == 0 for start in group_starts)
        return pl.multiple_of(sum(jnp.where(j == i, start, 0) for i, start in enumerate(group_starts)), F32_SUBLANES)

    return pl.pallas_call(
        _mod_kernel,
        grid=(steps,),
        in_specs=[
            whole((ns, D_MODEL)),
            whole((npr, D_MODEL)),
            pl.BlockSpec((D_MODEL, tn), lambda j: (0, j)),
            pl.BlockSpec((1, tn), lambda j: (0, j)),
            pl.BlockSpec((pl.Element(GROUP_WIDTH), pl.Element(D_MODEL)), lambda j: (group_start(j), 0)),
            pl.BlockSpec((pl.Element(2 * HEADS), pl.Element(D_MODEL)), lambda j: (IN_GATES, 0)),
        ],
        out_specs=[
            pl.BlockSpec((ns, tn), lambda j: (0, j)),
            pl.BlockSpec((npr, 1, tn), lambda j: (0, 0, j)),
            pl.BlockSpec((D_MODEL, GROUP_WIDTH), lambda j: (0, jnp.minimum(j, len(MAIN_GROUPS) - 1))),
            whole((D_MODEL, LANES)),
            whole((D_MODEL, GROUP_WIDTH)),
            whole((GROUP_WIDTH + GATE_ROWS, D_MODEL)),
        ],
        out_shape=[
            jax.ShapeDtypeStruct((ns, N_MOD * D_MODEL), F32),
            jax.ShapeDtypeStruct((npr, 1, N_MOD * D_MODEL), F32),
            jax.ShapeDtypeStruct((D_MODEL, N_MAIN), BF16),
            jax.ShapeDtypeStruct((D_MODEL, LANES), BF16),
            jax.ShapeDtypeStruct((D_MODEL, GROUP_WIDTH), BF16),
            jax.ShapeDtypeStruct((GROUP_WIDTH + GATE_ROWS, D_MODEL), BF16),
        ],
        compiler_params=pltpu.CompilerParams(dimension_semantics=("arbitrary",), vmem_limit_bytes=VMEM_LIMIT),
        name="modulation",
    )(c_sample, c_prompt, w_ada, b_ada, w_in_t, w_in_t)


def _mod_spec(piece, tm, per_row):
    if per_row:
        return pl.BlockSpec((None, tm, D_MODEL), lambda g, t: (g, t, piece))
    return pl.BlockSpec((None, 1, D_MODEL), lambda g, t: (g, 0, piece))


def _const_spec(shape):
    nd = len(shape)
    return pl.BlockSpec(shape, lambda g, t: (0,) * nd, pipeline_mode=pl.Buffered(1))


PROJ_TILE = 1024


def _proj_body(x_ref, sh_ref, sc_ref, g1_ref, w_ref, wg_ref, lng_ref, lnb_ref, a_ref, b_ref, g_ref):
    n_a = a_ref.shape[1]
    n_copy = n_a - GROUP_WIDTH
    x = x_ref[...]
    h = _rms(x) * g1_ref[...]
    h = (h * (1.0 + sc_ref[...]) + sh_ref[...]).astype(BF16)
    p = _dot(h, w_ref[...])
    a_ref[:, :n_copy] = p[:, :n_copy].astype(a_ref.dtype)
    vs = _gelu_tanh(p[:, n_copy:n_a])
    for hd in range(HEADS):
        sl = slice(hd * HEAD_DIM, (hd + 1) * HEAD_DIM)
        v = vs[:, sl]
        mu = jnp.mean(v, axis=-1, keepdims=True)
        vc = v - mu
        var = jnp.mean(vc * vc, axis=-1, keepdims=True)
        y = vc * lax.rsqrt(var + EPS) * lng_ref[:, sl] + lnb_ref[:, sl]
        a_ref[:, n_copy + hd * HEAD_DIM: n_copy + (hd + 1) * HEAD_DIM] = y.astype(a_ref.dtype)
    b_ref[:, :GROUP_WIDTH] = _sigmoid(p[:, n_a: n_a + GROUP_WIDTH])
    b_ref[:, GROUP_WIDTH:] = _gelu_tanh(p[:, n_a + GROUP_WIDTH: n_a + N_B])
    if wg_ref is not None:
        g_ref[...] = _dot(h, wg_ref[...])
    return h


def _proj_kernel(*refs, k_transposed, n_cast):
    ins, outs = refs[:9 + n_cast], refs[9 + n_cast:]
    wk_ref, k_ref = ins[8], outs[3]
    if k_transposed:
        h = _proj_body(*ins[:5], None, *ins[6:8], *outs[:3])
        kt_gates = _dot_nt(wk_ref[...], h)
        k_ref[...] = kt_gates[:GROUP_WIDTH].astype(k_ref.dtype)
        outs[2][...] = kt_gates[GROUP_WIDTH:]
    else:
        h = _proj_body(*ins[:8], *outs[:3])
        k_ref[...] = _dot(h, wk_ref[...]).astype(k_ref.dtype)
    for src, dst in zip(ins[9:], outs[4:]):
        dst[...] = src[...].astype(dst.dtype)


def _project(x, mod, g1, w_main, w_gates, ln_g, ln_b, w_k, cast_weights=(), *, k_transposed, tm, per_row, a_dtype):
    groups, t, _ = x.shape
    n_a = w_main.shape[1] - N_B
    steps_per_group = t // tm
    n_steps = groups * steps_per_group
    in_specs = [
        pl.BlockSpec((None, tm, D_MODEL), lambda g, i: (g, i, 0)),
        _mod_spec(0, tm, per_row),
        _mod_spec(1, tm, per_row),
        _const_spec((1, D_MODEL)),
        _const_spec(w_main.shape),
        _const_spec(w_gates.shape),
        _const_spec((1, GROUP_WIDTH)),
        _const_spec((1, GROUP_WIDTH)),
    ]
    out_specs = [
        pl.BlockSpec((None, tm, n_a), lambda g, i: (g, i, 0)),
        pl.BlockSpec((None, tm, N_B), lambda g, i: (g, i, 0)),
    ]
    out_shape = [
        jax.ShapeDtypeStruct((groups, t, n_a), a_dtype),
        jax.ShapeDtypeStruct((groups, t, N_B), F32),
    ]
    if k_transposed:
        out_specs.append(pl.BlockSpec((None, GATE_ROWS, tm), lambda g, i: (g, 0, i)))
        out_shape.append(jax.ShapeDtypeStruct((groups, GATE_ROWS, t), F32))
    else:
        out_specs.append(pl.BlockSpec((None, tm, LANES), lambda g, i: (g, i, 0)))
        out_shape.append(jax.ShapeDtypeStruct((groups, t, LANES), F32))
    args = [x, mod, mod, g1, w_main, w_gates, ln_g, ln_b, w_k]
    in_specs.append(_const_spec(w_k.shape))
    if k_transposed:
        out_specs.append(pl.BlockSpec((None, GROUP_WIDTH, tm), lambda g, i: (g, 0, i)))
        out_shape.append(jax.ShapeDtypeStruct((groups, GROUP_WIDTH, t), a_dtype))
    else:
        out_specs.append(pl.BlockSpec((None, tm, GROUP_WIDTH), lambda g, i: (g, i, 0)))
        out_shape.append(jax.ShapeDtypeStruct((groups, t, GROUP_WIDTH), a_dtype))
    for w in cast_weights:
        rows, cols = w.shape
        assert rows % (n_steps * 16) == 0, "row block must be a whole number of bf16 sublane tiles"
        spec = pl.BlockSpec((rows // n_steps, cols), lambda g, i: (g * steps_per_group + i, 0))
        in_specs.append(spec)
        out_specs.append(spec)
        out_shape.append(jax.ShapeDtypeStruct((rows, cols), BF16))
        args.append(w)
    return pl.pallas_call(
        functools.partial(_proj_kernel, k_transposed=k_transposed, n_cast=len(cast_weights)),
        grid=(groups, t // tm),
        in_specs=in_specs,
        out_specs=out_specs,
        out_shape=out_shape,
        compiler_params=pltpu.CompilerParams(
            dimension_semantics=("arbitrary", "arbitrary"), vmem_limit_bytes=VMEM_LIMIT),
        name="project",
    )(*args)


MIX_TILE = 512


def _dot3_lhs(lhs_f32, rhs_bf16):
    hi, mid, lo = _split3_bf16(lhs_f32)
    return _dot(hi, rhs_bf16) + _dot(mid, rhs_bf16) + _dot(lo, rhs_bf16)


FF_CHUNKS = (768, 768, 768, 512)
MIXER_LOOKAHEAD = 2
OUT_ROW_BLOCKS = 2


def _gate_rows(g_ref, bias_ref, triu, n_chunks):
    n_gates = 2 * HEADS
    assert n_chunks * n_gates <= LANES
    rows = [g_ref[0:n_gates, c * CHUNK:(c + 1) * CHUNK] + bias_ref[...] for c in range(n_chunks)]
    rows_all = jnp.concatenate(rows, axis=0)
    b_rows_all = _dot3_lhs(_log_sigmoid(rows_all), triu)
    bc_all = jnp.concatenate([b_rows_all, jnp.zeros((LANES - n_chunks * n_gates, CHUNK), F32)], axis=0).T
    bc, arow, blast = [], [], []
    for c in range(n_chunks):
        bc.append(bc_all[:, c * n_gates:(c + 1) * n_gates])
        b_rows = b_rows_all[c * n_gates:(c + 1) * n_gates, :]
        arow.append([rows[c][hd:hd + 1, :] - b_rows[HEADS + hd:HEADS + hd + 1, :] for hd in range(HEADS)])
        blast.append([jnp.min(b_rows[HEADS + hd:HEADS + hd + 1, :], axis=1, keepdims=True) for hd in range(HEADS)])
    return bc, arow, blast


def _mlstm_local(q, kt, a_row, b_last, causal):
    amat = jnp.where(causal, a_row, -jnp.inf)
    m_row = jnp.max(amat, axis=1, keepdims=True)
    s_loc = (_dot(q, kt) * jnp.exp(amat - m_row)).astype(BF16)
    g_row = b_last + a_row
    g_loc = jnp.max(g_row, axis=1, keepdims=True)
    kw = (kt.astype(F32) * jnp.exp(g_row - g_loc)).astype(BF16)
    return m_row, s_loc, g_loc, kw


def _mlstm_readout(local, q, v, og, g_head, b_last, b_rep, cta, m_prev, ones_blk):
    m_row, s_loc, g_loc, kw = local
    va = jnp.concatenate([v, ones_blk], axis=1)
    nd_loc = _dot(s_loc, va)
    u_aug = _dot(kw, va)
    inter = _dot(q, cta.astype(BF16))
    mm = jnp.maximum(m_prev, m_row)
    f_loc = jnp.exp(m_row - mm) * QK_SCALE
    f_int = jnp.exp(m_prev - mm) * QK_SCALE
    nd = f_loc * nd_loc + f_int * inter
    clamp = jnp.exp(-(b_rep + mm))
    hh = nd[:, :HEAD_DIM] / jnp.maximum(jnp.abs(nd[:, HEAD_DIM:]), clamp)
    ml = _rms(hh) * g_head * og

    dec = b_last + m_prev
    m_new = jnp.maximum(dec, g_loc)
    cta_new = jnp.exp(dec - m_new) * cta + jnp.exp(g_loc - m_new) * u_aug
    return ml, cta_new, m_new


def _prompt_tail_kernel(a_ref, kt_ref, b_ref, g_ref, x_ref, gt1_ref, sh2_ref, sc2_ref, gt2_ref,
                        bias_ref, gh_ref, ws_ref, bs_ref, wo_ref, g2_ref, wgu_ref, wdn_ref, gf_ref,
                        y_ref, c_out, n_out, m_out,
                        mix_ref, cta_ref, m_ref, *, tiles_per_seq):
    step = pl.program_id(0)
    n_tiles = pl.num_programs(0) - 1
    refs = (a_ref, kt_ref, b_ref, g_ref, x_ref, gt1_ref, sh2_ref, sc2_ref, gt2_ref,
            bias_ref, gh_ref, ws_ref, bs_ref, wo_ref, g2_ref, wgu_ref, wdn_ref, gf_ref,
            y_ref, mix_ref, cta_ref, m_ref)

    @pl.when(step % tiles_per_seq == 0)
    def _():
        cta_ref[...] = jnp.zeros_like(cta_ref)
        m_ref[...] = jnp.zeros_like(m_ref)

    @pl.when(step == 0)
    def _():
        _prompt_tail_body(refs, with_mixer=True, with_out=False)

    @pl.when(jnp.logical_and(step > 0, step < n_tiles))
    def _():
        _prompt_tail_body(refs, with_mixer=True, with_out=True)

    @pl.when(step == n_tiles)
    def _():
        _prompt_tail_body(refs, with_mixer=False, with_out=True)

    @pl.when(jnp.logical_and(step % tiles_per_seq == tiles_per_seq - 1, step < n_tiles))
    def _():
        for hd in range(HEADS):
            cta = cta_ref[hd]
            c_out[hd] = cta[:, :HEAD_DIM].T
            n_out[hd: hd + 1, :] = cta[:, HEAD_DIM:].T[0:1, :]
            m_out[hd: hd + 1, :] = m_ref[hd][0:1, :]


def _prompt_tail_body(refs, *, with_mixer, with_out):
    (a_ref, kt_ref, b_ref, g_ref, x_ref, gt1_ref, sh2_ref, sc2_ref, gt2_ref,
     bias_ref, gh_ref, ws_ref, bs_ref, wo_ref, g2_ref, wgu_ref, wdn_ref, gf_ref,
     y_ref, mix_ref, cta_ref, m_ref) = refs
    tm = a_ref.shape[0]
    n_chunks = tm // CHUNK
    n_copy = a_ref.shape[1] - GROUP_WIDTH

    row = lax.broadcasted_iota(jnp.int32, (CHUNK, CHUNK), 0)
    col = lax.broadcasted_iota(jnp.int32, (CHUNK, CHUNK), 1)
    causal = row >= col
    triu = jnp.where(row <= col, 1.0, 0.0).astype(BF16)
    ones_blk = jnp.ones((CHUNK, HEAD_DIM), BF16)

    blocks = [slice(r0, r0 + tm // OUT_ROW_BLOCKS) for r0 in range(0, tm, tm // OUT_ROW_BLOCKS)] if with_out else []
    outproj = [_dot(mix_ref[rs, :], wo_ref[...]) for rs in blocks]
    x1, h2 = [], []
    for rs, op in zip(blocks, outproj):
        x1.append(x_ref[rs, :] + gt1_ref[...] * op)
        h = _rms(x1[-1]) * g2_ref[...]
        h2.append((h * (1.0 + sc2_ref[...]) + sh2_ref[...]).astype(BF16))

    if with_mixer:
        bc, arow, blast = _gate_rows(g_ref, bias_ref, triu, n_chunks)
        state = [(cta_ref[hd], jnp.max(m_ref[hd][0:1, :], axis=1, keepdims=True)) for hd in range(HEADS)]
    sg_all = [None] * HEADS

    def mixer_local(c, hd):
        rs = slice(c * CHUNK, (c + 1) * CHUNK)
        sl = slice(hd * HEAD_DIM, (hd + 1) * HEAD_DIM)
        return _mlstm_local(a_ref[rs, sl], kt_ref[sl, rs], arow[c][hd], blast[c][hd], causal)

    def mixer_readout(c, hd, local):
        rs = slice(c * CHUNK, (c + 1) * CHUNK)
        sl = slice(hd * HEAD_DIM, (hd + 1) * HEAD_DIM)
        if c == 0:
            vsn_all = jnp.concatenate(
                [a_ref[cc * CHUNK:(cc + 1) * CHUNK, n_copy + hd * HEAD_DIM: n_copy + (hd + 1) * HEAD_DIM]
                 for cc in range(n_chunks)], axis=1)
            sg_all[hd] = _dot(ws_ref[hd], vsn_all)
        cta, m_prev = state[hd]
        b_rep = jnp.broadcast_to(bc[c][:, HEADS + hd: HEADS + hd + 1], (CHUNK, HEAD_DIM))
        ml, cta, m_prev = _mlstm_readout(
            local, a_ref[rs, sl], a_ref[rs, GROUP_WIDTH + hd * HEAD_DIM: GROUP_WIDTH + (hd + 1) * HEAD_DIM],
            b_ref[rs, sl], gh_ref[:, sl], blast[c][hd], b_rep, cta, m_prev, ones_blk)
        state[hd] = (cta, m_prev)
        mix_ref[rs, sl] = ml.astype(mix_ref.dtype)
        sg = sg_all[hd][:, c * CHUNK:(c + 1) * CHUNK] + bs_ref[hd]
        ug = b_ref[rs, GROUP_WIDTH + hd * HEAD_DIM: GROUP_WIDTH + (hd + 1) * HEAD_DIM]
        mix_ref[rs, GROUP_WIDTH + hd * HEAD_DIM: GROUP_WIDTH + (hd + 1) * HEAD_DIM] = (ug * sg).astype(mix_ref.dtype)

    pieces = [(c, hd) for c in range(n_chunks) for hd in range(HEADS)] if with_mixer else []
    per_ff = -(-len(pieces) // len(FF_CHUNKS))
    ahead = [mixer_local(*pieces[p]) for p in range(min(MIXER_LOOKAHEAD, len(pieces)))]
    acc = [None] * len(blocks)
    f0 = 0
    for j, width in enumerate(FF_CHUNKS):
        gate_up = [(_dot(h, wgu_ref[:, f0:f0 + width]), _dot(h, wgu_ref[:, D_FF + f0:D_FF + f0 + width]))
                   for h in h2]
        for idx in range(j * per_ff, min((j + 1) * per_ff, len(pieces))):
            if idx + MIXER_LOOKAHEAD < len(pieces):
                ahead.append(mixer_local(*pieces[idx + MIXER_LOOKAHEAD]))
            mixer_readout(*pieces[idx], ahead.pop(0))
        for r, (gate, up) in enumerate(gate_up):
            act = (gate * _sigmoid(gate) * up).astype(BF16)
            part = _dot(act, wdn_ref[f0:f0 + width, :])
            acc[r] = part if acc[r] is None else acc[r] + part
            if j + 1 == len(FF_CHUNKS):
                x2 = x1[r] + gt2_ref[...] * acc[r]
                y_ref[blocks[r], :] = _rms(x2) * gf_ref[...]
        f0 += width

    if with_mixer:
        for hd in range(HEADS):
            cta_ref[hd] = state[hd][0]
            m_ref[hd] = jnp.broadcast_to(state[hd][1], m_ref.shape[1:])


def _prompt_tail(a, kt, b, gates, x, mod, bias_row, g_head, ws_tril, bs_rep, w_out, g2, w_gu, w_dn, g_final):
    groups, t, n_a = a.shape
    tm = MIX_TILE
    tps = t // tm
    n_tiles = groups * tps
    assert sum(FF_CHUNKS) == D_FF

    def cur(i):
        return jnp.minimum(i, n_tiles - 1)

    def prev(i):
        return jnp.maximum(i - 1, 0)

    def rows(tile, width):
        return pl.BlockSpec((None, tm, width), lambda i: (tile(i) // tps, tile(i) % tps, 0))

    def mod_piece(piece):
        return pl.BlockSpec((None, 1, D_MODEL), lambda i: (prev(i) // tps, 0, piece))

    def const(shape):
        nd = len(shape)
        return pl.BlockSpec(shape, lambda i: (0,) * nd, pipeline_mode=pl.Buffered(1))

    def per_seq(shape):
        nd = len(shape)
        return pl.BlockSpec((None,) + shape, lambda i: (cur(i) // tps,) + (0,) * nd)

    return pl.pallas_call(
        functools.partial(_prompt_tail_kernel, tiles_per_seq=tps),
        grid=(n_tiles + 1,),
        in_specs=[
            rows(cur, n_a),
            pl.BlockSpec((None, GROUP_WIDTH, tm), lambda i: (cur(i) // tps, 0, cur(i) % tps)),
            rows(cur, N_B),
            pl.BlockSpec((None, GATE_ROWS, tm), lambda i: (cur(i) // tps, 0, cur(i) % tps)),
            rows(prev, D_MODEL),
            mod_piece(2), mod_piece(3), mod_piece(4), mod_piece(5),
            const((2 * HEADS, LANES)),
            const((1, GROUP_WIDTH)),
            const((HEADS, CHUNK, CHUNK)),
            const((HEADS, CHUNK, LANES)),
            const((D_MODEL, D_MODEL)),
            const((1, D_MODEL)),
            const((D_MODEL, 2 * D_FF)),
            const((D_FF, D_MODEL)),
            const((1, D_MODEL)),
        ],
        out_specs=[
            rows(prev, D_MODEL),
            per_seq((HEADS, HEAD_DIM, HEAD_DIM)),
            per_seq((HEADS, HEAD_DIM)),
            per_seq((HEADS, LANES)),
        ],
        out_shape=[
            jax.ShapeDtypeStruct((groups, t, D_MODEL), F32),
            jax.ShapeDtypeStruct((groups, HEADS, HEAD_DIM, HEAD_DIM), F32),
            jax.ShapeDtypeStruct((groups, HEADS, HEAD_DIM), F32),
            jax.ShapeDtypeStruct((groups, HEADS, LANES), F32),
        ],
        scratch_shapes=[
            pltpu.VMEM((tm, D_MODEL), BF16),
            pltpu.VMEM((HEADS, HEAD_DIM, 2 * HEAD_DIM), F32),
            pltpu.VMEM((HEADS, F32_SUBLANES, LANES), F32),
        ],
        compiler_params=pltpu.CompilerParams(
            dimension_semantics=("arbitrary",), vmem_limit_bytes=VMEM_LIMIT),
        name="prompt_tail",
    )(a, kt, b, gates, x, mod, mod, mod, mod, bias_row, g_head, ws_tril, bs_rep, w_out, g2, w_gu, w_dn, g_final)


SAMPLE_TOKENS_PER_STEP = 128
SAMPLE_UNROLL = 8


def _mix_sample_kernel(q_ref, k_ref, v_ref, vsn_ref, og_ref, ug_ref, g_ref, m0_ref, n0_ref, c_ref,
                       bias_ref, gh_ref, ws0_ref, bs0_ref,
                       mix_ref, c_out, n_out, m_out,
                       vt_ref, kp_ref, wd_ref, cqt_ref):
    hd = pl.program_id(0)
    grp = pl.program_id(1)
    nb = q_ref.shape[0]
    lane = lax.broadcasted_iota(jnp.int32, (nb, LANES), 1)

    def gate_terms():
        pre = g_ref[...] + bias_ref[...]
        i_pre = jnp.sum(jnp.where(lane == hd, pre, 0.0), axis=1, keepdims=True)
        f_pre = jnp.sum(jnp.where(lane == hd + HEADS, pre, 0.0), axis=1, keepdims=True)
        m_prev = jnp.sum(jnp.where(lane == hd, m0_ref[...], 0.0), axis=1, keepdims=True)
        inter = _log_sigmoid(f_pre) + m_prev
        m_t = jnp.maximum(inter, i_pre)
        return m_t, jnp.exp(i_pre - m_t), jnp.exp(inter - m_t)

    @pl.when(grp == 0)
    def _():
        _, w_in, w_dec = gate_terms()
        vt_ref[...] = v_ref[...].T
        kp_ref[...] = (w_in * k_ref[...]).astype(BF16)
        wd_ref[...] = jnp.broadcast_to(w_dec, wd_ref.shape)
        cqt_ref[...] = jnp.zeros_like(cqt_ref)

    lane_sq = lax.broadcasted_iota(jnp.int32, (HEAD_DIM, LANES), 1)
    tokens = c_ref.shape[0]

    def body(r, cqt):
        tok = grp * tokens + r
        q_row = q_ref[pl.ds(tok, 1), :]
        c_b = c_ref[r]
        sel = lane_sq == tok
        cq_col = jnp.sum(c_b * q_row, axis=1, keepdims=True)
        v_sel = jnp.where(sel, vt_ref[...], 0.0).astype(BF16)
        outer = _dot(v_sel, kp_ref[...])
        c_out[r] = wd_ref[pl.ds(tok, 1), :] * c_b + outer
        return jnp.where(sel, cq_col, cqt)

    cqt_ref[...] = lax.fori_loop(0, tokens, body, cqt_ref[...], unroll=SAMPLE_UNROLL)

    @pl.when(grp == pl.num_programs(1) - 1)
    def _():
        m_t, w_in, w_dec = gate_terms()
        q = q_ref[...]
        k = k_ref[...]
        v = v_ref[...]
        n0 = n0_ref[...]
        cq = cqt_ref[...].T
        s = jnp.sum(q * k, axis=1, keepdims=True) * (QK_SCALE * w_in)
        w_inter = w_dec * QK_SCALE
        num = s * v + w_inter * cq
        den = s + w_inter * jnp.sum(n0 * q, axis=1, keepdims=True)
        hh = num / jnp.maximum(jnp.abs(den), jnp.exp(-m_t))
        ml = _rms(hh) * gh_ref[...] * og_ref[...]
        cm = ug_ref[...] * (ws0_ref[...] * vsn_ref[...] + bs0_ref[...])
        n_out[...] = w_dec * n0 + w_in * k
        m_out[...] = jnp.broadcast_to(m_t, m_out.shape)
        for hh_static in range(HEADS):
            @pl.when(hd == hh_static)
            def _():
                mix_ref[:, hh_static * HEAD_DIM: (hh_static + 1) * HEAD_DIM] = ml
                mix_ref[:, GROUP_WIDTH + hh_static * HEAD_DIM: GROUP_WIDTH + (hh_static + 1) * HEAD_DIM] = cm


def _mix_sample(a, k, b, gates, m0_pad, n0, c0, bias_row, g_head, ws0_row, bs0_row):
    nb = a.shape[0]
    tb = SAMPLE_TOKENS_PER_STEP

    def head_block(offset):
        return pl.BlockSpec((nb, HEAD_DIM), lambda h, g: (0, offset + h))

    full = pl.BlockSpec((nb, LANES), lambda h, g: (0, 0))
    head_row = pl.BlockSpec((1, HEAD_DIM), lambda h, g: (0, h))
    c_spec = pl.BlockSpec((tb, None, HEAD_DIM, HEAD_DIM), lambda h, g: (g, h, 0, 0))
    return pl.pallas_call(
        _mix_sample_kernel,
        grid=(HEADS, nb // tb),
        in_specs=[
            head_block(0), head_block(0), head_block(HEADS), head_block(2 * HEADS),
            head_block(0), head_block(HEADS),
            full, full, head_block(0), c_spec,
            pl.BlockSpec((1, LANES), lambda h, g: (0, 0)),
            head_row, head_row, head_row,
        ],
        out_specs=[
            pl.BlockSpec((nb, D_MODEL), lambda h, g: (0, 0)),
            c_spec,
            head_block(0),
            head_block(0),
        ],
        out_shape=[
            jax.ShapeDtypeStruct((nb, D_MODEL), F32),
            jax.ShapeDtypeStruct(c0.shape, F32),
            jax.ShapeDtypeStruct((nb, GROUP_WIDTH), F32),
            jax.ShapeDtypeStruct((nb, GROUP_WIDTH), F32),
        ],
        scratch_shapes=[
            pltpu.VMEM((HEAD_DIM, nb), F32),
            pltpu.VMEM((nb, HEAD_DIM), BF16),
            pltpu.VMEM((nb, LANES), F32),
            pltpu.VMEM((HEAD_DIM, nb), F32),
        ],
        compiler_params=pltpu.CompilerParams(
            dimension_semantics=("arbitrary", "arbitrary"), vmem_limit_bytes=VMEM_LIMIT),
        name="mix_sample",
    )(a, k, a, a, b, b, gates, m0_pad, n0, c0, bias_row, g_head, ws0_row, bs0_row)


def _out_kernel(x_ref, mix_ref, gt1_ref, sh2_ref, sc2_ref, gt2_ref, wo_ref, g2_ref, wgu_ref, wdn_ref, gf_ref,
                y_ref):
    x1 = x_ref[...] + gt1_ref[...] * _dot(mix_ref[...].astype(BF16), wo_ref[...])
    h2 = _rms(x1) * g2_ref[...]
    h2 = (h2 * (1.0 + sc2_ref[...]) + sh2_ref[...]).astype(BF16)
    gate = _dot(h2, wgu_ref[:, :D_FF])
    up = _dot(h2, wgu_ref[:, D_FF:])
    act = (gate * _sigmoid(gate) * up).astype(BF16)
    x2 = x1 + gt2_ref[...] * _dot(act, wdn_ref[...])
    y_ref[:, 0, :] = _rms(x2) * gf_ref[...]


def _output_stage(x, mix, mod, w_out, g2, w_gu, w_dn, g_final, *, tm, per_row):
    groups, t, _ = x.shape
    return pl.pallas_call(
        _out_kernel,
        grid=(groups, t // tm),
        in_specs=[
            pl.BlockSpec((None, tm, D_MODEL), lambda g, i: (g, i, 0)),
            pl.BlockSpec((None, tm, D_MODEL), lambda g, i: (g, i, 0)),
            _mod_spec(2, tm, per_row),
            _mod_spec(3, tm, per_row),
            _mod_spec(4, tm, per_row),
            _mod_spec(5, tm, per_row),
            _const_spec((D_MODEL, D_MODEL)),
            _const_spec((1, D_MODEL)),
            _const_spec((D_MODEL, 2 * D_FF)),
            _const_spec((D_FF, D_MODEL)),
            _const_spec((1, D_MODEL)),
        ],
        out_specs=pl.BlockSpec((tm, 1, D_MODEL), lambda g, i: (g * (t // tm) + i, 0, 0)),
        out_shape=jax.ShapeDtypeStruct((groups * t, 1, D_MODEL), F32),
        compiler_params=pltpu.CompilerParams(
            dimension_semantics=("arbitrary", "arbitrary"), vmem_limit_bytes=VMEM_LIMIT),
        name="output_stage",
    )(x, mix, mod, mod, mod, mod, w_out, g2, w_gu, w_dn, g_final)


def kernel(x_prompt, x_sample, c_prompt, c_sample, state_mlstm_C, state_mlstm_n, state_mlstm_m, w_ada, b_ada, g_norm1, w_in, b_gate, g_mlstm_head, ln_v_g, ln_v_b, w_s, b_s, w_out, g_norm2, w_gate_up, w_down, g_final):
    depth = w_ada.shape[0]
    assert depth == 1, "single-layer trunk"
    batch, seq, _ = x_prompt.shape
    nb = x_sample.shape[0]
    assert x_sample.shape[1] == 1

    gw = GROUP_WIDTH
    g1 = g_norm1[0].reshape(1, D_MODEL)
    g2 = g_norm2[0].reshape(1, D_MODEL)
    gf = g_final.reshape(1, D_MODEL)
    ln_g = ln_v_g[0].reshape(1, gw)
    ln_b = ln_v_b[0].reshape(1, gw)
    g_head = g_mlstm_head[0].reshape(1, gw)
    bias_row = jnp.pad(b_gate[0], (0, LANES - 2 * HEADS)).reshape(1, LANES)
    bias_rep = jnp.broadcast_to(b_gate[0][:, None], (2 * HEADS, LANES))
    tril = jnp.tril(jnp.ones((CHUNK, CHUNK), dtype=bool))
    ws_tril = jnp.where(tril[None], w_s[0], 0.0).astype(BF16)
    bs_rep = jnp.broadcast_to(b_s[0][:, :, None], (HEADS, CHUNK, LANES))
    ws0_row = jnp.repeat(w_s[0][:, 0, 0], HEAD_DIM).reshape(1, gw)
    bs0_row = jnp.repeat(b_s[0][:, 0], HEAD_DIM).reshape(1, gw)

    mod_s, mod_p, w_main, w_gates, w_k, w_kt = _modulation(c_sample, c_prompt, w_ada[0], b_ada[0].reshape(1, -1), w_in[0].T)
    mod_s = mod_s.reshape(1, nb, N_MOD * D_MODEL)

    a_p, b_p, g_p, kt_p, w_out_b, w_gu_b, w_dn_b = _project(
        x_prompt, mod_p, g1, w_main, w_gates, ln_g, ln_b, w_kt, (w_out[0], w_gate_up[0], w_down[0]),
        k_transposed=True, tm=PROJ_TILE, per_row=False, a_dtype=BF16)
    y_p, c_p, n_p, m_p = _prompt_tail(a_p, kt_p, b_p, g_p, x_prompt, mod_p, bias_rep, g_head, ws_tril, bs_rep,
                                      w_out_b, g2, w_gu_b, w_dn_b, gf)

    xs = x_sample.reshape(1, nb, D_MODEL)
    a_s, b_s_act, g_s, k_s = _project(xs, mod_s, g1, w_main, w_gates, ln_g, ln_b, w_k, k_transposed=False,
                                      tm=nb, per_row=True, a_dtype=F32)
    a_s2, b_s2, g_s2 = a_s[0], b_s_act[0], g_s[0]
    m0_pad = jnp.pad(state_mlstm_m[0], ((0, 0), (0, LANES - HEADS)))
    n0 = state_mlstm_n[0].reshape(nb, gw)
    mix_s, c_s, n_s, m_s = _mix_sample(a_s2, k_s[0], b_s2, g_s2, m0_pad, n0, state_mlstm_C[0],
                                       bias_row, g_head, ws0_row, bs0_row)
    y_s = _output_stage(xs, mix_s.reshape(1, nb, D_MODEL), mod_s, w_out_b, g2, w_gu_b, w_dn_b, gf,
                        tm=nb, per_row=True)

    return (
        y_p,
        y_s,
        c_p[None],
        n_p[None],
        m_p[:, :, 0][None],
        c_s[None],
        n_s.reshape(nb, HEADS, HEAD_DIM)[None],
        m_s.reshape(nb, HEADS, HEAD_DIM)[:, :, 0][None],
        a_s2[:, 2 * gw:].reshape(nb, 1, HEADS, HEAD_DIM)[None],
    )
```

```python
import functools
import math

import jax
import jax.numpy as jnp
from jax import lax
from jax.experimental import pallas as pl
from jax.experimental.pallas import tpu as pltpu

F32 = jnp.float32
BF16 = jnp.bfloat16

D_MODEL = 1024
HEADS = 4
HEAD_DIM = 128
GROUP_WIDTH = HEADS * HEAD_DIM
CHUNK = 128
D_FF = 2816
N_MOD = 6
EPS = 1e-6
QK_SCALE = HEAD_DIM ** -0.5
LANES = 128
F32_SUBLANES = 8
BF16_SUBLANES = 16

N_B = 2 * GROUP_WIDTH

VMEM_LIMIT = 56 * 1024 * 1024


def _dot(a, b):
    return jnp.dot(a, b, preferred_element_type=F32)


def _dot_nt(a, b):
    return lax.dot_general(a, b, (((1,), (1,)), ((), ())), preferred_element_type=F32)


def _sigmoid(x):
    return 1.0 / (1.0 + jnp.exp(-x))


def _gelu_tanh(x):
    c = math.sqrt(2.0 / math.pi)
    return x * (0.5 * (1.0 + jnp.tanh(c * (x + 0.044715 * (x * x * x)))))


def _log_sigmoid(x):
    return jnp.minimum(x, 0.0) - jnp.log1p(jnp.exp(-jnp.abs(x)))


def _rms(x):
    return x * lax.rsqrt(jnp.mean(x * x, axis=-1, keepdims=True) + EPS)


def _split3_bf16(x):
    hi = x.astype(BF16)
    r1 = x - hi.astype(F32)
    mid = r1.astype(BF16)
    lo = (r1 - mid.astype(F32)).astype(BF16)
    return hi, mid, lo


MOD_TILE = 1024

IN_Q, IN_K, IN_V, IN_O, IN_GATES = 0, GROUP_WIDTH, 2 * GROUP_WIDTH, 3 * GROUP_WIDTH, 4 * GROUP_WIDTH
IN_U = IN_GATES + 2 * HEADS
IN_VS = IN_U + GROUP_WIDTH
N_IN = IN_VS + GROUP_WIDTH
MAIN_GROUPS = (IN_Q, IN_V, IN_VS, IN_O, IN_U)
N_MAIN = len(MAIN_GROUPS) * GROUP_WIDTH
GATE_ROWS = BF16_SUBLANES


def _mod_kernel(cs_ref, cp_ref, w_ref, b_ref, grp_ref, gate_ref, os_ref, op_ref, wmain_ref, wgate_ref, wk_ref, wkt_ref):
    step = pl.program_id(0)
    ns, npr = cs_ref.shape[0], cp_ref.shape[0]
    pad = -(ns + npr) % BF16_SUBLANES
    c = jnp.concatenate([cs_ref[...], cp_ref[...], jnp.zeros((pad, D_MODEL), F32)], axis=0)
    a = (c * _sigmoid(c)).astype(BF16)
    res = _dot(a, w_ref[...].astype(BF16)) + b_ref[...]
    os_ref[...] = res[:ns]
    op_ref[:, 0, :] = res[ns:ns + npr]

    def transposed_into(dst_ref):
        for blk in range(GROUP_WIDTH // LANES):
            rows = grp_ref[blk * LANES:(blk + 1) * LANES, :]
            dst_ref[:, blk * LANES:(blk + 1) * LANES] = rows.T.astype(BF16)

    @pl.when(step < len(MAIN_GROUPS))
    def _():
        transposed_into(wmain_ref)

    @pl.when(step == len(MAIN_GROUPS))
    def _():
        wkt_ref[:GROUP_WIDTH, :] = grp_ref[...].astype(BF16)
        wkt_ref[GROUP_WIDTH:, :] = jnp.concatenate(
            [gate_ref[...], jnp.zeros((GATE_ROWS - 2 * HEADS, D_MODEL), F32)], axis=0).astype(BF16)
        transposed_into(wk_ref)

    @pl.when(step == 0)
    def _():
        gate_rows = jnp.concatenate([gate_ref[...], jnp.zeros((LANES - 2 * HEADS, D_MODEL), F32)], axis=0)
        wgate_ref[...] = gate_rows.T.astype(BF16)


def _modulation(c_sample, c_prompt, w_ada, b_ada, w_in_t):
    ns, npr = c_sample.shape[0], c_prompt.shape[0]
    tn = MOD_TILE
    steps = N_MOD * D_MODEL // tn
    group_starts = MAIN_GROUPS + (IN_K,)
    assert w_in_t.shape == (N_IN, D_MODEL) and steps == len(group_starts)

    def whole(shape):
        return pl.BlockSpec(shape, lambda j: (0, 0))

    def group_start(j):
        assert all(start % F32_SUBLANES == 0 for start in group_starts)
        return pl.multiple_of(sum(jnp.where(j == i, start, 0) for i, start in enumerate(group_starts)), F32_SUBLANES)

    return pl.pallas_call(
        _mod_kernel,
        grid=(steps,),
        in_specs=[
            whole((ns, D_MODEL)),
            whole((npr, D_MODEL)),
            pl.BlockSpec((D_MODEL, tn), lambda j: (0, j)),
            pl.BlockSpec((1, tn), lambda j: (0, j)),
            pl.BlockSpec((pl.Element(GROUP_WIDTH), pl.Element(D_MODEL)), lambda j: (group_start(j), 0)),
            pl.BlockSpec((pl.Element(2 * HEADS), pl.Element(D_MODEL)), lambda j: (IN_GATES, 0)),
        ],
        out_specs=[
            pl.BlockSpec((ns, tn), lambda j: (0, j)),
            pl.BlockSpec((npr, 1, tn), lambda j: (0, 0, j)),
            pl.BlockSpec((D_MODEL, GROUP_WIDTH), lambda j: (0, jnp.minimum(j, len(MAIN_GROUPS) - 1))),
            whole((D_MODEL, LANES)),
            whole((D_MODEL, GROUP_WIDTH)),
            whole((GROUP_WIDTH + GATE_ROWS, D_MODEL)),
        ],
        out_shape=[
            jax.ShapeDtypeStruct((ns, N_MOD * D_MODEL), F32),
            jax.ShapeDtypeStruct((npr, 1, N_MOD * D_MODEL), F32),
            jax.ShapeDtypeStruct((D_MODEL, N_MAIN), BF16),
            jax.ShapeDtypeStruct((D_MODEL, LANES), BF16),
            jax.ShapeDtypeStruct((D_MODEL, GROUP_WIDTH), BF16),
            jax.ShapeDtypeStruct((GROUP_WIDTH + GATE_ROWS, D_MODEL), BF16),
        ],
        compiler_params=pltpu.CompilerParams(dimension_semantics=("arbitrary",), vmem_limit_bytes=VMEM_LIMIT),
        name="modulation",
    )(c_sample, c_prompt, w_ada, b_ada, w_in_t, w_in_t)


def _mod_spec(piece, tm, per_row):
    if per_row:
        return pl.BlockSpec((None, tm, D_MODEL), lambda g, t: (g, t, piece))
    return pl.BlockSpec((None, 1, D_MODEL), lambda g, t: (g, 0, piece))


def _const_spec(shape):
    nd = len(shape)
    return pl.BlockSpec(shape, lambda g, t: (0,) * nd, pipeline_mode=pl.Buffered(1))


PROJ_TILE = 1024


def _proj_body(x_ref, sh_ref, sc_ref, g1_ref, w_ref, wg_ref, lng_ref, lnb_ref, a_ref, b_ref, g_ref):
    n_a = a_ref.shape[1]
    n_copy = n_a - GROUP_WIDTH
    x = x_ref[...]
    h = _rms(x) * g1_ref[...]
    h = (h * (1.0 + sc_ref[...]) + sh_ref[...]).astype(BF16)
    p = _dot(h, w_ref[...])
    a_ref[:, :n_copy] = p[:, :n_copy].astype(a_ref.dtype)
    vs = _gelu_tanh(p[:, n_copy:n_a])
    for hd in range(HEADS):
        sl = slice(hd * HEAD_DIM, (hd + 1) * HEAD_DIM)
        v = vs[:, sl]
        mu = jnp.mean(v, axis=-1, keepdims=True)
        vc = v - mu
        var = jnp.mean(vc * vc, axis=-1, keepdims=True)
        y = vc * lax.rsqrt(var + EPS) * lng_ref[:, sl] + lnb_ref[:, sl]
        a_ref[:, n_copy + hd * HEAD_DIM: n_copy + (hd + 1) * HEAD_DIM] = y.astype(a_ref.dtype)
    if wg_ref is not None:
        b_ref[:, :GROUP_WIDTH] = _sigmoid(p[:, n_a: n_a + GROUP_WIDTH])
        b_ref[:, GROUP_WIDTH:] = _gelu_tanh(p[:, n_a + GROUP_WIDTH: n_a + N_B])
        g_ref[...] = _dot(h, wg_ref[...])
    else:
        b_ref[...] = p[:, n_a: n_a + N_B]
    return h


def _proj_kernel(*refs, k_transposed, n_cast):
    ins, outs = refs[:9 + n_cast], refs[9 + n_cast:]
    wk_ref, k_ref = ins[8], outs[3]
    if k_transposed:
        h = _proj_body(*ins[:5], None, *ins[6:8], *outs[:3])
        kt_gates = _dot_nt(wk_ref[...], h)
        k_ref[...] = kt_gates[:GROUP_WIDTH].astype(k_ref.dtype)
        outs[2][...] = kt_gates[GROUP_WIDTH:]
    else:
        h = _proj_body(*ins[:8], *outs[:3])
        k_ref[...] = _dot(h, wk_ref[...]).astype(k_ref.dtype)
    for src, dst in zip(ins[9:], outs[4:]):
        dst[...] = src[...].astype(dst.dtype)


def _project(x, mod, g1, w_main, w_gates, ln_g, ln_b, w_k, cast_weights=(), *, k_transposed, tm, per_row, a_dtype):
    groups, t, _ = x.shape
    n_a = w_main.shape[1] - N_B
    steps_per_group = t // tm
    n_steps = groups * steps_per_group
    in_specs = [
        pl.BlockSpec((None, tm, D_MODEL), lambda g, i: (g, i, 0)),
        _mod_spec(0, tm, per_row),
        _mod_spec(1, tm, per_row),
        _const_spec((1, D_MODEL)),
        _const_spec(w_main.shape),
        _const_spec(w_gates.shape),
        _const_spec((1, GROUP_WIDTH)),
        _const_spec((1, GROUP_WIDTH)),
    ]
    out_specs = [
        pl.BlockSpec((None, tm, n_a), lambda g, i: (g, i, 0)),
        pl.BlockSpec((None, tm, N_B), lambda g, i: (g, i, 0)),
    ]
    out_shape = [
        jax.ShapeDtypeStruct((groups, t, n_a), a_dtype),
        jax.ShapeDtypeStruct((groups, t, N_B), F32),
    ]
    if k_transposed:
        out_specs.append(pl.BlockSpec((None, GATE_ROWS, tm), lambda g, i: (g, 0, i)))
        out_shape.append(jax.ShapeDtypeStruct((groups, GATE_ROWS, t), F32))
    else:
        out_specs.append(pl.BlockSpec((None, tm, LANES), lambda g, i: (g, i, 0)))
        out_shape.append(jax.ShapeDtypeStruct((groups, t, LANES), F32))
    args = [x, mod, mod, g1, w_main, w_gates, ln_g, ln_b, w_k]
    in_specs.append(_const_spec(w_k.shape))
    if k_transposed:
        out_specs.append(pl.BlockSpec((None, GROUP_WIDTH, tm), lambda g, i: (g, 0, i)))
        out_shape.append(jax.ShapeDtypeStruct((groups, GROUP_WIDTH, t), a_dtype))
    else:
        out_specs.append(pl.BlockSpec((None, tm, GROUP_WIDTH), lambda g, i: (g, i, 0)))
        out_shape.append(jax.ShapeDtypeStruct((groups, t, GROUP_WIDTH), a_dtype))
    for w in cast_weights:
        rows, cols = w.shape
        assert rows % (n_steps * 16) == 0, "row block must be a whole number of bf16 sublane tiles"
        spec = pl.BlockSpec((rows // n_steps, cols), lambda g, i: (g * steps_per_group + i, 0))
        in_specs.append(spec)
        out_specs.append(spec)
        out_shape.append(jax.ShapeDtypeStruct((rows, cols), BF16))
        args.append(w)
    return pl.pallas_call(
        functools.partial(_proj_kernel, k_transposed=k_transposed, n_cast=len(cast_weights)),
        grid=(groups, t // tm),
        in_specs=in_specs,
        out_specs=out_specs,
        out_shape=out_shape,
        compiler_params=pltpu.CompilerParams(
            dimension_semantics=("arbitrary", "arbitrary"), vmem_limit_bytes=VMEM_LIMIT),
        name="project",
    )(*args)


MIX_TILE = 512


def _dot3_lhs(lhs_f32, rhs_bf16):
    hi, mid, lo = _split3_bf16(lhs_f32)
    return _dot(hi, rhs_bf16) + _dot(mid, rhs_bf16) + _dot(lo, rhs_bf16)


FF_CHUNKS = (768, 768, 768, 512)
MIXER_LOOKAHEAD = 2
OUT_ROW_BLOCKS = 2


def _gate_rows(g_ref, bias_ref, triu, n_chunks):
    n_gates = 2 * HEADS
    assert n_chunks * n_gates <= LANES
    rows = [g_ref[0:n_gates, c * CHUNK:(c + 1) * CHUNK] + bias_ref[...] for c in range(n_chunks)]
    rows_all = jnp.concatenate(rows, axis=0)
    b_rows_all = _dot3_lhs(_log_sigmoid(rows_all), triu)
    bc_all = jnp.concatenate([b_rows_all, jnp.zeros((LANES - n_chunks * n_gates, CHUNK), F32)], axis=0).T
    bc, arow, blast = [], [], []
    for c in range(n_chunks):
        bc.append(bc_all[:, c * n_gates:(c + 1) * n_gates])
        b_rows = b_rows_all[c * n_gates:(c + 1) * n_gates, :]
        arow.append([rows[c][hd:hd + 1, :] - b_rows[HEADS + hd:HEADS + hd + 1, :] for hd in range(HEADS)])
        blast.append([jnp.min(b_rows[HEADS + hd:HEADS + hd + 1, :], axis=1, keepdims=True) for hd in range(HEADS)])
    return bc, arow, blast


def _mlstm_local(q, kt, a_row, b_last, causal):
    amat = jnp.where(causal, a_row, -jnp.inf)
    m_row = jnp.max(amat, axis=1, keepdims=True)
    s_loc = (_dot(q, kt) * jnp.exp(amat - m_row)).astype(BF16)
    g_row = b_last + a_row
    g_loc = jnp.max(g_row, axis=1, keepdims=True)
    kw = (kt.astype(F32) * jnp.exp(g_row - g_loc)).astype(BF16)
    return m_row, s_loc, g_loc, kw


def _mlstm_readout(local, q, v, og, g_head, b_last, b_rep, cta, m_prev, ones_blk):
    m_row, s_loc, g_loc, kw = local
    va = jnp.concatenate([v, ones_blk], axis=1)
    nd_loc = _dot(s_loc, va)
    u_aug = _dot(kw, va)
    inter = _dot(q, cta.astype(BF16))
    mm = jnp.maximum(m_prev, m_row)
    f_loc = jnp.exp(m_row - mm) * QK_SCALE
    f_int = jnp.exp(m_prev - mm) * QK_SCALE
    nd = f_loc * nd_loc + f_int * inter
    clamp = jnp.exp(-(b_rep + mm))
    hh = nd[:, :HEAD_DIM] / jnp.maximum(jnp.abs(nd[:, HEAD_DIM:]), clamp)
    ml = _rms(hh) * g_head * og

    dec = b_last + m_prev
    m_new = jnp.maximum(dec, g_loc)
    cta_new = jnp.exp(dec - m_new) * cta + jnp.exp(g_loc - m_new) * u_aug
    return ml, cta_new, m_new


def _prompt_tail_kernel(a_ref, kt_ref, b_ref, g_ref, x_ref, gt1_ref, sh2_ref, sc2_ref, gt2_ref,
                        bias_ref, gh_ref, ws_ref, bs_ref, wo_ref, g2_ref, wgu_ref, wdn_ref, gf_ref,
                        y_ref, c_out, n_out, m_out,
                        mix_ref, cta_ref, m_ref, *, tiles_per_seq):
    step = pl.program_id(0)
    n_tiles = pl.num_programs(0) - 1
    refs = (a_ref, kt_ref, b_ref, g_ref, x_ref, gt1_ref, sh2_ref, sc2_ref, gt2_ref,
            bias_ref, gh_ref, ws_ref, bs_ref, wo_ref, g2_ref, wgu_ref, wdn_ref, gf_ref,
            y_ref, mix_ref, cta_ref, m_ref)

    @pl.when(step % tiles_per_seq == 0)
    def _():
        cta_ref[...] = jnp.zeros_like(cta_ref)
        m_ref[...] = jnp.zeros_like(m_ref)

    @pl.when(step == 0)
    def _():
        _prompt_tail_body(refs, with_mixer=True, with_out=False)

    @pl.when(jnp.logical_and(step > 0, step < n_tiles))
    def _():
        _prompt_tail_body(refs, with_mixer=True, with_out=True)

    @pl.when(step == n_tiles)
    def _():
        _prompt_tail_body(refs, with_mixer=False, with_out=True)

    @pl.when(jnp.logical_and(step % tiles_per_seq == tiles_per_seq - 1, step < n_tiles))
    def _():
        for hd in range(HEADS):
            cta = cta_ref[hd]
            c_out[hd] = cta[:, :HEAD_DIM].T
            n_out[hd: hd + 1, :] = cta[:, HEAD_DIM:].T[0:1, :]
            m_out[hd: hd + 1, :] = m_ref[hd][0:1, :]


def _prompt_tail_body(refs, *, with_mixer, with_out):
    (a_ref, kt_ref, b_ref, g_ref, x_ref, gt1_ref, sh2_ref, sc2_ref, gt2_ref,
     bias_ref, gh_ref, ws_ref, bs_ref, wo_ref, g2_ref, wgu_ref, wdn_ref, gf_ref,
     y_ref, mix_ref, cta_ref, m_ref) = refs
    tm = a_ref.shape[0]
    n_chunks = tm // CHUNK
    n_copy = a_ref.shape[1] - GROUP_WIDTH

    row = lax.broadcasted_iota(jnp.int32, (CHUNK, CHUNK), 0)
    col = lax.broadcasted_iota(jnp.int32, (CHUNK, CHUNK), 1)
    causal = row >= col
    triu = jnp.where(row <= col, 1.0, 0.0).astype(BF16)
    ones_blk = jnp.ones((CHUNK, HEAD_DIM), BF16)

    blocks = [slice(r0, r0 + tm // OUT_ROW_BLOCKS) for r0 in range(0, tm, tm // OUT_ROW_BLOCKS)] if with_out else []
    outproj = [_dot(mix_ref[rs, :], wo_ref[...]) for rs in blocks]
    x1, h2 = [], []
    for rs, op in zip(blocks, outproj):
        x1.append(x_ref[rs, :] + gt1_ref[...] * op)
        h = _rms(x1[-1]) * g2_ref[...]
        h2.append((h * (1.0 + sc2_ref[...]) + sh2_ref[...]).astype(BF16))

    if with_mixer:
        bc, arow, blast = _gate_rows(g_ref, bias_ref, triu, n_chunks)
        state = [(cta_ref[hd], jnp.max(m_ref[hd][0:1, :], axis=1, keepdims=True)) for hd in range(HEADS)]
    sg_all = [None] * HEADS

    def mixer_local(c, hd):
        rs = slice(c * CHUNK, (c + 1) * CHUNK)
        sl = slice(hd * HEAD_DIM, (hd + 1) * HEAD_DIM)
        return _mlstm_local(a_ref[rs, sl], kt_ref[sl, rs], arow[c][hd], blast[c][hd], causal)

    def mixer_readout(c, hd, local):
        rs = slice(c * CHUNK, (c + 1) * CHUNK)
        sl = slice(hd * HEAD_DIM, (hd + 1) * HEAD_DIM)
        if c == 0:
            vsn_all = jnp.concatenate(
                [a_ref[cc * CHUNK:(cc + 1) * CHUNK, n_copy + hd * HEAD_DIM: n_copy + (hd + 1) * HEAD_DIM]
                 for cc in range(n_chunks)], axis=1)
            sg_all[hd] = _dot(ws_ref[hd], vsn_all)
        cta, m_prev = state[hd]
        b_rep = jnp.broadcast_to(bc[c][:, HEADS + hd: HEADS + hd + 1], (CHUNK, HEAD_DIM))
        ml, cta, m_prev = _mlstm_readout(
            local, a_ref[rs, sl], a_ref[rs, GROUP_WIDTH + hd * HEAD_DIM: GROUP_WIDTH + (hd + 1) * HEAD_DIM],
            _sigmoid(b_ref[rs, sl]), gh_ref[:, sl], blast[c][hd], b_rep, cta, m_prev, ones_blk)
        state[hd] = (cta, m_prev)
        mix_ref[rs, sl] = ml.astype(mix_ref.dtype)
        sg = sg_all[hd][:, c * CHUNK:(c + 1) * CHUNK] + bs_ref[hd]
        ug = _gelu_tanh(b_ref[rs, GROUP_WIDTH + hd * HEAD_DIM: GROUP_WIDTH + (hd + 1) * HEAD_DIM])
        mix_ref[rs, GROUP_WIDTH + hd * HEAD_DIM: GROUP_WIDTH + (hd + 1) * HEAD_DIM] = (ug * sg).astype(mix_ref.dtype)

    pieces = [(c, hd) for c in range(n_chunks) for hd in range(HEADS)] if with_mixer else []
    per_ff = -(-len(pieces) // len(FF_CHUNKS))
    ahead = [mixer_local(*pieces[p]) for p in range(min(MIXER_LOOKAHEAD, len(pieces)))]
    acc = [None] * len(blocks)
    f0 = 0
    for j, width in enumerate(FF_CHUNKS):
        gate_up = [(_dot(h, wgu_ref[:, f0:f0 + width]), _dot(h, wgu_ref[:, D_FF + f0:D_FF + f0 + width]))
                   for h in h2]
        for idx in range(j * per_ff, min((j + 1) * per_ff, len(pieces))):
            if idx + MIXER_LOOKAHEAD < len(pieces):
                ahead.append(mixer_local(*pieces[idx + MIXER_LOOKAHEAD]))
            mixer_readout(*pieces[idx], ahead.pop(0))
        for r, (gate, up) in enumerate(gate_up):
            act = (gate * _sigmoid(gate) * up).astype(BF16)
            part = _dot(act, wdn_ref[f0:f0 + width, :])
            acc[r] = part if acc[r] is None else acc[r] + part
            if j + 1 == len(FF_CHUNKS):
                x2 = x1[r] + gt2_ref[...] * acc[r]
                y_ref[blocks[r], :] = _rms(x2) * gf_ref[...]
        f0 += width

    if with_mixer:
        for hd in range(HEADS):
            cta_ref[hd] = state[hd][0]
            m_ref[hd] = jnp.broadcast_to(state[hd][1], m_ref.shape[1:])


def _prompt_tail(a, kt, b, gates, x, mod, bias_row, g_head, ws_tril, bs_rep, w_out, g2, w_gu, w_dn, g_final):
    groups, t, n_a = a.shape
    tm = MIX_TILE
    tps = t // tm
    n_tiles = groups * tps
    assert sum(FF_CHUNKS) == D_FF

    def cur(i):
        return jnp.minimum(i, n_tiles - 1)

    def prev(i):
        return jnp.maximum(i - 1, 0)

    def rows(tile, width):
        return pl.BlockSpec((None, tm, width), lambda i: (tile(i) // tps, tile(i) % tps, 0))

    def mod_piece(piece):
        return pl.BlockSpec((None, 1, D_MODEL), lambda i: (prev(i) // tps, 0, piece))

    def const(shape):
        nd = len(shape)
        return pl.BlockSpec(shape, lambda i: (0,) * nd, pipeline_mode=pl.Buffered(1))

    def per_seq(shape):
        nd = len(shape)
        return pl.BlockSpec((None,) + shape, lambda i: (cur(i) // tps,) + (0,) * nd)

    return pl.pallas_call(
        functools.partial(_prompt_tail_kernel, tiles_per_seq=tps),
        grid=(n_tiles + 1,),
        in_specs=[
            rows(cur, n_a),
            pl.BlockSpec((None, GROUP_WIDTH, tm), lambda i: (cur(i) // tps, 0, cur(i) % tps)),
            rows(cur, N_B),
            pl.BlockSpec((None, GATE_ROWS, tm), lambda i: (cur(i) // tps, 0, cur(i) % tps)),
            rows(prev, D_MODEL),
            mod_piece(2), mod_piece(3), mod_piece(4), mod_piece(5),
            const((2 * HEADS, LANES)),
            const((1, GROUP_WIDTH)),
            const((HEADS, CHUNK, CHUNK)),
            const((HEADS, CHUNK, LANES)),
            const((D_MODEL, D_MODEL)),
            const((1, D_MODEL)),
            const((D_MODEL, 2 * D_FF)),
            const((D_FF, D_MODEL)),
            const((1, D_MODEL)),
        ],
        out_specs=[
            rows(prev, D_MODEL),
            per_seq((HEADS, HEAD_DIM, HEAD_DIM)),
            per_seq((HEADS, HEAD_DIM)),
            per_seq((HEADS, LANES)),
        ],
        out_shape=[
            jax.ShapeDtypeStruct((groups, t, D_MODEL), F32),
            jax.ShapeDtypeStruct((groups, HEADS, HEAD_DIM, HEAD_DIM), F32),
            jax.ShapeDtypeStruct((groups, HEADS, HEAD_DIM), F32),
            jax.ShapeDtypeStruct((groups, HEADS, LANES), F32),
        ],
        scratch_shapes=[
            pltpu.VMEM((tm, D_MODEL), BF16),
            pltpu.VMEM((HEADS, HEAD_DIM, 2 * HEAD_DIM), F32),
            pltpu.VMEM((HEADS, F32_SUBLANES, LANES), F32),
        ],
        compiler_params=pltpu.CompilerParams(
            dimension_semantics=("arbitrary",), vmem_limit_bytes=VMEM_LIMIT),
        name="prompt_tail",
    )(a, kt, b, gates, x, mod, mod, mod, mod, bias_row, g_head, ws_tril, bs_rep, w_out, g2, w_gu, w_dn, g_final)


SAMPLE_TOKENS_PER_STEP = 128
SAMPLE_UNROLL = 8


def _mix_sample_kernel(q_ref, k_ref, v_ref, vsn_ref, og_ref, ug_ref, g_ref, m0_ref, n0_ref, c_ref,
                       bias_ref, gh_ref, ws0_ref, bs0_ref,
                       mix_ref, c_out, n_out, m_out,
                       vt_ref, kp_ref, wd_ref, cqt_ref):
    hd = pl.program_id(0)
    grp = pl.program_id(1)
    nb = q_ref.shape[0]
    lane = lax.broadcasted_iota(jnp.int32, (nb, LANES), 1)

    def gate_terms():
        pre = g_ref[...] + bias_ref[...]
        i_pre = jnp.sum(jnp.where(lane == hd, pre, 0.0), axis=1, keepdims=True)
        f_pre = jnp.sum(jnp.where(lane == hd + HEADS, pre, 0.0), axis=1, keepdims=True)
        m_prev = jnp.sum(jnp.where(lane == hd, m0_ref[...], 0.0), axis=1, keepdims=True)
        inter = _log_sigmoid(f_pre) + m_prev
        m_t = jnp.maximum(inter, i_pre)
        return m_t, jnp.exp(i_pre - m_t), jnp.exp(inter - m_t)

    @pl.when(grp == 0)
    def _():
        _, w_in, w_dec = gate_terms()
        vt_ref[...] = v_ref[...].T
        kp_ref[...] = (w_in * k_ref[...]).astype(BF16)
        wd_ref[...] = jnp.broadcast_to(w_dec, wd_ref.shape)
        cqt_ref[...] = jnp.zeros_like(cqt_ref)

    lane_sq = lax.broadcasted_iota(jnp.int32, (HEAD_DIM, LANES), 1)
    tokens = c_ref.shape[0]

    def body(r, cqt):
        tok = grp * tokens + r
        q_row = q_ref[pl.ds(tok, 1), :]
        c_b = c_ref[r]
        sel = lane_sq == tok
        cq_col = jnp.sum(c_b * q_row, axis=1, keepdims=True)
        v_sel = jnp.where(sel, vt_ref[...], 0.0).astype(BF16)
        outer = _dot(v_sel, kp_ref[...])
        c_out[r] = wd_ref[pl.ds(tok, 1), :] * c_b + outer
        return jnp.where(sel, cq_col, cqt)

    cqt_ref[...] = lax.fori_loop(0, tokens, body, cqt_ref[...], unroll=SAMPLE_UNROLL)

    @pl.when(grp == pl.num_programs(1) - 1)
    def _():
        m_t, w_in, w_dec = gate_terms()
        q = q_ref[...]
        k = k_ref[...]
        v = v_ref[...]
        n0 = n0_ref[...]
        cq = cqt_ref[...].T
        s = jnp.sum(q * k, axis=1, keepdims=True) * (QK_SCALE * w_in)
        w_inter = w_dec * QK_SCALE
        num = s * v + w_inter * cq
        den = s + w_inter * jnp.sum(n0 * q, axis=1, keepdims=True)
        hh = num / jnp.maximum(jnp.abs(den), jnp.exp(-m_t))
        ml = _rms(hh) * gh_ref[...] * og_ref[...]
        cm = ug_ref[...] * (ws0_ref[...] * vsn_ref[...] + bs0_ref[...])
        n_out[...] = w_dec * n0 + w_in * k
        m_out[...] = jnp.broadcast_to(m_t, m_out.shape)
        for hh_static in range(HEADS):
            @pl.when(hd == hh_static)
            def _():
                mix_ref[:, hh_static * HEAD_DIM: (hh_static + 1) * HEAD_DIM] = ml
                mix_ref[:, GROUP_WIDTH + hh_static * HEAD_DIM: GROUP_WIDTH + (hh_static + 1) * HEAD_DIM] = cm


def _mix_sample(a, k, b, gates, m0_pad, n0, c0, bias_row, g_head, ws0_row, bs0_row):
    nb = a.shape[0]
    tb = SAMPLE_TOKENS_PER_STEP

    def head_block(offset):
        return pl.BlockSpec((nb, HEAD_DIM), lambda h, g: (0, offset + h))

    full = pl.BlockSpec((nb, LANES), lambda h, g: (0, 0))
    head_row = pl.BlockSpec((1, HEAD_DIM), lambda h, g: (0, h))
    c_spec = pl.BlockSpec((tb, None, HEAD_DIM, HEAD_DIM), lambda h, g: (g, h, 0, 0))
    return pl.pallas_call(
        _mix_sample_kernel,
        grid=(HEADS, nb // tb),
        in_specs=[
            head_block(0), head_block(0), head_block(HEADS), head_block(2 * HEADS),
            head_block(0), head_block(HEADS),
            full, full, head_block(0), c_spec,
            pl.BlockSpec((1, LANES), lambda h, g: (0, 0)),
            head_row, head_row, head_row,
        ],
        out_specs=[
            pl.BlockSpec((nb, D_MODEL), lambda h, g: (0, 0)),
            c_spec,
            head_block(0),
            head_block(0),
        ],
        out_shape=[
            jax.ShapeDtypeStruct((nb, D_MODEL), F32),
            jax.ShapeDtypeStruct(c0.shape, F32),
            jax.ShapeDtypeStruct((nb, GROUP_WIDTH), F32),
            jax.ShapeDtypeStruct((nb, GROUP_WIDTH), F32),
        ],
        scratch_shapes=[
            pltpu.VMEM((HEAD_DIM, nb), F32),
            pltpu.VMEM((nb, HEAD_DIM), BF16),
            pltpu.VMEM((nb, LANES), F32),
            pltpu.VMEM((HEAD_DIM, nb), F32),
        ],
        compiler_params=pltpu.CompilerParams(
            dimension_semantics=("arbitrary", "arbitrary"), vmem_limit_bytes=VMEM_LIMIT),
        name="mix_sample",
    )(a, k, a, a, b, b, gates, m0_pad, n0, c0, bias_row, g_head, ws0_row, bs0_row)


def _out_kernel(x_ref, mix_ref, gt1_ref, sh2_ref, sc2_ref, gt2_ref, wo_ref, g2_ref, wgu_ref, wdn_ref, gf_ref,
                y_ref):
    x1 = x_ref[...] + gt1_ref[...] * _dot(mix_ref[...].astype(BF16), wo_ref[...])
    h2 = _rms(x1) * g2_ref[...]
    h2 = (h2 * (1.0 + sc2_ref[...]) + sh2_ref[...]).astype(BF16)
    gate = _dot(h2, wgu_ref[:, :D_FF])
    up = _dot(h2, wgu_ref[:, D_FF:])
    act = (gate * _sigmoid(gate) * up).astype(BF16)
    x2 = x1 + gt2_ref[...] * _dot(act, wdn_ref[...])
    y_ref[:, 0, :] = _rms(x2) * gf_ref[...]


def _output_stage(x, mix, mod, w_out, g2, w_gu, w_dn, g_final, *, tm, per_row):
    groups, t, _ = x.shape
    return pl.pallas_call(
        _out_kernel,
        grid=(groups, t // tm),
        in_specs=[
            pl.BlockSpec((None, tm, D_MODEL), lambda g, i: (g, i, 0)),
            pl.BlockSpec((None, tm, D_MODEL), lambda g, i: (g, i, 0)),
            _mod_spec(2, tm, per_row),
            _mod_spec(3, tm, per_row),
            _mod_spec(4, tm, per_row),
            _mod_spec(5, tm, per_row),
            _const_spec((D_MODEL, D_MODEL)),
            _const_spec((1, D_MODEL)),
            _const_spec((D_MODEL, 2 * D_FF)),
            _const_spec((D_FF, D_MODEL)),
            _const_spec((1, D_MODEL)),
        ],
        out_specs=pl.BlockSpec((tm, 1, D_MODEL), lambda g, i: (g * (t // tm) + i, 0, 0)),
        out_shape=jax.ShapeDtypeStruct((groups * t, 1, D_MODEL), F32),
        compiler_params=pltpu.CompilerParams(
            dimension_semantics=("arbitrary", "arbitrary"), vmem_limit_bytes=VMEM_LIMIT),
        name="output_stage",
    )(x, mix, mod, mod, mod, mod, w_out, g2, w_gu, w_dn, g_final)


def kernel(x_prompt, x_sample, c_prompt, c_sample, state_mlstm_C, state_mlstm_n, state_mlstm_m, w_ada, b_ada, g_norm1, w_in, b_gate, g_mlstm_head, ln_v_g, ln_v_b, w_s, b_s, w_out, g_norm2, w_gate_up, w_down, g_final):
    depth = w_ada.shape[0]
    assert depth == 1, "single-layer trunk"
    batch, seq, _ = x_prompt.shape
    nb = x_sample.shape[0]
    assert x_sample.shape[1] == 1

    gw = GROUP_WIDTH
    g1 = g_norm1[0].reshape(1, D_MODEL)
    g2 = g_norm2[0].reshape(1, D_MODEL)
    gf = g_final.reshape(1, D_MODEL)
    ln_g = ln_v_g[0].reshape(1, gw)
    ln_b = ln_v_b[0].reshape(1, gw)
    g_head = g_mlstm_head[0].reshape(1, gw)
    bias_row = jnp.pad(b_gate[0], (0, LANES - 2 * HEADS)).reshape(1, LANES)
    bias_rep = jnp.broadcast_to(b_gate[0][:, None], (2 * HEADS, LANES))
    tril = jnp.tril(jnp.ones((CHUNK, CHUNK), dtype=bool))
    ws_tril = jnp.where(tril[None], w_s[0], 0.0).astype(BF16)
    bs_rep = jnp.broadcast_to(b_s[0][:, :, None], (HEADS, CHUNK, LANES))
    ws0_row = jnp.repeat(w_s[0][:, 0, 0], HEAD_DIM).reshape(1, gw)
    bs0_row = jnp.repeat(b_s[0][:, 0], HEAD_DIM).reshape(1, gw)

    mod_s, mod_p, w_main, w_gates, w_k, w_kt = _modulation(c_sample, c_prompt, w_ada[0], b_ada[0].reshape(1, -1), w_in[0].T)
    mod_s = mod_s.reshape(1, nb, N_MOD * D_MODEL)

    a_p, b_p, g_p, kt_p, w_out_b, w_gu_b, w_dn_b = _project(
        x_prompt, mod_p, g1, w_main, w_gates, ln_g, ln_b, w_kt, (w_out[0], w_gate_up[0], w_down[0]),
        k_transposed=True, tm=PROJ_TILE, per_row=False, a_dtype=BF16)
    y_p, c_p, n_p, m_p = _prompt_tail(a_p, kt_p, b_p, g_p, x_prompt, mod_p, bias_rep, g_head, ws_tril, bs_rep,
                                      w_out_b, g2, w_gu_b, w_dn_b, gf)

    xs = x_sample.reshape(1, nb, D_MODEL)
    a_s, b_s_act, g_s, k_s = _project(xs, mod_s, g1, w_main, w_gates, ln_g, ln_b, w_k, k_transposed=False,
                                      tm=nb, per_row=True, a_dtype=F32)
    a_s2, b_s2, g_s2 = a_s[0], b_s_act[0], g_s[0]
    m0_pad = jnp.pad(state_mlstm_m[0], ((0, 0), (0, LANES - HEADS)))
    n0 = state_mlstm_n[0].reshape(nb, gw)
    mix_s, c_s, n_s, m_s = _mix_sample(a_s2, k_s[0], b_s2, g_s2, m0_pad, n0, state_mlstm_C[0],
                                       bias_row, g_head, ws0_row, bs0_row)
    y_s = _output_stage(xs, mix_s.reshape(1, nb, D_MODEL), mod_s, w_out_b, g2, w_gu_b, w_dn_b, gf,
                        tm=nb, per_row=True)

    return (
        y_p,
        y_s,
        c_p[None],
        n_p[None],
        m_p[:, :, 0][None],
        c_s[None],
        n_s.reshape(nb, HEADS, HEAD_DIM)[None],
        m_s.reshape(nb, HEADS, HEAD_DIM)[:, :, 0][None],
        a_s2[:, 2 * gw:].reshape(nb, 1, HEADS, HEAD_DIM)[None],
    )
```

```python
import functools
import math

import jax
import jax.numpy as jnp
from jax import lax
from jax.experimental import pallas as pl
from jax.experimental.pallas import tpu as pltpu

F32 = jnp.float32
BF16 = jnp.bfloat16

D_MODEL = 1024
HEADS = 4
HEAD_DIM = 128
GROUP_WIDTH = HEADS * HEAD_DIM
CHUNK = 128
D_FF = 2816
N_MOD = 6
EPS = 1e-6
QK_SCALE = HEAD_DIM ** -0.5
LANES = 128
F32_SUBLANES = 8
BF16_SUBLANES = 16

N_B = 2 * GROUP_WIDTH

VMEM_LIMIT = 56 * 1024 * 1024


def _dot(a, b):
    return jnp.dot(a, b, preferred_element_type=F32)


def _dot_nt(a, b):
    return lax.dot_general(a, b, (((1,), (1,)), ((), ())), preferred_element_type=F32)


def _sigmoid(x):
    return 1.0 / (1.0 + jnp.exp(-x))


def _gelu_tanh(x):
    c = math.sqrt(2.0 / math.pi)
    return x * (0.5 * (1.0 + jnp.tanh(c * (x + 0.044715 * (x * x * x)))))


def _log_sigmoid(x):
    return jnp.minimum(x, 0.0) - jnp.log1p(jnp.exp(-jnp.abs(x)))


def _rms(x):
    return x * lax.rsqrt(jnp.mean(x * x, axis=-1, keepdims=True) + EPS)


def _split3_bf16(x):
    hi = x.astype(BF16)
    r1 = x - hi.astype(F32)
    mid = r1.astype(BF16)
    lo = (r1 - mid.astype(F32)).astype(BF16)
    return hi, mid, lo


MOD_TILE = 1024

IN_Q, IN_K, IN_V, IN_O, IN_GATES = 0, GROUP_WIDTH, 2 * GROUP_WIDTH, 3 * GROUP_WIDTH, 4 * GROUP_WIDTH
IN_U = IN_GATES + 2 * HEADS
IN_VS = IN_U + GROUP_WIDTH
N_IN = IN_VS + GROUP_WIDTH
MAIN_GROUPS = (IN_Q, IN_V, IN_VS, IN_O, IN_U)
N_MAIN = len(MAIN_GROUPS) * GROUP_WIDTH
GATE_ROWS = BF16_SUBLANES


def _mod_kernel(cs_ref, cp_ref, w_ref, b_ref, grp_ref, gate_ref, os_ref, op_ref, wmain_ref, wgate_ref, wk_ref, wkt_ref):
    step = pl.program_id(0)
    ns, npr = cs_ref.shape[0], cp_ref.shape[0]
    pad = -(ns + npr) % BF16_SUBLANES
    c = jnp.concatenate([cs_ref[...], cp_ref[...], jnp.zeros((pad, D_MODEL), F32)], axis=0)
    a = (c * _sigmoid(c)).astype(BF16)
    res = _dot(a, w_ref[...].astype(BF16)) + b_ref[...]
    os_ref[...] = res[:ns]
    op_ref[:, 0, :] = res[ns:ns + npr]

    def transposed_into(dst_ref):
        for blk in range(GROUP_WIDTH // LANES):
            rows = grp_ref[blk * LANES:(blk + 1) * LANES, :]
            dst_ref[:, blk * LANES:(blk + 1) * LANES] = rows.T.astype(BF16)

    @pl.when(step < len(MAIN_GROUPS))
    def _():
        transposed_into(wmain_ref)

    @pl.when(step == len(MAIN_GROUPS))
    def _():
        wkt_ref[:GROUP_WIDTH, :] = grp_ref[...].astype(BF16)
        wkt_ref[GROUP_WIDTH:, :] = jnp.concatenate(
            [gate_ref[...], jnp.zeros((GATE_ROWS - 2 * HEADS, D_MODEL), F32)], axis=0).astype(BF16)
        transposed_into(wk_ref)

    @pl.when(step == 0)
    def _():
        gate_rows = jnp.concatenate([gate_ref[...], jnp.zeros((LANES - 2 * HEADS, D_MODEL), F32)], axis=0)
        wgate_ref[...] = gate_rows.T.astype(BF16)


def _modulation(c_sample, c_prompt, w_ada, b_ada, w_in_t):
    ns, npr = c_sample.shape[0], c_prompt.shape[0]
    tn = MOD_TILE
    steps = N_MOD * D_MODEL // tn
    group_starts = MAIN_GROUPS + (IN_K,)
    assert w_in_t.shape == (N_IN, D_MODEL) and steps == len(group_starts)

    def whole(shape):
        return pl.BlockSpec(shape, lambda j: (0, 0))

    def group_start(j):
        assert all(start % F32_SUBLANES == 0 for start in group_starts)
        return pl.multiple_of(sum(jnp.where(j == i, start, 0) for i, start in enumerate(group_starts)), F32_SUBLANES)

    return pl.pallas_call(
        _mod_kernel,
        grid=(steps,),
        in_specs=[
            whole((ns, D_MODEL)),
            whole((npr, D_MODEL)),
            pl.BlockSpec((D_MODEL, tn), lambda j: (0, j)),
            pl.BlockSpec((1, tn), lambda j: (0, j)),
            pl.BlockSpec((pl.Element(GROUP_WIDTH), pl.Element(D_MODEL)), lambda j: (group_start(j), 0)),
            pl.BlockSpec((pl.Element(2 * HEADS), pl.Element(D_MODEL)), lambda j: (IN_GATES, 0)),
        ],
        out_specs=[
            pl.BlockSpec((ns, tn), lambda j: (0, j)),
            pl.BlockSpec((npr, 1, tn), lambda j: (0, 0, j)),
            pl.BlockSpec((D_MODEL, GROUP_WIDTH), lambda j: (0, jnp.minimum(j, len(MAIN_GROUPS) - 1))),
            whole((D_MODEL, LANES)),
            whole((D_MODEL, GROUP_WIDTH)),
            whole((GROUP_WIDTH + GATE_ROWS, D_MODEL)),
        ],
        out_shape=[
            jax.ShapeDtypeStruct((ns, N_MOD * D_MODEL), F32),
            jax.ShapeDtypeStruct((npr, 1, N_MOD * D_MODEL), F32),
            jax.ShapeDtypeStruct((D_MODEL, N_MAIN), BF16),
            jax.ShapeDtypeStruct((D_MODEL, LANES), BF16),
            jax.ShapeDtypeStruct((D_MODEL, GROUP_WIDTH), BF16),
            jax.ShapeDtypeStruct((GROUP_WIDTH + GATE_ROWS, D_MODEL), BF16),
        ],
        compiler_params=pltpu.CompilerParams(dimension_semantics=("arbitrary",), vmem_limit_bytes=VMEM_LIMIT),
        name="modulation",
    )(c_sample, c_prompt, w_ada, b_ada, w_in_t, w_in_t)


def _mod_spec(piece, tm, per_row):
    if per_row:
        return pl.BlockSpec((None, tm, D_MODEL), lambda g, t: (g, t, piece))
    return pl.BlockSpec((None, 1, D_MODEL), lambda g, t: (g, 0, piece))


def _const_spec(shape):
    nd = len(shape)
    return pl.BlockSpec(shape, lambda g, t: (0,) * nd, pipeline_mode=pl.Buffered(1))


PROJ_TILE = 1024


def _proj_body(x_ref, sh_ref, sc_ref, g1_ref, w_ref, wg_ref, lng_ref, lnb_ref, a_ref, b_ref, g_ref):
    n_a = a_ref.shape[1]
    n_copy = n_a - GROUP_WIDTH
    x = x_ref[...]
    h = _rms(x) * g1_ref[...]
    h = (h * (1.0 + sc_ref[...]) + sh_ref[...]).astype(BF16)
    p = _dot(h, w_ref[...])
    a_ref[:, :n_copy] = p[:, :n_copy].astype(a_ref.dtype)
    vs = _gelu_tanh(p[:, n_copy:n_a])
    for hd in range(HEADS):
        sl = slice(hd * HEAD_DIM, (hd + 1) * HEAD_DIM)
        v = vs[:, sl]
        mu = jnp.mean(v, axis=-1, keepdims=True)
        vc = v - mu
        var = jnp.mean(vc * vc, axis=-1, keepdims=True)
        y = vc * lax.rsqrt(var + EPS) * lng_ref[hd:hd + 1, :] + lnb_ref[hd:hd + 1, :]
        a_ref[:, n_copy + hd * HEAD_DIM: n_copy + (hd + 1) * HEAD_DIM] = y.astype(a_ref.dtype)
    if wg_ref is not None:
        b_ref[:, :GROUP_WIDTH] = _sigmoid(p[:, n_a: n_a + GROUP_WIDTH])
        b_ref[:, GROUP_WIDTH:] = _gelu_tanh(p[:, n_a + GROUP_WIDTH: n_a + N_B])
        g_ref[...] = _dot(h, wg_ref[...])
    else:
        b_ref[...] = p[:, n_a: n_a + N_B]
    return h


def _proj_kernel(*refs, k_transposed, n_cast):
    ins, outs = refs[:9 + n_cast], refs[9 + n_cast:]
    wk_ref, k_ref = ins[8], outs[3]
    if k_transposed:
        h = _proj_body(*ins[:5], None, *ins[6:8], *outs[:3])
        kt_gates = _dot_nt(wk_ref[...], h)
        k_ref[...] = kt_gates[:GROUP_WIDTH].astype(k_ref.dtype)
        outs[2][...] = kt_gates[GROUP_WIDTH:]
    else:
        h = _proj_body(*ins[:8], *outs[:3])
        k_ref[...] = _dot(h, wk_ref[...]).astype(k_ref.dtype)
    for src, dst in zip(ins[9:], outs[4:]):
        dst[...] = src[...].astype(dst.dtype)


def _project(x, mod, g1, w_main, w_gates, ln_g, ln_b, w_k, cast_weights=(), *, k_transposed, tm, per_row, a_dtype):
    groups, t, _ = x.shape
    n_a = w_main.shape[1] - N_B
    steps_per_group = t // tm
    n_steps = groups * steps_per_group
    in_specs = [
        pl.BlockSpec((None, tm, D_MODEL), lambda g, i: (g, i, 0)),
        _mod_spec(0, tm, per_row),
        _mod_spec(1, tm, per_row),
        _const_spec((1, D_MODEL)),
        _const_spec(w_main.shape),
        _const_spec(w_gates.shape),
        _const_spec((HEADS, HEAD_DIM)),
        _const_spec((HEADS, HEAD_DIM)),
    ]
    out_specs = [
        pl.BlockSpec((None, tm, n_a), lambda g, i: (g, i, 0)),
        pl.BlockSpec((None, tm, N_B), lambda g, i: (g, i, 0)),
    ]
    out_shape = [
        jax.ShapeDtypeStruct((groups, t, n_a), a_dtype),
        jax.ShapeDtypeStruct((groups, t, N_B), F32),
    ]
    if k_transposed:
        out_specs.append(pl.BlockSpec((None, GATE_ROWS, tm), lambda g, i: (g, 0, i)))
        out_shape.append(jax.ShapeDtypeStruct((groups, GATE_ROWS, t), F32))
    else:
        out_specs.append(pl.BlockSpec((None, tm, LANES), lambda g, i: (g, i, 0)))
        out_shape.append(jax.ShapeDtypeStruct((groups, t, LANES), F32))
    args = [x, mod, mod, g1, w_main, w_gates, ln_g, ln_b, w_k]
    in_specs.append(_const_spec(w_k.shape))
    if k_transposed:
        out_specs.append(pl.BlockSpec((None, GROUP_WIDTH, tm), lambda g, i: (g, 0, i)))
        out_shape.append(jax.ShapeDtypeStruct((groups, GROUP_WIDTH, t), a_dtype))
    else:
        out_specs.append(pl.BlockSpec((None, tm, GROUP_WIDTH), lambda g, i: (g, i, 0)))
        out_shape.append(jax.ShapeDtypeStruct((groups, t, GROUP_WIDTH), a_dtype))
    for w in cast_weights:
        rows, cols = w.shape
        assert rows % (n_steps * 16) == 0, "row block must be a whole number of bf16 sublane tiles"
        spec = pl.BlockSpec((rows // n_steps, cols), lambda g, i: (g * steps_per_group + i, 0))
        in_specs.append(spec)
        out_specs.append(spec)
        out_shape.append(jax.ShapeDtypeStruct((rows, cols), BF16))
        args.append(w)
    return pl.pallas_call(
        functools.partial(_proj_kernel, k_transposed=k_transposed, n_cast=len(cast_weights)),
        grid=(groups, t // tm),
        in_specs=in_specs,
        out_specs=out_specs,
        out_shape=out_shape,
        compiler_params=pltpu.CompilerParams(
            dimension_semantics=("arbitrary", "arbitrary"), vmem_limit_bytes=VMEM_LIMIT),
        name="project",
    )(*args)


MIX_TILE = 512


def _dot3_lhs(lhs_f32, rhs_bf16):
    hi, mid, lo = _split3_bf16(lhs_f32)
    return _dot(hi, rhs_bf16) + _dot(mid, rhs_bf16) + _dot(lo, rhs_bf16)


FF_CHUNKS = (768, 768, 768, 512)
MIXER_LOOKAHEAD = 2
OUT_ROW_BLOCKS = 2


def _gate_rows(g_ref, bias_ref, triu, n_chunks):
    n_gates = 2 * HEADS
    assert n_chunks * n_gates <= LANES
    rows = [g_ref[0:n_gates, c * CHUNK:(c + 1) * CHUNK] + bias_ref[...] for c in range(n_chunks)]
    rows_all = jnp.concatenate(rows, axis=0)
    b_rows_all = _dot3_lhs(_log_sigmoid(rows_all), triu)
    bc_all = jnp.concatenate([b_rows_all, jnp.zeros((LANES - n_chunks * n_gates, CHUNK), F32)], axis=0).T
    bc, arow, blast = [], [], []
    for c in range(n_chunks):
        bc.append(bc_all[:, c * n_gates:(c + 1) * n_gates])
        b_rows = b_rows_all[c * n_gates:(c + 1) * n_gates, :]
        arow.append([rows[c][hd:hd + 1, :] - b_rows[HEADS + hd:HEADS + hd + 1, :] for hd in range(HEADS)])
        blast.append([jnp.min(b_rows[HEADS + hd:HEADS + hd + 1, :], axis=1, keepdims=True) for hd in range(HEADS)])
    return bc, arow, blast


def _mlstm_local(q, kt, a_row, b_last, causal):
    amat = jnp.where(causal, a_row, -jnp.inf)
    m_row = jnp.max(amat, axis=1, keepdims=True)
    s_loc = (_dot(q, kt) * jnp.exp(amat - m_row)).astype(BF16)
    g_row = b_last + a_row
    g_loc = jnp.max(g_row, axis=1, keepdims=True)
    kw = (kt.astype(F32) * jnp.exp(g_row - g_loc)).astype(BF16)
    return m_row, s_loc, g_loc, kw


def _mlstm_readout(local, q, v, og, g_head, b_last, b_rep, cta, m_prev, ones_blk):
    m_row, s_loc, g_loc, kw = local
    va = jnp.concatenate([v, ones_blk], axis=1)
    nd_loc = _dot(s_loc, va)
    u_aug = _dot(kw, va)
    inter = _dot(q, cta.astype(BF16))
    mm = jnp.maximum(m_prev, m_row)
    f_loc = jnp.exp(m_row - mm) * QK_SCALE
    f_int = jnp.exp(m_prev - mm) * QK_SCALE
    nd = f_loc * nd_loc + f_int * inter
    clamp = jnp.exp(-(b_rep + mm))
    hh = nd[:, :HEAD_DIM] / jnp.maximum(jnp.abs(nd[:, HEAD_DIM:]), clamp)
    ml = _rms(hh) * g_head * og

    dec = b_last + m_prev
    m_new = jnp.maximum(dec, g_loc)
    cta_new = jnp.exp(dec - m_new) * cta + jnp.exp(g_loc - m_new) * u_aug
    return ml, cta_new, m_new


def _prompt_tail_kernel(a_ref, kt_ref, b_ref, g_ref, x_ref, gt1_ref, sh2_ref, sc2_ref, gt2_ref,
                        bias_ref, gh_ref, ws_ref, bs_ref, wo_ref, g2_ref, wgu_ref, wdn_ref, gf_ref,
                        y_ref, c_out, n_out, m_out,
                        mix_ref, cta_ref, m_ref, *, tiles_per_seq):
    step = pl.program_id(0)
    n_tiles = pl.num_programs(0) - 1
    refs = (a_ref, kt_ref, b_ref, g_ref, x_ref, gt1_ref, sh2_ref, sc2_ref, gt2_ref,
            bias_ref, gh_ref, ws_ref, bs_ref, wo_ref, g2_ref, wgu_ref, wdn_ref, gf_ref,
            y_ref, mix_ref, cta_ref, m_ref)

    @pl.when(step % tiles_per_seq == 0)
    def _():
        cta_ref[...] = jnp.zeros_like(cta_ref)
        m_ref[...] = jnp.zeros_like(m_ref)

    @pl.when(step == 0)
    def _():
        _prompt_tail_body(refs, with_mixer=True, with_out=False)

    @pl.when(jnp.logical_and(step > 0, step < n_tiles))
    def _():
        _prompt_tail_body(refs, with_mixer=True, with_out=True)

    @pl.when(step == n_tiles)
    def _():
        _prompt_tail_body(refs, with_mixer=False, with_out=True)

    @pl.when(jnp.logical_and(step % tiles_per_seq == tiles_per_seq - 1, step < n_tiles))
    def _():
        for hd in range(HEADS):
            cta = cta_ref[hd]
            c_out[hd] = cta[:, :HEAD_DIM].T
            n_out[hd: hd + 1, :] = cta[:, HEAD_DIM:].T[0:1, :]
            m_out[hd: hd + 1, :] = m_ref[hd][0:1, :]


def _prompt_tail_body(refs, *, with_mixer, with_out):
    (a_ref, kt_ref, b_ref, g_ref, x_ref, gt1_ref, sh2_ref, sc2_ref, gt2_ref,
     bias_ref, gh_ref, ws_ref, bs_ref, wo_ref, g2_ref, wgu_ref, wdn_ref, gf_ref,
     y_ref, mix_ref, cta_ref, m_ref) = refs
    tm = a_ref.shape[0]
    n_chunks = tm // CHUNK
    n_copy = a_ref.shape[1] - GROUP_WIDTH

    row = lax.broadcasted_iota(jnp.int32, (CHUNK, CHUNK), 0)
    col = lax.broadcasted_iota(jnp.int32, (CHUNK, CHUNK), 1)
    causal = row >= col
    triu = jnp.where(row <= col, 1.0, 0.0).astype(BF16)
    ones_blk = jnp.ones((CHUNK, HEAD_DIM), BF16)

    blocks = [slice(r0, r0 + tm // OUT_ROW_BLOCKS) for r0 in range(0, tm, tm // OUT_ROW_BLOCKS)] if with_out else []
    outproj = [_dot(mix_ref[rs, :], wo_ref[...]) for rs in blocks]
    x1, h2 = [], []
    for rs, op in zip(blocks, outproj):
        x1.append(x_ref[rs, :] + gt1_ref[...] * op)
        h = _rms(x1[-1]) * g2_ref[...]
        h2.append((h * (1.0 + sc2_ref[...]) + sh2_ref[...]).astype(BF16))

    if with_mixer:
        bc, arow, blast = _gate_rows(g_ref, bias_ref, triu, n_chunks)
        state = [(cta_ref[hd], jnp.max(m_ref[hd][0:1, :], axis=1, keepdims=True)) for hd in range(HEADS)]
    sg_all = [None] * HEADS

    def mixer_local(c, hd):
        rs = slice(c * CHUNK, (c + 1) * CHUNK)
        sl = slice(hd * HEAD_DIM, (hd + 1) * HEAD_DIM)
        return _mlstm_local(a_ref[rs, sl], kt_ref[sl, rs], arow[c][hd], blast[c][hd], causal)

    def mixer_readout(c, hd, local):
        rs = slice(c * CHUNK, (c + 1) * CHUNK)
        sl = slice(hd * HEAD_DIM, (hd + 1) * HEAD_DIM)
        if c == 0:
            vsn_all = jnp.concatenate(
                [a_ref[cc * CHUNK:(cc + 1) * CHUNK, n_copy + hd * HEAD_DIM: n_copy + (hd + 1) * HEAD_DIM]
                 for cc in range(n_chunks)], axis=1)
            sg_all[hd] = _dot(ws_ref[hd], vsn_all)
        cta, m_prev = state[hd]
        b_rep = jnp.broadcast_to(bc[c][:, HEADS + hd: HEADS + hd + 1], (CHUNK, HEAD_DIM))
        ml, cta, m_prev = _mlstm_readout(
            local, a_ref[rs, sl], a_ref[rs, GROUP_WIDTH + hd * HEAD_DIM: GROUP_WIDTH + (hd + 1) * HEAD_DIM],
            _sigmoid(b_ref[rs, sl]), gh_ref[hd:hd + 1, :], blast[c][hd], b_rep, cta, m_prev, ones_blk)
        state[hd] = (cta, m_prev)
        mix_ref[rs, sl] = ml.astype(mix_ref.dtype)
        sg = sg_all[hd][:, c * CHUNK:(c + 1) * CHUNK] + bs_ref[hd]
        ug = _gelu_tanh(b_ref[rs, GROUP_WIDTH + hd * HEAD_DIM: GROUP_WIDTH + (hd + 1) * HEAD_DIM])
        mix_ref[rs, GROUP_WIDTH + hd * HEAD_DIM: GROUP_WIDTH + (hd + 1) * HEAD_DIM] = (ug * sg).astype(mix_ref.dtype)

    pieces = [(c, hd) for c in range(n_chunks) for hd in range(HEADS)] if with_mixer else []
    per_ff = -(-len(pieces) // len(FF_CHUNKS))
    ahead = [mixer_local(*pieces[p]) for p in range(min(MIXER_LOOKAHEAD, len(pieces)))]
    acc = [None] * len(blocks)
    f0 = 0
    for j, width in enumerate(FF_CHUNKS):
        gate_up = [(_dot(h, wgu_ref[:, f0:f0 + width]), _dot(h, wgu_ref[:, D_FF + f0:D_FF + f0 + width]))
                   for h in h2]
        for idx in range(j * per_ff, min((j + 1) * per_ff, len(pieces))):
            if idx + MIXER_LOOKAHEAD < len(pieces):
                ahead.append(mixer_local(*pieces[idx + MIXER_LOOKAHEAD]))
            mixer_readout(*pieces[idx], ahead.pop(0))
        for r, (gate, up) in enumerate(gate_up):
            act = (gate * _sigmoid(gate) * up).astype(BF16)
            part = _dot(act, wdn_ref[f0:f0 + width, :])
            acc[r] = part if acc[r] is None else acc[r] + part
            if j + 1 == len(FF_CHUNKS):
                x2 = x1[r] + gt2_ref[...] * acc[r]
                y_ref[blocks[r], :] = _rms(x2) * gf_ref[...]
        f0 += width

    if with_mixer:
        for hd in range(HEADS):
            cta_ref[hd] = state[hd][0]
            m_ref[hd] = jnp.broadcast_to(state[hd][1], m_ref.shape[1:])


def _prompt_tail(a, kt, b, gates, x, mod, bias_row, g_head, ws_tril, bs_rep, w_out, g2, w_gu, w_dn, g_final):
    groups, t, n_a = a.shape
    tm = MIX_TILE
    tps = t // tm
    n_tiles = groups * tps
    assert sum(FF_CHUNKS) == D_FF

    def cur(i):
        return jnp.minimum(i, n_tiles - 1)

    def prev(i):
        return jnp.maximum(i - 1, 0)

    def rows(tile, width):
        return pl.BlockSpec((None, tm, width), lambda i: (tile(i) // tps, tile(i) % tps, 0))

    def mod_piece(piece):
        return pl.BlockSpec((None, 1, D_MODEL), lambda i: (prev(i) // tps, 0, piece))

    def const(shape):
        nd = len(shape)
        return pl.BlockSpec(shape, lambda i: (0,) * nd, pipeline_mode=pl.Buffered(1))

    def per_seq(shape):
        nd = len(shape)
        return pl.BlockSpec((None,) + shape, lambda i: (cur(i) // tps,) + (0,) * nd)

    return pl.pallas_call(
        functools.partial(_prompt_tail_kernel, tiles_per_seq=tps),
        grid=(n_tiles + 1,),
        in_specs=[
            rows(cur, n_a),
            pl.BlockSpec((None, GROUP_WIDTH, tm), lambda i: (cur(i) // tps, 0, cur(i) % tps)),
            rows(cur, N_B),
            pl.BlockSpec((None, GATE_ROWS, tm), lambda i: (cur(i) // tps, 0, cur(i) % tps)),
            rows(prev, D_MODEL),
            mod_piece(2), mod_piece(3), mod_piece(4), mod_piece(5),
            const((2 * HEADS, LANES)),
            const((HEADS, HEAD_DIM)),
            const((HEADS, CHUNK, CHUNK)),
            const((HEADS, CHUNK, LANES)),
            const((D_MODEL, D_MODEL)),
            const((1, D_MODEL)),
            const((D_MODEL, 2 * D_FF)),
            const((D_FF, D_MODEL)),
            const((1, D_MODEL)),
        ],
        out_specs=[
            rows(prev, D_MODEL),
            per_seq((HEADS, HEAD_DIM, HEAD_DIM)),
            per_seq((HEADS, HEAD_DIM)),
            per_seq((HEADS, LANES)),
        ],
        out_shape=[
            jax.ShapeDtypeStruct((groups, t, D_MODEL), F32),
            jax.ShapeDtypeStruct((groups, HEADS, HEAD_DIM, HEAD_DIM), F32),
            jax.ShapeDtypeStruct((groups, HEADS, HEAD_DIM), F32),
            jax.ShapeDtypeStruct((groups, HEADS, LANES), F32),
        ],
        scratch_shapes=[
            pltpu.VMEM((tm, D_MODEL), BF16),
            pltpu.VMEM((HEADS, HEAD_DIM, 2 * HEAD_DIM), F32),
            pltpu.VMEM((HEADS, F32_SUBLANES, LANES), F32),
        ],
        compiler_params=pltpu.CompilerParams(
            dimension_semantics=("arbitrary",), vmem_limit_bytes=VMEM_LIMIT),
        name="prompt_tail",
    )(a, kt, b, gates, x, mod, mod, mod, mod, bias_row, g_head, ws_tril, bs_rep, w_out, g2, w_gu, w_dn, g_final)


SAMPLE_TOKENS_PER_STEP = 128
SAMPLE_UNROLL = 8


def _mix_sample_kernel(q_ref, k_ref, v_ref, vsn_ref, og_ref, ug_ref, g_ref, m0_ref, n0_ref, c_ref,
                       bias_ref, gh_ref, ws_ref, bs_ref,
                       mix_ref, c_out, n_out, m_out,
                       vt_ref, kp_ref, wd_ref, cqt_ref):
    hd = pl.program_id(0)
    grp = pl.program_id(1)
    nb = q_ref.shape[0]
    lane = lax.broadcasted_iota(jnp.int32, (nb, LANES), 1)

    def gate_terms():
        pre = g_ref[...]
        i_pre = jnp.sum(jnp.where(lane == hd, pre, 0.0), axis=1, keepdims=True) + bias_ref[pl.ds(hd, 1), 0:1]
        f_pre = (jnp.sum(jnp.where(lane == hd + HEADS, pre, 0.0), axis=1, keepdims=True)
                 + bias_ref[pl.ds(hd + HEADS, 1), 0:1])
        head_lane = lax.broadcasted_iota(jnp.int32, m0_ref.shape, 1)
        m_prev = jnp.sum(jnp.where(head_lane == hd, m0_ref[...], 0.0), axis=1, keepdims=True)
        inter = _log_sigmoid(f_pre) + m_prev
        m_t = jnp.maximum(inter, i_pre)
        return m_t, jnp.exp(i_pre - m_t), jnp.exp(inter - m_t)

    @pl.when(grp == 0)
    def _():
        _, w_in, w_dec = gate_terms()
        vt_ref[...] = v_ref[...].T
        kp_ref[...] = (w_in * k_ref[...]).astype(BF16)
        wd_ref[...] = jnp.broadcast_to(w_dec, wd_ref.shape)
        cqt_ref[...] = jnp.zeros_like(cqt_ref)

    lane_sq = lax.broadcasted_iota(jnp.int32, (HEAD_DIM, LANES), 1)
    tokens = c_ref.shape[0]

    def body(r, cqt):
        tok = grp * tokens + r
        q_row = q_ref[pl.ds(tok, 1), :]
        c_b = c_ref[r]
        sel = lane_sq == tok
        cq_col = jnp.sum(c_b * q_row, axis=1, keepdims=True)
        v_sel = jnp.where(sel, vt_ref[...], 0.0).astype(BF16)
        outer = _dot(v_sel, kp_ref[...])
        c_out[r] = wd_ref[pl.ds(tok, 1), :] * c_b + outer
        return jnp.where(sel, cq_col, cqt)

    cqt_ref[...] = lax.fori_loop(0, tokens, body, cqt_ref[...], unroll=SAMPLE_UNROLL)

    @pl.when(grp == pl.num_programs(1) - 1)
    def _():
        m_t, w_in, w_dec = gate_terms()
        q = q_ref[...]
        k = k_ref[...]
        v = v_ref[...]
        n0 = n0_ref[...]
        cq = cqt_ref[...].T
        s = jnp.sum(q * k, axis=1, keepdims=True) * (QK_SCALE * w_in)
        w_inter = w_dec * QK_SCALE
        num = s * v + w_inter * cq
        den = s + w_inter * jnp.sum(n0 * q, axis=1, keepdims=True)
        hh = num / jnp.maximum(jnp.abs(den), jnp.exp(-m_t))
        ml = _rms(hh) * gh_ref[pl.ds(hd, 1), :] * og_ref[...]
        cm = ug_ref[...] * (ws_ref[hd, 0:1, 0:1] * vsn_ref[...] + bs_ref[pl.ds(hd, 1), 0:1])
        n_out[...] = w_dec * n0 + w_in * k
        m_out[...] = jnp.broadcast_to(m_t, m_out.shape)
        for hh_static in range(HEADS):
            @pl.when(hd == hh_static)
            def _():
                mix_ref[:, hh_static * HEAD_DIM: (hh_static + 1) * HEAD_DIM] = ml
                mix_ref[:, GROUP_WIDTH + hh_static * HEAD_DIM: GROUP_WIDTH + (hh_static + 1) * HEAD_DIM] = cm


def _mix_sample(a, k, b, gates, m0, n0, c0, bias_rep, g_head, w_s, b_s):
    nb = a.shape[0]
    tb = SAMPLE_TOKENS_PER_STEP

    def head_block(offset):
        return pl.BlockSpec((nb, HEAD_DIM), lambda h, g: (0, offset + h))

    def whole(shape):
        return pl.BlockSpec(shape, lambda h, g: (0,) * len(shape))

    c_spec = pl.BlockSpec((tb, None, HEAD_DIM, HEAD_DIM), lambda h, g: (g, h, 0, 0))
    return pl.pallas_call(
        _mix_sample_kernel,
        grid=(HEADS, nb // tb),
        in_specs=[
            head_block(0), head_block(0), head_block(HEADS), head_block(2 * HEADS),
            head_block(0), head_block(HEADS),
            whole((nb, LANES)), whole((nb, HEADS)), head_block(0), c_spec,
            whole((2 * HEADS, LANES)), whole((HEADS, HEAD_DIM)), whole((HEADS, CHUNK, CHUNK)), whole((HEADS, CHUNK)),
        ],
        out_specs=[
            pl.BlockSpec((nb, D_MODEL), lambda h, g: (0, 0)),
            c_spec,
            head_block(0),
            head_block(0),
        ],
        out_shape=[
            jax.ShapeDtypeStruct((nb, D_MODEL), F32),
            jax.ShapeDtypeStruct(c0.shape, F32),
            jax.ShapeDtypeStruct((nb, GROUP_WIDTH), F32),
            jax.ShapeDtypeStruct((nb, GROUP_WIDTH), F32),
        ],
        scratch_shapes=[
            pltpu.VMEM((HEAD_DIM, nb), F32),
            pltpu.VMEM((nb, HEAD_DIM), BF16),
            pltpu.VMEM((nb, LANES), F32),
            pltpu.VMEM((HEAD_DIM, nb), F32),
        ],
        compiler_params=pltpu.CompilerParams(
            dimension_semantics=("arbitrary", "arbitrary"), vmem_limit_bytes=VMEM_LIMIT),
        name="mix_sample",
    )(a, k, a, a, b, b, gates, m0, n0, c0, bias_rep, g_head, w_s, b_s)


def _out_kernel(x_ref, mix_ref, gt1_ref, sh2_ref, sc2_ref, gt2_ref, wo_ref, g2_ref, wgu_ref, wdn_ref, gf_ref,
                y_ref):
    x1 = x_ref[...] + gt1_ref[...] * _dot(mix_ref[...].astype(BF16), wo_ref[...])
    h2 = _rms(x1) * g2_ref[...]
    h2 = (h2 * (1.0 + sc2_ref[...]) + sh2_ref[...]).astype(BF16)
    gate = _dot(h2, wgu_ref[:, :D_FF])
    up = _dot(h2, wgu_ref[:, D_FF:])
    act = (gate * _sigmoid(gate) * up).astype(BF16)
    x2 = x1 + gt2_ref[...] * _dot(act, wdn_ref[...])
    y_ref[:, 0, :] = _rms(x2) * gf_ref[...]


def _output_stage(x, mix, mod, w_out, g2, w_gu, w_dn, g_final, *, tm, per_row):
    groups, t, _ = x.shape
    return pl.pallas_call(
        _out_kernel,
        grid=(groups, t // tm),
        in_specs=[
            pl.BlockSpec((None, tm, D_MODEL), lambda g, i: (g, i, 0)),
            pl.BlockSpec((None, tm, D_MODEL), lambda g, i: (g, i, 0)),
            _mod_spec(2, tm, per_row),
            _mod_spec(3, tm, per_row),
            _mod_spec(4, tm, per_row),
            _mod_spec(5, tm, per_row),
            _const_spec((D_MODEL, D_MODEL)),
            _const_spec((1, D_MODEL)),
            _const_spec((D_MODEL, 2 * D_FF)),
            _const_spec((D_FF, D_MODEL)),
            _const_spec((1, D_MODEL)),
        ],
        out_specs=pl.BlockSpec((tm, 1, D_MODEL), lambda g, i: (g * (t // tm) + i, 0, 0)),
        out_shape=jax.ShapeDtypeStruct((groups * t, 1, D_MODEL), F32),
        compiler_params=pltpu.CompilerParams(
            dimension_semantics=("arbitrary", "arbitrary"), vmem_limit_bytes=VMEM_LIMIT),
        name="output_stage",
    )(x, mix, mod, mod, mod, mod, w_out, g2, w_gu, w_dn, g_final)


def kernel(x_prompt, x_sample, c_prompt, c_sample, state_mlstm_C, state_mlstm_n, state_mlstm_m, w_ada, b_ada, g_norm1, w_in, b_gate, g_mlstm_head, ln_v_g, ln_v_b, w_s, b_s, w_out, g_norm2, w_gate_up, w_down, g_final):
    depth = w_ada.shape[0]
    assert depth == 1, "single-layer trunk"
    batch, seq, _ = x_prompt.shape
    nb = x_sample.shape[0]
    assert x_sample.shape[1] == 1

    gw = GROUP_WIDTH
    g1 = g_norm1[0].reshape(1, D_MODEL)
    g2 = g_norm2[0].reshape(1, D_MODEL)
    gf = g_final.reshape(1, D_MODEL)
    ln_g, ln_b, g_head = ln_v_g[0], ln_v_b[0], g_mlstm_head[0]
    bias_rep = jnp.broadcast_to(b_gate[0][:, None], (2 * HEADS, LANES))
    tril = jnp.tril(jnp.ones((CHUNK, CHUNK), dtype=bool))
    ws_tril = jnp.where(tril[None], w_s[0], 0.0).astype(BF16)
    bs_rep = jnp.broadcast_to(b_s[0][:, :, None], (HEADS, CHUNK, LANES))

    mod_s, mod_p, w_main, w_gates, w_k, w_kt = _modulation(c_sample, c_prompt, w_ada[0], b_ada[0].reshape(1, -1), w_in[0].T)
    mod_s = mod_s.reshape(1, nb, N_MOD * D_MODEL)

    a_p, b_p, g_p, kt_p, w_out_b, w_gu_b, w_dn_b = _project(
        x_prompt, mod_p, g1, w_main, w_gates, ln_g, ln_b, w_kt, (w_out[0], w_gate_up[0], w_down[0]),
        k_transposed=True, tm=PROJ_TILE, per_row=False, a_dtype=BF16)
    y_p, c_p, n_p, m_p = _prompt_tail(a_p, kt_p, b_p, g_p, x_prompt, mod_p, bias_rep, g_head, ws_tril, bs_rep,
                                      w_out_b, g2, w_gu_b, w_dn_b, gf)

    xs = x_sample.reshape(1, nb, D_MODEL)
    a_s, b_s_act, g_s, k_s = _project(xs, mod_s, g1, w_main, w_gates, ln_g, ln_b, w_k, k_transposed=False,
                                      tm=nb, per_row=True, a_dtype=F32)
    a_s2, b_s2, g_s2 = a_s[0], b_s_act[0], g_s[0]
    n0 = state_mlstm_n[0].reshape(nb, gw)
    mix_s, c_s, n_s, m_s = _mix_sample(a_s2, k_s[0], b_s2, g_s2, state_mlstm_m[0], n0, state_mlstm_C[0],
                                       bias_rep, g_head, w_s[0], b_s[0])
    y_s = _output_stage(xs, mix_s.reshape(1, nb, D_MODEL), mod_s, w_out_b, g2, w_gu_b, w_dn_b, gf,
                        tm=nb, per_row=True)

    return (
        y_p,
        y_s,
        c_p[None],
        n_p[None],
        m_p[:, :, 0][None],
        c_s[None],
        n_s.reshape(nb, HEADS, HEAD_DIM)[None],
        m_s.reshape(nb, HEADS, HEAD_DIM)[:, :, 0][None],
        a_s2[:, 2 * gw:].reshape(nb, 1, HEADS, HEAD_DIM)[None],
    )
```

```python
import functools
import math

import jax
import jax.numpy as jnp
from jax import lax
from jax.experimental import pallas as pl
from jax.experimental.pallas import tpu as pltpu

F32 = jnp.float32
BF16 = jnp.bfloat16

D_MODEL = 1024
HEADS = 4
HEAD_DIM = 128
GROUP_WIDTH = HEADS * HEAD_DIM
CHUNK = 128
D_FF = 2816
N_MOD = 6
EPS = 1e-6
QK_SCALE = HEAD_DIM ** -0.5
LANES = 128
F32_SUBLANES = 8
BF16_SUBLANES = 16

N_B = 2 * GROUP_WIDTH

VMEM_LIMIT = 56 * 1024 * 1024


def _dot(a, b):
    return jnp.dot(a, b, preferred_element_type=F32)


def _dot_nt(a, b):
    return lax.dot_general(a, b, (((1,), (1,)), ((), ())), preferred_element_type=F32)


def _sigmoid(x):
    return 1.0 / (1.0 + jnp.exp(-x))


def _gelu_tanh(x):
    c = math.sqrt(2.0 / math.pi)
    return x * (0.5 * (1.0 + jnp.tanh(c * (x + 0.044715 * (x * x * x)))))


def _log_sigmoid(x):
    return jnp.minimum(x, 0.0) - jnp.log1p(jnp.exp(-jnp.abs(x)))


def _rms(x):
    return x * lax.rsqrt(jnp.mean(x * x, axis=-1, keepdims=True) + EPS)


def _split3_bf16(x):
    hi = x.astype(BF16)
    r1 = x - hi.astype(F32)
    mid = r1.astype(BF16)
    lo = (r1 - mid.astype(F32)).astype(BF16)
    return hi, mid, lo


MOD_TILE = 1024

IN_Q, IN_K, IN_V, IN_O, IN_GATES = 0, GROUP_WIDTH, 2 * GROUP_WIDTH, 3 * GROUP_WIDTH, 4 * GROUP_WIDTH
IN_U = IN_GATES + 2 * HEADS
IN_VS = IN_U + GROUP_WIDTH
N_IN = IN_VS + GROUP_WIDTH
MAIN_GROUPS = (IN_Q, IN_V, IN_VS, IN_O, IN_U)
N_MAIN = len(MAIN_GROUPS) * GROUP_WIDTH
GATE_ROWS = BF16_SUBLANES


def _mod_kernel(cs_ref, cp_ref, w_ref, b_ref, grp_ref, gate_ref, os_ref, op_ref, wmain_ref, wgate_ref, wk_ref, wkt_ref):
    step = pl.program_id(0)
    ns, npr = cs_ref.shape[0], cp_ref.shape[0]
    pad = -(ns + npr) % BF16_SUBLANES
    c = jnp.concatenate([cs_ref[...], cp_ref[...], jnp.zeros((pad, D_MODEL), F32)], axis=0)
    a = (c * _sigmoid(c)).astype(BF16)
    res = _dot(a, w_ref[...].astype(BF16)) + b_ref[...]
    os_ref[...] = res[:ns]
    op_ref[:, 0, :] = res[ns:ns + npr]

    def transposed_into(dst_ref):
        for blk in range(GROUP_WIDTH // LANES):
            rows = grp_ref[blk * LANES:(blk + 1) * LANES, :]
            dst_ref[:, blk * LANES:(blk + 1) * LANES] = rows.T.astype(BF16)

    @pl.when(step < len(MAIN_GROUPS))
    def _():
        transposed_into(wmain_ref)

    @pl.when(step == len(MAIN_GROUPS))
    def _():
        wkt_ref[:GROUP_WIDTH, :] = grp_ref[...].astype(BF16)
        wkt_ref[GROUP_WIDTH:, :] = jnp.concatenate(
            [gate_ref[...], jnp.zeros((GATE_ROWS - 2 * HEADS, D_MODEL), F32)], axis=0).astype(BF16)
        transposed_into(wk_ref)

    @pl.when(step == 0)
    def _():
        gate_rows = jnp.concatenate([gate_ref[...], jnp.zeros((LANES - 2 * HEADS, D_MODEL), F32)], axis=0)
        wgate_ref[...] = gate_rows.T.astype(BF16)


def _modulation(c_sample, c_prompt, w_ada, b_ada, w_in_t):
    ns, npr = c_sample.shape[0], c_prompt.shape[0]
    tn = MOD_TILE
    steps = N_MOD * D_MODEL // tn
    group_starts = MAIN_GROUPS + (IN_K,)
    assert w_in_t.shape == (N_IN, D_MODEL) and steps == len(group_starts)

    def whole(shape):
        return pl.BlockSpec(shape, lambda j: (0, 0))

    def group_start(j):
        assert all(start % F32_SUBLANES == 0 for start in group_starts)
        return pl.multiple_of(sum(jnp.where(j == i, start, 0) for i, start in enumerate(group_starts)), F32_SUBLANES)

    return pl.pallas_call(
        _mod_kernel,
        grid=(steps,),
        in_specs=[
            whole((ns, D_MODEL)),
            whole((npr, D_MODEL)),
            pl.BlockSpec((D_MODEL, tn), lambda j: (0, j)),
            pl.BlockSpec((1, tn), lambda j: (0, j)),
            pl.BlockSpec((pl.Element(GROUP_WIDTH), pl.Element(D_MODEL)), lambda j: (group_start(j), 0)),
            pl.BlockSpec((pl.Element(2 * HEADS), pl.Element(D_MODEL)), lambda j: (IN_GATES, 0)),
        ],
        out_specs=[
            pl.BlockSpec((ns, tn), lambda j: (0, j)),
            pl.BlockSpec((npr, 1, tn), lambda j: (0, 0, j)),
            pl.BlockSpec((D_MODEL, GROUP_WIDTH), lambda j: (0, jnp.minimum(j, len(MAIN_GROUPS) - 1))),
            whole((D_MODEL, LANES)),
            whole((D_MODEL, GROUP_WIDTH)),
            whole((GROUP_WIDTH + GATE_ROWS, D_MODEL)),
        ],
        out_shape=[
            jax.ShapeDtypeStruct((ns, N_MOD * D_MODEL), F32),
            jax.ShapeDtypeStruct((npr, 1, N_MOD * D_MODEL), F32),
            jax.ShapeDtypeStruct((D_MODEL, N_MAIN), BF16),
            jax.ShapeDtypeStruct((D_MODEL, LANES), BF16),
            jax.ShapeDtypeStruct((D_MODEL, GROUP_WIDTH), BF16),
            jax.ShapeDtypeStruct((GROUP_WIDTH + GATE_ROWS, D_MODEL), BF16),
        ],
        compiler_params=pltpu.CompilerParams(dimension_semantics=("arbitrary",), vmem_limit_bytes=VMEM_LIMIT),
        name="modulation",
    )(c_sample, c_prompt, w_ada, b_ada, w_in_t, w_in_t)


def _mod_spec(piece, tm, per_row):
    if per_row:
        return pl.BlockSpec((None, tm, D_MODEL), lambda g, t: (g, t, piece))
    return pl.BlockSpec((None, 1, D_MODEL), lambda g, t: (g, 0, piece))


def _const_spec(shape):
    nd = len(shape)
    return pl.BlockSpec(shape, lambda g, t: (0,) * nd, pipeline_mode=pl.Buffered(1))


PROJ_TILE = 1024


def _proj_body(x_ref, sh_ref, sc_ref, g1_ref, w_ref, wg_ref, lng_ref, lnb_ref, a_ref, b_ref, g_ref):
    n_a = a_ref.shape[1]
    n_copy = n_a - GROUP_WIDTH
    x = x_ref[...]
    h = _rms(x) * g1_ref[...]
    h = (h * (1.0 + sc_ref[...]) + sh_ref[...]).astype(BF16)
    p = _dot(h, w_ref[...])
    a_ref[:, :n_copy] = p[:, :n_copy].astype(a_ref.dtype)
    vs = _gelu_tanh(p[:, n_copy:n_a])
    for hd in range(HEADS):
        sl = slice(hd * HEAD_DIM, (hd + 1) * HEAD_DIM)
        v = vs[:, sl]
        mu = jnp.mean(v, axis=-1, keepdims=True)
        vc = v - mu
        var = jnp.mean(vc * vc, axis=-1, keepdims=True)
        y = vc * lax.rsqrt(var + EPS) * lng_ref[hd:hd + 1, :] + lnb_ref[hd:hd + 1, :]
        a_ref[:, n_copy + hd * HEAD_DIM: n_copy + (hd + 1) * HEAD_DIM] = y.astype(a_ref.dtype)
    if wg_ref is not None:
        b_ref[:, :GROUP_WIDTH] = _sigmoid(p[:, n_a: n_a + GROUP_WIDTH])
        b_ref[:, GROUP_WIDTH:] = _gelu_tanh(p[:, n_a + GROUP_WIDTH: n_a + N_B])
        g_ref[...] = _dot(h, wg_ref[...])
    else:
        b_ref[...] = p[:, n_a: n_a + N_B]
    return h


def _proj_kernel(*refs, k_transposed, n_cast):
    ins, outs = refs[:9 + n_cast], refs[9 + n_cast:]
    wk_ref, k_ref = ins[8], outs[3]
    if k_transposed:
        h = _proj_body(*ins[:5], None, *ins[6:8], *outs[:3])
        kt_gates = _dot_nt(wk_ref[...], h)
        k_ref[...] = kt_gates[:GROUP_WIDTH].astype(k_ref.dtype)
        outs[2][...] = kt_gates[GROUP_WIDTH:]
    else:
        h = _proj_body(*ins[:8], *outs[:3])
        k_ref[...] = _dot(h, wk_ref[...]).astype(k_ref.dtype)
    for src, dst in zip(ins[9:], outs[4:]):
        dst[...] = src[...].astype(dst.dtype)


def _project(x, mod, g1, w_main, w_gates, ln_g, ln_b, w_k, cast_weights=(), *, k_transposed, tm, per_row, a_dtype):
    groups, t, _ = x.shape
    n_a = w_main.shape[1] - N_B
    steps_per_group = t // tm
    n_steps = groups * steps_per_group
    in_specs = [
        pl.BlockSpec((None, tm, D_MODEL), lambda g, i: (g, i, 0)),
        _mod_spec(0, tm, per_row),
        _mod_spec(1, tm, per_row),
        _const_spec((1, D_MODEL)),
        _const_spec(w_main.shape),
        _const_spec(w_gates.shape),
        _const_spec((HEADS, HEAD_DIM)),
        _const_spec((HEADS, HEAD_DIM)),
    ]
    out_specs = [
        pl.BlockSpec((None, tm, n_a), lambda g, i: (g, i, 0)),
        pl.BlockSpec((None, tm, N_B), lambda g, i: (g, i, 0)),
    ]
    out_shape = [
        jax.ShapeDtypeStruct((groups, t, n_a), a_dtype),
        jax.ShapeDtypeStruct((groups, t, N_B), F32),
    ]
    if k_transposed:
        out_specs.append(pl.BlockSpec((None, GATE_ROWS, tm), lambda g, i: (g, 0, i)))
        out_shape.append(jax.ShapeDtypeStruct((groups, GATE_ROWS, t), F32))
    else:
        out_specs.append(pl.BlockSpec((None, tm, LANES), lambda g, i: (g, i, 0)))
        out_shape.append(jax.ShapeDtypeStruct((groups, t, LANES), F32))
    args = [x, mod, mod, g1, w_main, w_gates, ln_g, ln_b, w_k]
    in_specs.append(_const_spec(w_k.shape))
    if k_transposed:
        out_specs.append(pl.BlockSpec((None, GROUP_WIDTH, tm), lambda g, i: (g, 0, i)))
        out_shape.append(jax.ShapeDtypeStruct((groups, GROUP_WIDTH, t), a_dtype))
    else:
        out_specs.append(pl.BlockSpec((None, tm, GROUP_WIDTH), lambda g, i: (g, i, 0)))
        out_shape.append(jax.ShapeDtypeStruct((groups, t, GROUP_WIDTH), a_dtype))
    for w in cast_weights:
        rows, cols = w.shape
        assert rows % (n_steps * 16) == 0, "row block must be a whole number of bf16 sublane tiles"
        spec = pl.BlockSpec((rows // n_steps, cols), lambda g, i: (g * steps_per_group + i, 0))
        in_specs.append(spec)
        out_specs.append(spec)
        out_shape.append(jax.ShapeDtypeStruct((rows, cols), BF16))
        args.append(w)
    return pl.pallas_call(
        functools.partial(_proj_kernel, k_transposed=k_transposed, n_cast=len(cast_weights)),
        grid=(groups, t // tm),
        in_specs=in_specs,
        out_specs=out_specs,
        out_shape=out_shape,
        compiler_params=pltpu.CompilerParams(
            dimension_semantics=("arbitrary", "arbitrary"), vmem_limit_bytes=VMEM_LIMIT),
        name="project",
    )(*args)


MIX_TILE = 512


def _dot3_lhs(lhs_f32, rhs_bf16):
    hi, mid, lo = _split3_bf16(lhs_f32)
    return _dot(hi, rhs_bf16) + _dot(mid, rhs_bf16) + _dot(lo, rhs_bf16)


FF_CHUNKS = (768, 768, 768, 512)
MIXER_LOOKAHEAD = 2
OUT_ROW_BLOCKS = 2


def _gate_rows(g_ref, bias_ref, triu, n_chunks):
    n_gates = 2 * HEADS
    assert n_chunks * n_gates <= LANES
    rows = [g_ref[0:n_gates, c * CHUNK:(c + 1) * CHUNK] + bias_ref[...] for c in range(n_chunks)]
    rows_all = jnp.concatenate(rows, axis=0)
    b_rows_all = _dot3_lhs(_log_sigmoid(rows_all), triu)
    bc_all = jnp.concatenate([b_rows_all, jnp.zeros((LANES - n_chunks * n_gates, CHUNK), F32)], axis=0).T
    bc, arow, blast = [], [], []
    for c in range(n_chunks):
        bc.append(bc_all[:, c * n_gates:(c + 1) * n_gates])
        b_rows = b_rows_all[c * n_gates:(c + 1) * n_gates, :]
        arow.append([rows[c][hd:hd + 1, :] - b_rows[HEADS + hd:HEADS + hd + 1, :] for hd in range(HEADS)])
        blast.append([jnp.min(b_rows[HEADS + hd:HEADS + hd + 1, :], axis=1, keepdims=True) for hd in range(HEADS)])
    return bc, arow, blast


def _mlstm_local(q, kt, a_row, b_last, causal):
    amat = jnp.where(causal, a_row, -jnp.inf)
    m_row = jnp.max(amat, axis=1, keepdims=True)
    s_loc = (_dot(q, kt) * jnp.exp(amat - m_row)).astype(BF16)
    g_row = b_last + a_row
    g_loc = jnp.max(g_row, axis=1, keepdims=True)
    kw = (kt.astype(F32) * jnp.exp(g_row - g_loc)).astype(BF16)
    return m_row, s_loc, g_loc, kw


def _mlstm_readout(local, q, v, og, g_head, b_last, b_rep, cta, m_prev, ones_blk):
    m_row, s_loc, g_loc, kw = local
    va = jnp.concatenate([v, ones_blk], axis=1)
    nd_loc = _dot(s_loc, va)
    u_aug = _dot(kw, va)
    inter = _dot(q, cta.astype(BF16))
    mm = jnp.maximum(m_prev, m_row)
    f_loc = jnp.exp(m_row - mm) * QK_SCALE
    f_int = jnp.exp(m_prev - mm) * QK_SCALE
    nd = f_loc * nd_loc + f_int * inter
    clamp = jnp.exp(-(b_rep + mm))
    hh = nd[:, :HEAD_DIM] / jnp.maximum(jnp.abs(nd[:, HEAD_DIM:]), clamp)
    ml = _rms(hh) * g_head * og

    dec = b_last + m_prev
    m_new = jnp.maximum(dec, g_loc)
    cta_new = jnp.exp(dec - m_new) * cta + jnp.exp(g_loc - m_new) * u_aug
    return ml, cta_new, m_new


def _prompt_tail_kernel(a_ref, kt_ref, b_ref, g_ref, x_ref, gt1_ref, sh2_ref, sc2_ref, gt2_ref,
                        bias_ref, gh_ref, ws_ref, bs_ref, wo_ref, g2_ref, wgu_ref, wdn_ref, gf_ref,
                        y_ref, c_out, n_out, m_out,
                        mix_ref, cta_ref, m_ref, *, tiles_per_seq):
    step = pl.program_id(0)
    n_tiles = pl.num_programs(0) - 1
    refs = (a_ref, kt_ref, b_ref, g_ref, x_ref, gt1_ref, sh2_ref, sc2_ref, gt2_ref,
            bias_ref, gh_ref, ws_ref, bs_ref, wo_ref, g2_ref, wgu_ref, wdn_ref, gf_ref,
            y_ref, mix_ref, cta_ref, m_ref)

    @pl.when(step % tiles_per_seq == 0)
    def _():
        cta_ref[...] = jnp.zeros_like(cta_ref)
        m_ref[...] = jnp.zeros_like(m_ref)

    @pl.when(step == 0)
    def _():
        _prompt_tail_body(refs, with_mixer=True, with_out=False)

    @pl.when(jnp.logical_and(step > 0, step < n_tiles))
    def _():
        _prompt_tail_body(refs, with_mixer=True, with_out=True)

    @pl.when(step == n_tiles)
    def _():
        _prompt_tail_body(refs, with_mixer=False, with_out=True)

    @pl.when(jnp.logical_and(step % tiles_per_seq == tiles_per_seq - 1, step < n_tiles))
    def _():
        for hd in range(HEADS):
            cta = cta_ref[hd]
            c_out[hd] = cta[:, :HEAD_DIM].T
            n_out[hd: hd + 1, :] = cta[:, HEAD_DIM:].T[0:1, :]
            m_out[hd: hd + 1, :] = m_ref[hd][0:1, :]


def _prompt_tail_body(refs, *, with_mixer, with_out):
    (a_ref, kt_ref, b_ref, g_ref, x_ref, gt1_ref, sh2_ref, sc2_ref, gt2_ref,
     bias_ref, gh_ref, ws_ref, bs_ref, wo_ref, g2_ref, wgu_ref, wdn_ref, gf_ref,
     y_ref, mix_ref, cta_ref, m_ref) = refs
    tm = a_ref.shape[0]
    n_chunks = tm // CHUNK
    n_copy = a_ref.shape[1] - GROUP_WIDTH

    row = lax.broadcasted_iota(jnp.int32, (CHUNK, CHUNK), 0)
    col = lax.broadcasted_iota(jnp.int32, (CHUNK, CHUNK), 1)
    causal = row >= col
    triu = jnp.where(row <= col, 1.0, 0.0).astype(BF16)
    ones_blk = jnp.ones((CHUNK, HEAD_DIM), BF16)

    blocks = [slice(r0, r0 + tm // OUT_ROW_BLOCKS) for r0 in range(0, tm, tm // OUT_ROW_BLOCKS)] if with_out else []
    outproj = [_dot(mix_ref[rs, :], wo_ref[...]) for rs in blocks]
    x1, h2 = [], []
    for rs, op in zip(blocks, outproj):
        x1.append(x_ref[rs, :] + gt1_ref[...] * op)
        h = _rms(x1[-1]) * g2_ref[...]
        h2.append((h * (1.0 + sc2_ref[...]) + sh2_ref[...]).astype(BF16))

    if with_mixer:
        bc, arow, blast = _gate_rows(g_ref, bias_ref, triu, n_chunks)
        state = [(cta_ref[hd], jnp.max(m_ref[hd][0:1, :], axis=1, keepdims=True)) for hd in range(HEADS)]
    sg_all = [None] * HEADS

    def mixer_local(c, hd):
        rs = slice(c * CHUNK, (c + 1) * CHUNK)
        sl = slice(hd * HEAD_DIM, (hd + 1) * HEAD_DIM)
        return _mlstm_local(a_ref[rs, sl], kt_ref[sl, rs], arow[c][hd], blast[c][hd], causal)

    def mixer_readout(c, hd, local):
        rs = slice(c * CHUNK, (c + 1) * CHUNK)
        sl = slice(hd * HEAD_DIM, (hd + 1) * HEAD_DIM)
        if c == 0:
            vsn_all = jnp.concatenate(
                [a_ref[cc * CHUNK:(cc + 1) * CHUNK, n_copy + hd * HEAD_DIM: n_copy + (hd + 1) * HEAD_DIM]
                 for cc in range(n_chunks)], axis=1)
            sg_all[hd] = _dot(ws_ref[hd], vsn_all)
        cta, m_prev = state[hd]
        b_rep = jnp.broadcast_to(bc[c][:, HEADS + hd: HEADS + hd + 1], (CHUNK, HEAD_DIM))
        ml, cta, m_prev = _mlstm_readout(
            local, a_ref[rs, sl], a_ref[rs, GROUP_WIDTH + hd * HEAD_DIM: GROUP_WIDTH + (hd + 1) * HEAD_DIM],
            _sigmoid(b_ref[rs, sl]), gh_ref[hd:hd + 1, :], blast[c][hd], b_rep, cta, m_prev, ones_blk)
        state[hd] = (cta, m_prev)
        mix_ref[rs, sl] = ml.astype(mix_ref.dtype)
        sg = sg_all[hd][:, c * CHUNK:(c + 1) * CHUNK] + bs_ref[hd]
        ug = _gelu_tanh(b_ref[rs, GROUP_WIDTH + hd * HEAD_DIM: GROUP_WIDTH + (hd + 1) * HEAD_DIM])
        mix_ref[rs, GROUP_WIDTH + hd * HEAD_DIM: GROUP_WIDTH + (hd + 1) * HEAD_DIM] = (ug * sg).astype(mix_ref.dtype)

    pieces = [(c, hd) for c in range(n_chunks) for hd in range(HEADS)] if with_mixer else []
    per_ff = -(-len(pieces) // len(FF_CHUNKS))
    ahead = [mixer_local(*pieces[p]) for p in range(min(MIXER_LOOKAHEAD, len(pieces)))]
    acc = [None] * len(blocks)
    f0 = 0
    for j, width in enumerate(FF_CHUNKS):
        gate_up = [(_dot(h, wgu_ref[:, f0:f0 + width]), _dot(h, wgu_ref[:, D_FF + f0:D_FF + f0 + width]))
                   for h in h2]
        for idx in range(j * per_ff, min((j + 1) * per_ff, len(pieces))):
            if idx + MIXER_LOOKAHEAD < len(pieces):
                ahead.append(mixer_local(*pieces[idx + MIXER_LOOKAHEAD]))
            mixer_readout(*pieces[idx], ahead.pop(0))
        for r, (gate, up) in enumerate(gate_up):
            act = (gate * _sigmoid(gate) * up).astype(BF16)
            part = _dot(act, wdn_ref[f0:f0 + width, :])
            acc[r] = part if acc[r] is None else acc[r] + part
            if j + 1 == len(FF_CHUNKS):
                x2 = x1[r] + gt2_ref[...] * acc[r]
                y_ref[blocks[r], :] = _rms(x2) * gf_ref[...]
        f0 += width

    if with_mixer:
        for hd in range(HEADS):
            cta_ref[hd] = state[hd][0]
            m_ref[hd] = jnp.broadcast_to(state[hd][1], m_ref.shape[1:])


def _prompt_tail(a, kt, b, gates, x, mod, bias_row, g_head, ws_tril, bs_rep, w_out, g2, w_gu, w_dn, g_final):
    groups, t, n_a = a.shape
    tm = MIX_TILE
    tps = t // tm
    n_tiles = groups * tps
    assert sum(FF_CHUNKS) == D_FF

    def cur(i):
        return jnp.minimum(i, n_tiles - 1)

    def prev(i):
        return jnp.maximum(i - 1, 0)

    def rows(tile, width):
        return pl.BlockSpec((None, tm, width), lambda i: (tile(i) // tps, tile(i) % tps, 0))

    def mod_piece(piece):
        return pl.BlockSpec((None, 1, D_MODEL), lambda i: (prev(i) // tps, 0, piece))

    def const(shape):
        nd = len(shape)
        return pl.BlockSpec(shape, lambda i: (0,) * nd, pipeline_mode=pl.Buffered(1))

    def per_seq(shape):
        nd = len(shape)
        return pl.BlockSpec((None,) + shape, lambda i: (cur(i) // tps,) + (0,) * nd)

    return pl.pallas_call(
        functools.partial(_prompt_tail_kernel, tiles_per_seq=tps),
        grid=(n_tiles + 1,),
        in_specs=[
            rows(cur, n_a),
            pl.BlockSpec((None, GROUP_WIDTH, tm), lambda i: (cur(i) // tps, 0, cur(i) % tps)),
            rows(cur, N_B),
            pl.BlockSpec((None, GATE_ROWS, tm), lambda i: (cur(i) // tps, 0, cur(i) % tps)),
            rows(prev, D_MODEL),
            mod_piece(2), mod_piece(3), mod_piece(4), mod_piece(5),
            const((2 * HEADS, LANES)),
            const((HEADS, HEAD_DIM)),
            const((HEADS, CHUNK, CHUNK)),
            const((HEADS, CHUNK, LANES)),
            const((D_MODEL, D_MODEL)),
            const((1, D_MODEL)),
            const((D_MODEL, 2 * D_FF)),
            const((D_FF, D_MODEL)),
            const((1, D_MODEL)),
        ],
        out_specs=[
            rows(prev, D_MODEL),
            per_seq((HEADS, HEAD_DIM, HEAD_DIM)),
            per_seq((HEADS, HEAD_DIM)),
            per_seq((HEADS, LANES)),
        ],
        out_shape=[
            jax.ShapeDtypeStruct((groups, t, D_MODEL), F32),
            jax.ShapeDtypeStruct((groups, HEADS, HEAD_DIM, HEAD_DIM), F32),
            jax.ShapeDtypeStruct((groups, HEADS, HEAD_DIM), F32),
            jax.ShapeDtypeStruct((groups, HEADS, LANES), F32),
        ],
        scratch_shapes=[
            pltpu.VMEM((tm, D_MODEL), BF16),
            pltpu.VMEM((HEADS, HEAD_DIM, 2 * HEAD_DIM), F32),
            pltpu.VMEM((HEADS, F32_SUBLANES, LANES), F32),
        ],
        compiler_params=pltpu.CompilerParams(
            dimension_semantics=("arbitrary",), vmem_limit_bytes=VMEM_LIMIT),
        name="prompt_tail",
    )(a, kt, b, gates, x, mod, mod, mod, mod, bias_row, g_head, ws_tril, bs_rep, w_out, g2, w_gu, w_dn, g_final)


SAMPLE_TOKENS_PER_STEP = 128
SAMPLE_UNROLL = 8


def _mix_sample_kernel(q_ref, k_ref, v_ref, vsn_ref, og_ref, ug_ref, g_ref, m0_ref, n0_ref, c_ref,
                       bias_ref, gh_ref, ws_ref, bs_ref,
                       mix_ref, c_out, n_out, m_out, vsn_out,
                       vt_ref, kp_ref, wd_ref, cqt_ref):
    hd = pl.program_id(0)
    grp = pl.program_id(1)
    nb = q_ref.shape[0]
    lane = lax.broadcasted_iota(jnp.int32, (nb, LANES), 1)

    def gate_terms():
        pre = g_ref[...]
        i_pre = jnp.sum(jnp.where(lane == hd, pre, 0.0), axis=1, keepdims=True) + bias_ref[pl.ds(hd, 1), 0:1]
        f_pre = (jnp.sum(jnp.where(lane == hd + HEADS, pre, 0.0), axis=1, keepdims=True)
                 + bias_ref[pl.ds(hd + HEADS, 1), 0:1])
        head_lane = lax.broadcasted_iota(jnp.int32, m0_ref.shape, 1)
        m_prev = jnp.sum(jnp.where(head_lane == hd, m0_ref[...], 0.0), axis=1, keepdims=True)
        inter = _log_sigmoid(f_pre) + m_prev
        m_t = jnp.maximum(inter, i_pre)
        return m_t, jnp.exp(i_pre - m_t), jnp.exp(inter - m_t)

    @pl.when(grp == 0)
    def _():
        _, w_in, w_dec = gate_terms()
        vt_ref[...] = v_ref[...].T
        kp_ref[...] = (w_in * k_ref[...]).astype(BF16)
        wd_ref[...] = jnp.broadcast_to(w_dec, wd_ref.shape)
        cqt_ref[...] = jnp.zeros_like(cqt_ref)

    lane_sq = lax.broadcasted_iota(jnp.int32, (HEAD_DIM, LANES), 1)
    tokens = c_ref.shape[0]

    def body(r, cqt):
        tok = grp * tokens + r
        q_row = q_ref[pl.ds(tok, 1), :]
        c_b = c_ref[r]
        sel = lane_sq == tok
        cq_col = jnp.sum(c_b * q_row, axis=1, keepdims=True)
        v_sel = jnp.where(sel, vt_ref[...], 0.0).astype(BF16)
        outer = _dot(v_sel, kp_ref[...])
        c_out[r] = wd_ref[pl.ds(tok, 1), :] * c_b + outer
        return jnp.where(sel, cq_col, cqt)

    cqt_ref[...] = lax.fori_loop(0, tokens, body, cqt_ref[...], unroll=SAMPLE_UNROLL)

    @pl.when(grp == pl.num_programs(1) - 1)
    def _():
        m_t, w_in, w_dec = gate_terms()
        q = q_ref[...]
        k = k_ref[...]
        v = v_ref[...]
        n0 = n0_ref[...]
        cq = cqt_ref[...].T
        s = jnp.sum(q * k, axis=1, keepdims=True) * (QK_SCALE * w_in)
        w_inter = w_dec * QK_SCALE
        num = s * v + w_inter * cq
        den = s + w_inter * jnp.sum(n0 * q, axis=1, keepdims=True)
        hh = num / jnp.maximum(jnp.abs(den), jnp.exp(-m_t))
        ml = _rms(hh) * gh_ref[pl.ds(hd, 1), :] * og_ref[...]
        cm = ug_ref[...] * (ws_ref[hd, 0:1, 0:1] * vsn_ref[...] + bs_ref[pl.ds(hd, 1), 0:1])
        n_new = w_dec * n0 + w_in * k
        for hh_static in range(HEADS):
            @pl.when(hd == hh_static)
            def _():
                n_out[:, hh_static, :] = n_new
                m_out[:, hh_static:hh_static + 1] = m_t
                vsn_out[:, hh_static, :] = vsn_ref[...]
                mix_ref[:, hh_static * HEAD_DIM: (hh_static + 1) * HEAD_DIM] = ml
                mix_ref[:, GROUP_WIDTH + hh_static * HEAD_DIM: GROUP_WIDTH + (hh_static + 1) * HEAD_DIM] = cm


def _mix_sample(a, k, b, gates, m0, n0, c0, bias_rep, g_head, w_s, b_s):
    nb = a.shape[0]
    tb = SAMPLE_TOKENS_PER_STEP

    def head_block(offset):
        return pl.BlockSpec((nb, HEAD_DIM), lambda h, g: (0, offset + h))

    def whole(shape):
        return pl.BlockSpec(shape, lambda h, g: (0,) * len(shape))

    c_spec = pl.BlockSpec((tb, None, HEAD_DIM, HEAD_DIM), lambda h, g: (g, h, 0, 0))
    return pl.pallas_call(
        _mix_sample_kernel,
        grid=(HEADS, nb // tb),
        in_specs=[
            head_block(0), head_block(0), head_block(HEADS), head_block(2 * HEADS),
            head_block(0), head_block(HEADS),
            whole((nb, LANES)), whole((nb, HEADS)), head_block(0), c_spec,
            whole((2 * HEADS, LANES)), whole((HEADS, HEAD_DIM)), whole((HEADS, CHUNK, CHUNK)), whole((HEADS, CHUNK)),
        ],
        out_specs=[
            pl.BlockSpec((nb, D_MODEL), lambda h, g: (0, 0)),
            c_spec,
            whole((nb, HEADS, HEAD_DIM)),
            whole((nb, HEADS)),
            whole((nb, HEADS, HEAD_DIM)),
        ],
        out_shape=[
            jax.ShapeDtypeStruct((nb, D_MODEL), F32),
            jax.ShapeDtypeStruct(c0.shape, F32),
            jax.ShapeDtypeStruct((nb, HEADS, HEAD_DIM), F32),
            jax.ShapeDtypeStruct((nb, HEADS), F32),
            jax.ShapeDtypeStruct((nb, HEADS, HEAD_DIM), F32),
        ],
        scratch_shapes=[
            pltpu.VMEM((HEAD_DIM, nb), F32),
            pltpu.VMEM((nb, HEAD_DIM), BF16),
            pltpu.VMEM((nb, LANES), F32),
            pltpu.VMEM((HEAD_DIM, nb), F32),
        ],
        compiler_params=pltpu.CompilerParams(
            dimension_semantics=("arbitrary", "arbitrary"), vmem_limit_bytes=VMEM_LIMIT),
        name="mix_sample",
    )(a, k, a, a, b, b, gates, m0, n0, c0, bias_rep, g_head, w_s, b_s)


def _out_kernel(x_ref, mix_ref, gt1_ref, sh2_ref, sc2_ref, gt2_ref, wo_ref, g2_ref, wgu_ref, wdn_ref, gf_ref,
                y_ref):
    x1 = x_ref[...] + gt1_ref[...] * _dot(mix_ref[...].astype(BF16), wo_ref[...])
    h2 = _rms(x1) * g2_ref[...]
    h2 = (h2 * (1.0 + sc2_ref[...]) + sh2_ref[...]).astype(BF16)
    gate = _dot(h2, wgu_ref[:, :D_FF])
    up = _dot(h2, wgu_ref[:, D_FF:])
    act = (gate * _sigmoid(gate) * up).astype(BF16)
    x2 = x1 + gt2_ref[...] * _dot(act, wdn_ref[...])
    y_ref[:, 0, :] = _rms(x2) * gf_ref[...]


def _output_stage(x, mix, mod, w_out, g2, w_gu, w_dn, g_final, *, tm, per_row):
    groups, t, _ = x.shape
    return pl.pallas_call(
        _out_kernel,
        grid=(groups, t // tm),
        in_specs=[
            pl.BlockSpec((None, tm, D_MODEL), lambda g, i: (g, i, 0)),
            pl.BlockSpec((None, tm, D_MODEL), lambda g, i: (g, i, 0)),
            _mod_spec(2, tm, per_row),
            _mod_spec(3, tm, per_row),
            _mod_spec(4, tm, per_row),
            _mod_spec(5, tm, per_row),
            _const_spec((D_MODEL, D_MODEL)),
            _const_spec((1, D_MODEL)),
            _const_spec((D_MODEL, 2 * D_FF)),
            _const_spec((D_FF, D_MODEL)),
            _const_spec((1, D_MODEL)),
        ],
        out_specs=pl.BlockSpec((tm, 1, D_MODEL), lambda g, i: (g * (t // tm) + i, 0, 0)),
        out_shape=jax.ShapeDtypeStruct((groups * t, 1, D_MODEL), F32),
        compiler_params=pltpu.CompilerParams(
            dimension_semantics=("arbitrary", "arbitrary"), vmem_limit_bytes=VMEM_LIMIT),
        name="output_stage",
    )(x, mix, mod, mod, mod, mod, w_out, g2, w_gu, w_dn, g_final)


def kernel(x_prompt, x_sample, c_prompt, c_sample, state_mlstm_C, state_mlstm_n, state_mlstm_m, w_ada, b_ada, g_norm1, w_in, b_gate, g_mlstm_head, ln_v_g, ln_v_b, w_s, b_s, w_out, g_norm2, w_gate_up, w_down, g_final):
    depth = w_ada.shape[0]
    assert depth == 1, "single-layer trunk"
    batch, seq, _ = x_prompt.shape
    nb = x_sample.shape[0]
    assert x_sample.shape[1] == 1

    gw = GROUP_WIDTH
    g1 = g_norm1[0].reshape(1, D_MODEL)
    g2 = g_norm2[0].reshape(1, D_MODEL)
    gf = g_final.reshape(1, D_MODEL)
    ln_g, ln_b, g_head = ln_v_g[0], ln_v_b[0], g_mlstm_head[0]
    bias_rep = jnp.broadcast_to(b_gate[0][:, None], (2 * HEADS, LANES))
    tril = jnp.tril(jnp.ones((CHUNK, CHUNK), dtype=bool))
    ws_tril = jnp.where(tril[None], w_s[0], 0.0).astype(BF16)
    bs_rep = jnp.broadcast_to(b_s[0][:, :, None], (HEADS, CHUNK, LANES))

    mod_s, mod_p, w_main, w_gates, w_k, w_kt = _modulation(c_sample, c_prompt, w_ada[0], b_ada[0].reshape(1, -1), w_in[0].T)
    mod_s = mod_s.reshape(1, nb, N_MOD * D_MODEL)

    a_p, b_p, g_p, kt_p, w_out_b, w_gu_b, w_dn_b = _project(
        x_prompt, mod_p, g1, w_main, w_gates, ln_g, ln_b, w_kt, (w_out[0], w_gate_up[0], w_down[0]),
        k_transposed=True, tm=PROJ_TILE, per_row=False, a_dtype=BF16)
    y_p, c_p, n_p, m_p = _prompt_tail(a_p, kt_p, b_p, g_p, x_prompt, mod_p, bias_rep, g_head, ws_tril, bs_rep,
                                      w_out_b, g2, w_gu_b, w_dn_b, gf)

    xs = x_sample.reshape(1, nb, D_MODEL)
    a_s, b_s_act, g_s, k_s = _project(xs, mod_s, g1, w_main, w_gates, ln_g, ln_b, w_k, k_transposed=False,
                                      tm=nb, per_row=True, a_dtype=F32)
    a_s2, b_s2, g_s2 = a_s[0], b_s_act[0], g_s[0]
    n0 = state_mlstm_n[0].reshape(nb, gw)
    mix_s, c_s, n_s, m_s, vsn_s = _mix_sample(a_s2, k_s[0], b_s2, g_s2, state_mlstm_m[0], n0, state_mlstm_C[0],
                                       bias_rep, g_head, w_s[0], b_s[0])
    y_s = _output_stage(xs, mix_s.reshape(1, nb, D_MODEL), mod_s, w_out_b, g2, w_gu_b, w_dn_b, gf,
                        tm=nb, per_row=True)

    return (
        y_p,
        y_s,
        c_p[None],
        n_p[None],
        m_p[:, :, 0][None],
        c_s[None],
        n_s[None],
        m_s[None],
        vsn_s.reshape(1, nb, 1, HEADS, HEAD_DIM),
    )
```

```python
import functools
import math

import jax
import jax.numpy as jnp
from jax import lax
from jax.experimental import pallas as pl
from jax.experimental.pallas import tpu as pltpu

F32 = jnp.float32
BF16 = jnp.bfloat16

D_MODEL = 1024
HEADS = 4
HEAD_DIM = 128
GROUP_WIDTH = HEADS * HEAD_DIM
CHUNK = 128
D_FF = 2816
N_MOD = 6
EPS = 1e-6
QK_SCALE = HEAD_DIM ** -0.5
LANES = 128
F32_SUBLANES = 8
BF16_SUBLANES = 16

N_B = 2 * GROUP_WIDTH

VMEM_LIMIT = 56 * 1024 * 1024


def _dot(a, b):
    return jnp.dot(a, b, preferred_element_type=F32)


def _dot_nt(a, b):
    return lax.dot_general(a, b, (((1,), (1,)), ((), ())), preferred_element_type=F32)


def _sigmoid(x):
    return 1.0 / (1.0 + jnp.exp(-x))


def _gelu_tanh(x):
    c = math.sqrt(2.0 / math.pi)
    return x * (0.5 * (1.0 + jnp.tanh(c * (x + 0.044715 * (x * x * x)))))


def _log_sigmoid(x):
    return jnp.minimum(x, 0.0) - jnp.log1p(jnp.exp(-jnp.abs(x)))


def _rms(x):
    return x * lax.rsqrt(jnp.mean(x * x, axis=-1, keepdims=True) + EPS)


def _split3_bf16(x):
    hi = x.astype(BF16)
    r1 = x - hi.astype(F32)
    mid = r1.astype(BF16)
    lo = (r1 - mid.astype(F32)).astype(BF16)
    return hi, mid, lo


MOD_TILE = 1024

IN_Q, IN_K, IN_V, IN_O, IN_GATES = 0, GROUP_WIDTH, 2 * GROUP_WIDTH, 3 * GROUP_WIDTH, 4 * GROUP_WIDTH
IN_U = IN_GATES + 2 * HEADS
IN_VS = IN_U + GROUP_WIDTH
N_IN = IN_VS + GROUP_WIDTH
MAIN_GROUPS = (IN_Q, IN_V, IN_VS, IN_O, IN_U)
N_MAIN = len(MAIN_GROUPS) * GROUP_WIDTH
GATE_ROWS = BF16_SUBLANES


def _mod_kernel(cs_ref, cp_ref, w_ref, b_ref, grp_ref, gate_ref, os_ref, op_ref, wmain_ref, wgate_ref, wk_ref, wkt_ref):
    step = pl.program_id(0)
    ns, npr = cs_ref.shape[0], cp_ref.shape[0]
    pad = -(ns + npr) % BF16_SUBLANES
    c = jnp.concatenate([cs_ref[...], cp_ref[...], jnp.zeros((pad, D_MODEL), F32)], axis=0)
    a = (c * _sigmoid(c)).astype(BF16)
    res = _dot(a, w_ref[...].astype(BF16)) + b_ref[...]
    os_ref[...] = res[:ns]
    op_ref[:, 0, :] = res[ns:ns + npr]

    def transposed_into(dst_ref):
        for blk in range(GROUP_WIDTH // LANES):
            rows = grp_ref[blk * LANES:(blk + 1) * LANES, :]
            dst_ref[:, blk * LANES:(blk + 1) * LANES] = rows.T.astype(BF16)

    @pl.when(step < len(MAIN_GROUPS))
    def _():
        transposed_into(wmain_ref)

    @pl.when(step == len(MAIN_GROUPS))
    def _():
        wkt_ref[:GROUP_WIDTH, :] = grp_ref[...].astype(BF16)
        wkt_ref[GROUP_WIDTH:, :] = jnp.concatenate(
            [gate_ref[...], jnp.zeros((GATE_ROWS - 2 * HEADS, D_MODEL), F32)], axis=0).astype(BF16)
        transposed_into(wk_ref)

    @pl.when(step == 0)
    def _():
        gate_rows = jnp.concatenate([gate_ref[...], jnp.zeros((LANES - 2 * HEADS, D_MODEL), F32)], axis=0)
        wgate_ref[...] = gate_rows.T.astype(BF16)


def _modulation(c_sample, c_prompt, w_ada, b_ada, w_in_t):
    ns, npr = c_sample.shape[0], c_prompt.shape[0]
    tn = MOD_TILE
    steps = N_MOD * D_MODEL // tn
    group_starts = MAIN_GROUPS + (IN_K,)
    assert w_in_t.shape == (N_IN, D_MODEL) and steps == len(group_starts)

    def whole(shape):
        return pl.BlockSpec(shape, lambda j: (0, 0))

    def group_start(j):
        assert all(start % F32_SUBLANES == 0 for start in group_starts)
        return pl.multiple_of(sum(jnp.where(j == i, start, 0) for i, start in enumerate(group_starts)), F32_SUBLANES)

    return pl.pallas_call(
        _mod_kernel,
        grid=(steps,),
        in_specs=[
            whole((ns, D_MODEL)),
            whole((npr, D_MODEL)),
            pl.BlockSpec((D_MODEL, tn), lambda j: (0, j)),
            pl.BlockSpec((1, tn), lambda j: (0, j)),
            pl.BlockSpec((pl.Element(GROUP_WIDTH), pl.Element(D_MODEL)), lambda j: (group_start(j), 0)),
            pl.BlockSpec((pl.Element(2 * HEADS), pl.Element(D_MODEL)), lambda j: (IN_GATES, 0)),
        ],
        out_specs=[
            pl.BlockSpec((ns, tn), lambda j: (0, j)),
            pl.BlockSpec((npr, 1, tn), lambda j: (0, 0, j)),
            pl.BlockSpec((D_MODEL, GROUP_WIDTH), lambda j: (0, jnp.minimum(j, len(MAIN_GROUPS) - 1))),
            whole((D_MODEL, LANES)),
            whole((D_MODEL, GROUP_WIDTH)),
            whole((GROUP_WIDTH + GATE_ROWS, D_MODEL)),
        ],
        out_shape=[
            jax.ShapeDtypeStruct((ns, N_MOD * D_MODEL), F32),
            jax.ShapeDtypeStruct((npr, 1, N_MOD * D_MODEL), F32),
            jax.ShapeDtypeStruct((D_MODEL, N_MAIN), BF16),
            jax.ShapeDtypeStruct((D_MODEL, LANES), BF16),
            jax.ShapeDtypeStruct((D_MODEL, GROUP_WIDTH), BF16),
            jax.ShapeDtypeStruct((GROUP_WIDTH + GATE_ROWS, D_MODEL), BF16),
        ],
        compiler_params=pltpu.CompilerParams(dimension_semantics=("arbitrary",), vmem_limit_bytes=VMEM_LIMIT),
        name="modulation",
    )(c_sample, c_prompt, w_ada, b_ada, w_in_t, w_in_t)


def _mod_spec(piece, tm, per_row):
    if per_row:
        return pl.BlockSpec((None, tm, D_MODEL), lambda g, t: (g, t, piece))
    return pl.BlockSpec((None, 1, D_MODEL), lambda g, t: (g, 0, piece))


def _const_spec(shape):
    nd = len(shape)
    return pl.BlockSpec(shape, lambda g, t: (0,) * nd, pipeline_mode=pl.Buffered(1))


PROJ_TILE = 1024


def _proj_body(x_ref, sh_ref, sc_ref, g1_ref, w_ref, wg_ref, lng_ref, lnb_ref, a_ref, b_ref, g_ref):
    n_a = a_ref.shape[1]
    n_copy = n_a - GROUP_WIDTH
    x = x_ref[...]
    h = _rms(x) * g1_ref[...]
    h = (h * (1.0 + sc_ref[...]) + sh_ref[...]).astype(BF16)
    p = _dot(h, w_ref[...])
    a_ref[:, :n_copy] = p[:, :n_copy].astype(a_ref.dtype)
    vs = _gelu_tanh(p[:, n_copy:n_a])
    for hd in range(HEADS):
        sl = slice(hd * HEAD_DIM, (hd + 1) * HEAD_DIM)
        v = vs[:, sl]
        mu = jnp.mean(v, axis=-1, keepdims=True)
        vc = v - mu
        var = jnp.mean(vc * vc, axis=-1, keepdims=True)
        y = vc * lax.rsqrt(var + EPS) * lng_ref[hd:hd + 1, :] + lnb_ref[hd:hd + 1, :]
        a_ref[:, n_copy + hd * HEAD_DIM: n_copy + (hd + 1) * HEAD_DIM] = y.astype(a_ref.dtype)
    if wg_ref is not None:
        b_ref[:, :GROUP_WIDTH] = _sigmoid(p[:, n_a: n_a + GROUP_WIDTH])
        b_ref[:, GROUP_WIDTH:] = _gelu_tanh(p[:, n_a + GROUP_WIDTH: n_a + N_B])
        g_ref[...] = _dot(h, wg_ref[...])
    else:
        b_ref[...] = p[:, n_a: n_a + N_B]
    return h


def _proj_kernel(*refs, k_transposed, n_cast):
    ins, outs = refs[:9 + n_cast], refs[9 + n_cast:]
    wk_ref, k_ref = ins[8], outs[3]
    if k_transposed:
        h = _proj_body(*ins[:5], None, *ins[6:8], *outs[:3])
        kt_gates = _dot_nt(wk_ref[...], h)
        k_ref[...] = kt_gates[:GROUP_WIDTH].astype(k_ref.dtype)
        outs[2][...] = kt_gates[GROUP_WIDTH:]
    else:
        h = _proj_body(*ins[:8], *outs[:3])
        k_ref[...] = _dot(h, wk_ref[...]).astype(k_ref.dtype)
    for src, dst in zip(ins[9:], outs[4:]):
        dst[...] = src[...].astype(dst.dtype)


def _project(x, mod, g1, w_main, w_gates, ln_g, ln_b, w_k, cast_weights=(), *, k_transposed, tm, per_row, a_dtype):
    groups, t, _ = x.shape
    n_a = w_main.shape[1] - N_B
    steps_per_group = t // tm
    n_steps = groups * steps_per_group
    in_specs = [
        pl.BlockSpec((None, tm, D_MODEL), lambda g, i: (g, i, 0)),
        _mod_spec(0, tm, per_row),
        _mod_spec(1, tm, per_row),
        _const_spec((1, D_MODEL)),
        _const_spec(w_main.shape),
        _const_spec(w_gates.shape),
        _const_spec((HEADS, HEAD_DIM)),
        _const_spec((HEADS, HEAD_DIM)),
    ]
    out_specs = [
        pl.BlockSpec((None, tm, n_a), lambda g, i: (g, i, 0)),
        pl.BlockSpec((None, tm, N_B), lambda g, i: (g, i, 0)),
    ]
    out_shape = [
        jax.ShapeDtypeStruct((groups, t, n_a), a_dtype),
        jax.ShapeDtypeStruct((groups, t, N_B), F32),
    ]
    if k_transposed:
        out_specs.append(pl.BlockSpec((None, GATE_ROWS, tm), lambda g, i: (g, 0, i)))
        out_shape.append(jax.ShapeDtypeStruct((groups, GATE_ROWS, t), F32))
    else:
        out_specs.append(pl.BlockSpec((None, tm, LANES), lambda g, i: (g, i, 0)))
        out_shape.append(jax.ShapeDtypeStruct((groups, t, LANES), F32))
    args = [x, mod, mod, g1, w_main, w_gates, ln_g, ln_b, w_k]
    in_specs.append(_const_spec(w_k.shape))
    if k_transposed:
        out_specs.append(pl.BlockSpec((None, GROUP_WIDTH, tm), lambda g, i: (g, 0, i)))
        out_shape.append(jax.ShapeDtypeStruct((groups, GROUP_WIDTH, t), a_dtype))
    else:
        out_specs.append(pl.BlockSpec((None, tm, GROUP_WIDTH), lambda g, i: (g, i, 0)))
        out_shape.append(jax.ShapeDtypeStruct((groups, t, GROUP_WIDTH), a_dtype))
    for w in cast_weights:
        rows, cols = w.shape
        assert rows % (n_steps * 16) == 0, "row block must be a whole number of bf16 sublane tiles"
        spec = pl.BlockSpec((rows // n_steps, cols), lambda g, i: (g * steps_per_group + i, 0))
        in_specs.append(spec)
        out_specs.append(spec)
        out_shape.append(jax.ShapeDtypeStruct((rows, cols), BF16))
        args.append(w)
    return pl.pallas_call(
        functools.partial(_proj_kernel, k_transposed=k_transposed, n_cast=len(cast_weights)),
        grid=(groups, t // tm),
        in_specs=in_specs,
        out_specs=out_specs,
        out_shape=out_shape,
        compiler_params=pltpu.CompilerParams(
            dimension_semantics=("arbitrary", "arbitrary"), vmem_limit_bytes=VMEM_LIMIT),
        name="project",
    )(*args)


MIX_TILE = 512


def _dot3_lhs(lhs_f32, rhs_bf16):
    hi, mid, lo = _split3_bf16(lhs_f32)
    return _dot(hi, rhs_bf16) + _dot(mid, rhs_bf16) + _dot(lo, rhs_bf16)


FF_CHUNKS = (768, 768, 768, 512)
MIXER_LOOKAHEAD = 2
OUT_ROW_BLOCKS = 2


def _gate_rows(g_ref, bias_ref, triu, n_chunks):
    n_gates = 2 * HEADS
    assert n_chunks * n_gates <= LANES
    rows = [g_ref[0:n_gates, c * CHUNK:(c + 1) * CHUNK] + bias_ref[...] for c in range(n_chunks)]
    rows_all = jnp.concatenate(rows, axis=0)
    b_rows_all = _dot3_lhs(_log_sigmoid(rows_all), triu)
    bc_all = jnp.concatenate([b_rows_all, jnp.zeros((LANES - n_chunks * n_gates, CHUNK), F32)], axis=0).T
    bc, arow, blast = [], [], []
    for c in range(n_chunks):
        bc.append(bc_all[:, c * n_gates:(c + 1) * n_gates])
        b_rows = b_rows_all[c * n_gates:(c + 1) * n_gates, :]
        arow.append([rows[c][hd:hd + 1, :] - b_rows[HEADS + hd:HEADS + hd + 1, :] for hd in range(HEADS)])
        blast.append([jnp.min(b_rows[HEADS + hd:HEADS + hd + 1, :], axis=1, keepdims=True) for hd in range(HEADS)])
    return bc, arow, blast


def _mlstm_local(q, kt, a_row, b_last, causal):
    amat = jnp.where(causal, a_row, -jnp.inf)
    m_row = jnp.max(amat, axis=1, keepdims=True)
    s_loc = (_dot(q, kt) * jnp.exp(amat - m_row)).astype(BF16)
    g_row = b_last + a_row
    g_loc = jnp.max(g_row, axis=1, keepdims=True)
    kw = (kt.astype(F32) * jnp.exp(g_row - g_loc)).astype(BF16)
    return m_row, s_loc, g_loc, kw


def _mlstm_readout(local, q, v, og, g_head, b_last, b_rep, cta, m_prev, ones_blk):
    m_row, s_loc, g_loc, kw = local
    va = jnp.concatenate([v, ones_blk], axis=1)
    nd_loc = _dot(s_loc, va)
    u_aug = _dot(kw, va)
    inter = _dot(q, cta.astype(BF16))
    mm = jnp.maximum(m_prev, m_row)
    f_loc = jnp.exp(m_row - mm) * QK_SCALE
    f_int = jnp.exp(m_prev - mm) * QK_SCALE
    nd = f_loc * nd_loc + f_int * inter
    clamp = jnp.exp(-(b_rep + mm))
    hh = nd[:, :HEAD_DIM] / jnp.maximum(jnp.abs(nd[:, HEAD_DIM:]), clamp)
    ml = _rms(hh) * g_head * og

    dec = b_last + m_prev
    m_new = jnp.maximum(dec, g_loc)
    cta_new = jnp.exp(dec - m_new) * cta + jnp.exp(g_loc - m_new) * u_aug
    return ml, cta_new, m_new


def _prompt_tail_kernel(a_ref, kt_ref, b_ref, g_ref, x_ref, gt1_ref, sh2_ref, sc2_ref, gt2_ref,
                        bias_ref, gh_ref, ws_ref, bs_ref, wo_ref, g2_ref, wgu_ref, wdn_ref, gf_ref,
                        y_ref, c_out, n_out, m_out,
                        mix_ref, cta_ref, m_ref, *, tiles_per_seq):
    step = pl.program_id(0)
    n_tiles = pl.num_programs(0) - 1
    refs = (a_ref, kt_ref, b_ref, g_ref, x_ref, gt1_ref, sh2_ref, sc2_ref, gt2_ref,
            bias_ref, gh_ref, ws_ref, bs_ref, wo_ref, g2_ref, wgu_ref, wdn_ref, gf_ref,
            y_ref, mix_ref, cta_ref, m_ref)

    @pl.when(step % tiles_per_seq == 0)
    def _():
        cta_ref[...] = jnp.zeros_like(cta_ref)
        m_ref[...] = jnp.zeros_like(m_ref)

    @pl.when(step == 0)
    def _():
        _prompt_tail_body(refs, with_mixer=True, with_out=False)

    @pl.when(jnp.logical_and(step > 0, step < n_tiles))
    def _():
        _prompt_tail_body(refs, with_mixer=True, with_out=True)

    @pl.when(step == n_tiles)
    def _():
        _prompt_tail_body(refs, with_mixer=False, with_out=True)

    @pl.when(jnp.logical_and(step % tiles_per_seq == tiles_per_seq - 1, step < n_tiles))
    def _():
        for hd in range(HEADS):
            cta = cta_ref[hd]
            c_out[hd] = cta[:, :HEAD_DIM].T
            n_out[hd: hd + 1, :] = cta[:, HEAD_DIM:].T[0:1, :]
            m_out[hd: hd + 1, :] = m_ref[hd][0:1, :]


def _prompt_tail_body(refs, *, with_mixer, with_out):
    (a_ref, kt_ref, b_ref, g_ref, x_ref, gt1_ref, sh2_ref, sc2_ref, gt2_ref,
     bias_ref, gh_ref, ws_ref, bs_ref, wo_ref, g2_ref, wgu_ref, wdn_ref, gf_ref,
     y_ref, mix_ref, cta_ref, m_ref) = refs
    tm = a_ref.shape[0]
    n_chunks = tm // CHUNK
    n_copy = a_ref.shape[1] - GROUP_WIDTH

    row = lax.broadcasted_iota(jnp.int32, (CHUNK, CHUNK), 0)
    col = lax.broadcasted_iota(jnp.int32, (CHUNK, CHUNK), 1)
    causal = row >= col
    triu = jnp.where(row <= col, 1.0, 0.0).astype(BF16)
    ones_blk = jnp.ones((CHUNK, HEAD_DIM), BF16)

    blocks = [slice(r0, r0 + tm // OUT_ROW_BLOCKS) for r0 in range(0, tm, tm // OUT_ROW_BLOCKS)] if with_out else []
    outproj = [_dot(mix_ref[rs, :], wo_ref[...]) for rs in blocks]
    x1, h2 = [], []
    for rs, op in zip(blocks, outproj):
        x1.append(x_ref[rs, :] + gt1_ref[...] * op)
        h = _rms(x1[-1]) * g2_ref[...]
        h2.append((h * (1.0 + sc2_ref[...]) + sh2_ref[...]).astype(BF16))

    if with_mixer:
        bc, arow, blast = _gate_rows(g_ref, bias_ref, triu, n_chunks)
        state = [(cta_ref[hd], jnp.max(m_ref[hd][0:1, :], axis=1, keepdims=True)) for hd in range(HEADS)]
    sg_all = [None] * HEADS

    def mixer_local(c, hd):
        rs = slice(c * CHUNK, (c + 1) * CHUNK)
        sl = slice(hd * HEAD_DIM, (hd + 1) * HEAD_DIM)
        return _mlstm_local(a_ref[rs, sl], kt_ref[sl, rs], arow[c][hd], blast[c][hd], causal)

    def mixer_readout(c, hd, local):
        rs = slice(c * CHUNK, (c + 1) * CHUNK)
        sl = slice(hd * HEAD_DIM, (hd + 1) * HEAD_DIM)
        if c == 0:
            vsn_all = jnp.concatenate(
                [a_ref[cc * CHUNK:(cc + 1) * CHUNK, n_copy + hd * HEAD_DIM: n_copy + (hd + 1) * HEAD_DIM]
                 for cc in range(n_chunks)], axis=1)
            sg_all[hd] = _dot(ws_ref[hd], vsn_all)
        cta, m_prev = state[hd]
        b_rep = jnp.broadcast_to(bc[c][:, HEADS + hd: HEADS + hd + 1], (CHUNK, HEAD_DIM))
        ml, cta, m_prev = _mlstm_readout(
            local, a_ref[rs, sl], a_ref[rs, GROUP_WIDTH + hd * HEAD_DIM: GROUP_WIDTH + (hd + 1) * HEAD_DIM],
            _sigmoid(b_ref[rs, sl]), gh_ref[hd:hd + 1, :], blast[c][hd], b_rep, cta, m_prev, ones_blk)
        state[hd] = (cta, m_prev)
        mix_ref[rs, sl] = ml.astype(mix_ref.dtype)
        sg = sg_all[hd][:, c * CHUNK:(c + 1) * CHUNK] + bs_ref[hd]
        ug = _gelu_tanh(b_ref[rs, GROUP_WIDTH + hd * HEAD_DIM: GROUP_WIDTH + (hd + 1) * HEAD_DIM])
        mix_ref[rs, GROUP_WIDTH + hd * HEAD_DIM: GROUP_WIDTH + (hd + 1) * HEAD_DIM] = (ug * sg).astype(mix_ref.dtype)

    pieces = [(c, hd) for c in range(n_chunks) for hd in range(HEADS)] if with_mixer else []
    per_ff = -(-len(pieces) // len(FF_CHUNKS))
    ahead = [mixer_local(*pieces[p]) for p in range(min(MIXER_LOOKAHEAD, len(pieces)))]
    acc = [None] * len(blocks)
    f0 = 0
    for j, width in enumerate(FF_CHUNKS):
        gate_up = [(_dot(h, wgu_ref[:, f0:f0 + width]), _dot(h, wgu_ref[:, D_FF + f0:D_FF + f0 + width]))
                   for h in h2]
        for idx in range(j * per_ff, min((j + 1) * per_ff, len(pieces))):
            if idx + MIXER_LOOKAHEAD < len(pieces):
                ahead.append(mixer_local(*pieces[idx + MIXER_LOOKAHEAD]))
            mixer_readout(*pieces[idx], ahead.pop(0))
        for r, (gate, up) in enumerate(gate_up):
            act = (gate * _sigmoid(gate) * up).astype(BF16)
            part = _dot(act, wdn_ref[f0:f0 + width, :])
            acc[r] = part if acc[r] is None else acc[r] + part
            if j + 1 == len(FF_CHUNKS):
                x2 = x1[r] + gt2_ref[...] * acc[r]
                y_ref[blocks[r], :] = _rms(x2) * gf_ref[...]
        f0 += width

    if with_mixer:
        for hd in range(HEADS):
            cta_ref[hd] = state[hd][0]
            m_ref[hd] = jnp.broadcast_to(state[hd][1], m_ref.shape[1:])


def _prompt_tail(a, kt, b, gates, x, mod, bias_row, g_head, ws_tril, bs_rep, w_out, g2, w_gu, w_dn, g_final):
    groups, t, n_a = a.shape
    tm = MIX_TILE
    tps = t // tm
    n_tiles = groups * tps
    assert sum(FF_CHUNKS) == D_FF

    def cur(i):
        return jnp.minimum(i, n_tiles - 1)

    def prev(i):
        return jnp.maximum(i - 1, 0)

    def rows(tile, width):
        return pl.BlockSpec((None, tm, width), lambda i: (tile(i) // tps, tile(i) % tps, 0))

    def mod_piece(piece):
        return pl.BlockSpec((None, 1, D_MODEL), lambda i: (prev(i) // tps, 0, piece))

    def const(shape):
        nd = len(shape)
        return pl.BlockSpec(shape, lambda i: (0,) * nd, pipeline_mode=pl.Buffered(1))

    def per_seq(shape):
        nd = len(shape)
        return pl.BlockSpec((None,) + shape, lambda i: (cur(i) // tps,) + (0,) * nd)

    return pl.pallas_call(
        functools.partial(_prompt_tail_kernel, tiles_per_seq=tps),
        grid=(n_tiles + 1,),
        in_specs=[
            rows(cur, n_a),
            pl.BlockSpec((None, GROUP_WIDTH, tm), lambda i: (cur(i) // tps, 0, cur(i) % tps)),
            rows(cur, N_B),
            pl.BlockSpec((None, GATE_ROWS, tm), lambda i: (cur(i) // tps, 0, cur(i) % tps)),
            rows(prev, D_MODEL),
            mod_piece(2), mod_piece(3), mod_piece(4), mod_piece(5),
            const((2 * HEADS, LANES)),
            const((HEADS, HEAD_DIM)),
            const((HEADS, CHUNK, CHUNK)),
            const((HEADS, CHUNK, LANES)),
            const((D_MODEL, D_MODEL)),
            const((1, D_MODEL)),
            const((D_MODEL, 2 * D_FF)),
            const((D_FF, D_MODEL)),
            const((1, D_MODEL)),
        ],
        out_specs=[
            rows(prev, D_MODEL),
            per_seq((HEADS, HEAD_DIM, HEAD_DIM)),
            per_seq((HEADS, HEAD_DIM)),
            per_seq((HEADS, LANES)),
        ],
        out_shape=[
            jax.ShapeDtypeStruct((groups, t, D_MODEL), F32),
            jax.ShapeDtypeStruct((groups, HEADS, HEAD_DIM, HEAD_DIM), F32),
            jax.ShapeDtypeStruct((groups, HEADS, HEAD_DIM), F32),
            jax.ShapeDtypeStruct((groups, HEADS, LANES), F32),
        ],
        scratch_shapes=[
            pltpu.VMEM((tm, D_MODEL), BF16),
            pltpu.VMEM((HEADS, HEAD_DIM, 2 * HEAD_DIM), F32),
            pltpu.VMEM((HEADS, F32_SUBLANES, LANES), F32),
        ],
        compiler_params=pltpu.CompilerParams(
            dimension_semantics=("arbitrary",), vmem_limit_bytes=VMEM_LIMIT),
        name="prompt_tail",
    )(a, kt, b, gates, x, mod, mod, mod, mod, bias_row, g_head, ws_tril, bs_rep, w_out, g2, w_gu, w_dn, g_final)


SAMPLE_TOKENS_PER_STEP = 128
SAMPLE_UNROLL = 8


def _mix_sample_kernel(q_ref, k_ref, v_ref, vsn_ref, og_ref, ug_ref, g_ref, m0_ref, n0_ref, c_ref,
                       bias_ref, gh_ref, ws_ref, bs_ref,
                       mix_ref, c_out, n_out, m_out, vsn_out,
                       vt_ref, kp_ref, wd_ref, cqt_ref):
    hd = pl.program_id(0)
    grp = pl.program_id(1)
    nb = q_ref.shape[0]
    lane = lax.broadcasted_iota(jnp.int32, (nb, LANES), 1)

    def gate_terms():
        pre = g_ref[...]
        i_pre = jnp.sum(jnp.where(lane == hd, pre, 0.0), axis=1, keepdims=True) + bias_ref[pl.ds(hd, 1), 0:1]
        f_pre = (jnp.sum(jnp.where(lane == hd + HEADS, pre, 0.0), axis=1, keepdims=True)
                 + bias_ref[pl.ds(hd + HEADS, 1), 0:1])
        head_lane = lax.broadcasted_iota(jnp.int32, m0_ref.shape, 1)
        m_prev = jnp.sum(jnp.where(head_lane == hd, m0_ref[...], 0.0), axis=1, keepdims=True)
        inter = _log_sigmoid(f_pre) + m_prev
        m_t = jnp.maximum(inter, i_pre)
        return m_t, jnp.exp(i_pre - m_t), jnp.exp(inter - m_t)

    @pl.when(grp == 0)
    def _():
        _, w_in, w_dec = gate_terms()
        vt_ref[...] = v_ref[...].T
        kp_ref[...] = (w_in * k_ref[...]).astype(BF16)
        wd_ref[...] = jnp.broadcast_to(w_dec, wd_ref.shape)
        cqt_ref[...] = jnp.zeros_like(cqt_ref)

    lane_sq = lax.broadcasted_iota(jnp.int32, (HEAD_DIM, LANES), 1)
    tokens = c_ref.shape[0]

    def body(r, cqt):
        tok = grp * tokens + r
        q_row = q_ref[pl.ds(tok, 1), :]
        c_b = c_ref[r]
        sel = lane_sq == tok
        cq_col = jnp.sum(c_b * q_row, axis=1, keepdims=True)
        v_sel = jnp.where(sel, vt_ref[...], 0.0).astype(BF16)
        outer = _dot(v_sel, kp_ref[...])
        c_out[r] = wd_ref[pl.ds(tok, 1), :] * c_b + outer
        return jnp.where(sel, cq_col, cqt)

    cqt_ref[...] = lax.fori_loop(0, tokens, body, cqt_ref[...], unroll=SAMPLE_UNROLL)

    @pl.when(grp == pl.num_programs(1) - 1)
    def _():
        m_t, w_in, w_dec = gate_terms()
        q = q_ref[...]
        k = k_ref[...]
        v = v_ref[...]
        n0 = n0_ref[...]
        cq = cqt_ref[...].T
        s = jnp.sum(q * k, axis=1, keepdims=True) * (QK_SCALE * w_in)
        w_inter = w_dec * QK_SCALE
        num = s * v + w_inter * cq
        den = s + w_inter * jnp.sum(n0 * q, axis=1, keepdims=True)
        hh = num / jnp.maximum(jnp.abs(den), jnp.exp(-m_t))
        ml = _rms(hh) * gh_ref[pl.ds(hd, 1), :] * og_ref[...]
        cm = ug_ref[...] * (ws_ref[hd, 0:1, 0:1] * vsn_ref[...] + bs_ref[pl.ds(hd, 1), 0:1])
        n_new = w_dec * n0 + w_in * k
        for hh_static in range(HEADS):
            @pl.when(hd == hh_static)
            def _():
                n_out[:, hh_static, :] = n_new
                m_out[:, hh_static:hh_static + 1] = m_t
                vsn_out[:, hh_static, :] = vsn_ref[...]
                mix_ref[:, hh_static * HEAD_DIM: (hh_static + 1) * HEAD_DIM] = ml
                mix_ref[:, GROUP_WIDTH + hh_static * HEAD_DIM: GROUP_WIDTH + (hh_static + 1) * HEAD_DIM] = cm


def _mix_sample(a, k, b, gates, m0, n0, c0, bias_rep, g_head, w_s, b_s):
    nb = a.shape[0]
    tb = SAMPLE_TOKENS_PER_STEP

    def head_block(offset):
        return pl.BlockSpec((nb, HEAD_DIM), lambda h, g: (0, offset + h))

    def whole(shape):
        return pl.BlockSpec(shape, lambda h, g: (0,) * len(shape))

    c_spec = pl.BlockSpec((tb, None, HEAD_DIM, HEAD_DIM), lambda h, g: (g, h, 0, 0))
    return pl.pallas_call(
        _mix_sample_kernel,
        grid=(HEADS, nb // tb),
        in_specs=[
            head_block(0), head_block(0), head_block(HEADS), head_block(2 * HEADS),
            head_block(0), head_block(HEADS),
            whole((nb, LANES)), whole((nb, HEADS)), head_block(0), c_spec,
            whole((2 * HEADS, LANES)), whole((HEADS, HEAD_DIM)), whole((HEADS, CHUNK, CHUNK)), whole((HEADS, CHUNK)),
        ],
        out_specs=[
            pl.BlockSpec((nb, D_MODEL), lambda h, g: (0, 0)),
            c_spec,
            whole((nb, HEADS, HEAD_DIM)),
            whole((nb, HEADS)),
            whole((nb, HEADS, HEAD_DIM)),
        ],
        out_shape=[
            jax.ShapeDtypeStruct((nb, D_MODEL), F32),
            jax.ShapeDtypeStruct(c0.shape, F32),
            jax.ShapeDtypeStruct((nb, HEADS, HEAD_DIM), F32),
            jax.ShapeDtypeStruct((nb, HEADS), F32),
            jax.ShapeDtypeStruct((nb, HEADS, HEAD_DIM), F32),
        ],
        scratch_shapes=[
            pltpu.VMEM((HEAD_DIM, nb), F32),
            pltpu.VMEM((nb, HEAD_DIM), BF16),
            pltpu.VMEM((nb, LANES), F32),
            pltpu.VMEM((HEAD_DIM, nb), F32),
        ],
        compiler_params=pltpu.CompilerParams(
            dimension_semantics=("arbitrary", "arbitrary"), vmem_limit_bytes=VMEM_LIMIT),
        name="mix_sample",
    )(a, k, a, a, b, b, gates, m0, n0, c0, bias_rep, g_head, w_s, b_s)


def _out_kernel(x_ref, mix_ref, gt1_ref, sh2_ref, sc2_ref, gt2_ref, wo_hbm, g2_ref, wgu_hbm, wdn_hbm, gf_ref,
                y_ref, wo_ref, wgu_ref, wdn_ref, sem):
    copies = [pltpu.make_async_copy(src, dst, sem.at[i])
              for i, (src, dst) in enumerate(((wo_hbm, wo_ref), (wgu_hbm, wgu_ref), (wdn_hbm, wdn_ref)))]
    for copy in copies:
        copy.start()
    copies[0].wait()
    x1 = x_ref[...] + gt1_ref[...] * _dot(mix_ref[...].astype(BF16), wo_ref[...])
    h2 = _rms(x1) * g2_ref[...]
    h2 = (h2 * (1.0 + sc2_ref[...]) + sh2_ref[...]).astype(BF16)
    copies[1].wait()
    gate = _dot(h2, wgu_ref[:, :D_FF])
    up = _dot(h2, wgu_ref[:, D_FF:])
    act = (gate * _sigmoid(gate) * up).astype(BF16)
    copies[2].wait()
    x2 = x1 + gt2_ref[...] * _dot(act, wdn_ref[...])
    y_ref[:, 0, :] = _rms(x2) * gf_ref[...]


def _output_stage(x, mix, mod, w_out, g2, w_gu, w_dn, g_final, *, tm, per_row):
    groups, t, _ = x.shape
    assert groups == 1 and t == tm, "the weights are fetched once, by the only grid step"
    in_hbm = pl.BlockSpec(memory_space=pl.ANY)
    return pl.pallas_call(
        _out_kernel,
        grid=(groups, t // tm),
        in_specs=[
            pl.BlockSpec((None, tm, D_MODEL), lambda g, i: (g, i, 0)),
            pl.BlockSpec((None, tm, D_MODEL), lambda g, i: (g, i, 0)),
            _mod_spec(2, tm, per_row),
            _mod_spec(3, tm, per_row),
            _mod_spec(4, tm, per_row),
            _mod_spec(5, tm, per_row),
            in_hbm,
            _const_spec((1, D_MODEL)),
            in_hbm,
            in_hbm,
            _const_spec((1, D_MODEL)),
        ],
        out_specs=pl.BlockSpec((tm, 1, D_MODEL), lambda g, i: (g * (t // tm) + i, 0, 0)),
        out_shape=jax.ShapeDtypeStruct((groups * t, 1, D_MODEL), F32),
        scratch_shapes=[
            pltpu.VMEM(w_out.shape, w_out.dtype),
            pltpu.VMEM(w_gu.shape, w_gu.dtype),
            pltpu.VMEM(w_dn.shape, w_dn.dtype),
            pltpu.SemaphoreType.DMA((3,)),
        ],
        compiler_params=pltpu.CompilerParams(
            dimension_semantics=("arbitrary", "arbitrary"), vmem_limit_bytes=VMEM_LIMIT),
        name="output_stage",
    )(x, mix, mod, mod, mod, mod, w_out, g2, w_gu, w_dn, g_final)


def kernel(x_prompt, x_sample, c_prompt, c_sample, state_mlstm_C, state_mlstm_n, state_mlstm_m, w_ada, b_ada, g_norm1, w_in, b_gate, g_mlstm_head, ln_v_g, ln_v_b, w_s, b_s, w_out, g_norm2, w_gate_up, w_down, g_final):
    depth = w_ada.shape[0]
    assert depth == 1, "single-layer trunk"
    batch, seq, _ = x_prompt.shape
    nb = x_sample.shape[0]
    assert x_sample.shape[1] == 1

    gw = GROUP_WIDTH
    g1 = g_norm1[0].reshape(1, D_MODEL)
    g2 = g_norm2[0].reshape(1, D_MODEL)
    gf = g_final.reshape(1, D_MODEL)
    ln_g, ln_b, g_head = ln_v_g[0], ln_v_b[0], g_mlstm_head[0]
    bias_rep = jnp.broadcast_to(b_gate[0][:, None], (2 * HEADS, LANES))
    tril = jnp.tril(jnp.ones((CHUNK, CHUNK), dtype=bool))
    ws_tril = jnp.where(tril[None], w_s[0], 0.0).astype(BF16)
    bs_rep = jnp.broadcast_to(b_s[0][:, :, None], (HEADS, CHUNK, LANES))

    mod_s, mod_p, w_main, w_gates, w_k, w_kt = _modulation(c_sample, c_prompt, w_ada[0], b_ada[0].reshape(1, -1), w_in[0].T)
    mod_s = mod_s.reshape(1, nb, N_MOD * D_MODEL)

    a_p, b_p, g_p, kt_p, w_out_b, w_gu_b, w_dn_b = _project(
        x_prompt, mod_p, g1, w_main, w_gates, ln_g, ln_b, w_kt, (w_out[0], w_gate_up[0], w_down[0]),
        k_transposed=True, tm=PROJ_TILE, per_row=False, a_dtype=BF16)
    y_p, c_p, n_p, m_p = _prompt_tail(a_p, kt_p, b_p, g_p, x_prompt, mod_p, bias_rep, g_head, ws_tril, bs_rep,
                                      w_out_b, g2, w_gu_b, w_dn_b, gf)

    xs = x_sample.reshape(1, nb, D_MODEL)
    a_s, b_s_act, g_s, k_s = _project(xs, mod_s, g1, w_main, w_gates, ln_g, ln_b, w_k, k_transposed=False,
                                      tm=nb, per_row=True, a_dtype=F32)
    a_s2, b_s2, g_s2 = a_s[0], b_s_act[0], g_s[0]
    n0 = state_mlstm_n[0].reshape(nb, gw)
    mix_s, c_s, n_s, m_s, vsn_s = _mix_sample(a_s2, k_s[0], b_s2, g_s2, state_mlstm_m[0], n0, state_mlstm_C[0],
                                       bias_rep, g_head, w_s[0], b_s[0])
    y_s = _output_stage(xs, mix_s.reshape(1, nb, D_MODEL), mod_s, w_out_b, g2, w_gu_b, w_dn_b, gf,
                        tm=nb, per_row=True)

    return (
        y_p,
        y_s,
        c_p[None],
        n_p[None],
        m_p[:, :, 0][None],
        c_s[None],
        n_s[None],
        m_s[None],
        vsn_s.reshape(1, nb, 1, HEADS, HEAD_DIM),
    )
```
